```python
import math
import jax, jax.numpy as jnp
from jax import lax
import numpy as np

D_MODEL = 1024
BATCH = 16
SEQ = 4096
DEPTH = 2

EPS = 1e-6
ROPE_THETA = 10000.0
BLOCK_Q = 128
RET_HEADS = 4
RET_DK = 64
RET_DV = 128
RET_CHUNK = 128
FOX_HEADS = 4
FOX_DH = 128
MLA_HEADS = 4
MLA_Q_RANK = 256
MLA_KV_RANK = 128
MLA_NOPE = 128
MLA_ROPE = 64
MLA_V = 128
N_BRANCH = 3
BRANCH_W = 512
N_GROUPS = 4
EXP_PER_GROUP = 8
N_EXPERTS = N_GROUPS * EXP_PER_GROUP
TOP_K = 2
D_EXPERT = 512
MOE_BLOCK = 128

IN_SIZES = (RET_HEADS * RET_DK, RET_HEADS * RET_DK, RET_HEADS * RET_DV, RET_HEADS * RET_DV,
            FOX_HEADS * FOX_DH, FOX_HEADS * FOX_DH, FOX_HEADS * FOX_DH, FOX_HEADS,
            MLA_Q_RANK, MLA_KV_RANK, MLA_ROPE)
IN_OFFSETS = tuple(sum(IN_SIZES[:i + 1]) for i in range(len(IN_SIZES) - 1))
D_IN = sum(IN_SIZES)

kernel_name = "hybrid_retention_fox_mla_hiermoe_adaln"


def rms_norm(x, g):
    xf = x.astype(jnp.float32)
    y = xf * lax.rsqrt(jnp.mean(xf * xf, axis=-1, keepdims=True) + EPS)
    return (y * g.astype(jnp.float32)).astype(x.dtype)


def modulate(h, shift, scale):
    return h * (1 + scale[:, None, :]) + shift[:, None, :]


def rope(x, positions):
    d = x.shape[-1]
    inv = ROPE_THETA ** (-jnp.arange(0, d, 2, dtype=jnp.float32) / d)
    ang = positions.astype(jnp.float32)[..., None] * inv
    cos = jnp.cos(ang)[:, :, None, :]
    sin = jnp.sin(ang)[:, :, None, :]
    xf = x.astype(jnp.float32)
    x1, x2 = xf[..., : d // 2], xf[..., d // 2:]
    return jnp.concatenate([x1 * cos - x2 * sin, x2 * cos + x1 * sin], axis=-1).astype(x.dtype)


def retention(q, k, v, g):
    B, S, H, dk = q.shape
    dv = v.shape[-1]
    C = RET_CHUNK
    N = S // C
    dt = q.dtype
    f32 = jnp.float32
    log_gamma = jnp.log1p(-jnp.exp2(-5.0 - jnp.arange(H, dtype=f32)))
    idx = jnp.arange(C, dtype=f32)
    rel = idx[:, None] - idx[None, :]
    dmask = jnp.where(rel >= 0, jnp.exp(log_gamma[:, None, None] * jnp.maximum(rel, 0.0)), 0.0)
    decay_in = jnp.exp(log_gamma[:, None] * (idx + 1.0))
    decay_out = jnp.exp(log_gamma[:, None] * (C - 1.0 - idx))
    decay_chunk = jnp.exp(log_gamma * C)
    qc = q.astype(f32).reshape(B, N, C, H, dk)
    kc = (k.astype(f32) * dk ** -0.5).reshape(B, N, C, H, dk)
    vc = v.astype(f32).reshape(B, N, C, H, dv)
    scores = jnp.einsum('bnchd,bnmhd->bhncm', qc, kc) * dmask[None, :, None]
    inner = jnp.einsum('bhncm,bnmhe->bnche', scores, vc)
    kv = jnp.einsum('bnmhd,hm,bnmhe->nbhde', kc, decay_out, vc)

    def step(state, kv_n):
        return state * decay_chunk[None, :, None, None] + kv_n, state

    _, s_prev = lax.scan(step, jnp.zeros((B, H, dk, dv), f32), kv)
    cross = jnp.einsum('bnchd,hc,nbhde->bnche', qc, decay_in, s_prev)
    o = (inner + cross).reshape(B, S, H, dv)
    mu = jnp.mean(o, axis=-1, keepdims=True)
    var = jnp.mean(jnp.square(o - mu), axis=-1, keepdims=True)
    o = ((o - mu) * lax.rsqrt(var + EPS)).reshape(B, S, H * dv)
    return (jax.nn.silu(g.astype(f32)) * o).astype(dt)


def blocked_causal_attention(q, k, v, log_f_cum=None):
    B, S, H, d = q.shape
    dv = v.shape[-1]
    NQ = S // BLOCK_Q
    scale = d ** -0.5
    kpos = jnp.arange(S)
    qb = q.reshape(B, NQ, BLOCK_Q, H, d).transpose(1, 0, 2, 3, 4)
    blk_ids = jnp.arange(NQ)
    use_forget = log_f_cum is not None
    if use_forget:
        fk = log_f_cum.transpose(0, 2, 1)
        fq = log_f_cum.reshape(B, NQ, BLOCK_Q, H).transpose(1, 0, 3, 2)
        xs = (qb, blk_ids, fq)
    else:
        xs = (qb, blk_ids)

    def body(args):
        qblk, bi = args[0], args[1]
        s = jnp.einsum('bqhd,bkhd->bhqk', qblk, k).astype(jnp.float32) * scale
        if use_forget:
            s = s + args[2][..., None] - fk[:, :, None, :]
        qpos = bi * BLOCK_Q + jnp.arange(BLOCK_Q)
        s = jnp.where(kpos[None, :] <= qpos[:, None], s, -jnp.inf)
        p = jax.nn.softmax(s, axis=-1)
        return jnp.einsum('bhqk,bkhe->bqhe', p.astype(v.dtype), v)

    out = lax.map(body, xs)
    return out.transpose(1, 0, 2, 3, 4).reshape(B, S, H * dv)


def token_mixer(h, positions, w_in, fox_fb, mla_q_norm_g, mla_wq_up, mla_kv_norm_g, mla_wkv_up,
                gate_w, gate_b, branch_w, out_w):
    B, S, D = h.shape
    proj = h @ w_in
    rq, rk, rv, rg, fq, fk, fv, ff, mq, mkv, mkr = jnp.split(proj, IN_OFFSETS, axis=-1)
    rq = rope(rq.reshape(B, S, RET_HEADS, RET_DK), positions)
    rk = rope(rk.reshape(B, S, RET_HEADS, RET_DK), positions)
    ya = retention(rq, rk, rv.reshape(B, S, RET_HEADS, RET_DV), rg)
    log_f = jax.nn.log_sigmoid((ff + fox_fb).astype(jnp.float32))
    f_cum = jnp.cumsum(log_f, axis=1)
    yb = blocked_causal_attention(fq.reshape(B, S, FOX_HEADS, FOX_DH),
                                  fk.reshape(B, S, FOX_HEADS, FOX_DH),
                                  fv.reshape(B, S, FOX_HEADS, FOX_DH), f_cum)
    qh = (rms_norm(mq, mla_q_norm_g) @ mla_wq_up).reshape(B, S, MLA_HEADS, MLA_NOPE + MLA_ROPE)
    q_nope, q_pe = qh[..., :MLA_NOPE], rope(qh[..., MLA_NOPE:], positions)
    kvh = (rms_norm(mkv, mla_kv_norm_g) @ mla_wkv_up).reshape(B, S, MLA_HEADS, MLA_NOPE + MLA_V)
    k_nope, v_c = kvh[..., :MLA_NOPE], kvh[..., MLA_NOPE:]
    k_pe = jnp.broadcast_to(rope(mkr[:, :, None, :], positions), (B, S, MLA_HEADS, MLA_ROPE))
    yc = blocked_causal_attention(jnp.concatenate([q_nope, q_pe], axis=-1),
                                  jnp.concatenate([k_nope, k_pe], axis=-1), v_c)
    gates = jax.nn.sigmoid((h @ gate_w + gate_b).astype(jnp.float32)).astype(h.dtype)
    gates = gates.reshape(B, S, N_BRANCH, D)
    merged = gates[:, :, 0] * (ya @ branch_w[0])
    merged = merged + gates[:, :, 1] * (yb @ branch_w[1])
    merged = merged + gates[:, :, 2] * (yc @ branch_w[2])
    return merged @ out_w


def hier_moe(h, w_grp, b_grp, w_exp, b_exp, w1, w3, w2):
    B, S, D = h.shape
    T = B * S
    f32 = jnp.float32
    t = h.reshape(T, D)
    tf = t.astype(f32)
    grp_p = jax.nn.softmax(tf @ w_grp.astype(f32) + b_grp.astype(f32), axis=-1)
    g_idx = jnp.argmax(grp_p, axis=-1)
    g_w = jnp.max(grp_p, axis=-1)
    exp_logits = (tf @ w_exp.astype(f32) + b_exp.astype(f32)).reshape(T, N_GROUPS, EXP_PER_GROUP)
    in_grp = jnp.take_along_axis(exp_logits, g_idx[:, None, None], axis=1)[:, 0]
    top_p, top_i = lax.top_k(jax.nn.softmax(in_grp, axis=-1), TOP_K)
    top_p = top_p / jnp.sum(top_p, axis=-1, keepdims=True)
    eid = (g_idx[:, None] * EXP_PER_GROUP + top_i).reshape(-1).astype(jnp.int32)
    wts = (g_w[:, None] * top_p).reshape(-1)
    tok = jnp.repeat(jnp.arange(T, dtype=jnp.int32), TOP_K)
    A = T * TOP_K
    order = jnp.argsort(eid)
    se, stok, sw = eid[order], tok[order], wts[order]
    counts = jnp.bincount(eid, length=N_EXPERTS)
    starts = jnp.cumsum(counts) - counts
    pcounts = (counts + MOE_BLOCK - 1) // MOE_BLOCK * MOE_BLOCK
    pends = jnp.cumsum(pcounts)
    pstarts = pends - pcounts
    dest = pstarts[se] + jnp.arange(A) - starts[se]
    n_blocks = -(-A // MOE_BLOCK) + N_EXPERTS
    P = n_blocks * MOE_BLOCK
    buf_tok = jnp.zeros((P,), jnp.int32).at[dest].set(stok)
    buf_w = jnp.zeros((P,), f32).at[dest].set(sw)
    blk_e = jnp.minimum(jnp.searchsorted(pends, jnp.arange(n_blocks) * MOE_BLOCK, side='right'),
                        N_EXPERTS - 1)

    def expert_block(args):
        tok_b, w_b, e = args
        xb = t[tok_b]
        hid = jax.nn.silu(xb @ w1[e]) * (xb @ w3[e])
        return (hid @ w2[e]) * w_b[:, None].astype(t.dtype)

    ys = lax.map(expert_block, (buf_tok.reshape(n_blocks, MOE_BLOCK),
                                buf_w.reshape(n_blocks, MOE_BLOCK), blk_e))
    out = jnp.zeros((T, D), t.dtype).at[buf_tok].add(ys.reshape(P, D))
    return out.reshape(B, S, D)


def setup_inputs(seed: int = 0) -> dict:
    key = jax.random.key(seed)
    ks = jax.random.split(key, 26)
    L, D = DEPTH, D_MODEL
    f32 = jnp.float32

    def nrm(k, shape, fan_in, mult=1.0):
        return jax.random.normal(k, shape, f32) * (mult * fan_in ** -0.5)

    def gain(k, shape):
        return 1.0 + 0.05 * jax.random.normal(k, shape, f32)

    def small(k, shape, s=0.02):
        return s * jax.random.normal(k, shape, f32)

    positions = (jnp.arange(SEQ, dtype=jnp.int32)[None, :]
                 + jax.random.randint(ks[2], (BATCH, 1), 0, 1024, dtype=jnp.int32))
    return {
        "x": jax.random.normal(ks[0], (BATCH, SEQ, D), f32),
        "c": jax.random.normal(ks[1], (BATCH, D), f32),
        "positions": positions,
        "ada_w": nrm(ks[3], (L, D, 6 * D), D, 0.5),
        "ada_b": small(ks[4], (L, 6 * D)),
        "norm1_g": gain(ks[5], (L, D)),
        "norm2_g": gain(ks[6], (L, D)),
        "w_in": nrm(ks[7], (L, D, D_IN), D),
        "fox_fb": 1.0 + 0.5 * jax.random.normal(ks[8], (L, FOX_HEADS), f32),
        "mla_q_norm_g": gain(ks[9], (L, MLA_Q_RANK)),
        "mla_wq_up": nrm(ks[10], (L, MLA_Q_RANK, MLA_HEADS * (MLA_NOPE + MLA_ROPE)), MLA_Q_RANK),
        "mla_kv_norm_g": gain(ks[11], (L, MLA_KV_RANK)),
        "mla_wkv_up": nrm(ks[12], (L, MLA_KV_RANK, MLA_HEADS * (MLA_NOPE + MLA_V)), MLA_KV_RANK),
        "gate_w": nrm(ks[13], (L, D, N_BRANCH * D), D),
        "gate_b": small(ks[14], (L, N_BRANCH * D)),
        "branch_w": nrm(ks[15], (L, N_BRANCH, BRANCH_W, D), BRANCH_W),
        "out_w": nrm(ks[16], (L, D, D), D),
        "router_grp_w": nrm(ks[17], (L, D, N_GROUPS), D),
        "router_grp_b": small(ks[18], (L, N_GROUPS), 0.01),
        "router_exp_w": nrm(ks[19], (L, D, N_EXPERTS), D),
        "router_exp_b": small(ks[20], (L, N_EXPERTS), 0.01),
        "exp_w1": nrm(ks[21], (L, N_EXPERTS, D, D_EXPERT), D),
        "exp_w3": nrm(ks[22], (L, N_EXPERTS, D, D_EXPERT), D),
        "exp_w2": nrm(ks[23], (L, N_EXPERTS, D_EXPERT, D), D_EXPERT),
        "final_g": gain(ks[24], (D,)),
    }


def reference(x, c, positions, ada_w, ada_b, norm1_g, norm2_g, w_in, fox_fb, mla_q_norm_g,
              mla_wq_up, mla_kv_norm_g, mla_wkv_up, gate_w, gate_b, branch_w, out_w,
              router_grp_w, router_grp_b, router_exp_w, router_exp_b, exp_w1, exp_w3, exp_w2,
              final_g):
    c_act = jax.nn.silu(c)
    for l in range(DEPTH):
        mod = c_act @ ada_w[l] + ada_b[l]
        sh1, sc1, gt1, sh2, sc2, gt2 = jnp.split(mod, 6, axis=-1)
        h = modulate(rms_norm(x, norm1_g[l]), sh1, sc1)
        mix = token_mixer(h, positions, w_in[l], fox_fb[l], mla_q_norm_g[l], mla_wq_up[l],
                          mla_kv_norm_g[l], mla_wkv_up[l], gate_w[l], gate_b[l], branch_w[l], out_w[l])
        x = x + gt1[:, None, :] * mix
        h = modulate(rms_norm(x, norm2_g[l]), sh2, sc2)
        ffn = hier_moe(h, router_grp_w[l], router_grp_b[l], router_exp_w[l], router_exp_b[l],
                       exp_w1[l], exp_w3[l], exp_w2[l])
        x = x + gt2[:, None, :] * ffn
    return rms_norm(x, final_g)
```

```python
import functools
import math

import jax
import jax.numpy as jnp
from jax import lax
from jax.experimental import pallas as pl
from jax.experimental.pallas import tpu as pltpu

F32 = jnp.float32
BF16 = jnp.bfloat16
I32 = jnp.int32
HIGHEST = lax.Precision.HIGHEST

EPS = 1e-6
ROPE_THETA = 10000.0
RET_HEADS = 4
RET_DK = 64
RET_DV = 128
RET_CHUNK = 128
FOX_HEADS = 4
FOX_DH = 128
MLA_HEADS = 4
MLA_Q_RANK = 256
MLA_KV_RANK = 128
MLA_NOPE = 128
MLA_ROPE = 64
MLA_V = 128
MLA_DQ = 256
N_BRANCH = 3
BRANCH_W = 512
N_GROUPS = 4
EXP_PER_GROUP = 8
N_EXPERTS = N_GROUPS * EXP_PER_GROUP
D_EXPERT = 512

LANES = 128
V7X_VMEM_LIMIT = 56 * 1024 * 1024

C_RQ, C_RK, C_RV, C_RG = 0, 256, 512, 1024
C_FQ, C_FK, C_FV = 1536, 2048, 2560
C_MQ, C_MKV, C_TAIL = 3072, 3328, 3456
D_IN_PAD = 3584
FF_LANE = MLA_ROPE

NEG_BIG = -1e30


def _cparams(sem):
    return pltpu.CompilerParams(dimension_semantics=sem, vmem_limit_bytes=V7X_VMEM_LIMIT)


def _pick(n, pref):
    t = min(n, pref)
    assert n % t == 0, (n, t)
    return t


def _adaln_kernel(c_ref, w_ref, b_ref, o_ref):
    c = c_ref[...]
    ca = c * jax.nn.sigmoid(c)
    o_ref[...] = jnp.dot(ca, w_ref[...], preferred_element_type=F32, precision=HIGHEST) + b_ref[...]


def _adaln(c, ada_w, ada_b):
    L, D, N = ada_w.shape
    B = c.shape[0]
    tn = _pick(N, 1536)
    return pl.pallas_call(
        _adaln_kernel,
        grid=(L, N // tn),
        in_specs=[pl.BlockSpec((B, D), lambda l, j: (0, 0)),
                  pl.BlockSpec((None, D, tn), lambda l, j: (l, 0, j)),
                  pl.BlockSpec((None, 1, tn), lambda l, j: (l, 0, j))],
        out_specs=pl.BlockSpec((None, B, tn), lambda l, j: (l, 0, j)),
        out_shape=jax.ShapeDtypeStruct((L, B, N), F32),
        compiler_params=_cparams(("arbitrary", "arbitrary")),
        name="adaln",
    )(c, ada_w, ada_b.reshape(L, 1, N))


def _rope_table_kernel(pos_ref, inv_ref, sign_ref, cos_ref, sin_ref):
    ang = pos_ref[...].astype(F32) * inv_ref[...]
    cos_ref[...] = jnp.cos(ang)
    sin_ref[...] = jnp.sin(ang) * sign_ref[...]


def _rope_tables(positions):
    T = positions.size
    tm = _pick(T, 2048)
    half = MLA_ROPE // 2
    inv = ROPE_THETA ** (-jnp.arange(0, MLA_ROPE, 2, dtype=F32) / MLA_ROPE)
    inv_t = jnp.tile(inv, LANES // half).reshape(1, LANES)
    sign = jnp.where((jnp.arange(LANES) % MLA_ROPE) < half, -1.0, 1.0).astype(F32).reshape(1, LANES)
    return pl.pallas_call(
        _rope_table_kernel,
        grid=(T // tm,),
        in_specs=[pl.BlockSpec((tm, 1), lambda i: (i, 0)),
                  pl.BlockSpec((1, LANES), lambda i: (0, 0)),
                  pl.BlockSpec((1, LANES), lambda i: (0, 0))],
        out_specs=[pl.BlockSpec((tm, LANES), lambda i: (i, 0))] * 2,
        out_shape=[jax.ShapeDtypeStruct((T, LANES), F32)] * 2,
        compiler_params=_cparams(("arbitrary",)),
        name="rope_tables",
    )(positions.reshape(T, 1), inv_t, sign)


def _rope_slab(x, cos_t, sin_t, lane):
    nxt = pltpu.roll(x, LANES - 32, axis=1)
    prv = pltpu.roll(x, 32, axis=1)
    swapped = jnp.where((lane & 32) == 0, nxt, prv)
    return x * cos_t + swapped * sin_t


def _norm_mod(x, g, shift, scale):
    y = x * lax.rsqrt(jnp.mean(x * x, axis=-1, keepdims=True) + EPS)
    return (y * g) * (1.0 + scale) + shift


def _rms(x, g):
    return x * lax.rsqrt(jnp.mean(x * x, axis=-1, keepdims=True) + EPS) * g


def _proj_kernel(x_ref, g_ref, sh_ref, sc_ref, w_ref, cos_ref, sin_ref,
                 gq_ref, wq_ref, gkv_ref, wkv_ref,
                 rq_ref, rk_ref, rv_ref, rg_ref, fq_ref, fk_ref, fv_ref,
                 mq_ref, mk_ref, mv_ref, ff_ref):
    h = _norm_mod(x_ref[...], g_ref[...], sh_ref[...], sc_ref[...]).astype(BF16)
    cos_t = cos_ref[...]
    sin_t = sin_ref[...]
    lane = lax.broadcasted_iota(I32, cos_t.shape, 1)

    def proj(c0, width):
        return jnp.dot(h, w_ref[:, c0:c0 + width], preferred_element_type=F32)

    rq = proj(C_RQ, 256)
    rk = proj(C_RK, 256)
    for s in range(2):
        sl = slice(s * LANES, (s + 1) * LANES)
        rq_ref[:, sl] = _rope_slab(rq[:, sl], cos_t, sin_t, lane).astype(BF16)
        rk_ref[:, sl] = (_rope_slab(rk[:, sl], cos_t, sin_t, lane) * (RET_DK ** -0.5)).astype(BF16)
    rv_ref[...] = proj(C_RV, 512).astype(BF16)
    rg_ref[...] = proj(C_RG, 512).astype(BF16)
    fq_ref[...] = (proj(C_FQ, 512) * (FOX_DH ** -0.5)).astype(BF16)
    fk_ref[...] = proj(C_FK, 512).astype(BF16)
    fv_ref[...] = proj(C_FV, 512).astype(BF16)

    tail = proj(C_TAIL, LANES)
    ff_ref[...] = tail
    kpe = jnp.where(lane < MLA_ROPE, _rope_slab(tail, cos_t, sin_t, lane), 0.0).astype(BF16)

    qn = _rms(proj(C_MQ, MLA_Q_RANK), gq_ref[...]).astype(BF16)
    qh = jnp.dot(qn, wq_ref[...], preferred_element_type=F32)
    q_scale = (MLA_NOPE + MLA_ROPE) ** -0.5
    for hd in range(MLA_HEADS):
        c0 = hd * MLA_DQ
        mq_ref[:, c0:c0 + LANES] = (qh[:, c0:c0 + LANES] * q_scale).astype(BF16)
        pe = _rope_slab(qh[:, c0 + LANES:c0 + 2 * LANES], cos_t, sin_t, lane)
        mq_ref[:, c0 + LANES:c0 + 2 * LANES] = jnp.where(lane < MLA_ROPE, pe * q_scale, 0.0).astype(BF16)

    kvn = _rms(proj(C_MKV, MLA_KV_RANK), gkv_ref[...]).astype(BF16)
    kvh = jnp.dot(kvn, wkv_ref[...], preferred_element_type=F32)
    for hd in range(MLA_HEADS):
        c0 = hd * (MLA_NOPE + MLA_V)
        mk_ref[:, hd * MLA_DQ:hd * MLA_DQ + LANES] = kvh[:, c0:c0 + MLA_NOPE].astype(BF16)
        mk_ref[:, hd * MLA_DQ + LANES:(hd + 1) * MLA_DQ] = kpe
        mv_ref[:, hd * MLA_V:(hd + 1) * MLA_V] = kvh[:, c0 + MLA_NOPE:c0 + MLA_NOPE + MLA_V].astype(BF16)


def _proj(x2, S, g, sh, sc, w_all, cos_t, sin_t, gq, wq, gkv, wkv):
    T, D = x2.shape
    tm = _pick(S, 512)
    per_b = S // tm
    row = lambda i: (i, 0)
    const = lambda i: (0, 0)
    batch = lambda i: (i // per_b, 0, 0)
    widths = [256, 256, 512, 512, 512, 512, 512, MLA_HEADS * MLA_DQ, MLA_HEADS * MLA_DQ,
              MLA_HEADS * MLA_V]
    out_shape = [jax.ShapeDtypeStruct((T, w), BF16) for w in widths]
    out_shape.append(jax.ShapeDtypeStruct((T, LANES), F32))
    out_specs = [pl.BlockSpec((tm, w), row) for w in widths] + [pl.BlockSpec((tm, LANES), row)]
    return pl.pallas_call(
        _proj_kernel,
        grid=(T // tm,),
        in_specs=[pl.BlockSpec((tm, D), row),
                  pl.BlockSpec((1, D), const),
                  pl.BlockSpec((None, 1, D), batch),
                  pl.BlockSpec((None, 1, D), batch),
                  pl.BlockSpec((D, D_IN_PAD), const),
                  pl.BlockSpec((tm, LANES), row),
                  pl.BlockSpec((tm, LANES), row),
                  pl.BlockSpec((1, MLA_Q_RANK), const),
                  pl.BlockSpec((MLA_Q_RANK, MLA_HEADS * MLA_DQ), const),
                  pl.BlockSpec((1, MLA_KV_RANK), const),
                  pl.BlockSpec((MLA_KV_RANK, MLA_HEADS * (MLA_NOPE + MLA_V)), const)],
        out_specs=out_specs,
        out_shape=out_shape,
        compiler_params=_cparams(("arbitrary",)),
        name="proj",
    )(x2, g, sh, sc, w_all, cos_t, sin_t, gq, wq, gkv, wkv)


def _fox_gate_kernel(fb_ref, ff_ref, upper_ref, ones_ref, lower_ref, o_ref):
    for hd in range(FOX_HEADS):
        z = ff_ref[hd] + fb_ref[hd]
        ls = -(jnp.maximum(-z, 0.0) + jnp.log1p(jnp.exp(-jnp.abs(z))))
        within = jnp.dot(ls, upper_ref[...], preferred_element_type=F32, precision=HIGHEST)
        totals = jnp.dot(ls, ones_ref[...], preferred_element_type=F32, precision=HIGHEST)
        before = jnp.dot(lower_ref[...], totals, preferred_element_type=F32, precision=HIGHEST)
        o_ref[hd] = within + before


def _fox_gate(ff, fb, B, S):
    ns = S // LANES
    fft = ff.reshape(B, S, FOX_HEADS).transpose(0, 2, 1).reshape(B, FOX_HEADS, ns, LANES)
    idx = jnp.arange(LANES)
    upper = (idx[:, None] <= idx[None, :]).astype(F32)
    ones = jnp.ones((LANES, LANES), F32)
    ridx = jnp.arange(ns)
    lower = (ridx[None, :] < ridx[:, None]).astype(F32)
    out = pl.pallas_call(
        _fox_gate_kernel,
        grid=(B,),
        in_specs=[pl.BlockSpec(memory_space=pltpu.SMEM),
                  pl.BlockSpec((None, FOX_HEADS, ns, LANES), lambda b: (b, 0, 0, 0)),
                  pl.BlockSpec((LANES, LANES), lambda b: (0, 0)),
                  pl.BlockSpec((LANES, LANES), lambda b: (0, 0)),
                  pl.BlockSpec((ns, ns), lambda b: (0, 0))],
        out_specs=pl.BlockSpec((None, FOX_HEADS, ns, LANES), lambda b: (b, 0, 0, 0)),
        out_shape=jax.ShapeDtypeStruct((B, FOX_HEADS, ns, LANES), F32),
        compiler_params=_cparams(("arbitrary",)),
        name="fox_gate",
    )(fb, fft, upper, ones, lower)
    return out.reshape(B, FOX_HEADS, S)


def _retention_kernel(dchunk_ref, q_ref, k_ref, v_ref, g_ref, dmask_ref, din_ref, dout_ref,
                      o_ref, state_ref, *, n_chunks):
    @pl.when(pl.program_id(1) == 0)
    def _():
        state_ref[...] = jnp.zeros_like(state_ref)

    C = RET_CHUNK
    lane = lax.broadcasted_iota(I32, (C, LANES), 1)
    for ci in range(n_chunks):
        rows = slice(ci * C, (ci + 1) * C)
        for hd in range(RET_HEADS):
            slab = slice((hd // 2) * LANES, (hd // 2 + 1) * LANES)
            mine = (lane < RET_DK) if hd % 2 == 0 else (lane >= RET_DK)
            q = jnp.where(mine, q_ref[rows, slab], 0)
            k = jnp.where(mine, k_ref[rows, slab], 0)
            vcols = slice(hd * RET_DV, (hd + 1) * RET_DV)
            v = v_ref[rows, vcols]
            state = state_ref[hd]
            scores = lax.dot_general(q, k, (((1,), (1,)), ((), ())),
                                     preferred_element_type=F32) * dmask_ref[hd]
            inner = jnp.dot(scores.astype(BF16), v, preferred_element_type=F32)
            cross = jnp.dot(q, state.astype(BF16), preferred_element_type=F32) * din_ref[hd]
            o = inner + cross
            vd = (v.astype(F32) * dout_ref[hd]).astype(BF16)
            kv = lax.dot_general(k, vd, (((0,), (0,)), ((), ())), preferred_element_type=F32)
            state_ref[hd] = state * dchunk_ref[hd] + kv
            mu = jnp.mean(o, axis=-1, keepdims=True)
            d = o - mu
            var = jnp.mean(d * d, axis=-1, keepdims=True)
            on = d * lax.rsqrt(var + EPS)
            g = g_ref[rows, vcols].astype(F32)
            o_ref[rows, vcols] = (g * jax.nn.sigmoid(g) * on).astype(BF16)


def _retention(rq, rk, rv, rg, B, S):
    H, C = RET_HEADS, RET_CHUNK
    tr = _pick(S, 4 * C)
    n_chunks = tr // C
    log_gamma = jnp.log1p(-jnp.exp2(-5.0 - jnp.arange(H, dtype=F32)))
    idx = jnp.arange(C, dtype=F32)
    rel = idx[:, None] - idx[None, :]
    dmask = jnp.where(rel >= 0, jnp.exp(log_gamma[:, None, None] * jnp.maximum(rel, 0.0)), 0.0)
    decay_in = jnp.exp(log_gamma[:, None] * (idx + 1.0))
    decay_out = jnp.exp(log_gamma[:, None] * (C - 1.0 - idx))
    decay_chunk = jnp.exp(log_gamma * C)
    din = jnp.broadcast_to(decay_in[:, :, None], (H, C, RET_DV))
    dout = jnp.broadcast_to(decay_out[:, :, None], (H, C, RET_DV))
    tok = lambda b, i: (b, i, 0)
    const3 = lambda b, i: (0, 0, 0)
    return pl.pallas_call(
        functools.partial(_retention_kernel, n_chunks=n_chunks),
        grid=(B, S // tr),
        in_specs=[pl.BlockSpec(memory_space=pltpu.SMEM),
                  pl.BlockSpec((None, tr, H * RET_DK), tok),
                  pl.BlockSpec((None, tr, H * RET_DK), tok),
                  pl.BlockSpec((None, tr, H * RET_DV), tok),
                  pl.BlockSpec((None, tr, H * RET_DV), tok),
                  pl.BlockSpec((H, C, C), const3),
                  pl.BlockSpec((H, C, RET_DV), const3),
                  pl.BlockSpec((H, C, RET_DV), const3)],
        out_specs=pl.BlockSpec((None, tr, H * RET_DV), tok),
        out_shape=jax.ShapeDtypeStruct((B, S, H * RET_DV), BF16),
        scratch_shapes=[pltpu.VMEM((H, LANES, RET_DV), F32)],
        compiler_params=_cparams(("arbitrary", "arbitrary")),
        name="retention",
    )(decay_chunk, rq.reshape(B, S, -1), rk.reshape(B, S, -1), rv.reshape(B, S, -1),
      rg.reshape(B, S, -1), dmask, din, dout)


def _flash_kernel(*refs, tk, use_bias):
    if use_bias:
        q_ref, k_ref, v_ref, fk_ref, o_ref, m_ref, l_ref, acc_ref = refs
    else:
        q_ref, k_ref, v_ref, o_ref, m_ref, l_ref, acc_ref = refs
        fk_ref = None
    qi = pl.program_id(2)
    tq = q_ref.shape[0]
    reps = tk // LANES
    m_ref[...] = jnp.full(m_ref.shape, NEG_BIG, F32)
    l_ref[...] = jnp.zeros_like(l_ref)
    acc_ref[...] = jnp.zeros_like(acc_ref)
    q = q_ref[...]

    def step(j, masked):
        start = pl.multiple_of(j * tk, tk)
        kj = k_ref[pl.ds(start, tk), :]
        vj = v_ref[pl.ds(start, tk), :]
        s = lax.dot_general(q, kj, (((1,), (1,)), ((), ())), preferred_element_type=F32)
        if use_bias:
            s = s - fk_ref[j]
        if masked:
            r = lax.broadcasted_iota(I32, s.shape, 0)
            c = lax.broadcasted_iota(I32, s.shape, 1)
            s = jnp.where(c <= r, s, NEG_BIG)
        m_prev = m_ref[...]
        m_new = jnp.maximum(m_prev, jnp.max(s, axis=-1, keepdims=True))
        alpha = jnp.exp(m_prev - m_new)
        p = jnp.exp(s - jnp.concatenate([m_new] * reps, axis=1))
        l_ref[...] = alpha * l_ref[...] + jnp.sum(p, axis=-1, keepdims=True)
        acc_ref[...] = alpha * acc_ref[...] + jnp.dot(p.astype(BF16), vj, preferred_element_type=F32)
        m_ref[...] = m_new

    def body(j, carry):
        step(j, False)
        return carry

    lax.fori_loop(0, qi, body, 0)
    step(qi, True)
    o_ref[...] = (acc_ref[...] / l_ref[...]).astype(o_ref.dtype)


def _flash(q, k, v, fk, B, S, H, dq, dv):
    assert dv == LANES
    tq = _pick(S, 512)
    tk = tq
    nq = S // tq
    in_specs = [pl.BlockSpec((None, tq, dq), lambda b, h, i: (b, i, h)),
                pl.BlockSpec((None, S, dq), lambda b, h, i: (b, 0, h)),
                pl.BlockSpec((None, S, dv), lambda b, h, i: (b, 0, h))]
    args = [q, k, v]
    if fk is not None:
        in_specs.append(pl.BlockSpec((None, None, nq, 1, tk), lambda b, h, i: (b, h, 0, 0, 0)))
        args.append(fk.reshape(B, H, nq, 1, tk))
    return pl.pallas_call(
        functools.partial(_flash_kernel, tk=tk, use_bias=fk is not None),
        grid=(B, H, nq),
        in_specs=in_specs,
        out_specs=pl.BlockSpec((None, tq, dv), lambda b, h, i: (b, i, h)),
        out_shape=jax.ShapeDtypeStruct((B, S, H * dv), BF16),
        scratch_shapes=[pltpu.VMEM((tq, LANES), F32), pltpu.VMEM((tq, LANES), F32),
                        pltpu.VMEM((tq, dv), F32)],
        compiler_params=_cparams(("arbitrary", "arbitrary", "arbitrary")),
        name="flash_fox" if fk is not None else "flash_mla",
    )(*args)


def _merge_kernel(x_ref, ya_ref, yb_ref, yc_ref, g1_ref, sh1_ref, sc1_ref, gt1_ref,
                  g2_ref, sh2_ref, sc2_ref, gw_ref, gb_ref, bw_ref, ow_ref,
                  rw_ref, rb_ref, tri_ref,
                  x1_ref, h2_ref, mi_ref, mf_ref, cnt_ref, carry_ref):
    D = x_ref.shape[1]

    @pl.when(pl.program_id(0) == 0)
    def _():
        carry_ref[...] = jnp.zeros_like(carry_ref)

    x = x_ref[...]
    h = _norm_mod(x, g1_ref[...], sh1_ref[...], sc1_ref[...]).astype(BF16)
    merged = None
    for i, y_ref in enumerate((ya_ref, yb_ref, yc_ref)):
        gate = jax.nn.sigmoid(jnp.dot(h, gw_ref[:, i * D:(i + 1) * D], preferred_element_type=F32)
                              + gb_ref[:, i * D:(i + 1) * D])
        br = jnp.dot(y_ref[...], bw_ref[i], preferred_element_type=F32)
        merged = gate * br if merged is None else merged + gate * br
    mix = jnp.dot(merged.astype(BF16), ow_ref[...], preferred_element_type=F32)
    x1 = x + gt1_ref[...] * mix
    x1_ref[...] = x1
    h2 = _norm_mod(x1, g2_ref[...], sh2_ref[...], sc2_ref[...])
    h2_ref[...] = h2

    logits = jnp.dot(h2, rw_ref[...], preferred_element_type=F32, precision=HIGHEST) + rb_ref[...]
    lane = lax.broadcasted_iota(I32, logits.shape, 1)
    big = jnp.int32(1 << 20)
    is_grp = (lane >= N_EXPERTS) & (lane < N_EXPERTS + N_GROUPS)
    gl = jnp.where(is_grp, logits, -jnp.inf)
    gmax = jnp.max(gl, axis=-1, keepdims=True)
    g_idx = jnp.min(jnp.where(gl == gmax, lane - N_EXPERTS, big), axis=-1, keepdims=True)
    g_w = 1.0 / jnp.sum(jnp.exp(gl - gmax), axis=-1, keepdims=True)
    in_grp = (lane < N_EXPERTS) & ((lane >> 3) == g_idx)
    el = jnp.where(in_grp, logits, -jnp.inf)
    e1 = jnp.max(el, axis=-1, keepdims=True)
    i1 = jnp.min(jnp.where(el == e1, lane, big), axis=-1, keepdims=True)
    el2 = jnp.where(lane == i1, -jnp.inf, el)
    e2 = jnp.max(el2, axis=-1, keepdims=True)
    i2 = jnp.min(jnp.where(el2 == e2, lane, big), axis=-1, keepdims=True)
    r = jnp.exp(e2 - e1)
    w1 = g_w / (1.0 + r)
    w2 = g_w * r / (1.0 + r)

    hit1 = lane == i1
    hit2 = lane == i2
    onehot = jnp.where(hit1 | hit2, 1.0, 0.0)
    before = jnp.dot(tri_ref[...], onehot.astype(BF16), preferred_element_type=F32) + carry_ref[...]
    rank1 = jnp.sum(jnp.where(hit1, before, 0.0), axis=-1, keepdims=True)
    rank2 = jnp.sum(jnp.where(hit2, before, 0.0), axis=-1, keepdims=True)
    carry_ref[...] = carry_ref[...] + jnp.sum(onehot, axis=0, keepdims=True)
    cnt_ref[...] = carry_ref[...]

    lane8 = lax.broadcasted_iota(I32, mi_ref.shape, 1)
    mi_ref[...] = jnp.where(lane8 == 0, i1,
                  jnp.where(lane8 == 1, i2,
                  jnp.where(lane8 == 2, rank1.astype(I32),
                  jnp.where(lane8 == 3, rank2.astype(I32), 0))))
    mf_ref[...] = jnp.where(lane8 == 0, w1, jnp.where(lane8 == 1, w2, 0.0))


def _merge(x2, S, ya, yb, yc, g1, sh1, sc1, gt1, g2, sh2, sc2, gw, gb, bw, ow, rw, rb):
    T, D = x2.shape
    tm = _pick(S, 512)
    per_b = S // tm
    row = lambda i: (i, 0)
    const = lambda i: (0, 0)
    batch = lambda i: (i // per_b, 0, 0)
    idx = jnp.arange(tm)
    tri = (idx[None, :] < idx[:, None]).astype(BF16)
    vecb = pl.BlockSpec((None, 1, D), batch)
    return pl.pallas_call(
        _merge_kernel,
        grid=(T // tm,),
        in_specs=[pl.BlockSpec((tm, D), row),
                  pl.BlockSpec((tm, BRANCH_W), row),
                  pl.BlockSpec((tm, BRANCH_W), row),
                  pl.BlockSpec((tm, BRANCH_W), row),
                  pl.BlockSpec((1, D), const), vecb, vecb, vecb,
                  pl.BlockSpec((1, D), const), vecb, vecb,
                  pl.BlockSpec((D, N_BRANCH * D), const),
                  pl.BlockSpec((1, N_BRANCH * D), const),
                  pl.BlockSpec((N_BRANCH, BRANCH_W, D), lambda i: (0, 0, 0)),
                  pl.BlockSpec((D, D), const),
                  pl.BlockSpec((D, LANES), const),
                  pl.BlockSpec((1, LANES), const),
                  pl.BlockSpec((tm, tm), const)],
        out_specs=[pl.BlockSpec((tm, D), row), pl.BlockSpec((tm, D), row),
                   pl.BlockSpec((tm, 8), row), pl.BlockSpec((tm, 8), row),
                   pl.BlockSpec((1, LANES), const)],
        out_shape=[jax.ShapeDtypeStruct((T, D), F32), jax.ShapeDtypeStruct((T, D), F32),
                   jax.ShapeDtypeStruct((T, 8), I32), jax.ShapeDtypeStruct((T, 8), F32),
                   jax.ShapeDtypeStruct((1, LANES), F32)],
        scratch_shapes=[pltpu.VMEM((1, LANES), F32)],
        compiler_params=_cparams(("arbitrary",)),
        name="merge",
    )(x2, ya, yb, yc, g1, sh1, sc1, gt1, g2, sh2, sc2, gw, gb, bw, ow, rw, rb, tri)


def _row_copy(src_ref, src_row, dst_ref, dst_row, sem):
    return pltpu.make_async_copy(src_ref.at[pl.ds(src_row, 1), :], dst_ref.at[pl.ds(dst_row, 1), :], sem)


def _dispatch_kernel(dest_ref, h_ref, xs_in_ref, xs_ref, dsm_ref, sem_ref, isem_ref):
    del xs_in_ref
    tm = h_ref.shape[0]
    cp = pltpu.make_async_copy(dest_ref.at[0], dsm_ref, isem_ref)
    cp.start()
    cp.wait()

    def issue(r, carry):
        _row_copy(h_ref, r, xs_ref, dsm_ref[0, 2 * r], sem_ref).start()
        _row_copy(h_ref, r, xs_ref, dsm_ref[0, 2 * r + 1], sem_ref).start()
        return carry

    lax.fori_loop(0, tm, issue, 0, unroll=8)

    def drain(r, carry):
        _row_copy(h_ref, 0, xs_ref, 0, sem_ref).wait()
        _row_copy(h_ref, 0, xs_ref, 0, sem_ref).wait()
        return carry

    lax.fori_loop(0, tm, drain, 0, unroll=8)


def _dispatch(h2, dest, P):
    T, D = h2.shape
    tm = _pick(T, 256)
    nt = T // tm
    xs0 = jnp.zeros((P, D), F32)
    return pl.pallas_call(
        _dispatch_kernel,
        grid=(nt,),
        in_specs=[pl.BlockSpec((1, 1, 2 * tm), lambda i: (i, 0, 0)),
                  pl.BlockSpec((tm, D), lambda i: (i, 0)),
                  pl.BlockSpec(memory_space=pl.ANY)],
        out_specs=pl.BlockSpec(memory_space=pl.ANY),
        out_shape=jax.ShapeDtypeStruct((P, D), F32),
        scratch_shapes=[pltpu.SMEM((1, 2 * tm), I32), pltpu.SemaphoreType.DMA, pltpu.SemaphoreType.DMA],
        input_output_aliases={2: 0},
        compiler_params=_cparams(("arbitrary",)),
        name="moe_dispatch",
    )(dest.reshape(nt, 1, 2 * tm), h2, xs0)


def _expert_kernel(blk_e_ref, nvalid_ref, xs_ref, w1_ref, w3_ref, w2_ref, ys_ref):
    del blk_e_ref
    valid = pl.program_id(0) < nvalid_ref[0]

    @pl.when(valid)
    def _():
        xb = xs_ref[...].astype(BF16)
        a = jnp.dot(xb, w1_ref[...], preferred_element_type=F32)
        b = jnp.dot(xb, w3_ref[...], preferred_element_type=F32)
        hid = (a * jax.nn.sigmoid(a) * b).astype(BF16)
        ys_ref[...] = jnp.dot(hid, w2_ref[...], preferred_element_type=F32)

    @pl.when(jnp.logical_not(valid))
    def _():
        ys_ref[...] = jnp.zeros_like(ys_ref)


def _experts(xs, blk_e, nvalid, w1, w3, w2, tb):
    P, D = xs.shape
    n_blocks = P // tb
    rows = lambda i, be, nv: (jnp.minimum(i, nv[0] - 1), 0)
    grid_spec = pltpu.PrefetchScalarGridSpec(
        num_scalar_prefetch=2,
        grid=(n_blocks,),
        in_specs=[pl.BlockSpec((tb, D), rows),
                  pl.BlockSpec((None, D, D_EXPERT), lambda i, be, nv: (be[i], 0, 0)),
                  pl.BlockSpec((None, D, D_EXPERT), lambda i, be, nv: (be[i], 0, 0)),
                  pl.BlockSpec((None, D_EXPERT, D), lambda i, be, nv: (be[i], 0, 0))],
        out_specs=pl.BlockSpec((tb, D), lambda i, be, nv: (i, 0)),
    )
    return pl.pallas_call(
        _expert_kernel,
        grid_spec=grid_spec,
        out_shape=jax.ShapeDtypeStruct((P, D), F32),
        compiler_params=_cparams(("arbitrary",)),
        name="moe_experts",
    )(blk_e, nvalid, xs, w1, w3, w2)


def _combine_kernel(dest_ref, x_ref, mf_ref, gt_ref, fg_ref, ys_ref, o_ref,
                    dsm_ref, ybuf_ref, sem_ref, isem_ref, *, final):
    tm = x_ref.shape[0]
    cp = pltpu.make_async_copy(dest_ref.at[0], dsm_ref, isem_ref)
    cp.start()
    cp.wait()

    def issue(r, carry):
        _row_copy(ys_ref, dsm_ref[0, 2 * r], ybuf_ref.at[0], r, sem_ref).start()
        _row_copy(ys_ref, dsm_ref[0, 2 * r + 1], ybuf_ref.at[1], r, sem_ref).start()
        return carry

    lax.fori_loop(0, tm, issue, 0, unroll=8)

    def drain(r, carry):
        _row_copy(ys_ref, 0, ybuf_ref.at[0], 0, sem_ref).wait()
        _row_copy(ys_ref, 0, ybuf_ref.at[1], 0, sem_ref).wait()
        return carry

    lax.fori_loop(0, tm, drain, 0, unroll=8)

    mf = mf_ref[...]
    ffn = ybuf_ref[0] * mf[:, 0:1] + ybuf_ref[1] * mf[:, 1:2]
    out = x_ref[...] + gt_ref[...] * ffn
    if final:
        out = _rms(out, fg_ref[...])
    o_ref[...] = out


def _combine(x1, S, mf, gt2, final_g, ys, dest, final):
    T, D = x1.shape
    tm = _pick(S, 256)
    per_b = S // tm
    nt = T // tm
    return pl.pallas_call(
        functools.partial(_combine_kernel, final=final),
        grid=(nt,),
        in_specs=[pl.BlockSpec((1, 1, 2 * tm), lambda i: (i, 0, 0)),
                  pl.BlockSpec((tm, D), lambda i: (i, 0)),
                  pl.BlockSpec((tm, 8), lambda i: (i, 0)),
                  pl.BlockSpec((None, 1, D), lambda i: (i // per_b, 0, 0)),
                  pl.BlockSpec((1, D), lambda i: (0, 0)),
                  pl.BlockSpec(memory_space=pl.ANY)],
        out_specs=pl.BlockSpec((tm, D), lambda i: (i, 0)),
        out_shape=jax.ShapeDtypeStruct((T, D), F32),
        scratch_shapes=[pltpu.SMEM((1, 2 * tm), I32), pltpu.VMEM((2, tm, D), F32),
                        pltpu.SemaphoreType.DMA, pltpu.SemaphoreType.DMA],
        compiler_params=_cparams(("arbitrary",)),
        name="moe_combine",
    )(dest.reshape(nt, 1, 2 * tm), x1, mf, gt2, final_g, ys)


def _prep_w_in(w):
    offs = [0]
    for s in (256, 256, 512, 512, 512, 512, 512, FOX_HEADS, MLA_Q_RANK, MLA_KV_RANK, MLA_ROPE):
        offs.append(offs[-1] + s)
    rq, rk, rv, rg, fq, fk, fv, ff, mq, mkv, mkr = [w[:, offs[i]:offs[i + 1]] for i in range(11)]
    pad = jnp.zeros((w.shape[0], LANES - MLA_ROPE - FOX_HEADS), w.dtype)
    return jnp.concatenate([rq, rk, rv, rg, fq, fk, fv, mq, mkv, mkr, ff, pad], axis=1).astype(BF16)


def _prep_wq_up(w):
    r = w.reshape(MLA_Q_RANK, MLA_HEADS, MLA_NOPE + MLA_ROPE)
    r = jnp.pad(r, ((0, 0), (0, 0), (0, MLA_DQ - MLA_NOPE - MLA_ROPE)))
    return r.reshape(MLA_Q_RANK, MLA_HEADS * MLA_DQ).astype(BF16)


def _prep_router(w_grp, b_grp, w_exp, b_exp):
    D = w_grp.shape[0]
    pad = LANES - N_EXPERTS - N_GROUPS
    rw = jnp.concatenate([w_exp, w_grp, jnp.zeros((D, pad), F32)], axis=1).astype(F32)
    rb = jnp.concatenate([b_exp, b_grp, jnp.zeros((pad,), F32)]).astype(F32).reshape(1, LANES)
    return rw, rb


def kernel(x, c, positions, ada_w, ada_b, norm1_g, norm2_g, w_in, fox_fb, mla_q_norm_g, mla_wq_up, mla_kv_norm_g, mla_wkv_up, gate_w, gate_b, branch_w, out_w, router_grp_w, router_grp_b, router_exp_w, router_exp_b, exp_w1, exp_w3, exp_w2, final_g):
    B, S, D = x.shape
    L = ada_w.shape[0]
    T = B * S
    A = 2 * T
    tb = _pick(A, 256)
    n_blocks = A // tb + N_EXPERTS
    P = n_blocks * tb

    mod = _adaln(c, ada_w, ada_b)
    cos_t, sin_t = _rope_tables(positions)
    x2 = x.reshape(T, D)
    final_g2 = final_g.reshape(1, D)

    for l in range(L):
        sh1, sc1, gt1, sh2, sc2, gt2 = [mod[l, :, i * D:(i + 1) * D].reshape(B, 1, D) for i in range(6)]
        g1 = norm1_g[l].reshape(1, D)
        g2 = norm2_g[l].reshape(1, D)
        (rq, rk, rv, rg, fq, fk, fv, mq, mk, mv, ffp) = _proj(
            x2, S, g1, sh1, sc1, _prep_w_in(w_in[l]), cos_t, sin_t,
            mla_q_norm_g[l].reshape(1, -1), _prep_wq_up(mla_wq_up[l]),
            mla_kv_norm_g[l].reshape(1, -1), mla_wkv_up[l].astype(BF16))

        ya = _retention(rq, rk, rv, rg, B, S).reshape(T, -1)
        f_cum = _fox_gate(ffp[:, FF_LANE:FF_LANE + FOX_HEADS], fox_fb[l], B, S)
        yb = _flash(fq.reshape(B, S, -1), fk.reshape(B, S, -1), fv.reshape(B, S, -1), f_cum,
                    B, S, FOX_HEADS, FOX_DH, FOX_DH).reshape(T, -1)
        yc = _flash(mq.reshape(B, S, -1), mk.reshape(B, S, -1), mv.reshape(B, S, -1), None,
                    B, S, MLA_HEADS, MLA_DQ, MLA_V).reshape(T, -1)

        rw, rb = _prep_router(router_grp_w[l], router_grp_b[l], router_exp_w[l], router_exp_b[l])
        x1, h2, mi, mf, cnt = _merge(
            x2, S, ya, yb, yc, g1, sh1, sc1, gt1, g2, sh2, sc2,
            gate_w[l].astype(BF16), gate_b[l].reshape(1, -1), branch_w[l].astype(BF16),
            out_w[l].astype(BF16), rw, rb)

        counts = cnt[0, :N_EXPERTS].astype(I32)
        pcounts = (counts + tb - 1) // tb * tb
        pends = jnp.cumsum(pcounts)
        pstarts = pends - pcounts
        dest = (pstarts[mi[:, 0:2]] + mi[:, 2:4]).reshape(A)
        blk_e = jnp.minimum(jnp.searchsorted(pends, jnp.arange(n_blocks, dtype=I32) * tb, side='right'),
                            N_EXPERTS - 1).astype(I32)
        nvalid = (pends[-1:] // tb).astype(I32)

        xs = _dispatch(h2, dest, P)
        ys = _experts(xs, blk_e, nvalid, exp_w1[l].astype(BF16), exp_w3[l].astype(BF16),
                      exp_w2[l].astype(BF16), tb)
        x2 = _combine(x1, S, mf, gt2, final_g2, ys, dest, final=(l == L - 1))

    return x2.reshape(B, S, D)
```

```python
import functools
import math

import jax
import jax.numpy as jnp
from jax import lax
from jax.experimental import pallas as pl
from jax.experimental.pallas import tpu as pltpu

F32 = jnp.float32
BF16 = jnp.bfloat16
I32 = jnp.int32
HIGHEST = lax.Precision.HIGHEST

EPS = 1e-6
ROPE_THETA = 10000.0
RET_HEADS = 4
RET_DK = 64
RET_DV = 128
RET_CHUNK = 128
FOX_HEADS = 4
FOX_DH = 128
MLA_HEADS = 4
MLA_Q_RANK = 256
MLA_KV_RANK = 128
MLA_NOPE = 128
MLA_ROPE = 64
MLA_V = 128
MLA_DQ = 256
N_BRANCH = 3
BRANCH_W = 512
N_GROUPS = 4
EXP_PER_GROUP = 8
N_EXPERTS = N_GROUPS * EXP_PER_GROUP
D_EXPERT = 512

LANES = 128
V7X_VMEM_LIMIT = 56 * 1024 * 1024

C_RQ, C_RK, C_RV, C_RG = 0, 256, 512, 1024
C_FQ, C_FK, C_FV = 1536, 2048, 2560
C_MQ, C_MKV, C_TAIL = 3072, 3328, 3456
D_IN_PAD = 3584
FF_LANE = MLA_ROPE

NEG_BIG = -1e30
LOG2E = math.log2(math.e)


def _cparams(sem):
    return pltpu.CompilerParams(dimension_semantics=sem, vmem_limit_bytes=V7X_VMEM_LIMIT)


def _pick(n, pref):
    t = min(n, pref)
    assert n % t == 0, (n, t)
    return t


def _adaln_kernel(c_ref, w_ref, b_ref, o_ref):
    c = c_ref[...]
    ca = c * jax.nn.sigmoid(c)
    o_ref[...] = jnp.dot(ca, w_ref[...], preferred_element_type=F32, precision=HIGHEST) + b_ref[...]


def _adaln(c, ada_w, ada_b):
    L, D, N = ada_w.shape
    B = c.shape[0]
    tn = _pick(N, 1536)
    return pl.pallas_call(
        _adaln_kernel,
        grid=(L, N // tn),
        in_specs=[pl.BlockSpec((B, D), lambda l, j: (0, 0)),
                  pl.BlockSpec((None, D, tn), lambda l, j: (l, 0, j)),
                  pl.BlockSpec((None, 1, tn), lambda l, j: (l, 0, j))],
        out_specs=pl.BlockSpec((None, B, tn), lambda l, j: (l, 0, j)),
        out_shape=jax.ShapeDtypeStruct((L, B, N), F32),
        compiler_params=_cparams(("arbitrary", "arbitrary")),
        name="adaln",
    )(c, ada_w, ada_b.reshape(L, 1, N))


def _rope_table_kernel(pos_ref, inv_ref, sign_ref, cos_ref, sin_ref):
    ang = pos_ref[...].astype(F32) * inv_ref[...]
    cos_ref[...] = jnp.cos(ang)
    sin_ref[...] = jnp.sin(ang) * sign_ref[...]


def _rope_tables(positions):
    T = positions.size
    tm = _pick(T, 2048)
    half = MLA_ROPE // 2
    inv = ROPE_THETA ** (-jnp.arange(0, MLA_ROPE, 2, dtype=F32) / MLA_ROPE)
    inv_t = jnp.tile(inv, LANES // half).reshape(1, LANES)
    sign = jnp.where((jnp.arange(LANES) % MLA_ROPE) < half, -1.0, 1.0).astype(F32).reshape(1, LANES)
    return pl.pallas_call(
        _rope_table_kernel,
        grid=(T // tm,),
        in_specs=[pl.BlockSpec((tm, 1), lambda i: (i, 0)),
                  pl.BlockSpec((1, LANES), lambda i: (0, 0)),
                  pl.BlockSpec((1, LANES), lambda i: (0, 0))],
        out_specs=[pl.BlockSpec((tm, LANES), lambda i: (i, 0))] * 2,
        out_shape=[jax.ShapeDtypeStruct((T, LANES), F32)] * 2,
        compiler_params=_cparams(("arbitrary",)),
        name="rope_tables",
    )(positions.reshape(T, 1), inv_t, sign)


def _rope_slab(x, cos_t, sin_t, lane):
    nxt = pltpu.roll(x, LANES - 32, axis=1)
    prv = pltpu.roll(x, 32, axis=1)
    swapped = jnp.where((lane & 32) == 0, nxt, prv)
    return x * cos_t + swapped * sin_t


def _norm_mod(x, g, shift, scale):
    y = x * lax.rsqrt(jnp.mean(x * x, axis=-1, keepdims=True) + EPS)
    return (y * g) * (1.0 + scale) + shift


def _rms(x, g):
    return x * lax.rsqrt(jnp.mean(x * x, axis=-1, keepdims=True) + EPS) * g


def _proj_kernel(x_ref, g_ref, sh_ref, sc_ref, w_ref, cos_ref, sin_ref,
                 gq_ref, wq_ref, gkv_ref, wkv_ref,
                 rq_ref, rk_ref, rv_ref, rg_ref, fq_ref, fk_ref, fv_ref,
                 mq_ref, mk_ref, kpe_ref, mv_ref, ff_ref):
    h = _norm_mod(x_ref[...], g_ref[...], sh_ref[...], sc_ref[...]).astype(BF16)
    cos_t = cos_ref[...]
    sin_t = sin_ref[...]
    lane = lax.broadcasted_iota(I32, cos_t.shape, 1)

    def proj(c0, width):
        return jnp.dot(h, w_ref[:, c0:c0 + width], preferred_element_type=F32)

    rq = proj(C_RQ, 256)
    rk = proj(C_RK, 256)
    for s in range(2):
        sl = slice(s * LANES, (s + 1) * LANES)
        rq_ref[:, sl] = _rope_slab(rq[:, sl], cos_t, sin_t, lane).astype(BF16)
        rk_ref[:, sl] = (_rope_slab(rk[:, sl], cos_t, sin_t, lane) * (RET_DK ** -0.5)).astype(BF16)
    rv_ref[...] = proj(C_RV, 512).astype(BF16)
    rg_ref[...] = proj(C_RG, 512).astype(BF16)
    fq_ref[...] = (proj(C_FQ, 512) * (FOX_DH ** -0.5 * LOG2E)).astype(BF16)
    fk_ref[...] = proj(C_FK, 512).astype(BF16)
    fv_ref[...] = proj(C_FV, 512).astype(BF16)

    tail = proj(C_TAIL, LANES)
    ff_ref[...] = tail
    kpe_ref[...] = jnp.where(lane < MLA_ROPE, _rope_slab(tail, cos_t, sin_t, lane), 0.0).astype(BF16)

    qn = _rms(proj(C_MQ, MLA_Q_RANK), gq_ref[...]).astype(BF16)
    qh = jnp.dot(qn, wq_ref[...], preferred_element_type=F32)
    q_scale = (MLA_NOPE + MLA_ROPE) ** -0.5 * LOG2E
    for hd in range(MLA_HEADS):
        c0 = hd * MLA_DQ
        mq_ref[:, c0:c0 + LANES] = (qh[:, c0:c0 + LANES] * q_scale).astype(BF16)
        pe = _rope_slab(qh[:, c0 + LANES:c0 + 2 * LANES], cos_t, sin_t, lane)
        mq_ref[:, c0 + LANES:c0 + 2 * LANES] = jnp.where(lane < MLA_ROPE, pe * q_scale, 0.0).astype(BF16)

    kvn = _rms(proj(C_MKV, MLA_KV_RANK), gkv_ref[...]).astype(BF16)
    kvh = jnp.dot(kvn, wkv_ref[...], preferred_element_type=F32)
    for hd in range(MLA_HEADS):
        c0 = hd * (MLA_NOPE + MLA_V)
        mk_ref[:, hd * MLA_NOPE:(hd + 1) * MLA_NOPE] = kvh[:, c0:c0 + MLA_NOPE].astype(BF16)
        mv_ref[:, hd * MLA_V:(hd + 1) * MLA_V] = kvh[:, c0 + MLA_NOPE:c0 + MLA_NOPE + MLA_V].astype(BF16)


def _proj(x2, S, g, sh, sc, w_all, cos_t, sin_t, gq, wq, gkv, wkv):
    T, D = x2.shape
    tm = _pick(S, 512)
    per_b = S // tm
    row = lambda i: (i, 0)
    const = lambda i: (0, 0)
    batch = lambda i: (i // per_b, 0, 0)
    widths = [256, 256, 512, 512, 512, 512, 512, MLA_HEADS * MLA_DQ, MLA_HEADS * MLA_NOPE, LANES,
              MLA_HEADS * MLA_V]
    out_shape = [jax.ShapeDtypeStruct((T, w), BF16) for w in widths]
    out_shape.append(jax.ShapeDtypeStruct((T, LANES), F32))
    out_specs = [pl.BlockSpec((tm, w), row) for w in widths] + [pl.BlockSpec((tm, LANES), row)]
    return pl.pallas_call(
        _proj_kernel,
        grid=(T // tm,),
        in_specs=[pl.BlockSpec((tm, D), row),
                  pl.BlockSpec((1, D), const),
                  pl.BlockSpec((None, 1, D), batch),
                  pl.BlockSpec((None, 1, D), batch),
                  pl.BlockSpec((D, D_IN_PAD), const),
                  pl.BlockSpec((tm, LANES), row),
                  pl.BlockSpec((tm, LANES), row),
                  pl.BlockSpec((1, MLA_Q_RANK), const),
                  pl.BlockSpec((MLA_Q_RANK, MLA_HEADS * MLA_DQ), const),
                  pl.BlockSpec((1, MLA_KV_RANK), const),
                  pl.BlockSpec((MLA_KV_RANK, MLA_HEADS * (MLA_NOPE + MLA_V)), const)],
        out_specs=out_specs,
        out_shape=out_shape,
        compiler_params=_cparams(("arbitrary",)),
        name="proj",
    )(x2, g, sh, sc, w_all, cos_t, sin_t, gq, wq, gkv, wkv)


def _split3(x):
    a = x.astype(BF16)
    r = x - a.astype(F32)
    b = r.astype(BF16)
    c = (r - b.astype(F32)).astype(BF16)
    return a, b, c


def _fox_gate_kernel(ff_ref, fb_ref, tri_ref, kb_ref, carry_ref, *, tiles_per_seq):
    @pl.when(pl.program_id(0) % tiles_per_seq == 0)
    def _():
        carry_ref[...] = jnp.zeros_like(carry_ref)

    tm = ff_ref.shape[0]
    lane = lax.broadcasted_iota(I32, (tm, LANES), 1)
    z = ff_ref[...] + fb_ref[...]
    ls = -(jnp.maximum(-z, 0.0) + jnp.log1p(jnp.exp(-jnp.abs(z))))
    ls = jnp.where((lane >= FF_LANE) & (lane < FF_LANE + FOX_HEADS), ls, 0.0)
    tri = tri_ref[...]
    f = carry_ref[...]
    for part in _split3(ls):
        f = f + jnp.dot(tri, part, preferred_element_type=F32)
    carry_ref[...] = f[tm - 1:tm, :]
    f2 = f * LOG2E
    for hd in range(FOX_HEADS):
        src = FF_LANE + hd
        g = jnp.where(lane == 0, pltpu.roll(f2, (LANES - src) % LANES, axis=1),
            jnp.where(lane == 1, pltpu.roll(f2, (LANES + 1 - src) % LANES, axis=1),
            jnp.where(lane == 2, pltpu.roll(f2, (LANES + 2 - src) % LANES, axis=1), 0.0)))
        hi, mid, lo = _split3(g)
        kb_ref[:, hd * LANES:(hd + 1) * LANES] = jnp.where(lane == 0, hi, jnp.where(lane == 1, mid, lo))


def _fox_gate(ffp, fb, S):
    T = ffp.shape[0]
    tm = _pick(S, 512)
    idx = jnp.arange(tm)
    tri = (idx[None, :] <= idx[:, None]).astype(BF16)
    fbv = jnp.zeros((1, LANES), F32).at[0, FF_LANE:FF_LANE + FOX_HEADS].set(fb)
    return pl.pallas_call(
        functools.partial(_fox_gate_kernel, tiles_per_seq=S // tm),
        grid=(T // tm,),
        in_specs=[pl.BlockSpec((tm, LANES), lambda i: (i, 0)),
                  pl.BlockSpec((1, LANES), lambda i: (0, 0)),
                  pl.BlockSpec((tm, tm), lambda i: (0, 0))],
        out_specs=pl.BlockSpec((tm, FOX_HEADS * LANES), lambda i: (i, 0)),
        out_shape=jax.ShapeDtypeStruct((T, FOX_HEADS * LANES), BF16),
        scratch_shapes=[pltpu.VMEM((1, LANES), F32)],
        compiler_params=_cparams(("arbitrary",)),
        name="fox_gate",
    )(ffp, fbv, tri)


def _retention_kernel(dchunk_ref, q_ref, k_ref, v_ref, g_ref, dmask_ref, din_ref, dout_ref,
                      o_ref, state_ref, *, n_chunks):
    @pl.when(pl.program_id(1) == 0)
    def _():
        state_ref[...] = jnp.zeros_like(state_ref)

    C = RET_CHUNK
    lane = lax.broadcasted_iota(I32, (C, LANES), 1)
    for ci in range(n_chunks):
        rows = slice(ci * C, (ci + 1) * C)
        for hd in range(RET_HEADS):
            slab = slice((hd // 2) * LANES, (hd // 2 + 1) * LANES)
            mine = (lane < RET_DK) if hd % 2 == 0 else (lane >= RET_DK)
            q = jnp.where(mine, q_ref[rows, slab], 0)
            k = jnp.where(mine, k_ref[rows, slab], 0)
            vcols = slice(hd * RET_DV, (hd + 1) * RET_DV)
            v = v_ref[rows, vcols]
            state = state_ref[hd]
            scores = lax.dot_general(q, k, (((1,), (1,)), ((), ())),
                                     preferred_element_type=F32) * dmask_ref[hd]
            inner = jnp.dot(scores.astype(BF16), v, preferred_element_type=F32)
            cross = jnp.dot(q, state.astype(BF16), preferred_element_type=F32) * din_ref[hd]
            o = inner + cross
            vd = (v.astype(F32) * dout_ref[hd]).astype(BF16)
            kv = lax.dot_general(k, vd, (((0,), (0,)), ((), ())), preferred_element_type=F32)
            state_ref[hd] = state * dchunk_ref[hd] + kv
            mu = jnp.mean(o, axis=-1, keepdims=True)
            d = o - mu
            var = jnp.mean(d * d, axis=-1, keepdims=True)
            on = d * lax.rsqrt(var + EPS)
            g = g_ref[rows, vcols].astype(F32)
            o_ref[rows, vcols] = (g * jax.nn.sigmoid(g) * on).astype(BF16)


def _retention(rq, rk, rv, rg, B, S):
    H, C = RET_HEADS, RET_CHUNK
    tr = _pick(S, 4 * C)
    n_chunks = tr // C
    log_gamma = jnp.log1p(-jnp.exp2(-5.0 - jnp.arange(H, dtype=F32)))
    idx = jnp.arange(C, dtype=F32)
    rel = idx[:, None] - idx[None, :]
    dmask = jnp.where(rel >= 0, jnp.exp(log_gamma[:, None, None] * jnp.maximum(rel, 0.0)), 0.0)
    decay_in = jnp.exp(log_gamma[:, None] * (idx + 1.0))
    decay_out = jnp.exp(log_gamma[:, None] * (C - 1.0 - idx))
    decay_chunk = jnp.exp(log_gamma * C)
    din = jnp.broadcast_to(decay_in[:, :, None], (H, C, RET_DV))
    dout = jnp.broadcast_to(decay_out[:, :, None], (H, C, RET_DV))
    tok = lambda b, i: (b, i, 0)
    const3 = lambda b, i: (0, 0, 0)
    return pl.pallas_call(
        functools.partial(_retention_kernel, n_chunks=n_chunks),
        grid=(B, S // tr),
        in_specs=[pl.BlockSpec(memory_space=pltpu.SMEM),
                  pl.BlockSpec((None, tr, H * RET_DK), tok),
                  pl.BlockSpec((None, tr, H * RET_DK), tok),
                  pl.BlockSpec((None, tr, H * RET_DV), tok),
                  pl.BlockSpec((None, tr, H * RET_DV), tok),
                  pl.BlockSpec((H, C, C), const3),
                  pl.BlockSpec((H, C, RET_DV), const3),
                  pl.BlockSpec((H, C, RET_DV), const3)],
        out_specs=pl.BlockSpec((None, tr, H * RET_DV), tok),
        out_shape=jax.ShapeDtypeStruct((B, S, H * RET_DV), BF16),
        scratch_shapes=[pltpu.VMEM((H, LANES, RET_DV), F32)],
        compiler_params=_cparams(("arbitrary", "arbitrary")),
        name="retention",
    )(decay_chunk, rq.reshape(B, S, -1), rk.reshape(B, S, -1), rv.reshape(B, S, -1),
      rg.reshape(B, S, -1), dmask, din, dout)


def _flash_kernel(q_ref, k_ref, ke_ref, v_ref, o_ref, m_ref, l_ref, acc_ref, s_ref, *, q_bias_cols):
    qi = pl.program_id(2)
    tq = q_ref.shape[0]
    tk = tq
    reps = tk // LANES
    m_ref[...] = jnp.full(m_ref.shape, NEG_BIG, F32)
    l_ref[...] = jnp.zeros_like(l_ref)
    acc_ref[...] = jnp.zeros_like(acc_ref)
    q = q_ref[...]
    if q_bias_cols:
        lane = lax.broadcasted_iota(I32, (tq, LANES), 1)
        q = jnp.concatenate([q, jnp.where(lane < q_bias_cols, -1.0, 0.0).astype(BF16)], axis=1)

    def scores(j):
        start = pl.multiple_of(j * tk, tk)
        kj = jnp.concatenate([k_ref[pl.ds(start, tk), :], ke_ref[pl.ds(start, tk), :]], axis=1)
        return lax.dot_general(q, kj, (((1,), (1,)), ((), ())), preferred_element_type=F32)

    def update(s, j):
        start = pl.multiple_of(j * tk, tk)
        vj = v_ref[pl.ds(start, tk), :]
        m_prev = m_ref[...]
        m_new = jnp.maximum(m_prev, jnp.max(s, axis=-1, keepdims=True))
        alpha = jnp.exp2(m_prev - m_new)
        p = jnp.exp2(s - jnp.concatenate([m_new] * reps, axis=1))
        l_ref[...] = alpha * l_ref[...] + jnp.sum(p, axis=-1, keepdims=True)
        acc_ref[...] = alpha * acc_ref[...] + jnp.dot(p.astype(BF16), vj, preferred_element_type=F32)
        m_ref[...] = m_new

    def causal(s):
        r = lax.broadcasted_iota(I32, s.shape, 0)
        c = lax.broadcasted_iota(I32, s.shape, 1)
        return jnp.where(c <= r, s, NEG_BIG)

    s_ref[0] = scores(0)

    def pair(p, carry):
        j = 2 * p
        s = s_ref[0]
        s_ref[1] = scores(j + 1)
        update(s, j)
        s = s_ref[1]
        s_ref[0] = scores(j + 2)
        update(s, j + 1)
        return carry

    lax.fori_loop(0, qi // 2, pair, 0)

    @pl.when(qi % 2 == 1)
    def _():
        s = s_ref[0]
        s_ref[1] = scores(qi)
        update(s, qi - 1)
        update(causal(s_ref[1]), qi)

    @pl.when(qi % 2 == 0)
    def _():
        update(causal(s_ref[0]), qi)

    o_ref[...] = (acc_ref[...] / l_ref[...]).astype(o_ref.dtype)


def _flash(q, k, ke, v, B, S, H, dq, ke_per_head, q_bias_cols, name):
    tq = _pick(S, 512)
    nq = S // tq
    ke_map = (lambda b, h, i: (b, 0, h)) if ke_per_head else (lambda b, h, i: (b, 0, 0))
    return pl.pallas_call(
        functools.partial(_flash_kernel, q_bias_cols=q_bias_cols),
        grid=(B, H, nq),
        in_specs=[pl.BlockSpec((None, tq, dq), lambda b, h, i: (b, i, h)),
                  pl.BlockSpec((None, S, LANES), lambda b, h, i: (b, 0, h)),
                  pl.BlockSpec((None, S, LANES), ke_map),
                  pl.BlockSpec((None, S, LANES), lambda b, h, i: (b, 0, h))],
        out_specs=pl.BlockSpec((None, tq, LANES), lambda b, h, i: (b, i, h)),
        out_shape=jax.ShapeDtypeStruct((B, S, H * LANES), BF16),
        scratch_shapes=[pltpu.VMEM((tq, LANES), F32), pltpu.VMEM((tq, LANES), F32),
                        pltpu.VMEM((tq, LANES), F32), pltpu.VMEM((2, tq, tq), F32)],
        compiler_params=_cparams(("arbitrary", "arbitrary", "arbitrary")),
        name=name,
    )(q, k, ke, v)


def _merge_kernel(x_ref, ya_ref, yb_ref, yc_ref, g1_ref, sh1_ref, sc1_ref, gt1_ref,
                  g2_ref, sh2_ref, sc2_ref, gw_ref, gb_ref, bw_ref, ow_ref,
                  rw_ref, rb_ref, tri_ref,
                  x1_ref, h2_ref, mi_ref, mf_ref, cnt_ref, carry_ref):
    D = x_ref.shape[1]

    @pl.when(pl.program_id(0) == 0)
    def _():
        carry_ref[...] = jnp.zeros_like(carry_ref)

    x = x_ref[...]
    h = _norm_mod(x, g1_ref[...], sh1_ref[...], sc1_ref[...]).astype(BF16)
    merged = None
    for i, y_ref in enumerate((ya_ref, yb_ref, yc_ref)):
        gate = jax.nn.sigmoid(jnp.dot(h, gw_ref[:, i * D:(i + 1) * D], preferred_element_type=F32)
                              + gb_ref[:, i * D:(i + 1) * D])
        br = jnp.dot(y_ref[...], bw_ref[i], preferred_element_type=F32)
        merged = gate * br if merged is None else merged + gate * br
    mix = jnp.dot(merged.astype(BF16), ow_ref[...], preferred_element_type=F32)
    x1 = x + gt1_ref[...] * mix
    x1_ref[...] = x1
    h2 = _norm_mod(x1, g2_ref[...], sh2_ref[...], sc2_ref[...])
    h2_ref[...] = h2

    logits = jnp.dot(h2, rw_ref[...], preferred_element_type=F32, precision=HIGHEST) + rb_ref[...]
    lane = lax.broadcasted_iota(I32, logits.shape, 1)
    big = jnp.int32(1 << 20)
    is_grp = (lane >= N_EXPERTS) & (lane < N_EXPERTS + N_GROUPS)
    gl = jnp.where(is_grp, logits, -jnp.inf)
    gmax = jnp.max(gl, axis=-1, keepdims=True)
    g_idx = jnp.min(jnp.where(gl == gmax, lane - N_EXPERTS, big), axis=-1, keepdims=True)
    g_w = 1.0 / jnp.sum(jnp.exp(gl - gmax), axis=-1, keepdims=True)
    in_grp = (lane < N_EXPERTS) & ((lane >> 3) == g_idx)
    el = jnp.where(in_grp, logits, -jnp.inf)
    e1 = jnp.max(el, axis=-1, keepdims=True)
    i1 = jnp.min(jnp.where(el == e1, lane, big), axis=-1, keepdims=True)
    el2 = jnp.where(lane == i1, -jnp.inf, el)
    e2 = jnp.max(el2, axis=-1, keepdims=True)
    i2 = jnp.min(jnp.where(el2 == e2, lane, big), axis=-1, keepdims=True)
    r = jnp.exp(e2 - e1)
    w1 = g_w / (1.0 + r)
    w2 = g_w * r / (1.0 + r)

    hit1 = lane == i1
    hit2 = lane == i2
    onehot = jnp.where(hit1 | hit2, 1.0, 0.0)
    before = jnp.dot(tri_ref[...], onehot.astype(BF16), preferred_element_type=F32) + carry_ref[...]
    rank1 = jnp.sum(jnp.where(hit1, before, 0.0), axis=-1, keepdims=True)
    rank2 = jnp.sum(jnp.where(hit2, before, 0.0), axis=-1, keepdims=True)
    carry_ref[...] = carry_ref[...] + jnp.sum(onehot, axis=0, keepdims=True)
    cnt_ref[...] = carry_ref[...]

    lane8 = lax.broadcasted_iota(I32, mi_ref.shape, 1)
    mi_ref[...] = jnp.where(lane8 == 0, i1,
                  jnp.where(lane8 == 1, i2,
                  jnp.where(lane8 == 2, rank1.astype(I32),
                  jnp.where(lane8 == 3, rank2.astype(I32), 0))))
    mf_ref[...] = jnp.where(lane8 == 0, w1, jnp.where(lane8 == 1, w2, 0.0))


def _merge(x2, S, ya, yb, yc, g1, sh1, sc1, gt1, g2, sh2, sc2, gw, gb, bw, ow, rw, rb):
    T, D = x2.shape
    tm = _pick(S, 512)
    per_b = S // tm
    row = lambda i: (i, 0)
    const = lambda i: (0, 0)
    batch = lambda i: (i // per_b, 0, 0)
    idx = jnp.arange(tm)
    tri = (idx[None, :] < idx[:, None]).astype(BF16)
    vecb = pl.BlockSpec((None, 1, D), batch)
    return pl.pallas_call(
        _merge_kernel,
        grid=(T // tm,),
        in_specs=[pl.BlockSpec((tm, D), row),
                  pl.BlockSpec((tm, BRANCH_W), row),
                  pl.BlockSpec((tm, BRANCH_W), row),
                  pl.BlockSpec((tm, BRANCH_W), row),
                  pl.BlockSpec((1, D), const), vecb, vecb, vecb,
                  pl.BlockSpec((1, D), const), vecb, vecb,
                  pl.BlockSpec((D, N_BRANCH * D), const),
                  pl.BlockSpec((1, N_BRANCH * D), const),
                  pl.BlockSpec((N_BRANCH, BRANCH_W, D), lambda i: (0, 0, 0)),
                  pl.BlockSpec((D, D), const),
                  pl.BlockSpec((D, LANES), const),
                  pl.BlockSpec((1, LANES), const),
                  pl.BlockSpec((tm, tm), const)],
        out_specs=[pl.BlockSpec((tm, D), row), pl.BlockSpec((tm, D), row),
                   pl.BlockSpec((tm, 8), row), pl.BlockSpec((tm, 8), row),
                   pl.BlockSpec((1, LANES), const)],
        out_shape=[jax.ShapeDtypeStruct((T, D), F32), jax.ShapeDtypeStruct((T, D), F32),
                   jax.ShapeDtypeStruct((T, 8), I32), jax.ShapeDtypeStruct((T, 8), F32),
                   jax.ShapeDtypeStruct((1, LANES), F32)],
        scratch_shapes=[pltpu.VMEM((1, LANES), F32)],
        compiler_params=_cparams(("arbitrary",)),
        name="merge",
    )(x2, ya, yb, yc, g1, sh1, sc1, gt1, g2, sh2, sc2, gw, gb, bw, ow, rw, rb, tri)


def _row_copy(src_ref, src_row, dst_ref, dst_row, sem):
    return pltpu.make_async_copy(src_ref.at[pl.ds(src_row, 1), :], dst_ref.at[pl.ds(dst_row, 1), :], sem)


def _dispatch_kernel(dest_ref, h_ref, xs_in_ref, xs_ref, dsm_ref, sem_ref, isem_ref):
    del xs_in_ref
    tm = h_ref.shape[0]
    cp = pltpu.make_async_copy(dest_ref.at[0], dsm_ref, isem_ref)
    cp.start()
    cp.wait()

    def issue(r, carry):
        _row_copy(h_ref, r, xs_ref, dsm_ref[0, 2 * r], sem_ref).start()
        _row_copy(h_ref, r, xs_ref, dsm_ref[0, 2 * r + 1], sem_ref).start()
        return carry

    lax.fori_loop(0, tm, issue, 0, unroll=8)

    def drain(r, carry):
        _row_copy(h_ref, 0, xs_ref, 0, sem_ref).wait()
        _row_copy(h_ref, 0, xs_ref, 0, sem_ref).wait()
        return carry

    lax.fori_loop(0, tm, drain, 0, unroll=8)


def _dispatch(h2, dest, P):
    T, D = h2.shape
    tm = _pick(T, 256)
    nt = T // tm
    xs0 = jnp.zeros((P, D), F32)
    return pl.pallas_call(
        _dispatch_kernel,
        grid=(nt,),
        in_specs=[pl.BlockSpec((1, 1, 2 * tm), lambda i: (i, 0, 0)),
                  pl.BlockSpec((tm, D), lambda i: (i, 0)),
                  pl.BlockSpec(memory_space=pl.ANY)],
        out_specs=pl.BlockSpec(memory_space=pl.ANY),
        out_shape=jax.ShapeDtypeStruct((P, D), F32),
        scratch_shapes=[pltpu.SMEM((1, 2 * tm), I32), pltpu.SemaphoreType.DMA, pltpu.SemaphoreType.DMA],
        input_output_aliases={2: 0},
        compiler_params=_cparams(("arbitrary",)),
        name="moe_dispatch",
    )(dest.reshape(nt, 1, 2 * tm), h2, xs0)


def _expert_kernel(blk_e_ref, nvalid_ref, xs_ref, w1_ref, w3_ref, w2_ref, ys_ref):
    del blk_e_ref
    valid = pl.program_id(0) < nvalid_ref[0]

    @pl.when(valid)
    def _():
        xb = xs_ref[...].astype(BF16)
        a = jnp.dot(xb, w1_ref[...], preferred_element_type=F32)
        b = jnp.dot(xb, w3_ref[...], preferred_element_type=F32)
        hid = (a * jax.nn.sigmoid(a) * b).astype(BF16)
        ys_ref[...] = jnp.dot(hid, w2_ref[...], preferred_element_type=F32)

    @pl.when(jnp.logical_not(valid))
    def _():
        ys_ref[...] = jnp.zeros_like(ys_ref)


def _experts(xs, blk_e, nvalid, w1, w3, w2, tb):
    P, D = xs.shape
    n_blocks = P // tb
    rows = lambda i, be, nv: (jnp.minimum(i, nv[0] - 1), 0)
    grid_spec = pltpu.PrefetchScalarGridSpec(
        num_scalar_prefetch=2,
        grid=(n_blocks,),
        in_specs=[pl.BlockSpec((tb, D), rows),
                  pl.BlockSpec((None, D, D_EXPERT), lambda i, be, nv: (be[i], 0, 0)),
                  pl.BlockSpec((None, D, D_EXPERT), lambda i, be, nv: (be[i], 0, 0)),
                  pl.BlockSpec((None, D_EXPERT, D), lambda i, be, nv: (be[i], 0, 0))],
        out_specs=pl.BlockSpec((tb, D), lambda i, be, nv: (i, 0)),
    )
    return pl.pallas_call(
        _expert_kernel,
        grid_spec=grid_spec,
        out_shape=jax.ShapeDtypeStruct((P, D), F32),
        compiler_params=_cparams(("arbitrary",)),
        name="moe_experts",
    )(blk_e, nvalid, xs, w1, w3, w2)


def _combine_kernel(dest_ref, x_ref, mf_ref, gt_ref, fg_ref, ys_ref, o_ref,
                    dsm_ref, ybuf_ref, sem_ref, isem_ref, *, final):
    tm = x_ref.shape[0]
    cp = pltpu.make_async_copy(dest_ref.at[0], dsm_ref, isem_ref)
    cp.start()
    cp.wait()

    def issue(r, carry):
        _row_copy(ys_ref, dsm_ref[0, 2 * r], ybuf_ref.at[0], r, sem_ref).start()
        _row_copy(ys_ref, dsm_ref[0, 2 * r + 1], ybuf_ref.at[1], r, sem_ref).start()
        return carry

    lax.fori_loop(0, tm, issue, 0, unroll=8)

    def drain(r, carry):
        _row_copy(ys_ref, 0, ybuf_ref.at[0], 0, sem_ref).wait()
        _row_copy(ys_ref, 0, ybuf_ref.at[1], 0, sem_ref).wait()
        return carry

    lax.fori_loop(0, tm, drain, 0, unroll=8)

    mf = mf_ref[...]
    ffn = ybuf_ref[0] * mf[:, 0:1] + ybuf_ref[1] * mf[:, 1:2]
    out = x_ref[...] + gt_ref[...] * ffn
    if final:
        out = _rms(out, fg_ref[...])
    o_ref[...] = out


def _combine(x1, S, mf, gt2, final_g, ys, dest, final):
    T, D = x1.shape
    tm = _pick(S, 256)
    per_b = S // tm
    nt = T // tm
    return pl.pallas_call(
        functools.partial(_combine_kernel, final=final),
        grid=(nt,),
        in_specs=[pl.BlockSpec((1, 1, 2 * tm), lambda i: (i, 0, 0)),
                  pl.BlockSpec((tm, D), lambda i: (i, 0)),
                  pl.BlockSpec((tm, 8), lambda i: (i, 0)),
                  pl.BlockSpec((None, 1, D), lambda i: (i // per_b, 0, 0)),
                  pl.BlockSpec((1, D), lambda i: (0, 0)),
                  pl.BlockSpec(memory_space=pl.ANY)],
        out_specs=pl.BlockSpec((tm, D), lambda i: (i, 0)),
        out_shape=jax.ShapeDtypeStruct((T, D), F32),
        scratch_shapes=[pltpu.SMEM((1, 2 * tm), I32), pltpu.VMEM((2, tm, D), F32),
                        pltpu.SemaphoreType.DMA, pltpu.SemaphoreType.DMA],
        compiler_params=_cparams(("arbitrary",)),
        name="moe_combine",
    )(dest.reshape(nt, 1, 2 * tm), x1, mf, gt2, final_g, ys)


def _prep_w_in(w):
    offs = [0]
    for s in (256, 256, 512, 512, 512, 512, 512, FOX_HEADS, MLA_Q_RANK, MLA_KV_RANK, MLA_ROPE):
        offs.append(offs[-1] + s)
    rq, rk, rv, rg, fq, fk, fv, ff, mq, mkv, mkr = [w[:, offs[i]:offs[i + 1]] for i in range(11)]
    pad = jnp.zeros((w.shape[0], LANES - MLA_ROPE - FOX_HEADS), w.dtype)
    return jnp.concatenate([rq, rk, rv, rg, fq, fk, fv, mq, mkv, mkr, ff, pad], axis=1).astype(BF16)


def _prep_wq_up(w):
    r = w.reshape(MLA_Q_RANK, MLA_HEADS, MLA_NOPE + MLA_ROPE)
    r = jnp.pad(r, ((0, 0), (0, 0), (0, MLA_DQ - MLA_NOPE - MLA_ROPE)))
    return r.reshape(MLA_Q_RANK, MLA_HEADS * MLA_DQ).astype(BF16)


def _prep_router(w_grp, b_grp, w_exp, b_exp):
    D = w_grp.shape[0]
    pad = LANES - N_EXPERTS - N_GROUPS
    rw = jnp.concatenate([w_exp, w_grp, jnp.zeros((D, pad), F32)], axis=1).astype(F32)
    rb = jnp.concatenate([b_exp, b_grp, jnp.zeros((pad,), F32)]).astype(F32).reshape(1, LANES)
    return rw, rb


def kernel(x, c, positions, ada_w, ada_b, norm1_g, norm2_g, w_in, fox_fb, mla_q_norm_g, mla_wq_up, mla_kv_norm_g, mla_wkv_up, gate_w, gate_b, branch_w, out_w, router_grp_w, router_grp_b, router_exp_w, router_exp_b, exp_w1, exp_w3, exp_w2, final_g):
    B, S, D = x.shape
    L = ada_w.shape[0]
    T = B * S
    A = 2 * T
    tb = _pick(A, 256)
    n_blocks = A // tb + N_EXPERTS
    P = n_blocks * tb

    mod = _adaln(c, ada_w, ada_b)
    cos_t, sin_t = _rope_tables(positions)
    x2 = x.reshape(T, D)
    final_g2 = final_g.reshape(1, D)

    for l in range(L):
        sh1, sc1, gt1, sh2, sc2, gt2 = [mod[l, :, i * D:(i + 1) * D].reshape(B, 1, D) for i in range(6)]
        g1 = norm1_g[l].reshape(1, D)
        g2 = norm2_g[l].reshape(1, D)
        (rq, rk, rv, rg, fq, fk, fv, mq, mk, kpe, mv, ffp) = _proj(
            x2, S, g1, sh1, sc1, _prep_w_in(w_in[l]), cos_t, sin_t,
            mla_q_norm_g[l].reshape(1, -1), _prep_wq_up(mla_wq_up[l]),
            mla_kv_norm_g[l].reshape(1, -1), mla_wkv_up[l].astype(BF16))

        ya = _retention(rq, rk, rv, rg, B, S).reshape(T, -1)
        kb = _fox_gate(ffp, fox_fb[l], S)
        r3 = lambda a: a.reshape(B, S, -1)
        yb = _flash(r3(fq), r3(fk), r3(kb), r3(fv), B, S, FOX_HEADS, FOX_DH, True, 3, "flash_fox").reshape(T, -1)
        yc = _flash(r3(mq), r3(mk), r3(kpe), r3(mv), B, S, MLA_HEADS, MLA_DQ, False, 0, "flash_mla").reshape(T, -1)

        rw, rb = _prep_router(router_grp_w[l], router_grp_b[l], router_exp_w[l], router_exp_b[l])
        x1, h2, mi, mf, cnt = _merge(
            x2, S, ya, yb, yc, g1, sh1, sc1, gt1, g2, sh2, sc2,
            gate_w[l].astype(BF16), gate_b[l].reshape(1, -1), branch_w[l].astype(BF16),
            out_w[l].astype(BF16), rw, rb)

        counts = cnt[0, :N_EXPERTS].astype(I32)
        pcounts = (counts + tb - 1) // tb * tb
        pends = jnp.cumsum(pcounts)
        pstarts = pends - pcounts
        dest = (pstarts[mi[:, 0:2]] + mi[:, 2:4]).reshape(A)
        blk_e = jnp.minimum(jnp.searchsorted(pends, jnp.arange(n_blocks, dtype=I32) * tb, side='right'),
                            N_EXPERTS - 1).astype(I32)
        nvalid = (pends[-1:] // tb).astype(I32)

        xs = _dispatch(h2, dest, P)
        ys = _experts(xs, blk_e, nvalid, exp_w1[l].astype(BF16), exp_w3[l].astype(BF16),
                      exp_w2[l].astype(BF16), tb)
        x2 = _combine(x1, S, mf, gt2, final_g2, ys, dest, final=(l == L - 1))

    return x2.reshape(B, S, D)
```

```python
import functools
import math

import jax
import jax.numpy as jnp
from jax import lax
from jax.experimental import pallas as pl
from jax.experimental.pallas import tpu as pltpu

F32 = jnp.float32
BF16 = jnp.bfloat16
I32 = jnp.int32
HIGHEST = lax.Precision.HIGHEST

EPS = 1e-6
ROPE_THETA = 10000.0
RET_HEADS = 4
RET_DK = 64
RET_DV = 128
RET_CHUNK = 128
FOX_HEADS = 4
FOX_DH = 128
MLA_HEADS = 4
MLA_Q_RANK = 256
MLA_KV_RANK = 128
MLA_NOPE = 128
MLA_ROPE = 64
MLA_V = 128
MLA_DQ = 256
N_BRANCH = 3
BRANCH_W = 512
N_GROUPS = 4
EXP_PER_GROUP = 8
N_EXPERTS = N_GROUPS * EXP_PER_GROUP
D_EXPERT = 512

LANES = 128
V7X_VMEM_LIMIT = 56 * 1024 * 1024

C_RQ, C_RK, C_RV, C_RG = 0, 256, 512, 1024
C_FQ, C_FK, C_FV = 1536, 2048, 2560
C_MQ, C_MKV, C_TAIL = 3072, 3328, 3456
D_IN_PAD = 3584
FF_LANE = MLA_ROPE

NEG_BIG = -1e30
MERGE_COL_CHUNK = 256
LOG2E = math.log2(math.e)


def _cparams(sem):
    return pltpu.CompilerParams(dimension_semantics=sem, vmem_limit_bytes=V7X_VMEM_LIMIT)


def _pick(n, pref):
    t = min(n, pref)
    assert n % t == 0, (n, t)
    return t


def _adaln_kernel(c_ref, w_ref, b_ref, o_ref):
    c = c_ref[...]
    ca = c * jax.nn.sigmoid(c)
    o_ref[...] = jnp.dot(ca, w_ref[...], preferred_element_type=F32, precision=HIGHEST) + b_ref[...]


def _adaln(c, ada_w, ada_b):
    L, D, N = ada_w.shape
    B = c.shape[0]
    tn = _pick(N, 1536)
    return pl.pallas_call(
        _adaln_kernel,
        grid=(L, N // tn),
        in_specs=[pl.BlockSpec((B, D), lambda l, j: (0, 0)),
                  pl.BlockSpec((None, D, tn), lambda l, j: (l, 0, j)),
                  pl.BlockSpec((None, 1, tn), lambda l, j: (l, 0, j))],
        out_specs=pl.BlockSpec((None, B, tn), lambda l, j: (l, 0, j)),
        out_shape=jax.ShapeDtypeStruct((L, B, N), F32),
        compiler_params=_cparams(("arbitrary", "arbitrary")),
        name="adaln",
    )(c, ada_w, ada_b.reshape(L, 1, N))


def _rope_table_kernel(pos_ref, inv_ref, sign_ref, cos_ref, sin_ref):
    ang = pos_ref[...].astype(F32) * inv_ref[...]
    cos_ref[...] = jnp.cos(ang)
    sin_ref[...] = jnp.sin(ang) * sign_ref[...]


def _rope_tables(positions):
    T = positions.size
    tm = _pick(T, 2048)
    half = MLA_ROPE // 2
    inv = ROPE_THETA ** (-jnp.arange(0, MLA_ROPE, 2, dtype=F32) / MLA_ROPE)
    inv_t = jnp.tile(inv, LANES // half).reshape(1, LANES)
    sign = jnp.where((jnp.arange(LANES) % MLA_ROPE) < half, -1.0, 1.0).astype(F32).reshape(1, LANES)
    return pl.pallas_call(
        _rope_table_kernel,
        grid=(T // tm,),
        in_specs=[pl.BlockSpec((tm, 1), lambda i: (i, 0)),
                  pl.BlockSpec((1, LANES), lambda i: (0, 0)),
                  pl.BlockSpec((1, LANES), lambda i: (0, 0))],
        out_specs=[pl.BlockSpec((tm, LANES), lambda i: (i, 0))] * 2,
        out_shape=[jax.ShapeDtypeStruct((T, LANES), F32)] * 2,
        compiler_params=_cparams(("arbitrary",)),
        name="rope_tables",
    )(positions.reshape(T, 1), inv_t, sign)


def _rope_slab(x, cos_t, sin_t, lane):
    nxt = pltpu.roll(x, LANES - 32, axis=1)
    prv = pltpu.roll(x, 32, axis=1)
    swapped = jnp.where((lane & 32) == 0, nxt, prv)
    return x * cos_t + swapped * sin_t


def _norm_mod(x, g, shift, scale):
    y = x * lax.rsqrt(jnp.mean(x * x, axis=-1, keepdims=True) + EPS)
    return (y * g) * (1.0 + scale) + shift


def _rms(x, g):
    return x * lax.rsqrt(jnp.mean(x * x, axis=-1, keepdims=True) + EPS) * g


def _proj_kernel(x_ref, g_ref, sh_ref, sc_ref, w_ref, cos_ref, sin_ref,
                 gq_ref, wq_ref, gkv_ref, wkv_ref,
                 rq_ref, rk_ref, rv_ref, rg_ref, fq_ref, fk_ref, fv_ref,
                 mq_ref, mk_ref, kpe_ref, mv_ref, ff_ref):
    h = _norm_mod(x_ref[...], g_ref[...], sh_ref[...], sc_ref[...]).astype(BF16)
    cos_t = cos_ref[...]
    sin_t = sin_ref[...]
    lane = lax.broadcasted_iota(I32, cos_t.shape, 1)

    def proj(c0, width):
        return jnp.dot(h, w_ref[:, c0:c0 + width], preferred_element_type=F32)

    rq = proj(C_RQ, 256)
    rk = proj(C_RK, 256)
    for s in range(2):
        sl = slice(s * LANES, (s + 1) * LANES)
        rq_ref[:, sl] = _rope_slab(rq[:, sl], cos_t, sin_t, lane).astype(BF16)
        rk_ref[:, sl] = (_rope_slab(rk[:, sl], cos_t, sin_t, lane) * (RET_DK ** -0.5)).astype(BF16)
    rv_ref[...] = proj(C_RV, 512).astype(BF16)
    rg_ref[...] = proj(C_RG, 512).astype(BF16)
    fq_ref[...] = (proj(C_FQ, 512) * (FOX_DH ** -0.5 * LOG2E)).astype(BF16)
    fk_ref[...] = proj(C_FK, 512).astype(BF16)
    fv_ref[...] = proj(C_FV, 512).astype(BF16)

    tail = proj(C_TAIL, LANES)
    ff_ref[...] = tail
    kpe_ref[...] = jnp.where(lane < MLA_ROPE, _rope_slab(tail, cos_t, sin_t, lane), 0.0).astype(BF16)

    qn = _rms(proj(C_MQ, MLA_Q_RANK), gq_ref[...]).astype(BF16)
    qh = jnp.dot(qn, wq_ref[...], preferred_element_type=F32)
    q_scale = (MLA_NOPE + MLA_ROPE) ** -0.5 * LOG2E
    for hd in range(MLA_HEADS):
        c0 = hd * MLA_DQ
        mq_ref[:, c0:c0 + LANES] = (qh[:, c0:c0 + LANES] * q_scale).astype(BF16)
        pe = _rope_slab(qh[:, c0 + LANES:c0 + 2 * LANES], cos_t, sin_t, lane)
        mq_ref[:, c0 + LANES:c0 + 2 * LANES] = jnp.where(lane < MLA_ROPE, pe * q_scale, 0.0).astype(BF16)

    kvn = _rms(proj(C_MKV, MLA_KV_RANK), gkv_ref[...]).astype(BF16)
    kvh = jnp.dot(kvn, wkv_ref[...], preferred_element_type=F32)
    for hd in range(MLA_HEADS):
        c0 = hd * (MLA_NOPE + MLA_V)
        mk_ref[:, hd * MLA_NOPE:(hd + 1) * MLA_NOPE] = kvh[:, c0:c0 + MLA_NOPE].astype(BF16)
        mv_ref[:, hd * MLA_V:(hd + 1) * MLA_V] = kvh[:, c0 + MLA_NOPE:c0 + MLA_NOPE + MLA_V].astype(BF16)


def _proj(x2, S, g, sh, sc, w_all, cos_t, sin_t, gq, wq, gkv, wkv):
    T, D = x2.shape
    tm = _pick(S, 512)
    per_b = S // tm
    row = lambda i: (i, 0)
    const = lambda i: (0, 0)
    batch = lambda i: (i // per_b, 0, 0)
    widths = [256, 256, 512, 512, 512, 512, 512, MLA_HEADS * MLA_DQ, MLA_HEADS * MLA_NOPE, LANES,
              MLA_HEADS * MLA_V]
    out_shape = [jax.ShapeDtypeStruct((T, w), BF16) for w in widths]
    out_shape.append(jax.ShapeDtypeStruct((T, LANES), F32))
    out_specs = [pl.BlockSpec((tm, w), row) for w in widths] + [pl.BlockSpec((tm, LANES), row)]
    return pl.pallas_call(
        _proj_kernel,
        grid=(T // tm,),
        in_specs=[pl.BlockSpec((tm, D), row),
                  pl.BlockSpec((1, D), const),
                  pl.BlockSpec((None, 1, D), batch),
                  pl.BlockSpec((None, 1, D), batch),
                  pl.BlockSpec((D, D_IN_PAD), const),
                  pl.BlockSpec((tm, LANES), row),
                  pl.BlockSpec((tm, LANES), row),
                  pl.BlockSpec((1, MLA_Q_RANK), const),
                  pl.BlockSpec((MLA_Q_RANK, MLA_HEADS * MLA_DQ), const),
                  pl.BlockSpec((1, MLA_KV_RANK), const),
                  pl.BlockSpec((MLA_KV_RANK, MLA_HEADS * (MLA_NOPE + MLA_V)), const)],
        out_specs=out_specs,
        out_shape=out_shape,
        compiler_params=_cparams(("arbitrary",)),
        name="proj",
    )(x2, g, sh, sc, w_all, cos_t, sin_t, gq, wq, gkv, wkv)


def _split3(x):
    a = x.astype(BF16)
    r = x - a.astype(F32)
    b = r.astype(BF16)
    c = (r - b.astype(F32)).astype(BF16)
    return a, b, c


def _fox_gate_kernel(ff_ref, fb_ref, tri_ref, kb_ref, carry_ref, *, tiles_per_seq):
    @pl.when(pl.program_id(0) % tiles_per_seq == 0)
    def _():
        carry_ref[...] = jnp.zeros_like(carry_ref)

    tm = ff_ref.shape[0]
    lane = lax.broadcasted_iota(I32, (tm, LANES), 1)
    z = ff_ref[...] + fb_ref[...]
    ls = -(jnp.maximum(-z, 0.0) + jnp.log1p(jnp.exp(-jnp.abs(z))))
    ls = jnp.where((lane >= FF_LANE) & (lane < FF_LANE + FOX_HEADS), ls, 0.0)
    tri = tri_ref[...]
    f = carry_ref[...]
    for part in _split3(ls):
        f = f + jnp.dot(tri, part, preferred_element_type=F32)
    carry_ref[...] = f[tm - 1:tm, :]
    f2 = f * LOG2E
    for hd in range(FOX_HEADS):
        src = FF_LANE + hd
        g = jnp.where(lane == 0, pltpu.roll(f2, (LANES - src) % LANES, axis=1),
            jnp.where(lane == 1, pltpu.roll(f2, (LANES + 1 - src) % LANES, axis=1),
            jnp.where(lane == 2, pltpu.roll(f2, (LANES + 2 - src) % LANES, axis=1), 0.0)))
        hi, mid, lo = _split3(g)
        kb_ref[:, hd * LANES:(hd + 1) * LANES] = jnp.where(lane == 0, hi, jnp.where(lane == 1, mid, lo))


def _fox_gate(ffp, fb, S):
    T = ffp.shape[0]
    tm = _pick(S, 512)
    idx = jnp.arange(tm)
    tri = (idx[None, :] <= idx[:, None]).astype(BF16)
    fbv = jnp.zeros((1, LANES), F32).at[0, FF_LANE:FF_LANE + FOX_HEADS].set(fb)
    return pl.pallas_call(
        functools.partial(_fox_gate_kernel, tiles_per_seq=S // tm),
        grid=(T // tm,),
        in_specs=[pl.BlockSpec((tm, LANES), lambda i: (i, 0)),
                  pl.BlockSpec((1, LANES), lambda i: (0, 0)),
                  pl.BlockSpec((tm, tm), lambda i: (0, 0))],
        out_specs=pl.BlockSpec((tm, FOX_HEADS * LANES), lambda i: (i, 0)),
        out_shape=jax.ShapeDtypeStruct((T, FOX_HEADS * LANES), BF16),
        scratch_shapes=[pltpu.VMEM((1, LANES), F32)],
        compiler_params=_cparams(("arbitrary",)),
        name="fox_gate",
    )(ffp, fbv, tri)


def _retention_kernel(dchunk_ref, q_ref, k_ref, v_ref, g_ref, dmask_ref, din_ref, dout_ref,
                      o_ref, state_ref, *, n_chunks):
    @pl.when(pl.program_id(1) == 0)
    def _():
        state_ref[...] = jnp.zeros_like(state_ref)

    C = RET_CHUNK
    lane = lax.broadcasted_iota(I32, (C, LANES), 1)
    for ci in range(n_chunks):
        rows = slice(ci * C, (ci + 1) * C)
        for hd in range(RET_HEADS):
            slab = slice((hd // 2) * LANES, (hd // 2 + 1) * LANES)
            mine = (lane < RET_DK) if hd % 2 == 0 else (lane >= RET_DK)
            q = jnp.where(mine, q_ref[rows, slab], 0)
            k = jnp.where(mine, k_ref[rows, slab], 0)
            vcols = slice(hd * RET_DV, (hd + 1) * RET_DV)
            v = v_ref[rows, vcols]
            state = state_ref[hd]
            scores = lax.dot_general(q, k, (((1,), (1,)), ((), ())),
                                     preferred_element_type=F32) * dmask_ref[hd]
            inner = jnp.dot(scores.astype(BF16), v, preferred_element_type=F32)
            cross = jnp.dot(q, state.astype(BF16), preferred_element_type=F32) * din_ref[hd]
            o = inner + cross
            vd = (v.astype(F32) * dout_ref[hd]).astype(BF16)
            kv = lax.dot_general(k, vd, (((0,), (0,)), ((), ())), preferred_element_type=F32)
            state_ref[hd] = state * dchunk_ref[hd] + kv
            mu = jnp.mean(o, axis=-1, keepdims=True)
            d = o - mu
            var = jnp.mean(d * d, axis=-1, keepdims=True)
            on = d * lax.rsqrt(var + EPS)
            g = g_ref[rows, vcols].astype(F32)
            o_ref[rows, vcols] = (g * jax.nn.sigmoid(g) * on).astype(BF16)


def _retention(rq, rk, rv, rg, B, S):
    H, C = RET_HEADS, RET_CHUNK
    tr = _pick(S, 4 * C)
    n_chunks = tr // C
    log_gamma = jnp.log1p(-jnp.exp2(-5.0 - jnp.arange(H, dtype=F32)))
    idx = jnp.arange(C, dtype=F32)
    rel = idx[:, None] - idx[None, :]
    dmask = jnp.where(rel >= 0, jnp.exp(log_gamma[:, None, None] * jnp.maximum(rel, 0.0)), 0.0)
    decay_in = jnp.exp(log_gamma[:, None] * (idx + 1.0))
    decay_out = jnp.exp(log_gamma[:, None] * (C - 1.0 - idx))
    decay_chunk = jnp.exp(log_gamma * C)
    din = jnp.broadcast_to(decay_in[:, :, None], (H, C, RET_DV))
    dout = jnp.broadcast_to(decay_out[:, :, None], (H, C, RET_DV))
    tok = lambda b, i: (b, i, 0)
    const3 = lambda b, i: (0, 0, 0)
    return pl.pallas_call(
        functools.partial(_retention_kernel, n_chunks=n_chunks),
        grid=(B, S // tr),
        in_specs=[pl.BlockSpec(memory_space=pltpu.SMEM),
                  pl.BlockSpec((None, tr, H * RET_DK), tok),
                  pl.BlockSpec((None, tr, H * RET_DK), tok),
                  pl.BlockSpec((None, tr, H * RET_DV), tok),
                  pl.BlockSpec((None, tr, H * RET_DV), tok),
                  pl.BlockSpec((H, C, C), const3),
                  pl.BlockSpec((H, C, RET_DV), const3),
                  pl.BlockSpec((H, C, RET_DV), const3)],
        out_specs=pl.BlockSpec((None, tr, H * RET_DV), tok),
        out_shape=jax.ShapeDtypeStruct((B, S, H * RET_DV), BF16),
        scratch_shapes=[pltpu.VMEM((H, LANES, RET_DV), F32)],
        compiler_params=_cparams(("arbitrary", "arbitrary")),
        name="retention",
    )(decay_chunk, rq.reshape(B, S, -1), rk.reshape(B, S, -1), rv.reshape(B, S, -1),
      rg.reshape(B, S, -1), dmask, din, dout)


def _flash_kernel(q_ref, k_ref, ke_ref, v_ref, o_ref, m_ref, l_ref, acc_ref, s_ref, *, q_bias_cols):
    qi = pl.program_id(2)
    tq = q_ref.shape[0]
    tk = tq
    reps = tk // LANES
    m_ref[...] = jnp.full(m_ref.shape, NEG_BIG, F32)
    l_ref[...] = jnp.zeros_like(l_ref)
    acc_ref[...] = jnp.zeros_like(acc_ref)
    q = q_ref[...]
    if q_bias_cols:
        lane = lax.broadcasted_iota(I32, (tq, LANES), 1)
        q = jnp.concatenate([q, jnp.where(lane < q_bias_cols, -1.0, 0.0).astype(BF16)], axis=1)

    def scores(j):
        start = pl.multiple_of(j * tk, tk)
        kj = jnp.concatenate([k_ref[pl.ds(start, tk), :], ke_ref[pl.ds(start, tk), :]], axis=1)
        return lax.dot_general(q, kj, (((1,), (1,)), ((), ())), preferred_element_type=F32)

    def update(s, j):
        start = pl.multiple_of(j * tk, tk)
        vj = v_ref[pl.ds(start, tk), :]
        m_prev = m_ref[...]
        m_new = jnp.maximum(m_prev, jnp.max(s, axis=-1, keepdims=True))
        alpha = jnp.exp2(m_prev - m_new)
        p = jnp.exp2(s - jnp.concatenate([m_new] * reps, axis=1))
        l_ref[...] = alpha * l_ref[...] + jnp.sum(p, axis=-1, keepdims=True)
        acc_ref[...] = alpha * acc_ref[...] + jnp.dot(p.astype(BF16), vj, preferred_element_type=F32)
        m_ref[...] = m_new

    def causal(s):
        r = lax.broadcasted_iota(I32, s.shape, 0)
        c = lax.broadcasted_iota(I32, s.shape, 1)
        return jnp.where(c <= r, s, NEG_BIG)

    s_ref[0] = scores(0)

    def pair(p, carry):
        j = 2 * p
        s = s_ref[0]
        s_ref[1] = scores(j + 1)
        update(s, j)
        s = s_ref[1]
        s_ref[0] = scores(j + 2)
        update(s, j + 1)
        return carry

    lax.fori_loop(0, qi // 2, pair, 0)

    @pl.when(qi % 2 == 1)
    def _():
        s = s_ref[0]
        s_ref[1] = scores(qi)
        update(s, qi - 1)
        update(causal(s_ref[1]), qi)

    @pl.when(qi % 2 == 0)
    def _():
        update(causal(s_ref[0]), qi)

    o_ref[...] = (acc_ref[...] / l_ref[...]).astype(o_ref.dtype)


def _flash(q, k, ke, v, B, S, H, dq, ke_per_head, q_bias_cols, name):
    tq = _pick(S, 512)
    nq = S // tq
    ke_map = (lambda b, h, i: (b, 0, h)) if ke_per_head else (lambda b, h, i: (b, 0, 0))
    return pl.pallas_call(
        functools.partial(_flash_kernel, q_bias_cols=q_bias_cols),
        grid=(B, H, nq),
        in_specs=[pl.BlockSpec((None, tq, dq), lambda b, h, i: (b, i, h)),
                  pl.BlockSpec((None, S, LANES), lambda b, h, i: (b, 0, h)),
                  pl.BlockSpec((None, S, LANES), ke_map),
                  pl.BlockSpec((None, S, LANES), lambda b, h, i: (b, 0, h))],
        out_specs=pl.BlockSpec((None, tq, LANES), lambda b, h, i: (b, i, h)),
        out_shape=jax.ShapeDtypeStruct((B, S, H * LANES), BF16),
        scratch_shapes=[pltpu.VMEM((tq, LANES), F32), pltpu.VMEM((tq, LANES), F32),
                        pltpu.VMEM((tq, LANES), F32), pltpu.VMEM((2, tq, tq), F32)],
        compiler_params=_cparams(("arbitrary", "arbitrary", "arbitrary")),
        name=name,
    )(q, k, ke, v)


def _merge_kernel(x_ref, ya_ref, yb_ref, yc_ref, g1_ref, sh1_ref, sc1_ref, gt1_ref,
                  g2_ref, sh2_ref, sc2_ref, gw_ref, gb_ref, bw_ref, ow_ref,
                  rwh_ref, rwl_ref, rb_ref, tri_ref,
                  x1_ref, h2_ref, mi_ref, mf_ref, cnt_ref, h_ref, mg_ref, carry_ref):
    tm, D = x_ref.shape
    cn = MERGE_COL_CHUNK

    @pl.when(pl.program_id(0) == 0)
    def _():
        carry_ref[...] = jnp.zeros_like(carry_ref)

    h_ref[...] = _norm_mod(x_ref[...], g1_ref[...], sh1_ref[...], sc1_ref[...]).astype(BF16)
    for n in range(D // cn):
        cols = slice(n * cn, (n + 1) * cn)
        merged = None
        for i, y_ref in enumerate((ya_ref, yb_ref, yc_ref)):
            gcols = slice(i * D + n * cn, i * D + (n + 1) * cn)
            gate = jax.nn.sigmoid(jnp.dot(h_ref[...], gw_ref[:, gcols], preferred_element_type=F32)
                                  + gb_ref[:, gcols])
            br = jnp.dot(y_ref[...], bw_ref[i, :, cols], preferred_element_type=F32)
            merged = gate * br if merged is None else merged + gate * br
        mg_ref[:, cols] = merged.astype(BF16)
    for n in range(D // cn):
        cols = slice(n * cn, (n + 1) * cn)
        mix = jnp.dot(mg_ref[...], ow_ref[:, cols], preferred_element_type=F32)
        x1_ref[:, cols] = x_ref[:, cols] + gt1_ref[:, cols] * mix
    h2 = _norm_mod(x1_ref[...], g2_ref[...], sh2_ref[...], sc2_ref[...])
    h2_ref[...] = h2

    hh = h2.astype(BF16)
    hl = (h2 - hh.astype(F32)).astype(BF16)
    nt = lambda a, b: lax.dot_general(a, b, (((1,), (1,)), ((), ())), preferred_element_type=F32)
    lt = nt(rwh_ref[...], hh) + nt(rwh_ref[...], hl) + nt(rwl_ref[...], hh) + rb_ref[...]
    row8 = lax.broadcasted_iota(I32, (EXP_PER_GROUP, tm), 0)
    gl = jnp.where(row8 < N_GROUPS, lt[N_EXPERTS:N_EXPERTS + EXP_PER_GROUP, :], -jnp.inf)
    gmax = jnp.max(gl, axis=0, keepdims=True)
    g_idx = jnp.min(jnp.where(gl == gmax, row8, EXP_PER_GROUP), axis=0, keepdims=True)
    g_w = 1.0 / jnp.sum(jnp.exp(gl - gmax), axis=0, keepdims=True)
    el = lt[(N_GROUPS - 1) * EXP_PER_GROUP:N_EXPERTS, :]
    for g in range(N_GROUPS - 2, -1, -1):
        el = jnp.where(g_idx == g, lt[g * EXP_PER_GROUP:(g + 1) * EXP_PER_GROUP, :], el)
    e1 = jnp.max(el, axis=0, keepdims=True)
    i1 = jnp.min(jnp.where(el == e1, row8, EXP_PER_GROUP), axis=0, keepdims=True)
    el2 = jnp.where(row8 == i1, -jnp.inf, el)
    e2 = jnp.max(el2, axis=0, keepdims=True)
    i2 = jnp.min(jnp.where(el2 == e2, row8, EXP_PER_GROUP), axis=0, keepdims=True)
    r = jnp.exp(e2 - e1)
    w1 = g_w / (1.0 + r)
    w2 = g_w * r / (1.0 + r)
    eid1 = g_idx * EXP_PER_GROUP + i1
    eid2 = g_idx * EXP_PER_GROUP + i2

    rowe = lax.broadcasted_iota(I32, (N_EXPERTS, tm), 0)
    hit1 = rowe == eid1
    hit2 = rowe == eid2
    onehot = jnp.where(hit1 | hit2, 1.0, 0.0)
    before = (jnp.dot(onehot.astype(BF16), tri_ref[...], preferred_element_type=F32)
              + jnp.concatenate([carry_ref[...]] * (tm // LANES), axis=1))
    rank1 = jnp.sum(jnp.where(hit1, before, 0.0), axis=0, keepdims=True)
    rank2 = jnp.sum(jnp.where(hit2, before, 0.0), axis=0, keepdims=True)
    carry_ref[...] = carry_ref[...] + jnp.sum(onehot, axis=1, keepdims=True)
    cnt_ref[...] = carry_ref[...]

    mi_ref[...] = jnp.where(row8 == 0, eid1,
                  jnp.where(row8 == 1, eid2,
                  jnp.where(row8 == 2, rank1.astype(I32),
                  jnp.where(row8 == 3, rank2.astype(I32), 0))))
    mf_ref[...] = jnp.where(row8 == 0, w1, jnp.where(row8 == 1, w2, 0.0))


def _merge(x2, S, ya, yb, yc, g1, sh1, sc1, gt1, g2, sh2, sc2, gw, gb, bw, ow, rwh, rwl, rb):
    T, D = x2.shape
    tm = _pick(S, 512)
    per_b = S // tm
    row = lambda i: (i, 0)
    col = lambda i: (0, i)
    const = lambda i: (0, 0)
    batch = lambda i: (i // per_b, 0, 0)
    idx = jnp.arange(tm)
    tri = (idx[:, None] < idx[None, :]).astype(BF16)
    vecb = pl.BlockSpec((None, 1, D), batch)
    return pl.pallas_call(
        _merge_kernel,
        grid=(T // tm,),
        in_specs=[pl.BlockSpec((tm, D), row),
                  pl.BlockSpec((tm, BRANCH_W), row),
                  pl.BlockSpec((tm, BRANCH_W), row),
                  pl.BlockSpec((tm, BRANCH_W), row),
                  pl.BlockSpec((1, D), const), vecb, vecb, vecb,
                  pl.BlockSpec((1, D), const), vecb, vecb,
                  pl.BlockSpec((D, N_BRANCH * D), const),
                  pl.BlockSpec((1, N_BRANCH * D), const),
                  pl.BlockSpec((N_BRANCH, BRANCH_W, D), lambda i: (0, 0, 0)),
                  pl.BlockSpec((D, D), const),
                  pl.BlockSpec((LANES, D), const),
                  pl.BlockSpec((LANES, D), const),
                  pl.BlockSpec((LANES, 1), const),
                  pl.BlockSpec((tm, tm), const)],
        out_specs=[pl.BlockSpec((tm, D), row), pl.BlockSpec((tm, D), row),
                   pl.BlockSpec((8, tm), col), pl.BlockSpec((8, tm), col),
                   pl.BlockSpec((N_EXPERTS, LANES), const)],
        out_shape=[jax.ShapeDtypeStruct((T, D), F32), jax.ShapeDtypeStruct((T, D), F32),
                   jax.ShapeDtypeStruct((8, T), I32), jax.ShapeDtypeStruct((8, T), F32),
                   jax.ShapeDtypeStruct((N_EXPERTS, LANES), F32)],
        scratch_shapes=[pltpu.VMEM((tm, D), BF16), pltpu.VMEM((tm, D), BF16),
                        pltpu.VMEM((N_EXPERTS, LANES), F32)],
        compiler_params=_cparams(("arbitrary",)),
        name="merge",
    )(x2, ya, yb, yc, g1, sh1, sc1, gt1, g2, sh2, sc2, gw, gb, bw, ow, rwh, rwl, rb, tri)


def _row_copy(src_ref, src_row, dst_ref, dst_row, sem):
    return pltpu.make_async_copy(src_ref.at[pl.ds(src_row, 1), :], dst_ref.at[pl.ds(dst_row, 1), :], sem)


def _dispatch_kernel(dest_ref, h_ref, xs_in_ref, xs_ref, dsm_ref, sem_ref, isem_ref):
    del xs_in_ref
    tm = h_ref.shape[0]
    cp = pltpu.make_async_copy(dest_ref.at[0], dsm_ref, isem_ref)
    cp.start()
    cp.wait()

    def issue(r, carry):
        _row_copy(h_ref, r, xs_ref, dsm_ref[0, 2 * r], sem_ref).start()
        _row_copy(h_ref, r, xs_ref, dsm_ref[0, 2 * r + 1], sem_ref).start()
        return carry

    lax.fori_loop(0, tm, issue, 0, unroll=8)

    def drain(r, carry):
        _row_copy(h_ref, 0, xs_ref, 0, sem_ref).wait()
        _row_copy(h_ref, 0, xs_ref, 0, sem_ref).wait()
        return carry

    lax.fori_loop(0, tm, drain, 0, unroll=8)


def _dispatch(h2, dest, P):
    T, D = h2.shape
    tm = _pick(T, 256)
    nt = T // tm
    xs0 = jnp.zeros((P, D), F32)
    return pl.pallas_call(
        _dispatch_kernel,
        grid=(nt,),
        in_specs=[pl.BlockSpec((1, 1, 2 * tm), lambda i: (i, 0, 0)),
                  pl.BlockSpec((tm, D), lambda i: (i, 0)),
                  pl.BlockSpec(memory_space=pl.ANY)],
        out_specs=pl.BlockSpec(memory_space=pl.ANY),
        out_shape=jax.ShapeDtypeStruct((P, D), F32),
        scratch_shapes=[pltpu.SMEM((1, 2 * tm), I32), pltpu.SemaphoreType.DMA, pltpu.SemaphoreType.DMA],
        input_output_aliases={2: 0},
        compiler_params=_cparams(("arbitrary",)),
        name="moe_dispatch",
    )(dest.reshape(nt, 1, 2 * tm), h2, xs0)


def _expert_kernel(blk_e_ref, nvalid_ref, xs_ref, w1_ref, w3_ref, w2_ref, ys_ref):
    del blk_e_ref
    valid = pl.program_id(0) < nvalid_ref[0]

    @pl.when(valid)
    def _():
        xb = xs_ref[...].astype(BF16)
        a = jnp.dot(xb, w1_ref[...], preferred_element_type=F32)
        b = jnp.dot(xb, w3_ref[...], preferred_element_type=F32)
        hid = (a * jax.nn.sigmoid(a) * b).astype(BF16)
        ys_ref[...] = jnp.dot(hid, w2_ref[...], preferred_element_type=F32)

    @pl.when(jnp.logical_not(valid))
    def _():
        ys_ref[...] = jnp.zeros_like(ys_ref)


def _experts(xs, blk_e, nvalid, w1, w3, w2, tb):
    P, D = xs.shape
    n_blocks = P // tb
    rows = lambda i, be, nv: (jnp.minimum(i, nv[0] - 1), 0)
    grid_spec = pltpu.PrefetchScalarGridSpec(
        num_scalar_prefetch=2,
        grid=(n_blocks,),
        in_specs=[pl.BlockSpec((tb, D), rows),
                  pl.BlockSpec((None, D, D_EXPERT), lambda i, be, nv: (be[i], 0, 0)),
                  pl.BlockSpec((None, D, D_EXPERT), lambda i, be, nv: (be[i], 0, 0)),
                  pl.BlockSpec((None, D_EXPERT, D), lambda i, be, nv: (be[i], 0, 0))],
        out_specs=pl.BlockSpec((tb, D), lambda i, be, nv: (i, 0)),
    )
    return pl.pallas_call(
        _expert_kernel,
        grid_spec=grid_spec,
        out_shape=jax.ShapeDtypeStruct((P, D), F32),
        compiler_params=_cparams(("arbitrary",)),
        name="moe_experts",
    )(blk_e, nvalid, xs, w1, w3, w2)


def _combine_kernel(dest_ref, x_ref, mf_ref, gt_ref, fg_ref, ys_ref, o_ref,
                    dsm_ref, ybuf_ref, sem_ref, isem_ref, *, final):
    tm = x_ref.shape[0]
    cp = pltpu.make_async_copy(dest_ref.at[0], dsm_ref, isem_ref)
    cp.start()
    cp.wait()

    def issue(r, carry):
        _row_copy(ys_ref, dsm_ref[0, 2 * r], ybuf_ref.at[0], r, sem_ref).start()
        _row_copy(ys_ref, dsm_ref[0, 2 * r + 1], ybuf_ref.at[1], r, sem_ref).start()
        return carry

    lax.fori_loop(0, tm, issue, 0, unroll=8)

    def drain(r, carry):
        _row_copy(ys_ref, 0, ybuf_ref.at[0], 0, sem_ref).wait()
        _row_copy(ys_ref, 0, ybuf_ref.at[1], 0, sem_ref).wait()
        return carry

    lax.fori_loop(0, tm, drain, 0, unroll=8)

    mf = mf_ref[...]
    ffn = ybuf_ref[0] * mf[:, 0:1] + ybuf_ref[1] * mf[:, 1:2]
    out = x_ref[...] + gt_ref[...] * ffn
    if final:
        out = _rms(out, fg_ref[...])
    o_ref[...] = out


def _combine(x1, S, mf, gt2, final_g, ys, dest, final):
    T, D = x1.shape
    tm = _pick(S, 256)
    per_b = S // tm
    nt = T // tm
    return pl.pallas_call(
        functools.partial(_combine_kernel, final=final),
        grid=(nt,),
        in_specs=[pl.BlockSpec((1, 1, 2 * tm), lambda i: (i, 0, 0)),
                  pl.BlockSpec((tm, D), lambda i: (i, 0)),
                  pl.BlockSpec((tm, 2), lambda i: (i, 0)),
                  pl.BlockSpec((None, 1, D), lambda i: (i // per_b, 0, 0)),
                  pl.BlockSpec((1, D), lambda i: (0, 0)),
                  pl.BlockSpec(memory_space=pl.ANY)],
        out_specs=pl.BlockSpec((tm, D), lambda i: (i, 0)),
        out_shape=jax.ShapeDtypeStruct((T, D), F32),
        scratch_shapes=[pltpu.SMEM((1, 2 * tm), I32), pltpu.VMEM((2, tm, D), F32),
                        pltpu.SemaphoreType.DMA, pltpu.SemaphoreType.DMA],
        compiler_params=_cparams(("arbitrary",)),
        name="moe_combine",
    )(dest.reshape(nt, 1, 2 * tm), x1, mf, gt2, final_g, ys)


def _prep_w_in(w):
    offs = [0]
    for s in (256, 256, 512, 512, 512, 512, 512, FOX_HEADS, MLA_Q_RANK, MLA_KV_RANK, MLA_ROPE):
        offs.append(offs[-1] + s)
    rq, rk, rv, rg, fq, fk, fv, ff, mq, mkv, mkr = [w[:, offs[i]:offs[i + 1]] for i in range(11)]
    pad = jnp.zeros((w.shape[0], LANES - MLA_ROPE - FOX_HEADS), w.dtype)
    return jnp.concatenate([rq, rk, rv, rg, fq, fk, fv, mq, mkv, mkr, ff, pad], axis=1).astype(BF16)


def _prep_wq_up(w):
    r = w.reshape(MLA_Q_RANK, MLA_HEADS, MLA_NOPE + MLA_ROPE)
    r = jnp.pad(r, ((0, 0), (0, 0), (0, MLA_DQ - MLA_NOPE - MLA_ROPE)))
    return r.reshape(MLA_Q_RANK, MLA_HEADS * MLA_DQ).astype(BF16)


def _prep_router(w_grp, b_grp, w_exp, b_exp):
    D = w_grp.shape[0]
    pad = LANES - N_EXPERTS - N_GROUPS
    rwt = jnp.concatenate([w_exp, w_grp, jnp.zeros((D, pad), F32)], axis=1).astype(F32).T
    rwh = rwt.astype(BF16)
    rwl = (rwt - rwh.astype(F32)).astype(BF16)
    rb = jnp.concatenate([b_exp, b_grp, jnp.zeros((pad,), F32)]).astype(F32).reshape(LANES, 1)
    return rwh, rwl, rb


def kernel(x, c, positions, ada_w, ada_b, norm1_g, norm2_g, w_in, fox_fb, mla_q_norm_g, mla_wq_up, mla_kv_norm_g, mla_wkv_up, gate_w, gate_b, branch_w, out_w, router_grp_w, router_grp_b, router_exp_w, router_exp_b, exp_w1, exp_w3, exp_w2, final_g):
    B, S, D = x.shape
    L = ada_w.shape[0]
    T = B * S
    A = 2 * T
    tb = _pick(A, 256)
    n_blocks = A // tb + N_EXPERTS
    P = n_blocks * tb

    mod = _adaln(c, ada_w, ada_b)
    cos_t, sin_t = _rope_tables(positions)
    x2 = x.reshape(T, D)
    final_g2 = final_g.reshape(1, D)

    for l in range(L):
        sh1, sc1, gt1, sh2, sc2, gt2 = [mod[l, :, i * D:(i + 1) * D].reshape(B, 1, D) for i in range(6)]
        g1 = norm1_g[l].reshape(1, D)
        g2 = norm2_g[l].reshape(1, D)
        (rq, rk, rv, rg, fq, fk, fv, mq, mk, kpe, mv, ffp) = _proj(
            x2, S, g1, sh1, sc1, _prep_w_in(w_in[l]), cos_t, sin_t,
            mla_q_norm_g[l].reshape(1, -1), _prep_wq_up(mla_wq_up[l]),
            mla_kv_norm_g[l].reshape(1, -1), mla_wkv_up[l].astype(BF16))

        ya = _retention(rq, rk, rv, rg, B, S).reshape(T, -1)
        kb = _fox_gate(ffp, fox_fb[l], S)
        r3 = lambda a: a.reshape(B, S, -1)
        yb = _flash(r3(fq), r3(fk), r3(kb), r3(fv), B, S, FOX_HEADS, FOX_DH, True, 3, "flash_fox").reshape(T, -1)
        yc = _flash(r3(mq), r3(mk), r3(kpe), r3(mv), B, S, MLA_HEADS, MLA_DQ, False, 0, "flash_mla").reshape(T, -1)

        rwh, rwl, rb = _prep_router(router_grp_w[l], router_grp_b[l], router_exp_w[l], router_exp_b[l])
        x1, h2, mi, mf, cnt = _merge(
            x2, S, ya, yb, yc, g1, sh1, sc1, gt1, g2, sh2, sc2,
            gate_w[l].astype(BF16), gate_b[l].reshape(1, -1), branch_w[l].astype(BF16),
            out_w[l].astype(BF16), rwh, rwl, rb)

        counts = cnt[:, 0].astype(I32)
        pcounts = (counts + tb - 1) // tb * tb
        pends = jnp.cumsum(pcounts)
        pstarts = pends - pcounts
        dest = (pstarts[mi[0:2]] + mi[2:4]).T.reshape(A)
        blk_pos = jnp.arange(n_blocks, dtype=I32) * tb
        blk_e = jnp.minimum(jnp.sum((pends[None, :] <= blk_pos[:, None]).astype(I32), axis=1), N_EXPERTS - 1)
        nvalid = (pends[-1:] // tb).astype(I32)

        xs = _dispatch(h2, dest, P)
        ys = _experts(xs, blk_e, nvalid, exp_w1[l].astype(BF16), exp_w3[l].astype(BF16),
                      exp_w2[l].astype(BF16), tb)
        x2 = _combine(x1, S, mf[0:2].T, gt2, final_g2, ys, dest, final=(l == L - 1))

    return x2.reshape(B, S, D)
```

```python
import functools
import math

import jax
import jax.numpy as jnp
from jax import lax
from jax.experimental import pallas as pl
from jax.experimental.pallas import tpu as pltpu

F32 = jnp.float32
BF16 = jnp.bfloat16
I32 = jnp.int32
HIGHEST = lax.Precision.HIGHEST

EPS = 1e-6
ROPE_THETA = 10000.0
RET_HEADS = 4
RET_DK = 64
RET_DV = 128
RET_CHUNK = 128
FOX_HEADS = 4
FOX_DH = 128
MLA_HEADS = 4
MLA_Q_RANK = 256
MLA_KV_RANK = 128
MLA_NOPE = 128
MLA_ROPE = 64
MLA_V = 128
MLA_DQ = 256
N_BRANCH = 3
BRANCH_W = 512
N_GROUPS = 4
EXP_PER_GROUP = 8
N_EXPERTS = N_GROUPS * EXP_PER_GROUP
D_EXPERT = 512

LANES = 128
V7X_VMEM_LIMIT = 56 * 1024 * 1024

C_RQ, C_RK, C_RV, C_RG = 0, 256, 512, 1024
C_FQ, C_FK, C_FV = 1536, 2048, 2560
C_MQ, C_MKV, C_TAIL = 3072, 3328, 3456
D_IN_PAD = 3584
FF_LANE = MLA_ROPE

NEG_BIG = -1e30
MERGE_COL_CHUNK = 256
LOG2E = math.log2(math.e)


def _cparams(sem):
    return pltpu.CompilerParams(dimension_semantics=sem, vmem_limit_bytes=V7X_VMEM_LIMIT)


def _pick(n, pref):
    t = min(n, pref)
    assert n % t == 0, (n, t)
    return t


def _adaln_kernel(c_ref, w_ref, b_ref, o_ref):
    c = c_ref[...]
    ca = c * jax.nn.sigmoid(c)
    o_ref[...] = jnp.dot(ca, w_ref[...], preferred_element_type=F32, precision=HIGHEST) + b_ref[...]


def _adaln(c, ada_w, ada_b):
    L, D, N = ada_w.shape
    B = c.shape[0]
    tn = _pick(N, 1536)
    return pl.pallas_call(
        _adaln_kernel,
        grid=(L, N // tn),
        in_specs=[pl.BlockSpec((B, D), lambda l, j: (0, 0)),
                  pl.BlockSpec((None, D, tn), lambda l, j: (l, 0, j)),
                  pl.BlockSpec((None, 1, tn), lambda l, j: (l, 0, j))],
        out_specs=pl.BlockSpec((None, B, tn), lambda l, j: (l, 0, j)),
        out_shape=jax.ShapeDtypeStruct((L, B, N), F32),
        compiler_params=_cparams(("arbitrary", "arbitrary")),
        name="adaln",
    )(c, ada_w, ada_b.reshape(L, 1, N))


def _rope_table_kernel(pos_ref, inv_ref, sign_ref, cos_ref, sin_ref):
    ang = pos_ref[...].astype(F32) * inv_ref[...]
    cos_ref[...] = jnp.cos(ang)
    sin_ref[...] = jnp.sin(ang) * sign_ref[...]


def _rope_tables(positions):
    T = positions.size
    tm = _pick(T, 2048)
    half = MLA_ROPE // 2
    inv = ROPE_THETA ** (-jnp.arange(0, MLA_ROPE, 2, dtype=F32) / MLA_ROPE)
    inv_t = jnp.tile(inv, LANES // half).reshape(1, LANES)
    sign = jnp.where((jnp.arange(LANES) % MLA_ROPE) < half, -1.0, 1.0).astype(F32).reshape(1, LANES)
    return pl.pallas_call(
        _rope_table_kernel,
        grid=(T // tm,),
        in_specs=[pl.BlockSpec((tm, 1), lambda i: (i, 0)),
                  pl.BlockSpec((1, LANES), lambda i: (0, 0)),
                  pl.BlockSpec((1, LANES), lambda i: (0, 0))],
        out_specs=[pl.BlockSpec((tm, LANES), lambda i: (i, 0))] * 2,
        out_shape=[jax.ShapeDtypeStruct((T, LANES), F32)] * 2,
        compiler_params=_cparams(("arbitrary",)),
        name="rope_tables",
    )(positions.reshape(T, 1), inv_t, sign)


def _rope_slab(x, cos_t, sin_t, lane):
    nxt = pltpu.roll(x, LANES - 32, axis=1)
    prv = pltpu.roll(x, 32, axis=1)
    swapped = jnp.where((lane & 32) == 0, nxt, prv)
    return x * cos_t + swapped * sin_t


def _norm_mod(x, g, shift, scale):
    y = x * lax.rsqrt(jnp.mean(x * x, axis=-1, keepdims=True) + EPS)
    return (y * g) * (1.0 + scale) + shift


def _rms(x, g):
    return x * lax.rsqrt(jnp.mean(x * x, axis=-1, keepdims=True) + EPS) * g


def _proj_kernel(x_ref, g_ref, sh_ref, sc_ref, w_ref, cos_ref, sin_ref,
                 gq_ref, wq_ref, gkv_ref, wkv_ref,
                 rq_ref, rk_ref, rv_ref, rg_ref, fq_ref, fk_ref, fv_ref,
                 mq_ref, mk_ref, kpe_ref, mv_ref, ff_ref):
    h = _norm_mod(x_ref[...], g_ref[...], sh_ref[...], sc_ref[...]).astype(BF16)
    cos_t = cos_ref[...]
    sin_t = sin_ref[...]
    lane = lax.broadcasted_iota(I32, cos_t.shape, 1)

    def proj(c0, width):
        return jnp.dot(h, w_ref[:, c0:c0 + width], preferred_element_type=F32)

    rq = proj(C_RQ, 256)
    rk = proj(C_RK, 256)
    for s in range(2):
        sl = slice(s * LANES, (s + 1) * LANES)
        rq_ref[:, sl] = _rope_slab(rq[:, sl], cos_t, sin_t, lane).astype(BF16)
        rk_ref[:, sl] = (_rope_slab(rk[:, sl], cos_t, sin_t, lane) * (RET_DK ** -0.5)).astype(BF16)
    rv_ref[...] = proj(C_RV, 512).astype(BF16)
    rg_ref[...] = proj(C_RG, 512).astype(BF16)
    fq_ref[...] = (proj(C_FQ, 512) * (FOX_DH ** -0.5 * LOG2E)).astype(BF16)
    fk_ref[...] = proj(C_FK, 512).astype(BF16)
    fv_ref[...] = proj(C_FV, 512).astype(BF16)

    tail = proj(C_TAIL, LANES)
    ff_ref[...] = tail
    kpe_ref[...] = jnp.where(lane < MLA_ROPE, _rope_slab(tail, cos_t, sin_t, lane), 0.0).astype(BF16)

    qn = _rms(proj(C_MQ, MLA_Q_RANK), gq_ref[...]).astype(BF16)
    qh = jnp.dot(qn, wq_ref[...], preferred_element_type=F32)
    q_scale = (MLA_NOPE + MLA_ROPE) ** -0.5 * LOG2E
    for hd in range(MLA_HEADS):
        c0 = hd * MLA_DQ
        mq_ref[:, c0:c0 + LANES] = (qh[:, c0:c0 + LANES] * q_scale).astype(BF16)
        pe = _rope_slab(qh[:, c0 + LANES:c0 + 2 * LANES], cos_t, sin_t, lane)
        mq_ref[:, c0 + LANES:c0 + 2 * LANES] = jnp.where(lane < MLA_ROPE, pe * q_scale, 0.0).astype(BF16)

    kvn = _rms(proj(C_MKV, MLA_KV_RANK), gkv_ref[...]).astype(BF16)
    kvh = jnp.dot(kvn, wkv_ref[...], preferred_element_type=F32)
    for hd in range(MLA_HEADS):
        c0 = hd * (MLA_NOPE + MLA_V)
        mk_ref[:, hd * MLA_NOPE:(hd + 1) * MLA_NOPE] = kvh[:, c0:c0 + MLA_NOPE].astype(BF16)
        mv_ref[:, hd * MLA_V:(hd + 1) * MLA_V] = kvh[:, c0 + MLA_NOPE:c0 + MLA_NOPE + MLA_V].astype(BF16)


def _proj(x2, S, g, sh, sc, w_all, cos_t, sin_t, gq, wq, gkv, wkv):
    T, D = x2.shape
    tm = _pick(S, 512)
    per_b = S // tm
    row = lambda i: (i, 0)
    const = lambda i: (0, 0)
    batch = lambda i: (i // per_b, 0, 0)
    widths = [256, 256, 512, 512, 512, 512, 512, MLA_HEADS * MLA_DQ, MLA_HEADS * MLA_NOPE, LANES,
              MLA_HEADS * MLA_V]
    out_shape = [jax.ShapeDtypeStruct((T, w), BF16) for w in widths]
    out_shape.append(jax.ShapeDtypeStruct((T, LANES), F32))
    out_specs = [pl.BlockSpec((tm, w), row) for w in widths] + [pl.BlockSpec((tm, LANES), row)]
    return pl.pallas_call(
        _proj_kernel,
        grid=(T // tm,),
        in_specs=[pl.BlockSpec((tm, D), row),
                  pl.BlockSpec((1, D), const),
                  pl.BlockSpec((None, 1, D), batch),
                  pl.BlockSpec((None, 1, D), batch),
                  pl.BlockSpec((D, D_IN_PAD), const),
                  pl.BlockSpec((tm, LANES), row),
                  pl.BlockSpec((tm, LANES), row),
                  pl.BlockSpec((1, MLA_Q_RANK), const),
                  pl.BlockSpec((MLA_Q_RANK, MLA_HEADS * MLA_DQ), const),
                  pl.BlockSpec((1, MLA_KV_RANK), const),
                  pl.BlockSpec((MLA_KV_RANK, MLA_HEADS * (MLA_NOPE + MLA_V)), const)],
        out_specs=out_specs,
        out_shape=out_shape,
        compiler_params=_cparams(("arbitrary",)),
        name="proj",
    )(x2, g, sh, sc, w_all, cos_t, sin_t, gq, wq, gkv, wkv)


def _split3(x):
    a = x.astype(BF16)
    r = x - a.astype(F32)
    b = r.astype(BF16)
    c = (r - b.astype(F32)).astype(BF16)
    return a, b, c


def _fox_gate_kernel(ff_ref, fb_ref, tri_ref, kb_ref, carry_ref, *, tiles_per_seq):
    @pl.when(pl.program_id(0) % tiles_per_seq == 0)
    def _():
        carry_ref[...] = jnp.zeros_like(carry_ref)

    tm = ff_ref.shape[0]
    lane = lax.broadcasted_iota(I32, (tm, LANES), 1)
    z = ff_ref[...] + fb_ref[...]
    ls = -(jnp.maximum(-z, 0.0) + jnp.log1p(jnp.exp(-jnp.abs(z))))
    ls = jnp.where((lane >= FF_LANE) & (lane < FF_LANE + FOX_HEADS), ls, 0.0)
    tri = tri_ref[...]
    f = carry_ref[...]
    for part in _split3(ls):
        f = f + jnp.dot(tri, part, preferred_element_type=F32)
    carry_ref[...] = f[tm - 1:tm, :]
    f2 = f * LOG2E
    for hd in range(FOX_HEADS):
        src = FF_LANE + hd
        g = jnp.where(lane == 0, pltpu.roll(f2, (LANES - src) % LANES, axis=1),
            jnp.where(lane == 1, pltpu.roll(f2, (LANES + 1 - src) % LANES, axis=1),
            jnp.where(lane == 2, pltpu.roll(f2, (LANES + 2 - src) % LANES, axis=1), 0.0)))
        hi, mid, lo = _split3(g)
        kb_ref[:, hd * LANES:(hd + 1) * LANES] = jnp.where(lane == 0, hi, jnp.where(lane == 1, mid, lo))


def _fox_gate(ffp, fb, S):
    T = ffp.shape[0]
    tm = _pick(S, 512)
    idx = jnp.arange(tm)
    tri = (idx[None, :] <= idx[:, None]).astype(BF16)
    fbv = jnp.zeros((1, LANES), F32).at[0, FF_LANE:FF_LANE + FOX_HEADS].set(fb)
    return pl.pallas_call(
        functools.partial(_fox_gate_kernel, tiles_per_seq=S // tm),
        grid=(T // tm,),
        in_specs=[pl.BlockSpec((tm, LANES), lambda i: (i, 0)),
                  pl.BlockSpec((1, LANES), lambda i: (0, 0)),
                  pl.BlockSpec((tm, tm), lambda i: (0, 0))],
        out_specs=pl.BlockSpec((tm, FOX_HEADS * LANES), lambda i: (i, 0)),
        out_shape=jax.ShapeDtypeStruct((T, FOX_HEADS * LANES), BF16),
        scratch_shapes=[pltpu.VMEM((1, LANES), F32)],
        compiler_params=_cparams(("arbitrary",)),
        name="fox_gate",
    )(ffp, fbv, tri)


def _retention_kernel(dchunk_ref, q_ref, k_ref, v_ref, g_ref, dmask_ref, din_ref, dout_ref,
                      o_ref, state_ref, *, n_chunks):
    @pl.when(pl.program_id(1) == 0)
    def _():
        state_ref[...] = jnp.zeros_like(state_ref)

    C = RET_CHUNK
    lane = lax.broadcasted_iota(I32, (C, LANES), 1)
    for ci in range(n_chunks):
        rows = slice(ci * C, (ci + 1) * C)
        for hd in range(RET_HEADS):
            slab = slice((hd // 2) * LANES, (hd // 2 + 1) * LANES)
            mine = (lane < RET_DK) if hd % 2 == 0 else (lane >= RET_DK)
            q = jnp.where(mine, q_ref[rows, slab], 0)
            k = jnp.where(mine, k_ref[rows, slab], 0)
            vcols = slice(hd * RET_DV, (hd + 1) * RET_DV)
            v = v_ref[rows, vcols]
            state = state_ref[hd]
            scores = lax.dot_general(q, k, (((1,), (1,)), ((), ())),
                                     preferred_element_type=F32) * dmask_ref[hd]
            inner = jnp.dot(scores.astype(BF16), v, preferred_element_type=F32)
            cross = jnp.dot(q, state.astype(BF16), preferred_element_type=F32) * din_ref[hd]
            o = inner + cross
            vd = (v.astype(F32) * dout_ref[hd]).astype(BF16)
            kv = lax.dot_general(k, vd, (((0,), (0,)), ((), ())), preferred_element_type=F32)
            state_ref[hd] = state * dchunk_ref[hd] + kv
            mu = jnp.mean(o, axis=-1, keepdims=True)
            d = o - mu
            var = jnp.mean(d * d, axis=-1, keepdims=True)
            on = d * lax.rsqrt(var + EPS)
            g = g_ref[rows, vcols].astype(F32)
            o_ref[rows, vcols] = (g * jax.nn.sigmoid(g) * on).astype(BF16)


def _retention(rq, rk, rv, rg, B, S):
    H, C = RET_HEADS, RET_CHUNK
    tr = _pick(S, 4 * C)
    n_chunks = tr // C
    log_gamma = jnp.log1p(-jnp.exp2(-5.0 - jnp.arange(H, dtype=F32)))
    idx = jnp.arange(C, dtype=F32)
    rel = idx[:, None] - idx[None, :]
    dmask = jnp.where(rel >= 0, jnp.exp(log_gamma[:, None, None] * jnp.maximum(rel, 0.0)), 0.0)
    decay_in = jnp.exp(log_gamma[:, None] * (idx + 1.0))
    decay_out = jnp.exp(log_gamma[:, None] * (C - 1.0 - idx))
    decay_chunk = jnp.exp(log_gamma * C)
    din = jnp.broadcast_to(decay_in[:, :, None], (H, C, RET_DV))
    dout = jnp.broadcast_to(decay_out[:, :, None], (H, C, RET_DV))
    tok = lambda b, i: (b, i, 0)
    const3 = lambda b, i: (0, 0, 0)
    return pl.pallas_call(
        functools.partial(_retention_kernel, n_chunks=n_chunks),
        grid=(B, S // tr),
        in_specs=[pl.BlockSpec(memory_space=pltpu.SMEM),
                  pl.BlockSpec((None, tr, H * RET_DK), tok),
                  pl.BlockSpec((None, tr, H * RET_DK), tok),
                  pl.BlockSpec((None, tr, H * RET_DV), tok),
                  pl.BlockSpec((None, tr, H * RET_DV), tok),
                  pl.BlockSpec((H, C, C), const3),
                  pl.BlockSpec((H, C, RET_DV), const3),
                  pl.BlockSpec((H, C, RET_DV), const3)],
        out_specs=pl.BlockSpec((None, tr, H * RET_DV), tok),
        out_shape=jax.ShapeDtypeStruct((B, S, H * RET_DV), BF16),
        scratch_shapes=[pltpu.VMEM((H, LANES, RET_DV), F32)],
        compiler_params=_cparams(("arbitrary", "arbitrary")),
        name="retention",
    )(decay_chunk, rq.reshape(B, S, -1), rk.reshape(B, S, -1), rv.reshape(B, S, -1),
      rg.reshape(B, S, -1), dmask, din, dout)


def _flash_kernel(q_ref, k_ref, ke_ref, v_ref, o_ref, m_ref, l_ref, acc_ref, s_ref, *, q_bias_cols):
    qi = pl.program_id(2)
    tq = q_ref.shape[0]
    tk = tq
    reps = tk // LANES
    m_ref[...] = jnp.full(m_ref.shape, NEG_BIG, F32)
    l_ref[...] = jnp.zeros_like(l_ref)
    acc_ref[...] = jnp.zeros_like(acc_ref)
    q = q_ref[...]
    if q_bias_cols:
        lane = lax.broadcasted_iota(I32, (tq, LANES), 1)
        q = jnp.concatenate([q, jnp.where(lane < q_bias_cols, -1.0, 0.0).astype(BF16)], axis=1)

    def scores(j):
        start = pl.multiple_of(j * tk, tk)
        kj = jnp.concatenate([k_ref[pl.ds(start, tk), :], ke_ref[pl.ds(start, tk), :]], axis=1)
        return lax.dot_general(q, kj, (((1,), (1,)), ((), ())), preferred_element_type=F32)

    def update(s, j):
        start = pl.multiple_of(j * tk, tk)
        vj = v_ref[pl.ds(start, tk), :]
        m_prev = m_ref[...]
        m_new = jnp.maximum(m_prev, jnp.max(s, axis=-1, keepdims=True))
        alpha = jnp.exp2(m_prev - m_new)
        p = jnp.exp2(s - jnp.concatenate([m_new] * reps, axis=1))
        l_ref[...] = alpha * l_ref[...] + jnp.sum(p, axis=-1, keepdims=True)
        acc_ref[...] = alpha * acc_ref[...] + jnp.dot(p.astype(BF16), vj, preferred_element_type=F32)
        m_ref[...] = m_new

    def causal(s):
        r = lax.broadcasted_iota(I32, s.shape, 0)
        c = lax.broadcasted_iota(I32, s.shape, 1)
        return jnp.where(c <= r, s, NEG_BIG)

    s_ref[0] = scores(0)

    def pair(p, carry):
        j = 2 * p
        s = s_ref[0]
        s_ref[1] = scores(j + 1)
        update(s, j)
        s = s_ref[1]
        s_ref[0] = scores(j + 2)
        update(s, j + 1)
        return carry

    lax.fori_loop(0, qi // 2, pair, 0)

    @pl.when(qi % 2 == 1)
    def _():
        s = s_ref[0]
        s_ref[1] = scores(qi)
        update(s, qi - 1)
        update(causal(s_ref[1]), qi)

    @pl.when(qi % 2 == 0)
    def _():
        update(causal(s_ref[0]), qi)

    o_ref[...] = (acc_ref[...] / l_ref[...]).astype(o_ref.dtype)


def _flash(q, k, ke, v, B, S, H, dq, ke_per_head, q_bias_cols, name):
    tq = _pick(S, 512)
    nq = S // tq
    ke_map = (lambda b, h, i: (b, 0, h)) if ke_per_head else (lambda b, h, i: (b, 0, 0))
    return pl.pallas_call(
        functools.partial(_flash_kernel, q_bias_cols=q_bias_cols),
        grid=(B, H, nq),
        in_specs=[pl.BlockSpec((None, tq, dq), lambda b, h, i: (b, i, h)),
                  pl.BlockSpec((None, S, LANES), lambda b, h, i: (b, 0, h)),
                  pl.BlockSpec((None, S, LANES), ke_map),
                  pl.BlockSpec((None, S, LANES), lambda b, h, i: (b, 0, h))],
        out_specs=pl.BlockSpec((None, tq, LANES), lambda b, h, i: (b, i, h)),
        out_shape=jax.ShapeDtypeStruct((B, S, H * LANES), BF16),
        scratch_shapes=[pltpu.VMEM((tq, LANES), F32), pltpu.VMEM((tq, LANES), F32),
                        pltpu.VMEM((tq, LANES), F32), pltpu.VMEM((2, tq, tq), F32)],
        compiler_params=_cparams(("arbitrary", "arbitrary", "arbitrary")),
        name=name,
    )(q, k, ke, v)


def _merge_kernel(x_ref, ya_ref, yb_ref, yc_ref, g1_ref, sh1_ref, sc1_ref, gt1_ref,
                  g2_ref, sh2_ref, sc2_ref, gw_ref, gb_ref, bw_ref, ow_ref,
                  rwh_ref, rwl_ref, rb_ref, tri_ref,
                  x1_ref, h2_ref, mi_ref, mf_ref, cnt_ref, h_ref, mg_ref, carry_ref):
    tm, D = x_ref.shape
    cn = MERGE_COL_CHUNK

    @pl.when(pl.program_id(0) == 0)
    def _():
        carry_ref[...] = jnp.zeros_like(carry_ref)

    h_ref[...] = _norm_mod(x_ref[...], g1_ref[...], sh1_ref[...], sc1_ref[...]).astype(BF16)
    for n in range(D // cn):
        cols = slice(n * cn, (n + 1) * cn)
        merged = None
        for i, y_ref in enumerate((ya_ref, yb_ref, yc_ref)):
            gcols = slice(i * D + n * cn, i * D + (n + 1) * cn)
            gate = jax.nn.sigmoid(jnp.dot(h_ref[...], gw_ref[:, gcols], preferred_element_type=F32)
                                  + gb_ref[:, gcols])
            br = jnp.dot(y_ref[...], bw_ref[i, :, cols], preferred_element_type=F32)
            merged = gate * br if merged is None else merged + gate * br
        mg_ref[:, cols] = merged.astype(BF16)
    for n in range(D // cn):
        cols = slice(n * cn, (n + 1) * cn)
        mix = jnp.dot(mg_ref[...], ow_ref[:, cols], preferred_element_type=F32)
        x1_ref[:, cols] = x_ref[:, cols] + gt1_ref[:, cols] * mix
    h2 = _norm_mod(x1_ref[...], g2_ref[...], sh2_ref[...], sc2_ref[...])
    h2_ref[...] = h2

    hh = h2.astype(BF16)
    hl = (h2 - hh.astype(F32)).astype(BF16)
    nt = lambda a, b: lax.dot_general(a, b, (((1,), (1,)), ((), ())), preferred_element_type=F32)
    lt = nt(rwh_ref[...], hh) + nt(rwh_ref[...], hl) + nt(rwl_ref[...], hh) + rb_ref[...]
    row8 = lax.broadcasted_iota(I32, (EXP_PER_GROUP, tm), 0)
    gl = jnp.where(row8 < N_GROUPS, lt[N_EXPERTS:N_EXPERTS + EXP_PER_GROUP, :], -jnp.inf)
    gmax = jnp.max(gl, axis=0, keepdims=True)
    g_idx = jnp.min(jnp.where(gl == gmax, row8, EXP_PER_GROUP), axis=0, keepdims=True)
    g_w = 1.0 / jnp.sum(jnp.exp(gl - gmax), axis=0, keepdims=True)
    el = lt[(N_GROUPS - 1) * EXP_PER_GROUP:N_EXPERTS, :]
    for g in range(N_GROUPS - 2, -1, -1):
        el = jnp.where(g_idx == g, lt[g * EXP_PER_GROUP:(g + 1) * EXP_PER_GROUP, :], el)
    e1 = jnp.max(el, axis=0, keepdims=True)
    i1 = jnp.min(jnp.where(el == e1, row8, EXP_PER_GROUP), axis=0, keepdims=True)
    el2 = jnp.where(row8 == i1, -jnp.inf, el)
    e2 = jnp.max(el2, axis=0, keepdims=True)
    i2 = jnp.min(jnp.where(el2 == e2, row8, EXP_PER_GROUP), axis=0, keepdims=True)
    r = jnp.exp(e2 - e1)
    w1 = g_w / (1.0 + r)
    w2 = g_w * r / (1.0 + r)
    eid1 = g_idx * EXP_PER_GROUP + i1
    eid2 = g_idx * EXP_PER_GROUP + i2

    rowe = lax.broadcasted_iota(I32, (N_EXPERTS, tm), 0)
    hit1 = rowe == eid1
    hit2 = rowe == eid2
    onehot = jnp.where(hit1 | hit2, 1.0, 0.0)
    before = (jnp.dot(onehot.astype(BF16), tri_ref[...], preferred_element_type=F32)
              + jnp.concatenate([carry_ref[...]] * (tm // LANES), axis=1))
    rank1 = jnp.sum(jnp.where(hit1, before, 0.0), axis=0, keepdims=True)
    rank2 = jnp.sum(jnp.where(hit2, before, 0.0), axis=0, keepdims=True)
    carry_ref[...] = carry_ref[...] + jnp.sum(onehot, axis=1, keepdims=True)
    cnt_ref[...] = carry_ref[...]

    mi_ref[...] = jnp.where(row8 == 0, eid1,
                  jnp.where(row8 == 1, eid2,
                  jnp.where(row8 == 2, rank1.astype(I32),
                  jnp.where(row8 == 3, rank2.astype(I32), 0))))
    mf_ref[...] = jnp.where(row8 == 0, w1, jnp.where(row8 == 1, w2, 0.0))


def _merge(x2, S, ya, yb, yc, g1, sh1, sc1, gt1, g2, sh2, sc2, gw, gb, bw, ow, rwh, rwl, rb):
    T, D = x2.shape
    tm = _pick(S, 512)
    per_b = S // tm
    row = lambda i: (i, 0)
    col = lambda i: (0, i)
    const = lambda i: (0, 0)
    batch = lambda i: (i // per_b, 0, 0)
    idx = jnp.arange(tm)
    tri = (idx[:, None] < idx[None, :]).astype(BF16)
    vecb = pl.BlockSpec((None, 1, D), batch)
    return pl.pallas_call(
        _merge_kernel,
        grid=(T // tm,),
        in_specs=[pl.BlockSpec((tm, D), row),
                  pl.BlockSpec((tm, BRANCH_W), row),
                  pl.BlockSpec((tm, BRANCH_W), row),
                  pl.BlockSpec((tm, BRANCH_W), row),
                  pl.BlockSpec((1, D), const), vecb, vecb, vecb,
                  pl.BlockSpec((1, D), const), vecb, vecb,
                  pl.BlockSpec((D, N_BRANCH * D), const),
                  pl.BlockSpec((1, N_BRANCH * D), const),
                  pl.BlockSpec((N_BRANCH, BRANCH_W, D), lambda i: (0, 0, 0)),
                  pl.BlockSpec((D, D), const),
                  pl.BlockSpec((LANES, D), const),
                  pl.BlockSpec((LANES, D), const),
                  pl.BlockSpec((LANES, 1), const),
                  pl.BlockSpec((tm, tm), const)],
        out_specs=[pl.BlockSpec((tm, D), row), pl.BlockSpec((tm, D), row),
                   pl.BlockSpec((8, tm), col), pl.BlockSpec((8, tm), col),
                   pl.BlockSpec((N_EXPERTS, LANES), const)],
        out_shape=[jax.ShapeDtypeStruct((T, D), F32), jax.ShapeDtypeStruct((T, D), F32),
                   jax.ShapeDtypeStruct((8, T), I32), jax.ShapeDtypeStruct((8, T), F32),
                   jax.ShapeDtypeStruct((N_EXPERTS, LANES), F32)],
        scratch_shapes=[pltpu.VMEM((tm, D), BF16), pltpu.VMEM((tm, D), BF16),
                        pltpu.VMEM((N_EXPERTS, LANES), F32)],
        compiler_params=_cparams(("arbitrary",)),
        name="merge",
    )(x2, ya, yb, yc, g1, sh1, sc1, gt1, g2, sh2, sc2, gw, gb, bw, ow, rwh, rwl, rb, tri)


def _row_copy(src_ref, src_row, dst_ref, dst_row, sem):
    return pltpu.make_async_copy(src_ref.at[pl.ds(src_row, 1), :], dst_ref.at[pl.ds(dst_row, 1), :], sem)


def _drain_rows(src_ref, dst_ref, sem, n):
    def body(r, carry):
        _row_copy(src_ref, 0, dst_ref, 0, sem).wait()
        return carry
    lax.fori_loop(0, n, body, 0, unroll=8)


def _dispatch_kernel(dest_ref, h_ref, xs_in_ref, xs_ref, dsm_ref, sem_ref, isem_ref):
    del xs_in_ref
    i = pl.program_id(0)
    tm = dsm_ref.shape[1] // 2
    cp = pltpu.make_async_copy(dest_ref.at[0], dsm_ref, isem_ref)
    cp.start()
    cp.wait()
    slot = i % 2
    base = i * tm

    def issue(r, carry):
        _row_copy(h_ref, base + r, xs_ref, dsm_ref[0, 2 * r], sem_ref.at[slot]).start(priority=0)
        _row_copy(h_ref, base + r, xs_ref, dsm_ref[0, 2 * r + 1], sem_ref.at[slot]).start(priority=1)
        return carry

    lax.fori_loop(0, tm, issue, 0, unroll=8)

    @pl.when(i > 0)
    def _():
        _drain_rows(h_ref, xs_ref, sem_ref.at[1 - slot], 2 * tm)

    @pl.when(i == pl.num_programs(0) - 1)
    def _():
        _drain_rows(h_ref, xs_ref, sem_ref.at[slot], 2 * tm)


def _dispatch(h2, dest, P):
    T, D = h2.shape
    tm = _pick(T, 256)
    nt = T // tm
    xs0 = jnp.zeros((P, D), F32)
    return pl.pallas_call(
        _dispatch_kernel,
        grid=(nt,),
        in_specs=[pl.BlockSpec((1, 1, 2 * tm), lambda i: (i, 0, 0)),
                  pl.BlockSpec(memory_space=pl.ANY),
                  pl.BlockSpec(memory_space=pl.ANY)],
        out_specs=pl.BlockSpec(memory_space=pl.ANY),
        out_shape=jax.ShapeDtypeStruct((P, D), F32),
        scratch_shapes=[pltpu.SMEM((1, 2 * tm), I32), pltpu.SemaphoreType.DMA((2,)),
                        pltpu.SemaphoreType.DMA],
        input_output_aliases={2: 0},
        compiler_params=_cparams(("arbitrary",)),
        name="moe_dispatch",
    )(dest.reshape(nt, 1, 2 * tm), h2, xs0)


def _expert_kernel(blk_e_ref, nvalid_ref, xs_ref, w1_ref, w3_ref, w2_ref, ys_ref):
    del blk_e_ref
    valid = pl.program_id(0) < nvalid_ref[0]

    @pl.when(valid)
    def _():
        xb = xs_ref[...].astype(BF16)
        a = jnp.dot(xb, w1_ref[...], preferred_element_type=F32)
        b = jnp.dot(xb, w3_ref[...], preferred_element_type=F32)
        hid = (a * jax.nn.sigmoid(a) * b).astype(BF16)
        ys_ref[...] = jnp.dot(hid, w2_ref[...], preferred_element_type=F32)

    @pl.when(jnp.logical_not(valid))
    def _():
        ys_ref[...] = jnp.zeros_like(ys_ref)


def _experts(xs, blk_e, nvalid, w1, w3, w2, tb):
    P, D = xs.shape
    n_blocks = P // tb
    rows = lambda i, be, nv: (jnp.minimum(i, nv[0] - 1), 0)
    grid_spec = pltpu.PrefetchScalarGridSpec(
        num_scalar_prefetch=2,
        grid=(n_blocks,),
        in_specs=[pl.BlockSpec((tb, D), rows),
                  pl.BlockSpec((None, D, D_EXPERT), lambda i, be, nv: (be[i], 0, 0)),
                  pl.BlockSpec((None, D, D_EXPERT), lambda i, be, nv: (be[i], 0, 0)),
                  pl.BlockSpec((None, D_EXPERT, D), lambda i, be, nv: (be[i], 0, 0))],
        out_specs=pl.BlockSpec((tb, D), lambda i, be, nv: (i, 0)),
    )
    return pl.pallas_call(
        _expert_kernel,
        grid_spec=grid_spec,
        out_shape=jax.ShapeDtypeStruct((P, D), F32),
        compiler_params=_cparams(("arbitrary",)),
        name="moe_experts",
    )(blk_e, nvalid, xs, w1, w3, w2)


def _combine_kernel(dcur_ref, dnext_ref, x_ref, mf_ref, gt_ref, fg_ref, ys_ref, o_ref,
                    dsm_ref, ybuf_ref, sem_ref, isem_ref, *, final):
    i = pl.program_id(0)
    tm = x_ref.shape[0]
    slot = i % 2

    def fetch(d_ref, s):
        cp = pltpu.make_async_copy(d_ref.at[0], dsm_ref, isem_ref)
        cp.start()
        cp.wait()

        def issue(r, carry):
            _row_copy(ys_ref, dsm_ref[0, 2 * r], ybuf_ref.at[s, 0], r, sem_ref.at[s]).start(priority=0)
            _row_copy(ys_ref, dsm_ref[0, 2 * r + 1], ybuf_ref.at[s, 1], r, sem_ref.at[s]).start(priority=1)
            return carry

        lax.fori_loop(0, tm, issue, 0, unroll=8)

    @pl.when(i == 0)
    def _():
        fetch(dcur_ref, 0)

    @pl.when(i + 1 < pl.num_programs(0))
    def _():
        fetch(dnext_ref, 1 - slot)

    _drain_rows(ys_ref, ybuf_ref.at[slot, 0], sem_ref.at[slot], 2 * tm)

    mf = mf_ref[...]
    ffn = ybuf_ref[slot, 0] * mf[:, 0:1] + ybuf_ref[slot, 1] * mf[:, 1:2]
    out = x_ref[...] + gt_ref[...] * ffn
    if final:
        out = _rms(out, fg_ref[...])
    o_ref[...] = out


def _combine(x1, S, mf, gt2, final_g, ys, dest, final):
    T, D = x1.shape
    tm = _pick(S, 256)
    per_b = S // tm
    nt = T // tm
    dest3 = dest.reshape(nt, 1, 2 * tm)
    return pl.pallas_call(
        functools.partial(_combine_kernel, final=final),
        grid=(nt,),
        in_specs=[pl.BlockSpec((1, 1, 2 * tm), lambda i: (i, 0, 0)),
                  pl.BlockSpec((1, 1, 2 * tm), lambda i: (jnp.minimum(i + 1, nt - 1), 0, 0)),
                  pl.BlockSpec((tm, D), lambda i: (i, 0)),
                  pl.BlockSpec((tm, 2), lambda i: (i, 0)),
                  pl.BlockSpec((None, 1, D), lambda i: (i // per_b, 0, 0)),
                  pl.BlockSpec((1, D), lambda i: (0, 0)),
                  pl.BlockSpec(memory_space=pl.ANY)],
        out_specs=pl.BlockSpec((tm, D), lambda i: (i, 0)),
        out_shape=jax.ShapeDtypeStruct((T, D), F32),
        scratch_shapes=[pltpu.SMEM((1, 2 * tm), I32), pltpu.VMEM((2, 2, tm, D), F32),
                        pltpu.SemaphoreType.DMA((2,)), pltpu.SemaphoreType.DMA],
        compiler_params=_cparams(("arbitrary",)),
        name="moe_combine",
    )(dest3, dest3, x1, mf, gt2, final_g, ys)


def _prep_w_in(w):
    offs = [0]
    for s in (256, 256, 512, 512, 512, 512, 512, FOX_HEADS, MLA_Q_RANK, MLA_KV_RANK, MLA_ROPE):
        offs.append(offs[-1] + s)
    rq, rk, rv, rg, fq, fk, fv, ff, mq, mkv, mkr = [w[:, offs[i]:offs[i + 1]] for i in range(11)]
    pad = jnp.zeros((w.shape[0], LANES - MLA_ROPE - FOX_HEADS), w.dtype)
    return jnp.concatenate([rq, rk, rv, rg, fq, fk, fv, mq, mkv, mkr, ff, pad], axis=1).astype(BF16)


def _prep_wq_up(w):
    r = w.reshape(MLA_Q_RANK, MLA_HEADS, MLA_NOPE + MLA_ROPE)
    r = jnp.pad(r, ((0, 0), (0, 0), (0, MLA_DQ - MLA_NOPE - MLA_ROPE)))
    return r.reshape(MLA_Q_RANK, MLA_HEADS * MLA_DQ).astype(BF16)


def _prep_router(w_grp, b_grp, w_exp, b_exp):
    D = w_grp.shape[0]
    pad = LANES - N_EXPERTS - N_GROUPS
    rwt = jnp.concatenate([w_exp, w_grp, jnp.zeros((D, pad), F32)], axis=1).astype(F32).T
    rwh = rwt.astype(BF16)
    rwl = (rwt - rwh.astype(F32)).astype(BF16)
    rb = jnp.concatenate([b_exp, b_grp, jnp.zeros((pad,), F32)]).astype(F32).reshape(LANES, 1)
    return rwh, rwl, rb


def kernel(x, c, positions, ada_w, ada_b, norm1_g, norm2_g, w_in, fox_fb, mla_q_norm_g, mla_wq_up, mla_kv_norm_g, mla_wkv_up, gate_w, gate_b, branch_w, out_w, router_grp_w, router_grp_b, router_exp_w, router_exp_b, exp_w1, exp_w3, exp_w2, final_g):
    B, S, D = x.shape
    L = ada_w.shape[0]
    T = B * S
    A = 2 * T
    tb = _pick(A, 256)
    n_blocks = A // tb + N_EXPERTS
    P = n_blocks * tb

    mod = _adaln(c, ada_w, ada_b)
    cos_t, sin_t = _rope_tables(positions)
    x2 = x.reshape(T, D)
    final_g2 = final_g.reshape(1, D)

    for l in range(L):
        sh1, sc1, gt1, sh2, sc2, gt2 = [mod[l, :, i * D:(i + 1) * D].reshape(B, 1, D) for i in range(6)]
        g1 = norm1_g[l].reshape(1, D)
        g2 = norm2_g[l].reshape(1, D)
        (rq, rk, rv, rg, fq, fk, fv, mq, mk, kpe, mv, ffp) = _proj(
            x2, S, g1, sh1, sc1, _prep_w_in(w_in[l]), cos_t, sin_t,
            mla_q_norm_g[l].reshape(1, -1), _prep_wq_up(mla_wq_up[l]),
            mla_kv_norm_g[l].reshape(1, -1), mla_wkv_up[l].astype(BF16))

        ya = _retention(rq, rk, rv, rg, B, S).reshape(T, -1)
        kb = _fox_gate(ffp, fox_fb[l], S)
        r3 = lambda a: a.reshape(B, S, -1)
        yb = _flash(r3(fq), r3(fk), r3(kb), r3(fv), B, S, FOX_HEADS, FOX_DH, True, 3, "flash_fox").reshape(T, -1)
        yc = _flash(r3(mq), r3(mk), r3(kpe), r3(mv), B, S, MLA_HEADS, MLA_DQ, False, 0, "flash_mla").reshape(T, -1)

        rwh, rwl, rb = _prep_router(router_grp_w[l], router_grp_b[l], router_exp_w[l], router_exp_b[l])
        x1, h2, mi, mf, cnt = _merge(
            x2, S, ya, yb, yc, g1, sh1, sc1, gt1, g2, sh2, sc2,
            gate_w[l].astype(BF16), gate_b[l].reshape(1, -1), branch_w[l].astype(BF16),
            out_w[l].astype(BF16), rwh, rwl, rb)

        counts = cnt[:, 0].astype(I32)
        pcounts = (counts + tb - 1) // tb * tb
        pends = jnp.cumsum(pcounts)
        pstarts = pends - pcounts
        sel = mi[0:2, :, None] == jnp.arange(N_EXPERTS, dtype=I32)
        dest = (jnp.sum(jnp.where(sel, pstarts, 0), axis=-1) + mi[2:4]).T.reshape(A)
        blk_pos = jnp.arange(n_blocks, dtype=I32) * tb
        blk_e = jnp.minimum(jnp.sum((pends[None, :] <= blk_pos[:, None]).astype(I32), axis=1), N_EXPERTS - 1)
        nvalid = (pends[-1:] // tb).astype(I32)

        xs = _dispatch(h2, dest, P)
        ys = _experts(xs, blk_e, nvalid, exp_w1[l].astype(BF16), exp_w3[l].astype(BF16),
                      exp_w2[l].astype(BF16), tb)
        x2 = _combine(x1, S, mf[0:2].T, gt2, final_g2, ys, dest, final=(l == L - 1))

    return x2.reshape(B, S, D)
```

```python
import functools
import math

import jax
import jax.numpy as jnp
from jax import lax
from jax.experimental import pallas as pl
from jax.experimental.pallas import tpu as pltpu

F32 = jnp.float32
BF16 = jnp.bfloat16
I32 = jnp.int32
HIGHEST = lax.Precision.HIGHEST

EPS = 1e-6
ROPE_THETA = 10000.0
RET_HEADS = 4
RET_DK = 64
RET_DV = 128
RET_CHUNK = 128
FOX_HEADS = 4
FOX_DH = 128
MLA_HEADS = 4
MLA_Q_RANK = 256
MLA_KV_RANK = 128
MLA_NOPE = 128
MLA_ROPE = 64
MLA_V = 128
MLA_DQ = 256
N_BRANCH = 3
BRANCH_W = 512
N_GROUPS = 4
EXP_PER_GROUP = 8
N_EXPERTS = N_GROUPS * EXP_PER_GROUP
D_EXPERT = 512

LANES = 128
V7X_VMEM_LIMIT = 56 * 1024 * 1024

C_RQ, C_RK, C_RV, C_RG = 0, 256, 512, 1024
C_FQ, C_FK, C_FV = 1536, 2048, 2560
C_MQ, C_MKV, C_TAIL = 3072, 3328, 3456
D_IN_PAD = 3584
FF_LANE = MLA_ROPE

NEG_BIG = -1e30
MERGE_COL_CHUNK = 256
FLASH_ROW_BLOCK = 128
LOG2E = math.log2(math.e)


def _cparams(sem):
    return pltpu.CompilerParams(dimension_semantics=sem, vmem_limit_bytes=V7X_VMEM_LIMIT)


def _pick(n, pref):
    t = min(n, pref)
    assert n % t == 0, (n, t)
    return t


def _adaln_kernel(c_ref, w_ref, b_ref, o_ref):
    c = c_ref[...]
    ca = c * jax.nn.sigmoid(c)
    o_ref[...] = jnp.dot(ca, w_ref[...], preferred_element_type=F32, precision=HIGHEST) + b_ref[...]


def _adaln(c, ada_w, ada_b):
    L, D, N = ada_w.shape
    B = c.shape[0]
    tn = _pick(N, 1536)
    return pl.pallas_call(
        _adaln_kernel,
        grid=(L, N // tn),
        in_specs=[pl.BlockSpec((B, D), lambda l, j: (0, 0)),
                  pl.BlockSpec((None, D, tn), lambda l, j: (l, 0, j)),
                  pl.BlockSpec((None, 1, tn), lambda l, j: (l, 0, j))],
        out_specs=pl.BlockSpec((None, B, tn), lambda l, j: (l, 0, j)),
        out_shape=jax.ShapeDtypeStruct((L, B, N), F32),
        compiler_params=_cparams(("arbitrary", "arbitrary")),
        name="adaln",
    )(c, ada_w, ada_b.reshape(L, 1, N))


def _rope_table_kernel(pos_ref, inv_ref, sign_ref, cos_ref, sin_ref):
    ang = pos_ref[...].astype(F32) * inv_ref[...]
    cos_ref[...] = jnp.cos(ang)
    sin_ref[...] = jnp.sin(ang) * sign_ref[...]


def _rope_tables(positions):
    T = positions.size
    tm = _pick(T, 2048)
    half = MLA_ROPE // 2
    inv = ROPE_THETA ** (-jnp.arange(0, MLA_ROPE, 2, dtype=F32) / MLA_ROPE)
    inv_t = jnp.tile(inv, LANES // half).reshape(1, LANES)
    sign = jnp.where((jnp.arange(LANES) % MLA_ROPE) < half, -1.0, 1.0).astype(F32).reshape(1, LANES)
    return pl.pallas_call(
        _rope_table_kernel,
        grid=(T // tm,),
        in_specs=[pl.BlockSpec((tm, 1), lambda i: (i, 0)),
                  pl.BlockSpec((1, LANES), lambda i: (0, 0)),
                  pl.BlockSpec((1, LANES), lambda i: (0, 0))],
        out_specs=[pl.BlockSpec((tm, LANES), lambda i: (i, 0))] * 2,
        out_shape=[jax.ShapeDtypeStruct((T, LANES), F32)] * 2,
        compiler_params=_cparams(("arbitrary",)),
        name="rope_tables",
    )(positions.reshape(T, 1), inv_t, sign)


def _rope_slab(x, cos_t, sin_t, lane):
    nxt = pltpu.roll(x, LANES - 32, axis=1)
    prv = pltpu.roll(x, 32, axis=1)
    swapped = jnp.where((lane & 32) == 0, nxt, prv)
    return x * cos_t + swapped * sin_t


def _norm_mod(x, g, shift, scale):
    y = x * lax.rsqrt(jnp.mean(x * x, axis=-1, keepdims=True) + EPS)
    return (y * g) * (1.0 + scale) + shift


def _rms(x, g):
    return x * lax.rsqrt(jnp.mean(x * x, axis=-1, keepdims=True) + EPS) * g


def _proj_kernel(x_ref, g_ref, sh_ref, sc_ref, w_ref, cos_ref, sin_ref,
                 gq_ref, wq_ref, gkv_ref, wkv_ref,
                 rq_ref, rk_ref, rv_ref, rg_ref, fq_ref, fk_ref, fv_ref,
                 mq_ref, mk_ref, kpe_ref, mv_ref, ff_ref):
    h = _norm_mod(x_ref[...], g_ref[...], sh_ref[...], sc_ref[...]).astype(BF16)
    cos_t = cos_ref[...]
    sin_t = sin_ref[...]
    lane = lax.broadcasted_iota(I32, cos_t.shape, 1)

    def proj(c0, width):
        return jnp.dot(h, w_ref[:, c0:c0 + width], preferred_element_type=F32)

    rq = proj(C_RQ, 256)
    rk = proj(C_RK, 256)
    for s in range(2):
        sl = slice(s * LANES, (s + 1) * LANES)
        rq_ref[:, sl] = _rope_slab(rq[:, sl], cos_t, sin_t, lane).astype(BF16)
        rk_ref[:, sl] = (_rope_slab(rk[:, sl], cos_t, sin_t, lane) * (RET_DK ** -0.5)).astype(BF16)
    rv_ref[...] = proj(C_RV, 512).astype(BF16)
    rg_ref[...] = proj(C_RG, 512).astype(BF16)
    fq_ref[...] = (proj(C_FQ, 512) * (FOX_DH ** -0.5 * LOG2E)).astype(BF16)
    fk_ref[...] = proj(C_FK, 512).astype(BF16)
    fv_ref[...] = proj(C_FV, 512).astype(BF16)

    tail = proj(C_TAIL, LANES)
    ff_ref[...] = tail
    kpe_ref[...] = jnp.where(lane < MLA_ROPE, _rope_slab(tail, cos_t, sin_t, lane), 0.0).astype(BF16)

    qn = _rms(proj(C_MQ, MLA_Q_RANK), gq_ref[...]).astype(BF16)
    qh = jnp.dot(qn, wq_ref[...], preferred_element_type=F32)
    q_scale = (MLA_NOPE + MLA_ROPE) ** -0.5 * LOG2E
    for hd in range(MLA_HEADS):
        c0 = hd * MLA_DQ
        mq_ref[:, c0:c0 + LANES] = (qh[:, c0:c0 + LANES] * q_scale).astype(BF16)
        pe = _rope_slab(qh[:, c0 + LANES:c0 + 2 * LANES], cos_t, sin_t, lane)
        mq_ref[:, c0 + LANES:c0 + 2 * LANES] = jnp.where(lane < MLA_ROPE, pe * q_scale, 0.0).astype(BF16)

    kvn = _rms(proj(C_MKV, MLA_KV_RANK), gkv_ref[...]).astype(BF16)
    kvh = jnp.dot(kvn, wkv_ref[...], preferred_element_type=F32)
    for hd in range(MLA_HEADS):
        c0 = hd * (MLA_NOPE + MLA_V)
        mk_ref[:, hd * MLA_NOPE:(hd + 1) * MLA_NOPE] = kvh[:, c0:c0 + MLA_NOPE].astype(BF16)
        mv_ref[:, hd * MLA_V:(hd + 1) * MLA_V] = kvh[:, c0 + MLA_NOPE:c0 + MLA_NOPE + MLA_V].astype(BF16)


def _proj(x2, S, g, sh, sc, w_all, cos_t, sin_t, gq, wq, gkv, wkv):
    T, D = x2.shape
    tm = _pick(S, 512)
    per_b = S // tm
    row = lambda i: (i, 0)
    const = lambda i: (0, 0)
    batch = lambda i: (i // per_b, 0, 0)
    widths = [256, 256, 512, 512, 512, 512, 512, MLA_HEADS * MLA_DQ, MLA_HEADS * MLA_NOPE, LANES,
              MLA_HEADS * MLA_V]
    out_shape = [jax.ShapeDtypeStruct((T, w), BF16) for w in widths]
    out_shape.append(jax.ShapeDtypeStruct((T, LANES), F32))
    out_specs = [pl.BlockSpec((tm, w), row) for w in widths] + [pl.BlockSpec((tm, LANES), row)]
    return pl.pallas_call(
        _proj_kernel,
        grid=(T // tm,),
        in_specs=[pl.BlockSpec((tm, D), row),
                  pl.BlockSpec((1, D), const),
                  pl.BlockSpec((None, 1, D), batch),
                  pl.BlockSpec((None, 1, D), batch),
                  pl.BlockSpec((D, D_IN_PAD), const),
                  pl.BlockSpec((tm, LANES), row),
                  pl.BlockSpec((tm, LANES), row),
                  pl.BlockSpec((1, MLA_Q_RANK), const),
                  pl.BlockSpec((MLA_Q_RANK, MLA_HEADS * MLA_DQ), const),
                  pl.BlockSpec((1, MLA_KV_RANK), const),
                  pl.BlockSpec((MLA_KV_RANK, MLA_HEADS * (MLA_NOPE + MLA_V)), const)],
        out_specs=out_specs,
        out_shape=out_shape,
        compiler_params=_cparams(("arbitrary",)),
        name="proj",
    )(x2, g, sh, sc, w_all, cos_t, sin_t, gq, wq, gkv, wkv)


def _split3(x):
    a = x.astype(BF16)
    r = x - a.astype(F32)
    b = r.astype(BF16)
    c = (r - b.astype(F32)).astype(BF16)
    return a, b, c


def _fox_gate_kernel(ff_ref, fb_ref, tri_ref, kb_ref, carry_ref, *, tiles_per_seq):
    @pl.when(pl.program_id(0) % tiles_per_seq == 0)
    def _():
        carry_ref[...] = jnp.zeros_like(carry_ref)

    tm = ff_ref.shape[0]
    lane = lax.broadcasted_iota(I32, (tm, LANES), 1)
    z = ff_ref[...] + fb_ref[...]
    ls = -(jnp.maximum(-z, 0.0) + jnp.log1p(jnp.exp(-jnp.abs(z))))
    ls = jnp.where((lane >= FF_LANE) & (lane < FF_LANE + FOX_HEADS), ls, 0.0)
    tri = tri_ref[...]
    f = carry_ref[...]
    for part in _split3(ls):
        f = f + jnp.dot(tri, part, preferred_element_type=F32)
    carry_ref[...] = f[tm - 1:tm, :]
    f2 = f * LOG2E
    for hd in range(FOX_HEADS):
        src = FF_LANE + hd
        g = jnp.where(lane == 0, pltpu.roll(f2, (LANES - src) % LANES, axis=1),
            jnp.where(lane == 1, pltpu.roll(f2, (LANES + 1 - src) % LANES, axis=1),
            jnp.where(lane == 2, pltpu.roll(f2, (LANES + 2 - src) % LANES, axis=1), 0.0)))
        hi, mid, lo = _split3(g)
        kb_ref[:, hd * LANES:(hd + 1) * LANES] = jnp.where(lane == 0, hi, jnp.where(lane == 1, mid, lo))


def _fox_gate(ffp, fb, S):
    T = ffp.shape[0]
    tm = _pick(S, 512)
    idx = jnp.arange(tm)
    tri = (idx[None, :] <= idx[:, None]).astype(BF16)
    fbv = jnp.zeros((1, LANES), F32).at[0, FF_LANE:FF_LANE + FOX_HEADS].set(fb)
    return pl.pallas_call(
        functools.partial(_fox_gate_kernel, tiles_per_seq=S // tm),
        grid=(T // tm,),
        in_specs=[pl.BlockSpec((tm, LANES), lambda i: (i, 0)),
                  pl.BlockSpec((1, LANES), lambda i: (0, 0)),
                  pl.BlockSpec((tm, tm), lambda i: (0, 0))],
        out_specs=pl.BlockSpec((tm, FOX_HEADS * LANES), lambda i: (i, 0)),
        out_shape=jax.ShapeDtypeStruct((T, FOX_HEADS * LANES), BF16),
        scratch_shapes=[pltpu.VMEM((1, LANES), F32)],
        compiler_params=_cparams(("arbitrary",)),
        name="fox_gate",
    )(ffp, fbv, tri)


def _retention_kernel(dchunk_ref, q_ref, k_ref, v_ref, g_ref, dmask_ref, din_ref, dout_ref,
                      o_ref, state_ref, *, n_chunks):
    @pl.when(pl.program_id(1) == 0)
    def _():
        state_ref[...] = jnp.zeros_like(state_ref)

    C = RET_CHUNK
    lane = lax.broadcasted_iota(I32, (C, LANES), 1)
    for ci in range(n_chunks):
        rows = slice(ci * C, (ci + 1) * C)
        for hd in range(RET_HEADS):
            slab = slice((hd // 2) * LANES, (hd // 2 + 1) * LANES)
            mine = (lane < RET_DK) if hd % 2 == 0 else (lane >= RET_DK)
            q = jnp.where(mine, q_ref[rows, slab], 0)
            k = jnp.where(mine, k_ref[rows, slab], 0)
            vcols = slice(hd * RET_DV, (hd + 1) * RET_DV)
            v = v_ref[rows, vcols]
            state = state_ref[hd]
            scores = lax.dot_general(q, k, (((1,), (1,)), ((), ())),
                                     preferred_element_type=F32) * dmask_ref[hd]
            inner = jnp.dot(scores.astype(BF16), v, preferred_element_type=F32)
            cross = jnp.dot(q, state.astype(BF16), preferred_element_type=F32) * din_ref[hd]
            o = inner + cross
            vd = (v.astype(F32) * dout_ref[hd]).astype(BF16)
            kv = lax.dot_general(k, vd, (((0,), (0,)), ((), ())), preferred_element_type=F32)
            state_ref[hd] = state * dchunk_ref[hd] + kv
            mu = jnp.mean(o, axis=-1, keepdims=True)
            d = o - mu
            var = jnp.mean(d * d, axis=-1, keepdims=True)
            on = d * lax.rsqrt(var + EPS)
            g = g_ref[rows, vcols].astype(F32)
            o_ref[rows, vcols] = (g * jax.nn.sigmoid(g) * on).astype(BF16)


def _retention(rq, rk, rv, rg, B, S):
    H, C = RET_HEADS, RET_CHUNK
    tr = _pick(S, 4 * C)
    n_chunks = tr // C
    log_gamma = jnp.log1p(-jnp.exp2(-5.0 - jnp.arange(H, dtype=F32)))
    idx = jnp.arange(C, dtype=F32)
    rel = idx[:, None] - idx[None, :]
    dmask = jnp.where(rel >= 0, jnp.exp(log_gamma[:, None, None] * jnp.maximum(rel, 0.0)), 0.0)
    decay_in = jnp.exp(log_gamma[:, None] * (idx + 1.0))
    decay_out = jnp.exp(log_gamma[:, None] * (C - 1.0 - idx))
    decay_chunk = jnp.exp(log_gamma * C)
    din = jnp.broadcast_to(decay_in[:, :, None], (H, C, RET_DV))
    dout = jnp.broadcast_to(decay_out[:, :, None], (H, C, RET_DV))
    tok = lambda b, i: (b, i, 0)
    const3 = lambda b, i: (0, 0, 0)
    return pl.pallas_call(
        functools.partial(_retention_kernel, n_chunks=n_chunks),
        grid=(B, S // tr),
        in_specs=[pl.BlockSpec(memory_space=pltpu.SMEM),
                  pl.BlockSpec((None, tr, H * RET_DK), tok),
                  pl.BlockSpec((None, tr, H * RET_DK), tok),
                  pl.BlockSpec((None, tr, H * RET_DV), tok),
                  pl.BlockSpec((None, tr, H * RET_DV), tok),
                  pl.BlockSpec((H, C, C), const3),
                  pl.BlockSpec((H, C, RET_DV), const3),
                  pl.BlockSpec((H, C, RET_DV), const3)],
        out_specs=pl.BlockSpec((None, tr, H * RET_DV), tok),
        out_shape=jax.ShapeDtypeStruct((B, S, H * RET_DV), BF16),
        scratch_shapes=[pltpu.VMEM((H, LANES, RET_DV), F32)],
        compiler_params=_cparams(("arbitrary", "arbitrary")),
        name="retention",
    )(decay_chunk, rq.reshape(B, S, -1), rk.reshape(B, S, -1), rv.reshape(B, S, -1),
      rg.reshape(B, S, -1), dmask, din, dout)


def _flash_kernel(q_ref, k_ref, ke_ref, v_ref, o_ref, m_ref, l_ref, acc_ref, s_ref, p_ref, *, q_bias_cols):
    qi = pl.program_id(2)
    tq = q_ref.shape[0]
    tk = tq
    reps = tk // LANES
    m_ref[...] = jnp.full(m_ref.shape, NEG_BIG, F32)
    l_ref[...] = jnp.zeros_like(l_ref)
    acc_ref[...] = jnp.zeros_like(acc_ref)
    q = q_ref[...]
    if q_bias_cols:
        lane = lax.broadcasted_iota(I32, (tq, LANES), 1)
        q = jnp.concatenate([q, jnp.where(lane < q_bias_cols, -1.0, 0.0).astype(BF16)], axis=1)

    def scores(j):
        start = pl.multiple_of(j * tk, tk)
        kj = jnp.concatenate([k_ref[pl.ds(start, tk), :], ke_ref[pl.ds(start, tk), :]], axis=1)
        return lax.dot_general(q, kj, (((1,), (1,)), ((), ())), preferred_element_type=F32)

    def update(slot, j, diag):
        start = pl.multiple_of(j * tk, tk)
        for rb in range(tq // FLASH_ROW_BLOCK):
            rows = slice(rb * FLASH_ROW_BLOCK, (rb + 1) * FLASH_ROW_BLOCK)
            s = s_ref[slot, rows, :]
            if diag:
                r = lax.broadcasted_iota(I32, s.shape, 0) + rb * FLASH_ROW_BLOCK
                c = lax.broadcasted_iota(I32, s.shape, 1)
                s = jnp.where(c <= r, s, NEG_BIG)
            m_prev = m_ref[rows, :]
            m_new = jnp.maximum(m_prev, jnp.max(s, axis=-1, keepdims=True))
            alpha = jnp.exp2(m_prev - m_new)
            p = jnp.exp2(s - jnp.concatenate([m_new] * reps, axis=1))
            l_ref[rows, :] = alpha * l_ref[rows, :] + jnp.sum(p, axis=-1, keepdims=True)
            acc_ref[rows, :] = alpha * acc_ref[rows, :]
            m_ref[rows, :] = m_new
            p_ref[rows, :] = p.astype(BF16)
        acc_ref[...] += jnp.dot(p_ref[...], v_ref[pl.ds(start, tk), :], preferred_element_type=F32)

    s_ref[0] = scores(0)

    def pair(p, carry):
        j = 2 * p
        s_ref[1] = scores(j + 1)
        update(0, j, False)
        s_ref[0] = scores(j + 2)
        update(1, j + 1, False)
        return carry

    lax.fori_loop(0, qi // 2, pair, 0)

    @pl.when(qi % 2 == 1)
    def _():
        s_ref[1] = scores(qi)
        update(0, qi - 1, False)
        update(1, qi, True)

    @pl.when(qi % 2 == 0)
    def _():
        update(0, qi, True)

    o_ref[...] = (acc_ref[...] / l_ref[...]).astype(o_ref.dtype)


def _flash(q, k, ke, v, B, S, H, dq, ke_per_head, q_bias_cols, name):
    tq = _pick(S, 512)
    nq = S // tq
    ke_map = (lambda b, h, i: (b, 0, h)) if ke_per_head else (lambda b, h, i: (b, 0, 0))
    return pl.pallas_call(
        functools.partial(_flash_kernel, q_bias_cols=q_bias_cols),
        grid=(B, H, nq),
        in_specs=[pl.BlockSpec((None, tq, dq), lambda b, h, i: (b, i, h)),
                  pl.BlockSpec((None, S, LANES), lambda b, h, i: (b, 0, h)),
                  pl.BlockSpec((None, S, LANES), ke_map),
                  pl.BlockSpec((None, S, LANES), lambda b, h, i: (b, 0, h))],
        out_specs=pl.BlockSpec((None, tq, LANES), lambda b, h, i: (b, i, h)),
        out_shape=jax.ShapeDtypeStruct((B, S, H * LANES), BF16),
        scratch_shapes=[pltpu.VMEM((tq, LANES), F32), pltpu.VMEM((tq, LANES), F32),
                        pltpu.VMEM((tq, LANES), F32), pltpu.VMEM((2, tq, tq), F32),
                        pltpu.VMEM((tq, tq), BF16)],
        compiler_params=_cparams(("arbitrary", "arbitrary", "arbitrary")),
        name=name,
    )(q, k, ke, v)


def _merge_kernel(x_ref, ya_ref, yb_ref, yc_ref, g1_ref, sh1_ref, sc1_ref, gt1_ref,
                  g2_ref, sh2_ref, sc2_ref, gw_ref, gb_ref, bw_ref, ow_ref,
                  rwh_ref, rwl_ref, rb_ref, tri_ref,
                  x1_ref, h2_ref, mi_ref, mf_ref, cnt_ref, h_ref, mg_ref, carry_ref):
    tm, D = x_ref.shape
    cn = MERGE_COL_CHUNK

    @pl.when(pl.program_id(0) == 0)
    def _():
        carry_ref[...] = jnp.zeros_like(carry_ref)

    h_ref[...] = _norm_mod(x_ref[...], g1_ref[...], sh1_ref[...], sc1_ref[...]).astype(BF16)
    for n in range(D // cn):
        cols = slice(n * cn, (n + 1) * cn)
        merged = None
        for i, y_ref in enumerate((ya_ref, yb_ref, yc_ref)):
            gcols = slice(i * D + n * cn, i * D + (n + 1) * cn)
            gate = jax.nn.sigmoid(jnp.dot(h_ref[...], gw_ref[:, gcols], preferred_element_type=F32)
                                  + gb_ref[:, gcols])
            br = jnp.dot(y_ref[...], bw_ref[i, :, cols], preferred_element_type=F32)
            merged = gate * br if merged is None else merged + gate * br
        mg_ref[:, cols] = merged.astype(BF16)
    for n in range(D // cn):
        cols = slice(n * cn, (n + 1) * cn)
        mix = jnp.dot(mg_ref[...], ow_ref[:, cols], preferred_element_type=F32)
        x1_ref[:, cols] = x_ref[:, cols] + gt1_ref[:, cols] * mix
    h2 = _norm_mod(x1_ref[...], g2_ref[...], sh2_ref[...], sc2_ref[...])
    h2_ref[...] = h2

    hh = h2.astype(BF16)
    hl = (h2 - hh.astype(F32)).astype(BF16)
    nt = lambda a, b: lax.dot_general(a, b, (((1,), (1,)), ((), ())), preferred_element_type=F32)
    lt = nt(rwh_ref[...], hh) + nt(rwh_ref[...], hl) + nt(rwl_ref[...], hh) + rb_ref[...]
    row8 = lax.broadcasted_iota(I32, (EXP_PER_GROUP, tm), 0)
    gl = jnp.where(row8 < N_GROUPS, lt[N_EXPERTS:N_EXPERTS + EXP_PER_GROUP, :], -jnp.inf)
    gmax = jnp.max(gl, axis=0, keepdims=True)
    g_idx = jnp.min(jnp.where(gl == gmax, row8, EXP_PER_GROUP), axis=0, keepdims=True)
    g_w = 1.0 / jnp.sum(jnp.exp(gl - gmax), axis=0, keepdims=True)
    el = lt[(N_GROUPS - 1) * EXP_PER_GROUP:N_EXPERTS, :]
    for g in range(N_GROUPS - 2, -1, -1):
        el = jnp.where(g_idx == g, lt[g * EXP_PER_GROUP:(g + 1) * EXP_PER_GROUP, :], el)
    e1 = jnp.max(el, axis=0, keepdims=True)
    i1 = jnp.min(jnp.where(el == e1, row8, EXP_PER_GROUP), axis=0, keepdims=True)
    el2 = jnp.where(row8 == i1, -jnp.inf, el)
    e2 = jnp.max(el2, axis=0, keepdims=True)
    i2 = jnp.min(jnp.where(el2 == e2, row8, EXP_PER_GROUP), axis=0, keepdims=True)
    r = jnp.exp(e2 - e1)
    w1 = g_w / (1.0 + r)
    w2 = g_w * r / (1.0 + r)
    eid1 = g_idx * EXP_PER_GROUP + i1
    eid2 = g_idx * EXP_PER_GROUP + i2

    rowe = lax.broadcasted_iota(I32, (N_EXPERTS, tm), 0)
    hit1 = rowe == eid1
    hit2 = rowe == eid2
    onehot = jnp.where(hit1 | hit2, 1.0, 0.0)
    before = (jnp.dot(onehot.astype(BF16), tri_ref[...], preferred_element_type=F32)
              + jnp.concatenate([carry_ref[...]] * (tm // LANES), axis=1))
    rank1 = jnp.sum(jnp.where(hit1, before, 0.0), axis=0, keepdims=True)
    rank2 = jnp.sum(jnp.where(hit2, before, 0.0), axis=0, keepdims=True)
    carry_ref[...] = carry_ref[...] + jnp.sum(onehot, axis=1, keepdims=True)
    cnt_ref[...] = carry_ref[...]

    mi_ref[...] = jnp.where(row8 == 0, eid1,
                  jnp.where(row8 == 1, eid2,
                  jnp.where(row8 == 2, rank1.astype(I32),
                  jnp.where(row8 == 3, rank2.astype(I32), 0))))
    mf_ref[...] = jnp.where(row8 == 0, w1, jnp.where(row8 == 1, w2, 0.0))


def _merge(x2, S, ya, yb, yc, g1, sh1, sc1, gt1, g2, sh2, sc2, gw, gb, bw, ow, rwh, rwl, rb):
    T, D = x2.shape
    tm = _pick(S, 512)
    per_b = S // tm
    row = lambda i: (i, 0)
    col = lambda i: (0, i)
    const = lambda i: (0, 0)
    batch = lambda i: (i // per_b, 0, 0)
    idx = jnp.arange(tm)
    tri = (idx[:, None] < idx[None, :]).astype(BF16)
    vecb = pl.BlockSpec((None, 1, D), batch)
    return pl.pallas_call(
        _merge_kernel,
        grid=(T // tm,),
        in_specs=[pl.BlockSpec((tm, D), row),
                  pl.BlockSpec((tm, BRANCH_W), row),
                  pl.BlockSpec((tm, BRANCH_W), row),
                  pl.BlockSpec((tm, BRANCH_W), row),
                  pl.BlockSpec((1, D), const), vecb, vecb, vecb,
                  pl.BlockSpec((1, D), const), vecb, vecb,
                  pl.BlockSpec((D, N_BRANCH * D), const),
                  pl.BlockSpec((1, N_BRANCH * D), const),
                  pl.BlockSpec((N_BRANCH, BRANCH_W, D), lambda i: (0, 0, 0)),
                  pl.BlockSpec((D, D), const),
                  pl.BlockSpec((LANES, D), const),
                  pl.BlockSpec((LANES, D), const),
                  pl.BlockSpec((LANES, 1), const),
                  pl.BlockSpec((tm, tm), const)],
        out_specs=[pl.BlockSpec((tm, D), row), pl.BlockSpec((tm, D), row),
                   pl.BlockSpec((8, tm), col), pl.BlockSpec((8, tm), col),
                   pl.BlockSpec((N_EXPERTS, LANES), const)],
        out_shape=[jax.ShapeDtypeStruct((T, D), F32), jax.ShapeDtypeStruct((T, D), F32),
                   jax.ShapeDtypeStruct((8, T), I32), jax.ShapeDtypeStruct((8, T), F32),
                   jax.ShapeDtypeStruct((N_EXPERTS, LANES), F32)],
        scratch_shapes=[pltpu.VMEM((tm, D), BF16), pltpu.VMEM((tm, D), BF16),
                        pltpu.VMEM((N_EXPERTS, LANES), F32)],
        compiler_params=_cparams(("arbitrary",)),
        name="merge",
    )(x2, ya, yb, yc, g1, sh1, sc1, gt1, g2, sh2, sc2, gw, gb, bw, ow, rwh, rwl, rb, tri)


def _row_copy(src_ref, src_row, dst_ref, dst_row, sem):
    return pltpu.make_async_copy(src_ref.at[pl.ds(src_row, 1), :], dst_ref.at[pl.ds(dst_row, 1), :], sem)


def _drain_rows(src_ref, dst_ref, sem, n):
    def body(r, carry):
        _row_copy(src_ref, 0, dst_ref, 0, sem).wait()
        return carry
    lax.fori_loop(0, n, body, 0, unroll=8)


def _dispatch_kernel(dest_ref, h_ref, xs_in_ref, xs_ref, dsm_ref, sem_ref, isem_ref):
    del xs_in_ref
    tm = h_ref.shape[0]
    cp = pltpu.make_async_copy(dest_ref.at[0], dsm_ref, isem_ref)
    cp.start()
    cp.wait()

    def issue(r, carry):
        _row_copy(h_ref, r, xs_ref, dsm_ref[0, 2 * r], sem_ref).start(priority=0)
        _row_copy(h_ref, r, xs_ref, dsm_ref[0, 2 * r + 1], sem_ref).start(priority=1)
        return carry

    lax.fori_loop(0, tm, issue, 0, unroll=8)
    _drain_rows(h_ref, xs_ref, sem_ref, 2 * tm)


def _dispatch(h2, dest, P):
    T, D = h2.shape
    tm = _pick(T, 512)
    nt = T // tm
    xs0 = jnp.zeros((P, D), F32)
    return pl.pallas_call(
        _dispatch_kernel,
        grid=(nt,),
        in_specs=[pl.BlockSpec((1, 1, 2 * tm), lambda i: (i, 0, 0)),
                  pl.BlockSpec((tm, D), lambda i: (i, 0)),
                  pl.BlockSpec(memory_space=pl.ANY)],
        out_specs=pl.BlockSpec(memory_space=pl.ANY),
        out_shape=jax.ShapeDtypeStruct((P, D), F32),
        scratch_shapes=[pltpu.SMEM((1, 2 * tm), I32), pltpu.SemaphoreType.DMA,
                        pltpu.SemaphoreType.DMA],
        input_output_aliases={2: 0},
        compiler_params=_cparams(("arbitrary",)),
        name="moe_dispatch",
    )(dest.reshape(nt, 1, 2 * tm), h2, xs0)


def _expert_kernel(blk_e_ref, nvalid_ref, xs_ref, w1_ref, w3_ref, w2_ref, ys_ref):
    del blk_e_ref
    valid = pl.program_id(0) < nvalid_ref[0]

    @pl.when(valid)
    def _():
        xb = xs_ref[...].astype(BF16)
        a = jnp.dot(xb, w1_ref[...], preferred_element_type=F32)
        b = jnp.dot(xb, w3_ref[...], preferred_element_type=F32)
        hid = (a * jax.nn.sigmoid(a) * b).astype(BF16)
        ys_ref[...] = jnp.dot(hid, w2_ref[...], preferred_element_type=F32)

    @pl.when(jnp.logical_not(valid))
    def _():
        ys_ref[...] = jnp.zeros_like(ys_ref)


def _experts(xs, blk_e, nvalid, w1, w3, w2, tb):
    P, D = xs.shape
    n_blocks = P // tb
    rows = lambda i, be, nv: (jnp.minimum(i, nv[0] - 1), 0)
    grid_spec = pltpu.PrefetchScalarGridSpec(
        num_scalar_prefetch=2,
        grid=(n_blocks,),
        in_specs=[pl.BlockSpec((tb, D), rows),
                  pl.BlockSpec((None, D, D_EXPERT), lambda i, be, nv: (be[i], 0, 0)),
                  pl.BlockSpec((None, D, D_EXPERT), lambda i, be, nv: (be[i], 0, 0)),
                  pl.BlockSpec((None, D_EXPERT, D), lambda i, be, nv: (be[i], 0, 0))],
        out_specs=pl.BlockSpec((tb, D), lambda i, be, nv: (i, 0)),
    )
    return pl.pallas_call(
        _expert_kernel,
        grid_spec=grid_spec,
        out_shape=jax.ShapeDtypeStruct((P, D), F32),
        compiler_params=_cparams(("arbitrary",)),
        name="moe_experts",
    )(blk_e, nvalid, xs, w1, w3, w2)


def _combine_kernel(dcur_ref, dnext_ref, x_ref, mf_ref, gt_ref, fg_ref, ys_ref, o_ref,
                    dsm_ref, ybuf_ref, sem_ref, isem_ref, *, final):
    i = pl.program_id(0)
    tm = x_ref.shape[0]
    slot = i % 2

    def fetch(d_ref, s):
        cp = pltpu.make_async_copy(d_ref.at[0], dsm_ref, isem_ref)
        cp.start()
        cp.wait()

        def issue(r, carry):
            _row_copy(ys_ref, dsm_ref[0, 2 * r], ybuf_ref.at[s, 0], r, sem_ref.at[s]).start(priority=0)
            _row_copy(ys_ref, dsm_ref[0, 2 * r + 1], ybuf_ref.at[s, 1], r, sem_ref.at[s]).start(priority=1)
            return carry

        lax.fori_loop(0, tm, issue, 0, unroll=8)

    @pl.when(i == 0)
    def _():
        fetch(dcur_ref, 0)

    @pl.when(i + 1 < pl.num_programs(0))
    def _():
        fetch(dnext_ref, 1 - slot)

    _drain_rows(ys_ref, ybuf_ref.at[slot, 0], sem_ref.at[slot], 2 * tm)

    mf = mf_ref[...]
    ffn = ybuf_ref[slot, 0] * mf[:, 0:1] + ybuf_ref[slot, 1] * mf[:, 1:2]
    out = x_ref[...] + gt_ref[...] * ffn
    if final:
        out = _rms(out, fg_ref[...])
    o_ref[...] = out


def _combine(x1, S, mf, gt2, final_g, ys, dest, final):
    T, D = x1.shape
    tm = _pick(S, 256)
    per_b = S // tm
    nt = T // tm
    dest3 = dest.reshape(nt, 1, 2 * tm)
    return pl.pallas_call(
        functools.partial(_combine_kernel, final=final),
        grid=(nt,),
        in_specs=[pl.BlockSpec((1, 1, 2 * tm), lambda i: (i, 0, 0)),
                  pl.BlockSpec((1, 1, 2 * tm), lambda i: (jnp.minimum(i + 1, nt - 1), 0, 0)),
                  pl.BlockSpec((tm, D), lambda i: (i, 0)),
                  pl.BlockSpec((tm, 2), lambda i: (i, 0)),
                  pl.BlockSpec((None, 1, D), lambda i: (i // per_b, 0, 0)),
                  pl.BlockSpec((1, D), lambda i: (0, 0)),
                  pl.BlockSpec(memory_space=pl.ANY)],
        out_specs=pl.BlockSpec((tm, D), lambda i: (i, 0)),
        out_shape=jax.ShapeDtypeStruct((T, D), F32),
        scratch_shapes=[pltpu.SMEM((1, 2 * tm), I32), pltpu.VMEM((2, 2, tm, D), F32),
                        pltpu.SemaphoreType.DMA((2,)), pltpu.SemaphoreType.DMA],
        compiler_params=_cparams(("arbitrary",)),
        name="moe_combine",
    )(dest3, dest3, x1, mf, gt2, final_g, ys)


def _prep_w_in(w):
    offs = [0]
    for s in (256, 256, 512, 512, 512, 512, 512, FOX_HEADS, MLA_Q_RANK, MLA_KV_RANK, MLA_ROPE):
        offs.append(offs[-1] + s)
    rq, rk, rv, rg, fq, fk, fv, ff, mq, mkv, mkr = [w[:, offs[i]:offs[i + 1]] for i in range(11)]
    pad = jnp.zeros((w.shape[0], LANES - MLA_ROPE - FOX_HEADS), w.dtype)
    return jnp.concatenate([rq, rk, rv, rg, fq, fk, fv, mq, mkv, mkr, ff, pad], axis=1).astype(BF16)


def _prep_wq_up(w):
    r = w.reshape(MLA_Q_RANK, MLA_HEADS, MLA_NOPE + MLA_ROPE)
    r = jnp.pad(r, ((0, 0), (0, 0), (0, MLA_DQ - MLA_NOPE - MLA_ROPE)))
    return r.reshape(MLA_Q_RANK, MLA_HEADS * MLA_DQ).astype(BF16)


def _prep_router(w_grp, b_grp, w_exp, b_exp):
    D = w_grp.shape[0]
    pad = LANES - N_EXPERTS - N_GROUPS
    rwt = jnp.concatenate([w_exp, w_grp, jnp.zeros((D, pad), F32)], axis=1).astype(F32).T
    rwh = rwt.astype(BF16)
    rwl = (rwt - rwh.astype(F32)).astype(BF16)
    rb = jnp.concatenate([b_exp, b_grp, jnp.zeros((pad,), F32)]).astype(F32).reshape(LANES, 1)
    return rwh, rwl, rb


def kernel(x, c, positions, ada_w, ada_b, norm1_g, norm2_g, w_in, fox_fb, mla_q_norm_g, mla_wq_up, mla_kv_norm_g, mla_wkv_up, gate_w, gate_b, branch_w, out_w, router_grp_w, router_grp_b, router_exp_w, router_exp_b, exp_w1, exp_w3, exp_w2, final_g):
    B, S, D = x.shape
    L = ada_w.shape[0]
    T = B * S
    A = 2 * T
    tb = _pick(A, 256)
    n_blocks = A // tb + N_EXPERTS
    P = n_blocks * tb

    mod = _adaln(c, ada_w, ada_b)
    cos_t, sin_t = _rope_tables(positions)
    x2 = x.reshape(T, D)
    final_g2 = final_g.reshape(1, D)

    for l in range(L):
        sh1, sc1, gt1, sh2, sc2, gt2 = [mod[l, :, i * D:(i + 1) * D].reshape(B, 1, D) for i in range(6)]
        g1 = norm1_g[l].reshape(1, D)
        g2 = norm2_g[l].reshape(1, D)
        (rq, rk, rv, rg, fq, fk, fv, mq, mk, kpe, mv, ffp) = _proj(
            x2, S, g1, sh1, sc1, _prep_w_in(w_in[l]), cos_t, sin_t,
            mla_q_norm_g[l].reshape(1, -1), _prep_wq_up(mla_wq_up[l]),
            mla_kv_norm_g[l].reshape(1, -1), mla_wkv_up[l].astype(BF16))

        ya = _retention(rq, rk, rv, rg, B, S).reshape(T, -1)
        kb = _fox_gate(ffp, fox_fb[l], S)
        r3 = lambda a: a.reshape(B, S, -1)
        yb = _flash(r3(fq), r3(fk), r3(kb), r3(fv), B, S, FOX_HEADS, FOX_DH, True, 3, "flash_fox").reshape(T, -1)
        yc = _flash(r3(mq), r3(mk), r3(kpe), r3(mv), B, S, MLA_HEADS, MLA_DQ, False, 0, "flash_mla").reshape(T, -1)

        rwh, rwl, rb = _prep_router(router_grp_w[l], router_grp_b[l], router_exp_w[l], router_exp_b[l])
        x1, h2, mi, mf, cnt = _merge(
            x2, S, ya, yb, yc, g1, sh1, sc1, gt1, g2, sh2, sc2,
            gate_w[l].astype(BF16), gate_b[l].reshape(1, -1), branch_w[l].astype(BF16),
            out_w[l].astype(BF16), rwh, rwl, rb)

        counts = cnt[:, 0].astype(I32)
        pcounts = (counts + tb - 1) // tb * tb
        pends = jnp.cumsum(pcounts)
        pstarts = pends - pcounts
        sel = mi[0:2, :, None] == jnp.arange(N_EXPERTS, dtype=I32)
        dest = (jnp.sum(jnp.where(sel, pstarts, 0), axis=-1) + mi[2:4]).T.reshape(A)
        blk_pos = jnp.arange(n_blocks, dtype=I32) * tb
        blk_e = jnp.minimum(jnp.sum((pends[None, :] <= blk_pos[:, None]).astype(I32), axis=1), N_EXPERTS - 1)
        nvalid = (pends[-1:] // tb).astype(I32)

        xs = _dispatch(h2, dest, P)
        ys = _experts(xs, blk_e, nvalid, exp_w1[l].astype(BF16), exp_w3[l].astype(BF16),
                      exp_w2[l].astype(BF16), tb)
        x2 = _combine(x1, S, mf[0:2].T, gt2, final_g2, ys, dest, final=(l == L - 1))

    return x2.reshape(B, S, D)
```

```python
import functools
import math

import jax
import jax.numpy as jnp
from jax import lax
from jax.experimental import pallas as pl
from jax.experimental.pallas import tpu as pltpu

F32 = jnp.float32
BF16 = jnp.bfloat16
I32 = jnp.int32
U32 = jnp.uint32
HIGHEST = lax.Precision.HIGHEST

EPS = 1e-6
ROPE_THETA = 10000.0
RET_HEADS = 4
RET_DK = 64
RET_DV = 128
RET_CHUNK = 128
FOX_HEADS = 4
FOX_DH = 128
MLA_HEADS = 4
MLA_Q_RANK = 256
MLA_KV_RANK = 128
MLA_NOPE = 128
MLA_ROPE = 64
MLA_V = 128
MLA_DQ = 256
N_BRANCH = 3
BRANCH_W = 512
N_GROUPS = 4
EXP_PER_GROUP = 8
N_EXPERTS = N_GROUPS * EXP_PER_GROUP
D_EXPERT = 512

LANES = 128
V7X_VMEM_LIMIT = 56 * 1024 * 1024

C_RQ, C_RK, C_RV, C_RG = 0, 256, 512, 1024
C_FQ, C_FK, C_FV = 1536, 2048, 2560
C_MQ, C_MKV, C_TAIL = 3072, 3328, 3456
D_IN_PAD = 3584
FF_LANE = MLA_ROPE

NEG_BIG = -1e30
MERGE_COL_CHUNK = 256
FLASH_ROW_BLOCK = 128
LOG2E = math.log2(math.e)


def _cparams(sem):
    return pltpu.CompilerParams(dimension_semantics=sem, vmem_limit_bytes=V7X_VMEM_LIMIT)


def _pick(n, pref):
    t = min(n, pref)
    assert n % t == 0, (n, t)
    return t


def _adaln_kernel(c_ref, w_ref, b_ref, o_ref):
    c = c_ref[...]
    ca = c * jax.nn.sigmoid(c)
    o_ref[...] = jnp.dot(ca, w_ref[...], preferred_element_type=F32, precision=HIGHEST) + b_ref[...]


def _adaln(c, ada_w, ada_b):
    L, D, N = ada_w.shape
    B = c.shape[0]
    tn = _pick(N, 1536)
    return pl.pallas_call(
        _adaln_kernel,
        grid=(L, N // tn),
        in_specs=[pl.BlockSpec((B, D), lambda l, j: (0, 0)),
                  pl.BlockSpec((None, D, tn), lambda l, j: (l, 0, j)),
                  pl.BlockSpec((None, 1, tn), lambda l, j: (l, 0, j))],
        out_specs=pl.BlockSpec((None, B, tn), lambda l, j: (l, 0, j)),
        out_shape=jax.ShapeDtypeStruct((L, B, N), F32),
        compiler_params=_cparams(("arbitrary", "arbitrary")),
        name="adaln",
    )(c, ada_w, ada_b.reshape(L, 1, N))


def _rope_table_kernel(pos_ref, inv_ref, sign_ref, cos_ref, sin_ref):
    ang = pos_ref[...].astype(F32) * inv_ref[...]
    cos_ref[...] = jnp.cos(ang)
    sin_ref[...] = jnp.sin(ang) * sign_ref[...]


def _rope_tables(positions):
    T = positions.size
    tm = _pick(T, 2048)
    half = MLA_ROPE // 2
    inv = ROPE_THETA ** (-jnp.arange(0, MLA_ROPE, 2, dtype=F32) / MLA_ROPE)
    inv_t = jnp.tile(inv, LANES // half).reshape(1, LANES)
    sign = jnp.where((jnp.arange(LANES) % MLA_ROPE) < half, -1.0, 1.0).astype(F32).reshape(1, LANES)
    return pl.pallas_call(
        _rope_table_kernel,
        grid=(T // tm,),
        in_specs=[pl.BlockSpec((tm, 1), lambda i: (i, 0)),
                  pl.BlockSpec((1, LANES), lambda i: (0, 0)),
                  pl.BlockSpec((1, LANES), lambda i: (0, 0))],
        out_specs=[pl.BlockSpec((tm, LANES), lambda i: (i, 0))] * 2,
        out_shape=[jax.ShapeDtypeStruct((T, LANES), F32)] * 2,
        compiler_params=_cparams(("arbitrary",)),
        name="rope_tables",
    )(positions.reshape(T, 1), inv_t, sign)


def _rope_slab(x, cos_t, sin_t, lane):
    nxt = pltpu.roll(x, LANES - 32, axis=1)
    prv = pltpu.roll(x, 32, axis=1)
    swapped = jnp.where((lane & 32) == 0, nxt, prv)
    return x * cos_t + swapped * sin_t


def _pack_bf16_pair(lo, hi):
    lo_bits = lax.shift_right_logical(lax.bitcast_convert_type(lo.astype(BF16).astype(F32), U32), jnp.uint32(16))
    hi_bits = lax.bitcast_convert_type(hi.astype(BF16).astype(F32), U32) & jnp.uint32(0xFFFF0000)
    return hi_bits | lo_bits


def _unpack_bf16_pair(w):
    lo = lax.bitcast_convert_type(lax.shift_left(w, jnp.uint32(16)), F32)
    hi = lax.bitcast_convert_type(w & jnp.uint32(0xFFFF0000), F32)
    return lo, hi


def _norm_mod(x, g, shift, scale):
    y = x * lax.rsqrt(jnp.mean(x * x, axis=-1, keepdims=True) + EPS)
    return (y * g) * (1.0 + scale) + shift


def _rms(x, g):
    return x * lax.rsqrt(jnp.mean(x * x, axis=-1, keepdims=True) + EPS) * g


def _proj_kernel(x_ref, g_ref, sh_ref, sc_ref, w_ref, cos_ref, sin_ref,
                 gq_ref, wq_ref, gkv_ref, wkv_ref,
                 rq_ref, rk_ref, rv_ref, rg_ref, fq_ref, fk_ref, fv_ref,
                 mq_ref, mk_ref, kpe_ref, mv_ref, ff_ref):
    h = _norm_mod(x_ref[...], g_ref[...], sh_ref[...], sc_ref[...]).astype(BF16)
    cos_t = cos_ref[...]
    sin_t = sin_ref[...]
    lane = lax.broadcasted_iota(I32, cos_t.shape, 1)

    def proj(c0, width):
        return jnp.dot(h, w_ref[:, c0:c0 + width], preferred_element_type=F32)

    rq = proj(C_RQ, 256)
    rk = proj(C_RK, 256)
    for s in range(2):
        sl = slice(s * LANES, (s + 1) * LANES)
        rq_ref[:, sl] = _rope_slab(rq[:, sl], cos_t, sin_t, lane).astype(BF16)
        rk_ref[:, sl] = (_rope_slab(rk[:, sl], cos_t, sin_t, lane) * (RET_DK ** -0.5)).astype(BF16)
    rv_ref[...] = proj(C_RV, 512).astype(BF16)
    rg_ref[...] = proj(C_RG, 512).astype(BF16)
    fq_ref[...] = (proj(C_FQ, 512) * (FOX_DH ** -0.5 * LOG2E)).astype(BF16)
    fk_ref[...] = proj(C_FK, 512).astype(BF16)
    fv_ref[...] = proj(C_FV, 512).astype(BF16)

    tail = proj(C_TAIL, LANES)
    ff_ref[...] = tail
    kpe_ref[...] = jnp.where(lane < MLA_ROPE, _rope_slab(tail, cos_t, sin_t, lane), 0.0).astype(BF16)

    qn = _rms(proj(C_MQ, MLA_Q_RANK), gq_ref[...]).astype(BF16)
    qh = jnp.dot(qn, wq_ref[...], preferred_element_type=F32)
    q_scale = (MLA_NOPE + MLA_ROPE) ** -0.5 * LOG2E
    for hd in range(MLA_HEADS):
        c0 = hd * MLA_DQ
        mq_ref[:, c0:c0 + LANES] = (qh[:, c0:c0 + LANES] * q_scale).astype(BF16)
        pe = _rope_slab(qh[:, c0 + LANES:c0 + 2 * LANES], cos_t, sin_t, lane)
        mq_ref[:, c0 + LANES:c0 + 2 * LANES] = jnp.where(lane < MLA_ROPE, pe * q_scale, 0.0).astype(BF16)

    kvn = _rms(proj(C_MKV, MLA_KV_RANK), gkv_ref[...]).astype(BF16)
    kvh = jnp.dot(kvn, wkv_ref[...], preferred_element_type=F32)
    for hd in range(MLA_HEADS):
        c0 = hd * (MLA_NOPE + MLA_V)
        mk_ref[:, hd * MLA_NOPE:(hd + 1) * MLA_NOPE] = kvh[:, c0:c0 + MLA_NOPE].astype(BF16)
        mv_ref[:, hd * MLA_V:(hd + 1) * MLA_V] = kvh[:, c0 + MLA_NOPE:c0 + MLA_NOPE + MLA_V].astype(BF16)


def _proj(x2, S, g, sh, sc, w_all, cos_t, sin_t, gq, wq, gkv, wkv):
    T, D = x2.shape
    tm = _pick(S, 512)
    per_b = S // tm
    row = lambda i: (i, 0)
    const = lambda i: (0, 0)
    batch = lambda i: (i // per_b, 0, 0)
    widths = [256, 256, 512, 512, 512, 512, 512, MLA_HEADS * MLA_DQ, MLA_HEADS * MLA_NOPE, LANES,
              MLA_HEADS * MLA_V]
    out_shape = [jax.ShapeDtypeStruct((T, w), BF16) for w in widths]
    out_shape.append(jax.ShapeDtypeStruct((T, LANES), F32))
    out_specs = [pl.BlockSpec((tm, w), row) for w in widths] + [pl.BlockSpec((tm, LANES), row)]
    return pl.pallas_call(
        _proj_kernel,
        grid=(T // tm,),
        in_specs=[pl.BlockSpec((tm, D), row),
                  pl.BlockSpec((1, D), const),
                  pl.BlockSpec((None, 1, D), batch),
                  pl.BlockSpec((None, 1, D), batch),
                  pl.BlockSpec((D, D_IN_PAD), const),
                  pl.BlockSpec((tm, LANES), row),
                  pl.BlockSpec((tm, LANES), row),
                  pl.BlockSpec((1, MLA_Q_RANK), const),
                  pl.BlockSpec((MLA_Q_RANK, MLA_HEADS * MLA_DQ), const),
                  pl.BlockSpec((1, MLA_KV_RANK), const),
                  pl.BlockSpec((MLA_KV_RANK, MLA_HEADS * (MLA_NOPE + MLA_V)), const)],
        out_specs=out_specs,
        out_shape=out_shape,
        compiler_params=_cparams(("arbitrary",)),
        name="proj",
    )(x2, g, sh, sc, w_all, cos_t, sin_t, gq, wq, gkv, wkv)


def _split3(x):
    a = x.astype(BF16)
    r = x - a.astype(F32)
    b = r.astype(BF16)
    c = (r - b.astype(F32)).astype(BF16)
    return a, b, c


def _fox_gate_kernel(ff_ref, fb_ref, tri_ref, kb_ref, carry_ref, *, tiles_per_seq):
    @pl.when(pl.program_id(0) % tiles_per_seq == 0)
    def _():
        carry_ref[...] = jnp.zeros_like(carry_ref)

    tm = ff_ref.shape[0]
    lane = lax.broadcasted_iota(I32, (tm, LANES), 1)
    z = ff_ref[...] + fb_ref[...]
    ls = -(jnp.maximum(-z, 0.0) + jnp.log1p(jnp.exp(-jnp.abs(z))))
    ls = jnp.where((lane >= FF_LANE) & (lane < FF_LANE + FOX_HEADS), ls, 0.0)
    tri = tri_ref[...]
    f = carry_ref[...]
    for part in _split3(ls):
        f = f + jnp.dot(tri, part, preferred_element_type=F32)
    carry_ref[...] = f[tm - 1:tm, :]
    f2 = f * LOG2E
    for hd in range(FOX_HEADS):
        src = FF_LANE + hd
        g = jnp.where(lane == 0, pltpu.roll(f2, (LANES - src) % LANES, axis=1),
            jnp.where(lane == 1, pltpu.roll(f2, (LANES + 1 - src) % LANES, axis=1),
            jnp.where(lane == 2, pltpu.roll(f2, (LANES + 2 - src) % LANES, axis=1), 0.0)))
        hi, mid, lo = _split3(g)
        kb_ref[:, hd * LANES:(hd + 1) * LANES] = jnp.where(lane == 0, hi, jnp.where(lane == 1, mid, lo))


def _fox_gate(ffp, fb, S):
    T = ffp.shape[0]
    tm = _pick(S, 512)
    idx = jnp.arange(tm)
    tri = (idx[None, :] <= idx[:, None]).astype(BF16)
    fbv = jnp.zeros((1, LANES), F32).at[0, FF_LANE:FF_LANE + FOX_HEADS].set(fb)
    return pl.pallas_call(
        functools.partial(_fox_gate_kernel, tiles_per_seq=S // tm),
        grid=(T // tm,),
        in_specs=[pl.BlockSpec((tm, LANES), lambda i: (i, 0)),
                  pl.BlockSpec((1, LANES), lambda i: (0, 0)),
                  pl.BlockSpec((tm, tm), lambda i: (0, 0))],
        out_specs=pl.BlockSpec((tm, FOX_HEADS * LANES), lambda i: (i, 0)),
        out_shape=jax.ShapeDtypeStruct((T, FOX_HEADS * LANES), BF16),
        scratch_shapes=[pltpu.VMEM((1, LANES), F32)],
        compiler_params=_cparams(("arbitrary",)),
        name="fox_gate",
    )(ffp, fbv, tri)


def _retention_kernel(dchunk_ref, q_ref, k_ref, v_ref, g_ref, dmask_ref, din_ref, dout_ref,
                      o_ref, state_ref, *, n_chunks):
    @pl.when(pl.program_id(1) == 0)
    def _():
        state_ref[...] = jnp.zeros_like(state_ref)

    C = RET_CHUNK
    lane = lax.broadcasted_iota(I32, (C, LANES), 1)
    for ci in range(n_chunks):
        rows = slice(ci * C, (ci + 1) * C)
        for hd in range(RET_HEADS):
            slab = slice((hd // 2) * LANES, (hd // 2 + 1) * LANES)
            mine = (lane < RET_DK) if hd % 2 == 0 else (lane >= RET_DK)
            q = jnp.where(mine, q_ref[rows, slab], 0)
            k = jnp.where(mine, k_ref[rows, slab], 0)
            vcols = slice(hd * RET_DV, (hd + 1) * RET_DV)
            v = v_ref[rows, vcols]
            state = state_ref[hd]
            scores = lax.dot_general(q, k, (((1,), (1,)), ((), ())),
                                     preferred_element_type=F32) * dmask_ref[hd]
            inner = jnp.dot(scores.astype(BF16), v, preferred_element_type=F32)
            cross = jnp.dot(q, state.astype(BF16), preferred_element_type=F32) * din_ref[hd]
            o = inner + cross
            vd = (v.astype(F32) * dout_ref[hd]).astype(BF16)
            kv = lax.dot_general(k, vd, (((0,), (0,)), ((), ())), preferred_element_type=F32)
            state_ref[hd] = state * dchunk_ref[hd] + kv
            mu = jnp.mean(o, axis=-1, keepdims=True)
            d = o - mu
            var = jnp.mean(d * d, axis=-1, keepdims=True)
            on = d * lax.rsqrt(var + EPS)
            g = g_ref[rows, vcols].astype(F32)
            o_ref[rows, vcols] = (g * jax.nn.sigmoid(g) * on).astype(BF16)


def _retention(rq, rk, rv, rg, B, S):
    H, C = RET_HEADS, RET_CHUNK
    tr = _pick(S, 4 * C)
    n_chunks = tr // C
    log_gamma = jnp.log1p(-jnp.exp2(-5.0 - jnp.arange(H, dtype=F32)))
    idx = jnp.arange(C, dtype=F32)
    rel = idx[:, None] - idx[None, :]
    dmask = jnp.where(rel >= 0, jnp.exp(log_gamma[:, None, None] * jnp.maximum(rel, 0.0)), 0.0)
    decay_in = jnp.exp(log_gamma[:, None] * (idx + 1.0))
    decay_out = jnp.exp(log_gamma[:, None] * (C - 1.0 - idx))
    decay_chunk = jnp.exp(log_gamma * C)
    din = jnp.broadcast_to(decay_in[:, :, None], (H, C, RET_DV))
    dout = jnp.broadcast_to(decay_out[:, :, None], (H, C, RET_DV))
    tok = lambda b, i: (b, i, 0)
    const3 = lambda b, i: (0, 0, 0)
    return pl.pallas_call(
        functools.partial(_retention_kernel, n_chunks=n_chunks),
        grid=(B, S // tr),
        in_specs=[pl.BlockSpec(memory_space=pltpu.SMEM),
                  pl.BlockSpec((None, tr, H * RET_DK), tok),
                  pl.BlockSpec((None, tr, H * RET_DK), tok),
                  pl.BlockSpec((None, tr, H * RET_DV), tok),
                  pl.BlockSpec((None, tr, H * RET_DV), tok),
                  pl.BlockSpec((H, C, C), const3),
                  pl.BlockSpec((H, C, RET_DV), const3),
                  pl.BlockSpec((H, C, RET_DV), const3)],
        out_specs=pl.BlockSpec((None, tr, H * RET_DV), tok),
        out_shape=jax.ShapeDtypeStruct((B, S, H * RET_DV), BF16),
        scratch_shapes=[pltpu.VMEM((H, LANES, RET_DV), F32)],
        compiler_params=_cparams(("arbitrary", "arbitrary")),
        name="retention",
    )(decay_chunk, rq.reshape(B, S, -1), rk.reshape(B, S, -1), rv.reshape(B, S, -1),
      rg.reshape(B, S, -1), dmask, din, dout)


def _flash_kernel(q_ref, k_ref, ke_ref, v_ref, o_ref, m_ref, l_ref, acc_ref, s_ref, p_ref, *, q_bias_cols):
    qi = pl.program_id(2)
    tq = q_ref.shape[0]
    tk = tq
    reps = tk // LANES
    m_ref[...] = jnp.full(m_ref.shape, NEG_BIG, F32)
    l_ref[...] = jnp.zeros_like(l_ref)
    acc_ref[...] = jnp.zeros_like(acc_ref)
    q = q_ref[...]
    if q_bias_cols:
        lane = lax.broadcasted_iota(I32, (tq, LANES), 1)
        q = jnp.concatenate([q, jnp.where(lane < q_bias_cols, -1.0, 0.0).astype(BF16)], axis=1)

    def scores(j):
        start = pl.multiple_of(j * tk, tk)
        kj = jnp.concatenate([k_ref[pl.ds(start, tk), :], ke_ref[pl.ds(start, tk), :]], axis=1)
        return lax.dot_general(q, kj, (((1,), (1,)), ((), ())), preferred_element_type=F32)

    def update(slot, j, diag):
        start = pl.multiple_of(j * tk, tk)
        for rb in range(tq // FLASH_ROW_BLOCK):
            rows = slice(rb * FLASH_ROW_BLOCK, (rb + 1) * FLASH_ROW_BLOCK)
            s = s_ref[slot, rows, :]
            if diag:
                r = lax.broadcasted_iota(I32, s.shape, 0) + rb * FLASH_ROW_BLOCK
                c = lax.broadcasted_iota(I32, s.shape, 1)
                s = jnp.where(c <= r, s, NEG_BIG)
            m_prev = m_ref[rows, :]
            m_new = jnp.maximum(m_prev, jnp.max(s, axis=-1, keepdims=True))
            alpha = jnp.exp2(m_prev - m_new)
            p = jnp.exp2(s - jnp.concatenate([m_new] * reps, axis=1))
            l_ref[rows, :] = alpha * l_ref[rows, :] + jnp.sum(p, axis=-1, keepdims=True)
            acc_ref[rows, :] = alpha * acc_ref[rows, :]
            m_ref[rows, :] = m_new
            p_ref[rows, :] = p.astype(BF16)
        acc_ref[...] += jnp.dot(p_ref[...], v_ref[pl.ds(start, tk), :], preferred_element_type=F32)

    s_ref[0] = scores(0)

    def pair(p, carry):
        j = 2 * p
        s_ref[1] = scores(j + 1)
        update(0, j, False)
        s_ref[0] = scores(j + 2)
        update(1, j + 1, False)
        return carry

    lax.fori_loop(0, qi // 2, pair, 0)

    @pl.when(qi % 2 == 1)
    def _():
        s_ref[1] = scores(qi)
        update(0, qi - 1, False)
        update(1, qi, True)

    @pl.when(qi % 2 == 0)
    def _():
        update(0, qi, True)

    o_ref[...] = (acc_ref[...] / l_ref[...]).astype(o_ref.dtype)


def _flash(q, k, ke, v, B, S, H, dq, ke_per_head, q_bias_cols, name):
    tq = _pick(S, 512)
    nq = S // tq
    ke_map = (lambda b, h, i: (b, 0, h)) if ke_per_head else (lambda b, h, i: (b, 0, 0))
    return pl.pallas_call(
        functools.partial(_flash_kernel, q_bias_cols=q_bias_cols),
        grid=(B, H, nq),
        in_specs=[pl.BlockSpec((None, tq, dq), lambda b, h, i: (b, i, h)),
                  pl.BlockSpec((None, S, LANES), lambda b, h, i: (b, 0, h)),
                  pl.BlockSpec((None, S, LANES), ke_map),
                  pl.BlockSpec((None, S, LANES), lambda b, h, i: (b, 0, h))],
        out_specs=pl.BlockSpec((None, tq, LANES), lambda b, h, i: (b, i, h)),
        out_shape=jax.ShapeDtypeStruct((B, S, H * LANES), BF16),
        scratch_shapes=[pltpu.VMEM((tq, LANES), F32), pltpu.VMEM((tq, LANES), F32),
                        pltpu.VMEM((tq, LANES), F32), pltpu.VMEM((2, tq, tq), F32),
                        pltpu.VMEM((tq, tq), BF16)],
        compiler_params=_cparams(("arbitrary", "arbitrary", "arbitrary")),
        name=name,
    )(q, k, ke, v)


def _merge_kernel(x_ref, ya_ref, yb_ref, yc_ref, g1_ref, sh1_ref, sc1_ref, gt1_ref,
                  g2_ref, sh2_ref, sc2_ref, gw_ref, gb_ref, bw_ref, ow_ref,
                  rwh_ref, rwl_ref, rb_ref, tri_ref,
                  x1_ref, h2_ref, mi_ref, mf_ref, cnt_ref, h_ref, mg_ref, carry_ref):
    tm, D = x_ref.shape
    cn = MERGE_COL_CHUNK

    @pl.when(pl.program_id(0) == 0)
    def _():
        carry_ref[...] = jnp.zeros_like(carry_ref)

    h_ref[...] = _norm_mod(x_ref[...], g1_ref[...], sh1_ref[...], sc1_ref[...]).astype(BF16)
    for n in range(D // cn):
        cols = slice(n * cn, (n + 1) * cn)
        merged = None
        for i, y_ref in enumerate((ya_ref, yb_ref, yc_ref)):
            gcols = slice(i * D + n * cn, i * D + (n + 1) * cn)
            gate = jax.nn.sigmoid(jnp.dot(h_ref[...], gw_ref[:, gcols], preferred_element_type=F32)
                                  + gb_ref[:, gcols])
            br = jnp.dot(y_ref[...], bw_ref[i, :, cols], preferred_element_type=F32)
            merged = gate * br if merged is None else merged + gate * br
        mg_ref[:, cols] = merged.astype(BF16)
    for n in range(D // cn):
        cols = slice(n * cn, (n + 1) * cn)
        mix = jnp.dot(mg_ref[...], ow_ref[:, cols], preferred_element_type=F32)
        x1_ref[:, cols] = x_ref[:, cols] + gt1_ref[:, cols] * mix
    h2 = _norm_mod(x1_ref[...], g2_ref[...], sh2_ref[...], sc2_ref[...])
    h2_ref[...] = _pack_bf16_pair(h2[:, :D // 2], h2[:, D // 2:])

    hh = h2.astype(BF16)
    hl = (h2 - hh.astype(F32)).astype(BF16)
    nt = lambda a, b: lax.dot_general(a, b, (((1,), (1,)), ((), ())), preferred_element_type=F32)
    lt = nt(rwh_ref[...], hh) + nt(rwh_ref[...], hl) + nt(rwl_ref[...], hh) + rb_ref[...]
    row8 = lax.broadcasted_iota(I32, (EXP_PER_GROUP, tm), 0)
    gl = jnp.where(row8 < N_GROUPS, lt[N_EXPERTS:N_EXPERTS + EXP_PER_GROUP, :], -jnp.inf)
    gmax = jnp.max(gl, axis=0, keepdims=True)
    g_idx = jnp.min(jnp.where(gl == gmax, row8, EXP_PER_GROUP), axis=0, keepdims=True)
    g_w = 1.0 / jnp.sum(jnp.exp(gl - gmax), axis=0, keepdims=True)
    el = lt[(N_GROUPS - 1) * EXP_PER_GROUP:N_EXPERTS, :]
    for g in range(N_GROUPS - 2, -1, -1):
        el = jnp.where(g_idx == g, lt[g * EXP_PER_GROUP:(g + 1) * EXP_PER_GROUP, :], el)
    e1 = jnp.max(el, axis=0, keepdims=True)
    i1 = jnp.min(jnp.where(el == e1, row8, EXP_PER_GROUP), axis=0, keepdims=True)
    el2 = jnp.where(row8 == i1, -jnp.inf, el)
    e2 = jnp.max(el2, axis=0, keepdims=True)
    i2 = jnp.min(jnp.where(el2 == e2, row8, EXP_PER_GROUP), axis=0, keepdims=True)
    r = jnp.exp(e2 - e1)
    w1 = g_w / (1.0 + r)
    w2 = g_w * r / (1.0 + r)
    eid1 = g_idx * EXP_PER_GROUP + i1
    eid2 = g_idx * EXP_PER_GROUP + i2

    rowe = lax.broadcasted_iota(I32, (N_EXPERTS, tm), 0)
    hit1 = rowe == eid1
    hit2 = rowe == eid2
    onehot = jnp.where(hit1 | hit2, 1.0, 0.0)
    before = (jnp.dot(onehot.astype(BF16), tri_ref[...], preferred_element_type=F32)
              + jnp.concatenate([carry_ref[...]] * (tm // LANES), axis=1))
    rank1 = jnp.sum(jnp.where(hit1, before, 0.0), axis=0, keepdims=True)
    rank2 = jnp.sum(jnp.where(hit2, before, 0.0), axis=0, keepdims=True)
    carry_ref[...] = carry_ref[...] + jnp.sum(onehot, axis=1, keepdims=True)
    cnt_ref[...] = carry_ref[...]

    mi_ref[...] = jnp.where(row8 == 0, eid1,
                  jnp.where(row8 == 1, eid2,
                  jnp.where(row8 == 2, rank1.astype(I32),
                  jnp.where(row8 == 3, rank2.astype(I32), 0))))
    mf_ref[...] = jnp.where(row8 == 0, w1, jnp.where(row8 == 1, w2, 0.0))


def _merge(x2, S, ya, yb, yc, g1, sh1, sc1, gt1, g2, sh2, sc2, gw, gb, bw, ow, rwh, rwl, rb):
    T, D = x2.shape
    tm = _pick(S, 512)
    per_b = S // tm
    row = lambda i: (i, 0)
    col = lambda i: (0, i)
    const = lambda i: (0, 0)
    batch = lambda i: (i // per_b, 0, 0)
    idx = jnp.arange(tm)
    tri = (idx[:, None] < idx[None, :]).astype(BF16)
    vecb = pl.BlockSpec((None, 1, D), batch)
    return pl.pallas_call(
        _merge_kernel,
        grid=(T // tm,),
        in_specs=[pl.BlockSpec((tm, D), row),
                  pl.BlockSpec((tm, BRANCH_W), row),
                  pl.BlockSpec((tm, BRANCH_W), row),
                  pl.BlockSpec((tm, BRANCH_W), row),
                  pl.BlockSpec((1, D), const), vecb, vecb, vecb,
                  pl.BlockSpec((1, D), const), vecb, vecb,
                  pl.BlockSpec((D, N_BRANCH * D), const),
                  pl.BlockSpec((1, N_BRANCH * D), const),
                  pl.BlockSpec((N_BRANCH, BRANCH_W, D), lambda i: (0, 0, 0)),
                  pl.BlockSpec((D, D), const),
                  pl.BlockSpec((LANES, D), const),
                  pl.BlockSpec((LANES, D), const),
                  pl.BlockSpec((LANES, 1), const),
                  pl.BlockSpec((tm, tm), const)],
        out_specs=[pl.BlockSpec((tm, D), row), pl.BlockSpec((tm, D // 2), row),
                   pl.BlockSpec((8, tm), col), pl.BlockSpec((8, tm), col),
                   pl.BlockSpec((N_EXPERTS, LANES), const)],
        out_shape=[jax.ShapeDtypeStruct((T, D), F32), jax.ShapeDtypeStruct((T, D // 2), U32),
                   jax.ShapeDtypeStruct((8, T), I32), jax.ShapeDtypeStruct((8, T), F32),
                   jax.ShapeDtypeStruct((N_EXPERTS, LANES), F32)],
        scratch_shapes=[pltpu.VMEM((tm, D), BF16), pltpu.VMEM((tm, D), BF16),
                        pltpu.VMEM((N_EXPERTS, LANES), F32)],
        compiler_params=_cparams(("arbitrary",)),
        name="merge",
    )(x2, ya, yb, yc, g1, sh1, sc1, gt1, g2, sh2, sc2, gw, gb, bw, ow, rwh, rwl, rb, tri)


def _row_copy(src_ref, src_row, dst_ref, dst_row, sem):
    return pltpu.make_async_copy(src_ref.at[pl.ds(src_row, 1), :], dst_ref.at[pl.ds(dst_row, 1), :], sem)


def _drain_rows(src_ref, dst_ref, sem, n):
    def body(r, carry):
        _row_copy(src_ref, 0, dst_ref, 0, sem).wait()
        return carry
    lax.fori_loop(0, n, body, 0, unroll=8)


def _dispatch_kernel(dest_ref, h_ref, xs_in_ref, xs_ref, dsm_ref, sem_ref, isem_ref):
    del xs_in_ref
    tm = h_ref.shape[0]
    cp = pltpu.make_async_copy(dest_ref.at[0], dsm_ref, isem_ref)
    cp.start()
    cp.wait()

    def issue(r, carry):
        _row_copy(h_ref, r, xs_ref, dsm_ref[0, 2 * r], sem_ref).start(priority=0)
        _row_copy(h_ref, r, xs_ref, dsm_ref[0, 2 * r + 1], sem_ref).start(priority=1)
        return carry

    lax.fori_loop(0, tm, issue, 0, unroll=8)
    _drain_rows(h_ref, xs_ref, sem_ref, 2 * tm)


def _dispatch(h2, dest, P):
    T, D = h2.shape
    tm = _pick(T, 512)
    nt = T // tm
    xs0 = jnp.zeros((P, D), h2.dtype)
    return pl.pallas_call(
        _dispatch_kernel,
        grid=(nt,),
        in_specs=[pl.BlockSpec((1, 1, 2 * tm), lambda i: (i, 0, 0)),
                  pl.BlockSpec((tm, D), lambda i: (i, 0)),
                  pl.BlockSpec(memory_space=pl.ANY)],
        out_specs=pl.BlockSpec(memory_space=pl.ANY),
        out_shape=jax.ShapeDtypeStruct((P, D), h2.dtype),
        scratch_shapes=[pltpu.SMEM((1, 2 * tm), I32), pltpu.SemaphoreType.DMA,
                        pltpu.SemaphoreType.DMA],
        input_output_aliases={2: 0},
        compiler_params=_cparams(("arbitrary",)),
        name="moe_dispatch",
    )(dest.reshape(nt, 1, 2 * tm), h2, xs0)


def _expert_kernel(blk_e_ref, nvalid_ref, xs_ref, w1_ref, w3_ref, w2_ref, ys_ref):
    del blk_e_ref
    valid = pl.program_id(0) < nvalid_ref[0]

    @pl.when(valid)
    def _():
        half = xs_ref.shape[1]
        x_lo, x_hi = _unpack_bf16_pair(xs_ref[...])
        x_lo = x_lo.astype(BF16)
        x_hi = x_hi.astype(BF16)

        def up(w_ref):
            return (jnp.dot(x_lo, w_ref[:half, :], preferred_element_type=F32)
                    + jnp.dot(x_hi, w_ref[half:, :], preferred_element_type=F32))

        a = up(w1_ref)
        b = up(w3_ref)
        hid = (a * jax.nn.sigmoid(a) * b).astype(BF16)
        y = jnp.dot(hid, w2_ref[...], preferred_element_type=F32)
        ys_ref[...] = _pack_bf16_pair(y[:, :half], y[:, half:])

    @pl.when(jnp.logical_not(valid))
    def _():
        ys_ref[...] = jnp.zeros_like(ys_ref)


def _experts(xs, blk_e, nvalid, w1, w3, w2, tb):
    P, half = xs.shape
    D = 2 * half
    n_blocks = P // tb
    rows = lambda i, be, nv: (jnp.minimum(i, nv[0] - 1), 0)
    grid_spec = pltpu.PrefetchScalarGridSpec(
        num_scalar_prefetch=2,
        grid=(n_blocks,),
        in_specs=[pl.BlockSpec((tb, half), rows),
                  pl.BlockSpec((None, D, D_EXPERT), lambda i, be, nv: (be[i], 0, 0)),
                  pl.BlockSpec((None, D, D_EXPERT), lambda i, be, nv: (be[i], 0, 0)),
                  pl.BlockSpec((None, D_EXPERT, D), lambda i, be, nv: (be[i], 0, 0))],
        out_specs=pl.BlockSpec((tb, half), lambda i, be, nv: (i, 0)),
    )
    return pl.pallas_call(
        _expert_kernel,
        grid_spec=grid_spec,
        out_shape=jax.ShapeDtypeStruct((P, half), U32),
        compiler_params=_cparams(("arbitrary",)),
        name="moe_experts",
    )(blk_e, nvalid, xs, w1, w3, w2)


def _combine_kernel(dcur_ref, dnext_ref, x_ref, mf_ref, gt_ref, fg_ref, ys_ref, o_ref,
                    dsm_ref, ybuf_ref, sem_ref, isem_ref, *, final):
    i = pl.program_id(0)
    tm = x_ref.shape[0]
    slot = i % 2

    def fetch(d_ref, s):
        cp = pltpu.make_async_copy(d_ref.at[0], dsm_ref, isem_ref)
        cp.start()
        cp.wait()

        def issue(r, carry):
            _row_copy(ys_ref, dsm_ref[0, 2 * r], ybuf_ref.at[s, 0], r, sem_ref.at[s]).start(priority=0)
            _row_copy(ys_ref, dsm_ref[0, 2 * r + 1], ybuf_ref.at[s, 1], r, sem_ref.at[s]).start(priority=1)
            return carry

        lax.fori_loop(0, tm, issue, 0, unroll=8)

    @pl.when(i == 0)
    def _():
        fetch(dcur_ref, 0)

    @pl.when(i + 1 < pl.num_programs(0))
    def _():
        fetch(dnext_ref, 1 - slot)

    _drain_rows(ys_ref, ybuf_ref.at[slot, 0], sem_ref.at[slot], 2 * tm)

    mf = mf_ref[...]
    lo0, hi0 = _unpack_bf16_pair(ybuf_ref[slot, 0])
    lo1, hi1 = _unpack_bf16_pair(ybuf_ref[slot, 1])
    w0 = mf[:, 0:1]
    w1 = mf[:, 1:2]
    ffn = jnp.concatenate([lo0 * w0 + lo1 * w1, hi0 * w0 + hi1 * w1], axis=1)
    out = x_ref[...] + gt_ref[...] * ffn
    if final:
        out = _rms(out, fg_ref[...])
    o_ref[...] = out


def _combine(x1, S, mf, gt2, final_g, ys, dest, final):
    T, D = x1.shape
    tm = _pick(S, 256)
    per_b = S // tm
    nt = T // tm
    dest3 = dest.reshape(nt, 1, 2 * tm)
    return pl.pallas_call(
        functools.partial(_combine_kernel, final=final),
        grid=(nt,),
        in_specs=[pl.BlockSpec((1, 1, 2 * tm), lambda i: (i, 0, 0)),
                  pl.BlockSpec((1, 1, 2 * tm), lambda i: (jnp.minimum(i + 1, nt - 1), 0, 0)),
                  pl.BlockSpec((tm, D), lambda i: (i, 0)),
                  pl.BlockSpec((tm, 2), lambda i: (i, 0)),
                  pl.BlockSpec((None, 1, D), lambda i: (i // per_b, 0, 0)),
                  pl.BlockSpec((1, D), lambda i: (0, 0)),
                  pl.BlockSpec(memory_space=pl.ANY)],
        out_specs=pl.BlockSpec((tm, D), lambda i: (i, 0)),
        out_shape=jax.ShapeDtypeStruct((T, D), F32),
        scratch_shapes=[pltpu.SMEM((1, 2 * tm), I32), pltpu.VMEM((2, 2, tm, D // 2), U32),
                        pltpu.SemaphoreType.DMA((2,)), pltpu.SemaphoreType.DMA],
        compiler_params=_cparams(("arbitrary",)),
        name="moe_combine",
    )(dest3, dest3, x1, mf, gt2, final_g, ys)


def _prep_w_in(w):
    offs = [0]
    for s in (256, 256, 512, 512, 512, 512, 512, FOX_HEADS, MLA_Q_RANK, MLA_KV_RANK, MLA_ROPE):
        offs.append(offs[-1] + s)
    rq, rk, rv, rg, fq, fk, fv, ff, mq, mkv, mkr = [w[:, offs[i]:offs[i + 1]] for i in range(11)]
    pad = jnp.zeros((w.shape[0], LANES - MLA_ROPE - FOX_HEADS), w.dtype)
    return jnp.concatenate([rq, rk, rv, rg, fq, fk, fv, mq, mkv, mkr, ff, pad], axis=1).astype(BF16)


def _prep_wq_up(w):
    r = w.reshape(MLA_Q_RANK, MLA_HEADS, MLA_NOPE + MLA_ROPE)
    r = jnp.pad(r, ((0, 0), (0, 0), (0, MLA_DQ - MLA_NOPE - MLA_ROPE)))
    return r.reshape(MLA_Q_RANK, MLA_HEADS * MLA_DQ).astype(BF16)


def _prep_router(w_grp, b_grp, w_exp, b_exp):
    D = w_grp.shape[0]
    pad = LANES - N_EXPERTS - N_GROUPS
    rwt = jnp.concatenate([w_exp, w_grp, jnp.zeros((D, pad), F32)], axis=1).astype(F32).T
    rwh = rwt.astype(BF16)
    rwl = (rwt - rwh.astype(F32)).astype(BF16)
    rb = jnp.concatenate([b_exp, b_grp, jnp.zeros((pad,), F32)]).astype(F32).reshape(LANES, 1)
    return rwh, rwl, rb


def kernel(x, c, positions, ada_w, ada_b, norm1_g, norm2_g, w_in, fox_fb, mla_q_norm_g, mla_wq_up, mla_kv_norm_g, mla_wkv_up, gate_w, gate_b, branch_w, out_w, router_grp_w, router_grp_b, router_exp_w, router_exp_b, exp_w1, exp_w3, exp_w2, final_g):
    B, S, D = x.shape
    L = ada_w.shape[0]
    T = B * S
    A = 2 * T
    tb = _pick(A, 256)
    n_blocks = A // tb + N_EXPERTS
    P = n_blocks * tb

    mod = _adaln(c, ada_w, ada_b)
    cos_t, sin_t = _rope_tables(positions)
    x2 = x.reshape(T, D)
    final_g2 = final_g.reshape(1, D)

    for l in range(L):
        sh1, sc1, gt1, sh2, sc2, gt2 = [mod[l, :, i * D:(i + 1) * D].reshape(B, 1, D) for i in range(6)]
        g1 = norm1_g[l].reshape(1, D)
        g2 = norm2_g[l].reshape(1, D)
        (rq, rk, rv, rg, fq, fk, fv, mq, mk, kpe, mv, ffp) = _proj(
            x2, S, g1, sh1, sc1, _prep_w_in(w_in[l]), cos_t, sin_t,
            mla_q_norm_g[l].reshape(1, -1), _prep_wq_up(mla_wq_up[l]),
            mla_kv_norm_g[l].reshape(1, -1), mla_wkv_up[l].astype(BF16))

        ya = _retention(rq, rk, rv, rg, B, S).reshape(T, -1)
        kb = _fox_gate(ffp, fox_fb[l], S)
        r3 = lambda a: a.reshape(B, S, -1)
        yb = _flash(r3(fq), r3(fk), r3(kb), r3(fv), B, S, FOX_HEADS, FOX_DH, True, 3, "flash_fox").reshape(T, -1)
        yc = _flash(r3(mq), r3(mk), r3(kpe), r3(mv), B, S, MLA_HEADS, MLA_DQ, False, 0, "flash_mla").reshape(T, -1)

        rwh, rwl, rb = _prep_router(router_grp_w[l], router_grp_b[l], router_exp_w[l], router_exp_b[l])
        x1, h2, mi, mf, cnt = _merge(
            x2, S, ya, yb, yc, g1, sh1, sc1, gt1, g2, sh2, sc2,
            gate_w[l].astype(BF16), gate_b[l].reshape(1, -1), branch_w[l].astype(BF16),
            out_w[l].astype(BF16), rwh, rwl, rb)

        counts = cnt[:, 0].astype(I32)
        pcounts = (counts + tb - 1) // tb * tb
        pends = jnp.cumsum(pcounts)
        pstarts = pends - pcounts
        sel = mi[0:2, :, None] == jnp.arange(N_EXPERTS, dtype=I32)
        dest = (jnp.sum(jnp.where(sel, pstarts, 0), axis=-1) + mi[2:4]).T.reshape(A)
        blk_pos = jnp.arange(n_blocks, dtype=I32) * tb
        blk_e = jnp.minimum(jnp.sum((pends[None, :] <= blk_pos[:, None]).astype(I32), axis=1), N_EXPERTS - 1)
        nvalid = (pends[-1:] // tb).astype(I32)

        xs = _dispatch(h2, dest, P)
        ys = _experts(xs, blk_e, nvalid, exp_w1[l].astype(BF16), exp_w3[l].astype(BF16),
                      exp_w2[l].astype(BF16), tb)
        x2 = _combine(x1, S, mf[0:2].T, gt2, final_g2, ys, dest, final=(l == L - 1))

    return x2.reshape(B, S, D)
```

```python
import functools
import math

import jax
import jax.numpy as jnp
from jax import lax
from jax.experimental import pallas as pl
from jax.experimental.pallas import tpu as pltpu
from jax.experimental.pallas import tpu_sc as plsc

F32 = jnp.float32
BF16 = jnp.bfloat16
I32 = jnp.int32
U32 = jnp.uint32
HIGHEST = lax.Precision.HIGHEST

EPS = 1e-6
ROPE_THETA = 10000.0
RET_HEADS = 4
RET_DK = 64
RET_DV = 128
RET_CHUNK = 128
FOX_HEADS = 4
FOX_DH = 128
MLA_HEADS = 4
MLA_Q_RANK = 256
MLA_KV_RANK = 128
MLA_NOPE = 128
MLA_ROPE = 64
MLA_V = 128
MLA_DQ = 256
N_BRANCH = 3
BRANCH_W = 512
N_GROUPS = 4
EXP_PER_GROUP = 8
N_EXPERTS = N_GROUPS * EXP_PER_GROUP
D_EXPERT = 512

LANES = 128
V7X_VMEM_LIMIT = 56 * 1024 * 1024

C_RQ, C_RK, C_RV, C_RG = 0, 256, 512, 1024
C_FQ, C_FK, C_FV = 1536, 2048, 2560
C_MQ, C_MKV, C_TAIL = 3072, 3328, 3456
D_IN_PAD = 3584
FF_LANE = MLA_ROPE

NEG_BIG = -1e30
SC_GATHER_WINDOW = 128
MERGE_COL_CHUNK = 256
FLASH_ROW_BLOCK = 128
LOG2E = math.log2(math.e)


def _cparams(sem):
    return pltpu.CompilerParams(dimension_semantics=sem, vmem_limit_bytes=V7X_VMEM_LIMIT)


def _pick(n, pref):
    t = min(n, pref)
    assert n % t == 0, (n, t)
    return t


def _adaln_kernel(c_ref, w_ref, b_ref, o_ref):
    c = c_ref[...]
    ca = c * jax.nn.sigmoid(c)
    o_ref[...] = jnp.dot(ca, w_ref[...], preferred_element_type=F32, precision=HIGHEST) + b_ref[...]


def _adaln(c, ada_w, ada_b):
    L, D, N = ada_w.shape
    B = c.shape[0]
    tn = _pick(N, 1536)
    return pl.pallas_call(
        _adaln_kernel,
        grid=(L, N // tn),
        in_specs=[pl.BlockSpec((B, D), lambda l, j: (0, 0)),
                  pl.BlockSpec((None, D, tn), lambda l, j: (l, 0, j)),
                  pl.BlockSpec((None, 1, tn), lambda l, j: (l, 0, j))],
        out_specs=pl.BlockSpec((None, B, tn), lambda l, j: (l, 0, j)),
        out_shape=jax.ShapeDtypeStruct((L, B, N), F32),
        compiler_params=_cparams(("arbitrary", "arbitrary")),
        name="adaln",
    )(c, ada_w, ada_b.reshape(L, 1, N))


def _rope_table_kernel(pos_ref, inv_ref, sign_ref, cos_ref, sin_ref):
    ang = pos_ref[...].astype(F32) * inv_ref[...]
    cos_ref[...] = jnp.cos(ang)
    sin_ref[...] = jnp.sin(ang) * sign_ref[...]


def _rope_tables(positions):
    T = positions.size
    tm = _pick(T, 2048)
    half = MLA_ROPE // 2
    inv = ROPE_THETA ** (-jnp.arange(0, MLA_ROPE, 2, dtype=F32) / MLA_ROPE)
    inv_t = jnp.tile(inv, LANES // half).reshape(1, LANES)
    sign = jnp.where((jnp.arange(LANES) % MLA_ROPE) < half, -1.0, 1.0).astype(F32).reshape(1, LANES)
    return pl.pallas_call(
        _rope_table_kernel,
        grid=(T // tm,),
        in_specs=[pl.BlockSpec((tm, 1), lambda i: (i, 0)),
                  pl.BlockSpec((1, LANES), lambda i: (0, 0)),
                  pl.BlockSpec((1, LANES), lambda i: (0, 0))],
        out_specs=[pl.BlockSpec((tm, LANES), lambda i: (i, 0))] * 2,
        out_shape=[jax.ShapeDtypeStruct((T, LANES), F32)] * 2,
        compiler_params=_cparams(("arbitrary",)),
        name="rope_tables",
    )(positions.reshape(T, 1), inv_t, sign)


def _rope_slab(x, cos_t, sin_t, lane):
    nxt = pltpu.roll(x, LANES - 32, axis=1)
    prv = pltpu.roll(x, 32, axis=1)
    swapped = jnp.where((lane & 32) == 0, nxt, prv)
    return x * cos_t + swapped * sin_t


def _pack_bf16_pair(lo, hi):
    lo_bits = lax.shift_right_logical(lax.bitcast_convert_type(lo.astype(BF16).astype(F32), U32), jnp.uint32(16))
    hi_bits = lax.bitcast_convert_type(hi.astype(BF16).astype(F32), U32) & jnp.uint32(0xFFFF0000)
    return hi_bits | lo_bits


def _unpack_bf16_pair(w):
    lo = lax.bitcast_convert_type(lax.shift_left(w, jnp.uint32(16)), F32)
    hi = lax.bitcast_convert_type(w & jnp.uint32(0xFFFF0000), F32)
    return lo, hi


def _norm_mod(x, g, shift, scale):
    y = x * lax.rsqrt(jnp.mean(x * x, axis=-1, keepdims=True) + EPS)
    return (y * g) * (1.0 + scale) + shift


def _rms(x, g):
    return x * lax.rsqrt(jnp.mean(x * x, axis=-1, keepdims=True) + EPS) * g


def _proj_kernel(x_ref, g_ref, sh_ref, sc_ref, w_ref, cos_ref, sin_ref,
                 gq_ref, wq_ref, gkv_ref, wkv_ref,
                 rq_ref, rk_ref, rv_ref, rg_ref, fq_ref, fk_ref, fv_ref,
                 mq_ref, mk_ref, kpe_ref, mv_ref, ff_ref):
    h = _norm_mod(x_ref[...], g_ref[...], sh_ref[...], sc_ref[...]).astype(BF16)
    cos_t = cos_ref[...]
    sin_t = sin_ref[...]
    lane = lax.broadcasted_iota(I32, cos_t.shape, 1)

    def proj(c0, width):
        return jnp.dot(h, w_ref[:, c0:c0 + width], preferred_element_type=F32)

    rq = proj(C_RQ, 256)
    rk = proj(C_RK, 256)
    for s in range(2):
        sl = slice(s * LANES, (s + 1) * LANES)
        rq_ref[:, sl] = _rope_slab(rq[:, sl], cos_t, sin_t, lane).astype(BF16)
        rk_ref[:, sl] = (_rope_slab(rk[:, sl], cos_t, sin_t, lane) * (RET_DK ** -0.5)).astype(BF16)
    rv_ref[...] = proj(C_RV, 512).astype(BF16)
    rg_ref[...] = proj(C_RG, 512).astype(BF16)
    fq_ref[...] = (proj(C_FQ, 512) * (FOX_DH ** -0.5 * LOG2E)).astype(BF16)
    fk_ref[...] = proj(C_FK, 512).astype(BF16)
    fv_ref[...] = proj(C_FV, 512).astype(BF16)

    tail = proj(C_TAIL, LANES)
    ff_ref[...] = tail
    kpe_ref[...] = jnp.where(lane < MLA_ROPE, _rope_slab(tail, cos_t, sin_t, lane), 0.0).astype(BF16)

    qn = _rms(proj(C_MQ, MLA_Q_RANK), gq_ref[...]).astype(BF16)
    qh = jnp.dot(qn, wq_ref[...], preferred_element_type=F32)
    q_scale = (MLA_NOPE + MLA_ROPE) ** -0.5 * LOG2E
    for hd in range(MLA_HEADS):
        c0 = hd * MLA_DQ
        mq_ref[:, c0:c0 + LANES] = (qh[:, c0:c0 + LANES] * q_scale).astype(BF16)
        pe = _rope_slab(qh[:, c0 + LANES:c0 + 2 * LANES], cos_t, sin_t, lane)
        mq_ref[:, c0 + LANES:c0 + 2 * LANES] = jnp.where(lane < MLA_ROPE, pe * q_scale, 0.0).astype(BF16)

    kvn = _rms(proj(C_MKV, MLA_KV_RANK), gkv_ref[...]).astype(BF16)
    kvh = jnp.dot(kvn, wkv_ref[...], preferred_element_type=F32)
    for hd in range(MLA_HEADS):
        c0 = hd * (MLA_NOPE + MLA_V)
        mk_ref[:, hd * MLA_NOPE:(hd + 1) * MLA_NOPE] = kvh[:, c0:c0 + MLA_NOPE].astype(BF16)
        mv_ref[:, hd * MLA_V:(hd + 1) * MLA_V] = kvh[:, c0 + MLA_NOPE:c0 + MLA_NOPE + MLA_V].astype(BF16)


def _proj(x2, S, g, sh, sc, w_all, cos_t, sin_t, gq, wq, gkv, wkv):
    T, D = x2.shape
    tm = _pick(S, 512)
    per_b = S // tm
    row = lambda i: (i, 0)
    const = lambda i: (0, 0)
    batch = lambda i: (i // per_b, 0, 0)
    widths = [256, 256, 512, 512, 512, 512, 512, MLA_HEADS * MLA_DQ, MLA_HEADS * MLA_NOPE, LANES,
              MLA_HEADS * MLA_V]
    out_shape = [jax.ShapeDtypeStruct((T, w), BF16) for w in widths]
    out_shape.append(jax.ShapeDtypeStruct((T, LANES), F32))
    out_specs = [pl.BlockSpec((tm, w), row) for w in widths] + [pl.BlockSpec((tm, LANES), row)]
    return pl.pallas_call(
        _proj_kernel,
        grid=(T // tm,),
        in_specs=[pl.BlockSpec((tm, D), row),
                  pl.BlockSpec((1, D), const),
                  pl.BlockSpec((None, 1, D), batch),
                  pl.BlockSpec((None, 1, D), batch),
                  pl.BlockSpec((D, D_IN_PAD), const),
                  pl.BlockSpec((tm, LANES), row),
                  pl.BlockSpec((tm, LANES), row),
                  pl.BlockSpec((1, MLA_Q_RANK), const),
                  pl.BlockSpec((MLA_Q_RANK, MLA_HEADS * MLA_DQ), const),
                  pl.BlockSpec((1, MLA_KV_RANK), const),
                  pl.BlockSpec((MLA_KV_RANK, MLA_HEADS * (MLA_NOPE + MLA_V)), const)],
        out_specs=out_specs,
        out_shape=out_shape,
        compiler_params=_cparams(("arbitrary",)),
        name="proj",
    )(x2, g, sh, sc, w_all, cos_t, sin_t, gq, wq, gkv, wkv)


def _split3(x):
    a = x.astype(BF16)
    r = x - a.astype(F32)
    b = r.astype(BF16)
    c = (r - b.astype(F32)).astype(BF16)
    return a, b, c


def _fox_gate_kernel(ff_ref, fb_ref, tri_ref, kb_ref, carry_ref, *, tiles_per_seq):
    @pl.when(pl.program_id(0) % tiles_per_seq == 0)
    def _():
        carry_ref[...] = jnp.zeros_like(carry_ref)

    tm = ff_ref.shape[0]
    lane = lax.broadcasted_iota(I32, (tm, LANES), 1)
    z = ff_ref[...] + fb_ref[...]
    ls = -(jnp.maximum(-z, 0.0) + jnp.log1p(jnp.exp(-jnp.abs(z))))
    ls = jnp.where((lane >= FF_LANE) & (lane < FF_LANE + FOX_HEADS), ls, 0.0)
    tri = tri_ref[...]
    f = carry_ref[...]
    for part in _split3(ls):
        f = f + jnp.dot(tri, part, preferred_element_type=F32)
    carry_ref[...] = f[tm - 1:tm, :]
    f2 = f * LOG2E
    for hd in range(FOX_HEADS):
        src = FF_LANE + hd
        g = jnp.where(lane == 0, pltpu.roll(f2, (LANES - src) % LANES, axis=1),
            jnp.where(lane == 1, pltpu.roll(f2, (LANES + 1 - src) % LANES, axis=1),
            jnp.where(lane == 2, pltpu.roll(f2, (LANES + 2 - src) % LANES, axis=1), 0.0)))
        hi, mid, lo = _split3(g)
        kb_ref[:, hd * LANES:(hd + 1) * LANES] = jnp.where(lane == 0, hi, jnp.where(lane == 1, mid, lo))


def _fox_gate(ffp, fb, S):
    T = ffp.shape[0]
    tm = _pick(S, 512)
    idx = jnp.arange(tm)
    tri = (idx[None, :] <= idx[:, None]).astype(BF16)
    fbv = jnp.zeros((1, LANES), F32).at[0, FF_LANE:FF_LANE + FOX_HEADS].set(fb)
    return pl.pallas_call(
        functools.partial(_fox_gate_kernel, tiles_per_seq=S // tm),
        grid=(T // tm,),
        in_specs=[pl.BlockSpec((tm, LANES), lambda i: (i, 0)),
                  pl.BlockSpec((1, LANES), lambda i: (0, 0)),
                  pl.BlockSpec((tm, tm), lambda i: (0, 0))],
        out_specs=pl.BlockSpec((tm, FOX_HEADS * LANES), lambda i: (i, 0)),
        out_shape=jax.ShapeDtypeStruct((T, FOX_HEADS * LANES), BF16),
        scratch_shapes=[pltpu.VMEM((1, LANES), F32)],
        compiler_params=_cparams(("arbitrary",)),
        name="fox_gate",
    )(ffp, fbv, tri)


def _retention_kernel(dchunk_ref, q_ref, k_ref, v_ref, g_ref, dmask_ref, din_ref, dout_ref,
                      o_ref, state_ref, *, n_chunks):
    @pl.when(pl.program_id(1) == 0)
    def _():
        state_ref[...] = jnp.zeros_like(state_ref)

    C = RET_CHUNK
    lane = lax.broadcasted_iota(I32, (C, LANES), 1)
    for ci in range(n_chunks):
        rows = slice(ci * C, (ci + 1) * C)
        for hd in range(RET_HEADS):
            slab = slice((hd // 2) * LANES, (hd // 2 + 1) * LANES)
            mine = (lane < RET_DK) if hd % 2 == 0 else (lane >= RET_DK)
            q = jnp.where(mine, q_ref[rows, slab], 0)
            k = jnp.where(mine, k_ref[rows, slab], 0)
            vcols = slice(hd * RET_DV, (hd + 1) * RET_DV)
            v = v_ref[rows, vcols]
            state = state_ref[hd]
            scores = lax.dot_general(q, k, (((1,), (1,)), ((), ())),
                                     preferred_element_type=F32) * dmask_ref[hd]
            inner = jnp.dot(scores.astype(BF16), v, preferred_element_type=F32)
            cross = jnp.dot(q, state.astype(BF16), preferred_element_type=F32) * din_ref[hd]
            o = inner + cross
            vd = (v.astype(F32) * dout_ref[hd]).astype(BF16)
            kv = lax.dot_general(k, vd, (((0,), (0,)), ((), ())), preferred_element_type=F32)
            state_ref[hd] = state * dchunk_ref[hd] + kv
            mu = jnp.mean(o, axis=-1, keepdims=True)
            d = o - mu
            var = jnp.mean(d * d, axis=-1, keepdims=True)
            on = d * lax.rsqrt(var + EPS)
            g = g_ref[rows, vcols].astype(F32)
            o_ref[rows, vcols] = (g * jax.nn.sigmoid(g) * on).astype(BF16)


def _retention(rq, rk, rv, rg, B, S):
    H, C = RET_HEADS, RET_CHUNK
    tr = _pick(S, 4 * C)
    n_chunks = tr // C
    log_gamma = jnp.log1p(-jnp.exp2(-5.0 - jnp.arange(H, dtype=F32)))
    idx = jnp.arange(C, dtype=F32)
    rel = idx[:, None] - idx[None, :]
    dmask = jnp.where(rel >= 0, jnp.exp(log_gamma[:, None, None] * jnp.maximum(rel, 0.0)), 0.0)
    decay_in = jnp.exp(log_gamma[:, None] * (idx + 1.0))
    decay_out = jnp.exp(log_gamma[:, None] * (C - 1.0 - idx))
    decay_chunk = jnp.exp(log_gamma * C)
    din = jnp.broadcast_to(decay_in[:, :, None], (H, C, RET_DV))
    dout = jnp.broadcast_to(decay_out[:, :, None], (H, C, RET_DV))
    tok = lambda b, i: (b, i, 0)
    const3 = lambda b, i: (0, 0, 0)
    return pl.pallas_call(
        functools.partial(_retention_kernel, n_chunks=n_chunks),
        grid=(B, S // tr),
        in_specs=[pl.BlockSpec(memory_space=pltpu.SMEM),
                  pl.BlockSpec((None, tr, H * RET_DK), tok),
                  pl.BlockSpec((None, tr, H * RET_DK), tok),
                  pl.BlockSpec((None, tr, H * RET_DV), tok),
                  pl.BlockSpec((None, tr, H * RET_DV), tok),
                  pl.BlockSpec((H, C, C), const3),
                  pl.BlockSpec((H, C, RET_DV), const3),
                  pl.BlockSpec((H, C, RET_DV), const3)],
        out_specs=pl.BlockSpec((None, tr, H * RET_DV), tok),
        out_shape=jax.ShapeDtypeStruct((B, S, H * RET_DV), BF16),
        scratch_shapes=[pltpu.VMEM((H, LANES, RET_DV), F32)],
        compiler_params=_cparams(("arbitrary", "arbitrary")),
        name="retention",
    )(decay_chunk, rq.reshape(B, S, -1), rk.reshape(B, S, -1), rv.reshape(B, S, -1),
      rg.reshape(B, S, -1), dmask, din, dout)


def _flash_kernel(q_ref, k_ref, ke_ref, v_ref, o_ref, m_ref, l_ref, acc_ref, s_ref, p_ref, *, q_bias_cols):
    qi = pl.program_id(2)
    tq = q_ref.shape[0]
    tk = tq
    reps = tk // LANES
    m_ref[...] = jnp.full(m_ref.shape, NEG_BIG, F32)
    l_ref[...] = jnp.zeros_like(l_ref)
    acc_ref[...] = jnp.zeros_like(acc_ref)
    q = q_ref[...]
    if q_bias_cols:
        lane = lax.broadcasted_iota(I32, (tq, LANES), 1)
        q = jnp.concatenate([q, jnp.where(lane < q_bias_cols, -1.0, 0.0).astype(BF16)], axis=1)

    def scores(j):
        start = pl.multiple_of(j * tk, tk)
        kj = jnp.concatenate([k_ref[pl.ds(start, tk), :], ke_ref[pl.ds(start, tk), :]], axis=1)
        return lax.dot_general(q, kj, (((1,), (1,)), ((), ())), preferred_element_type=F32)

    def update(slot, j, diag):
        start = pl.multiple_of(j * tk, tk)
        for rb in range(tq // FLASH_ROW_BLOCK):
            rows = slice(rb * FLASH_ROW_BLOCK, (rb + 1) * FLASH_ROW_BLOCK)
            s = s_ref[slot, rows, :]
            if diag:
                r = lax.broadcasted_iota(I32, s.shape, 0) + rb * FLASH_ROW_BLOCK
                c = lax.broadcasted_iota(I32, s.shape, 1)
                s = jnp.where(c <= r, s, NEG_BIG)
            m_prev = m_ref[rows, :]
            m_new = jnp.maximum(m_prev, jnp.max(s, axis=-1, keepdims=True))
            alpha = jnp.exp2(m_prev - m_new)
            p = jnp.exp2(s - jnp.concatenate([m_new] * reps, axis=1))
            l_ref[rows, :] = alpha * l_ref[rows, :] + jnp.sum(p, axis=-1, keepdims=True)
            acc_ref[rows, :] = alpha * acc_ref[rows, :]
            m_ref[rows, :] = m_new
            p_ref[rows, :] = p.astype(BF16)
        acc_ref[...] += jnp.dot(p_ref[...], v_ref[pl.ds(start, tk), :], preferred_element_type=F32)

    s_ref[0] = scores(0)

    def pair(p, carry):
        j = 2 * p
        s_ref[1] = scores(j + 1)
        update(0, j, False)
        s_ref[0] = scores(j + 2)
        update(1, j + 1, False)
        return carry

    lax.fori_loop(0, qi // 2, pair, 0)

    @pl.when(qi % 2 == 1)
    def _():
        s_ref[1] = scores(qi)
        update(0, qi - 1, False)
        update(1, qi, True)

    @pl.when(qi % 2 == 0)
    def _():
        update(0, qi, True)

    o_ref[...] = (acc_ref[...] / l_ref[...]).astype(o_ref.dtype)


def _flash(q, k, ke, v, B, S, H, dq, ke_per_head, q_bias_cols, name):
    tq = _pick(S, 512)
    nq = S // tq
    ke_map = (lambda b, h, i: (b, 0, h)) if ke_per_head else (lambda b, h, i: (b, 0, 0))
    return pl.pallas_call(
        functools.partial(_flash_kernel, q_bias_cols=q_bias_cols),
        grid=(B, H, nq),
        in_specs=[pl.BlockSpec((None, tq, dq), lambda b, h, i: (b, i, h)),
                  pl.BlockSpec((None, S, LANES), lambda b, h, i: (b, 0, h)),
                  pl.BlockSpec((None, S, LANES), ke_map),
                  pl.BlockSpec((None, S, LANES), lambda b, h, i: (b, 0, h))],
        out_specs=pl.BlockSpec((None, tq, LANES), lambda b, h, i: (b, i, h)),
        out_shape=jax.ShapeDtypeStruct((B, S, H * LANES), BF16),
        scratch_shapes=[pltpu.VMEM((tq, LANES), F32), pltpu.VMEM((tq, LANES), F32),
                        pltpu.VMEM((tq, LANES), F32), pltpu.VMEM((2, tq, tq), F32),
                        pltpu.VMEM((tq, tq), BF16)],
        compiler_params=_cparams(("arbitrary", "arbitrary", "arbitrary")),
        name=name,
    )(q, k, ke, v)


def _merge_kernel(x_ref, ya_ref, yb_ref, yc_ref, g1_ref, sh1_ref, sc1_ref, gt1_ref,
                  g2_ref, sh2_ref, sc2_ref, gw_ref, gb_ref, bw_ref, ow_ref,
                  rwh_ref, rwl_ref, rb_ref, tri_ref,
                  x1_ref, h2_ref, mi_ref, mf_ref, cnt_ref, h_ref, mg_ref, carry_ref):
    tm, D = x_ref.shape
    cn = MERGE_COL_CHUNK

    @pl.when(pl.program_id(0) == 0)
    def _():
        carry_ref[...] = jnp.zeros_like(carry_ref)

    h_ref[...] = _norm_mod(x_ref[...], g1_ref[...], sh1_ref[...], sc1_ref[...]).astype(BF16)
    for n in range(D // cn):
        cols = slice(n * cn, (n + 1) * cn)
        merged = None
        for i, y_ref in enumerate((ya_ref, yb_ref, yc_ref)):
            gcols = slice(i * D + n * cn, i * D + (n + 1) * cn)
            gate = jax.nn.sigmoid(jnp.dot(h_ref[...], gw_ref[:, gcols], preferred_element_type=F32)
                                  + gb_ref[:, gcols])
            br = jnp.dot(y_ref[...], bw_ref[i, :, cols], preferred_element_type=F32)
            merged = gate * br if merged is None else merged + gate * br
        mg_ref[:, cols] = merged.astype(BF16)
    for n in range(D // cn):
        cols = slice(n * cn, (n + 1) * cn)
        mix = jnp.dot(mg_ref[...], ow_ref[:, cols], preferred_element_type=F32)
        x1_ref[:, cols] = x_ref[:, cols] + gt1_ref[:, cols] * mix
    h2 = _norm_mod(x1_ref[...], g2_ref[...], sh2_ref[...], sc2_ref[...])
    h2_ref[...] = _pack_bf16_pair(h2[:, :D // 2], h2[:, D // 2:])

    hh = h2.astype(BF16)
    hl = (h2 - hh.astype(F32)).astype(BF16)
    nt = lambda a, b: lax.dot_general(a, b, (((1,), (1,)), ((), ())), preferred_element_type=F32)
    lt = nt(rwh_ref[...], hh) + nt(rwh_ref[...], hl) + nt(rwl_ref[...], hh) + rb_ref[...]
    row8 = lax.broadcasted_iota(I32, (EXP_PER_GROUP, tm), 0)
    gl = jnp.where(row8 < N_GROUPS, lt[N_EXPERTS:N_EXPERTS + EXP_PER_GROUP, :], -jnp.inf)
    gmax = jnp.max(gl, axis=0, keepdims=True)
    g_idx = jnp.min(jnp.where(gl == gmax, row8, EXP_PER_GROUP), axis=0, keepdims=True)
    g_w = 1.0 / jnp.sum(jnp.exp(gl - gmax), axis=0, keepdims=True)
    el = lt[(N_GROUPS - 1) * EXP_PER_GROUP:N_EXPERTS, :]
    for g in range(N_GROUPS - 2, -1, -1):
        el = jnp.where(g_idx == g, lt[g * EXP_PER_GROUP:(g + 1) * EXP_PER_GROUP, :], el)
    e1 = jnp.max(el, axis=0, keepdims=True)
    i1 = jnp.min(jnp.where(el == e1, row8, EXP_PER_GROUP), axis=0, keepdims=True)
    el2 = jnp.where(row8 == i1, -jnp.inf, el)
    e2 = jnp.max(el2, axis=0, keepdims=True)
    i2 = jnp.min(jnp.where(el2 == e2, row8, EXP_PER_GROUP), axis=0, keepdims=True)
    r = jnp.exp(e2 - e1)
    w1 = g_w / (1.0 + r)
    w2 = g_w * r / (1.0 + r)
    eid1 = g_idx * EXP_PER_GROUP + i1
    eid2 = g_idx * EXP_PER_GROUP + i2

    rowe = lax.broadcasted_iota(I32, (N_EXPERTS, tm), 0)
    hit1 = rowe == eid1
    hit2 = rowe == eid2
    onehot = jnp.where(hit1 | hit2, 1.0, 0.0)
    before = (jnp.dot(onehot.astype(BF16), tri_ref[...], preferred_element_type=F32)
              + jnp.concatenate([carry_ref[...]] * (tm // LANES), axis=1))
    rank1 = jnp.sum(jnp.where(hit1, before, 0.0), axis=0, keepdims=True)
    rank2 = jnp.sum(jnp.where(hit2, before, 0.0), axis=0, keepdims=True)
    carry_ref[...] = carry_ref[...] + jnp.sum(onehot, axis=1, keepdims=True)
    cnt_ref[...] = carry_ref[...]

    mi_ref[...] = jnp.where(row8 == 0, eid1,
                  jnp.where(row8 == 1, eid2,
                  jnp.where(row8 == 2, rank1.astype(I32),
                  jnp.where(row8 == 3, rank2.astype(I32), 0))))
    mf_ref[...] = jnp.where(row8 == 0, w1, jnp.where(row8 == 1, w2, 0.0))


def _merge(x2, S, ya, yb, yc, g1, sh1, sc1, gt1, g2, sh2, sc2, gw, gb, bw, ow, rwh, rwl, rb):
    T, D = x2.shape
    tm = _pick(S, 512)
    per_b = S // tm
    row = lambda i: (i, 0)
    col = lambda i: (0, i)
    const = lambda i: (0, 0)
    batch = lambda i: (i // per_b, 0, 0)
    idx = jnp.arange(tm)
    tri = (idx[:, None] < idx[None, :]).astype(BF16)
    vecb = pl.BlockSpec((None, 1, D), batch)
    return pl.pallas_call(
        _merge_kernel,
        grid=(T // tm,),
        in_specs=[pl.BlockSpec((tm, D), row),
                  pl.BlockSpec((tm, BRANCH_W), row),
                  pl.BlockSpec((tm, BRANCH_W), row),
                  pl.BlockSpec((tm, BRANCH_W), row),
                  pl.BlockSpec((1, D), const), vecb, vecb, vecb,
                  pl.BlockSpec((1, D), const), vecb, vecb,
                  pl.BlockSpec((D, N_BRANCH * D), const),
                  pl.BlockSpec((1, N_BRANCH * D), const),
                  pl.BlockSpec((N_BRANCH, BRANCH_W, D), lambda i: (0, 0, 0)),
                  pl.BlockSpec((D, D), const),
                  pl.BlockSpec((LANES, D), const),
                  pl.BlockSpec((LANES, D), const),
                  pl.BlockSpec((LANES, 1), const),
                  pl.BlockSpec((tm, tm), const)],
        out_specs=[pl.BlockSpec((tm, D), row), pl.BlockSpec((tm, D // 2), row),
                   pl.BlockSpec((8, tm), col), pl.BlockSpec((8, tm), col),
                   pl.BlockSpec((N_EXPERTS, LANES), const)],
        out_shape=[jax.ShapeDtypeStruct((T, D), F32), jax.ShapeDtypeStruct((T, D // 2), U32),
                   jax.ShapeDtypeStruct((8, T), I32), jax.ShapeDtypeStruct((8, T), F32),
                   jax.ShapeDtypeStruct((N_EXPERTS, LANES), F32)],
        scratch_shapes=[pltpu.VMEM((tm, D), BF16), pltpu.VMEM((tm, D), BF16),
                        pltpu.VMEM((N_EXPERTS, LANES), F32)],
        compiler_params=_cparams(("arbitrary",)),
        name="merge",
    )(x2, ya, yb, yc, g1, sh1, sc1, gt1, g2, sh2, sc2, gw, gb, bw, ow, rwh, rwl, rb, tri)


def _row_copy(src_ref, src_row, dst_ref, dst_row, sem):
    return pltpu.make_async_copy(src_ref.at[pl.ds(src_row, 1), :], dst_ref.at[pl.ds(dst_row, 1), :], sem)


def _drain_rows(src_ref, dst_ref, sem, n):
    def body(r, carry):
        _row_copy(src_ref, 0, dst_ref, 0, sem).wait()
        return carry
    lax.fori_loop(0, n, body, 0, unroll=8)


def _dispatch_kernel(dest_ref, h_ref, xs_in_ref, xs_ref, dsm_ref, sem_ref, isem_ref):
    del xs_in_ref
    tm = h_ref.shape[0]
    cp = pltpu.make_async_copy(dest_ref.at[0], dsm_ref, isem_ref)
    cp.start()
    cp.wait()

    def issue(r, carry):
        _row_copy(h_ref, r, xs_ref, dsm_ref[0, 2 * r], sem_ref).start(priority=0)
        _row_copy(h_ref, r, xs_ref, dsm_ref[0, 2 * r + 1], sem_ref).start(priority=1)
        return carry

    lax.fori_loop(0, tm, issue, 0, unroll=8)
    _drain_rows(h_ref, xs_ref, sem_ref, 2 * tm)


def _dispatch(h2, dest, P):
    T, D = h2.shape
    tm = _pick(T, 512)
    nt = T // tm
    xs0 = jnp.zeros((P, D), h2.dtype)
    return pl.pallas_call(
        _dispatch_kernel,
        grid=(nt,),
        in_specs=[pl.BlockSpec((1, 1, 2 * tm), lambda i: (i, 0, 0)),
                  pl.BlockSpec((tm, D), lambda i: (i, 0)),
                  pl.BlockSpec(memory_space=pl.ANY)],
        out_specs=pl.BlockSpec(memory_space=pl.ANY),
        out_shape=jax.ShapeDtypeStruct((P, D), h2.dtype),
        scratch_shapes=[pltpu.SMEM((1, 2 * tm), I32), pltpu.SemaphoreType.DMA,
                        pltpu.SemaphoreType.DMA],
        input_output_aliases={2: 0},
        compiler_params=_cparams(("arbitrary",)),
        name="moe_dispatch",
    )(dest.reshape(nt, 1, 2 * tm), h2, xs0)


def _expert_kernel(blk_e_ref, nvalid_ref, xs_ref, w1_ref, w3_ref, w2_ref, ys_ref):
    del blk_e_ref
    valid = pl.program_id(0) < nvalid_ref[0]

    @pl.when(valid)
    def _():
        half = xs_ref.shape[1]
        x_lo, x_hi = _unpack_bf16_pair(xs_ref[...])
        x_lo = x_lo.astype(BF16)
        x_hi = x_hi.astype(BF16)

        def up(w_ref):
            return (jnp.dot(x_lo, w_ref[:half, :], preferred_element_type=F32)
                    + jnp.dot(x_hi, w_ref[half:, :], preferred_element_type=F32))

        a = up(w1_ref)
        b = up(w3_ref)
        hid = (a * jax.nn.sigmoid(a) * b).astype(BF16)
        y = jnp.dot(hid, w2_ref[...], preferred_element_type=F32)
        ys_ref[...] = _pack_bf16_pair(y[:, :half], y[:, half:])

    @pl.when(jnp.logical_not(valid))
    def _():
        ys_ref[...] = jnp.zeros_like(ys_ref)


def _experts(xs, blk_e, nvalid, w1, w3, w2, tb):
    P, half = xs.shape
    D = 2 * half
    n_blocks = P // tb
    rows = lambda i, be, nv: (jnp.minimum(i, nv[0] - 1), 0)
    grid_spec = pltpu.PrefetchScalarGridSpec(
        num_scalar_prefetch=2,
        grid=(n_blocks,),
        in_specs=[pl.BlockSpec((tb, half), rows),
                  pl.BlockSpec((None, D, D_EXPERT), lambda i, be, nv: (be[i], 0, 0)),
                  pl.BlockSpec((None, D, D_EXPERT), lambda i, be, nv: (be[i], 0, 0)),
                  pl.BlockSpec((None, D_EXPERT, D), lambda i, be, nv: (be[i], 0, 0))],
        out_specs=pl.BlockSpec((tb, half), lambda i, be, nv: (i, 0)),
    )
    return pl.pallas_call(
        _expert_kernel,
        grid_spec=grid_spec,
        out_shape=jax.ShapeDtypeStruct((P, half), U32),
        compiler_params=_cparams(("arbitrary",)),
        name="moe_experts",
    )(blk_e, nvalid, xs, w1, w3, w2)


def _sc_gather(data, idx):
    M = idx.shape[0]
    D = data.shape[1]
    W = SC_GATHER_WINDOW
    assert M % W == 0, (M, W)
    mesh = plsc.VectorSubcoreMesh(core_axis_name="core", subcore_axis_name="subcore")
    n_workers = mesh.num_cores * mesh.num_subcores
    assert M % (W * n_workers) == 0, (M, W, n_workers)

    @functools.partial(pl.kernel, out_type=jax.ShapeDtypeStruct((M, D), data.dtype), mesh=mesh,
                       scratch_types=[pltpu.VMEM((W,), I32), pltpu.VMEM((W, D), data.dtype)])
    def gather_kernel(x_hbm, i_hbm, o_hbm, i_vmem, buf):
        worker = lax.axis_index("core") * mesh.num_subcores + lax.axis_index("subcore")

        @pl.loop(0, M // (W * n_workers))
        def _(t):
            start = (t * n_workers + worker) * W
            pltpu.sync_copy(i_hbm.at[pl.ds(start, W)], i_vmem)
            pltpu.sync_copy(x_hbm.at[i_vmem], buf)
            pltpu.sync_copy(buf, o_hbm.at[pl.ds(start, W)])

    return gather_kernel(data, idx)


def _combine_kernel(x_ref, g_ref, mf_ref, gt_ref, fg_ref, o_ref, *, final):
    half = g_ref.shape[1] // 2
    mf = mf_ref[...]
    lo0, hi0 = _unpack_bf16_pair(g_ref[:, :half])
    lo1, hi1 = _unpack_bf16_pair(g_ref[:, half:])
    w0 = mf[:, 0:1]
    w1 = mf[:, 1:2]
    ffn = jnp.concatenate([lo0 * w0 + lo1 * w1, hi0 * w0 + hi1 * w1], axis=1)
    out = x_ref[...] + gt_ref[...] * ffn
    if final:
        out = _rms(out, fg_ref[...])
    o_ref[...] = out


def _combine(x1, S, mf, gt2, final_g, g, final):
    T, D = x1.shape
    tm = _pick(S, 512)
    per_b = S // tm
    return pl.pallas_call(
        functools.partial(_combine_kernel, final=final),
        grid=(T // tm,),
        in_specs=[pl.BlockSpec((tm, D), lambda i: (i, 0)),
                  pl.BlockSpec((tm, D), lambda i: (i, 0)),
                  pl.BlockSpec((tm, 2), lambda i: (i, 0)),
                  pl.BlockSpec((None, 1, D), lambda i: (i // per_b, 0, 0)),
                  pl.BlockSpec((1, D), lambda i: (0, 0))],
        out_specs=pl.BlockSpec((tm, D), lambda i: (i, 0)),
        out_shape=jax.ShapeDtypeStruct((T, D), F32),
        compiler_params=_cparams(("arbitrary",)),
        name="moe_combine",
    )(x1, g, mf, gt2, final_g)


def _prep_w_in(w):
    offs = [0]
    for s in (256, 256, 512, 512, 512, 512, 512, FOX_HEADS, MLA_Q_RANK, MLA_KV_RANK, MLA_ROPE):
        offs.append(offs[-1] + s)
    rq, rk, rv, rg, fq, fk, fv, ff, mq, mkv, mkr = [w[:, offs[i]:offs[i + 1]] for i in range(11)]
    pad = jnp.zeros((w.shape[0], LANES - MLA_ROPE - FOX_HEADS), w.dtype)
    return jnp.concatenate([rq, rk, rv, rg, fq, fk, fv, mq, mkv, mkr, ff, pad], axis=1).astype(BF16)


def _prep_wq_up(w):
    r = w.reshape(MLA_Q_RANK, MLA_HEADS, MLA_NOPE + MLA_ROPE)
    r = jnp.pad(r, ((0, 0), (0, 0), (0, MLA_DQ - MLA_NOPE - MLA_ROPE)))
    return r.reshape(MLA_Q_RANK, MLA_HEADS * MLA_DQ).astype(BF16)


def _prep_router(w_grp, b_grp, w_exp, b_exp):
    D = w_grp.shape[0]
    pad = LANES - N_EXPERTS - N_GROUPS
    rwt = jnp.concatenate([w_exp, w_grp, jnp.zeros((D, pad), F32)], axis=1).astype(F32).T
    rwh = rwt.astype(BF16)
    rwl = (rwt - rwh.astype(F32)).astype(BF16)
    rb = jnp.concatenate([b_exp, b_grp, jnp.zeros((pad,), F32)]).astype(F32).reshape(LANES, 1)
    return rwh, rwl, rb


def kernel(x, c, positions, ada_w, ada_b, norm1_g, norm2_g, w_in, fox_fb, mla_q_norm_g, mla_wq_up, mla_kv_norm_g, mla_wkv_up, gate_w, gate_b, branch_w, out_w, router_grp_w, router_grp_b, router_exp_w, router_exp_b, exp_w1, exp_w3, exp_w2, final_g):
    B, S, D = x.shape
    L = ada_w.shape[0]
    T = B * S
    A = 2 * T
    tb = _pick(A, 256)
    n_blocks = A // tb + N_EXPERTS
    P = n_blocks * tb

    mod = _adaln(c, ada_w, ada_b)
    cos_t, sin_t = _rope_tables(positions)
    x2 = x.reshape(T, D)
    final_g2 = final_g.reshape(1, D)

    for l in range(L):
        sh1, sc1, gt1, sh2, sc2, gt2 = [mod[l, :, i * D:(i + 1) * D].reshape(B, 1, D) for i in range(6)]
        g1 = norm1_g[l].reshape(1, D)
        g2 = norm2_g[l].reshape(1, D)
        (rq, rk, rv, rg, fq, fk, fv, mq, mk, kpe, mv, ffp) = _proj(
            x2, S, g1, sh1, sc1, _prep_w_in(w_in[l]), cos_t, sin_t,
            mla_q_norm_g[l].reshape(1, -1), _prep_wq_up(mla_wq_up[l]),
            mla_kv_norm_g[l].reshape(1, -1), mla_wkv_up[l].astype(BF16))

        ya = _retention(rq, rk, rv, rg, B, S).reshape(T, -1)
        kb = _fox_gate(ffp, fox_fb[l], S)
        r3 = lambda a: a.reshape(B, S, -1)
        yb = _flash(r3(fq), r3(fk), r3(kb), r3(fv), B, S, FOX_HEADS, FOX_DH, True, 3, "flash_fox").reshape(T, -1)
        yc = _flash(r3(mq), r3(mk), r3(kpe), r3(mv), B, S, MLA_HEADS, MLA_DQ, False, 0, "flash_mla").reshape(T, -1)

        rwh, rwl, rb = _prep_router(router_grp_w[l], router_grp_b[l], router_exp_w[l], router_exp_b[l])
        x1, h2, mi, mf, cnt = _merge(
            x2, S, ya, yb, yc, g1, sh1, sc1, gt1, g2, sh2, sc2,
            gate_w[l].astype(BF16), gate_b[l].reshape(1, -1), branch_w[l].astype(BF16),
            out_w[l].astype(BF16), rwh, rwl, rb)

        counts = cnt[:, 0].astype(I32)
        pcounts = (counts + tb - 1) // tb * tb
        pends = jnp.cumsum(pcounts)
        pstarts = pends - pcounts
        sel = mi[0:2, :, None] == jnp.arange(N_EXPERTS, dtype=I32)
        dest = (jnp.sum(jnp.where(sel, pstarts, 0), axis=-1) + mi[2:4]).T.reshape(A)
        blk_pos = jnp.arange(n_blocks, dtype=I32) * tb
        blk_e = jnp.minimum(jnp.sum((pends[None, :] <= blk_pos[:, None]).astype(I32), axis=1), N_EXPERTS - 1)
        nvalid = (pends[-1:] // tb).astype(I32)

        xs = _dispatch(h2, dest, P)
        ys = _experts(xs, blk_e, nvalid, exp_w1[l].astype(BF16), exp_w3[l].astype(BF16),
                      exp_w2[l].astype(BF16), tb)
        g = _sc_gather(ys, dest).reshape(T, D)
        x2 = _combine(x1, S, mf[0:2].T, gt2, final_g2, g, final=(l == L - 1))

    return x2.reshape(B, S, D)
```

```python
import functools
import math

import jax
import jax.numpy as jnp
from jax import lax
from jax.experimental import pallas as pl
from jax.experimental.pallas import tpu as pltpu
from jax.experimental.pallas import tpu_sc as plsc

F32 = jnp.float32
BF16 = jnp.bfloat16
I32 = jnp.int32
U32 = jnp.uint32
HIGHEST = lax.Precision.HIGHEST

EPS = 1e-6
ROPE_THETA = 10000.0
RET_HEADS = 4
RET_DK = 64
RET_DV = 128
RET_CHUNK = 128
FOX_HEADS = 4
FOX_DH = 128
MLA_HEADS = 4
MLA_Q_RANK = 256
MLA_KV_RANK = 128
MLA_NOPE = 128
MLA_ROPE = 64
MLA_V = 128
MLA_DQ = 256
N_BRANCH = 3
BRANCH_W = 512
N_GROUPS = 4
EXP_PER_GROUP = 8
N_EXPERTS = N_GROUPS * EXP_PER_GROUP
D_EXPERT = 512

LANES = 128
V7X_VMEM_LIMIT = 56 * 1024 * 1024

C_RQ, C_RK, C_RV, C_RG = 0, 256, 512, 1024
C_FQ, C_FK, C_FV = 1536, 2048, 2560
C_MQ, C_MKV, C_TAIL = 3072, 3328, 3456
D_IN_PAD = 3584
FF_LANE = MLA_ROPE

NEG_BIG = -1e30
SC_GATHER_WINDOW = 128
MERGE_COL_CHUNK = 256
FLASH_ROW_BLOCK = 128
LOG2E = math.log2(math.e)


def _cparams(sem):
    return pltpu.CompilerParams(dimension_semantics=sem, vmem_limit_bytes=V7X_VMEM_LIMIT)


def _pick(n, pref):
    t = min(n, pref)
    assert n % t == 0, (n, t)
    return t


def _adaln_kernel(c_ref, w_ref, b_ref, o_ref):
    c = c_ref[...]
    ca = c * jax.nn.sigmoid(c)
    o_ref[...] = jnp.dot(ca, w_ref[...], preferred_element_type=F32, precision=HIGHEST) + b_ref[...]


def _adaln(c, ada_w, ada_b):
    L, D, N = ada_w.shape
    B = c.shape[0]
    tn = _pick(N, 1536)
    return pl.pallas_call(
        _adaln_kernel,
        grid=(L, N // tn),
        in_specs=[pl.BlockSpec((B, D), lambda l, j: (0, 0)),
                  pl.BlockSpec((None, D, tn), lambda l, j: (l, 0, j)),
                  pl.BlockSpec((None, 1, tn), lambda l, j: (l, 0, j))],
        out_specs=pl.BlockSpec((None, B, tn), lambda l, j: (l, 0, j)),
        out_shape=jax.ShapeDtypeStruct((L, B, N), F32),
        compiler_params=_cparams(("arbitrary", "arbitrary")),
        name="adaln",
    )(c, ada_w, ada_b.reshape(L, 1, N))


def _rope_table_kernel(pos_ref, inv_ref, sign_ref, cos_ref, sin_ref):
    ang = pos_ref[...].astype(F32) * inv_ref[...]
    cos_ref[...] = jnp.cos(ang)
    sin_ref[...] = jnp.sin(ang) * sign_ref[...]


def _rope_tables(positions):
    T = positions.size
    tm = _pick(T, 2048)
    half = MLA_ROPE // 2
    inv = ROPE_THETA ** (-jnp.arange(0, MLA_ROPE, 2, dtype=F32) / MLA_ROPE)
    inv_t = jnp.tile(inv, LANES // half).reshape(1, LANES)
    sign = jnp.where((jnp.arange(LANES) % MLA_ROPE) < half, -1.0, 1.0).astype(F32).reshape(1, LANES)
    return pl.pallas_call(
        _rope_table_kernel,
        grid=(T // tm,),
        in_specs=[pl.BlockSpec((tm, 1), lambda i: (i, 0)),
                  pl.BlockSpec((1, LANES), lambda i: (0, 0)),
                  pl.BlockSpec((1, LANES), lambda i: (0, 0))],
        out_specs=[pl.BlockSpec((tm, LANES), lambda i: (i, 0))] * 2,
        out_shape=[jax.ShapeDtypeStruct((T, LANES), F32)] * 2,
        compiler_params=_cparams(("arbitrary",)),
        name="rope_tables",
    )(positions.reshape(T, 1), inv_t, sign)


def _rope_slab(x, cos_t, sin_t, lane):
    nxt = pltpu.roll(x, LANES - 32, axis=1)
    prv = pltpu.roll(x, 32, axis=1)
    swapped = jnp.where((lane & 32) == 0, nxt, prv)
    return x * cos_t + swapped * sin_t


def _pack_bf16_pair(lo, hi):
    lo_bits = lax.shift_right_logical(lax.bitcast_convert_type(lo.astype(BF16).astype(F32), U32), jnp.uint32(16))
    hi_bits = lax.bitcast_convert_type(hi.astype(BF16).astype(F32), U32) & jnp.uint32(0xFFFF0000)
    return hi_bits | lo_bits


def _unpack_bf16_pair(w):
    lo = lax.bitcast_convert_type(lax.shift_left(w, jnp.uint32(16)), F32)
    hi = lax.bitcast_convert_type(w & jnp.uint32(0xFFFF0000), F32)
    return lo, hi


def _norm_mod(x, g, shift, scale):
    y = x * lax.rsqrt(jnp.mean(x * x, axis=-1, keepdims=True) + EPS)
    return (y * g) * (1.0 + scale) + shift


def _rms(x, g):
    return x * lax.rsqrt(jnp.mean(x * x, axis=-1, keepdims=True) + EPS) * g


def _proj_kernel(x_ref, g_ref, sh_ref, sc_ref, w_ref, cos_ref, sin_ref,
                 gq_ref, wq_ref, gkv_ref, wkv_ref,
                 rq_ref, rk_ref, rv_ref, rg_ref, fq_ref, fk_ref, fv_ref,
                 mq_ref, mk_ref, kpe_ref, mv_ref, ff_ref):
    h = _norm_mod(x_ref[...], g_ref[...], sh_ref[...], sc_ref[...]).astype(BF16)
    cos_t = cos_ref[...]
    sin_t = sin_ref[...]
    lane = lax.broadcasted_iota(I32, cos_t.shape, 1)

    def proj(c0, width):
        return jnp.dot(h, w_ref[:, c0:c0 + width], preferred_element_type=F32)

    rq = proj(C_RQ, 256)
    rk = proj(C_RK, 256)
    for s in range(2):
        sl = slice(s * LANES, (s + 1) * LANES)
        rq_ref[:, sl] = _rope_slab(rq[:, sl], cos_t, sin_t, lane).astype(BF16)
        rk_ref[:, sl] = (_rope_slab(rk[:, sl], cos_t, sin_t, lane) * (RET_DK ** -0.5)).astype(BF16)
    rv_ref[...] = proj(C_RV, 512).astype(BF16)
    rg_ref[...] = proj(C_RG, 512).astype(BF16)
    fq_ref[...] = (proj(C_FQ, 512) * (FOX_DH ** -0.5 * LOG2E)).astype(BF16)
    fk_ref[...] = proj(C_FK, 512).astype(BF16)
    fv_ref[...] = proj(C_FV, 512).astype(BF16)

    tail = proj(C_TAIL, LANES)
    ff_ref[...] = tail
    kpe_ref[...] = jnp.where(lane < MLA_ROPE, _rope_slab(tail, cos_t, sin_t, lane), 0.0).astype(BF16)

    qn = _rms(proj(C_MQ, MLA_Q_RANK), gq_ref[...]).astype(BF16)
    qh = jnp.dot(qn, wq_ref[...], preferred_element_type=F32)
    q_scale = (MLA_NOPE + MLA_ROPE) ** -0.5 * LOG2E
    for hd in range(MLA_HEADS):
        c0 = hd * MLA_DQ
        mq_ref[:, c0:c0 + LANES] = (qh[:, c0:c0 + LANES] * q_scale).astype(BF16)
        pe = _rope_slab(qh[:, c0 + LANES:c0 + 2 * LANES], cos_t, sin_t, lane)
        mq_ref[:, c0 + LANES:c0 + 2 * LANES] = jnp.where(lane < MLA_ROPE, pe * q_scale, 0.0).astype(BF16)

    kvn = _rms(proj(C_MKV, MLA_KV_RANK), gkv_ref[...]).astype(BF16)
    kvh = jnp.dot(kvn, wkv_ref[...], preferred_element_type=F32)
    for hd in range(MLA_HEADS):
        c0 = hd * (MLA_NOPE + MLA_V)
        mk_ref[:, hd * MLA_NOPE:(hd + 1) * MLA_NOPE] = kvh[:, c0:c0 + MLA_NOPE].astype(BF16)
        mv_ref[:, hd * MLA_V:(hd + 1) * MLA_V] = kvh[:, c0 + MLA_NOPE:c0 + MLA_NOPE + MLA_V].astype(BF16)


def _proj(x2, S, g, sh, sc, w_all, cos_t, sin_t, gq, wq, gkv, wkv):
    T, D = x2.shape
    tm = _pick(S, 512)
    per_b = S // tm
    row = lambda i: (i, 0)
    const = lambda i: (0, 0)
    batch = lambda i: (i // per_b, 0, 0)
    widths = [256, 256, 512, 512, 512, 512, 512, MLA_HEADS * MLA_DQ, MLA_HEADS * MLA_NOPE, LANES,
              MLA_HEADS * MLA_V]
    out_shape = [jax.ShapeDtypeStruct((T, w), BF16) for w in widths]
    out_shape.append(jax.ShapeDtypeStruct((T, LANES), F32))
    out_specs = [pl.BlockSpec((tm, w), row) for w in widths] + [pl.BlockSpec((tm, LANES), row)]
    return pl.pallas_call(
        _proj_kernel,
        grid=(T // tm,),
        in_specs=[pl.BlockSpec((tm, D), row),
                  pl.BlockSpec((1, D), const),
                  pl.BlockSpec((None, 1, D), batch),
                  pl.BlockSpec((None, 1, D), batch),
                  pl.BlockSpec((D, D_IN_PAD), const),
                  pl.BlockSpec((tm, LANES), row),
                  pl.BlockSpec((tm, LANES), row),
                  pl.BlockSpec((1, MLA_Q_RANK), const),
                  pl.BlockSpec((MLA_Q_RANK, MLA_HEADS * MLA_DQ), const),
                  pl.BlockSpec((1, MLA_KV_RANK), const),
                  pl.BlockSpec((MLA_KV_RANK, MLA_HEADS * (MLA_NOPE + MLA_V)), const)],
        out_specs=out_specs,
        out_shape=out_shape,
        compiler_params=_cparams(("arbitrary",)),
        name="proj",
    )(x2, g, sh, sc, w_all, cos_t, sin_t, gq, wq, gkv, wkv)


def _split3(x):
    a = x.astype(BF16)
    r = x - a.astype(F32)
    b = r.astype(BF16)
    c = (r - b.astype(F32)).astype(BF16)
    return a, b, c


def _fox_gate_kernel(ff_ref, fb_ref, tri_ref, kb_ref, carry_ref, *, tiles_per_seq):
    @pl.when(pl.program_id(0) % tiles_per_seq == 0)
    def _():
        carry_ref[...] = jnp.zeros_like(carry_ref)

    tm = ff_ref.shape[0]
    lane = lax.broadcasted_iota(I32, (tm, LANES), 1)
    z = ff_ref[...] + fb_ref[...]
    ls = -(jnp.maximum(-z, 0.0) + jnp.log1p(jnp.exp(-jnp.abs(z))))
    ls = jnp.where((lane >= FF_LANE) & (lane < FF_LANE + FOX_HEADS), ls, 0.0)
    tri = tri_ref[...]
    f = carry_ref[...]
    for part in _split3(ls):
        f = f + jnp.dot(tri, part, preferred_element_type=F32)
    carry_ref[...] = f[tm - 1:tm, :]
    f2 = f * LOG2E
    for hd in range(FOX_HEADS):
        src = FF_LANE + hd
        g = jnp.where(lane == 0, pltpu.roll(f2, (LANES - src) % LANES, axis=1),
            jnp.where(lane == 1, pltpu.roll(f2, (LANES + 1 - src) % LANES, axis=1),
            jnp.where(lane == 2, pltpu.roll(f2, (LANES + 2 - src) % LANES, axis=1), 0.0)))
        hi, mid, lo = _split3(g)
        kb_ref[:, hd * LANES:(hd + 1) * LANES] = jnp.where(lane == 0, hi, jnp.where(lane == 1, mid, lo))


def _fox_gate(ffp, fb, S):
    T = ffp.shape[0]
    tm = _pick(S, 512)
    idx = jnp.arange(tm)
    tri = (idx[None, :] <= idx[:, None]).astype(BF16)
    fbv = jnp.zeros((1, LANES), F32).at[0, FF_LANE:FF_LANE + FOX_HEADS].set(fb)
    return pl.pallas_call(
        functools.partial(_fox_gate_kernel, tiles_per_seq=S // tm),
        grid=(T // tm,),
        in_specs=[pl.BlockSpec((tm, LANES), lambda i: (i, 0)),
                  pl.BlockSpec((1, LANES), lambda i: (0, 0)),
                  pl.BlockSpec((tm, tm), lambda i: (0, 0))],
        out_specs=pl.BlockSpec((tm, FOX_HEADS * LANES), lambda i: (i, 0)),
        out_shape=jax.ShapeDtypeStruct((T, FOX_HEADS * LANES), BF16),
        scratch_shapes=[pltpu.VMEM((1, LANES), F32)],
        compiler_params=_cparams(("arbitrary",)),
        name="fox_gate",
    )(ffp, fbv, tri)


def _retention_kernel(dchunk_ref, q_ref, k_ref, v_ref, g_ref, dmask_ref, din_ref, dout_ref,
                      o_ref, state_ref, *, n_chunks):
    @pl.when(pl.program_id(1) == 0)
    def _():
        state_ref[...] = jnp.zeros_like(state_ref)

    C = RET_CHUNK
    lane = lax.broadcasted_iota(I32, (C, LANES), 1)
    for ci in range(n_chunks):
        rows = slice(ci * C, (ci + 1) * C)
        for hd in range(RET_HEADS):
            slab = slice((hd // 2) * LANES, (hd // 2 + 1) * LANES)
            mine = (lane < RET_DK) if hd % 2 == 0 else (lane >= RET_DK)
            q = jnp.where(mine, q_ref[rows, slab], 0)
            k = jnp.where(mine, k_ref[rows, slab], 0)
            vcols = slice(hd * RET_DV, (hd + 1) * RET_DV)
            v = v_ref[rows, vcols]
            state = state_ref[hd]
            scores = lax.dot_general(q, k, (((1,), (1,)), ((), ())),
                                     preferred_element_type=F32) * dmask_ref[hd]
            inner = jnp.dot(scores.astype(BF16), v, preferred_element_type=F32)
            cross = jnp.dot(q, state.astype(BF16), preferred_element_type=F32) * din_ref[hd]
            o = inner + cross
            vd = (v.astype(F32) * dout_ref[hd]).astype(BF16)
            kv = lax.dot_general(k, vd, (((0,), (0,)), ((), ())), preferred_element_type=F32)
            state_ref[hd] = state * dchunk_ref[hd] + kv
            mu = jnp.mean(o, axis=-1, keepdims=True)
            d = o - mu
            var = jnp.mean(d * d, axis=-1, keepdims=True)
            on = d * lax.rsqrt(var + EPS)
            g = g_ref[rows, vcols].astype(F32)
            o_ref[rows, vcols] = (g * jax.nn.sigmoid(g) * on).astype(BF16)


def _retention(rq, rk, rv, rg, B, S):
    H, C = RET_HEADS, RET_CHUNK
    tr = _pick(S, 4 * C)
    n_chunks = tr // C
    log_gamma = jnp.log1p(-jnp.exp2(-5.0 - jnp.arange(H, dtype=F32)))
    idx = jnp.arange(C, dtype=F32)
    rel = idx[:, None] - idx[None, :]
    dmask = jnp.where(rel >= 0, jnp.exp(log_gamma[:, None, None] * jnp.maximum(rel, 0.0)), 0.0)
    decay_in = jnp.exp(log_gamma[:, None] * (idx + 1.0))
    decay_out = jnp.exp(log_gamma[:, None] * (C - 1.0 - idx))
    decay_chunk = jnp.exp(log_gamma * C)
    din = jnp.broadcast_to(decay_in[:, :, None], (H, C, RET_DV))
    dout = jnp.broadcast_to(decay_out[:, :, None], (H, C, RET_DV))
    tok = lambda b, i: (b, i, 0)
    const3 = lambda b, i: (0, 0, 0)
    return pl.pallas_call(
        functools.partial(_retention_kernel, n_chunks=n_chunks),
        grid=(B, S // tr),
        in_specs=[pl.BlockSpec(memory_space=pltpu.SMEM),
                  pl.BlockSpec((None, tr, H * RET_DK), tok),
                  pl.BlockSpec((None, tr, H * RET_DK), tok),
                  pl.BlockSpec((None, tr, H * RET_DV), tok),
                  pl.BlockSpec((None, tr, H * RET_DV), tok),
                  pl.BlockSpec((H, C, C), const3),
                  pl.BlockSpec((H, C, RET_DV), const3),
                  pl.BlockSpec((H, C, RET_DV), const3)],
        out_specs=pl.BlockSpec((None, tr, H * RET_DV), tok),
        out_shape=jax.ShapeDtypeStruct((B, S, H * RET_DV), BF16),
        scratch_shapes=[pltpu.VMEM((H, LANES, RET_DV), F32)],
        compiler_params=_cparams(("arbitrary", "arbitrary")),
        name="retention",
    )(decay_chunk, rq.reshape(B, S, -1), rk.reshape(B, S, -1), rv.reshape(B, S, -1),
      rg.reshape(B, S, -1), dmask, din, dout)


def _flash_kernel(q_ref, k_ref, ke_ref, v_ref, o_ref, m_ref, l_ref, acc_ref, s_ref, p_ref, *, q_bias_cols):
    qi = pl.program_id(2)
    tq = q_ref.shape[0]
    tk = tq
    reps = tk // LANES
    m_ref[...] = jnp.full(m_ref.shape, NEG_BIG, F32)
    l_ref[...] = jnp.zeros_like(l_ref)
    acc_ref[...] = jnp.zeros_like(acc_ref)
    q = q_ref[...]
    if q_bias_cols:
        lane = lax.broadcasted_iota(I32, (tq, LANES), 1)
        q = jnp.concatenate([q, jnp.where(lane < q_bias_cols, -1.0, 0.0).astype(BF16)], axis=1)

    def scores(j):
        start = pl.multiple_of(j * tk, tk)
        kj = jnp.concatenate([k_ref[pl.ds(start, tk), :], ke_ref[pl.ds(start, tk), :]], axis=1)
        return lax.dot_general(q, kj, (((1,), (1,)), ((), ())), preferred_element_type=F32)

    def update(slot, j, diag):
        start = pl.multiple_of(j * tk, tk)
        for rb in range(tq // FLASH_ROW_BLOCK):
            rows = slice(rb * FLASH_ROW_BLOCK, (rb + 1) * FLASH_ROW_BLOCK)
            s = s_ref[slot, rows, :]
            if diag:
                r = lax.broadcasted_iota(I32, s.shape, 0) + rb * FLASH_ROW_BLOCK
                c = lax.broadcasted_iota(I32, s.shape, 1)
                s = jnp.where(c <= r, s, NEG_BIG)
            m_prev = m_ref[rows, :]
            m_new = jnp.maximum(m_prev, jnp.max(s, axis=-1, keepdims=True))
            alpha = jnp.exp2(m_prev - m_new)
            p = jnp.exp2(s - jnp.concatenate([m_new] * reps, axis=1))
            l_ref[rows, :] = alpha * l_ref[rows, :] + jnp.sum(p, axis=-1, keepdims=True)
            acc_ref[rows, :] = alpha * acc_ref[rows, :]
            m_ref[rows, :] = m_new
            p_ref[rows, :] = p.astype(BF16)
        acc_ref[...] += jnp.dot(p_ref[...], v_ref[pl.ds(start, tk), :], preferred_element_type=F32)

    s_ref[0] = scores(0)

    def pair(p, carry):
        j = 2 * p
        s_ref[1] = scores(j + 1)
        update(0, j, False)
        s_ref[0] = scores(j + 2)
        update(1, j + 1, False)
        return carry

    lax.fori_loop(0, qi // 2, pair, 0)

    @pl.when(qi % 2 == 1)
    def _():
        s_ref[1] = scores(qi)
        update(0, qi - 1, False)
        update(1, qi, True)

    @pl.when(qi % 2 == 0)
    def _():
        update(0, qi, True)

    o_ref[...] = (acc_ref[...] / l_ref[...]).astype(o_ref.dtype)


def _flash(q, k, ke, v, B, S, H, dq, ke_per_head, q_bias_cols, name):
    tq = _pick(S, 512)
    nq = S // tq
    ke_map = (lambda b, h, i: (b, 0, h)) if ke_per_head else (lambda b, h, i: (b, 0, 0))
    return pl.pallas_call(
        functools.partial(_flash_kernel, q_bias_cols=q_bias_cols),
        grid=(B, H, nq),
        in_specs=[pl.BlockSpec((None, tq, dq), lambda b, h, i: (b, i, h)),
                  pl.BlockSpec((None, S, LANES), lambda b, h, i: (b, 0, h)),
                  pl.BlockSpec((None, S, LANES), ke_map),
                  pl.BlockSpec((None, S, LANES), lambda b, h, i: (b, 0, h))],
        out_specs=pl.BlockSpec((None, tq, LANES), lambda b, h, i: (b, i, h)),
        out_shape=jax.ShapeDtypeStruct((B, S, H * LANES), BF16),
        scratch_shapes=[pltpu.VMEM((tq, LANES), F32), pltpu.VMEM((tq, LANES), F32),
                        pltpu.VMEM((tq, LANES), F32), pltpu.VMEM((2, tq, tq), F32),
                        pltpu.VMEM((tq, tq), BF16)],
        compiler_params=_cparams(("arbitrary", "arbitrary", "arbitrary")),
        name=name,
    )(q, k, ke, v)


def _merge_kernel(x_ref, ya_ref, yb_ref, yc_ref, g1_ref, sh1_ref, sc1_ref, gt1_ref,
                  g2_ref, sh2_ref, sc2_ref, gw_ref, gb_ref, bw_ref, ow_ref,
                  rwh_ref, rwl_ref, rb_ref, tri_ref,
                  x1_ref, h2_ref, mi_ref, mf_ref, cnt_ref, h_ref, mg_ref, carry_ref):
    tm, D = x_ref.shape
    cn = MERGE_COL_CHUNK

    @pl.when(pl.program_id(0) == 0)
    def _():
        carry_ref[...] = jnp.zeros_like(carry_ref)

    h_ref[...] = _norm_mod(x_ref[...], g1_ref[...], sh1_ref[...], sc1_ref[...]).astype(BF16)
    for n in range(D // cn):
        cols = slice(n * cn, (n + 1) * cn)
        merged = None
        for i, y_ref in enumerate((ya_ref, yb_ref, yc_ref)):
            gcols = slice(i * D + n * cn, i * D + (n + 1) * cn)
            gate = jax.nn.sigmoid(jnp.dot(h_ref[...], gw_ref[:, gcols], preferred_element_type=F32)
                                  + gb_ref[:, gcols])
            br = jnp.dot(y_ref[...], bw_ref[i, :, cols], preferred_element_type=F32)
            merged = gate * br if merged is None else merged + gate * br
        mg_ref[:, cols] = merged.astype(BF16)
    for n in range(D // cn):
        cols = slice(n * cn, (n + 1) * cn)
        mix = jnp.dot(mg_ref[...], ow_ref[:, cols], preferred_element_type=F32)
        x1_ref[:, cols] = x_ref[:, cols] + gt1_ref[:, cols] * mix
    h2 = _norm_mod(x1_ref[...], g2_ref[...], sh2_ref[...], sc2_ref[...])
    h2_ref[...] = _pack_bf16_pair(h2[:, :D // 2], h2[:, D // 2:])

    hh = h2.astype(BF16)
    hl = (h2 - hh.astype(F32)).astype(BF16)
    nt = lambda a, b: lax.dot_general(a, b, (((1,), (1,)), ((), ())), preferred_element_type=F32)
    lt = nt(rwh_ref[...], hh) + nt(rwh_ref[...], hl) + nt(rwl_ref[...], hh) + rb_ref[...]
    row8 = lax.broadcasted_iota(I32, (EXP_PER_GROUP, tm), 0)
    gl = jnp.where(row8 < N_GROUPS, lt[N_EXPERTS:N_EXPERTS + EXP_PER_GROUP, :], -jnp.inf)
    gmax = jnp.max(gl, axis=0, keepdims=True)
    g_idx = jnp.min(jnp.where(gl == gmax, row8, EXP_PER_GROUP), axis=0, keepdims=True)
    g_w = 1.0 / jnp.sum(jnp.exp(gl - gmax), axis=0, keepdims=True)
    el = lt[(N_GROUPS - 1) * EXP_PER_GROUP:N_EXPERTS, :]
    for g in range(N_GROUPS - 2, -1, -1):
        el = jnp.where(g_idx == g, lt[g * EXP_PER_GROUP:(g + 1) * EXP_PER_GROUP, :], el)
    e1 = jnp.max(el, axis=0, keepdims=True)
    i1 = jnp.min(jnp.where(el == e1, row8, EXP_PER_GROUP), axis=0, keepdims=True)
    el2 = jnp.where(row8 == i1, -jnp.inf, el)
    e2 = jnp.max(el2, axis=0, keepdims=True)
    i2 = jnp.min(jnp.where(el2 == e2, row8, EXP_PER_GROUP), axis=0, keepdims=True)
    r = jnp.exp(e2 - e1)
    w1 = g_w / (1.0 + r)
    w2 = g_w * r / (1.0 + r)
    eid1 = g_idx * EXP_PER_GROUP + i1
    eid2 = g_idx * EXP_PER_GROUP + i2

    rowe = lax.broadcasted_iota(I32, (N_EXPERTS, tm), 0)
    hit1 = rowe == eid1
    hit2 = rowe == eid2
    onehot = jnp.where(hit1 | hit2, 1.0, 0.0)
    before = (jnp.dot(onehot.astype(BF16), tri_ref[...], preferred_element_type=F32)
              + jnp.concatenate([carry_ref[...]] * (tm // LANES), axis=1))
    rank1 = jnp.sum(jnp.where(hit1, before, 0.0), axis=0, keepdims=True)
    rank2 = jnp.sum(jnp.where(hit2, before, 0.0), axis=0, keepdims=True)
    carry_ref[...] = carry_ref[...] + jnp.sum(onehot, axis=1, keepdims=True)
    cnt_ref[...] = carry_ref[...]

    mi_ref[...] = jnp.where(row8 == 0, eid1,
                  jnp.where(row8 == 1, eid2,
                  jnp.where(row8 == 2, rank1.astype(I32),
                  jnp.where(row8 == 3, rank2.astype(I32), 0))))
    mf_ref[...] = jnp.where(row8 == 0, w1, jnp.where(row8 == 1, w2, 0.0))


def _merge(x2, S, ya, yb, yc, g1, sh1, sc1, gt1, g2, sh2, sc2, gw, gb, bw, ow, rwh, rwl, rb):
    T, D = x2.shape
    tm = _pick(S, 512)
    per_b = S // tm
    row = lambda i: (i, 0)
    col = lambda i: (0, i)
    const = lambda i: (0, 0)
    batch = lambda i: (i // per_b, 0, 0)
    idx = jnp.arange(tm)
    tri = (idx[:, None] < idx[None, :]).astype(BF16)
    vecb = pl.BlockSpec((None, 1, D), batch)
    return pl.pallas_call(
        _merge_kernel,
        grid=(T // tm,),
        in_specs=[pl.BlockSpec((tm, D), row),
                  pl.BlockSpec((tm, BRANCH_W), row),
                  pl.BlockSpec((tm, BRANCH_W), row),
                  pl.BlockSpec((tm, BRANCH_W), row),
                  pl.BlockSpec((1, D), const), vecb, vecb, vecb,
                  pl.BlockSpec((1, D), const), vecb, vecb,
                  pl.BlockSpec((D, N_BRANCH * D), const),
                  pl.BlockSpec((1, N_BRANCH * D), const),
                  pl.BlockSpec((N_BRANCH, BRANCH_W, D), lambda i: (0, 0, 0)),
                  pl.BlockSpec((D, D), const),
                  pl.BlockSpec((LANES, D), const),
                  pl.BlockSpec((LANES, D), const),
                  pl.BlockSpec((LANES, 1), const),
                  pl.BlockSpec((tm, tm), const)],
        out_specs=[pl.BlockSpec((tm, D), row), pl.BlockSpec((tm, D // 2), row),
                   pl.BlockSpec((8, tm), col), pl.BlockSpec((8, tm), col),
                   pl.BlockSpec((N_EXPERTS, LANES), const)],
        out_shape=[jax.ShapeDtypeStruct((T, D), F32), jax.ShapeDtypeStruct((T, D // 2), U32),
                   jax.ShapeDtypeStruct((8, T), I32), jax.ShapeDtypeStruct((8, T), F32),
                   jax.ShapeDtypeStruct((N_EXPERTS, LANES), F32)],
        scratch_shapes=[pltpu.VMEM((tm, D), BF16), pltpu.VMEM((tm, D), BF16),
                        pltpu.VMEM((N_EXPERTS, LANES), F32)],
        compiler_params=_cparams(("arbitrary",)),
        name="merge",
    )(x2, ya, yb, yc, g1, sh1, sc1, gt1, g2, sh2, sc2, gw, gb, bw, ow, rwh, rwl, rb, tri)


def _expert_kernel(blk_e_ref, nvalid_ref, xs_ref, w1_ref, w3_ref, w2_ref, ys_ref):
    del blk_e_ref
    valid = pl.program_id(0) < nvalid_ref[0]

    @pl.when(valid)
    def _():
        half = xs_ref.shape[1]
        x_lo, x_hi = _unpack_bf16_pair(xs_ref[...])
        x_lo = x_lo.astype(BF16)
        x_hi = x_hi.astype(BF16)

        def up(w_ref):
            return (jnp.dot(x_lo, w_ref[:half, :], preferred_element_type=F32)
                    + jnp.dot(x_hi, w_ref[half:, :], preferred_element_type=F32))

        a = up(w1_ref)
        b = up(w3_ref)
        hid = (a * jax.nn.sigmoid(a) * b).astype(BF16)
        y = jnp.dot(hid, w2_ref[...], preferred_element_type=F32)
        ys_ref[...] = _pack_bf16_pair(y[:, :half], y[:, half:])

    @pl.when(jnp.logical_not(valid))
    def _():
        ys_ref[...] = jnp.zeros_like(ys_ref)


def _experts(xs, blk_e, nvalid, w1, w3, w2, tb):
    P, half = xs.shape
    D = 2 * half
    n_blocks = P // tb
    rows = lambda i, be, nv: (jnp.minimum(i, nv[0] - 1), 0)
    grid_spec = pltpu.PrefetchScalarGridSpec(
        num_scalar_prefetch=2,
        grid=(n_blocks,),
        in_specs=[pl.BlockSpec((tb, half), rows),
                  pl.BlockSpec((None, D, D_EXPERT), lambda i, be, nv: (be[i], 0, 0)),
                  pl.BlockSpec((None, D, D_EXPERT), lambda i, be, nv: (be[i], 0, 0)),
                  pl.BlockSpec((None, D_EXPERT, D), lambda i, be, nv: (be[i], 0, 0))],
        out_specs=pl.BlockSpec((tb, half), lambda i, be, nv: (i, 0)),
    )
    return pl.pallas_call(
        _expert_kernel,
        grid_spec=grid_spec,
        out_shape=jax.ShapeDtypeStruct((P, half), U32),
        compiler_params=_cparams(("arbitrary",)),
        name="moe_experts",
    )(blk_e, nvalid, xs, w1, w3, w2)


def _sc_gather(data, idx):
    M = idx.shape[0]
    D = data.shape[1]
    W = SC_GATHER_WINDOW
    assert M % W == 0, (M, W)
    mesh = plsc.VectorSubcoreMesh(core_axis_name="core", subcore_axis_name="subcore")
    n_workers = mesh.num_cores * mesh.num_subcores
    assert M % (W * n_workers) == 0, (M, W, n_workers)

    @functools.partial(pl.kernel, out_type=jax.ShapeDtypeStruct((M, D), data.dtype), mesh=mesh,
                       scratch_types=[pltpu.VMEM((W,), I32), pltpu.VMEM((W, D), data.dtype)])
    def gather_kernel(x_hbm, i_hbm, o_hbm, i_vmem, buf):
        worker = lax.axis_index("core") * mesh.num_subcores + lax.axis_index("subcore")

        @pl.loop(0, M // (W * n_workers))
        def _(t):
            start = (t * n_workers + worker) * W
            pltpu.sync_copy(i_hbm.at[pl.ds(start, W)], i_vmem)
            pltpu.sync_copy(x_hbm.at[i_vmem], buf)
            pltpu.sync_copy(buf, o_hbm.at[pl.ds(start, W)])

    return gather_kernel(data, idx)


def _combine_kernel(x_ref, g0_ref, g1_ref, mf_ref, gt_ref, fg_ref, o_ref, *, final):
    mf = mf_ref[...]
    lo0, hi0 = _unpack_bf16_pair(g0_ref[...])
    lo1, hi1 = _unpack_bf16_pair(g1_ref[...])
    w0 = mf[:, 0:1]
    w1 = mf[:, 1:2]
    ffn = jnp.concatenate([lo0 * w0 + lo1 * w1, hi0 * w0 + hi1 * w1], axis=1)
    out = x_ref[...] + gt_ref[...] * ffn
    if final:
        out = _rms(out, fg_ref[...])
    o_ref[...] = out


def _combine(x1, S, mf, gt2, final_g, g, final):
    T, D = x1.shape
    tm = _pick(S, 512)
    per_b = S // tm
    nt = T // tm
    return pl.pallas_call(
        functools.partial(_combine_kernel, final=final),
        grid=(nt,),
        in_specs=[pl.BlockSpec((tm, D), lambda i: (i, 0)),
                  pl.BlockSpec((tm, D // 2), lambda i: (i, 0)),
                  pl.BlockSpec((tm, D // 2), lambda i: (i + nt, 0)),
                  pl.BlockSpec((tm, 2), lambda i: (i, 0)),
                  pl.BlockSpec((None, 1, D), lambda i: (i // per_b, 0, 0)),
                  pl.BlockSpec((1, D), lambda i: (0, 0))],
        out_specs=pl.BlockSpec((tm, D), lambda i: (i, 0)),
        out_shape=jax.ShapeDtypeStruct((T, D), F32),
        compiler_params=_cparams(("arbitrary",)),
        name="moe_combine",
    )(x1, g, g, mf, gt2, final_g)


def _prep_w_in(w):
    offs = [0]
    for s in (256, 256, 512, 512, 512, 512, 512, FOX_HEADS, MLA_Q_RANK, MLA_KV_RANK, MLA_ROPE):
        offs.append(offs[-1] + s)
    rq, rk, rv, rg, fq, fk, fv, ff, mq, mkv, mkr = [w[:, offs[i]:offs[i + 1]] for i in range(11)]
    pad = jnp.zeros((w.shape[0], LANES - MLA_ROPE - FOX_HEADS), w.dtype)
    return jnp.concatenate([rq, rk, rv, rg, fq, fk, fv, mq, mkv, mkr, ff, pad], axis=1).astype(BF16)


def _prep_wq_up(w):
    r = w.reshape(MLA_Q_RANK, MLA_HEADS, MLA_NOPE + MLA_ROPE)
    r = jnp.pad(r, ((0, 0), (0, 0), (0, MLA_DQ - MLA_NOPE - MLA_ROPE)))
    return r.reshape(MLA_Q_RANK, MLA_HEADS * MLA_DQ).astype(BF16)


def _prep_router(w_grp, b_grp, w_exp, b_exp):
    D = w_grp.shape[0]
    pad = LANES - N_EXPERTS - N_GROUPS
    rwt = jnp.concatenate([w_exp, w_grp, jnp.zeros((D, pad), F32)], axis=1).astype(F32).T
    rwh = rwt.astype(BF16)
    rwl = (rwt - rwh.astype(F32)).astype(BF16)
    rb = jnp.concatenate([b_exp, b_grp, jnp.zeros((pad,), F32)]).astype(F32).reshape(LANES, 1)
    return rwh, rwl, rb


def kernel(x, c, positions, ada_w, ada_b, norm1_g, norm2_g, w_in, fox_fb, mla_q_norm_g, mla_wq_up, mla_kv_norm_g, mla_wkv_up, gate_w, gate_b, branch_w, out_w, router_grp_w, router_grp_b, router_exp_w, router_exp_b, exp_w1, exp_w3, exp_w2, final_g):
    B, S, D = x.shape
    L = ada_w.shape[0]
    T = B * S
    A = 2 * T
    tb = _pick(A, 256)
    n_blocks = A // tb + N_EXPERTS
    P = n_blocks * tb

    mod = _adaln(c, ada_w, ada_b)
    cos_t, sin_t = _rope_tables(positions)
    x2 = x.reshape(T, D)
    final_g2 = final_g.reshape(1, D)

    for l in range(L):
        sh1, sc1, gt1, sh2, sc2, gt2 = [mod[l, :, i * D:(i + 1) * D].reshape(B, 1, D) for i in range(6)]
        g1 = norm1_g[l].reshape(1, D)
        g2 = norm2_g[l].reshape(1, D)
        (rq, rk, rv, rg, fq, fk, fv, mq, mk, kpe, mv, ffp) = _proj(
            x2, S, g1, sh1, sc1, _prep_w_in(w_in[l]), cos_t, sin_t,
            mla_q_norm_g[l].reshape(1, -1), _prep_wq_up(mla_wq_up[l]),
            mla_kv_norm_g[l].reshape(1, -1), mla_wkv_up[l].astype(BF16))

        ya = _retention(rq, rk, rv, rg, B, S).reshape(T, -1)
        kb = _fox_gate(ffp, fox_fb[l], S)
        r3 = lambda a: a.reshape(B, S, -1)
        yb = _flash(r3(fq), r3(fk), r3(kb), r3(fv), B, S, FOX_HEADS, FOX_DH, True, 3, "flash_fox").reshape(T, -1)
        yc = _flash(r3(mq), r3(mk), r3(kpe), r3(mv), B, S, MLA_HEADS, MLA_DQ, False, 0, "flash_mla").reshape(T, -1)

        rwh, rwl, rb = _prep_router(router_grp_w[l], router_grp_b[l], router_exp_w[l], router_exp_b[l])
        x1, h2, mi, mf, cnt = _merge(
            x2, S, ya, yb, yc, g1, sh1, sc1, gt1, g2, sh2, sc2,
            gate_w[l].astype(BF16), gate_b[l].reshape(1, -1), branch_w[l].astype(BF16),
            out_w[l].astype(BF16), rwh, rwl, rb)

        counts = cnt[:, 0].astype(I32)
        pcounts = (counts + tb - 1) // tb * tb
        pends = jnp.cumsum(pcounts)
        pstarts = pends - pcounts
        sel = mi[0:2, :, None] == jnp.arange(N_EXPERTS, dtype=I32)
        dest = (jnp.sum(jnp.where(sel, pstarts, 0), axis=-1) + mi[2:4]).reshape(A)
        blk_pos = jnp.arange(n_blocks, dtype=I32) * tb
        blk_e = jnp.minimum(jnp.sum((pends[None, :] <= blk_pos[:, None]).astype(I32), axis=1), N_EXPERTS - 1)
        nvalid = (pends[-1:] // tb).astype(I32)
        tok = jnp.tile(jnp.arange(T, dtype=I32), 2)
        src_tok = jnp.zeros((P,), I32).at[dest].set(tok, unique_indices=True)

        xs = _sc_gather(h2, src_tok)
        ys = _experts(xs, blk_e, nvalid, exp_w1[l].astype(BF16), exp_w3[l].astype(BF16),
                      exp_w2[l].astype(BF16), tb)
        g = _sc_gather(ys, dest)
        x2 = _combine(x1, S, mf[0:2].T, gt2, final_g2, g, final=(l == L - 1))

    return x2.reshape(B, S, D)
```

```python
import functools
import math

import jax
import jax.numpy as jnp
from jax import lax
from jax.experimental import pallas as pl
from jax.experimental.pallas import tpu as pltpu
from jax.experimental.pallas import tpu_sc as plsc

F32 = jnp.float32
BF16 = jnp.bfloat16
I32 = jnp.int32
U32 = jnp.uint32
HIGHEST = lax.Precision.HIGHEST

EPS = 1e-6
ROPE_THETA = 10000.0
RET_HEADS = 4
RET_DK = 64
RET_DV = 128
RET_CHUNK = 128
FOX_HEADS = 4
FOX_DH = 128
MLA_HEADS = 4
MLA_Q_RANK = 256
MLA_KV_RANK = 128
MLA_NOPE = 128
MLA_ROPE = 64
MLA_V = 128
MLA_DQ = 256
N_BRANCH = 3
BRANCH_W = 512
N_GROUPS = 4
EXP_PER_GROUP = 8
N_EXPERTS = N_GROUPS * EXP_PER_GROUP
D_EXPERT = 512

LANES = 128
V7X_VMEM_LIMIT = 56 * 1024 * 1024

C_RQ, C_RK, C_RV, C_RG = 0, 256, 512, 1024
C_FQ, C_FK, C_FV = 1536, 2048, 2560
C_MQ, C_MKV, C_TAIL = 3072, 3328, 3456
D_IN_PAD = 3584
FF_LANE = MLA_ROPE

NEG_BIG = -1e30
SC_GATHER_WINDOW = 128
MERGE_COL_CHUNK = 256
FLASH_ROW_BLOCK = 128
LOG2E = math.log2(math.e)


def _cparams(sem):
    return pltpu.CompilerParams(dimension_semantics=sem, vmem_limit_bytes=V7X_VMEM_LIMIT)


def _pick(n, pref):
    t = min(n, pref)
    assert n % t == 0, (n, t)
    return t


def _adaln_kernel(c_ref, w_ref, b_ref, o_ref):
    c = c_ref[...]
    ca = c * jax.nn.sigmoid(c)
    o_ref[...] = jnp.dot(ca, w_ref[...], preferred_element_type=F32, precision=HIGHEST) + b_ref[...]


def _adaln(c, ada_w, ada_b):
    L, D, N = ada_w.shape
    B = c.shape[0]
    tn = _pick(N, 1536)
    return pl.pallas_call(
        _adaln_kernel,
        grid=(L, N // tn),
        in_specs=[pl.BlockSpec((B, D), lambda l, j: (0, 0)),
                  pl.BlockSpec((None, D, tn), lambda l, j: (l, 0, j)),
                  pl.BlockSpec((None, 1, tn), lambda l, j: (l, 0, j))],
        out_specs=pl.BlockSpec((None, B, tn), lambda l, j: (l, 0, j)),
        out_shape=jax.ShapeDtypeStruct((L, B, N), F32),
        compiler_params=_cparams(("arbitrary", "arbitrary")),
        name="adaln",
    )(c, ada_w, ada_b.reshape(L, 1, N))


def _rope_table_kernel(pos_ref, inv_ref, sign_ref, cos_ref, sin_ref):
    ang = pos_ref[...].astype(F32) * inv_ref[...]
    cos_ref[...] = jnp.cos(ang)
    sin_ref[...] = jnp.sin(ang) * sign_ref[...]


def _rope_tables(positions):
    T = positions.size
    tm = _pick(T, 2048)
    half = MLA_ROPE // 2
    inv = ROPE_THETA ** (-jnp.arange(0, MLA_ROPE, 2, dtype=F32) / MLA_ROPE)
    inv_t = jnp.tile(inv, LANES // half).reshape(1, LANES)
    sign = jnp.where((jnp.arange(LANES) % MLA_ROPE) < half, -1.0, 1.0).astype(F32).reshape(1, LANES)
    return pl.pallas_call(
        _rope_table_kernel,
        grid=(T // tm,),
        in_specs=[pl.BlockSpec((tm, 1), lambda i: (i, 0)),
                  pl.BlockSpec((1, LANES), lambda i: (0, 0)),
                  pl.BlockSpec((1, LANES), lambda i: (0, 0))],
        out_specs=[pl.BlockSpec((tm, LANES), lambda i: (i, 0))] * 2,
        out_shape=[jax.ShapeDtypeStruct((T, LANES), F32)] * 2,
        compiler_params=_cparams(("arbitrary",)),
        name="rope_tables",
    )(positions.reshape(T, 1), inv_t, sign)


def _rope_slab(x, cos_t, sin_t, lane):
    nxt = pltpu.roll(x, LANES - 32, axis=1)
    prv = pltpu.roll(x, 32, axis=1)
    swapped = jnp.where((lane & 32) == 0, nxt, prv)
    return x * cos_t + swapped * sin_t


def _pack_bf16_pair(lo, hi):
    lo_bits = lax.shift_right_logical(lax.bitcast_convert_type(lo.astype(BF16).astype(F32), U32), jnp.uint32(16))
    hi_bits = lax.bitcast_convert_type(hi.astype(BF16).astype(F32), U32) & jnp.uint32(0xFFFF0000)
    return hi_bits | lo_bits


def _unpack_bf16_pair(w):
    lo = lax.bitcast_convert_type(lax.shift_left(w, jnp.uint32(16)), F32)
    hi = lax.bitcast_convert_type(w & jnp.uint32(0xFFFF0000), F32)
    return lo, hi


def _norm_mod(x, g, shift, scale):
    y = x * lax.rsqrt(jnp.mean(x * x, axis=-1, keepdims=True) + EPS)
    return (y * g) * (1.0 + scale) + shift


def _rms(x, g):
    return x * lax.rsqrt(jnp.mean(x * x, axis=-1, keepdims=True) + EPS) * g


def _proj_kernel(x_ref, g_ref, sh_ref, sc_ref, w_ref, cos_ref, sin_ref,
                 gq_ref, wq_ref, gkv_ref, wkv_ref,
                 rq_ref, rk_ref, rv_ref, rg_ref, fq_ref, fk_ref, fv_ref,
                 mq_ref, mk_ref, kpe_ref, mv_ref, ff_ref):
    h = _norm_mod(x_ref[...], g_ref[...], sh_ref[...], sc_ref[...]).astype(BF16)
    cos_t = cos_ref[...]
    sin_t = sin_ref[...]
    lane = lax.broadcasted_iota(I32, cos_t.shape, 1)

    def proj(c0, width):
        return jnp.dot(h, w_ref[:, c0:c0 + width], preferred_element_type=F32)

    rq = proj(C_RQ, 256)
    rk = proj(C_RK, 256)
    for s in range(2):
        sl = slice(s * LANES, (s + 1) * LANES)
        rq_ref[:, sl] = _rope_slab(rq[:, sl], cos_t, sin_t, lane).astype(BF16)
        rk_ref[:, sl] = (_rope_slab(rk[:, sl], cos_t, sin_t, lane) * (RET_DK ** -0.5)).astype(BF16)
    rv_ref[...] = proj(C_RV, 512).astype(BF16)
    rg_ref[...] = proj(C_RG, 512).astype(BF16)
    fq_ref[...] = (proj(C_FQ, 512) * (FOX_DH ** -0.5 * LOG2E)).astype(BF16)
    fk_ref[...] = proj(C_FK, 512).astype(BF16)
    fv_ref[...] = proj(C_FV, 512).astype(BF16)

    tail = proj(C_TAIL, LANES)
    ff_ref[...] = tail
    kpe_ref[...] = jnp.where(lane < MLA_ROPE, _rope_slab(tail, cos_t, sin_t, lane), 0.0).astype(BF16)

    qn = _rms(proj(C_MQ, MLA_Q_RANK), gq_ref[...]).astype(BF16)
    qh = jnp.dot(qn, wq_ref[...], preferred_element_type=F32)
    q_scale = (MLA_NOPE + MLA_ROPE) ** -0.5 * LOG2E
    for hd in range(MLA_HEADS):
        c0 = hd * MLA_DQ
        mq_ref[:, c0:c0 + LANES] = (qh[:, c0:c0 + LANES] * q_scale).astype(BF16)
        pe = _rope_slab(qh[:, c0 + LANES:c0 + 2 * LANES], cos_t, sin_t, lane)
        mq_ref[:, c0 + LANES:c0 + 2 * LANES] = jnp.where(lane < MLA_ROPE, pe * q_scale, 0.0).astype(BF16)

    kvn = _rms(proj(C_MKV, MLA_KV_RANK), gkv_ref[...]).astype(BF16)
    kvh = jnp.dot(kvn, wkv_ref[...], preferred_element_type=F32)
    for hd in range(MLA_HEADS):
        c0 = hd * (MLA_NOPE + MLA_V)
        mk_ref[:, hd * MLA_NOPE:(hd + 1) * MLA_NOPE] = kvh[:, c0:c0 + MLA_NOPE].astype(BF16)
        mv_ref[:, hd * MLA_V:(hd + 1) * MLA_V] = kvh[:, c0 + MLA_NOPE:c0 + MLA_NOPE + MLA_V].astype(BF16)


def _proj(x2, S, g, sh, sc, w_all, cos_t, sin_t, gq, wq, gkv, wkv):
    T, D = x2.shape
    tm = _pick(S, 512)
    per_b = S // tm
    row = lambda i: (i, 0)
    const = lambda i: (0, 0)
    batch = lambda i: (i // per_b, 0, 0)
    widths = [256, 256, 512, 512, 512, 512, 512, MLA_HEADS * MLA_DQ, MLA_HEADS * MLA_NOPE, LANES,
              MLA_HEADS * MLA_V]
    out_shape = [jax.ShapeDtypeStruct((T, w), BF16) for w in widths]
    out_shape.append(jax.ShapeDtypeStruct((T, LANES), F32))
    out_specs = [pl.BlockSpec((tm, w), row) for w in widths] + [pl.BlockSpec((tm, LANES), row)]
    return pl.pallas_call(
        _proj_kernel,
        grid=(T // tm,),
        in_specs=[pl.BlockSpec((tm, D), row),
                  pl.BlockSpec((1, D), const),
                  pl.BlockSpec((None, 1, D), batch),
                  pl.BlockSpec((None, 1, D), batch),
                  pl.BlockSpec((D, D_IN_PAD), const),
                  pl.BlockSpec((tm, LANES), row),
                  pl.BlockSpec((tm, LANES), row),
                  pl.BlockSpec((1, MLA_Q_RANK), const),
                  pl.BlockSpec((MLA_Q_RANK, MLA_HEADS * MLA_DQ), const),
                  pl.BlockSpec((1, MLA_KV_RANK), const),
                  pl.BlockSpec((MLA_KV_RANK, MLA_HEADS * (MLA_NOPE + MLA_V)), const)],
        out_specs=out_specs,
        out_shape=out_shape,
        compiler_params=_cparams(("arbitrary",)),
        name="proj",
    )(x2, g, sh, sc, w_all, cos_t, sin_t, gq, wq, gkv, wkv)


def _split3(x):
    a = x.astype(BF16)
    r = x - a.astype(F32)
    b = r.astype(BF16)
    c = (r - b.astype(F32)).astype(BF16)
    return a, b, c


def _fox_gate_kernel(ff_ref, fb_ref, tri_ref, kb_ref, carry_ref, *, tiles_per_seq):
    @pl.when(pl.program_id(0) % tiles_per_seq == 0)
    def _():
        carry_ref[...] = jnp.zeros_like(carry_ref)

    tm = ff_ref.shape[0]
    lane = lax.broadcasted_iota(I32, (tm, LANES), 1)
    z = ff_ref[...] + fb_ref[...]
    ls = -(jnp.maximum(-z, 0.0) + jnp.log1p(jnp.exp(-jnp.abs(z))))
    ls = jnp.where((lane >= FF_LANE) & (lane < FF_LANE + FOX_HEADS), ls, 0.0)
    tri = tri_ref[...]
    f = carry_ref[...]
    for part in _split3(ls):
        f = f + jnp.dot(tri, part, preferred_element_type=F32)
    carry_ref[...] = f[tm - 1:tm, :]
    f2 = f * LOG2E
    for hd in range(FOX_HEADS):
        src = FF_LANE + hd
        g = jnp.where(lane == 0, pltpu.roll(f2, (LANES - src) % LANES, axis=1),
            jnp.where(lane == 1, pltpu.roll(f2, (LANES + 1 - src) % LANES, axis=1),
            jnp.where(lane == 2, pltpu.roll(f2, (LANES + 2 - src) % LANES, axis=1), 0.0)))
        hi, mid, lo = _split3(g)
        kb_ref[:, hd * LANES:(hd + 1) * LANES] = jnp.where(lane == 0, hi, jnp.where(lane == 1, mid, lo))


def _fox_gate(ffp, fb, S):
    T = ffp.shape[0]
    tm = _pick(S, 512)
    idx = jnp.arange(tm)
    tri = (idx[None, :] <= idx[:, None]).astype(BF16)
    fbv = jnp.zeros((1, LANES), F32).at[0, FF_LANE:FF_LANE + FOX_HEADS].set(fb)
    return pl.pallas_call(
        functools.partial(_fox_gate_kernel, tiles_per_seq=S // tm),
        grid=(T // tm,),
        in_specs=[pl.BlockSpec((tm, LANES), lambda i: (i, 0)),
                  pl.BlockSpec((1, LANES), lambda i: (0, 0)),
                  pl.BlockSpec((tm, tm), lambda i: (0, 0))],
        out_specs=pl.BlockSpec((tm, FOX_HEADS * LANES), lambda i: (i, 0)),
        out_shape=jax.ShapeDtypeStruct((T, FOX_HEADS * LANES), BF16),
        scratch_shapes=[pltpu.VMEM((1, LANES), F32)],
        compiler_params=_cparams(("arbitrary",)),
        name="fox_gate",
    )(ffp, fbv, tri)


def _retention_kernel(dchunk_ref, q_ref, k_ref, v_ref, g_ref, dmask_ref, din_ref, dout_ref,
                      o_ref, state_ref, *, n_chunks):
    @pl.when(pl.program_id(1) == 0)
    def _():
        state_ref[...] = jnp.zeros_like(state_ref)

    C = RET_CHUNK
    lane = lax.broadcasted_iota(I32, (C, LANES), 1)
    for ci in range(n_chunks):
        rows = slice(ci * C, (ci + 1) * C)
        for hd in range(RET_HEADS):
            slab = slice((hd // 2) * LANES, (hd // 2 + 1) * LANES)
            mine = (lane < RET_DK) if hd % 2 == 0 else (lane >= RET_DK)
            q = jnp.where(mine, q_ref[rows, slab], 0)
            k = jnp.where(mine, k_ref[rows, slab], 0)
            vcols = slice(hd * RET_DV, (hd + 1) * RET_DV)
            v = v_ref[rows, vcols]
            state = state_ref[hd]
            scores = lax.dot_general(q, k, (((1,), (1,)), ((), ())),
                                     preferred_element_type=F32) * dmask_ref[hd]
            inner = jnp.dot(scores.astype(BF16), v, preferred_element_type=F32)
            cross = jnp.dot(q, state.astype(BF16), preferred_element_type=F32) * din_ref[hd]
            o = inner + cross
            vd = (v.astype(F32) * dout_ref[hd]).astype(BF16)
            kv = lax.dot_general(k, vd, (((0,), (0,)), ((), ())), preferred_element_type=F32)
            state_ref[hd] = state * dchunk_ref[hd] + kv
            mu = jnp.mean(o, axis=-1, keepdims=True)
            d = o - mu
            var = jnp.mean(d * d, axis=-1, keepdims=True)
            on = d * lax.rsqrt(var + EPS)
            g = g_ref[rows, vcols].astype(F32)
            o_ref[rows, vcols] = (g * jax.nn.sigmoid(g) * on).astype(BF16)


def _retention(rq, rk, rv, rg, B, S):
    H, C = RET_HEADS, RET_CHUNK
    tr = _pick(S, 4 * C)
    n_chunks = tr // C
    log_gamma = jnp.log1p(-jnp.exp2(-5.0 - jnp.arange(H, dtype=F32)))
    idx = jnp.arange(C, dtype=F32)
    rel = idx[:, None] - idx[None, :]
    dmask = jnp.where(rel >= 0, jnp.exp(log_gamma[:, None, None] * jnp.maximum(rel, 0.0)), 0.0)
    decay_in = jnp.exp(log_gamma[:, None] * (idx + 1.0))
    decay_out = jnp.exp(log_gamma[:, None] * (C - 1.0 - idx))
    decay_chunk = jnp.exp(log_gamma * C)
    din = jnp.broadcast_to(decay_in[:, :, None], (H, C, RET_DV))
    dout = jnp.broadcast_to(decay_out[:, :, None], (H, C, RET_DV))
    tok = lambda b, i: (b, i, 0)
    const3 = lambda b, i: (0, 0, 0)
    return pl.pallas_call(
        functools.partial(_retention_kernel, n_chunks=n_chunks),
        grid=(B, S // tr),
        in_specs=[pl.BlockSpec(memory_space=pltpu.SMEM),
                  pl.BlockSpec((None, tr, H * RET_DK), tok),
                  pl.BlockSpec((None, tr, H * RET_DK), tok),
                  pl.BlockSpec((None, tr, H * RET_DV), tok),
                  pl.BlockSpec((None, tr, H * RET_DV), tok),
                  pl.BlockSpec((H, C, C), const3),
                  pl.BlockSpec((H, C, RET_DV), const3),
                  pl.BlockSpec((H, C, RET_DV), const3)],
        out_specs=pl.BlockSpec((None, tr, H * RET_DV), tok),
        out_shape=jax.ShapeDtypeStruct((B, S, H * RET_DV), BF16),
        scratch_shapes=[pltpu.VMEM((H, LANES, RET_DV), F32)],
        compiler_params=_cparams(("arbitrary", "arbitrary")),
        name="retention",
    )(decay_chunk, rq.reshape(B, S, -1), rk.reshape(B, S, -1), rv.reshape(B, S, -1),
      rg.reshape(B, S, -1), dmask, din, dout)


def _flash_kernel(q_ref, k_ref, ke_ref, v_ref, o_ref, m_ref, l_ref, acc_ref, s_ref, p_ref, *, q_bias_cols):
    qi = pl.program_id(2)
    tq = q_ref.shape[0]
    tk = tq
    reps = tk // LANES
    m_ref[...] = jnp.full(m_ref.shape, NEG_BIG, F32)
    l_ref[...] = jnp.zeros_like(l_ref)
    acc_ref[...] = jnp.zeros_like(acc_ref)
    q = q_ref[...]
    if q_bias_cols:
        lane = lax.broadcasted_iota(I32, (tq, LANES), 1)
        q = jnp.concatenate([q, jnp.where(lane < q_bias_cols, -1.0, 0.0).astype(BF16)], axis=1)

    def scores(j):
        start = pl.multiple_of(j * tk, tk)
        kj = jnp.concatenate([k_ref[pl.ds(start, tk), :], ke_ref[pl.ds(start, tk), :]], axis=1)
        return lax.dot_general(q, kj, (((1,), (1,)), ((), ())), preferred_element_type=F32)

    def update(slot, j, diag):
        start = pl.multiple_of(j * tk, tk)
        for rb in range(tq // FLASH_ROW_BLOCK):
            rows = slice(rb * FLASH_ROW_BLOCK, (rb + 1) * FLASH_ROW_BLOCK)
            s = s_ref[slot, rows, :]
            if diag:
                r = lax.broadcasted_iota(I32, s.shape, 0) + rb * FLASH_ROW_BLOCK
                c = lax.broadcasted_iota(I32, s.shape, 1)
                s = jnp.where(c <= r, s, NEG_BIG)
            m_prev = m_ref[rows, :]
            m_new = jnp.maximum(m_prev, jnp.max(s, axis=-1, keepdims=True))
            alpha = jnp.exp2(m_prev - m_new)
            p = jnp.exp2(s - jnp.concatenate([m_new] * reps, axis=1))
            l_ref[rows, :] = alpha * l_ref[rows, :] + jnp.sum(p, axis=-1, keepdims=True)
            acc_ref[rows, :] = alpha * acc_ref[rows, :]
            m_ref[rows, :] = m_new
            p_ref[rows, :] = p.astype(BF16)
        acc_ref[...] += jnp.dot(p_ref[...], v_ref[pl.ds(start, tk), :], preferred_element_type=F32)

    s_ref[0] = scores(0)

    def pair(p, carry):
        j = 2 * p
        s_ref[1] = scores(j + 1)
        update(0, j, False)
        s_ref[0] = scores(j + 2)
        update(1, j + 1, False)
        return carry

    lax.fori_loop(0, qi // 2, pair, 0)

    @pl.when(qi % 2 == 1)
    def _():
        s_ref[1] = scores(qi)
        update(0, qi - 1, False)
        update(1, qi, True)

    @pl.when(qi % 2 == 0)
    def _():
        update(0, qi, True)

    o_ref[...] = (acc_ref[...] / l_ref[...]).astype(o_ref.dtype)


def _flash(q, k, ke, v, B, S, H, dq, ke_per_head, q_bias_cols, name):
    tq = _pick(S, 512)
    nq = S // tq
    ke_map = (lambda b, h, i: (b, 0, h)) if ke_per_head else (lambda b, h, i: (b, 0, 0))
    return pl.pallas_call(
        functools.partial(_flash_kernel, q_bias_cols=q_bias_cols),
        grid=(B, H, nq),
        in_specs=[pl.BlockSpec((None, tq, dq), lambda b, h, i: (b, i, h)),
                  pl.BlockSpec((None, S, LANES), lambda b, h, i: (b, 0, h)),
                  pl.BlockSpec((None, S, LANES), ke_map),
                  pl.BlockSpec((None, S, LANES), lambda b, h, i: (b, 0, h))],
        out_specs=pl.BlockSpec((None, tq, LANES), lambda b, h, i: (b, i, h)),
        out_shape=jax.ShapeDtypeStruct((B, S, H * LANES), BF16),
        scratch_shapes=[pltpu.VMEM((tq, LANES), F32), pltpu.VMEM((tq, LANES), F32),
                        pltpu.VMEM((tq, LANES), F32), pltpu.VMEM((2, tq, tq), F32),
                        pltpu.VMEM((tq, tq), BF16)],
        compiler_params=_cparams(("arbitrary", "arbitrary", "arbitrary")),
        name=name,
    )(q, k, ke, v)


def _merge_kernel(x_ref, ya_ref, yb_ref, yc_ref, g1_ref, sh1_ref, sc1_ref, gt1_ref,
                  g2_ref, sh2_ref, sc2_ref, gw_ref, gb_ref, bw_ref, ow_ref,
                  rwh_ref, rwl_ref, rb_ref, tri_ref,
                  x1_ref, h2_ref, mi_ref, mf_ref, cnt_ref, h_ref, mg_ref, carry_ref):
    tm, D = x_ref.shape
    cn = MERGE_COL_CHUNK

    @pl.when(pl.program_id(0) == 0)
    def _():
        carry_ref[...] = jnp.zeros_like(carry_ref)

    h_ref[...] = _norm_mod(x_ref[...], g1_ref[...], sh1_ref[...], sc1_ref[...]).astype(BF16)
    for n in range(D // cn):
        cols = slice(n * cn, (n + 1) * cn)
        merged = None
        for i, y_ref in enumerate((ya_ref, yb_ref, yc_ref)):
            gcols = slice(i * D + n * cn, i * D + (n + 1) * cn)
            gate = jax.nn.sigmoid(jnp.dot(h_ref[...], gw_ref[:, gcols], preferred_element_type=F32)
                                  + gb_ref[:, gcols])
            br = jnp.dot(y_ref[...], bw_ref[i, :, cols], preferred_element_type=F32)
            merged = gate * br if merged is None else merged + gate * br
        mg_ref[:, cols] = merged.astype(BF16)
    for n in range(D // cn):
        cols = slice(n * cn, (n + 1) * cn)
        mix = jnp.dot(mg_ref[...], ow_ref[:, cols], preferred_element_type=F32)
        x1_ref[:, cols] = x_ref[:, cols] + gt1_ref[:, cols] * mix
    h2 = _norm_mod(x1_ref[...], g2_ref[...], sh2_ref[...], sc2_ref[...])
    h2_ref[...] = _pack_bf16_pair(h2[:, :D // 2], h2[:, D // 2:])

    hh = h2.astype(BF16)
    hl = (h2 - hh.astype(F32)).astype(BF16)
    nt = lambda a, b: lax.dot_general(a, b, (((1,), (1,)), ((), ())), preferred_element_type=F32)
    lt = nt(rwh_ref[...], hh) + nt(rwh_ref[...], hl) + nt(rwl_ref[...], hh) + rb_ref[...]
    row8 = lax.broadcasted_iota(I32, (EXP_PER_GROUP, tm), 0)
    gl = jnp.where(row8 < N_GROUPS, lt[N_EXPERTS:N_EXPERTS + EXP_PER_GROUP, :], -jnp.inf)
    gmax = jnp.max(gl, axis=0, keepdims=True)
    g_idx = jnp.min(jnp.where(gl == gmax, row8, EXP_PER_GROUP), axis=0, keepdims=True)
    g_w = 1.0 / jnp.sum(jnp.exp(gl - gmax), axis=0, keepdims=True)
    el = lt[(N_GROUPS - 1) * EXP_PER_GROUP:N_EXPERTS, :]
    for g in range(N_GROUPS - 2, -1, -1):
        el = jnp.where(g_idx == g, lt[g * EXP_PER_GROUP:(g + 1) * EXP_PER_GROUP, :], el)
    e1 = jnp.max(el, axis=0, keepdims=True)
    i1 = jnp.min(jnp.where(el == e1, row8, EXP_PER_GROUP), axis=0, keepdims=True)
    el2 = jnp.where(row8 == i1, -jnp.inf, el)
    e2 = jnp.max(el2, axis=0, keepdims=True)
    i2 = jnp.min(jnp.where(el2 == e2, row8, EXP_PER_GROUP), axis=0, keepdims=True)
    r = jnp.exp(e2 - e1)
    w1 = g_w / (1.0 + r)
    w2 = g_w * r / (1.0 + r)
    eid1 = g_idx * EXP_PER_GROUP + i1
    eid2 = g_idx * EXP_PER_GROUP + i2

    rowe = lax.broadcasted_iota(I32, (N_EXPERTS, tm), 0)
    hit1 = rowe == eid1
    hit2 = rowe == eid2
    onehot = jnp.where(hit1 | hit2, 1.0, 0.0)
    before = (jnp.dot(onehot.astype(BF16), tri_ref[...], preferred_element_type=F32)
              + jnp.concatenate([carry_ref[...]] * (tm // LANES), axis=1))
    rank1 = jnp.sum(jnp.where(hit1, before, 0.0), axis=0, keepdims=True)
    rank2 = jnp.sum(jnp.where(hit2, before, 0.0), axis=0, keepdims=True)
    carry_ref[...] = carry_ref[...] + jnp.sum(onehot, axis=1, keepdims=True)
    cnt_ref[...] = carry_ref[...]

    mi_ref[...] = jnp.where(row8 == 0, eid1,
                  jnp.where(row8 == 1, eid2,
                  jnp.where(row8 == 2, rank1.astype(I32),
                  jnp.where(row8 == 3, rank2.astype(I32), 0))))
    mf_ref[...] = jnp.where(row8 == 0, w1, jnp.where(row8 == 1, w2, 0.0))


def _merge(x2, S, ya, yb, yc, g1, sh1, sc1, gt1, g2, sh2, sc2, gw, gb, bw, ow, rwh, rwl, rb):
    T, D = x2.shape
    tm = _pick(S, 512)
    per_b = S // tm
    row = lambda i: (i, 0)
    col = lambda i: (0, i)
    const = lambda i: (0, 0)
    batch = lambda i: (i // per_b, 0, 0)
    idx = jnp.arange(tm)
    tri = (idx[:, None] < idx[None, :]).astype(BF16)
    vecb = pl.BlockSpec((None, 1, D), batch)
    return pl.pallas_call(
        _merge_kernel,
        grid=(T // tm,),
        in_specs=[pl.BlockSpec((tm, D), row),
                  pl.BlockSpec((tm, BRANCH_W), row),
                  pl.BlockSpec((tm, BRANCH_W), row),
                  pl.BlockSpec((tm, BRANCH_W), row),
                  pl.BlockSpec((1, D), const), vecb, vecb, vecb,
                  pl.BlockSpec((1, D), const), vecb, vecb,
                  pl.BlockSpec((D, N_BRANCH * D), const),
                  pl.BlockSpec((1, N_BRANCH * D), const),
                  pl.BlockSpec((N_BRANCH, BRANCH_W, D), lambda i: (0, 0, 0)),
                  pl.BlockSpec((D, D), const),
                  pl.BlockSpec((LANES, D), const),
                  pl.BlockSpec((LANES, D), const),
                  pl.BlockSpec((LANES, 1), const),
                  pl.BlockSpec((tm, tm), const)],
        out_specs=[pl.BlockSpec((tm, D), row), pl.BlockSpec((tm, D // 2), row),
                   pl.BlockSpec((8, tm), col), pl.BlockSpec((8, tm), col),
                   pl.BlockSpec((N_EXPERTS, LANES), const)],
        out_shape=[jax.ShapeDtypeStruct((T, D), F32), jax.ShapeDtypeStruct((T, D // 2), U32),
                   jax.ShapeDtypeStruct((8, T), I32), jax.ShapeDtypeStruct((8, T), F32),
                   jax.ShapeDtypeStruct((N_EXPERTS, LANES), F32)],
        scratch_shapes=[pltpu.VMEM((tm, D), BF16), pltpu.VMEM((tm, D), BF16),
                        pltpu.VMEM((N_EXPERTS, LANES), F32)],
        compiler_params=_cparams(("arbitrary",)),
        name="merge",
    )(x2, ya, yb, yc, g1, sh1, sc1, gt1, g2, sh2, sc2, gw, gb, bw, ow, rwh, rwl, rb, tri)


def _expert_kernel(blk_e_ref, nvalid_ref, xs_ref, w1_ref, w3_ref, w2_ref, ys_ref):
    del blk_e_ref
    valid = pl.program_id(0) < nvalid_ref[0]

    @pl.when(valid)
    def _():
        half = xs_ref.shape[1]
        x_lo, x_hi = _unpack_bf16_pair(xs_ref[...])
        x_lo = x_lo.astype(BF16)
        x_hi = x_hi.astype(BF16)

        def up(w_ref):
            return (jnp.dot(x_lo, w_ref[:half, :], preferred_element_type=F32)
                    + jnp.dot(x_hi, w_ref[half:, :], preferred_element_type=F32))

        a = up(w1_ref)
        b = up(w3_ref)
        hid = (a * jax.nn.sigmoid(a) * b).astype(BF16)
        y = jnp.dot(hid, w2_ref[...], preferred_element_type=F32)
        ys_ref[...] = _pack_bf16_pair(y[:, :half], y[:, half:])

    @pl.when(jnp.logical_not(valid))
    def _():
        ys_ref[...] = jnp.zeros_like(ys_ref)


def _experts(xs, blk_e, nvalid, w1, w3, w2, tb):
    P, half = xs.shape
    D = 2 * half
    n_blocks = P // tb
    rows = lambda i, be, nv: (jnp.minimum(i, nv[0] - 1), 0)
    grid_spec = pltpu.PrefetchScalarGridSpec(
        num_scalar_prefetch=2,
        grid=(n_blocks,),
        in_specs=[pl.BlockSpec((tb, half), rows),
                  pl.BlockSpec((None, D, D_EXPERT), lambda i, be, nv: (be[i], 0, 0)),
                  pl.BlockSpec((None, D, D_EXPERT), lambda i, be, nv: (be[i], 0, 0)),
                  pl.BlockSpec((None, D_EXPERT, D), lambda i, be, nv: (be[i], 0, 0))],
        out_specs=pl.BlockSpec((tb, half), lambda i, be, nv: (i, 0)),
    )
    return pl.pallas_call(
        _expert_kernel,
        grid_spec=grid_spec,
        out_shape=jax.ShapeDtypeStruct((P, half), U32),
        compiler_params=_cparams(("arbitrary",)),
        name="moe_experts",
    )(blk_e, nvalid, xs, w1, w3, w2)


def _sc_gather(data, idx):
    M = idx.shape[0]
    D = data.shape[1]
    W = SC_GATHER_WINDOW
    assert M % W == 0, (M, W)
    mesh = plsc.VectorSubcoreMesh(core_axis_name="core", subcore_axis_name="subcore")
    n_workers = mesh.num_cores * mesh.num_subcores
    assert M % (W * n_workers) == 0, (M, W, n_workers)

    @functools.partial(pl.kernel, out_type=jax.ShapeDtypeStruct((M, D), data.dtype), mesh=mesh,
                       scratch_types=[pltpu.VMEM((W,), I32), pltpu.VMEM((W, D), data.dtype)])
    def gather_kernel(x_hbm, i_hbm, o_hbm, i_vmem, buf):
        worker = lax.axis_index("core") * mesh.num_subcores + lax.axis_index("subcore")

        @pl.loop(0, M // (W * n_workers))
        def _(t):
            start = (t * n_workers + worker) * W
            pltpu.sync_copy(i_hbm.at[pl.ds(start, W)], i_vmem)
            pltpu.sync_copy(x_hbm.at[i_vmem], buf)
            pltpu.sync_copy(buf, o_hbm.at[pl.ds(start, W)])

    return gather_kernel(data, idx)


def _sc_dispatch(h2, dest, fill_idx, P):
    T, D = h2.shape
    W = SC_GATHER_WINDOW
    n_fill = fill_idx.shape[0]
    mesh = plsc.VectorSubcoreMesh(core_axis_name="core", subcore_axis_name="subcore")
    n_workers = mesh.num_cores * mesh.num_subcores
    assert T % (W * n_workers) == 0 and n_fill % (W * n_workers) == 0, (T, n_fill, W, n_workers)
    zeros = jnp.zeros((W, D), h2.dtype)

    @functools.partial(pl.kernel, out_type=jax.ShapeDtypeStruct((P, D), h2.dtype), mesh=mesh,
                       scratch_types=[pltpu.VMEM((W,), I32), pltpu.VMEM((W, D), h2.dtype)])
    def dispatch_kernel(h_hbm, d_hbm, f_hbm, z_hbm, o_hbm, i_vmem, buf):
        worker = lax.axis_index("core") * mesh.num_subcores + lax.axis_index("subcore")

        @pl.loop(0, T // (W * n_workers))
        def _(t):
            start = (t * n_workers + worker) * W
            pltpu.sync_copy(h_hbm.at[pl.ds(start, W)], buf)
            for c in range(2):
                pltpu.sync_copy(d_hbm.at[pl.ds(c * T + start, W)], i_vmem)
                pltpu.sync_copy(buf, o_hbm.at[i_vmem])

        pltpu.sync_copy(z_hbm, buf)

        @pl.loop(0, n_fill // (W * n_workers))
        def _(t):
            start = (t * n_workers + worker) * W
            pltpu.sync_copy(f_hbm.at[pl.ds(start, W)], i_vmem)
            pltpu.sync_copy(buf, o_hbm.at[i_vmem])

    return dispatch_kernel(h2, dest, fill_idx, zeros)


def _combine_kernel(x_ref, g0_ref, g1_ref, mf_ref, gt_ref, fg_ref, o_ref, *, final):
    mf = mf_ref[...]
    lo0, hi0 = _unpack_bf16_pair(g0_ref[...])
    lo1, hi1 = _unpack_bf16_pair(g1_ref[...])
    w0 = mf[:, 0:1]
    w1 = mf[:, 1:2]
    ffn = jnp.concatenate([lo0 * w0 + lo1 * w1, hi0 * w0 + hi1 * w1], axis=1)
    out = x_ref[...] + gt_ref[...] * ffn
    if final:
        out = _rms(out, fg_ref[...])
    o_ref[...] = out


def _combine(x1, S, mf, gt2, final_g, g, final):
    T, D = x1.shape
    tm = _pick(S, 512)
    per_b = S // tm
    nt = T // tm
    return pl.pallas_call(
        functools.partial(_combine_kernel, final=final),
        grid=(nt,),
        in_specs=[pl.BlockSpec((tm, D), lambda i: (i, 0)),
                  pl.BlockSpec((tm, D // 2), lambda i: (i, 0)),
                  pl.BlockSpec((tm, D // 2), lambda i: (i + nt, 0)),
                  pl.BlockSpec((tm, 2), lambda i: (i, 0)),
                  pl.BlockSpec((None, 1, D), lambda i: (i // per_b, 0, 0)),
                  pl.BlockSpec((1, D), lambda i: (0, 0))],
        out_specs=pl.BlockSpec((tm, D), lambda i: (i, 0)),
        out_shape=jax.ShapeDtypeStruct((T, D), F32),
        compiler_params=_cparams(("arbitrary",)),
        name="moe_combine",
    )(x1, g, g, mf, gt2, final_g)


def _prep_w_in(w):
    offs = [0]
    for s in (256, 256, 512, 512, 512, 512, 512, FOX_HEADS, MLA_Q_RANK, MLA_KV_RANK, MLA_ROPE):
        offs.append(offs[-1] + s)
    rq, rk, rv, rg, fq, fk, fv, ff, mq, mkv, mkr = [w[:, offs[i]:offs[i + 1]] for i in range(11)]
    pad = jnp.zeros((w.shape[0], LANES - MLA_ROPE - FOX_HEADS), w.dtype)
    return jnp.concatenate([rq, rk, rv, rg, fq, fk, fv, mq, mkv, mkr, ff, pad], axis=1).astype(BF16)


def _prep_wq_up(w):
    r = w.reshape(MLA_Q_RANK, MLA_HEADS, MLA_NOPE + MLA_ROPE)
    r = jnp.pad(r, ((0, 0), (0, 0), (0, MLA_DQ - MLA_NOPE - MLA_ROPE)))
    return r.reshape(MLA_Q_RANK, MLA_HEADS * MLA_DQ).astype(BF16)


def _prep_router(w_grp, b_grp, w_exp, b_exp):
    D = w_grp.shape[0]
    pad = LANES - N_EXPERTS - N_GROUPS
    rwt = jnp.concatenate([w_exp, w_grp, jnp.zeros((D, pad), F32)], axis=1).astype(F32).T
    rwh = rwt.astype(BF16)
    rwl = (rwt - rwh.astype(F32)).astype(BF16)
    rb = jnp.concatenate([b_exp, b_grp, jnp.zeros((pad,), F32)]).astype(F32).reshape(LANES, 1)
    return rwh, rwl, rb


def kernel(x, c, positions, ada_w, ada_b, norm1_g, norm2_g, w_in, fox_fb, mla_q_norm_g, mla_wq_up, mla_kv_norm_g, mla_wkv_up, gate_w, gate_b, branch_w, out_w, router_grp_w, router_grp_b, router_exp_w, router_exp_b, exp_w1, exp_w3, exp_w2, final_g):
    B, S, D = x.shape
    L = ada_w.shape[0]
    T = B * S
    A = 2 * T
    tb = _pick(A, 256)
    n_blocks = A // tb + N_EXPERTS
    P = n_blocks * tb

    mod = _adaln(c, ada_w, ada_b)
    cos_t, sin_t = _rope_tables(positions)
    x2 = x.reshape(T, D)
    final_g2 = final_g.reshape(1, D)

    for l in range(L):
        sh1, sc1, gt1, sh2, sc2, gt2 = [mod[l, :, i * D:(i + 1) * D].reshape(B, 1, D) for i in range(6)]
        g1 = norm1_g[l].reshape(1, D)
        g2 = norm2_g[l].reshape(1, D)
        (rq, rk, rv, rg, fq, fk, fv, mq, mk, kpe, mv, ffp) = _proj(
            x2, S, g1, sh1, sc1, _prep_w_in(w_in[l]), cos_t, sin_t,
            mla_q_norm_g[l].reshape(1, -1), _prep_wq_up(mla_wq_up[l]),
            mla_kv_norm_g[l].reshape(1, -1), mla_wkv_up[l].astype(BF16))

        ya = _retention(rq, rk, rv, rg, B, S).reshape(T, -1)
        kb = _fox_gate(ffp, fox_fb[l], S)
        r3 = lambda a: a.reshape(B, S, -1)
        yb = _flash(r3(fq), r3(fk), r3(kb), r3(fv), B, S, FOX_HEADS, FOX_DH, True, 3, "flash_fox").reshape(T, -1)
        yc = _flash(r3(mq), r3(mk), r3(kpe), r3(mv), B, S, MLA_HEADS, MLA_DQ, False, 0, "flash_mla").reshape(T, -1)

        rwh, rwl, rb = _prep_router(router_grp_w[l], router_grp_b[l], router_exp_w[l], router_exp_b[l])
        x1, h2, mi, mf, cnt = _merge(
            x2, S, ya, yb, yc, g1, sh1, sc1, gt1, g2, sh2, sc2,
            gate_w[l].astype(BF16), gate_b[l].reshape(1, -1), branch_w[l].astype(BF16),
            out_w[l].astype(BF16), rwh, rwl, rb)

        counts = cnt[:, 0].astype(I32)
        pcounts = (counts + tb - 1) // tb * tb
        pends = jnp.cumsum(pcounts)
        pstarts = pends - pcounts
        sel = mi[0:2, :, None] == jnp.arange(N_EXPERTS, dtype=I32)
        dest = (jnp.sum(jnp.where(sel, pstarts, 0), axis=-1) + mi[2:4]).reshape(A)
        blk_pos = jnp.arange(n_blocks, dtype=I32) * tb
        blk_e = jnp.minimum(jnp.sum((pends[None, :] <= blk_pos[:, None]).astype(I32), axis=1), N_EXPERTS - 1)
        nvalid = (pends[-1:] // tb).astype(I32)
        fr = jnp.arange(tb, dtype=I32)[None, :]
        is_pad = (fr < (pcounts - counts)[:, None]).reshape(-1)
        pad_slot = ((pstarts + counts)[:, None] + fr).reshape(-1)
        tail_rank = jnp.cumsum(jnp.logical_not(is_pad).astype(I32)) - 1
        fill_idx = jnp.where(is_pad, pad_slot, pends[-1] + tail_rank)

        xs = _sc_dispatch(h2, dest, fill_idx, P)
        ys = _experts(xs, blk_e, nvalid, exp_w1[l].astype(BF16), exp_w3[l].astype(BF16),
                      exp_w2[l].astype(BF16), tb)
        g = _sc_gather(ys, dest)
        x2 = _combine(x1, S, mf[0:2].T, gt2, final_g2, g, final=(l == L - 1))

    return x2.reshape(B, S, D)
```

```python
import functools
import math

import jax
import jax.numpy as jnp
from jax import lax
from jax.experimental import pallas as pl
from jax.experimental.pallas import tpu as pltpu
from jax.experimental.pallas import tpu_sc as plsc

F32 = jnp.float32
BF16 = jnp.bfloat16
I32 = jnp.int32
U32 = jnp.uint32
HIGHEST = lax.Precision.HIGHEST

EPS = 1e-6
ROPE_THETA = 10000.0
RET_HEADS = 4
RET_DK = 64
RET_DV = 128
RET_CHUNK = 128
FOX_HEADS = 4
FOX_DH = 128
MLA_HEADS = 4
MLA_Q_RANK = 256
MLA_KV_RANK = 128
MLA_NOPE = 128
MLA_ROPE = 64
MLA_V = 128
MLA_DQ = 256
N_BRANCH = 3
BRANCH_W = 512
N_GROUPS = 4
EXP_PER_GROUP = 8
N_EXPERTS = N_GROUPS * EXP_PER_GROUP
D_EXPERT = 512

LANES = 128
V7X_VMEM_LIMIT = 56 * 1024 * 1024

C_RQ, C_RK, C_RV, C_RG = 0, 256, 512, 1024
C_FQ, C_FK, C_FV = 1536, 2048, 2560
C_MQ, C_MKV, C_TAIL = 3072, 3328, 3456
D_IN_PAD = 3584
FF_LANE = MLA_ROPE

NEG_BIG = -1e30
SC_GATHER_WINDOW = 128
MERGE_COL_CHUNK = 256
LOG2E = math.log2(math.e)


def _cparams(sem):
    return pltpu.CompilerParams(dimension_semantics=sem, vmem_limit_bytes=V7X_VMEM_LIMIT)


def _pick(n, pref):
    t = min(n, pref)
    assert n % t == 0, (n, t)
    return t


def _adaln_kernel(c_ref, w_ref, b_ref, o_ref):
    c = c_ref[...]
    ca = c * jax.nn.sigmoid(c)
    o_ref[...] = jnp.dot(ca, w_ref[...], preferred_element_type=F32, precision=HIGHEST) + b_ref[...]


def _adaln(c, ada_w, ada_b):
    L, D, N = ada_w.shape
    B = c.shape[0]
    tn = _pick(N, 1536)
    return pl.pallas_call(
        _adaln_kernel,
        grid=(L, N // tn),
        in_specs=[pl.BlockSpec((B, D), lambda l, j: (0, 0)),
                  pl.BlockSpec((None, D, tn), lambda l, j: (l, 0, j)),
                  pl.BlockSpec((None, 1, tn), lambda l, j: (l, 0, j))],
        out_specs=pl.BlockSpec((None, B, tn), lambda l, j: (l, 0, j)),
        out_shape=jax.ShapeDtypeStruct((L, B, N), F32),
        compiler_params=_cparams(("arbitrary", "arbitrary")),
        name="adaln",
    )(c, ada_w, ada_b.reshape(L, 1, N))


def _rope_table_kernel(pos_ref, inv_ref, sign_ref, cos_ref, sin_ref):
    ang = pos_ref[...].astype(F32) * inv_ref[...]
    cos_ref[...] = jnp.cos(ang)
    sin_ref[...] = jnp.sin(ang) * sign_ref[...]


def _rope_tables(positions):
    T = positions.size
    tm = _pick(T, 2048)
    half = MLA_ROPE // 2
    inv = ROPE_THETA ** (-jnp.arange(0, MLA_ROPE, 2, dtype=F32) / MLA_ROPE)
    inv_t = jnp.tile(inv, LANES // half).reshape(1, LANES)
    sign = jnp.where((jnp.arange(LANES) % MLA_ROPE) < half, -1.0, 1.0).astype(F32).reshape(1, LANES)
    return pl.pallas_call(
        _rope_table_kernel,
        grid=(T // tm,),
        in_specs=[pl.BlockSpec((tm, 1), lambda i: (i, 0)),
                  pl.BlockSpec((1, LANES), lambda i: (0, 0)),
                  pl.BlockSpec((1, LANES), lambda i: (0, 0))],
        out_specs=[pl.BlockSpec((tm, LANES), lambda i: (i, 0))] * 2,
        out_shape=[jax.ShapeDtypeStruct((T, LANES), F32)] * 2,
        compiler_params=_cparams(("arbitrary",)),
        name="rope_tables",
    )(positions.reshape(T, 1), inv_t, sign)


def _rope_slab(x, cos_t, sin_t, lane):
    nxt = pltpu.roll(x, LANES - 32, axis=1)
    prv = pltpu.roll(x, 32, axis=1)
    swapped = jnp.where((lane & 32) == 0, nxt, prv)
    return x * cos_t + swapped * sin_t


def _pack_bf16_pair(lo, hi):
    lo_bits = lax.shift_right_logical(lax.bitcast_convert_type(lo.astype(BF16).astype(F32), U32), jnp.uint32(16))
    hi_bits = lax.bitcast_convert_type(hi.astype(BF16).astype(F32), U32) & jnp.uint32(0xFFFF0000)
    return hi_bits | lo_bits


def _unpack_bf16_pair(w):
    lo = lax.bitcast_convert_type(lax.shift_left(w, jnp.uint32(16)), F32)
    hi = lax.bitcast_convert_type(w & jnp.uint32(0xFFFF0000), F32)
    return lo, hi


def _norm_mod(x, g, shift, scale):
    y = x * lax.rsqrt(jnp.mean(x * x, axis=-1, keepdims=True) + EPS)
    return (y * g) * (1.0 + scale) + shift


def _rms(x, g):
    return x * lax.rsqrt(jnp.mean(x * x, axis=-1, keepdims=True) + EPS) * g


def _proj_kernel(x_ref, g_ref, sh_ref, sc_ref, w_ref, cos_ref, sin_ref,
                 gq_ref, wq_ref, gkv_ref, wkv_ref,
                 rq_ref, rk_ref, rv_ref, rg_ref, fq_ref, fk_ref, fv_ref,
                 mq_ref, mk_ref, kpe_ref, mv_ref, ff_ref):
    h = _norm_mod(x_ref[...], g_ref[...], sh_ref[...], sc_ref[...]).astype(BF16)
    cos_t = cos_ref[...]
    sin_t = sin_ref[...]
    lane = lax.broadcasted_iota(I32, cos_t.shape, 1)

    def proj(c0, width):
        return jnp.dot(h, w_ref[:, c0:c0 + width], preferred_element_type=F32)

    rq = proj(C_RQ, 256)
    rk = proj(C_RK, 256)
    for s in range(2):
        sl = slice(s * LANES, (s + 1) * LANES)
        rq_ref[:, sl] = _rope_slab(rq[:, sl], cos_t, sin_t, lane).astype(BF16)
        rk_ref[:, sl] = (_rope_slab(rk[:, sl], cos_t, sin_t, lane) * (RET_DK ** -0.5)).astype(BF16)
    rv_ref[...] = proj(C_RV, 512).astype(BF16)
    rg_ref[...] = proj(C_RG, 512).astype(BF16)
    fq_ref[...] = (proj(C_FQ, 512) * (FOX_DH ** -0.5 * LOG2E)).astype(BF16)
    fk_ref[...] = proj(C_FK, 512).astype(BF16)
    fv_ref[...] = proj(C_FV, 512).astype(BF16)

    tail = proj(C_TAIL, LANES)
    ff_ref[...] = tail
    kpe_ref[...] = jnp.where(lane < MLA_ROPE, _rope_slab(tail, cos_t, sin_t, lane), 0.0).astype(BF16)

    qn = _rms(proj(C_MQ, MLA_Q_RANK), gq_ref[...]).astype(BF16)
    qh = jnp.dot(qn, wq_ref[...], preferred_element_type=F32)
    q_scale = (MLA_NOPE + MLA_ROPE) ** -0.5 * LOG2E
    for hd in range(MLA_HEADS):
        c0 = hd * MLA_DQ
        mq_ref[:, c0:c0 + LANES] = (qh[:, c0:c0 + LANES] * q_scale).astype(BF16)
        pe = _rope_slab(qh[:, c0 + LANES:c0 + 2 * LANES], cos_t, sin_t, lane)
        mq_ref[:, c0 + LANES:c0 + 2 * LANES] = jnp.where(lane < MLA_ROPE, pe * q_scale, 0.0).astype(BF16)

    kvn = _rms(proj(C_MKV, MLA_KV_RANK), gkv_ref[...]).astype(BF16)
    kvh = jnp.dot(kvn, wkv_ref[...], preferred_element_type=F32)
    for hd in range(MLA_HEADS):
        c0 = hd * (MLA_NOPE + MLA_V)
        mk_ref[:, hd * MLA_NOPE:(hd + 1) * MLA_NOPE] = kvh[:, c0:c0 + MLA_NOPE].astype(BF16)
        mv_ref[:, hd * MLA_V:(hd + 1) * MLA_V] = kvh[:, c0 + MLA_NOPE:c0 + MLA_NOPE + MLA_V].astype(BF16)


def _proj(x2, S, g, sh, sc, w_all, cos_t, sin_t, gq, wq, gkv, wkv):
    T, D = x2.shape
    tm = _pick(S, 512)
    per_b = S // tm
    row = lambda i: (i, 0)
    const = lambda i: (0, 0)
    batch = lambda i: (i // per_b, 0, 0)
    widths = [256, 256, 512, 512, 512, 512, 512, MLA_HEADS * MLA_DQ, MLA_HEADS * MLA_NOPE, LANES,
              MLA_HEADS * MLA_V]
    out_shape = [jax.ShapeDtypeStruct((T, w), BF16) for w in widths]
    out_shape.append(jax.ShapeDtypeStruct((T, LANES), F32))
    out_specs = [pl.BlockSpec((tm, w), row) for w in widths] + [pl.BlockSpec((tm, LANES), row)]
    return pl.pallas_call(
        _proj_kernel,
        grid=(T // tm,),
        in_specs=[pl.BlockSpec((tm, D), row),
                  pl.BlockSpec((1, D), const),
                  pl.BlockSpec((None, 1, D), batch),
                  pl.BlockSpec((None, 1, D), batch),
                  pl.BlockSpec((D, D_IN_PAD), const),
                  pl.BlockSpec((tm, LANES), row),
                  pl.BlockSpec((tm, LANES), row),
                  pl.BlockSpec((1, MLA_Q_RANK), const),
                  pl.BlockSpec((MLA_Q_RANK, MLA_HEADS * MLA_DQ), const),
                  pl.BlockSpec((1, MLA_KV_RANK), const),
                  pl.BlockSpec((MLA_KV_RANK, MLA_HEADS * (MLA_NOPE + MLA_V)), const)],
        out_specs=out_specs,
        out_shape=out_shape,
        compiler_params=_cparams(("arbitrary",)),
        name="proj",
    )(x2, g, sh, sc, w_all, cos_t, sin_t, gq, wq, gkv, wkv)


def _split3(x):
    a = x.astype(BF16)
    r = x - a.astype(F32)
    b = r.astype(BF16)
    c = (r - b.astype(F32)).astype(BF16)
    return a, b, c


def _fox_gate_kernel(ff_ref, fb_ref, tri_ref, kb_ref, carry_ref, *, tiles_per_seq):
    @pl.when(pl.program_id(0) % tiles_per_seq == 0)
    def _():
        carry_ref[...] = jnp.zeros_like(carry_ref)

    tm = ff_ref.shape[0]
    lane = lax.broadcasted_iota(I32, (tm, LANES), 1)
    z = ff_ref[...] + fb_ref[...]
    ls = -(jnp.maximum(-z, 0.0) + jnp.log1p(jnp.exp(-jnp.abs(z))))
    ls = jnp.where((lane >= FF_LANE) & (lane < FF_LANE + FOX_HEADS), ls, 0.0)
    tri = tri_ref[...]
    f = carry_ref[...]
    for part in _split3(ls):
        f = f + jnp.dot(tri, part, preferred_element_type=F32)
    carry_ref[...] = f[tm - 1:tm, :]
    f2 = f * LOG2E
    for hd in range(FOX_HEADS):
        src = FF_LANE + hd
        g = jnp.where(lane == 0, pltpu.roll(f2, (LANES - src) % LANES, axis=1),
            jnp.where(lane == 1, pltpu.roll(f2, (LANES + 1 - src) % LANES, axis=1),
            jnp.where(lane == 2, pltpu.roll(f2, (LANES + 2 - src) % LANES, axis=1), 0.0)))
        hi, mid, lo = _split3(g)
        kb_ref[:, hd * LANES:(hd + 1) * LANES] = jnp.where(lane == 0, hi, jnp.where(lane == 1, mid, lo))


def _fox_gate(ffp, fb, S):
    T = ffp.shape[0]
    tm = _pick(S, 512)
    idx = jnp.arange(tm)
    tri = (idx[None, :] <= idx[:, None]).astype(BF16)
    fbv = jnp.zeros((1, LANES), F32).at[0, FF_LANE:FF_LANE + FOX_HEADS].set(fb)
    return pl.pallas_call(
        functools.partial(_fox_gate_kernel, tiles_per_seq=S // tm),
        grid=(T // tm,),
        in_specs=[pl.BlockSpec((tm, LANES), lambda i: (i, 0)),
                  pl.BlockSpec((1, LANES), lambda i: (0, 0)),
                  pl.BlockSpec((tm, tm), lambda i: (0, 0))],
        out_specs=pl.BlockSpec((tm, FOX_HEADS * LANES), lambda i: (i, 0)),
        out_shape=jax.ShapeDtypeStruct((T, FOX_HEADS * LANES), BF16),
        scratch_shapes=[pltpu.VMEM((1, LANES), F32)],
        compiler_params=_cparams(("arbitrary",)),
        name="fox_gate",
    )(ffp, fbv, tri)


def _retention_kernel(dchunk_ref, q_ref, k_ref, v_ref, g_ref, dmask_ref, din_ref, dout_ref,
                      o_ref, state_ref, *, n_chunks):
    @pl.when(pl.program_id(1) == 0)
    def _():
        state_ref[...] = jnp.zeros_like(state_ref)

    C = RET_CHUNK
    lane = lax.broadcasted_iota(I32, (C, LANES), 1)
    for ci in range(n_chunks):
        rows = slice(ci * C, (ci + 1) * C)
        for hd in range(RET_HEADS):
            slab = slice((hd // 2) * LANES, (hd // 2 + 1) * LANES)
            mine = (lane < RET_DK) if hd % 2 == 0 else (lane >= RET_DK)
            q = jnp.where(mine, q_ref[rows, slab], 0)
            k = jnp.where(mine, k_ref[rows, slab], 0)
            vcols = slice(hd * RET_DV, (hd + 1) * RET_DV)
            v = v_ref[rows, vcols]
            state = state_ref[hd]
            scores = lax.dot_general(q, k, (((1,), (1,)), ((), ())),
                                     preferred_element_type=F32) * dmask_ref[hd]
            inner = jnp.dot(scores.astype(BF16), v, preferred_element_type=F32)
            cross = jnp.dot(q, state.astype(BF16), preferred_element_type=F32) * din_ref[hd]
            o = inner + cross
            vd = (v.astype(F32) * dout_ref[hd]).astype(BF16)
            kv = lax.dot_general(k, vd, (((0,), (0,)), ((), ())), preferred_element_type=F32)
            state_ref[hd] = state * dchunk_ref[hd] + kv
            mu = jnp.mean(o, axis=-1, keepdims=True)
            d = o - mu
            var = jnp.mean(d * d, axis=-1, keepdims=True)
            on = d * lax.rsqrt(var + EPS)
            g = g_ref[rows, vcols].astype(F32)
            o_ref[rows, vcols] = (g * jax.nn.sigmoid(g) * on).astype(BF16)


def _retention(rq, rk, rv, rg, B, S):
    H, C = RET_HEADS, RET_CHUNK
    tr = _pick(S, 4 * C)
    n_chunks = tr // C
    log_gamma = jnp.log1p(-jnp.exp2(-5.0 - jnp.arange(H, dtype=F32)))
    idx = jnp.arange(C, dtype=F32)
    rel = idx[:, None] - idx[None, :]
    dmask = jnp.where(rel >= 0, jnp.exp(log_gamma[:, None, None] * jnp.maximum(rel, 0.0)), 0.0)
    decay_in = jnp.exp(log_gamma[:, None] * (idx + 1.0))
    decay_out = jnp.exp(log_gamma[:, None] * (C - 1.0 - idx))
    decay_chunk = jnp.exp(log_gamma * C)
    din = jnp.broadcast_to(decay_in[:, :, None], (H, C, RET_DV))
    dout = jnp.broadcast_to(decay_out[:, :, None], (H, C, RET_DV))
    tok = lambda b, i: (b, i, 0)
    const3 = lambda b, i: (0, 0, 0)
    return pl.pallas_call(
        functools.partial(_retention_kernel, n_chunks=n_chunks),
        grid=(B, S // tr),
        in_specs=[pl.BlockSpec(memory_space=pltpu.SMEM),
                  pl.BlockSpec((None, tr, H * RET_DK), tok),
                  pl.BlockSpec((None, tr, H * RET_DK), tok),
                  pl.BlockSpec((None, tr, H * RET_DV), tok),
                  pl.BlockSpec((None, tr, H * RET_DV), tok),
                  pl.BlockSpec((H, C, C), const3),
                  pl.BlockSpec((H, C, RET_DV), const3),
                  pl.BlockSpec((H, C, RET_DV), const3)],
        out_specs=pl.BlockSpec((None, tr, H * RET_DV), tok),
        out_shape=jax.ShapeDtypeStruct((B, S, H * RET_DV), BF16),
        scratch_shapes=[pltpu.VMEM((H, LANES, RET_DV), F32)],
        compiler_params=_cparams(("arbitrary", "arbitrary")),
        name="retention",
    )(decay_chunk, rq.reshape(B, S, -1), rk.reshape(B, S, -1), rv.reshape(B, S, -1),
      rg.reshape(B, S, -1), dmask, din, dout)


def _flash_kernel(q_ref, k_ref, ke_ref, vt_ref, o_ref, m_ref, l_ref, acc_ref, s_ref, *, q_bias_cols):
    qi = pl.program_id(2)
    tq = q_ref.shape[0]
    tk = tq
    m_ref[...] = jnp.full(m_ref.shape, NEG_BIG, F32)
    l_ref[...] = jnp.zeros_like(l_ref)
    acc_ref[...] = jnp.zeros_like(acc_ref)
    q = q_ref[...]
    if q_bias_cols:
        lane = lax.broadcasted_iota(I32, (tq, LANES), 1)
        q = jnp.concatenate([q, jnp.where(lane < q_bias_cols, -1.0, 0.0).astype(BF16)], axis=1)

    def scores(j):
        start = pl.multiple_of(j * tk, tk)
        kj = jnp.concatenate([k_ref[pl.ds(start, tk), :], ke_ref[pl.ds(start, tk), :]], axis=1)
        return lax.dot_general(kj, q, (((1,), (1,)), ((), ())), preferred_element_type=F32)

    def update(s, j):
        m_prev = m_ref[...]
        m_new = jnp.maximum(m_prev, jnp.max(s, axis=0, keepdims=True))
        alpha = jnp.exp2(m_prev - m_new)
        p = jnp.exp2(s - m_new)
        l_ref[...] = alpha * l_ref[...] + jnp.sum(p, axis=0, keepdims=True)
        acc_ref[...] = alpha * acc_ref[...] + jnp.dot(vt_ref[j], p.astype(BF16),
                                                      preferred_element_type=F32)
        m_ref[...] = m_new

    def causal(s):
        key = lax.broadcasted_iota(I32, s.shape, 0)
        qry = lax.broadcasted_iota(I32, s.shape, 1)
        return jnp.where(key <= qry, s, NEG_BIG)

    s_ref[0] = scores(0)

    def pair(p, carry):
        j = 2 * p
        s = s_ref[0]
        s_ref[1] = scores(j + 1)
        update(s, j)
        s = s_ref[1]
        s_ref[0] = scores(j + 2)
        update(s, j + 1)
        return carry

    lax.fori_loop(0, qi // 2, pair, 0)

    @pl.when(qi % 2 == 1)
    def _():
        s = s_ref[0]
        s_ref[1] = scores(qi)
        update(s, qi - 1)
        update(causal(s_ref[1]), qi)

    @pl.when(qi % 2 == 0)
    def _():
        update(causal(s_ref[0]), qi)

    o_ref[...] = (acc_ref[...] / l_ref[...]).T.astype(o_ref.dtype)


def _flash(q, k, ke, v, B, S, H, dq, ke_per_head, q_bias_cols, name):
    tq = _pick(S, 512)
    nq = S // tq
    ke_map = (lambda b, h, i: (b, 0, h)) if ke_per_head else (lambda b, h, i: (b, 0, 0))
    vt = v.reshape(B, nq, tq, H, LANES).transpose(0, 3, 1, 4, 2)
    return pl.pallas_call(
        functools.partial(_flash_kernel, q_bias_cols=q_bias_cols),
        grid=(B, H, nq),
        in_specs=[pl.BlockSpec((None, tq, dq), lambda b, h, i: (b, i, h)),
                  pl.BlockSpec((None, S, LANES), lambda b, h, i: (b, 0, h)),
                  pl.BlockSpec((None, S, LANES), ke_map),
                  pl.BlockSpec((None, None, nq, LANES, tq), lambda b, h, i: (b, h, 0, 0, 0))],
        out_specs=pl.BlockSpec((None, tq, LANES), lambda b, h, i: (b, i, h)),
        out_shape=jax.ShapeDtypeStruct((B, S, H * LANES), BF16),
        scratch_shapes=[pltpu.VMEM((1, tq), F32), pltpu.VMEM((1, tq), F32),
                        pltpu.VMEM((LANES, tq), F32), pltpu.VMEM((2, tq, tq), F32)],
        compiler_params=_cparams(("arbitrary", "arbitrary", "arbitrary")),
        name=name,
    )(q, k, ke, vt)


def _merge_kernel(x_ref, ya_ref, yb_ref, yc_ref, g1_ref, sh1_ref, sc1_ref, gt1_ref,
                  g2_ref, sh2_ref, sc2_ref, gw_ref, gb_ref, bw_ref, ow_ref,
                  rwh_ref, rwl_ref, rb_ref, tri_ref,
                  x1_ref, h2_ref, mi_ref, mf_ref, cnt_ref, h_ref, mg_ref, carry_ref):
    tm, D = x_ref.shape
    cn = MERGE_COL_CHUNK

    @pl.when(pl.program_id(0) == 0)
    def _():
        carry_ref[...] = jnp.zeros_like(carry_ref)

    h_ref[...] = _norm_mod(x_ref[...], g1_ref[...], sh1_ref[...], sc1_ref[...]).astype(BF16)
    for n in range(D // cn):
        cols = slice(n * cn, (n + 1) * cn)
        merged = None
        for i, y_ref in enumerate((ya_ref, yb_ref, yc_ref)):
            gcols = slice(i * D + n * cn, i * D + (n + 1) * cn)
            gate = jax.nn.sigmoid(jnp.dot(h_ref[...], gw_ref[:, gcols], preferred_element_type=F32)
                                  + gb_ref[:, gcols])
            br = jnp.dot(y_ref[...], bw_ref[i, :, cols], preferred_element_type=F32)
            merged = gate * br if merged is None else merged + gate * br
        mg_ref[:, cols] = merged.astype(BF16)
    for n in range(D // cn):
        cols = slice(n * cn, (n + 1) * cn)
        mix = jnp.dot(mg_ref[...], ow_ref[:, cols], preferred_element_type=F32)
        x1_ref[:, cols] = x_ref[:, cols] + gt1_ref[:, cols] * mix
    h2 = _norm_mod(x1_ref[...], g2_ref[...], sh2_ref[...], sc2_ref[...])
    h2_ref[...] = _pack_bf16_pair(h2[:, :D // 2], h2[:, D // 2:])

    hh = h2.astype(BF16)
    hl = (h2 - hh.astype(F32)).astype(BF16)
    nt = lambda a, b: lax.dot_general(a, b, (((1,), (1,)), ((), ())), preferred_element_type=F32)
    lt = nt(rwh_ref[...], hh) + nt(rwh_ref[...], hl) + nt(rwl_ref[...], hh) + rb_ref[...]
    row8 = lax.broadcasted_iota(I32, (EXP_PER_GROUP, tm), 0)
    gl = jnp.where(row8 < N_GROUPS, lt[N_EXPERTS:N_EXPERTS + EXP_PER_GROUP, :], -jnp.inf)
    gmax = jnp.max(gl, axis=0, keepdims=True)
    g_idx = jnp.min(jnp.where(gl == gmax, row8, EXP_PER_GROUP), axis=0, keepdims=True)
    g_w = 1.0 / jnp.sum(jnp.exp(gl - gmax), axis=0, keepdims=True)
    el = lt[(N_GROUPS - 1) * EXP_PER_GROUP:N_EXPERTS, :]
    for g in range(N_GROUPS - 2, -1, -1):
        el = jnp.where(g_idx == g, lt[g * EXP_PER_GROUP:(g + 1) * EXP_PER_GROUP, :], el)
    e1 = jnp.max(el, axis=0, keepdims=True)
    i1 = jnp.min(jnp.where(el == e1, row8, EXP_PER_GROUP), axis=0, keepdims=True)
    el2 = jnp.where(row8 == i1, -jnp.inf, el)
    e2 = jnp.max(el2, axis=0, keepdims=True)
    i2 = jnp.min(jnp.where(el2 == e2, row8, EXP_PER_GROUP), axis=0, keepdims=True)
    r = jnp.exp(e2 - e1)
    w1 = g_w / (1.0 + r)
    w2 = g_w * r / (1.0 + r)
    eid1 = g_idx * EXP_PER_GROUP + i1
    eid2 = g_idx * EXP_PER_GROUP + i2

    rowe = lax.broadcasted_iota(I32, (N_EXPERTS, tm), 0)
    hit1 = rowe == eid1
    hit2 = rowe == eid2
    onehot = jnp.where(hit1 | hit2, 1.0, 0.0)
    before = (jnp.dot(onehot.astype(BF16), tri_ref[...], preferred_element_type=F32)
              + jnp.concatenate([carry_ref[...]] * (tm // LANES), axis=1))
    rank1 = jnp.sum(jnp.where(hit1, before, 0.0), axis=0, keepdims=True)
    rank2 = jnp.sum(jnp.where(hit2, before, 0.0), axis=0, keepdims=True)
    carry_ref[...] = carry_ref[...] + jnp.sum(onehot, axis=1, keepdims=True)
    cnt_ref[...] = carry_ref[...]

    mi_ref[...] = jnp.where(row8 == 0, eid1,
                  jnp.where(row8 == 1, eid2,
                  jnp.where(row8 == 2, rank1.astype(I32),
                  jnp.where(row8 == 3, rank2.astype(I32), 0))))
    mf_ref[...] = jnp.where(row8 == 0, w1, jnp.where(row8 == 1, w2, 0.0))


def _merge(x2, S, ya, yb, yc, g1, sh1, sc1, gt1, g2, sh2, sc2, gw, gb, bw, ow, rwh, rwl, rb):
    T, D = x2.shape
    tm = _pick(S, 512)
    per_b = S // tm
    row = lambda i: (i, 0)
    col = lambda i: (0, i)
    const = lambda i: (0, 0)
    batch = lambda i: (i // per_b, 0, 0)
    idx = jnp.arange(tm)
    tri = (idx[:, None] < idx[None, :]).astype(BF16)
    vecb = pl.BlockSpec((None, 1, D), batch)
    return pl.pallas_call(
        _merge_kernel,
        grid=(T // tm,),
        in_specs=[pl.BlockSpec((tm, D), row),
                  pl.BlockSpec((tm, BRANCH_W), row),
                  pl.BlockSpec((tm, BRANCH_W), row),
                  pl.BlockSpec((tm, BRANCH_W), row),
                  pl.BlockSpec((1, D), const), vecb, vecb, vecb,
                  pl.BlockSpec((1, D), const), vecb, vecb,
                  pl.BlockSpec((D, N_BRANCH * D), const),
                  pl.BlockSpec((1, N_BRANCH * D), const),
                  pl.BlockSpec((N_BRANCH, BRANCH_W, D), lambda i: (0, 0, 0)),
                  pl.BlockSpec((D, D), const),
                  pl.BlockSpec((LANES, D), const),
                  pl.BlockSpec((LANES, D), const),
                  pl.BlockSpec((LANES, 1), const),
                  pl.BlockSpec((tm, tm), const)],
        out_specs=[pl.BlockSpec((tm, D), row), pl.BlockSpec((tm, D // 2), row),
                   pl.BlockSpec((8, tm), col), pl.BlockSpec((8, tm), col),
                   pl.BlockSpec((N_EXPERTS, LANES), const)],
        out_shape=[jax.ShapeDtypeStruct((T, D), F32), jax.ShapeDtypeStruct((T, D // 2), U32),
                   jax.ShapeDtypeStruct((8, T), I32), jax.ShapeDtypeStruct((8, T), F32),
                   jax.ShapeDtypeStruct((N_EXPERTS, LANES), F32)],
        scratch_shapes=[pltpu.VMEM((tm, D), BF16), pltpu.VMEM((tm, D), BF16),
                        pltpu.VMEM((N_EXPERTS, LANES), F32)],
        compiler_params=_cparams(("arbitrary",)),
        name="merge",
    )(x2, ya, yb, yc, g1, sh1, sc1, gt1, g2, sh2, sc2, gw, gb, bw, ow, rwh, rwl, rb, tri)


def _expert_kernel(blk_e_ref, nvalid_ref, xs_ref, w1_ref, w3_ref, w2_ref, ys_ref):
    del blk_e_ref
    valid = pl.program_id(0) < nvalid_ref[0]

    @pl.when(valid)
    def _():
        half = xs_ref.shape[1]
        x_lo, x_hi = _unpack_bf16_pair(xs_ref[...])
        x_lo = x_lo.astype(BF16)
        x_hi = x_hi.astype(BF16)

        def up(w_ref):
            return (jnp.dot(x_lo, w_ref[:half, :], preferred_element_type=F32)
                    + jnp.dot(x_hi, w_ref[half:, :], preferred_element_type=F32))

        a = up(w1_ref)
        b = up(w3_ref)
        hid = (a * jax.nn.sigmoid(a) * b).astype(BF16)
        y = jnp.dot(hid, w2_ref[...], preferred_element_type=F32)
        ys_ref[...] = _pack_bf16_pair(y[:, :half], y[:, half:])

    @pl.when(jnp.logical_not(valid))
    def _():
        ys_ref[...] = jnp.zeros_like(ys_ref)


def _experts(xs, blk_e, nvalid, w1, w3, w2, tb):
    P, half = xs.shape
    D = 2 * half
    n_blocks = P // tb
    rows = lambda i, be, nv: (jnp.minimum(i, nv[0] - 1), 0)
    grid_spec = pltpu.PrefetchScalarGridSpec(
        num_scalar_prefetch=2,
        grid=(n_blocks,),
        in_specs=[pl.BlockSpec((tb, half), rows),
                  pl.BlockSpec((None, D, D_EXPERT), lambda i, be, nv: (be[i], 0, 0)),
                  pl.BlockSpec((None, D, D_EXPERT), lambda i, be, nv: (be[i], 0, 0)),
                  pl.BlockSpec((None, D_EXPERT, D), lambda i, be, nv: (be[i], 0, 0))],
        out_specs=pl.BlockSpec((tb, half), lambda i, be, nv: (i, 0)),
    )
    return pl.pallas_call(
        _expert_kernel,
        grid_spec=grid_spec,
        out_shape=jax.ShapeDtypeStruct((P, half), U32),
        compiler_params=_cparams(("arbitrary",)),
        name="moe_experts",
    )(blk_e, nvalid, xs, w1, w3, w2)


def _sc_gather(data, idx):
    M = idx.shape[0]
    D = data.shape[1]
    W = SC_GATHER_WINDOW
    assert M % W == 0, (M, W)
    mesh = plsc.VectorSubcoreMesh(core_axis_name="core", subcore_axis_name="subcore")
    n_workers = mesh.num_cores * mesh.num_subcores
    assert M % (W * n_workers) == 0, (M, W, n_workers)

    @functools.partial(pl.kernel, out_type=jax.ShapeDtypeStruct((M, D), data.dtype), mesh=mesh,
                       scratch_types=[pltpu.VMEM((W,), I32), pltpu.VMEM((W, D), data.dtype)])
    def gather_kernel(x_hbm, i_hbm, o_hbm, i_vmem, buf):
        worker = lax.axis_index("core") * mesh.num_subcores + lax.axis_index("subcore")

        @pl.loop(0, M // (W * n_workers))
        def _(t):
            start = (t * n_workers + worker) * W
            pltpu.sync_copy(i_hbm.at[pl.ds(start, W)], i_vmem)
            pltpu.sync_copy(x_hbm.at[i_vmem], buf)
            pltpu.sync_copy(buf, o_hbm.at[pl.ds(start, W)])

    return gather_kernel(data, idx)


def _sc_dispatch(h2, dest, fill_idx, P):
    T, D = h2.shape
    W = SC_GATHER_WINDOW
    n_fill = fill_idx.shape[0]
    mesh = plsc.VectorSubcoreMesh(core_axis_name="core", subcore_axis_name="subcore")
    n_workers = mesh.num_cores * mesh.num_subcores
    assert T % (W * n_workers) == 0 and n_fill % (W * n_workers) == 0, (T, n_fill, W, n_workers)
    zeros = jnp.zeros((W, D), h2.dtype)

    @functools.partial(pl.kernel, out_type=jax.ShapeDtypeStruct((P, D), h2.dtype), mesh=mesh,
                       scratch_types=[pltpu.VMEM((W,), I32), pltpu.VMEM((W, D), h2.dtype)])
    def dispatch_kernel(h_hbm, d_hbm, f_hbm, z_hbm, o_hbm, i_vmem, buf):
        worker = lax.axis_index("core") * mesh.num_subcores + lax.axis_index("subcore")

        @pl.loop(0, T // (W * n_workers))
        def _(t):
            start = (t * n_workers + worker) * W
            pltpu.sync_copy(h_hbm.at[pl.ds(start, W)], buf)
            for c in range(2):
                pltpu.sync_copy(d_hbm.at[pl.ds(c * T + start, W)], i_vmem)
                pltpu.sync_copy(buf, o_hbm.at[i_vmem])

        pltpu.sync_copy(z_hbm, buf)

        @pl.loop(0, n_fill // (W * n_workers))
        def _(t):
            start = (t * n_workers + worker) * W
            pltpu.sync_copy(f_hbm.at[pl.ds(start, W)], i_vmem)
            pltpu.sync_copy(buf, o_hbm.at[i_vmem])

    return dispatch_kernel(h2, dest, fill_idx, zeros)


def _combine_kernel(x_ref, g0_ref, g1_ref, mf_ref, gt_ref, fg_ref, o_ref, *, final):
    mf = mf_ref[...]
    lo0, hi0 = _unpack_bf16_pair(g0_ref[...])
    lo1, hi1 = _unpack_bf16_pair(g1_ref[...])
    w0 = mf[:, 0:1]
    w1 = mf[:, 1:2]
    ffn = jnp.concatenate([lo0 * w0 + lo1 * w1, hi0 * w0 + hi1 * w1], axis=1)
    out = x_ref[...] + gt_ref[...] * ffn
    if final:
        out = _rms(out, fg_ref[...])
    o_ref[...] = out


def _combine(x1, S, mf, gt2, final_g, g, final):
    T, D = x1.shape
    tm = _pick(S, 512)
    per_b = S // tm
    nt = T // tm
    return pl.pallas_call(
        functools.partial(_combine_kernel, final=final),
        grid=(nt,),
        in_specs=[pl.BlockSpec((tm, D), lambda i: (i, 0)),
                  pl.BlockSpec((tm, D // 2), lambda i: (i, 0)),
                  pl.BlockSpec((tm, D // 2), lambda i: (i + nt, 0)),
                  pl.BlockSpec((tm, 2), lambda i: (i, 0)),
                  pl.BlockSpec((None, 1, D), lambda i: (i // per_b, 0, 0)),
                  pl.BlockSpec((1, D), lambda i: (0, 0))],
        out_specs=pl.BlockSpec((tm, D), lambda i: (i, 0)),
        out_shape=jax.ShapeDtypeStruct((T, D), F32),
        compiler_params=_cparams(("arbitrary",)),
        name="moe_combine",
    )(x1, g, g, mf, gt2, final_g)


def _prep_w_in(w):
    offs = [0]
    for s in (256, 256, 512, 512, 512, 512, 512, FOX_HEADS, MLA_Q_RANK, MLA_KV_RANK, MLA_ROPE):
        offs.append(offs[-1] + s)
    rq, rk, rv, rg, fq, fk, fv, ff, mq, mkv, mkr = [w[:, offs[i]:offs[i + 1]] for i in range(11)]
    pad = jnp.zeros((w.shape[0], LANES - MLA_ROPE - FOX_HEADS), w.dtype)
    return jnp.concatenate([rq, rk, rv, rg, fq, fk, fv, mq, mkv, mkr, ff, pad], axis=1).astype(BF16)


def _prep_wq_up(w):
    r = w.reshape(MLA_Q_RANK, MLA_HEADS, MLA_NOPE + MLA_ROPE)
    r = jnp.pad(r, ((0, 0), (0, 0), (0, MLA_DQ - MLA_NOPE - MLA_ROPE)))
    return r.reshape(MLA_Q_RANK, MLA_HEADS * MLA_DQ).astype(BF16)


def _prep_router(w_grp, b_grp, w_exp, b_exp):
    D = w_grp.shape[0]
    pad = LANES - N_EXPERTS - N_GROUPS
    rwt = jnp.concatenate([w_exp, w_grp, jnp.zeros((D, pad), F32)], axis=1).astype(F32).T
    rwh = rwt.astype(BF16)
    rwl = (rwt - rwh.astype(F32)).astype(BF16)
    rb = jnp.concatenate([b_exp, b_grp, jnp.zeros((pad,), F32)]).astype(F32).reshape(LANES, 1)
    return rwh, rwl, rb


def kernel(x, c, positions, ada_w, ada_b, norm1_g, norm2_g, w_in, fox_fb, mla_q_norm_g, mla_wq_up, mla_kv_norm_g, mla_wkv_up, gate_w, gate_b, branch_w, out_w, router_grp_w, router_grp_b, router_exp_w, router_exp_b, exp_w1, exp_w3, exp_w2, final_g):
    B, S, D = x.shape
    L = ada_w.shape[0]
    T = B * S
    A = 2 * T
    tb = _pick(A, 256)
    n_blocks = A // tb + N_EXPERTS
    P = n_blocks * tb

    mod = _adaln(c, ada_w, ada_b)
    cos_t, sin_t = _rope_tables(positions)
    x2 = x.reshape(T, D)
    final_g2 = final_g.reshape(1, D)

    for l in range(L):
        sh1, sc1, gt1, sh2, sc2, gt2 = [mod[l, :, i * D:(i + 1) * D].reshape(B, 1, D) for i in range(6)]
        g1 = norm1_g[l].reshape(1, D)
        g2 = norm2_g[l].reshape(1, D)
        (rq, rk, rv, rg, fq, fk, fv, mq, mk, kpe, mv, ffp) = _proj(
            x2, S, g1, sh1, sc1, _prep_w_in(w_in[l]), cos_t, sin_t,
            mla_q_norm_g[l].reshape(1, -1), _prep_wq_up(mla_wq_up[l]),
            mla_kv_norm_g[l].reshape(1, -1), mla_wkv_up[l].astype(BF16))

        ya = _retention(rq, rk, rv, rg, B, S).reshape(T, -1)
        kb = _fox_gate(ffp, fox_fb[l], S)
        r3 = lambda a: a.reshape(B, S, -1)
        yb = _flash(r3(fq), r3(fk), r3(kb), r3(fv), B, S, FOX_HEADS, FOX_DH, True, 3, "flash_fox").reshape(T, -1)
        yc = _flash(r3(mq), r3(mk), r3(kpe), r3(mv), B, S, MLA_HEADS, MLA_DQ, False, 0, "flash_mla").reshape(T, -1)

        rwh, rwl, rb = _prep_router(router_grp_w[l], router_grp_b[l], router_exp_w[l], router_exp_b[l])
        x1, h2, mi, mf, cnt = _merge(
            x2, S, ya, yb, yc, g1, sh1, sc1, gt1, g2, sh2, sc2,
            gate_w[l].astype(BF16), gate_b[l].reshape(1, -1), branch_w[l].astype(BF16),
            out_w[l].astype(BF16), rwh, rwl, rb)

        counts = cnt[:, 0].astype(I32)
        pcounts = (counts + tb - 1) // tb * tb
        pends = jnp.cumsum(pcounts)
        pstarts = pends - pcounts
        sel = mi[0:2, :, None] == jnp.arange(N_EXPERTS, dtype=I32)
        dest = (jnp.sum(jnp.where(sel, pstarts, 0), axis=-1) + mi[2:4]).reshape(A)
        blk_pos = jnp.arange(n_blocks, dtype=I32) * tb
        blk_e = jnp.minimum(jnp.sum((pends[None, :] <= blk_pos[:, None]).astype(I32), axis=1), N_EXPERTS - 1)
        nvalid = (pends[-1:] // tb).astype(I32)
        fr = jnp.arange(tb, dtype=I32)[None, :]
        is_pad = (fr < (pcounts - counts)[:, None]).reshape(-1)
        pad_slot = ((pstarts + counts)[:, None] + fr).reshape(-1)
        tail_rank = jnp.cumsum(jnp.logical_not(is_pad).astype(I32)) - 1
        fill_idx = jnp.where(is_pad, pad_slot, pends[-1] + tail_rank)

        xs = _sc_dispatch(h2, dest, fill_idx, P)
        ys = _experts(xs, blk_e, nvalid, exp_w1[l].astype(BF16), exp_w3[l].astype(BF16),
                      exp_w2[l].astype(BF16), tb)
        g = _sc_gather(ys, dest)
        x2 = _combine(x1, S, mf[0:2].T, gt2, final_g2, g, final=(l == L - 1))

    return x2.reshape(B, S, D)
```

```python
import functools
import math

import jax
import jax.numpy as jnp
from jax import lax
from jax.experimental import pallas as pl
from jax.experimental.pallas import tpu as pltpu
from jax.experimental.pallas import tpu_sc as plsc

F32 = jnp.float32
BF16 = jnp.bfloat16
I32 = jnp.int32
U32 = jnp.uint32
HIGHEST = lax.Precision.HIGHEST

EPS = 1e-6
ROPE_THETA = 10000.0
RET_HEADS = 4
RET_DK = 64
RET_DV = 128
RET_CHUNK = 128
FOX_HEADS = 4
FOX_DH = 128
MLA_HEADS = 4
MLA_Q_RANK = 256
MLA_KV_RANK = 128
MLA_NOPE = 128
MLA_ROPE = 64
MLA_V = 128
MLA_DQ = 256
N_BRANCH = 3
BRANCH_W = 512
N_GROUPS = 4
EXP_PER_GROUP = 8
N_EXPERTS = N_GROUPS * EXP_PER_GROUP
D_EXPERT = 512

LANES = 128
V7X_VMEM_LIMIT = 56 * 1024 * 1024

C_RQ, C_RK, C_RV, C_RG = 0, 256, 512, 1024
C_FQ, C_FK, C_FV = 1536, 2048, 2560
C_MQ, C_MKV, C_TAIL = 3072, 3328, 3456
D_IN_PAD = 3584
FF_LANE = MLA_ROPE

NEG_BIG = -1e30
SC_GATHER_WINDOW = 128
MERGE_COL_CHUNK = 256
LOG2E = math.log2(math.e)


def _cparams(sem):
    return pltpu.CompilerParams(dimension_semantics=sem, vmem_limit_bytes=V7X_VMEM_LIMIT)


def _pick(n, pref):
    t = min(n, pref)
    assert n % t == 0, (n, t)
    return t


def _adaln_kernel(c_ref, w_ref, b_ref, o_ref):
    c = c_ref[...]
    ca = c * jax.nn.sigmoid(c)
    o_ref[...] = jnp.dot(ca, w_ref[...], preferred_element_type=F32, precision=HIGHEST) + b_ref[...]


def _adaln(c, ada_w, ada_b):
    L, D, N = ada_w.shape
    B = c.shape[0]
    tn = _pick(N, 1536)
    return pl.pallas_call(
        _adaln_kernel,
        grid=(L, N // tn),
        in_specs=[pl.BlockSpec((B, D), lambda l, j: (0, 0)),
                  pl.BlockSpec((None, D, tn), lambda l, j: (l, 0, j)),
                  pl.BlockSpec((None, 1, tn), lambda l, j: (l, 0, j))],
        out_specs=pl.BlockSpec((None, B, tn), lambda l, j: (l, 0, j)),
        out_shape=jax.ShapeDtypeStruct((L, B, N), F32),
        compiler_params=_cparams(("arbitrary", "arbitrary")),
        name="adaln",
    )(c, ada_w, ada_b.reshape(L, 1, N))


def _rope_table_kernel(pos_ref, inv_ref, sign_ref, cos_ref, sin_ref):
    ang = pos_ref[...].astype(F32) * inv_ref[...]
    cos_ref[...] = jnp.cos(ang)
    sin_ref[...] = jnp.sin(ang) * sign_ref[...]


def _rope_tables(positions):
    T = positions.size
    tm = _pick(T, 2048)
    half = MLA_ROPE // 2
    inv = ROPE_THETA ** (-jnp.arange(0, MLA_ROPE, 2, dtype=F32) / MLA_ROPE)
    inv_t = jnp.tile(inv, LANES // half).reshape(1, LANES)
    sign = jnp.where((jnp.arange(LANES) % MLA_ROPE) < half, -1.0, 1.0).astype(F32).reshape(1, LANES)
    return pl.pallas_call(
        _rope_table_kernel,
        grid=(T // tm,),
        in_specs=[pl.BlockSpec((tm, 1), lambda i: (i, 0)),
                  pl.BlockSpec((1, LANES), lambda i: (0, 0)),
                  pl.BlockSpec((1, LANES), lambda i: (0, 0))],
        out_specs=[pl.BlockSpec((tm, LANES), lambda i: (i, 0))] * 2,
        out_shape=[jax.ShapeDtypeStruct((T, LANES), F32)] * 2,
        compiler_params=_cparams(("arbitrary",)),
        name="rope_tables",
    )(positions.reshape(T, 1), inv_t, sign)


def _rope_slab(x, cos_t, sin_t, lane):
    nxt = pltpu.roll(x, LANES - 32, axis=1)
    prv = pltpu.roll(x, 32, axis=1)
    swapped = jnp.where((lane & 32) == 0, nxt, prv)
    return x * cos_t + swapped * sin_t


def _pack_bf16_pair(lo, hi):
    lo_bits = lax.shift_right_logical(lax.bitcast_convert_type(lo.astype(BF16).astype(F32), U32), jnp.uint32(16))
    hi_bits = lax.bitcast_convert_type(hi.astype(BF16).astype(F32), U32) & jnp.uint32(0xFFFF0000)
    return hi_bits | lo_bits


def _unpack_bf16_pair(w):
    lo = lax.bitcast_convert_type(lax.shift_left(w, jnp.uint32(16)), F32)
    hi = lax.bitcast_convert_type(w & jnp.uint32(0xFFFF0000), F32)
    return lo, hi


def _norm_mod(x, g, shift, scale):
    y = x * lax.rsqrt(jnp.mean(x * x, axis=-1, keepdims=True) + EPS)
    return (y * g) * (1.0 + scale) + shift


def _rms(x, g):
    return x * lax.rsqrt(jnp.mean(x * x, axis=-1, keepdims=True) + EPS) * g


def _proj_kernel(x_ref, g_ref, sh_ref, sc_ref, w_ref, cos_ref, sin_ref,
                 gq_ref, wq_ref, gkv_ref, wkv_ref,
                 rq_ref, rk_ref, rv_ref, rg_ref, fq_ref, fk_ref, fv_ref,
                 mq_ref, mk_ref, kpe_ref, mv_ref, ff_ref):
    h = _norm_mod(x_ref[...], g_ref[...], sh_ref[...], sc_ref[...]).astype(BF16)
    cos_t = cos_ref[...]
    sin_t = sin_ref[...]
    lane = lax.broadcasted_iota(I32, cos_t.shape, 1)

    def proj(c0, width):
        return jnp.dot(h, w_ref[:, c0:c0 + width], preferred_element_type=F32)

    rq = proj(C_RQ, 256)
    rk = proj(C_RK, 256)
    for s in range(2):
        sl = slice(s * LANES, (s + 1) * LANES)
        rq_ref[:, sl] = _rope_slab(rq[:, sl], cos_t, sin_t, lane).astype(BF16)
        rk_ref[:, sl] = (_rope_slab(rk[:, sl], cos_t, sin_t, lane) * (RET_DK ** -0.5)).astype(BF16)
    rv_ref[...] = proj(C_RV, 512).astype(BF16)
    rg_ref[...] = proj(C_RG, 512).astype(BF16)
    fq_ref[...] = (proj(C_FQ, 512) * (FOX_DH ** -0.5 * LOG2E)).astype(BF16)
    fk_ref[...] = proj(C_FK, 512).astype(BF16)
    fv_ref[...] = proj(C_FV, 512).astype(BF16)

    tail = proj(C_TAIL, LANES)
    ff_ref[...] = tail
    kpe_ref[...] = jnp.where(lane < MLA_ROPE, _rope_slab(tail, cos_t, sin_t, lane), 0.0).astype(BF16)

    qn = _rms(proj(C_MQ, MLA_Q_RANK), gq_ref[...]).astype(BF16)
    qh = jnp.dot(qn, wq_ref[...], preferred_element_type=F32)
    q_scale = (MLA_NOPE + MLA_ROPE) ** -0.5 * LOG2E
    for hd in range(MLA_HEADS):
        c0 = hd * MLA_DQ
        mq_ref[:, c0:c0 + LANES] = (qh[:, c0:c0 + LANES] * q_scale).astype(BF16)
        pe = _rope_slab(qh[:, c0 + LANES:c0 + 2 * LANES], cos_t, sin_t, lane)
        mq_ref[:, c0 + LANES:c0 + 2 * LANES] = jnp.where(lane < MLA_ROPE, pe * q_scale, 0.0).astype(BF16)

    kvn = _rms(proj(C_MKV, MLA_KV_RANK), gkv_ref[...]).astype(BF16)
    kvh = jnp.dot(kvn, wkv_ref[...], preferred_element_type=F32)
    for hd in range(MLA_HEADS):
        c0 = hd * (MLA_NOPE + MLA_V)
        mk_ref[:, hd * MLA_NOPE:(hd + 1) * MLA_NOPE] = kvh[:, c0:c0 + MLA_NOPE].astype(BF16)
        mv_ref[:, hd * MLA_V:(hd + 1) * MLA_V] = kvh[:, c0 + MLA_NOPE:c0 + MLA_NOPE + MLA_V].astype(BF16)


def _proj(x2, S, g, sh, sc, w_all, cos_t, sin_t, gq, wq, gkv, wkv):
    T, D = x2.shape
    tm = _pick(S, 512)
    per_b = S // tm
    row = lambda i: (i, 0)
    const = lambda i: (0, 0)
    batch = lambda i: (i // per_b, 0, 0)
    widths = [256, 256, 512, 512, 512, 512, 512, MLA_HEADS * MLA_DQ, MLA_HEADS * MLA_NOPE, LANES,
              MLA_HEADS * MLA_V]
    out_shape = [jax.ShapeDtypeStruct((T, w), BF16) for w in widths]
    out_shape.append(jax.ShapeDtypeStruct((T, LANES), F32))
    out_specs = [pl.BlockSpec((tm, w), row) for w in widths] + [pl.BlockSpec((tm, LANES), row)]
    return pl.pallas_call(
        _proj_kernel,
        grid=(T // tm,),
        in_specs=[pl.BlockSpec((tm, D), row),
                  pl.BlockSpec((1, D), const),
                  pl.BlockSpec((None, 1, D), batch),
                  pl.BlockSpec((None, 1, D), batch),
                  pl.BlockSpec((D, D_IN_PAD), const),
                  pl.BlockSpec((tm, LANES), row),
                  pl.BlockSpec((tm, LANES), row),
                  pl.BlockSpec((1, MLA_Q_RANK), const),
                  pl.BlockSpec((MLA_Q_RANK, MLA_HEADS * MLA_DQ), const),
                  pl.BlockSpec((1, MLA_KV_RANK), const),
                  pl.BlockSpec((MLA_KV_RANK, MLA_HEADS * (MLA_NOPE + MLA_V)), const)],
        out_specs=out_specs,
        out_shape=out_shape,
        compiler_params=_cparams(("arbitrary",)),
        name="proj",
    )(x2, g, sh, sc, w_all, cos_t, sin_t, gq, wq, gkv, wkv)


def _split3(x):
    a = x.astype(BF16)
    r = x - a.astype(F32)
    b = r.astype(BF16)
    c = (r - b.astype(F32)).astype(BF16)
    return a, b, c


def _fox_gate_kernel(ff_ref, fb_ref, tri_ref, kb_ref, carry_ref, *, tiles_per_seq):
    @pl.when(pl.program_id(0) % tiles_per_seq == 0)
    def _():
        carry_ref[...] = jnp.zeros_like(carry_ref)

    tm = ff_ref.shape[0]
    lane = lax.broadcasted_iota(I32, (tm, LANES), 1)
    z = ff_ref[...] + fb_ref[...]
    ls = -(jnp.maximum(-z, 0.0) + jnp.log1p(jnp.exp(-jnp.abs(z))))
    ls = jnp.where((lane >= FF_LANE) & (lane < FF_LANE + FOX_HEADS), ls, 0.0)
    tri = tri_ref[...]
    f = carry_ref[...]
    for part in _split3(ls):
        f = f + jnp.dot(tri, part, preferred_element_type=F32)
    carry_ref[...] = f[tm - 1:tm, :]
    f2 = f * LOG2E
    for hd in range(FOX_HEADS):
        src = FF_LANE + hd
        g = jnp.where(lane == 0, pltpu.roll(f2, (LANES - src) % LANES, axis=1),
            jnp.where(lane == 1, pltpu.roll(f2, (LANES + 1 - src) % LANES, axis=1),
            jnp.where(lane == 2, pltpu.roll(f2, (LANES + 2 - src) % LANES, axis=1), 0.0)))
        hi, mid, lo = _split3(g)
        kb_ref[:, hd * LANES:(hd + 1) * LANES] = jnp.where(lane == 0, hi, jnp.where(lane == 1, mid, lo))


def _fox_gate(ffp, fb, S):
    T = ffp.shape[0]
    tm = _pick(S, 512)
    idx = jnp.arange(tm)
    tri = (idx[None, :] <= idx[:, None]).astype(BF16)
    fbv = jnp.zeros((1, LANES), F32).at[0, FF_LANE:FF_LANE + FOX_HEADS].set(fb)
    return pl.pallas_call(
        functools.partial(_fox_gate_kernel, tiles_per_seq=S // tm),
        grid=(T // tm,),
        in_specs=[pl.BlockSpec((tm, LANES), lambda i: (i, 0)),
                  pl.BlockSpec((1, LANES), lambda i: (0, 0)),
                  pl.BlockSpec((tm, tm), lambda i: (0, 0))],
        out_specs=pl.BlockSpec((tm, FOX_HEADS * LANES), lambda i: (i, 0)),
        out_shape=jax.ShapeDtypeStruct((T, FOX_HEADS * LANES), BF16),
        scratch_shapes=[pltpu.VMEM((1, LANES), F32)],
        compiler_params=_cparams(("arbitrary",)),
        name="fox_gate",
    )(ffp, fbv, tri)


def _retention_kernel(dchunk_ref, q_ref, k_ref, v_ref, g_ref, dmask_ref, din_ref, dout_ref,
                      o_ref, state_ref, *, n_chunks):
    @pl.when(pl.program_id(1) == 0)
    def _():
        state_ref[...] = jnp.zeros_like(state_ref)

    C = RET_CHUNK
    lane = lax.broadcasted_iota(I32, (C, LANES), 1)
    for ci in range(n_chunks):
        rows = slice(ci * C, (ci + 1) * C)
        for hd in range(RET_HEADS):
            slab = slice((hd // 2) * LANES, (hd // 2 + 1) * LANES)
            mine = (lane < RET_DK) if hd % 2 == 0 else (lane >= RET_DK)
            q = jnp.where(mine, q_ref[rows, slab], 0)
            k = jnp.where(mine, k_ref[rows, slab], 0)
            vcols = slice(hd * RET_DV, (hd + 1) * RET_DV)
            v = v_ref[rows, vcols]
            state = state_ref[hd]
            scores = lax.dot_general(q, k, (((1,), (1,)), ((), ())),
                                     preferred_element_type=F32) * dmask_ref[hd]
            inner = jnp.dot(scores.astype(BF16), v, preferred_element_type=F32)
            cross = jnp.dot(q, state.astype(BF16), preferred_element_type=F32) * din_ref[hd]
            o = inner + cross
            vd = (v.astype(F32) * dout_ref[hd]).astype(BF16)
            kv = lax.dot_general(k, vd, (((0,), (0,)), ((), ())), preferred_element_type=F32)
            state_ref[hd] = state * dchunk_ref[hd] + kv
            mu = jnp.mean(o, axis=-1, keepdims=True)
            d = o - mu
            var = jnp.mean(d * d, axis=-1, keepdims=True)
            on = d * lax.rsqrt(var + EPS)
            g = g_ref[rows, vcols].astype(F32)
            o_ref[rows, vcols] = (g * jax.nn.sigmoid(g) * on).astype(BF16)


def _retention(rq, rk, rv, rg, B, S):
    H, C = RET_HEADS, RET_CHUNK
    tr = _pick(S, 4 * C)
    n_chunks = tr // C
    log_gamma = jnp.log1p(-jnp.exp2(-5.0 - jnp.arange(H, dtype=F32)))
    idx = jnp.arange(C, dtype=F32)
    rel = idx[:, None] - idx[None, :]
    dmask = jnp.where(rel >= 0, jnp.exp(log_gamma[:, None, None] * jnp.maximum(rel, 0.0)), 0.0)
    decay_in = jnp.exp(log_gamma[:, None] * (idx + 1.0))
    decay_out = jnp.exp(log_gamma[:, None] * (C - 1.0 - idx))
    decay_chunk = jnp.exp(log_gamma * C)
    din = jnp.broadcast_to(decay_in[:, :, None], (H, C, RET_DV))
    dout = jnp.broadcast_to(decay_out[:, :, None], (H, C, RET_DV))
    tok = lambda b, i: (b, i, 0)
    const3 = lambda b, i: (0, 0, 0)
    return pl.pallas_call(
        functools.partial(_retention_kernel, n_chunks=n_chunks),
        grid=(B, S // tr),
        in_specs=[pl.BlockSpec(memory_space=pltpu.SMEM),
                  pl.BlockSpec((None, tr, H * RET_DK), tok),
                  pl.BlockSpec((None, tr, H * RET_DK), tok),
                  pl.BlockSpec((None, tr, H * RET_DV), tok),
                  pl.BlockSpec((None, tr, H * RET_DV), tok),
                  pl.BlockSpec((H, C, C), const3),
                  pl.BlockSpec((H, C, RET_DV), const3),
                  pl.BlockSpec((H, C, RET_DV), const3)],
        out_specs=pl.BlockSpec((None, tr, H * RET_DV), tok),
        out_shape=jax.ShapeDtypeStruct((B, S, H * RET_DV), BF16),
        scratch_shapes=[pltpu.VMEM((H, LANES, RET_DV), F32)],
        compiler_params=_cparams(("arbitrary", "arbitrary")),
        name="retention",
    )(decay_chunk, rq.reshape(B, S, -1), rk.reshape(B, S, -1), rv.reshape(B, S, -1),
      rg.reshape(B, S, -1), dmask, din, dout)


def _flash_kernel(q_ref, k_ref, ke_ref, v_ref, o_ref, m_ref, l_ref, acc_ref, s_ref, *, q_bias_cols):
    qi = pl.program_id(2)
    tq = q_ref.shape[0]
    tk = tq
    m_ref[...] = jnp.full(m_ref.shape, NEG_BIG, F32)
    l_ref[...] = jnp.zeros_like(l_ref)
    acc_ref[...] = jnp.zeros_like(acc_ref)
    q = q_ref[...]
    if q_bias_cols:
        lane = lax.broadcasted_iota(I32, (tq, LANES), 1)
        q = jnp.concatenate([q, jnp.where(lane < q_bias_cols, -1.0, 0.0).astype(BF16)], axis=1)

    def scores(j):
        start = pl.multiple_of(j * tk, tk)
        kj = jnp.concatenate([k_ref[pl.ds(start, tk), :], ke_ref[pl.ds(start, tk), :]], axis=1)
        return lax.dot_general(kj, q, (((1,), (1,)), ((), ())), preferred_element_type=F32)

    def update(s, j):
        m_prev = m_ref[...]
        m_new = jnp.maximum(m_prev, jnp.max(s, axis=0, keepdims=True))
        alpha = jnp.exp2(m_prev - m_new)
        p = jnp.exp2(s - m_new)
        l_ref[...] = alpha * l_ref[...] + jnp.sum(p, axis=0, keepdims=True)
        start = pl.multiple_of(j * tk, tk)
        vj = v_ref[pl.ds(start, tk), :]
        acc_ref[...] = alpha * acc_ref[...] + lax.dot_general(
            vj, p.astype(BF16), (((0,), (0,)), ((), ())), preferred_element_type=F32)
        m_ref[...] = m_new

    def causal(s):
        key = lax.broadcasted_iota(I32, s.shape, 0)
        qry = lax.broadcasted_iota(I32, s.shape, 1)
        return jnp.where(key <= qry, s, NEG_BIG)

    s_ref[0] = scores(0)

    def pair(p, carry):
        j = 2 * p
        s = s_ref[0]
        s_ref[1] = scores(j + 1)
        update(s, j)
        s = s_ref[1]
        s_ref[0] = scores(j + 2)
        update(s, j + 1)
        return carry

    lax.fori_loop(0, qi // 2, pair, 0)

    @pl.when(qi % 2 == 1)
    def _():
        s = s_ref[0]
        s_ref[1] = scores(qi)
        update(s, qi - 1)
        update(causal(s_ref[1]), qi)

    @pl.when(qi % 2 == 0)
    def _():
        update(causal(s_ref[0]), qi)

    o_ref[...] = (acc_ref[...] / l_ref[...]).T.astype(o_ref.dtype)


def _flash(q, k, ke, v, B, S, H, dq, ke_per_head, q_bias_cols, name):
    tq = _pick(S, 512)
    nq = S // tq
    ke_map = (lambda b, h, i: (b, 0, h)) if ke_per_head else (lambda b, h, i: (b, 0, 0))
    return pl.pallas_call(
        functools.partial(_flash_kernel, q_bias_cols=q_bias_cols),
        grid=(B, H, nq),
        in_specs=[pl.BlockSpec((None, tq, dq), lambda b, h, i: (b, i, h)),
                  pl.BlockSpec((None, S, LANES), lambda b, h, i: (b, 0, h)),
                  pl.BlockSpec((None, S, LANES), ke_map),
                  pl.BlockSpec((None, S, LANES), lambda b, h, i: (b, 0, h))],
        out_specs=pl.BlockSpec((None, tq, LANES), lambda b, h, i: (b, i, h)),
        out_shape=jax.ShapeDtypeStruct((B, S, H * LANES), BF16),
        scratch_shapes=[pltpu.VMEM((1, tq), F32), pltpu.VMEM((1, tq), F32),
                        pltpu.VMEM((LANES, tq), F32), pltpu.VMEM((2, tq, tq), F32)],
        compiler_params=_cparams(("arbitrary", "arbitrary", "arbitrary")),
        name=name,
    )(q, k, ke, v)


def _merge_kernel(x_ref, ya_ref, yb_ref, yc_ref, g1_ref, sh1_ref, sc1_ref, gt1_ref,
                  g2_ref, sh2_ref, sc2_ref, gw_ref, gb_ref, bw_ref, ow_ref,
                  rwh_ref, rwl_ref, rb_ref, tri_ref,
                  x1_ref, h2_ref, mi_ref, mf_ref, cnt_ref, h_ref, mg_ref, carry_ref):
    tm, D = x_ref.shape
    cn = MERGE_COL_CHUNK

    @pl.when(pl.program_id(0) == 0)
    def _():
        carry_ref[...] = jnp.zeros_like(carry_ref)

    h_ref[...] = _norm_mod(x_ref[...], g1_ref[...], sh1_ref[...], sc1_ref[...]).astype(BF16)
    for n in range(D // cn):
        cols = slice(n * cn, (n + 1) * cn)
        merged = None
        for i, y_ref in enumerate((ya_ref, yb_ref, yc_ref)):
            gcols = slice(i * D + n * cn, i * D + (n + 1) * cn)
            gate = jax.nn.sigmoid(jnp.dot(h_ref[...], gw_ref[:, gcols], preferred_element_type=F32)
                                  + gb_ref[:, gcols])
            br = jnp.dot(y_ref[...], bw_ref[i, :, cols], preferred_element_type=F32)
            merged = gate * br if merged is None else merged + gate * br
        mg_ref[:, cols] = merged.astype(BF16)
    for n in range(D // cn):
        cols = slice(n * cn, (n + 1) * cn)
        mix = jnp.dot(mg_ref[...], ow_ref[:, cols], preferred_element_type=F32)
        x1_ref[:, cols] = x_ref[:, cols] + gt1_ref[:, cols] * mix
    h2 = _norm_mod(x1_ref[...], g2_ref[...], sh2_ref[...], sc2_ref[...])
    h2_ref[...] = _pack_bf16_pair(h2[:, :D // 2], h2[:, D // 2:])

    hh = h2.astype(BF16)
    hl = (h2 - hh.astype(F32)).astype(BF16)
    nt = lambda a, b: lax.dot_general(a, b, (((1,), (1,)), ((), ())), preferred_element_type=F32)
    lt = nt(rwh_ref[...], hh) + nt(rwh_ref[...], hl) + nt(rwl_ref[...], hh) + rb_ref[...]
    row8 = lax.broadcasted_iota(I32, (EXP_PER_GROUP, tm), 0)
    gl = jnp.where(row8 < N_GROUPS, lt[N_EXPERTS:N_EXPERTS + EXP_PER_GROUP, :], -jnp.inf)
    gmax = jnp.max(gl, axis=0, keepdims=True)
    g_idx = jnp.min(jnp.where(gl == gmax, row8, EXP_PER_GROUP), axis=0, keepdims=True)
    g_w = 1.0 / jnp.sum(jnp.exp(gl - gmax), axis=0, keepdims=True)
    el = lt[(N_GROUPS - 1) * EXP_PER_GROUP:N_EXPERTS, :]
    for g in range(N_GROUPS - 2, -1, -1):
        el = jnp.where(g_idx == g, lt[g * EXP_PER_GROUP:(g + 1) * EXP_PER_GROUP, :], el)
    e1 = jnp.max(el, axis=0, keepdims=True)
    i1 = jnp.min(jnp.where(el == e1, row8, EXP_PER_GROUP), axis=0, keepdims=True)
    el2 = jnp.where(row8 == i1, -jnp.inf, el)
    e2 = jnp.max(el2, axis=0, keepdims=True)
    i2 = jnp.min(jnp.where(el2 == e2, row8, EXP_PER_GROUP), axis=0, keepdims=True)
    r = jnp.exp(e2 - e1)
    w1 = g_w / (1.0 + r)
    w2 = g_w * r / (1.0 + r)
    eid1 = g_idx * EXP_PER_GROUP + i1
    eid2 = g_idx * EXP_PER_GROUP + i2

    rowe = lax.broadcasted_iota(I32, (N_EXPERTS, tm), 0)
    hit1 = rowe == eid1
    hit2 = rowe == eid2
    onehot = jnp.where(hit1 | hit2, 1.0, 0.0)
    before = (jnp.dot(onehot.astype(BF16), tri_ref[...], preferred_element_type=F32)
              + jnp.concatenate([carry_ref[...]] * (tm // LANES), axis=1))
    rank1 = jnp.sum(jnp.where(hit1, before, 0.0), axis=0, keepdims=True)
    rank2 = jnp.sum(jnp.where(hit2, before, 0.0), axis=0, keepdims=True)
    carry_ref[...] = carry_ref[...] + jnp.sum(onehot, axis=1, keepdims=True)
    cnt_ref[...] = carry_ref[...]

    mi_ref[...] = jnp.where(row8 == 0, eid1,
                  jnp.where(row8 == 1, eid2,
                  jnp.where(row8 == 2, rank1.astype(I32),
                  jnp.where(row8 == 3, rank2.astype(I32), 0))))
    mf_ref[...] = jnp.where(row8 == 0, w1, jnp.where(row8 == 1, w2, 0.0))


def _merge(x2, S, ya, yb, yc, g1, sh1, sc1, gt1, g2, sh2, sc2, gw, gb, bw, ow, rwh, rwl, rb):
    T, D = x2.shape
    tm = _pick(S, 512)
    per_b = S // tm
    row = lambda i: (i, 0)
    col = lambda i: (0, i)
    const = lambda i: (0, 0)
    batch = lambda i: (i // per_b, 0, 0)
    idx = jnp.arange(tm)
    tri = (idx[:, None] < idx[None, :]).astype(BF16)
    vecb = pl.BlockSpec((None, 1, D), batch)
    return pl.pallas_call(
        _merge_kernel,
        grid=(T // tm,),
        in_specs=[pl.BlockSpec((tm, D), row),
                  pl.BlockSpec((tm, BRANCH_W), row),
                  pl.BlockSpec((tm, BRANCH_W), row),
                  pl.BlockSpec((tm, BRANCH_W), row),
                  pl.BlockSpec((1, D), const), vecb, vecb, vecb,
                  pl.BlockSpec((1, D), const), vecb, vecb,
                  pl.BlockSpec((D, N_BRANCH * D), const),
                  pl.BlockSpec((1, N_BRANCH * D), const),
                  pl.BlockSpec((N_BRANCH, BRANCH_W, D), lambda i: (0, 0, 0)),
                  pl.BlockSpec((D, D), const),
                  pl.BlockSpec((LANES, D), const),
                  pl.BlockSpec((LANES, D), const),
                  pl.BlockSpec((LANES, 1), const),
                  pl.BlockSpec((tm, tm), const)],
        out_specs=[pl.BlockSpec((tm, D), row), pl.BlockSpec((tm, D // 2), row),
                   pl.BlockSpec((8, tm), col), pl.BlockSpec((8, tm), col),
                   pl.BlockSpec((N_EXPERTS, LANES), const)],
        out_shape=[jax.ShapeDtypeStruct((T, D), F32), jax.ShapeDtypeStruct((T, D // 2), U32),
                   jax.ShapeDtypeStruct((8, T), I32), jax.ShapeDtypeStruct((8, T), F32),
                   jax.ShapeDtypeStruct((N_EXPERTS, LANES), F32)],
        scratch_shapes=[pltpu.VMEM((tm, D), BF16), pltpu.VMEM((tm, D), BF16),
                        pltpu.VMEM((N_EXPERTS, LANES), F32)],
        compiler_params=_cparams(("arbitrary",)),
        name="merge",
    )(x2, ya, yb, yc, g1, sh1, sc1, gt1, g2, sh2, sc2, gw, gb, bw, ow, rwh, rwl, rb, tri)


def _expert_kernel(blk_e_ref, nvalid_ref, xs_ref, w1_ref, w3_ref, w2_ref, ys_ref):
    del blk_e_ref
    valid = pl.program_id(0) < nvalid_ref[0]

    @pl.when(valid)
    def _():
        half = xs_ref.shape[1]
        x_lo, x_hi = _unpack_bf16_pair(xs_ref[...])
        x_lo = x_lo.astype(BF16)
        x_hi = x_hi.astype(BF16)

        def up(w_ref):
            return (jnp.dot(x_lo, w_ref[:half, :], preferred_element_type=F32)
                    + jnp.dot(x_hi, w_ref[half:, :], preferred_element_type=F32))

        a = up(w1_ref)
        b = up(w3_ref)
        hid = (a * jax.nn.sigmoid(a) * b).astype(BF16)
        y = jnp.dot(hid, w2_ref[...], preferred_element_type=F32)
        ys_ref[...] = _pack_bf16_pair(y[:, :half], y[:, half:])

    @pl.when(jnp.logical_not(valid))
    def _():
        ys_ref[...] = jnp.zeros_like(ys_ref)


def _experts(xs, blk_e, nvalid, w1, w3, w2, tb):
    P, half = xs.shape
    D = 2 * half
    n_blocks = P // tb
    rows = lambda i, be, nv: (jnp.minimum(i, nv[0] - 1), 0)
    grid_spec = pltpu.PrefetchScalarGridSpec(
        num_scalar_prefetch=2,
        grid=(n_blocks,),
        in_specs=[pl.BlockSpec((tb, half), rows),
                  pl.BlockSpec((None, D, D_EXPERT), lambda i, be, nv: (be[i], 0, 0)),
                  pl.BlockSpec((None, D, D_EXPERT), lambda i, be, nv: (be[i], 0, 0)),
                  pl.BlockSpec((None, D_EXPERT, D), lambda i, be, nv: (be[i], 0, 0))],
        out_specs=pl.BlockSpec((tb, half), lambda i, be, nv: (i, 0)),
    )
    return pl.pallas_call(
        _expert_kernel,
        grid_spec=grid_spec,
        out_shape=jax.ShapeDtypeStruct((P, half), U32),
        compiler_params=_cparams(("arbitrary",)),
        name="moe_experts",
    )(blk_e, nvalid, xs, w1, w3, w2)


def _sc_gather(data, idx):
    M = idx.shape[0]
    D = data.shape[1]
    W = SC_GATHER_WINDOW
    assert M % W == 0, (M, W)
    mesh = plsc.VectorSubcoreMesh(core_axis_name="core", subcore_axis_name="subcore")
    n_workers = mesh.num_cores * mesh.num_subcores
    assert M % (W * n_workers) == 0, (M, W, n_workers)

    @functools.partial(pl.kernel, out_type=jax.ShapeDtypeStruct((M, D), data.dtype), mesh=mesh,
                       scratch_types=[pltpu.VMEM((W,), I32), pltpu.VMEM((W, D), data.dtype)])
    def gather_kernel(x_hbm, i_hbm, o_hbm, i_vmem, buf):
        worker = lax.axis_index("core") * mesh.num_subcores + lax.axis_index("subcore")

        @pl.loop(0, M // (W * n_workers))
        def _(t):
            start = (t * n_workers + worker) * W
            pltpu.sync_copy(i_hbm.at[pl.ds(start, W)], i_vmem)
            pltpu.sync_copy(x_hbm.at[i_vmem], buf)
            pltpu.sync_copy(buf, o_hbm.at[pl.ds(start, W)])

    return gather_kernel(data, idx)


def _sc_dispatch(h2, dest, fill_idx, P):
    T, D = h2.shape
    W = SC_GATHER_WINDOW
    n_fill = fill_idx.shape[0]
    mesh = plsc.VectorSubcoreMesh(core_axis_name="core", subcore_axis_name="subcore")
    n_workers = mesh.num_cores * mesh.num_subcores
    assert T % (W * n_workers) == 0 and n_fill % (W * n_workers) == 0, (T, n_fill, W, n_workers)
    zeros = jnp.zeros((W, D), h2.dtype)

    @functools.partial(pl.kernel, out_type=jax.ShapeDtypeStruct((P, D), h2.dtype), mesh=mesh,
                       scratch_types=[pltpu.VMEM((W,), I32), pltpu.VMEM((W, D), h2.dtype)])
    def dispatch_kernel(h_hbm, d_hbm, f_hbm, z_hbm, o_hbm, i_vmem, buf):
        worker = lax.axis_index("core") * mesh.num_subcores + lax.axis_index("subcore")

        @pl.loop(0, T // (W * n_workers))
        def _(t):
            start = (t * n_workers + worker) * W
            pltpu.sync_copy(h_hbm.at[pl.ds(start, W)], buf)
            for c in range(2):
                pltpu.sync_copy(d_hbm.at[pl.ds(c * T + start, W)], i_vmem)
                pltpu.sync_copy(buf, o_hbm.at[i_vmem])

        pltpu.sync_copy(z_hbm, buf)

        @pl.loop(0, n_fill // (W * n_workers))
        def _(t):
            start = (t * n_workers + worker) * W
            pltpu.sync_copy(f_hbm.at[pl.ds(start, W)], i_vmem)
            pltpu.sync_copy(buf, o_hbm.at[i_vmem])

    return dispatch_kernel(h2, dest, fill_idx, zeros)


def _combine_kernel(x_ref, g0_ref, g1_ref, mf_ref, gt_ref, fg_ref, o_ref, *, final):
    mf = mf_ref[...]
    lo0, hi0 = _unpack_bf16_pair(g0_ref[...])
    lo1, hi1 = _unpack_bf16_pair(g1_ref[...])
    w0 = mf[:, 0:1]
    w1 = mf[:, 1:2]
    ffn = jnp.concatenate([lo0 * w0 + lo1 * w1, hi0 * w0 + hi1 * w1], axis=1)
    out = x_ref[...] + gt_ref[...] * ffn
    if final:
        out = _rms(out, fg_ref[...])
    o_ref[...] = out


def _combine(x1, S, mf, gt2, final_g, g, final):
    T, D = x1.shape
    tm = _pick(S, 512)
    per_b = S // tm
    nt = T // tm
    return pl.pallas_call(
        functools.partial(_combine_kernel, final=final),
        grid=(nt,),
        in_specs=[pl.BlockSpec((tm, D), lambda i: (i, 0)),
                  pl.BlockSpec((tm, D // 2), lambda i: (i, 0)),
                  pl.BlockSpec((tm, D // 2), lambda i: (i + nt, 0)),
                  pl.BlockSpec((tm, 2), lambda i: (i, 0)),
                  pl.BlockSpec((None, 1, D), lambda i: (i // per_b, 0, 0)),
                  pl.BlockSpec((1, D), lambda i: (0, 0))],
        out_specs=pl.BlockSpec((tm, D), lambda i: (i, 0)),
        out_shape=jax.ShapeDtypeStruct((T, D), F32),
        compiler_params=_cparams(("arbitrary",)),
        name="moe_combine",
    )(x1, g, g, mf, gt2, final_g)


def _prep_w_in(w):
    offs = [0]
    for s in (256, 256, 512, 512, 512, 512, 512, FOX_HEADS, MLA_Q_RANK, MLA_KV_RANK, MLA_ROPE):
        offs.append(offs[-1] + s)
    rq, rk, rv, rg, fq, fk, fv, ff, mq, mkv, mkr = [w[:, offs[i]:offs[i + 1]] for i in range(11)]
    pad = jnp.zeros((w.shape[0], LANES - MLA_ROPE - FOX_HEADS), w.dtype)
    return jnp.concatenate([rq, rk, rv, rg, fq, fk, fv, mq, mkv, mkr, ff, pad], axis=1).astype(BF16)


def _prep_wq_up(w):
    r = w.reshape(MLA_Q_RANK, MLA_HEADS, MLA_NOPE + MLA_ROPE)
    r = jnp.pad(r, ((0, 0), (0, 0), (0, MLA_DQ - MLA_NOPE - MLA_ROPE)))
    return r.reshape(MLA_Q_RANK, MLA_HEADS * MLA_DQ).astype(BF16)


def _prep_router(w_grp, b_grp, w_exp, b_exp):
    D = w_grp.shape[0]
    pad = LANES - N_EXPERTS - N_GROUPS
    rwt = jnp.concatenate([w_exp, w_grp, jnp.zeros((D, pad), F32)], axis=1).astype(F32).T
    rwh = rwt.astype(BF16)
    rwl = (rwt - rwh.astype(F32)).astype(BF16)
    rb = jnp.concatenate([b_exp, b_grp, jnp.zeros((pad,), F32)]).astype(F32).reshape(LANES, 1)
    return rwh, rwl, rb


def kernel(x, c, positions, ada_w, ada_b, norm1_g, norm2_g, w_in, fox_fb, mla_q_norm_g, mla_wq_up, mla_kv_norm_g, mla_wkv_up, gate_w, gate_b, branch_w, out_w, router_grp_w, router_grp_b, router_exp_w, router_exp_b, exp_w1, exp_w3, exp_w2, final_g):
    B, S, D = x.shape
    L = ada_w.shape[0]
    T = B * S
    A = 2 * T
    tb = _pick(A, 256)
    n_blocks = A // tb + N_EXPERTS
    P = n_blocks * tb

    mod = _adaln(c, ada_w, ada_b)
    cos_t, sin_t = _rope_tables(positions)
    x2 = x.reshape(T, D)
    final_g2 = final_g.reshape(1, D)

    for l in range(L):
        sh1, sc1, gt1, sh2, sc2, gt2 = [mod[l, :, i * D:(i + 1) * D].reshape(B, 1, D) for i in range(6)]
        g1 = norm1_g[l].reshape(1, D)
        g2 = norm2_g[l].reshape(1, D)
        (rq, rk, rv, rg, fq, fk, fv, mq, mk, kpe, mv, ffp) = _proj(
            x2, S, g1, sh1, sc1, _prep_w_in(w_in[l]), cos_t, sin_t,
            mla_q_norm_g[l].reshape(1, -1), _prep_wq_up(mla_wq_up[l]),
            mla_kv_norm_g[l].reshape(1, -1), mla_wkv_up[l].astype(BF16))

        ya = _retention(rq, rk, rv, rg, B, S).reshape(T, -1)
        kb = _fox_gate(ffp, fox_fb[l], S)
        r3 = lambda a: a.reshape(B, S, -1)
        yb = _flash(r3(fq), r3(fk), r3(kb), r3(fv), B, S, FOX_HEADS, FOX_DH, True, 3, "flash_fox").reshape(T, -1)
        yc = _flash(r3(mq), r3(mk), r3(kpe), r3(mv), B, S, MLA_HEADS, MLA_DQ, False, 0, "flash_mla").reshape(T, -1)

        rwh, rwl, rb = _prep_router(router_grp_w[l], router_grp_b[l], router_exp_w[l], router_exp_b[l])
        x1, h2, mi, mf, cnt = _merge(
            x2, S, ya, yb, yc, g1, sh1, sc1, gt1, g2, sh2, sc2,
            gate_w[l].astype(BF16), gate_b[l].reshape(1, -1), branch_w[l].astype(BF16),
            out_w[l].astype(BF16), rwh, rwl, rb)

        counts = cnt[:, 0].astype(I32)
        pcounts = (counts + tb - 1) // tb * tb
        pends = jnp.cumsum(pcounts)
        pstarts = pends - pcounts
        sel = mi[0:2, :, None] == jnp.arange(N_EXPERTS, dtype=I32)
        dest = (jnp.sum(jnp.where(sel, pstarts, 0), axis=-1) + mi[2:4]).reshape(A)
        blk_pos = jnp.arange(n_blocks, dtype=I32) * tb
        blk_e = jnp.minimum(jnp.sum((pends[None, :] <= blk_pos[:, None]).astype(I32), axis=1), N_EXPERTS - 1)
        nvalid = (pends[-1:] // tb).astype(I32)
        fr = jnp.arange(tb, dtype=I32)[None, :]
        is_pad = (fr < (pcounts - counts)[:, None]).reshape(-1)
        pad_slot = ((pstarts + counts)[:, None] + fr).reshape(-1)
        tail_rank = jnp.cumsum(jnp.logical_not(is_pad).astype(I32)) - 1
        fill_idx = jnp.where(is_pad, pad_slot, pends[-1] + tail_rank)

        xs = _sc_dispatch(h2, dest, fill_idx, P)
        ys = _experts(xs, blk_e, nvalid, exp_w1[l].astype(BF16), exp_w3[l].astype(BF16),
                      exp_w2[l].astype(BF16), tb)
        g = _sc_gather(ys, dest)
        x2 = _combine(x1, S, mf[0:2].T, gt2, final_g2, g, final=(l == L - 1))

    return x2.reshape(B, S, D)
```

```python
import functools
import math

import jax
import jax.numpy as jnp
from jax import lax
from jax.experimental import pallas as pl
from jax.experimental.pallas import tpu as pltpu
from jax.experimental.pallas import tpu_sc as plsc

F32 = jnp.float32
BF16 = jnp.bfloat16
I32 = jnp.int32
U32 = jnp.uint32
HIGHEST = lax.Precision.HIGHEST

EPS = 1e-6
ROPE_THETA = 10000.0
RET_HEADS = 4
RET_DK = 64
RET_DV = 128
RET_CHUNK = 128
FOX_HEADS = 4
FOX_DH = 128
MLA_HEADS = 4
MLA_Q_RANK = 256
MLA_KV_RANK = 128
MLA_NOPE = 128
MLA_ROPE = 64
MLA_V = 128
MLA_DQ = 256
N_BRANCH = 3
BRANCH_W = 512
N_GROUPS = 4
EXP_PER_GROUP = 8
N_EXPERTS = N_GROUPS * EXP_PER_GROUP
D_EXPERT = 512

LANES = 128
V7X_VMEM_LIMIT = 56 * 1024 * 1024

C_RQ, C_RK, C_RV, C_RG = 0, 256, 512, 1024
C_FQ, C_FK, C_FV = 1536, 2048, 2560
C_MQ, C_MKV, C_TAIL = 3072, 3328, 3456
D_IN_PAD = 3584
FF_LANE = MLA_ROPE

NEG_BIG = -1e30
SC_GATHER_WINDOW = 128
MERGE_COL_CHUNK = 256
LOG2E = math.log2(math.e)


def _cparams(sem):
    return pltpu.CompilerParams(dimension_semantics=sem, vmem_limit_bytes=V7X_VMEM_LIMIT)


def _pick(n, pref):
    t = min(n, pref)
    assert n % t == 0, (n, t)
    return t


def _adaln_kernel(c_ref, w_ref, b_ref, o_ref):
    c = c_ref[...]
    ca = c * jax.nn.sigmoid(c)
    o_ref[...] = jnp.dot(ca, w_ref[...], preferred_element_type=F32, precision=HIGHEST) + b_ref[...]


def _adaln(c, ada_w, ada_b):
    L, D, N = ada_w.shape
    B = c.shape[0]
    tn = _pick(N, 1536)
    return pl.pallas_call(
        _adaln_kernel,
        grid=(L, N // tn),
        in_specs=[pl.BlockSpec((B, D), lambda l, j: (0, 0)),
                  pl.BlockSpec((None, D, tn), lambda l, j: (l, 0, j)),
                  pl.BlockSpec((None, 1, tn), lambda l, j: (l, 0, j))],
        out_specs=pl.BlockSpec((None, B, tn), lambda l, j: (l, 0, j)),
        out_shape=jax.ShapeDtypeStruct((L, B, N), F32),
        compiler_params=_cparams(("arbitrary", "arbitrary")),
        name="adaln",
    )(c, ada_w, ada_b.reshape(L, 1, N))


def _rope_table_kernel(pos_ref, inv_ref, sign_ref, cos_ref, sin_ref):
    ang = pos_ref[...].astype(F32) * inv_ref[...]
    cos_ref[...] = jnp.cos(ang)
    sin_ref[...] = jnp.sin(ang) * sign_ref[...]


def _rope_tables(positions):
    T = positions.size
    tm = _pick(T, 2048)
    half = MLA_ROPE // 2
    inv = ROPE_THETA ** (-jnp.arange(0, MLA_ROPE, 2, dtype=F32) / MLA_ROPE)
    inv_t = jnp.tile(inv, LANES // half).reshape(1, LANES)
    sign = jnp.where((jnp.arange(LANES) % MLA_ROPE) < half, -1.0, 1.0).astype(F32).reshape(1, LANES)
    return pl.pallas_call(
        _rope_table_kernel,
        grid=(T // tm,),
        in_specs=[pl.BlockSpec((tm, 1), lambda i: (i, 0)),
                  pl.BlockSpec((1, LANES), lambda i: (0, 0)),
                  pl.BlockSpec((1, LANES), lambda i: (0, 0))],
        out_specs=[pl.BlockSpec((tm, LANES), lambda i: (i, 0))] * 2,
        out_shape=[jax.ShapeDtypeStruct((T, LANES), F32)] * 2,
        compiler_params=_cparams(("arbitrary",)),
        name="rope_tables",
    )(positions.reshape(T, 1), inv_t, sign)


def _rope_slab(x, cos_t, sin_t, lane):
    nxt = pltpu.roll(x, LANES - 32, axis=1)
    prv = pltpu.roll(x, 32, axis=1)
    swapped = jnp.where((lane & 32) == 0, nxt, prv)
    return x * cos_t + swapped * sin_t


def _pack_bf16_pair(lo, hi):
    lo_bits = lax.shift_right_logical(lax.bitcast_convert_type(lo.astype(BF16).astype(F32), U32), jnp.uint32(16))
    hi_bits = lax.bitcast_convert_type(hi.astype(BF16).astype(F32), U32) & jnp.uint32(0xFFFF0000)
    return hi_bits | lo_bits


def _unpack_bf16_pair(w):
    lo = lax.bitcast_convert_type(lax.shift_left(w, jnp.uint32(16)), F32)
    hi = lax.bitcast_convert_type(w & jnp.uint32(0xFFFF0000), F32)
    return lo, hi


def _norm_mod(x, g, shift, scale):
    y = x * lax.rsqrt(jnp.mean(x * x, axis=-1, keepdims=True) + EPS)
    return (y * g) * (1.0 + scale) + shift


def _rms(x, g):
    return x * lax.rsqrt(jnp.mean(x * x, axis=-1, keepdims=True) + EPS) * g


def _proj_kernel(x_ref, g_ref, sh_ref, sc_ref, w_ref, cos_ref, sin_ref,
                 gq_ref, wq_ref, gkv_ref, wkv_ref,
                 rq_ref, rk_ref, rv_ref, rg_ref, fq_ref, fk_ref, fv_ref,
                 mq_ref, mk_ref, kpe_ref, mv_ref, ff_ref):
    h = _norm_mod(x_ref[...], g_ref[...], sh_ref[...], sc_ref[...]).astype(BF16)
    cos_t = cos_ref[...]
    sin_t = sin_ref[...]
    lane = lax.broadcasted_iota(I32, cos_t.shape, 1)

    def proj(c0, width):
        return jnp.dot(h, w_ref[:, c0:c0 + width], preferred_element_type=F32)

    rq = proj(C_RQ, 256)
    rk = proj(C_RK, 256)
    for s in range(2):
        sl = slice(s * LANES, (s + 1) * LANES)
        rq_ref[:, sl] = _rope_slab(rq[:, sl], cos_t, sin_t, lane).astype(BF16)
        rk_ref[:, sl] = (_rope_slab(rk[:, sl], cos_t, sin_t, lane) * (RET_DK ** -0.5)).astype(BF16)
    rv_ref[...] = proj(C_RV, 512).astype(BF16)
    rg_ref[...] = proj(C_RG, 512).astype(BF16)
    fq_ref[...] = (proj(C_FQ, 512) * (FOX_DH ** -0.5 * LOG2E)).astype(BF16)
    fk_ref[...] = proj(C_FK, 512).astype(BF16)
    fv_ref[...] = proj(C_FV, 512).astype(BF16)

    tail = proj(C_TAIL, LANES)
    ff_ref[...] = tail
    kpe_ref[...] = jnp.where(lane < MLA_ROPE, _rope_slab(tail, cos_t, sin_t, lane), 0.0).astype(BF16)

    qn = _rms(proj(C_MQ, MLA_Q_RANK), gq_ref[...]).astype(BF16)
    qh = jnp.dot(qn, wq_ref[...], preferred_element_type=F32)
    q_scale = (MLA_NOPE + MLA_ROPE) ** -0.5 * LOG2E
    for hd in range(MLA_HEADS):
        c0 = hd * MLA_DQ
        mq_ref[:, c0:c0 + LANES] = (qh[:, c0:c0 + LANES] * q_scale).astype(BF16)
        pe = _rope_slab(qh[:, c0 + LANES:c0 + 2 * LANES], cos_t, sin_t, lane)
        mq_ref[:, c0 + LANES:c0 + 2 * LANES] = jnp.where(lane < MLA_ROPE, pe * q_scale, 0.0).astype(BF16)

    kvn = _rms(proj(C_MKV, MLA_KV_RANK), gkv_ref[...]).astype(BF16)
    kvh = jnp.dot(kvn, wkv_ref[...], preferred_element_type=F32)
    for hd in range(MLA_HEADS):
        c0 = hd * (MLA_NOPE + MLA_V)
        mk_ref[:, hd * MLA_NOPE:(hd + 1) * MLA_NOPE] = kvh[:, c0:c0 + MLA_NOPE].astype(BF16)
        mv_ref[:, hd * MLA_V:(hd + 1) * MLA_V] = kvh[:, c0 + MLA_NOPE:c0 + MLA_NOPE + MLA_V].astype(BF16)


def _proj(x2, S, g, sh, sc, w_all, cos_t, sin_t, gq, wq, gkv, wkv):
    T, D = x2.shape
    tm = _pick(S, 512)
    per_b = S // tm
    row = lambda i: (i, 0)
    const = lambda i: (0, 0)
    batch = lambda i: (i // per_b, 0, 0)
    widths = [256, 256, 512, 512, 512, 512, 512, MLA_HEADS * MLA_DQ, MLA_HEADS * MLA_NOPE, LANES,
              MLA_HEADS * MLA_V]
    out_shape = [jax.ShapeDtypeStruct((T, w), BF16) for w in widths]
    out_shape.append(jax.ShapeDtypeStruct((T, LANES), F32))
    out_specs = [pl.BlockSpec((tm, w), row) for w in widths] + [pl.BlockSpec((tm, LANES), row)]
    return pl.pallas_call(
        _proj_kernel,
        grid=(T // tm,),
        in_specs=[pl.BlockSpec((tm, D), row),
                  pl.BlockSpec((1, D), const),
                  pl.BlockSpec((None, 1, D), batch),
                  pl.BlockSpec((None, 1, D), batch),
                  pl.BlockSpec((D, D_IN_PAD), const),
                  pl.BlockSpec((tm, LANES), row),
                  pl.BlockSpec((tm, LANES), row),
                  pl.BlockSpec((1, MLA_Q_RANK), const),
                  pl.BlockSpec((MLA_Q_RANK, MLA_HEADS * MLA_DQ), const),
                  pl.BlockSpec((1, MLA_KV_RANK), const),
                  pl.BlockSpec((MLA_KV_RANK, MLA_HEADS * (MLA_NOPE + MLA_V)), const)],
        out_specs=out_specs,
        out_shape=out_shape,
        compiler_params=_cparams(("arbitrary",)),
        name="proj",
    )(x2, g, sh, sc, w_all, cos_t, sin_t, gq, wq, gkv, wkv)


def _split3(x):
    a = x.astype(BF16)
    r = x - a.astype(F32)
    b = r.astype(BF16)
    c = (r - b.astype(F32)).astype(BF16)
    return a, b, c


def _fox_gate_kernel(ff_ref, fb_ref, tri_ref, kb_ref, carry_ref, *, tiles_per_seq):
    @pl.when(pl.program_id(0) % tiles_per_seq == 0)
    def _():
        carry_ref[...] = jnp.zeros_like(carry_ref)

    tm = ff_ref.shape[0]
    lane = lax.broadcasted_iota(I32, (tm, LANES), 1)
    z = ff_ref[...] + fb_ref[...]
    ls = -(jnp.maximum(-z, 0.0) + jnp.log1p(jnp.exp(-jnp.abs(z))))
    ls = jnp.where((lane >= FF_LANE) & (lane < FF_LANE + FOX_HEADS), ls, 0.0)
    tri = tri_ref[...]
    f = carry_ref[...]
    for part in _split3(ls):
        f = f + jnp.dot(tri, part, preferred_element_type=F32)
    carry_ref[...] = f[tm - 1:tm, :]
    f2 = f * LOG2E
    for hd in range(FOX_HEADS):
        src = FF_LANE + hd
        g = jnp.where(lane == 0, pltpu.roll(f2, (LANES - src) % LANES, axis=1),
            jnp.where(lane == 1, pltpu.roll(f2, (LANES + 1 - src) % LANES, axis=1),
            jnp.where(lane == 2, pltpu.roll(f2, (LANES + 2 - src) % LANES, axis=1), 0.0)))
        hi, mid, lo = _split3(g)
        kb_ref[:, hd * LANES:(hd + 1) * LANES] = jnp.where(lane == 0, hi, jnp.where(lane == 1, mid, lo))


def _fox_gate(ffp, fb, S):
    T = ffp.shape[0]
    tm = _pick(S, 512)
    idx = jnp.arange(tm)
    tri = (idx[None, :] <= idx[:, None]).astype(BF16)
    fbv = jnp.zeros((1, LANES), F32).at[0, FF_LANE:FF_LANE + FOX_HEADS].set(fb)
    return pl.pallas_call(
        functools.partial(_fox_gate_kernel, tiles_per_seq=S // tm),
        grid=(T // tm,),
        in_specs=[pl.BlockSpec((tm, LANES), lambda i: (i, 0)),
                  pl.BlockSpec((1, LANES), lambda i: (0, 0)),
                  pl.BlockSpec((tm, tm), lambda i: (0, 0))],
        out_specs=pl.BlockSpec((tm, FOX_HEADS * LANES), lambda i: (i, 0)),
        out_shape=jax.ShapeDtypeStruct((T, FOX_HEADS * LANES), BF16),
        scratch_shapes=[pltpu.VMEM((1, LANES), F32)],
        compiler_params=_cparams(("arbitrary",)),
        name="fox_gate",
    )(ffp, fbv, tri)


def _retention_kernel(dchunk_ref, q_ref, k_ref, v_ref, g_ref, dmask_ref, din_ref, dout_ref,
                      o_ref, state_ref, *, n_chunks):
    @pl.when(pl.program_id(1) == 0)
    def _():
        state_ref[...] = jnp.zeros_like(state_ref)

    C = RET_CHUNK
    lane = lax.broadcasted_iota(I32, (C, LANES), 1)
    for ci in range(n_chunks):
        rows = slice(ci * C, (ci + 1) * C)
        for hd in range(RET_HEADS):
            slab = slice((hd // 2) * LANES, (hd // 2 + 1) * LANES)
            mine = (lane < RET_DK) if hd % 2 == 0 else (lane >= RET_DK)
            q = jnp.where(mine, q_ref[rows, slab], 0)
            k = jnp.where(mine, k_ref[rows, slab], 0)
            vcols = slice(hd * RET_DV, (hd + 1) * RET_DV)
            v = v_ref[rows, vcols]
            state = state_ref[hd]
            scores = lax.dot_general(q, k, (((1,), (1,)), ((), ())),
                                     preferred_element_type=F32) * dmask_ref[hd]
            inner = jnp.dot(scores.astype(BF16), v, preferred_element_type=F32)
            cross = jnp.dot(q, state.astype(BF16), preferred_element_type=F32) * din_ref[hd]
            o = inner + cross
            vd = (v.astype(F32) * dout_ref[hd]).astype(BF16)
            kv = lax.dot_general(k, vd, (((0,), (0,)), ((), ())), preferred_element_type=F32)
            state_ref[hd] = state * dchunk_ref[hd] + kv
            mu = jnp.mean(o, axis=-1, keepdims=True)
            d = o - mu
            var = jnp.mean(d * d, axis=-1, keepdims=True)
            on = d * lax.rsqrt(var + EPS)
            g = g_ref[rows, vcols].astype(F32)
            o_ref[rows, vcols] = (g * jax.nn.sigmoid(g) * on).astype(BF16)


def _retention(rq, rk, rv, rg, B, S):
    H, C = RET_HEADS, RET_CHUNK
    tr = _pick(S, 4 * C)
    n_chunks = tr // C
    log_gamma = jnp.log1p(-jnp.exp2(-5.0 - jnp.arange(H, dtype=F32)))
    idx = jnp.arange(C, dtype=F32)
    rel = idx[:, None] - idx[None, :]
    dmask = jnp.where(rel >= 0, jnp.exp(log_gamma[:, None, None] * jnp.maximum(rel, 0.0)), 0.0)
    decay_in = jnp.exp(log_gamma[:, None] * (idx + 1.0))
    decay_out = jnp.exp(log_gamma[:, None] * (C - 1.0 - idx))
    decay_chunk = jnp.exp(log_gamma * C)
    din = jnp.broadcast_to(decay_in[:, :, None], (H, C, RET_DV))
    dout = jnp.broadcast_to(decay_out[:, :, None], (H, C, RET_DV))
    tok = lambda b, i: (b, i, 0)
    const3 = lambda b, i: (0, 0, 0)
    return pl.pallas_call(
        functools.partial(_retention_kernel, n_chunks=n_chunks),
        grid=(B, S // tr),
        in_specs=[pl.BlockSpec(memory_space=pltpu.SMEM),
                  pl.BlockSpec((None, tr, H * RET_DK), tok),
                  pl.BlockSpec((None, tr, H * RET_DK), tok),
                  pl.BlockSpec((None, tr, H * RET_DV), tok),
                  pl.BlockSpec((None, tr, H * RET_DV), tok),
                  pl.BlockSpec((H, C, C), const3),
                  pl.BlockSpec((H, C, RET_DV), const3),
                  pl.BlockSpec((H, C, RET_DV), const3)],
        out_specs=pl.BlockSpec((None, tr, H * RET_DV), tok),
        out_shape=jax.ShapeDtypeStruct((B, S, H * RET_DV), BF16),
        scratch_shapes=[pltpu.VMEM((H, LANES, RET_DV), F32)],
        compiler_params=_cparams(("arbitrary", "arbitrary")),
        name="retention",
    )(decay_chunk, rq.reshape(B, S, -1), rk.reshape(B, S, -1), rv.reshape(B, S, -1),
      rg.reshape(B, S, -1), dmask, din, dout)


def _flash_kernel(q_ref, k_ref, ke_ref, v_ref, o_ref, m_ref, l_ref, acc_ref, s_ref, *, q_bias_cols):
    qi = pl.program_id(2)
    tq = q_ref.shape[0]
    tk = tq
    m_ref[...] = jnp.full(m_ref.shape, NEG_BIG, F32)
    l_ref[...] = jnp.zeros_like(l_ref)
    acc_ref[...] = jnp.zeros_like(acc_ref)
    q = q_ref[...]
    if q_bias_cols:
        lane = lax.broadcasted_iota(I32, (tq, LANES), 1)
        q = jnp.concatenate([q, jnp.where(lane < q_bias_cols, -1.0, 0.0).astype(BF16)], axis=1)

    def scores(j):
        start = pl.multiple_of(j * tk, tk)
        kj = jnp.concatenate([k_ref[pl.ds(start, tk), :], ke_ref[pl.ds(start, tk), :]], axis=1)
        return lax.dot_general(kj, q, (((1,), (1,)), ((), ())), preferred_element_type=F32)

    def update(s, j):
        m_prev = m_ref[...]
        m_new = jnp.maximum(m_prev, jnp.max(s, axis=0, keepdims=True))
        alpha = jnp.exp2(m_prev - m_new)
        p = jnp.exp2(s - m_new)
        l_ref[...] = alpha * l_ref[...] + jnp.sum(p, axis=0, keepdims=True)
        start = pl.multiple_of(j * tk, tk)
        vj = v_ref[pl.ds(start, tk), :]
        acc_ref[...] = alpha * acc_ref[...] + lax.dot_general(
            vj, p.astype(BF16), (((0,), (0,)), ((), ())), preferred_element_type=F32)
        m_ref[...] = m_new

    def causal(s):
        key = lax.broadcasted_iota(I32, s.shape, 0)
        qry = lax.broadcasted_iota(I32, s.shape, 1)
        return jnp.where(key <= qry, s, NEG_BIG)

    s_ref[0] = scores(0)

    def pair(p, carry):
        j = 2 * p
        s = s_ref[0]
        s_ref[1] = scores(j + 1)
        update(s, j)
        s = s_ref[1]
        s_ref[0] = scores(j + 2)
        update(s, j + 1)
        return carry

    lax.fori_loop(0, qi // 2, pair, 0)

    @pl.when(qi % 2 == 1)
    def _():
        s = s_ref[0]
        s_ref[1] = scores(qi)
        update(s, qi - 1)
        update(causal(s_ref[1]), qi)

    @pl.when(qi % 2 == 0)
    def _():
        update(causal(s_ref[0]), qi)

    o_ref[...] = (acc_ref[...] / l_ref[...]).T.astype(o_ref.dtype)


def _flash(q, k, ke, v, B, S, H, dq, ke_per_head, q_bias_cols, name):
    tq = _pick(S, 512)
    nq = S // tq
    ke_map = (lambda b, h, i: (b, 0, h)) if ke_per_head else (lambda b, h, i: (b, 0, 0))
    return pl.pallas_call(
        functools.partial(_flash_kernel, q_bias_cols=q_bias_cols),
        grid=(B, H, nq),
        in_specs=[pl.BlockSpec((None, tq, dq), lambda b, h, i: (b, i, h)),
                  pl.BlockSpec((None, S, LANES), lambda b, h, i: (b, 0, h)),
                  pl.BlockSpec((None, S, LANES), ke_map),
                  pl.BlockSpec((None, S, LANES), lambda b, h, i: (b, 0, h))],
        out_specs=pl.BlockSpec((None, tq, LANES), lambda b, h, i: (b, i, h)),
        out_shape=jax.ShapeDtypeStruct((B, S, H * LANES), BF16),
        scratch_shapes=[pltpu.VMEM((1, tq), F32), pltpu.VMEM((1, tq), F32),
                        pltpu.VMEM((LANES, tq), F32), pltpu.VMEM((2, tq, tq), F32)],
        compiler_params=_cparams(("arbitrary", "arbitrary", "arbitrary")),
        name=name,
    )(q, k, ke, v)


def _merge_kernel(x_ref, ya_ref, yb_ref, yc_ref, g1_ref, sh1_ref, sc1_ref, gt1_ref,
                  g2_ref, sh2_ref, sc2_ref, gw_ref, gb_ref, bw_ref, ow_ref,
                  rwh_ref, rwl_ref, rb_ref, tri_ref,
                  x1_ref, h2_ref, mi_ref, mf_ref, cnt_ref, h_ref, mg_ref, carry_ref):
    tm, D = x_ref.shape
    cn = MERGE_COL_CHUNK

    @pl.when(pl.program_id(0) == 0)
    def _():
        carry_ref[...] = jnp.zeros_like(carry_ref)

    h_ref[...] = _norm_mod(x_ref[...], g1_ref[...], sh1_ref[...], sc1_ref[...]).astype(BF16)
    for n in range(D // cn):
        cols = slice(n * cn, (n + 1) * cn)
        merged = None
        for i, y_ref in enumerate((ya_ref, yb_ref, yc_ref)):
            gcols = slice(i * D + n * cn, i * D + (n + 1) * cn)
            gate = jax.nn.sigmoid(jnp.dot(h_ref[...], gw_ref[:, gcols], preferred_element_type=F32)
                                  + gb_ref[:, gcols])
            br = jnp.dot(y_ref[...], bw_ref[i, :, cols], preferred_element_type=F32)
            merged = gate * br if merged is None else merged + gate * br
        mg_ref[:, cols] = merged.astype(BF16)
    for n in range(D // cn):
        cols = slice(n * cn, (n + 1) * cn)
        mix = jnp.dot(mg_ref[...], ow_ref[:, cols], preferred_element_type=F32)
        x1_ref[:, cols] = x_ref[:, cols] + gt1_ref[:, cols] * mix
    h2 = _norm_mod(x1_ref[...], g2_ref[...], sh2_ref[...], sc2_ref[...])
    h2_ref[...] = _pack_bf16_pair(h2[:, :D // 2], h2[:, D // 2:])

    hh = h2.astype(BF16)
    hl = (h2 - hh.astype(F32)).astype(BF16)
    nt = lambda a, b: lax.dot_general(a, b, (((1,), (1,)), ((), ())), preferred_element_type=F32)
    lt = nt(rwh_ref[...], hh) + nt(rwh_ref[...], hl) + nt(rwl_ref[...], hh) + rb_ref[...]
    row8 = lax.broadcasted_iota(I32, (EXP_PER_GROUP, tm), 0)
    gl = jnp.where(row8 < N_GROUPS, lt[N_EXPERTS:N_EXPERTS + EXP_PER_GROUP, :], -jnp.inf)
    gmax = jnp.max(gl, axis=0, keepdims=True)
    g_idx = jnp.min(jnp.where(gl == gmax, row8, EXP_PER_GROUP), axis=0, keepdims=True)
    g_w = 1.0 / jnp.sum(jnp.exp(gl - gmax), axis=0, keepdims=True)
    el = lt[(N_GROUPS - 1) * EXP_PER_GROUP:N_EXPERTS, :]
    for g in range(N_GROUPS - 2, -1, -1):
        el = jnp.where(g_idx == g, lt[g * EXP_PER_GROUP:(g + 1) * EXP_PER_GROUP, :], el)
    e1 = jnp.max(el, axis=0, keepdims=True)
    i1 = jnp.min(jnp.where(el == e1, row8, EXP_PER_GROUP), axis=0, keepdims=True)
    el2 = jnp.where(row8 == i1, -jnp.inf, el)
    e2 = jnp.max(el2, axis=0, keepdims=True)
    i2 = jnp.min(jnp.where(el2 == e2, row8, EXP_PER_GROUP), axis=0, keepdims=True)
    r = jnp.exp(e2 - e1)
    w1 = g_w / (1.0 + r)
    w2 = g_w * r / (1.0 + r)
    eid1 = g_idx * EXP_PER_GROUP + i1
    eid2 = g_idx * EXP_PER_GROUP + i2

    rowe = lax.broadcasted_iota(I32, (N_EXPERTS, tm), 0)
    hit1 = rowe == eid1
    hit2 = rowe == eid2
    onehot = jnp.where(hit1 | hit2, 1.0, 0.0)
    before = (jnp.dot(onehot.astype(BF16), tri_ref[...], preferred_element_type=F32)
              + jnp.concatenate([carry_ref[...]] * (tm // LANES), axis=1))
    rank1 = jnp.sum(jnp.where(hit1, before, 0.0), axis=0, keepdims=True)
    rank2 = jnp.sum(jnp.where(hit2, before, 0.0), axis=0, keepdims=True)
    carry_ref[...] = carry_ref[...] + jnp.sum(onehot, axis=1, keepdims=True)
    cnt_ref[...] = carry_ref[...]

    mi_ref[...] = jnp.where(row8 == 0, eid1,
                  jnp.where(row8 == 1, eid2,
                  jnp.where(row8 == 2, rank1.astype(I32),
                  jnp.where(row8 == 3, rank2.astype(I32), 0))))
    mf_ref[...] = jnp.where(row8 == 0, w1, jnp.where(row8 == 1, w2, 0.0))


def _merge(x2, S, ya, yb, yc, g1, sh1, sc1, gt1, g2, sh2, sc2, gw, gb, bw, ow, rwh, rwl, rb):
    T, D = x2.shape
    tm = _pick(S, 512)
    per_b = S // tm
    row = lambda i: (i, 0)
    col = lambda i: (0, i)
    const = lambda i: (0, 0)
    batch = lambda i: (i // per_b, 0, 0)
    idx = jnp.arange(tm)
    tri = (idx[:, None] < idx[None, :]).astype(BF16)
    vecb = pl.BlockSpec((None, 1, D), batch)
    return pl.pallas_call(
        _merge_kernel,
        grid=(T // tm,),
        in_specs=[pl.BlockSpec((tm, D), row),
                  pl.BlockSpec((tm, BRANCH_W), row),
                  pl.BlockSpec((tm, BRANCH_W), row),
                  pl.BlockSpec((tm, BRANCH_W), row),
                  pl.BlockSpec((1, D), const), vecb, vecb, vecb,
                  pl.BlockSpec((1, D), const), vecb, vecb,
                  pl.BlockSpec((D, N_BRANCH * D), const),
                  pl.BlockSpec((1, N_BRANCH * D), const),
                  pl.BlockSpec((N_BRANCH, BRANCH_W, D), lambda i: (0, 0, 0)),
                  pl.BlockSpec((D, D), const),
                  pl.BlockSpec((LANES, D), const),
                  pl.BlockSpec((LANES, D), const),
                  pl.BlockSpec((LANES, 1), const),
                  pl.BlockSpec((tm, tm), const)],
        out_specs=[pl.BlockSpec((tm, D), row), pl.BlockSpec((tm, D // 2), row),
                   pl.BlockSpec((8, tm), col), pl.BlockSpec((8, tm), col),
                   pl.BlockSpec((N_EXPERTS, LANES), const)],
        out_shape=[jax.ShapeDtypeStruct((T, D), F32), jax.ShapeDtypeStruct((T, D // 2), U32),
                   jax.ShapeDtypeStruct((8, T), I32), jax.ShapeDtypeStruct((8, T), F32),
                   jax.ShapeDtypeStruct((N_EXPERTS, LANES), F32)],
        scratch_shapes=[pltpu.VMEM((tm, D), BF16), pltpu.VMEM((tm, D), BF16),
                        pltpu.VMEM((N_EXPERTS, LANES), F32)],
        compiler_params=_cparams(("arbitrary",)),
        name="merge",
    )(x2, ya, yb, yc, g1, sh1, sc1, gt1, g2, sh2, sc2, gw, gb, bw, ow, rwh, rwl, rb, tri)


def _expert_kernel(blk_e_ref, nvalid_ref, xs_ref, w1_ref, w3_ref, w2_ref, ys_ref, w1b_ref, w3b_ref, w2b_ref):
    i = pl.program_id(0)
    valid = i < nvalid_ref[0]
    new_expert = jnp.logical_or(i == 0, blk_e_ref[i] != blk_e_ref[jnp.maximum(i - 1, 0)])

    @pl.when(jnp.logical_and(valid, new_expert))
    def _():
        w1b_ref[...] = w1_ref[...].astype(BF16)
        w3b_ref[...] = w3_ref[...].astype(BF16)
        w2b_ref[...] = w2_ref[...].astype(BF16)

    @pl.when(valid)
    def _():
        half = xs_ref.shape[1]
        x_lo, x_hi = _unpack_bf16_pair(xs_ref[...])
        x_lo = x_lo.astype(BF16)
        x_hi = x_hi.astype(BF16)

        def up(w_ref):
            return (jnp.dot(x_lo, w_ref[:half, :], preferred_element_type=F32)
                    + jnp.dot(x_hi, w_ref[half:, :], preferred_element_type=F32))

        a = up(w1b_ref)
        b = up(w3b_ref)
        hid = (a * jax.nn.sigmoid(a) * b).astype(BF16)
        y = jnp.dot(hid, w2b_ref[...], preferred_element_type=F32)
        ys_ref[...] = _pack_bf16_pair(y[:, :half], y[:, half:])

    @pl.when(jnp.logical_not(valid))
    def _():
        ys_ref[...] = jnp.zeros_like(ys_ref)


def _experts(xs, blk_e, nvalid, w1, w3, w2, layer, tb):
    P, half = xs.shape
    D = 2 * half
    n_blocks = P // tb
    rows = lambda i, be, nv: (jnp.minimum(i, nv[0] - 1), 0)
    grid_spec = pltpu.PrefetchScalarGridSpec(
        num_scalar_prefetch=2,
        grid=(n_blocks,),
        in_specs=[pl.BlockSpec((tb, half), rows),
                  pl.BlockSpec((None, None, D, D_EXPERT), lambda i, be, nv: (layer, be[i], 0, 0)),
                  pl.BlockSpec((None, None, D, D_EXPERT), lambda i, be, nv: (layer, be[i], 0, 0)),
                  pl.BlockSpec((None, None, D_EXPERT, D), lambda i, be, nv: (layer, be[i], 0, 0))],
        out_specs=pl.BlockSpec((tb, half), lambda i, be, nv: (i, 0)),
        scratch_shapes=[pltpu.VMEM((D, D_EXPERT), BF16), pltpu.VMEM((D, D_EXPERT), BF16),
                        pltpu.VMEM((D_EXPERT, D), BF16)],
    )
    return pl.pallas_call(
        _expert_kernel,
        grid_spec=grid_spec,
        out_shape=jax.ShapeDtypeStruct((P, half), U32),
        compiler_params=_cparams(("arbitrary",)),
        name="moe_experts",
    )(blk_e, nvalid, xs, w1, w3, w2)


def _sc_gather(data, idx):
    M = idx.shape[0]
    D = data.shape[1]
    W = SC_GATHER_WINDOW
    assert M % W == 0, (M, W)
    mesh = plsc.VectorSubcoreMesh(core_axis_name="core", subcore_axis_name="subcore")
    n_workers = mesh.num_cores * mesh.num_subcores
    assert M % (W * n_workers) == 0, (M, W, n_workers)

    @functools.partial(pl.kernel, out_type=jax.ShapeDtypeStruct((M, D), data.dtype), mesh=mesh,
                       scratch_types=[pltpu.VMEM((W,), I32), pltpu.VMEM((W, D), data.dtype)])
    def gather_kernel(x_hbm, i_hbm, o_hbm, i_vmem, buf):
        worker = lax.axis_index("core") * mesh.num_subcores + lax.axis_index("subcore")

        @pl.loop(0, M // (W * n_workers))
        def _(t):
            start = (t * n_workers + worker) * W
            pltpu.sync_copy(i_hbm.at[pl.ds(start, W)], i_vmem)
            pltpu.sync_copy(x_hbm.at[i_vmem], buf)
            pltpu.sync_copy(buf, o_hbm.at[pl.ds(start, W)])

    return gather_kernel(data, idx)


def _sc_dispatch(h2, dest, fill_idx, P):
    T, D = h2.shape
    W = SC_GATHER_WINDOW
    n_fill = fill_idx.shape[0]
    mesh = plsc.VectorSubcoreMesh(core_axis_name="core", subcore_axis_name="subcore")
    n_workers = mesh.num_cores * mesh.num_subcores
    assert T % (W * n_workers) == 0 and n_fill % (W * n_workers) == 0, (T, n_fill, W, n_workers)
    zeros = jnp.zeros((W, D), h2.dtype)

    @functools.partial(pl.kernel, out_type=jax.ShapeDtypeStruct((P, D), h2.dtype), mesh=mesh,
                       scratch_types=[pltpu.VMEM((W,), I32), pltpu.VMEM((W, D), h2.dtype)])
    def dispatch_kernel(h_hbm, d_hbm, f_hbm, z_hbm, o_hbm, i_vmem, buf):
        worker = lax.axis_index("core") * mesh.num_subcores + lax.axis_index("subcore")

        @pl.loop(0, T // (W * n_workers))
        def _(t):
            start = (t * n_workers + worker) * W
            pltpu.sync_copy(h_hbm.at[pl.ds(start, W)], buf)
            for c in range(2):
                pltpu.sync_copy(d_hbm.at[pl.ds(c * T + start, W)], i_vmem)
                pltpu.sync_copy(buf, o_hbm.at[i_vmem])

        pltpu.sync_copy(z_hbm, buf)

        @pl.loop(0, n_fill // (W * n_workers))
        def _(t):
            start = (t * n_workers + worker) * W
            pltpu.sync_copy(f_hbm.at[pl.ds(start, W)], i_vmem)
            pltpu.sync_copy(buf, o_hbm.at[i_vmem])

    return dispatch_kernel(h2, dest, fill_idx, zeros)


def _combine_kernel(x_ref, g0_ref, g1_ref, mf_ref, gt_ref, fg_ref, o_ref, *, final):
    mf = mf_ref[...]
    lo0, hi0 = _unpack_bf16_pair(g0_ref[...])
    lo1, hi1 = _unpack_bf16_pair(g1_ref[...])
    w0 = mf[:, 0:1]
    w1 = mf[:, 1:2]
    ffn = jnp.concatenate([lo0 * w0 + lo1 * w1, hi0 * w0 + hi1 * w1], axis=1)
    out = x_ref[...] + gt_ref[...] * ffn
    if final:
        out = _rms(out, fg_ref[...])
    o_ref[...] = out


def _combine(x1, S, mf, gt2, final_g, g, final):
    T, D = x1.shape
    tm = _pick(S, 512)
    per_b = S // tm
    nt = T // tm
    return pl.pallas_call(
        functools.partial(_combine_kernel, final=final),
        grid=(nt,),
        in_specs=[pl.BlockSpec((tm, D), lambda i: (i, 0)),
                  pl.BlockSpec((tm, D // 2), lambda i: (i, 0)),
                  pl.BlockSpec((tm, D // 2), lambda i: (i + nt, 0)),
                  pl.BlockSpec((tm, 2), lambda i: (i, 0)),
                  pl.BlockSpec((None, 1, D), lambda i: (i // per_b, 0, 0)),
                  pl.BlockSpec((1, D), lambda i: (0, 0))],
        out_specs=pl.BlockSpec((tm, D), lambda i: (i, 0)),
        out_shape=jax.ShapeDtypeStruct((T, D), F32),
        compiler_params=_cparams(("arbitrary",)),
        name="moe_combine",
    )(x1, g, g, mf, gt2, final_g)


def _prep_w_in(w):
    offs = [0]
    for s in (256, 256, 512, 512, 512, 512, 512, FOX_HEADS, MLA_Q_RANK, MLA_KV_RANK, MLA_ROPE):
        offs.append(offs[-1] + s)
    rq, rk, rv, rg, fq, fk, fv, ff, mq, mkv, mkr = [w[:, offs[i]:offs[i + 1]] for i in range(11)]
    pad = jnp.zeros((w.shape[0], LANES - MLA_ROPE - FOX_HEADS), w.dtype)
    return jnp.concatenate([rq, rk, rv, rg, fq, fk, fv, mq, mkv, mkr, ff, pad], axis=1).astype(BF16)


def _prep_wq_up(w):
    r = w.reshape(MLA_Q_RANK, MLA_HEADS, MLA_NOPE + MLA_ROPE)
    r = jnp.pad(r, ((0, 0), (0, 0), (0, MLA_DQ - MLA_NOPE - MLA_ROPE)))
    return r.reshape(MLA_Q_RANK, MLA_HEADS * MLA_DQ).astype(BF16)


def _prep_router(w_grp, b_grp, w_exp, b_exp):
    D = w_grp.shape[0]
    pad = LANES - N_EXPERTS - N_GROUPS
    rwt = jnp.concatenate([w_exp, w_grp, jnp.zeros((D, pad), F32)], axis=1).astype(F32).T
    rwh = rwt.astype(BF16)
    rwl = (rwt - rwh.astype(F32)).astype(BF16)
    rb = jnp.concatenate([b_exp, b_grp, jnp.zeros((pad,), F32)]).astype(F32).reshape(LANES, 1)
    return rwh, rwl, rb


def kernel(x, c, positions, ada_w, ada_b, norm1_g, norm2_g, w_in, fox_fb, mla_q_norm_g, mla_wq_up, mla_kv_norm_g, mla_wkv_up, gate_w, gate_b, branch_w, out_w, router_grp_w, router_grp_b, router_exp_w, router_exp_b, exp_w1, exp_w3, exp_w2, final_g):
    B, S, D = x.shape
    L = ada_w.shape[0]
    T = B * S
    A = 2 * T
    tb = _pick(A, 512)
    n_blocks = A // tb + N_EXPERTS
    P = n_blocks * tb

    mod = _adaln(c, ada_w, ada_b)
    cos_t, sin_t = _rope_tables(positions)
    x2 = x.reshape(T, D)
    final_g2 = final_g.reshape(1, D)

    for l in range(L):
        sh1, sc1, gt1, sh2, sc2, gt2 = [mod[l, :, i * D:(i + 1) * D].reshape(B, 1, D) for i in range(6)]
        g1 = norm1_g[l].reshape(1, D)
        g2 = norm2_g[l].reshape(1, D)
        (rq, rk, rv, rg, fq, fk, fv, mq, mk, kpe, mv, ffp) = _proj(
            x2, S, g1, sh1, sc1, _prep_w_in(w_in[l]), cos_t, sin_t,
            mla_q_norm_g[l].reshape(1, -1), _prep_wq_up(mla_wq_up[l]),
            mla_kv_norm_g[l].reshape(1, -1), mla_wkv_up[l].astype(BF16))

        ya = _retention(rq, rk, rv, rg, B, S).reshape(T, -1)
        kb = _fox_gate(ffp, fox_fb[l], S)
        r3 = lambda a: a.reshape(B, S, -1)
        yb = _flash(r3(fq), r3(fk), r3(kb), r3(fv), B, S, FOX_HEADS, FOX_DH, True, 3, "flash_fox").reshape(T, -1)
        yc = _flash(r3(mq), r3(mk), r3(kpe), r3(mv), B, S, MLA_HEADS, MLA_DQ, False, 0, "flash_mla").reshape(T, -1)

        rwh, rwl, rb = _prep_router(router_grp_w[l], router_grp_b[l], router_exp_w[l], router_exp_b[l])
        x1, h2, mi, mf, cnt = _merge(
            x2, S, ya, yb, yc, g1, sh1, sc1, gt1, g2, sh2, sc2,
            gate_w[l].astype(BF16), gate_b[l].reshape(1, -1), branch_w[l].astype(BF16),
            out_w[l].astype(BF16), rwh, rwl, rb)

        counts = cnt[:, 0].astype(I32)
        pcounts = (counts + tb - 1) // tb * tb
        pends = jnp.cumsum(pcounts)
        pstarts = pends - pcounts
        sel = mi[0:2, :, None] == jnp.arange(N_EXPERTS, dtype=I32)
        dest = (jnp.sum(jnp.where(sel, pstarts, 0), axis=-1) + mi[2:4]).reshape(A)
        blk_pos = jnp.arange(n_blocks, dtype=I32) * tb
        blk_e = jnp.minimum(jnp.sum((pends[None, :] <= blk_pos[:, None]).astype(I32), axis=1), N_EXPERTS - 1)
        nvalid = (pends[-1:] // tb).astype(I32)
        fr = jnp.arange(tb, dtype=I32)[None, :]
        is_pad = (fr < (pcounts - counts)[:, None]).reshape(-1)
        pad_slot = ((pstarts + counts)[:, None] + fr).reshape(-1)
        tail_rank = jnp.cumsum(jnp.logical_not(is_pad).astype(I32)) - 1
        fill_idx = jnp.where(is_pad, pad_slot, pends[-1] + tail_rank)

        xs = _sc_dispatch(h2, dest, fill_idx, P)
        ys = _experts(xs, blk_e, nvalid, exp_w1, exp_w3, exp_w2, l, tb)
        g = _sc_gather(ys, dest)
        x2 = _combine(x1, S, mf[0:2].T, gt2, final_g2, g, final=(l == L - 1))

    return x2.reshape(B, S, D)
```

```python
import functools
import math

import jax
import jax.numpy as jnp
from jax import lax
from jax.experimental import pallas as pl
from jax.experimental.pallas import tpu as pltpu
from jax.experimental.pallas import tpu_sc as plsc

F32 = jnp.float32
BF16 = jnp.bfloat16
I32 = jnp.int32
U32 = jnp.uint32
HIGHEST = lax.Precision.HIGHEST

EPS = 1e-6
ROPE_THETA = 10000.0
RET_HEADS = 4
RET_DK = 64
RET_DV = 128
RET_CHUNK = 128
FOX_HEADS = 4
FOX_DH = 128
MLA_HEADS = 4
MLA_Q_RANK = 256
MLA_KV_RANK = 128
MLA_NOPE = 128
MLA_ROPE = 64
MLA_V = 128
MLA_DQ = 256
N_BRANCH = 3
BRANCH_W = 512
N_GROUPS = 4
EXP_PER_GROUP = 8
N_EXPERTS = N_GROUPS * EXP_PER_GROUP
D_EXPERT = 512

LANES = 128
V7X_VMEM_LIMIT = 56 * 1024 * 1024

C_RQ, C_RK, C_RV, C_RG = 0, 256, 512, 1024
C_FQ, C_FK, C_FV = 1536, 2048, 2560
C_MQ, C_MKV, C_TAIL = 3072, 3328, 3456
D_IN_PAD = 3584
FF_LANE = MLA_ROPE

NEG_BIG = -1e30
SC_GATHER_WINDOW = 128
N_STREAMS = 2
MERGE_COL_CHUNK = 256
LOG2E = math.log2(math.e)


def _cparams(sem):
    return pltpu.CompilerParams(dimension_semantics=sem, vmem_limit_bytes=V7X_VMEM_LIMIT)


def _pick(n, pref):
    t = min(n, pref)
    assert n % t == 0, (n, t)
    return t


def _adaln_kernel(c_ref, w_ref, b_ref, o_ref):
    c = c_ref[...]
    ca = c * jax.nn.sigmoid(c)
    o_ref[...] = jnp.dot(ca, w_ref[...], preferred_element_type=F32, precision=HIGHEST) + b_ref[...]


def _adaln(c, ada_w, ada_b):
    L, D, N = ada_w.shape
    B = c.shape[0]
    tn = _pick(N, 1536)
    return pl.pallas_call(
        _adaln_kernel,
        grid=(L, N // tn),
        in_specs=[pl.BlockSpec((B, D), lambda l, j: (0, 0)),
                  pl.BlockSpec((None, D, tn), lambda l, j: (l, 0, j)),
                  pl.BlockSpec((None, 1, tn), lambda l, j: (l, 0, j))],
        out_specs=pl.BlockSpec((None, B, tn), lambda l, j: (l, 0, j)),
        out_shape=jax.ShapeDtypeStruct((L, B, N), F32),
        compiler_params=_cparams(("arbitrary", "arbitrary")),
        name="adaln",
    )(c, ada_w, ada_b.reshape(L, 1, N))


def _rope_table_kernel(pos_ref, inv_ref, sign_ref, cos_ref, sin_ref):
    ang = pos_ref[...].astype(F32) * inv_ref[...]
    cos_ref[...] = jnp.cos(ang)
    sin_ref[...] = jnp.sin(ang) * sign_ref[...]


def _rope_tables(positions):
    T = positions.size
    tm = _pick(T, 2048)
    half = MLA_ROPE // 2
    inv = ROPE_THETA ** (-jnp.arange(0, MLA_ROPE, 2, dtype=F32) / MLA_ROPE)
    inv_t = jnp.tile(inv, LANES // half).reshape(1, LANES)
    sign = jnp.where((jnp.arange(LANES) % MLA_ROPE) < half, -1.0, 1.0).astype(F32).reshape(1, LANES)
    return pl.pallas_call(
        _rope_table_kernel,
        grid=(T // tm,),
        in_specs=[pl.BlockSpec((tm, 1), lambda i: (i, 0)),
                  pl.BlockSpec((1, LANES), lambda i: (0, 0)),
                  pl.BlockSpec((1, LANES), lambda i: (0, 0))],
        out_specs=[pl.BlockSpec((tm, LANES), lambda i: (i, 0))] * 2,
        out_shape=[jax.ShapeDtypeStruct((T, LANES), F32)] * 2,
        compiler_params=_cparams(("arbitrary",)),
        name="rope_tables",
    )(positions.reshape(T, 1), inv_t, sign)


def _rope_slab(x, cos_t, sin_t, lane):
    nxt = pltpu.roll(x, LANES - 32, axis=1)
    prv = pltpu.roll(x, 32, axis=1)
    swapped = jnp.where((lane & 32) == 0, nxt, prv)
    return x * cos_t + swapped * sin_t


def _pack_bf16_pair(lo, hi):
    lo_bits = lax.shift_right_logical(lax.bitcast_convert_type(lo.astype(BF16).astype(F32), U32), jnp.uint32(16))
    hi_bits = lax.bitcast_convert_type(hi.astype(BF16).astype(F32), U32) & jnp.uint32(0xFFFF0000)
    return hi_bits | lo_bits


def _unpack_bf16_pair(w):
    lo = lax.bitcast_convert_type(lax.shift_left(w, jnp.uint32(16)), F32)
    hi = lax.bitcast_convert_type(w & jnp.uint32(0xFFFF0000), F32)
    return lo, hi


def _norm_mod(x, g, shift, scale):
    y = x * lax.rsqrt(jnp.mean(x * x, axis=-1, keepdims=True) + EPS)
    return (y * g) * (1.0 + scale) + shift


def _rms(x, g):
    return x * lax.rsqrt(jnp.mean(x * x, axis=-1, keepdims=True) + EPS) * g


def _proj_kernel(x_ref, g_ref, sh_ref, sc_ref, w_ref, cos_ref, sin_ref,
                 gq_ref, wq_ref, gkv_ref, wkv_ref,
                 rq_ref, rk_ref, rv_ref, rg_ref, fq_ref, fk_ref, fv_ref,
                 mq_ref, mk_ref, kpe_ref, mv_ref, ff_ref):
    h = _norm_mod(x_ref[...], g_ref[...], sh_ref[...], sc_ref[...]).astype(BF16)
    cos_t = cos_ref[...]
    sin_t = sin_ref[...]
    lane = lax.broadcasted_iota(I32, cos_t.shape, 1)

    def proj(c0, width):
        return jnp.dot(h, w_ref[:, c0:c0 + width], preferred_element_type=F32)

    rq = proj(C_RQ, 256)
    rk = proj(C_RK, 256)
    for s in range(2):
        sl = slice(s * LANES, (s + 1) * LANES)
        rq_ref[:, sl] = _rope_slab(rq[:, sl], cos_t, sin_t, lane).astype(BF16)
        rk_ref[:, sl] = (_rope_slab(rk[:, sl], cos_t, sin_t, lane) * (RET_DK ** -0.5)).astype(BF16)
    rv_ref[...] = proj(C_RV, 512).astype(BF16)
    rg_ref[...] = proj(C_RG, 512).astype(BF16)
    fq_ref[...] = (proj(C_FQ, 512) * (FOX_DH ** -0.5 * LOG2E)).astype(BF16)
    fk_ref[...] = proj(C_FK, 512).astype(BF16)
    fv_ref[...] = proj(C_FV, 512).astype(BF16)

    tail = proj(C_TAIL, LANES)
    ff_ref[...] = tail
    kpe_ref[...] = jnp.where(lane < MLA_ROPE, _rope_slab(tail, cos_t, sin_t, lane), 0.0).astype(BF16)

    qn = _rms(proj(C_MQ, MLA_Q_RANK), gq_ref[...]).astype(BF16)
    qh = jnp.dot(qn, wq_ref[...], preferred_element_type=F32)
    q_scale = (MLA_NOPE + MLA_ROPE) ** -0.5 * LOG2E
    for hd in range(MLA_HEADS):
        c0 = hd * MLA_DQ
        mq_ref[:, c0:c0 + LANES] = (qh[:, c0:c0 + LANES] * q_scale).astype(BF16)
        pe = _rope_slab(qh[:, c0 + LANES:c0 + 2 * LANES], cos_t, sin_t, lane)
        mq_ref[:, c0 + LANES:c0 + 2 * LANES] = jnp.where(lane < MLA_ROPE, pe * q_scale, 0.0).astype(BF16)

    kvn = _rms(proj(C_MKV, MLA_KV_RANK), gkv_ref[...]).astype(BF16)
    kvh = jnp.dot(kvn, wkv_ref[...], preferred_element_type=F32)
    for hd in range(MLA_HEADS):
        c0 = hd * (MLA_NOPE + MLA_V)
        mk_ref[:, hd * MLA_NOPE:(hd + 1) * MLA_NOPE] = kvh[:, c0:c0 + MLA_NOPE].astype(BF16)
        mv_ref[:, hd * MLA_V:(hd + 1) * MLA_V] = kvh[:, c0 + MLA_NOPE:c0 + MLA_NOPE + MLA_V].astype(BF16)


def _proj(x2, x_tile0, S, g, sh, sc, w_all, cos_t, sin_t, gq, wq, gkv, wkv):
    D = x2.shape[1]
    T = cos_t.shape[0]
    tm = _pick(S, 512)
    per_b = S // tm
    row = lambda i: (i, 0)
    xrow = lambda i: (i + x_tile0, 0)
    const = lambda i: (0, 0)
    batch = lambda i: (i // per_b, 0, 0)
    widths = [256, 256, 512, 512, 512, 512, 512, MLA_HEADS * MLA_DQ, MLA_HEADS * MLA_NOPE, LANES,
              MLA_HEADS * MLA_V]
    out_shape = [jax.ShapeDtypeStruct((T, w), BF16) for w in widths]
    out_shape.append(jax.ShapeDtypeStruct((T, LANES), F32))
    out_specs = [pl.BlockSpec((tm, w), row) for w in widths] + [pl.BlockSpec((tm, LANES), row)]
    return pl.pallas_call(
        _proj_kernel,
        grid=(T // tm,),
        in_specs=[pl.BlockSpec((tm, D), xrow),
                  pl.BlockSpec((1, D), const),
                  pl.BlockSpec((None, 1, D), batch),
                  pl.BlockSpec((None, 1, D), batch),
                  pl.BlockSpec((D, D_IN_PAD), const),
                  pl.BlockSpec((tm, LANES), row),
                  pl.BlockSpec((tm, LANES), row),
                  pl.BlockSpec((1, MLA_Q_RANK), const),
                  pl.BlockSpec((MLA_Q_RANK, MLA_HEADS * MLA_DQ), const),
                  pl.BlockSpec((1, MLA_KV_RANK), const),
                  pl.BlockSpec((MLA_KV_RANK, MLA_HEADS * (MLA_NOPE + MLA_V)), const)],
        out_specs=out_specs,
        out_shape=out_shape,
        compiler_params=_cparams(("arbitrary",)),
        name="proj",
    )(x2, g, sh, sc, w_all, cos_t, sin_t, gq, wq, gkv, wkv)


def _split3(x):
    a = x.astype(BF16)
    r = x - a.astype(F32)
    b = r.astype(BF16)
    c = (r - b.astype(F32)).astype(BF16)
    return a, b, c


def _fox_gate_kernel(ff_ref, fb_ref, tri_ref, kb_ref, carry_ref, *, tiles_per_seq):
    @pl.when(pl.program_id(0) % tiles_per_seq == 0)
    def _():
        carry_ref[...] = jnp.zeros_like(carry_ref)

    tm = ff_ref.shape[0]
    lane = lax.broadcasted_iota(I32, (tm, LANES), 1)
    z = ff_ref[...] + fb_ref[...]
    ls = -(jnp.maximum(-z, 0.0) + jnp.log1p(jnp.exp(-jnp.abs(z))))
    ls = jnp.where((lane >= FF_LANE) & (lane < FF_LANE + FOX_HEADS), ls, 0.0)
    tri = tri_ref[...]
    f = carry_ref[...]
    for part in _split3(ls):
        f = f + jnp.dot(tri, part, preferred_element_type=F32)
    carry_ref[...] = f[tm - 1:tm, :]
    f2 = f * LOG2E
    for hd in range(FOX_HEADS):
        src = FF_LANE + hd
        g = jnp.where(lane == 0, pltpu.roll(f2, (LANES - src) % LANES, axis=1),
            jnp.where(lane == 1, pltpu.roll(f2, (LANES + 1 - src) % LANES, axis=1),
            jnp.where(lane == 2, pltpu.roll(f2, (LANES + 2 - src) % LANES, axis=1), 0.0)))
        hi, mid, lo = _split3(g)
        kb_ref[:, hd * LANES:(hd + 1) * LANES] = jnp.where(lane == 0, hi, jnp.where(lane == 1, mid, lo))


def _fox_gate(ffp, fb, S):
    T = ffp.shape[0]
    tm = _pick(S, 512)
    idx = jnp.arange(tm)
    tri = (idx[None, :] <= idx[:, None]).astype(BF16)
    fbv = jnp.zeros((1, LANES), F32).at[0, FF_LANE:FF_LANE + FOX_HEADS].set(fb)
    return pl.pallas_call(
        functools.partial(_fox_gate_kernel, tiles_per_seq=S // tm),
        grid=(T // tm,),
        in_specs=[pl.BlockSpec((tm, LANES), lambda i: (i, 0)),
                  pl.BlockSpec((1, LANES), lambda i: (0, 0)),
                  pl.BlockSpec((tm, tm), lambda i: (0, 0))],
        out_specs=pl.BlockSpec((tm, FOX_HEADS * LANES), lambda i: (i, 0)),
        out_shape=jax.ShapeDtypeStruct((T, FOX_HEADS * LANES), BF16),
        scratch_shapes=[pltpu.VMEM((1, LANES), F32)],
        compiler_params=_cparams(("arbitrary",)),
        name="fox_gate",
    )(ffp, fbv, tri)


def _retention_kernel(dchunk_ref, q_ref, k_ref, v_ref, g_ref, dmask_ref, din_ref, dout_ref,
                      o_ref, state_ref, *, n_chunks):
    @pl.when(pl.program_id(1) == 0)
    def _():
        state_ref[...] = jnp.zeros_like(state_ref)

    C = RET_CHUNK
    lane = lax.broadcasted_iota(I32, (C, LANES), 1)
    for ci in range(n_chunks):
        rows = slice(ci * C, (ci + 1) * C)
        for hd in range(RET_HEADS):
            slab = slice((hd // 2) * LANES, (hd // 2 + 1) * LANES)
            mine = (lane < RET_DK) if hd % 2 == 0 else (lane >= RET_DK)
            q = jnp.where(mine, q_ref[rows, slab], 0)
            k = jnp.where(mine, k_ref[rows, slab], 0)
            vcols = slice(hd * RET_DV, (hd + 1) * RET_DV)
            v = v_ref[rows, vcols]
            state = state_ref[hd]
            scores = lax.dot_general(q, k, (((1,), (1,)), ((), ())),
                                     preferred_element_type=F32) * dmask_ref[hd]
            inner = jnp.dot(scores.astype(BF16), v, preferred_element_type=F32)
            cross = jnp.dot(q, state.astype(BF16), preferred_element_type=F32) * din_ref[hd]
            o = inner + cross
            vd = (v.astype(F32) * dout_ref[hd]).astype(BF16)
            kv = lax.dot_general(k, vd, (((0,), (0,)), ((), ())), preferred_element_type=F32)
            state_ref[hd] = state * dchunk_ref[hd] + kv
            mu = jnp.mean(o, axis=-1, keepdims=True)
            d = o - mu
            var = jnp.mean(d * d, axis=-1, keepdims=True)
            on = d * lax.rsqrt(var + EPS)
            g = g_ref[rows, vcols].astype(F32)
            o_ref[rows, vcols] = (g * jax.nn.sigmoid(g) * on).astype(BF16)


def _retention(rq, rk, rv, rg, B, S):
    H, C = RET_HEADS, RET_CHUNK
    tr = _pick(S, 4 * C)
    n_chunks = tr // C
    log_gamma = jnp.log1p(-jnp.exp2(-5.0 - jnp.arange(H, dtype=F32)))
    idx = jnp.arange(C, dtype=F32)
    rel = idx[:, None] - idx[None, :]
    dmask = jnp.where(rel >= 0, jnp.exp(log_gamma[:, None, None] * jnp.maximum(rel, 0.0)), 0.0)
    decay_in = jnp.exp(log_gamma[:, None] * (idx + 1.0))
    decay_out = jnp.exp(log_gamma[:, None] * (C - 1.0 - idx))
    decay_chunk = jnp.exp(log_gamma * C)
    din = jnp.broadcast_to(decay_in[:, :, None], (H, C, RET_DV))
    dout = jnp.broadcast_to(decay_out[:, :, None], (H, C, RET_DV))
    tok = lambda b, i: (b, i, 0)
    const3 = lambda b, i: (0, 0, 0)
    return pl.pallas_call(
        functools.partial(_retention_kernel, n_chunks=n_chunks),
        grid=(B, S // tr),
        in_specs=[pl.BlockSpec(memory_space=pltpu.SMEM),
                  pl.BlockSpec((None, tr, H * RET_DK), tok),
                  pl.BlockSpec((None, tr, H * RET_DK), tok),
                  pl.BlockSpec((None, tr, H * RET_DV), tok),
                  pl.BlockSpec((None, tr, H * RET_DV), tok),
                  pl.BlockSpec((H, C, C), const3),
                  pl.BlockSpec((H, C, RET_DV), const3),
                  pl.BlockSpec((H, C, RET_DV), const3)],
        out_specs=pl.BlockSpec((None, tr, H * RET_DV), tok),
        out_shape=jax.ShapeDtypeStruct((B, S, H * RET_DV), BF16),
        scratch_shapes=[pltpu.VMEM((H, LANES, RET_DV), F32)],
        compiler_params=_cparams(("arbitrary", "arbitrary")),
        name="retention",
    )(decay_chunk, rq.reshape(B, S, -1), rk.reshape(B, S, -1), rv.reshape(B, S, -1),
      rg.reshape(B, S, -1), dmask, din, dout)


def _flash_kernel(q_ref, k_ref, ke_ref, v_ref, o_ref, m_ref, l_ref, acc_ref, s_ref, *, q_bias_cols):
    qi = pl.program_id(2)
    tq = q_ref.shape[0]
    tk = tq
    m_ref[...] = jnp.full(m_ref.shape, NEG_BIG, F32)
    l_ref[...] = jnp.zeros_like(l_ref)
    acc_ref[...] = jnp.zeros_like(acc_ref)
    q = q_ref[...]
    if q_bias_cols:
        lane = lax.broadcasted_iota(I32, (tq, LANES), 1)
        q = jnp.concatenate([q, jnp.where(lane < q_bias_cols, -1.0, 0.0).astype(BF16)], axis=1)

    def scores(j):
        start = pl.multiple_of(j * tk, tk)
        kj = jnp.concatenate([k_ref[pl.ds(start, tk), :], ke_ref[pl.ds(start, tk), :]], axis=1)
        return lax.dot_general(kj, q, (((1,), (1,)), ((), ())), preferred_element_type=F32)

    def update(s, j):
        m_prev = m_ref[...]
        m_new = jnp.maximum(m_prev, jnp.max(s, axis=0, keepdims=True))
        alpha = jnp.exp2(m_prev - m_new)
        p = jnp.exp2(s - m_new)
        l_ref[...] = alpha * l_ref[...] + jnp.sum(p, axis=0, keepdims=True)
        start = pl.multiple_of(j * tk, tk)
        vj = v_ref[pl.ds(start, tk), :]
        acc_ref[...] = alpha * acc_ref[...] + lax.dot_general(
            vj, p.astype(BF16), (((0,), (0,)), ((), ())), preferred_element_type=F32)
        m_ref[...] = m_new

    def causal(s):
        key = lax.broadcasted_iota(I32, s.shape, 0)
        qry = lax.broadcasted_iota(I32, s.shape, 1)
        return jnp.where(key <= qry, s, NEG_BIG)

    s_ref[0] = scores(0)

    def pair(p, carry):
        j = 2 * p
        s = s_ref[0]
        s_ref[1] = scores(j + 1)
        update(s, j)
        s = s_ref[1]
        s_ref[0] = scores(j + 2)
        update(s, j + 1)
        return carry

    lax.fori_loop(0, qi // 2, pair, 0)

    @pl.when(qi % 2 == 1)
    def _():
        s = s_ref[0]
        s_ref[1] = scores(qi)
        update(s, qi - 1)
        update(causal(s_ref[1]), qi)

    @pl.when(qi % 2 == 0)
    def _():
        update(causal(s_ref[0]), qi)

    o_ref[...] = (acc_ref[...] / l_ref[...]).T.astype(o_ref.dtype)


def _flash(q, k, ke, v, B, S, H, dq, ke_per_head, q_bias_cols, name):
    tq = _pick(S, 512)
    nq = S // tq
    ke_map = (lambda b, h, i: (b, 0, h)) if ke_per_head else (lambda b, h, i: (b, 0, 0))
    return pl.pallas_call(
        functools.partial(_flash_kernel, q_bias_cols=q_bias_cols),
        grid=(B, H, nq),
        in_specs=[pl.BlockSpec((None, tq, dq), lambda b, h, i: (b, i, h)),
                  pl.BlockSpec((None, S, LANES), lambda b, h, i: (b, 0, h)),
                  pl.BlockSpec((None, S, LANES), ke_map),
                  pl.BlockSpec((None, S, LANES), lambda b, h, i: (b, 0, h))],
        out_specs=pl.BlockSpec((None, tq, LANES), lambda b, h, i: (b, i, h)),
        out_shape=jax.ShapeDtypeStruct((B, S, H * LANES), BF16),
        scratch_shapes=[pltpu.VMEM((1, tq), F32), pltpu.VMEM((1, tq), F32),
                        pltpu.VMEM((LANES, tq), F32), pltpu.VMEM((2, tq, tq), F32)],
        compiler_params=_cparams(("arbitrary", "arbitrary", "arbitrary")),
        name=name,
    )(q, k, ke, v)


def _merge_kernel(x_ref, ya_ref, yb_ref, yc_ref, g1_ref, sh1_ref, sc1_ref, gt1_ref,
                  g2_ref, sh2_ref, sc2_ref, gw_ref, gb_ref, bw_ref, ow_ref,
                  rwh_ref, rwl_ref, rb_ref, tri_ref,
                  x1_ref, h2_ref, mi_ref, mf_ref, cnt_ref, h_ref, mg_ref, carry_ref):
    tm, D = x_ref.shape
    cn = MERGE_COL_CHUNK

    @pl.when(pl.program_id(0) == 0)
    def _():
        carry_ref[...] = jnp.zeros_like(carry_ref)

    h_ref[...] = _norm_mod(x_ref[...], g1_ref[...], sh1_ref[...], sc1_ref[...]).astype(BF16)
    for n in range(D // cn):
        cols = slice(n * cn, (n + 1) * cn)
        merged = None
        for i, y_ref in enumerate((ya_ref, yb_ref, yc_ref)):
            gcols = slice(i * D + n * cn, i * D + (n + 1) * cn)
            gate = jax.nn.sigmoid(jnp.dot(h_ref[...], gw_ref[:, gcols], preferred_element_type=F32)
                                  + gb_ref[:, gcols])
            br = jnp.dot(y_ref[...], bw_ref[i, :, cols], preferred_element_type=F32)
            merged = gate * br if merged is None else merged + gate * br
        mg_ref[:, cols] = merged.astype(BF16)
    for n in range(D // cn):
        cols = slice(n * cn, (n + 1) * cn)
        mix = jnp.dot(mg_ref[...], ow_ref[:, cols], preferred_element_type=F32)
        x1_ref[:, cols] = x_ref[:, cols] + gt1_ref[:, cols] * mix
    h2 = _norm_mod(x1_ref[...], g2_ref[...], sh2_ref[...], sc2_ref[...])
    h2_ref[...] = _pack_bf16_pair(h2[:, :D // 2], h2[:, D // 2:])

    hh = h2.astype(BF16)
    hl = (h2 - hh.astype(F32)).astype(BF16)
    nt = lambda a, b: lax.dot_general(a, b, (((1,), (1,)), ((), ())), preferred_element_type=F32)
    lt = nt(rwh_ref[...], hh) + nt(rwh_ref[...], hl) + nt(rwl_ref[...], hh) + rb_ref[...]
    row8 = lax.broadcasted_iota(I32, (EXP_PER_GROUP, tm), 0)
    gl = jnp.where(row8 < N_GROUPS, lt[N_EXPERTS:N_EXPERTS + EXP_PER_GROUP, :], -jnp.inf)
    gmax = jnp.max(gl, axis=0, keepdims=True)
    g_idx = jnp.min(jnp.where(gl == gmax, row8, EXP_PER_GROUP), axis=0, keepdims=True)
    g_w = 1.0 / jnp.sum(jnp.exp(gl - gmax), axis=0, keepdims=True)
    el = lt[(N_GROUPS - 1) * EXP_PER_GROUP:N_EXPERTS, :]
    for g in range(N_GROUPS - 2, -1, -1):
        el = jnp.where(g_idx == g, lt[g * EXP_PER_GROUP:(g + 1) * EXP_PER_GROUP, :], el)
    e1 = jnp.max(el, axis=0, keepdims=True)
    i1 = jnp.min(jnp.where(el == e1, row8, EXP_PER_GROUP), axis=0, keepdims=True)
    el2 = jnp.where(row8 == i1, -jnp.inf, el)
    e2 = jnp.max(el2, axis=0, keepdims=True)
    i2 = jnp.min(jnp.where(el2 == e2, row8, EXP_PER_GROUP), axis=0, keepdims=True)
    r = jnp.exp(e2 - e1)
    w1 = g_w / (1.0 + r)
    w2 = g_w * r / (1.0 + r)
    eid1 = g_idx * EXP_PER_GROUP + i1
    eid2 = g_idx * EXP_PER_GROUP + i2

    rowe = lax.broadcasted_iota(I32, (N_EXPERTS, tm), 0)
    hit1 = rowe == eid1
    hit2 = rowe == eid2
    onehot = jnp.where(hit1 | hit2, 1.0, 0.0)
    before = (jnp.dot(onehot.astype(BF16), tri_ref[...], preferred_element_type=F32)
              + jnp.concatenate([carry_ref[...]] * (tm // LANES), axis=1))
    rank1 = jnp.sum(jnp.where(hit1, before, 0.0), axis=0, keepdims=True)
    rank2 = jnp.sum(jnp.where(hit2, before, 0.0), axis=0, keepdims=True)
    carry_ref[...] = carry_ref[...] + jnp.sum(onehot, axis=1, keepdims=True)
    cnt_ref[...] = carry_ref[...]

    mi_ref[...] = jnp.where(row8 == 0, eid1,
                  jnp.where(row8 == 1, eid2,
                  jnp.where(row8 == 2, rank1.astype(I32),
                  jnp.where(row8 == 3, rank2.astype(I32), 0))))
    mf_ref[...] = jnp.where(row8 == 0, w1, jnp.where(row8 == 1, w2, 0.0))


def _merge(x2, x_tile0, S, ya, yb, yc, g1, sh1, sc1, gt1, g2, sh2, sc2, gw, gb, bw, ow, rwh, rwl, rb):
    D = x2.shape[1]
    T = ya.shape[0]
    tm = _pick(S, 512)
    per_b = S // tm
    row = lambda i: (i, 0)
    xrow = lambda i: (i + x_tile0, 0)
    col = lambda i: (0, i)
    const = lambda i: (0, 0)
    batch = lambda i: (i // per_b, 0, 0)
    idx = jnp.arange(tm)
    tri = (idx[:, None] < idx[None, :]).astype(BF16)
    vecb = pl.BlockSpec((None, 1, D), batch)
    return pl.pallas_call(
        _merge_kernel,
        grid=(T // tm,),
        in_specs=[pl.BlockSpec((tm, D), xrow),
                  pl.BlockSpec((tm, BRANCH_W), row),
                  pl.BlockSpec((tm, BRANCH_W), row),
                  pl.BlockSpec((tm, BRANCH_W), row),
                  pl.BlockSpec((1, D), const), vecb, vecb, vecb,
                  pl.BlockSpec((1, D), const), vecb, vecb,
                  pl.BlockSpec((D, N_BRANCH * D), const),
                  pl.BlockSpec((1, N_BRANCH * D), const),
                  pl.BlockSpec((N_BRANCH, BRANCH_W, D), lambda i: (0, 0, 0)),
                  pl.BlockSpec((D, D), const),
                  pl.BlockSpec((LANES, D), const),
                  pl.BlockSpec((LANES, D), const),
                  pl.BlockSpec((LANES, 1), const),
                  pl.BlockSpec((tm, tm), const)],
        out_specs=[pl.BlockSpec((tm, D), row), pl.BlockSpec((tm, D // 2), row),
                   pl.BlockSpec((8, tm), col), pl.BlockSpec((8, tm), col),
                   pl.BlockSpec((N_EXPERTS, LANES), const)],
        out_shape=[jax.ShapeDtypeStruct((T, D), F32), jax.ShapeDtypeStruct((T, D // 2), U32),
                   jax.ShapeDtypeStruct((8, T), I32), jax.ShapeDtypeStruct((8, T), F32),
                   jax.ShapeDtypeStruct((N_EXPERTS, LANES), F32)],
        scratch_shapes=[pltpu.VMEM((tm, D), BF16), pltpu.VMEM((tm, D), BF16),
                        pltpu.VMEM((N_EXPERTS, LANES), F32)],
        compiler_params=_cparams(("arbitrary",)),
        name="merge",
    )(x2, ya, yb, yc, g1, sh1, sc1, gt1, g2, sh2, sc2, gw, gb, bw, ow, rwh, rwl, rb, tri)


def _expert_kernel(blk_e_ref, nvalid_ref, xs_ref, w1_ref, w3_ref, w2_ref, ys_ref, w1b_ref, w3b_ref, w2b_ref):
    i = pl.program_id(0)
    valid = i < nvalid_ref[0]
    new_expert = jnp.logical_or(i == 0, blk_e_ref[i] != blk_e_ref[jnp.maximum(i - 1, 0)])

    @pl.when(jnp.logical_and(valid, new_expert))
    def _():
        w1b_ref[...] = w1_ref[...].astype(BF16)
        w3b_ref[...] = w3_ref[...].astype(BF16)
        w2b_ref[...] = w2_ref[...].astype(BF16)

    @pl.when(valid)
    def _():
        half = xs_ref.shape[1]
        x_lo, x_hi = _unpack_bf16_pair(xs_ref[...])
        x_lo = x_lo.astype(BF16)
        x_hi = x_hi.astype(BF16)

        def up(w_ref):
            return (jnp.dot(x_lo, w_ref[:half, :], preferred_element_type=F32)
                    + jnp.dot(x_hi, w_ref[half:, :], preferred_element_type=F32))

        a = up(w1b_ref)
        b = up(w3b_ref)
        hid = (a * jax.nn.sigmoid(a) * b).astype(BF16)
        y = jnp.dot(hid, w2b_ref[...], preferred_element_type=F32)
        ys_ref[...] = _pack_bf16_pair(y[:, :half], y[:, half:])

    @pl.when(jnp.logical_not(valid))
    def _():
        ys_ref[...] = jnp.zeros_like(ys_ref)


def _experts(xs, blk_e, nvalid, w1, w3, w2, layer, tb):
    P, half = xs.shape
    D = 2 * half
    n_blocks = P // tb
    rows = lambda i, be, nv: (jnp.minimum(i, nv[0] - 1), 0)
    grid_spec = pltpu.PrefetchScalarGridSpec(
        num_scalar_prefetch=2,
        grid=(n_blocks,),
        in_specs=[pl.BlockSpec((tb, half), rows),
                  pl.BlockSpec((None, None, D, D_EXPERT), lambda i, be, nv: (layer, be[i], 0, 0)),
                  pl.BlockSpec((None, None, D, D_EXPERT), lambda i, be, nv: (layer, be[i], 0, 0)),
                  pl.BlockSpec((None, None, D_EXPERT, D), lambda i, be, nv: (layer, be[i], 0, 0))],
        out_specs=pl.BlockSpec((tb, half), lambda i, be, nv: (i, 0)),
        scratch_shapes=[pltpu.VMEM((D, D_EXPERT), BF16), pltpu.VMEM((D, D_EXPERT), BF16),
                        pltpu.VMEM((D_EXPERT, D), BF16)],
    )
    return pl.pallas_call(
        _expert_kernel,
        grid_spec=grid_spec,
        out_shape=jax.ShapeDtypeStruct((P, half), U32),
        compiler_params=_cparams(("arbitrary",)),
        name="moe_experts",
    )(blk_e, nvalid, xs, w1, w3, w2)


def _sc_gather(data, idx):
    M = idx.shape[0]
    D = data.shape[1]
    W = SC_GATHER_WINDOW
    assert M % W == 0, (M, W)
    mesh = plsc.VectorSubcoreMesh(core_axis_name="core", subcore_axis_name="subcore")
    n_workers = mesh.num_cores * mesh.num_subcores
    assert M % (W * n_workers) == 0, (M, W, n_workers)

    @functools.partial(pl.kernel, out_type=jax.ShapeDtypeStruct((M, D), data.dtype), mesh=mesh,
                       scratch_types=[pltpu.VMEM((W,), I32), pltpu.VMEM((W, D), data.dtype)])
    def gather_kernel(x_hbm, i_hbm, o_hbm, i_vmem, buf):
        worker = lax.axis_index("core") * mesh.num_subcores + lax.axis_index("subcore")

        @pl.loop(0, M // (W * n_workers))
        def _(t):
            start = (t * n_workers + worker) * W
            pltpu.sync_copy(i_hbm.at[pl.ds(start, W)], i_vmem)
            pltpu.sync_copy(x_hbm.at[i_vmem], buf)
            pltpu.sync_copy(buf, o_hbm.at[pl.ds(start, W)])

    return gather_kernel(data, idx)


def _sc_dispatch(h2, dest, fill_idx, P):
    T, D = h2.shape
    W = SC_GATHER_WINDOW
    n_fill = fill_idx.shape[0]
    mesh = plsc.VectorSubcoreMesh(core_axis_name="core", subcore_axis_name="subcore")
    n_workers = mesh.num_cores * mesh.num_subcores
    assert T % (W * n_workers) == 0 and n_fill % (W * n_workers) == 0, (T, n_fill, W, n_workers)
    zeros = jnp.zeros((W, D), h2.dtype)

    @functools.partial(pl.kernel, out_type=jax.ShapeDtypeStruct((P, D), h2.dtype), mesh=mesh,
                       scratch_types=[pltpu.VMEM((W,), I32), pltpu.VMEM((W, D), h2.dtype)])
    def dispatch_kernel(h_hbm, d_hbm, f_hbm, z_hbm, o_hbm, i_vmem, buf):
        worker = lax.axis_index("core") * mesh.num_subcores + lax.axis_index("subcore")

        @pl.loop(0, T // (W * n_workers))
        def _(t):
            start = (t * n_workers + worker) * W
            pltpu.sync_copy(h_hbm.at[pl.ds(start, W)], buf)
            for c in range(2):
                pltpu.sync_copy(d_hbm.at[pl.ds(c * T + start, W)], i_vmem)
                pltpu.sync_copy(buf, o_hbm.at[i_vmem])

        pltpu.sync_copy(z_hbm, buf)

        @pl.loop(0, n_fill // (W * n_workers))
        def _(t):
            start = (t * n_workers + worker) * W
            pltpu.sync_copy(f_hbm.at[pl.ds(start, W)], i_vmem)
            pltpu.sync_copy(buf, o_hbm.at[i_vmem])

    return dispatch_kernel(h2, dest, fill_idx, zeros)


def _combine_kernel(x_ref, g0_ref, g1_ref, mf_ref, gt_ref, fg_ref, *rest, final):
    o_ref = rest[-1]
    mf = mf_ref[...]
    lo0, hi0 = _unpack_bf16_pair(g0_ref[...])
    lo1, hi1 = _unpack_bf16_pair(g1_ref[...])
    w0 = mf[:, 0:1]
    w1 = mf[:, 1:2]
    ffn = jnp.concatenate([lo0 * w0 + lo1 * w1, hi0 * w0 + hi1 * w1], axis=1)
    out = x_ref[...] + gt_ref[...] * ffn
    if final:
        out = _rms(out, fg_ref[...])
    o_ref[...] = out


def _combine(x1, S, mf, gt2, final_g, g, final, out_rows=None, out_tile0=0, out_prev=None):
    T, D = x1.shape
    tm = _pick(S, 512)
    per_b = S // tm
    nt = T // tm
    in_specs = [pl.BlockSpec((tm, D), lambda i: (i, 0)),
                pl.BlockSpec((tm, D // 2), lambda i: (i, 0)),
                pl.BlockSpec((tm, D // 2), lambda i: (i + nt, 0)),
                pl.BlockSpec((tm, 2), lambda i: (i, 0)),
                pl.BlockSpec((None, 1, D), lambda i: (i // per_b, 0, 0)),
                pl.BlockSpec((1, D), lambda i: (0, 0))]
    args = [x1, g, g, mf, gt2, final_g]
    aliases = {}
    if out_prev is not None:
        in_specs.append(pl.BlockSpec(memory_space=pl.ANY))
        args.append(out_prev)
        aliases = {len(args) - 1: 0}
    return pl.pallas_call(
        functools.partial(_combine_kernel, final=final),
        grid=(nt,),
        in_specs=in_specs,
        out_specs=pl.BlockSpec((tm, D), lambda i: (i + out_tile0, 0)),
        out_shape=jax.ShapeDtypeStruct((T if out_rows is None else out_rows, D), F32),
        input_output_aliases=aliases,
        compiler_params=_cparams(("arbitrary",)),
        name="moe_combine",
    )(*args)


def _prep_w_in(w):
    offs = [0]
    for s in (256, 256, 512, 512, 512, 512, 512, FOX_HEADS, MLA_Q_RANK, MLA_KV_RANK, MLA_ROPE):
        offs.append(offs[-1] + s)
    rq, rk, rv, rg, fq, fk, fv, ff, mq, mkv, mkr = [w[:, offs[i]:offs[i + 1]] for i in range(11)]
    pad = jnp.zeros((w.shape[0], LANES - MLA_ROPE - FOX_HEADS), w.dtype)
    return jnp.concatenate([rq, rk, rv, rg, fq, fk, fv, mq, mkv, mkr, ff, pad], axis=1).astype(BF16)


def _prep_wq_up(w):
    r = w.reshape(MLA_Q_RANK, MLA_HEADS, MLA_NOPE + MLA_ROPE)
    r = jnp.pad(r, ((0, 0), (0, 0), (0, MLA_DQ - MLA_NOPE - MLA_ROPE)))
    return r.reshape(MLA_Q_RANK, MLA_HEADS * MLA_DQ).astype(BF16)


def _prep_router(w_grp, b_grp, w_exp, b_exp):
    D = w_grp.shape[0]
    pad = LANES - N_EXPERTS - N_GROUPS
    rwt = jnp.concatenate([w_exp, w_grp, jnp.zeros((D, pad), F32)], axis=1).astype(F32).T
    rwh = rwt.astype(BF16)
    rwl = (rwt - rwh.astype(F32)).astype(BF16)
    rb = jnp.concatenate([b_exp, b_grp, jnp.zeros((pad,), F32)]).astype(F32).reshape(LANES, 1)
    return rwh, rwl, rb


def kernel(x, c, positions, ada_w, ada_b, norm1_g, norm2_g, w_in, fox_fb, mla_q_norm_g, mla_wq_up, mla_kv_norm_g, mla_wkv_up, gate_w, gate_b, branch_w, out_w, router_grp_w, router_grp_b, router_exp_w, router_exp_b, exp_w1, exp_w3, exp_w2, final_g):
    B, S, D = x.shape
    L = ada_w.shape[0]
    T = B * S
    n_str = N_STREAMS if B % N_STREAMS == 0 else 1
    Bs = B // n_str
    Ts = Bs * S
    As = 2 * Ts
    tb = _pick(As, 512)
    n_blocks = As // tb + N_EXPERTS
    P = n_blocks * tb
    tiles_per_stream = Ts // _pick(S, 512)

    mod = _adaln(c, ada_w, ada_b)
    cos_t, sin_t = _rope_tables(positions)
    cos_s = [cos_t[h * Ts:(h + 1) * Ts] for h in range(n_str)]
    sin_s = [sin_t[h * Ts:(h + 1) * Ts] for h in range(n_str)]
    x_full = x.reshape(T, D)
    xs2 = [x_full] * n_str
    x_tile0 = [h * tiles_per_stream for h in range(n_str)]
    final_g2 = final_g.reshape(1, D)
    r3 = lambda a: a.reshape(Bs, S, -1)
    out = None

    for l in range(L):
        mods = [[mod[l, h * Bs:(h + 1) * Bs, i * D:(i + 1) * D].reshape(Bs, 1, D) for i in range(6)]
                for h in range(n_str)]
        g1 = norm1_g[l].reshape(1, D)
        g2 = norm2_g[l].reshape(1, D)
        w_all = _prep_w_in(w_in[l])
        wq = _prep_wq_up(mla_wq_up[l])
        wkv = mla_wkv_up[l].astype(BF16)
        gq = mla_q_norm_g[l].reshape(1, -1)
        gkv = mla_kv_norm_g[l].reshape(1, -1)
        rwh, rwl, rb = _prep_router(router_grp_w[l], router_grp_b[l], router_exp_w[l], router_exp_b[l])
        gw = gate_w[l].astype(BF16)
        gb = gate_b[l].reshape(1, -1)
        bw = branch_w[l].astype(BF16)
        ow = out_w[l].astype(BF16)
        last = l == L - 1

        merged = []
        for h in range(n_str):
            sh1, sc1, gt1, sh2, sc2, gt2 = mods[h]
            (rq, rk, rv, rg, fq, fk, fv, mq, mk, kpe, mv, ffp) = _proj(
                xs2[h], x_tile0[h], S, g1, sh1, sc1, w_all, cos_s[h], sin_s[h], gq, wq, gkv, wkv)
            ya = _retention(rq, rk, rv, rg, Bs, S).reshape(Ts, -1)
            kb = _fox_gate(ffp, fox_fb[l], S)
            yb = _flash(r3(fq), r3(fk), r3(kb), r3(fv), Bs, S, FOX_HEADS, FOX_DH, True, 3,
                        "flash_fox").reshape(Ts, -1)
            yc = _flash(r3(mq), r3(mk), r3(kpe), r3(mv), Bs, S, MLA_HEADS, MLA_DQ, False, 0,
                        "flash_mla").reshape(Ts, -1)
            merged.append(_merge(xs2[h], x_tile0[h], S, ya, yb, yc, g1, sh1, sc1, gt1, g2, sh2, sc2,
                                 gw, gb, bw, ow, rwh, rwl, rb))

        routed = []
        for h in range(n_str):
            x1, h2, mi, mf, cnt = merged[h]
            counts = cnt[:, 0].astype(I32)
            pcounts = (counts + tb - 1) // tb * tb
            pends = jnp.cumsum(pcounts)
            pstarts = pends - pcounts
            sel = mi[0:2, :, None] == jnp.arange(N_EXPERTS, dtype=I32)
            dest = (jnp.sum(jnp.where(sel, pstarts, 0), axis=-1) + mi[2:4]).reshape(As)
            blk_pos = jnp.arange(n_blocks, dtype=I32) * tb
            blk_e = jnp.minimum(jnp.sum((pends[None, :] <= blk_pos[:, None]).astype(I32), axis=1),
                                N_EXPERTS - 1)
            nvalid = (pends[-1:] // tb).astype(I32)
            fr = jnp.arange(tb, dtype=I32)[None, :]
            is_pad = (fr < (pcounts - counts)[:, None]).reshape(-1)
            pad_slot = ((pstarts + counts)[:, None] + fr).reshape(-1)
            tail_rank = jnp.cumsum(jnp.logical_not(is_pad).astype(I32)) - 1
            fill_idx = jnp.where(is_pad, pad_slot, pends[-1] + tail_rank)
            routed.append((dest, blk_e, nvalid, _sc_dispatch(h2, dest, fill_idx, P)))

        ys = [_experts(routed[h][3], routed[h][1], routed[h][2], exp_w1, exp_w3, exp_w2, l, tb)
              for h in range(n_str)]
        gathered = [_sc_gather(ys[h], routed[h][0]) for h in range(n_str)]
        for h in range(n_str):
            x1, _, _, mf, _ = merged[h]
            gt2 = mods[h][5]
            if last:
                out = _combine(x1, S, mf[0:2].T, gt2, final_g2, gathered[h], True,
                               out_rows=T, out_tile0=h * tiles_per_stream, out_prev=out)
            else:
                xs2[h] = _combine(x1, S, mf[0:2].T, gt2, final_g2, gathered[h], False)
        x_tile0 = [0] * n_str

    return out.reshape(B, S, D)
```

```python
import functools
import math

import jax
import jax.numpy as jnp
from jax import lax
from jax.experimental import pallas as pl
from jax.experimental.pallas import tpu as pltpu
from jax.experimental.pallas import tpu_sc as plsc

F32 = jnp.float32
BF16 = jnp.bfloat16
I32 = jnp.int32
U32 = jnp.uint32
HIGHEST = lax.Precision.HIGHEST

EPS = 1e-6
ROPE_THETA = 10000.0
RET_HEADS = 4
RET_DK = 64
RET_DV = 128
RET_CHUNK = 128
FOX_HEADS = 4
FOX_DH = 128
MLA_HEADS = 4
MLA_Q_RANK = 256
MLA_KV_RANK = 128
MLA_NOPE = 128
MLA_ROPE = 64
MLA_V = 128
MLA_DQ = 256
N_BRANCH = 3
BRANCH_W = 512
N_GROUPS = 4
EXP_PER_GROUP = 8
N_EXPERTS = N_GROUPS * EXP_PER_GROUP
D_EXPERT = 512

LANES = 128
V7X_VMEM_LIMIT = 56 * 1024 * 1024

C_RQ, C_RK, C_RV, C_RG = 0, 256, 512, 1024
C_FQ, C_FK, C_FV = 1536, 2048, 2560
C_MQ, C_MKV, C_TAIL = 3072, 3328, 3456
D_IN_PAD = 3584
FF_LANE = MLA_ROPE

NEG_BIG = -1e30
SC_GATHER_WINDOW = 128
N_STREAMS = 1
FLASH_UNROLL = 4
MERGE_COL_CHUNK = 256
LOG2E = math.log2(math.e)


def _cparams(sem):
    return pltpu.CompilerParams(dimension_semantics=sem, vmem_limit_bytes=V7X_VMEM_LIMIT)


def _pick(n, pref):
    t = min(n, pref)
    assert n % t == 0, (n, t)
    return t


def _adaln_kernel(c_ref, w_ref, b_ref, o_ref):
    c = c_ref[...]
    ca = c * jax.nn.sigmoid(c)
    o_ref[...] = jnp.dot(ca, w_ref[...], preferred_element_type=F32, precision=HIGHEST) + b_ref[...]


def _adaln(c, ada_w, ada_b):
    L, D, N = ada_w.shape
    B = c.shape[0]
    tn = _pick(N, 1536)
    return pl.pallas_call(
        _adaln_kernel,
        grid=(L, N // tn),
        in_specs=[pl.BlockSpec((B, D), lambda l, j: (0, 0)),
                  pl.BlockSpec((None, D, tn), lambda l, j: (l, 0, j)),
                  pl.BlockSpec((None, 1, tn), lambda l, j: (l, 0, j))],
        out_specs=pl.BlockSpec((None, B, tn), lambda l, j: (l, 0, j)),
        out_shape=jax.ShapeDtypeStruct((L, B, N), F32),
        compiler_params=_cparams(("arbitrary", "arbitrary")),
        name="adaln",
    )(c, ada_w, ada_b.reshape(L, 1, N))


def _rope_table_kernel(pos_ref, inv_ref, sign_ref, cos_ref, sin_ref):
    ang = pos_ref[...].astype(F32) * inv_ref[...]
    cos_ref[...] = jnp.cos(ang)
    sin_ref[...] = jnp.sin(ang) * sign_ref[...]


def _rope_tables(positions):
    T = positions.size
    tm = _pick(T, 2048)
    half = MLA_ROPE // 2
    inv = ROPE_THETA ** (-jnp.arange(0, MLA_ROPE, 2, dtype=F32) / MLA_ROPE)
    inv_t = jnp.tile(inv, LANES // half).reshape(1, LANES)
    sign = jnp.where((jnp.arange(LANES) % MLA_ROPE) < half, -1.0, 1.0).astype(F32).reshape(1, LANES)
    return pl.pallas_call(
        _rope_table_kernel,
        grid=(T // tm,),
        in_specs=[pl.BlockSpec((tm, 1), lambda i: (i, 0)),
                  pl.BlockSpec((1, LANES), lambda i: (0, 0)),
                  pl.BlockSpec((1, LANES), lambda i: (0, 0))],
        out_specs=[pl.BlockSpec((tm, LANES), lambda i: (i, 0))] * 2,
        out_shape=[jax.ShapeDtypeStruct((T, LANES), F32)] * 2,
        compiler_params=_cparams(("arbitrary",)),
        name="rope_tables",
    )(positions.reshape(T, 1), inv_t, sign)


def _rope_slab(x, cos_t, sin_t, lane):
    nxt = pltpu.roll(x, LANES - 32, axis=1)
    prv = pltpu.roll(x, 32, axis=1)
    swapped = jnp.where((lane & 32) == 0, nxt, prv)
    return x * cos_t + swapped * sin_t


def _pack_bf16_pair(lo, hi):
    lo_bits = lax.shift_right_logical(lax.bitcast_convert_type(lo.astype(BF16).astype(F32), U32), jnp.uint32(16))
    hi_bits = lax.bitcast_convert_type(hi.astype(BF16).astype(F32), U32) & jnp.uint32(0xFFFF0000)
    return hi_bits | lo_bits


def _unpack_bf16_pair(w):
    lo = lax.bitcast_convert_type(lax.shift_left(w, jnp.uint32(16)), F32)
    hi = lax.bitcast_convert_type(w & jnp.uint32(0xFFFF0000), F32)
    return lo, hi


def _norm_mod(x, g, shift, scale):
    y = x * lax.rsqrt(jnp.mean(x * x, axis=-1, keepdims=True) + EPS)
    return (y * g) * (1.0 + scale) + shift


def _rms(x, g):
    return x * lax.rsqrt(jnp.mean(x * x, axis=-1, keepdims=True) + EPS) * g


def _proj_kernel(x_ref, g_ref, sh_ref, sc_ref, w_ref, cos_ref, sin_ref,
                 gq_ref, wq_ref, gkv_ref, wkv_ref,
                 rq_ref, rk_ref, rv_ref, rg_ref, fq_ref, fk_ref, fv_ref,
                 mq_ref, mk_ref, kpe_ref, mv_ref, ff_ref):
    h = _norm_mod(x_ref[...], g_ref[...], sh_ref[...], sc_ref[...]).astype(BF16)
    cos_t = cos_ref[...]
    sin_t = sin_ref[...]
    lane = lax.broadcasted_iota(I32, cos_t.shape, 1)

    def proj(c0, width):
        return jnp.dot(h, w_ref[:, c0:c0 + width], preferred_element_type=F32)

    rq = proj(C_RQ, 256)
    rk = proj(C_RK, 256)
    for s in range(2):
        sl = slice(s * LANES, (s + 1) * LANES)
        q2 = _rope_slab(rq[:, sl], cos_t, sin_t, lane)
        k2 = _rope_slab(rk[:, sl], cos_t, sin_t, lane) * (RET_DK ** -0.5)
        for half in range(2):
            mine = (lane < RET_DK) if half == 0 else (lane >= RET_DK)
            hs = slice((2 * s + half) * LANES, (2 * s + half + 1) * LANES)
            rq_ref[:, hs] = jnp.where(mine, q2, 0.0).astype(BF16)
            rk_ref[:, hs] = jnp.where(mine, k2, 0.0).astype(BF16)
    rv_ref[...] = proj(C_RV, 512).astype(BF16)
    rg_ref[...] = proj(C_RG, 512).astype(BF16)
    fq_ref[...] = (proj(C_FQ, 512) * (FOX_DH ** -0.5 * LOG2E)).astype(BF16)
    fk_ref[...] = proj(C_FK, 512).astype(BF16)
    fv_ref[...] = proj(C_FV, 512).astype(BF16)

    tail = proj(C_TAIL, LANES)
    ff_ref[...] = tail
    kpe_ref[...] = jnp.where(lane < MLA_ROPE, _rope_slab(tail, cos_t, sin_t, lane), 0.0).astype(BF16)

    qn = _rms(proj(C_MQ, MLA_Q_RANK), gq_ref[...]).astype(BF16)
    qh = jnp.dot(qn, wq_ref[...], preferred_element_type=F32)
    q_scale = (MLA_NOPE + MLA_ROPE) ** -0.5 * LOG2E
    for hd in range(MLA_HEADS):
        c0 = hd * MLA_DQ
        mq_ref[:, c0:c0 + LANES] = (qh[:, c0:c0 + LANES] * q_scale).astype(BF16)
        pe = _rope_slab(qh[:, c0 + LANES:c0 + 2 * LANES], cos_t, sin_t, lane)
        mq_ref[:, c0 + LANES:c0 + 2 * LANES] = jnp.where(lane < MLA_ROPE, pe * q_scale, 0.0).astype(BF16)

    kvn = _rms(proj(C_MKV, MLA_KV_RANK), gkv_ref[...]).astype(BF16)
    kvh = jnp.dot(kvn, wkv_ref[...], preferred_element_type=F32)
    for hd in range(MLA_HEADS):
        c0 = hd * (MLA_NOPE + MLA_V)
        mk_ref[:, hd * MLA_NOPE:(hd + 1) * MLA_NOPE] = kvh[:, c0:c0 + MLA_NOPE].astype(BF16)
        mv_ref[:, hd * MLA_V:(hd + 1) * MLA_V] = kvh[:, c0 + MLA_NOPE:c0 + MLA_NOPE + MLA_V].astype(BF16)


def _proj(x2, x_tile0, S, g, sh, sc, w_all, cos_t, sin_t, gq, wq, gkv, wkv):
    D = x2.shape[1]
    T = cos_t.shape[0]
    tm = _pick(S, 512)
    per_b = S // tm
    row = lambda i: (i, 0)
    xrow = lambda i: (i + x_tile0, 0)
    const = lambda i: (0, 0)
    batch = lambda i: (i // per_b, 0, 0)
    widths = [512, 512, 512, 512, 512, 512, 512, MLA_HEADS * MLA_DQ, MLA_HEADS * MLA_NOPE, LANES,
              MLA_HEADS * MLA_V]
    out_shape = [jax.ShapeDtypeStruct((T, w), BF16) for w in widths]
    out_shape.append(jax.ShapeDtypeStruct((T, LANES), F32))
    out_specs = [pl.BlockSpec((tm, w), row) for w in widths] + [pl.BlockSpec((tm, LANES), row)]
    return pl.pallas_call(
        _proj_kernel,
        grid=(T // tm,),
        in_specs=[pl.BlockSpec((tm, D), xrow),
                  pl.BlockSpec((1, D), const),
                  pl.BlockSpec((None, 1, D), batch),
                  pl.BlockSpec((None, 1, D), batch),
                  pl.BlockSpec((D, D_IN_PAD), const),
                  pl.BlockSpec((tm, LANES), row),
                  pl.BlockSpec((tm, LANES), row),
                  pl.BlockSpec((1, MLA_Q_RANK), const),
                  pl.BlockSpec((MLA_Q_RANK, MLA_HEADS * MLA_DQ), const),
                  pl.BlockSpec((1, MLA_KV_RANK), const),
                  pl.BlockSpec((MLA_KV_RANK, MLA_HEADS * (MLA_NOPE + MLA_V)), const)],
        out_specs=out_specs,
        out_shape=out_shape,
        compiler_params=_cparams(("arbitrary",)),
        name="proj",
    )(x2, g, sh, sc, w_all, cos_t, sin_t, gq, wq, gkv, wkv)


def _split3(x):
    a = x.astype(BF16)
    r = x - a.astype(F32)
    b = r.astype(BF16)
    c = (r - b.astype(F32)).astype(BF16)
    return a, b, c


def _fox_gate_kernel(ff_ref, fb_ref, tri_ref, kb_ref, carry_ref, *, tiles_per_seq):
    @pl.when(pl.program_id(0) % tiles_per_seq == 0)
    def _():
        carry_ref[...] = jnp.zeros_like(carry_ref)

    tm = ff_ref.shape[0]
    lane = lax.broadcasted_iota(I32, (tm, LANES), 1)
    z = ff_ref[...] + fb_ref[...]
    ls = -(jnp.maximum(-z, 0.0) + jnp.log1p(jnp.exp(-jnp.abs(z))))
    ls = jnp.where((lane >= FF_LANE) & (lane < FF_LANE + FOX_HEADS), ls, 0.0)
    tri = tri_ref[...]
    f = carry_ref[...]
    for part in _split3(ls):
        f = f + jnp.dot(tri, part, preferred_element_type=F32)
    carry_ref[...] = f[tm - 1:tm, :]
    hi, mid, lo = [part.astype(F32) for part in _split3(f * LOG2E)]
    for hd in range(FOX_HEADS):
        src = FF_LANE + hd
        slab = jnp.where(lane == 0, pltpu.roll(hi, (LANES - src) % LANES, axis=1),
               jnp.where(lane == 1, pltpu.roll(mid, (LANES + 1 - src) % LANES, axis=1),
               jnp.where(lane == 2, pltpu.roll(lo, (LANES + 2 - src) % LANES, axis=1), 0.0)))
        kb_ref[:, hd * LANES:(hd + 1) * LANES] = slab.astype(BF16)


def _fox_gate(ffp, fb, S):
    T = ffp.shape[0]
    tm = _pick(S, 512)
    idx = jnp.arange(tm)
    tri = (idx[None, :] <= idx[:, None]).astype(BF16)
    fbv = jnp.zeros((1, LANES), F32).at[0, FF_LANE:FF_LANE + FOX_HEADS].set(fb)
    return pl.pallas_call(
        functools.partial(_fox_gate_kernel, tiles_per_seq=S // tm),
        grid=(T // tm,),
        in_specs=[pl.BlockSpec((tm, LANES), lambda i: (i, 0)),
                  pl.BlockSpec((1, LANES), lambda i: (0, 0)),
                  pl.BlockSpec((tm, tm), lambda i: (0, 0))],
        out_specs=pl.BlockSpec((tm, FOX_HEADS * LANES), lambda i: (i, 0)),
        out_shape=jax.ShapeDtypeStruct((T, FOX_HEADS * LANES), BF16),
        scratch_shapes=[pltpu.VMEM((1, LANES), F32)],
        compiler_params=_cparams(("arbitrary",)),
        name="fox_gate",
    )(ffp, fbv, tri)


def _retention_kernel(dchunk_ref, q_ref, k_ref, v_ref, g_ref, dmask_ref, din_ref, dout_ref,
                      o_ref, state_ref, *, n_chunks):
    @pl.when(pl.program_id(1) == 0)
    def _():
        state_ref[...] = jnp.zeros_like(state_ref)

    C = RET_CHUNK
    for ci in range(n_chunks):
        rows = slice(ci * C, (ci + 1) * C)
        for hd in range(RET_HEADS):
            slab = slice(hd * LANES, (hd + 1) * LANES)
            q = q_ref[rows, slab]
            k = k_ref[rows, slab]
            vcols = slice(hd * RET_DV, (hd + 1) * RET_DV)
            v = v_ref[rows, vcols]
            state = state_ref[hd]
            scores = lax.dot_general(q, k, (((1,), (1,)), ((), ())),
                                     preferred_element_type=F32) * dmask_ref[hd]
            inner = jnp.dot(scores.astype(BF16), v, preferred_element_type=F32)
            cross = jnp.dot(q, state.astype(BF16), preferred_element_type=F32) * din_ref[hd]
            o = inner + cross
            vd = (v.astype(F32) * dout_ref[hd]).astype(BF16)
            kv = lax.dot_general(k, vd, (((0,), (0,)), ((), ())), preferred_element_type=F32)
            state_ref[hd] = state * dchunk_ref[hd] + kv
            mu = jnp.mean(o, axis=-1, keepdims=True)
            d = o - mu
            var = jnp.mean(d * d, axis=-1, keepdims=True)
            on = d * lax.rsqrt(var + EPS)
            g = g_ref[rows, vcols].astype(F32)
            o_ref[rows, vcols] = (g * jax.nn.sigmoid(g) * on).astype(BF16)


def _retention(rq, rk, rv, rg, B, S):
    H, C = RET_HEADS, RET_CHUNK
    tr = _pick(S, 4 * C)
    n_chunks = tr // C
    log_gamma = jnp.log1p(-jnp.exp2(-5.0 - jnp.arange(H, dtype=F32)))
    idx = jnp.arange(C, dtype=F32)
    rel = idx[:, None] - idx[None, :]
    dmask = jnp.where(rel >= 0, jnp.exp(log_gamma[:, None, None] * jnp.maximum(rel, 0.0)), 0.0)
    decay_in = jnp.exp(log_gamma[:, None] * (idx + 1.0))
    decay_out = jnp.exp(log_gamma[:, None] * (C - 1.0 - idx))
    decay_chunk = jnp.exp(log_gamma * C)
    din = jnp.broadcast_to(decay_in[:, :, None], (H, C, RET_DV))
    dout = jnp.broadcast_to(decay_out[:, :, None], (H, C, RET_DV))
    tok = lambda b, i: (b, i, 0)
    const3 = lambda b, i: (0, 0, 0)
    return pl.pallas_call(
        functools.partial(_retention_kernel, n_chunks=n_chunks),
        grid=(B, S // tr),
        in_specs=[pl.BlockSpec(memory_space=pltpu.SMEM),
                  pl.BlockSpec((None, tr, H * LANES), tok),
                  pl.BlockSpec((None, tr, H * LANES), tok),
                  pl.BlockSpec((None, tr, H * RET_DV), tok),
                  pl.BlockSpec((None, tr, H * RET_DV), tok),
                  pl.BlockSpec((H, C, C), const3),
                  pl.BlockSpec((H, C, RET_DV), const3),
                  pl.BlockSpec((H, C, RET_DV), const3)],
        out_specs=pl.BlockSpec((None, tr, H * RET_DV), tok),
        out_shape=jax.ShapeDtypeStruct((B, S, H * RET_DV), BF16),
        scratch_shapes=[pltpu.VMEM((H, LANES, RET_DV), F32)],
        compiler_params=_cparams(("arbitrary", "arbitrary")),
        name="retention",
    )(decay_chunk, rq.reshape(B, S, -1), rk.reshape(B, S, -1), rv.reshape(B, S, -1),
      rg.reshape(B, S, -1), dmask, din, dout)


def _flash_kernel(q_ref, k_ref, ke_ref, v_ref, o_ref, m_ref, l_ref, acc_ref, s_ref, *, q_bias_cols):
    qi = pl.program_id(2)
    tq = q_ref.shape[0]
    tk = tq
    m_ref[...] = jnp.full(m_ref.shape, NEG_BIG, F32)
    l_ref[...] = jnp.zeros_like(l_ref)
    acc_ref[...] = jnp.zeros_like(acc_ref)
    q = q_ref[...]
    if q_bias_cols:
        lane = lax.broadcasted_iota(I32, (tq, LANES), 1)
        q = jnp.concatenate([q, jnp.where(lane < q_bias_cols, -1.0, 0.0).astype(BF16)], axis=1)

    def scores(j):
        start = pl.multiple_of(j * tk, tk)
        kj = jnp.concatenate([k_ref[pl.ds(start, tk), :], ke_ref[pl.ds(start, tk), :]], axis=1)
        return lax.dot_general(kj, q, (((1,), (1,)), ((), ())), preferred_element_type=F32)

    def update(s, j):
        m_prev = m_ref[...]
        m_new = jnp.maximum(m_prev, jnp.max(s, axis=0, keepdims=True))
        alpha = jnp.exp2(m_prev - m_new)
        p = jnp.exp2(s - m_new)
        l_ref[...] = alpha * l_ref[...] + jnp.sum(p, axis=0, keepdims=True)
        start = pl.multiple_of(j * tk, tk)
        vj = v_ref[pl.ds(start, tk), :]
        acc_ref[...] = alpha * acc_ref[...] + lax.dot_general(
            vj, p.astype(BF16), (((0,), (0,)), ((), ())), preferred_element_type=F32)
        m_ref[...] = m_new

    def causal(s):
        key = lax.broadcasted_iota(I32, s.shape, 0)
        qry = lax.broadcasted_iota(I32, s.shape, 1)
        return jnp.where(key <= qry, s, NEG_BIG)

    def run(n_steps, base):
        for t in range(n_steps):
            s = s_ref[t % 2]
            s_ref[(t + 1) % 2] = scores(base + t + 1)
            update(s, base + t)

    s_ref[0] = scores(0)

    def unrolled(p, carry):
        run(FLASH_UNROLL, FLASH_UNROLL * p)
        return carry

    lax.fori_loop(0, qi // FLASH_UNROLL, unrolled, 0)

    rem = qi % FLASH_UNROLL
    for r in range(FLASH_UNROLL):
        @pl.when(rem == r)
        def _(r=r):
            run(r, qi - r)
            update(causal(s_ref[r % 2]), qi)

    o_ref[...] = (acc_ref[...] / l_ref[...]).T.astype(o_ref.dtype)


def _flash(q, k, ke, v, B, S, H, dq, ke_per_head, q_bias_cols, name):
    tq = _pick(S, 512)
    nq = S // tq
    ke_map = (lambda b, h, i: (b, 0, h)) if ke_per_head else (lambda b, h, i: (b, 0, 0))
    return pl.pallas_call(
        functools.partial(_flash_kernel, q_bias_cols=q_bias_cols),
        grid=(B, H, nq),
        in_specs=[pl.BlockSpec((None, tq, dq), lambda b, h, i: (b, i, h)),
                  pl.BlockSpec((None, S, LANES), lambda b, h, i: (b, 0, h)),
                  pl.BlockSpec((None, S, LANES), ke_map),
                  pl.BlockSpec((None, S, LANES), lambda b, h, i: (b, 0, h))],
        out_specs=pl.BlockSpec((None, tq, LANES), lambda b, h, i: (b, i, h)),
        out_shape=jax.ShapeDtypeStruct((B, S, H * LANES), BF16),
        scratch_shapes=[pltpu.VMEM((1, tq), F32), pltpu.VMEM((1, tq), F32),
                        pltpu.VMEM((LANES, tq), F32), pltpu.VMEM((2, tq, tq), F32)],
        compiler_params=_cparams(("arbitrary", "arbitrary", "arbitrary")),
        name=name,
    )(q, k, ke, v)


def _merge_kernel(x_ref, ya_ref, yb_ref, yc_ref, g1_ref, sh1_ref, sc1_ref, gt1_ref,
                  g2_ref, sh2_ref, sc2_ref, gw_ref, gb_ref, bw_ref, ow_ref,
                  rwh_ref, rwl_ref, rb_ref, tri_ref,
                  x1_ref, h2_ref, mi_ref, mf_ref, cnt_ref, h_ref, mg_ref, carry_ref):
    tm, D = x_ref.shape
    cn = MERGE_COL_CHUNK

    @pl.when(pl.program_id(0) == 0)
    def _():
        carry_ref[...] = jnp.zeros_like(carry_ref)

    h_ref[...] = _norm_mod(x_ref[...], g1_ref[...], sh1_ref[...], sc1_ref[...]).astype(BF16)
    for n in range(D // cn):
        cols = slice(n * cn, (n + 1) * cn)
        merged = None
        for i, y_ref in enumerate((ya_ref, yb_ref, yc_ref)):
            gcols = slice(i * D + n * cn, i * D + (n + 1) * cn)
            gate = jax.nn.sigmoid(jnp.dot(h_ref[...], gw_ref[:, gcols], preferred_element_type=F32)
                                  + gb_ref[:, gcols])
            br = jnp.dot(y_ref[...], bw_ref[i, :, cols], preferred_element_type=F32)
            merged = gate * br if merged is None else merged + gate * br
        mg_ref[:, cols] = merged.astype(BF16)
    for n in range(D // cn):
        cols = slice(n * cn, (n + 1) * cn)
        mix = jnp.dot(mg_ref[...], ow_ref[:, cols], preferred_element_type=F32)
        x1_ref[:, cols] = x_ref[:, cols] + gt1_ref[:, cols] * mix
    h2 = _norm_mod(x1_ref[...], g2_ref[...], sh2_ref[...], sc2_ref[...])
    h2_ref[...] = _pack_bf16_pair(h2[:, :D // 2], h2[:, D // 2:])

    hh = h2.astype(BF16)
    hl = (h2 - hh.astype(F32)).astype(BF16)
    nt = lambda a, b: lax.dot_general(a, b, (((1,), (1,)), ((), ())), preferred_element_type=F32)
    lt = nt(rwh_ref[...], hh) + nt(rwh_ref[...], hl) + nt(rwl_ref[...], hh) + rb_ref[...]
    row8 = lax.broadcasted_iota(I32, (EXP_PER_GROUP, tm), 0)
    gl = jnp.where(row8 < N_GROUPS, lt[N_EXPERTS:N_EXPERTS + EXP_PER_GROUP, :], -jnp.inf)
    gmax = jnp.max(gl, axis=0, keepdims=True)
    g_idx = jnp.min(jnp.where(gl == gmax, row8, EXP_PER_GROUP), axis=0, keepdims=True)
    g_w = 1.0 / jnp.sum(jnp.exp(gl - gmax), axis=0, keepdims=True)
    el = lt[(N_GROUPS - 1) * EXP_PER_GROUP:N_EXPERTS, :]
    for g in range(N_GROUPS - 2, -1, -1):
        el = jnp.where(g_idx == g, lt[g * EXP_PER_GROUP:(g + 1) * EXP_PER_GROUP, :], el)
    e1 = jnp.max(el, axis=0, keepdims=True)
    i1 = jnp.min(jnp.where(el == e1, row8, EXP_PER_GROUP), axis=0, keepdims=True)
    el2 = jnp.where(row8 == i1, -jnp.inf, el)
    e2 = jnp.max(el2, axis=0, keepdims=True)
    i2 = jnp.min(jnp.where(el2 == e2, row8, EXP_PER_GROUP), axis=0, keepdims=True)
    r = jnp.exp(e2 - e1)
    w1 = g_w / (1.0 + r)
    w2 = g_w * r / (1.0 + r)
    eid1 = g_idx * EXP_PER_GROUP + i1
    eid2 = g_idx * EXP_PER_GROUP + i2

    rowe = lax.broadcasted_iota(I32, (N_EXPERTS, tm), 0)
    hit1 = rowe == eid1
    hit2 = rowe == eid2
    onehot = jnp.where(hit1 | hit2, 1.0, 0.0)
    before = (jnp.dot(onehot.astype(BF16), tri_ref[...], preferred_element_type=F32)
              + jnp.concatenate([carry_ref[...]] * (tm // LANES), axis=1))
    rank1 = jnp.sum(jnp.where(hit1, before, 0.0), axis=0, keepdims=True)
    rank2 = jnp.sum(jnp.where(hit2, before, 0.0), axis=0, keepdims=True)
    carry_ref[...] = carry_ref[...] + jnp.sum(onehot, axis=1, keepdims=True)
    cnt_ref[...] = carry_ref[...]

    mi_ref[...] = jnp.where(row8 == 0, eid1,
                  jnp.where(row8 == 1, eid2,
                  jnp.where(row8 == 2, rank1.astype(I32),
                  jnp.where(row8 == 3, rank2.astype(I32), 0))))
    mf_ref[...] = jnp.where(row8 == 0, w1, jnp.where(row8 == 1, w2, 0.0))


def _merge(x2, x_tile0, S, ya, yb, yc, g1, sh1, sc1, gt1, g2, sh2, sc2, gw, gb, bw, ow, rwh, rwl, rb):
    D = x2.shape[1]
    T = ya.shape[0]
    tm = _pick(S, 512)
    per_b = S // tm
    row = lambda i: (i, 0)
    xrow = lambda i: (i + x_tile0, 0)
    col = lambda i: (0, i)
    const = lambda i: (0, 0)
    batch = lambda i: (i // per_b, 0, 0)
    idx = jnp.arange(tm)
    tri = (idx[:, None] < idx[None, :]).astype(BF16)
    vecb = pl.BlockSpec((None, 1, D), batch)
    return pl.pallas_call(
        _merge_kernel,
        grid=(T // tm,),
        in_specs=[pl.BlockSpec((tm, D), xrow),
                  pl.BlockSpec((tm, BRANCH_W), row),
                  pl.BlockSpec((tm, BRANCH_W), row),
                  pl.BlockSpec((tm, BRANCH_W), row),
                  pl.BlockSpec((1, D), const), vecb, vecb, vecb,
                  pl.BlockSpec((1, D), const), vecb, vecb,
                  pl.BlockSpec((D, N_BRANCH * D), const),
                  pl.BlockSpec((1, N_BRANCH * D), const),
                  pl.BlockSpec((N_BRANCH, BRANCH_W, D), lambda i: (0, 0, 0)),
                  pl.BlockSpec((D, D), const),
                  pl.BlockSpec((LANES, D), const),
                  pl.BlockSpec((LANES, D), const),
                  pl.BlockSpec((LANES, 1), const),
                  pl.BlockSpec((tm, tm), const)],
        out_specs=[pl.BlockSpec((tm, D), row), pl.BlockSpec((tm, D // 2), row),
                   pl.BlockSpec((8, tm), col), pl.BlockSpec((8, tm), col),
                   pl.BlockSpec((N_EXPERTS, LANES), const)],
        out_shape=[jax.ShapeDtypeStruct((T, D), F32), jax.ShapeDtypeStruct((T, D // 2), U32),
                   jax.ShapeDtypeStruct((8, T), I32), jax.ShapeDtypeStruct((8, T), F32),
                   jax.ShapeDtypeStruct((N_EXPERTS, LANES), F32)],
        scratch_shapes=[pltpu.VMEM((tm, D), BF16), pltpu.VMEM((tm, D), BF16),
                        pltpu.VMEM((N_EXPERTS, LANES), F32)],
        compiler_params=_cparams(("arbitrary",)),
        name="merge",
    )(x2, ya, yb, yc, g1, sh1, sc1, gt1, g2, sh2, sc2, gw, gb, bw, ow, rwh, rwl, rb, tri)


def _expert_kernel(blk_e_ref, nvalid_ref, xs_ref, w1_ref, w3_ref, w2_ref, ys_ref, w1b_ref, w3b_ref, w2b_ref):
    i = pl.program_id(0)
    valid = i < nvalid_ref[0]
    new_expert = jnp.logical_or(i == 0, blk_e_ref[i] != blk_e_ref[jnp.maximum(i - 1, 0)])

    @pl.when(jnp.logical_and(valid, new_expert))
    def _():
        w1b_ref[...] = w1_ref[...].astype(BF16)
        w3b_ref[...] = w3_ref[...].astype(BF16)
        w2b_ref[...] = w2_ref[...].astype(BF16)

    @pl.when(valid)
    def _():
        half = xs_ref.shape[1]
        x_lo, x_hi = _unpack_bf16_pair(xs_ref[...])
        x_lo = x_lo.astype(BF16)
        x_hi = x_hi.astype(BF16)

        def up(w_ref):
            return (jnp.dot(x_lo, w_ref[:half, :], preferred_element_type=F32)
                    + jnp.dot(x_hi, w_ref[half:, :], preferred_element_type=F32))

        a = up(w1b_ref)
        b = up(w3b_ref)
        hid = (a * jax.nn.sigmoid(a) * b).astype(BF16)
        y = jnp.dot(hid, w2b_ref[...], preferred_element_type=F32)
        ys_ref[...] = _pack_bf16_pair(y[:, :half], y[:, half:])

    @pl.when(jnp.logical_not(valid))
    def _():
        ys_ref[...] = jnp.zeros_like(ys_ref)


def _experts(xs, blk_e, nvalid, w1, w3, w2, layer, tb):
    P, half = xs.shape
    D = 2 * half
    n_blocks = P // tb
    rows = lambda i, be, nv: (jnp.minimum(i, nv[0] - 1), 0)
    grid_spec = pltpu.PrefetchScalarGridSpec(
        num_scalar_prefetch=2,
        grid=(n_blocks,),
        in_specs=[pl.BlockSpec((tb, half), rows),
                  pl.BlockSpec((None, None, D, D_EXPERT), lambda i, be, nv: (layer, be[i], 0, 0)),
                  pl.BlockSpec((None, None, D, D_EXPERT), lambda i, be, nv: (layer, be[i], 0, 0)),
                  pl.BlockSpec((None, None, D_EXPERT, D), lambda i, be, nv: (layer, be[i], 0, 0))],
        out_specs=pl.BlockSpec((tb, half), lambda i, be, nv: (i, 0)),
        scratch_shapes=[pltpu.VMEM((D, D_EXPERT), BF16), pltpu.VMEM((D, D_EXPERT), BF16),
                        pltpu.VMEM((D_EXPERT, D), BF16)],
    )
    return pl.pallas_call(
        _expert_kernel,
        grid_spec=grid_spec,
        out_shape=jax.ShapeDtypeStruct((P, half), U32),
        compiler_params=_cparams(("arbitrary",)),
        name="moe_experts",
    )(blk_e, nvalid, xs, w1, w3, w2)


def _sc_gather(data, idx):
    M = idx.shape[0]
    D = data.shape[1]
    W = SC_GATHER_WINDOW
    assert M % W == 0, (M, W)
    mesh = plsc.VectorSubcoreMesh(core_axis_name="core", subcore_axis_name="subcore")
    n_workers = mesh.num_cores * mesh.num_subcores
    assert M % (W * n_workers) == 0, (M, W, n_workers)

    @functools.partial(pl.kernel, out_type=jax.ShapeDtypeStruct((M, D), data.dtype), mesh=mesh,
                       scratch_types=[pltpu.VMEM((W,), I32), pltpu.VMEM((W, D), data.dtype)])
    def gather_kernel(x_hbm, i_hbm, o_hbm, i_vmem, buf):
        worker = lax.axis_index("core") * mesh.num_subcores + lax.axis_index("subcore")

        @pl.loop(0, M // (W * n_workers))
        def _(t):
            start = (t * n_workers + worker) * W
            pltpu.sync_copy(i_hbm.at[pl.ds(start, W)], i_vmem)
            pltpu.sync_copy(x_hbm.at[i_vmem], buf)
            pltpu.sync_copy(buf, o_hbm.at[pl.ds(start, W)])

    return gather_kernel(data, idx)


def _sc_dispatch(h2, dest, fill_idx, P):
    T, D = h2.shape
    W = SC_GATHER_WINDOW
    n_fill = fill_idx.shape[0]
    mesh = plsc.VectorSubcoreMesh(core_axis_name="core", subcore_axis_name="subcore")
    n_workers = mesh.num_cores * mesh.num_subcores
    assert T % (W * n_workers) == 0 and n_fill % (W * n_workers) == 0, (T, n_fill, W, n_workers)
    zeros = jnp.zeros((W, D), h2.dtype)

    @functools.partial(pl.kernel, out_type=jax.ShapeDtypeStruct((P, D), h2.dtype), mesh=mesh,
                       scratch_types=[pltpu.VMEM((W,), I32), pltpu.VMEM((W, D), h2.dtype)])
    def dispatch_kernel(h_hbm, d_hbm, f_hbm, z_hbm, o_hbm, i_vmem, buf):
        worker = lax.axis_index("core") * mesh.num_subcores + lax.axis_index("subcore")

        @pl.loop(0, T // (W * n_workers))
        def _(t):
            start = (t * n_workers + worker) * W
            pltpu.sync_copy(h_hbm.at[pl.ds(start, W)], buf)
            for c in range(2):
                pltpu.sync_copy(d_hbm.at[pl.ds(c * T + start, W)], i_vmem)
                pltpu.sync_copy(buf, o_hbm.at[i_vmem])

        pltpu.sync_copy(z_hbm, buf)

        @pl.loop(0, n_fill // (W * n_workers))
        def _(t):
            start = (t * n_workers + worker) * W
            pltpu.sync_copy(f_hbm.at[pl.ds(start, W)], i_vmem)
            pltpu.sync_copy(buf, o_hbm.at[i_vmem])

    return dispatch_kernel(h2, dest, fill_idx, zeros)


def _combine_kernel(x_ref, g0_ref, g1_ref, mf_ref, gt_ref, fg_ref, *rest, final):
    o_ref = rest[-1]
    mf = mf_ref[...]
    lo0, hi0 = _unpack_bf16_pair(g0_ref[...])
    lo1, hi1 = _unpack_bf16_pair(g1_ref[...])
    w0 = mf[:, 0:1]
    w1 = mf[:, 1:2]
    ffn = jnp.concatenate([lo0 * w0 + lo1 * w1, hi0 * w0 + hi1 * w1], axis=1)
    out = x_ref[...] + gt_ref[...] * ffn
    if final:
        out = _rms(out, fg_ref[...])
    o_ref[...] = out


def _combine(x1, S, mf, gt2, final_g, g, final, out_rows=None, out_tile0=0, out_prev=None):
    T, D = x1.shape
    tm = _pick(S, 512)
    per_b = S // tm
    nt = T // tm
    in_specs = [pl.BlockSpec((tm, D), lambda i: (i, 0)),
                pl.BlockSpec((tm, D // 2), lambda i: (i, 0)),
                pl.BlockSpec((tm, D // 2), lambda i: (i + nt, 0)),
                pl.BlockSpec((tm, 2), lambda i: (i, 0)),
                pl.BlockSpec((None, 1, D), lambda i: (i // per_b, 0, 0)),
                pl.BlockSpec((1, D), lambda i: (0, 0))]
    args = [x1, g, g, mf, gt2, final_g]
    aliases = {}
    if out_prev is not None:
        in_specs.append(pl.BlockSpec(memory_space=pl.ANY))
        args.append(out_prev)
        aliases = {len(args) - 1: 0}
    return pl.pallas_call(
        functools.partial(_combine_kernel, final=final),
        grid=(nt,),
        in_specs=in_specs,
        out_specs=pl.BlockSpec((tm, D), lambda i: (i + out_tile0, 0)),
        out_shape=jax.ShapeDtypeStruct((T if out_rows is None else out_rows, D), F32),
        input_output_aliases=aliases,
        compiler_params=_cparams(("arbitrary",)),
        name="moe_combine",
    )(*args)


def _prep_w_in(w):
    offs = [0]
    for s in (256, 256, 512, 512, 512, 512, 512, FOX_HEADS, MLA_Q_RANK, MLA_KV_RANK, MLA_ROPE):
        offs.append(offs[-1] + s)
    rq, rk, rv, rg, fq, fk, fv, ff, mq, mkv, mkr = [w[:, offs[i]:offs[i + 1]] for i in range(11)]
    pad = jnp.zeros((w.shape[0], LANES - MLA_ROPE - FOX_HEADS), w.dtype)
    return jnp.concatenate([rq, rk, rv, rg, fq, fk, fv, mq, mkv, mkr, ff, pad], axis=1).astype(BF16)


def _prep_wq_up(w):
    r = w.reshape(MLA_Q_RANK, MLA_HEADS, MLA_NOPE + MLA_ROPE)
    r = jnp.pad(r, ((0, 0), (0, 0), (0, MLA_DQ - MLA_NOPE - MLA_ROPE)))
    return r.reshape(MLA_Q_RANK, MLA_HEADS * MLA_DQ).astype(BF16)


def _prep_router(w_grp, b_grp, w_exp, b_exp):
    D = w_grp.shape[0]
    pad = LANES - N_EXPERTS - N_GROUPS
    rwt = jnp.concatenate([w_exp, w_grp, jnp.zeros((D, pad), F32)], axis=1).astype(F32).T
    rwh = rwt.astype(BF16)
    rwl = (rwt - rwh.astype(F32)).astype(BF16)
    rb = jnp.concatenate([b_exp, b_grp, jnp.zeros((pad,), F32)]).astype(F32).reshape(LANES, 1)
    return rwh, rwl, rb


def kernel(x, c, positions, ada_w, ada_b, norm1_g, norm2_g, w_in, fox_fb, mla_q_norm_g, mla_wq_up, mla_kv_norm_g, mla_wkv_up, gate_w, gate_b, branch_w, out_w, router_grp_w, router_grp_b, router_exp_w, router_exp_b, exp_w1, exp_w3, exp_w2, final_g):
    B, S, D = x.shape
    L = ada_w.shape[0]
    T = B * S
    n_str = N_STREAMS if B % N_STREAMS == 0 else 1
    Bs = B // n_str
    Ts = Bs * S
    As = 2 * Ts
    tb = _pick(As, 512)
    n_blocks = As // tb + N_EXPERTS
    P = n_blocks * tb
    tiles_per_stream = Ts // _pick(S, 512)

    mod = _adaln(c, ada_w, ada_b)
    cos_t, sin_t = _rope_tables(positions)
    cos_s = [cos_t[h * Ts:(h + 1) * Ts] for h in range(n_str)]
    sin_s = [sin_t[h * Ts:(h + 1) * Ts] for h in range(n_str)]
    x_full = x.reshape(T, D)
    xs2 = [x_full] * n_str
    x_tile0 = [h * tiles_per_stream for h in range(n_str)]
    final_g2 = final_g.reshape(1, D)
    r3 = lambda a: a.reshape(Bs, S, -1)
    out = None

    for l in range(L):
        mods = [[mod[l, h * Bs:(h + 1) * Bs, i * D:(i + 1) * D].reshape(Bs, 1, D) for i in range(6)]
                for h in range(n_str)]
        g1 = norm1_g[l].reshape(1, D)
        g2 = norm2_g[l].reshape(1, D)
        w_all = _prep_w_in(w_in[l])
        wq = _prep_wq_up(mla_wq_up[l])
        wkv = mla_wkv_up[l].astype(BF16)
        gq = mla_q_norm_g[l].reshape(1, -1)
        gkv = mla_kv_norm_g[l].reshape(1, -1)
        rwh, rwl, rb = _prep_router(router_grp_w[l], router_grp_b[l], router_exp_w[l], router_exp_b[l])
        gw = gate_w[l].astype(BF16)
        gb = gate_b[l].reshape(1, -1)
        bw = branch_w[l].astype(BF16)
        ow = out_w[l].astype(BF16)
        last = l == L - 1

        merged = []
        for h in range(n_str):
            sh1, sc1, gt1, sh2, sc2, gt2 = mods[h]
            (rq, rk, rv, rg, fq, fk, fv, mq, mk, kpe, mv, ffp) = _proj(
                xs2[h], x_tile0[h], S, g1, sh1, sc1, w_all, cos_s[h], sin_s[h], gq, wq, gkv, wkv)
            ya = _retention(rq, rk, rv, rg, Bs, S).reshape(Ts, -1)
            kb = _fox_gate(ffp, fox_fb[l], S)
            yb = _flash(r3(fq), r3(fk), r3(kb), r3(fv), Bs, S, FOX_HEADS, FOX_DH, True, 3,
                        "flash_fox").reshape(Ts, -1)
            yc = _flash(r3(mq), r3(mk), r3(kpe), r3(mv), Bs, S, MLA_HEADS, MLA_DQ, False, 0,
                        "flash_mla").reshape(Ts, -1)
            merged.append(_merge(xs2[h], x_tile0[h], S, ya, yb, yc, g1, sh1, sc1, gt1, g2, sh2, sc2,
                                 gw, gb, bw, ow, rwh, rwl, rb))

        routed = []
        for h in range(n_str):
            x1, h2, mi, mf, cnt = merged[h]
            counts = cnt[:, 0].astype(I32)
            pcounts = (counts + tb - 1) // tb * tb
            pends = jnp.cumsum(pcounts)
            pstarts = pends - pcounts
            sel = mi[0:2, :, None] == jnp.arange(N_EXPERTS, dtype=I32)
            dest = (jnp.sum(jnp.where(sel, pstarts, 0), axis=-1) + mi[2:4]).reshape(As)
            blk_pos = jnp.arange(n_blocks, dtype=I32) * tb
            blk_e = jnp.minimum(jnp.sum((pends[None, :] <= blk_pos[:, None]).astype(I32), axis=1),
                                N_EXPERTS - 1)
            nvalid = (pends[-1:] // tb).astype(I32)
            fr = jnp.arange(tb, dtype=I32)[None, :]
            is_pad = (fr < (pcounts - counts)[:, None]).reshape(-1)
            pad_slot = ((pstarts + counts)[:, None] + fr).reshape(-1)
            tail_rank = jnp.cumsum(jnp.logical_not(is_pad).astype(I32)) - 1
            fill_idx = jnp.where(is_pad, pad_slot, pends[-1] + tail_rank)
            routed.append((dest, blk_e, nvalid, _sc_dispatch(h2, dest, fill_idx, P)))

        ys = [_experts(routed[h][3], routed[h][1], routed[h][2], exp_w1, exp_w3, exp_w2, l, tb)
              for h in range(n_str)]
        gathered = [_sc_gather(ys[h], routed[h][0]) for h in range(n_str)]
        for h in range(n_str):
            x1, _, _, mf, _ = merged[h]
            gt2 = mods[h][5]
            if last:
                out = _combine(x1, S, mf[0:2].T, gt2, final_g2, gathered[h], True,
                               out_rows=T, out_tile0=h * tiles_per_stream, out_prev=out)
            else:
                xs2[h] = _combine(x1, S, mf[0:2].T, gt2, final_g2, gathered[h], False)
        x_tile0 = [0] * n_str

    return out.reshape(B, S, D)
```

```python
import functools
import math

import jax
import jax.numpy as jnp
from jax import lax
from jax.experimental import pallas as pl
from jax.experimental.pallas import tpu as pltpu
from jax.experimental.pallas import tpu_sc as plsc

F32 = jnp.float32
BF16 = jnp.bfloat16
I32 = jnp.int32
U32 = jnp.uint32
HIGHEST = lax.Precision.HIGHEST

EPS = 1e-6
ROPE_THETA = 10000.0
RET_HEADS = 4
RET_DK = 64
RET_DV = 128
RET_CHUNK = 128
FOX_HEADS = 4
FOX_DH = 128
MLA_HEADS = 4
MLA_Q_RANK = 256
MLA_KV_RANK = 128
MLA_NOPE = 128
MLA_ROPE = 64
MLA_V = 128
MLA_DQ = 256
N_BRANCH = 3
BRANCH_W = 512
N_GROUPS = 4
EXP_PER_GROUP = 8
N_EXPERTS = N_GROUPS * EXP_PER_GROUP
D_EXPERT = 512

LANES = 128
V7X_VMEM_LIMIT = 56 * 1024 * 1024

C_RQ, C_RK, C_RV, C_RG = 0, 256, 512, 1024
C_FQ, C_FK, C_FV = 1536, 2048, 2560
C_MQ, C_MKV, C_TAIL = 3072, 3328, 3456
D_IN_PAD = 3584
FF_LANE = MLA_ROPE

NEG_BIG = -1e30
SC_GATHER_WINDOW = 128
N_STREAMS = 1
FLASH_UNROLL = 8
MERGE_COL_CHUNK = 256
LOG2E = math.log2(math.e)


def _cparams(sem):
    return pltpu.CompilerParams(dimension_semantics=sem, vmem_limit_bytes=V7X_VMEM_LIMIT)


def _pick(n, pref):
    t = min(n, pref)
    assert n % t == 0, (n, t)
    return t


def _adaln_kernel(c_ref, w_ref, b_ref, o_ref):
    c = c_ref[...]
    ca = c * jax.nn.sigmoid(c)
    o_ref[...] = jnp.dot(ca, w_ref[...], preferred_element_type=F32, precision=HIGHEST) + b_ref[...]


def _adaln(c, ada_w, ada_b):
    L, D, N = ada_w.shape
    B = c.shape[0]
    tn = _pick(N, 1536)
    return pl.pallas_call(
        _adaln_kernel,
        grid=(L, N // tn),
        in_specs=[pl.BlockSpec((B, D), lambda l, j: (0, 0)),
                  pl.BlockSpec((None, D, tn), lambda l, j: (l, 0, j)),
                  pl.BlockSpec((None, 1, tn), lambda l, j: (l, 0, j))],
        out_specs=pl.BlockSpec((None, B, tn), lambda l, j: (l, 0, j)),
        out_shape=jax.ShapeDtypeStruct((L, B, N), F32),
        compiler_params=_cparams(("arbitrary", "arbitrary")),
        name="adaln",
    )(c, ada_w, ada_b.reshape(L, 1, N))


def _rope_table_kernel(pos_ref, inv_ref, sign_ref, cos_ref, sin_ref):
    ang = pos_ref[...].astype(F32) * inv_ref[...]
    cos_ref[...] = jnp.cos(ang)
    sin_ref[...] = jnp.sin(ang) * sign_ref[...]


def _rope_tables(positions):
    T = positions.size
    tm = _pick(T, 2048)
    half = MLA_ROPE // 2
    inv = ROPE_THETA ** (-jnp.arange(0, MLA_ROPE, 2, dtype=F32) / MLA_ROPE)
    inv_t = jnp.tile(inv, LANES // half).reshape(1, LANES)
    sign = jnp.where((jnp.arange(LANES) % MLA_ROPE) < half, -1.0, 1.0).astype(F32).reshape(1, LANES)
    return pl.pallas_call(
        _rope_table_kernel,
        grid=(T // tm,),
        in_specs=[pl.BlockSpec((tm, 1), lambda i: (i, 0)),
                  pl.BlockSpec((1, LANES), lambda i: (0, 0)),
                  pl.BlockSpec((1, LANES), lambda i: (0, 0))],
        out_specs=[pl.BlockSpec((tm, LANES), lambda i: (i, 0))] * 2,
        out_shape=[jax.ShapeDtypeStruct((T, LANES), F32)] * 2,
        compiler_params=_cparams(("arbitrary",)),
        name="rope_tables",
    )(positions.reshape(T, 1), inv_t, sign)


def _rope_slab(x, cos_t, sin_t, lane):
    nxt = pltpu.roll(x, LANES - 32, axis=1)
    prv = pltpu.roll(x, 32, axis=1)
    swapped = jnp.where((lane & 32) == 0, nxt, prv)
    return x * cos_t + swapped * sin_t


def _pack_bf16_pair(lo, hi):
    lo_bits = lax.shift_right_logical(lax.bitcast_convert_type(lo.astype(BF16).astype(F32), U32), jnp.uint32(16))
    hi_bits = lax.bitcast_convert_type(hi.astype(BF16).astype(F32), U32) & jnp.uint32(0xFFFF0000)
    return hi_bits | lo_bits


def _unpack_bf16_pair(w):
    lo = lax.bitcast_convert_type(lax.shift_left(w, jnp.uint32(16)), F32)
    hi = lax.bitcast_convert_type(w & jnp.uint32(0xFFFF0000), F32)
    return lo, hi


def _norm_mod(x, g, shift, scale):
    y = x * lax.rsqrt(jnp.mean(x * x, axis=-1, keepdims=True) + EPS)
    return (y * g) * (1.0 + scale) + shift


def _rms(x, g):
    return x * lax.rsqrt(jnp.mean(x * x, axis=-1, keepdims=True) + EPS) * g


def _proj_kernel(x_ref, g_ref, sh_ref, sc_ref, w_ref, cos_ref, sin_ref,
                 gq_ref, wq_ref, gkv_ref, wkv_ref,
                 rq_ref, rk_ref, rv_ref, rg_ref, fq_ref, fk_ref, fv_ref,
                 mq_ref, mk_ref, kpe_ref, mv_ref, ff_ref):
    h = _norm_mod(x_ref[...], g_ref[...], sh_ref[...], sc_ref[...]).astype(BF16)
    cos_t = cos_ref[...]
    sin_t = sin_ref[...]
    lane = lax.broadcasted_iota(I32, cos_t.shape, 1)

    def proj(c0, width):
        return jnp.dot(h, w_ref[:, c0:c0 + width], preferred_element_type=F32)

    rq = proj(C_RQ, 256)
    rk = proj(C_RK, 256)
    for s in range(2):
        sl = slice(s * LANES, (s + 1) * LANES)
        q2 = _rope_slab(rq[:, sl], cos_t, sin_t, lane)
        k2 = _rope_slab(rk[:, sl], cos_t, sin_t, lane) * (RET_DK ** -0.5)
        for half in range(2):
            mine = (lane < RET_DK) if half == 0 else (lane >= RET_DK)
            hs = slice((2 * s + half) * LANES, (2 * s + half + 1) * LANES)
            rq_ref[:, hs] = jnp.where(mine, q2, 0.0).astype(BF16)
            rk_ref[:, hs] = jnp.where(mine, k2, 0.0).astype(BF16)
    rv_ref[...] = proj(C_RV, 512).astype(BF16)
    rg_ref[...] = proj(C_RG, 512).astype(BF16)
    fq_ref[...] = (proj(C_FQ, 512) * (FOX_DH ** -0.5 * LOG2E)).astype(BF16)
    fk_ref[...] = proj(C_FK, 512).astype(BF16)
    fv_ref[...] = proj(C_FV, 512).astype(BF16)

    tail = proj(C_TAIL, LANES)
    ff_ref[...] = tail
    kpe_ref[...] = jnp.where(lane < MLA_ROPE, _rope_slab(tail, cos_t, sin_t, lane), 0.0).astype(BF16)

    qn = _rms(proj(C_MQ, MLA_Q_RANK), gq_ref[...]).astype(BF16)
    qh = jnp.dot(qn, wq_ref[...], preferred_element_type=F32)
    q_scale = (MLA_NOPE + MLA_ROPE) ** -0.5 * LOG2E
    for hd in range(MLA_HEADS):
        c0 = hd * MLA_DQ
        mq_ref[:, c0:c0 + LANES] = (qh[:, c0:c0 + LANES] * q_scale).astype(BF16)
        pe = _rope_slab(qh[:, c0 + LANES:c0 + 2 * LANES], cos_t, sin_t, lane)
        mq_ref[:, c0 + LANES:c0 + 2 * LANES] = jnp.where(lane < MLA_ROPE, pe * q_scale, 0.0).astype(BF16)

    kvn = _rms(proj(C_MKV, MLA_KV_RANK), gkv_ref[...]).astype(BF16)
    kvh = jnp.dot(kvn, wkv_ref[...], preferred_element_type=F32)
    for hd in range(MLA_HEADS):
        c0 = hd * (MLA_NOPE + MLA_V)
        mk_ref[:, hd * MLA_NOPE:(hd + 1) * MLA_NOPE] = kvh[:, c0:c0 + MLA_NOPE].astype(BF16)
        mv_ref[:, hd * MLA_V:(hd + 1) * MLA_V] = kvh[:, c0 + MLA_NOPE:c0 + MLA_NOPE + MLA_V].astype(BF16)


def _proj(x2, x_tile0, S, g, sh, sc, w_all, cos_t, sin_t, gq, wq, gkv, wkv):
    D = x2.shape[1]
    T = cos_t.shape[0]
    tm = _pick(S, 512)
    per_b = S // tm
    row = lambda i: (i, 0)
    xrow = lambda i: (i + x_tile0, 0)
    const = lambda i: (0, 0)
    batch = lambda i: (i // per_b, 0, 0)
    widths = [512, 512, 512, 512, 512, 512, 512, MLA_HEADS * MLA_DQ, MLA_HEADS * MLA_NOPE, LANES,
              MLA_HEADS * MLA_V]
    out_shape = [jax.ShapeDtypeStruct((T, w), BF16) for w in widths]
    out_shape.append(jax.ShapeDtypeStruct((T, LANES), F32))
    out_specs = [pl.BlockSpec((tm, w), row) for w in widths] + [pl.BlockSpec((tm, LANES), row)]
    return pl.pallas_call(
        _proj_kernel,
        grid=(T // tm,),
        in_specs=[pl.BlockSpec((tm, D), xrow),
                  pl.BlockSpec((1, D), const),
                  pl.BlockSpec((None, 1, D), batch),
                  pl.BlockSpec((None, 1, D), batch),
                  pl.BlockSpec((D, D_IN_PAD), const),
                  pl.BlockSpec((tm, LANES), row),
                  pl.BlockSpec((tm, LANES), row),
                  pl.BlockSpec((1, MLA_Q_RANK), const),
                  pl.BlockSpec((MLA_Q_RANK, MLA_HEADS * MLA_DQ), const),
                  pl.BlockSpec((1, MLA_KV_RANK), const),
                  pl.BlockSpec((MLA_KV_RANK, MLA_HEADS * (MLA_NOPE + MLA_V)), const)],
        out_specs=out_specs,
        out_shape=out_shape,
        compiler_params=_cparams(("arbitrary",)),
        name="proj",
    )(x2, g, sh, sc, w_all, cos_t, sin_t, gq, wq, gkv, wkv)


def _split3(x):
    a = x.astype(BF16)
    r = x - a.astype(F32)
    b = r.astype(BF16)
    c = (r - b.astype(F32)).astype(BF16)
    return a, b, c


def _fox_gate_kernel(ff_ref, fb_ref, tri_ref, kb_ref, carry_ref, *, tiles_per_seq):
    @pl.when(pl.program_id(0) % tiles_per_seq == 0)
    def _():
        carry_ref[...] = jnp.zeros_like(carry_ref)

    tm = ff_ref.shape[0]
    lane = lax.broadcasted_iota(I32, (tm, LANES), 1)
    z = ff_ref[...] + fb_ref[...]
    ls = -(jnp.maximum(-z, 0.0) + jnp.log1p(jnp.exp(-jnp.abs(z))))
    ls = jnp.where((lane >= FF_LANE) & (lane < FF_LANE + FOX_HEADS), ls, 0.0)
    tri = tri_ref[...]
    f = carry_ref[...]
    for part in _split3(ls):
        f = f + jnp.dot(tri, part, preferred_element_type=F32)
    carry_ref[...] = f[tm - 1:tm, :]
    hi, mid, lo = [part.astype(F32) for part in _split3(f * LOG2E)]
    for hd in range(FOX_HEADS):
        src = FF_LANE + hd
        slab = jnp.where(lane == 0, pltpu.roll(hi, (LANES - src) % LANES, axis=1),
               jnp.where(lane == 1, pltpu.roll(mid, (LANES + 1 - src) % LANES, axis=1),
               jnp.where(lane == 2, pltpu.roll(lo, (LANES + 2 - src) % LANES, axis=1), 0.0)))
        kb_ref[:, hd * LANES:(hd + 1) * LANES] = slab.astype(BF16)


def _fox_gate(ffp, fb, S):
    T = ffp.shape[0]
    tm = _pick(S, 512)
    idx = jnp.arange(tm)
    tri = (idx[None, :] <= idx[:, None]).astype(BF16)
    fbv = jnp.zeros((1, LANES), F32).at[0, FF_LANE:FF_LANE + FOX_HEADS].set(fb)
    return pl.pallas_call(
        functools.partial(_fox_gate_kernel, tiles_per_seq=S // tm),
        grid=(T // tm,),
        in_specs=[pl.BlockSpec((tm, LANES), lambda i: (i, 0)),
                  pl.BlockSpec((1, LANES), lambda i: (0, 0)),
                  pl.BlockSpec((tm, tm), lambda i: (0, 0))],
        out_specs=pl.BlockSpec((tm, FOX_HEADS * LANES), lambda i: (i, 0)),
        out_shape=jax.ShapeDtypeStruct((T, FOX_HEADS * LANES), BF16),
        scratch_shapes=[pltpu.VMEM((1, LANES), F32)],
        compiler_params=_cparams(("arbitrary",)),
        name="fox_gate",
    )(ffp, fbv, tri)


def _retention_kernel(dchunk_ref, q_ref, k_ref, v_ref, g_ref, dmask_ref, din_ref, dout_ref,
                      o_ref, state_ref, *, n_chunks):
    @pl.when(pl.program_id(1) == 0)
    def _():
        state_ref[...] = jnp.zeros_like(state_ref)

    C = RET_CHUNK
    for ci in range(n_chunks):
        rows = slice(ci * C, (ci + 1) * C)
        for hd in range(RET_HEADS):
            slab = slice(hd * LANES, (hd + 1) * LANES)
            q = q_ref[rows, slab]
            k = k_ref[rows, slab]
            vcols = slice(hd * RET_DV, (hd + 1) * RET_DV)
            v = v_ref[rows, vcols]
            state = state_ref[hd]
            scores = lax.dot_general(q, k, (((1,), (1,)), ((), ())),
                                     preferred_element_type=F32) * dmask_ref[hd]
            inner = jnp.dot(scores.astype(BF16), v, preferred_element_type=F32)
            cross = jnp.dot(q, state.astype(BF16), preferred_element_type=F32) * din_ref[hd]
            o = inner + cross
            vd = (v.astype(F32) * dout_ref[hd]).astype(BF16)
            kv = lax.dot_general(k, vd, (((0,), (0,)), ((), ())), preferred_element_type=F32)
            state_ref[hd] = state * dchunk_ref[hd] + kv
            mu = jnp.mean(o, axis=-1, keepdims=True)
            d = o - mu
            var = jnp.mean(d * d, axis=-1, keepdims=True)
            on = d * lax.rsqrt(var + EPS)
            g = g_ref[rows, vcols].astype(F32)
            o_ref[rows, vcols] = (g * jax.nn.sigmoid(g) * on).astype(BF16)


def _retention(rq, rk, rv, rg, B, S):
    H, C = RET_HEADS, RET_CHUNK
    tr = _pick(S, 4 * C)
    n_chunks = tr // C
    log_gamma = jnp.log1p(-jnp.exp2(-5.0 - jnp.arange(H, dtype=F32)))
    idx = jnp.arange(C, dtype=F32)
    rel = idx[:, None] - idx[None, :]
    dmask = jnp.where(rel >= 0, jnp.exp(log_gamma[:, None, None] * jnp.maximum(rel, 0.0)), 0.0)
    decay_in = jnp.exp(log_gamma[:, None] * (idx + 1.0))
    decay_out = jnp.exp(log_gamma[:, None] * (C - 1.0 - idx))
    decay_chunk = jnp.exp(log_gamma * C)
    din = jnp.broadcast_to(decay_in[:, :, None], (H, C, RET_DV))
    dout = jnp.broadcast_to(decay_out[:, :, None], (H, C, RET_DV))
    tok = lambda b, i: (b, i, 0)
    const3 = lambda b, i: (0, 0, 0)
    return pl.pallas_call(
        functools.partial(_retention_kernel, n_chunks=n_chunks),
        grid=(B, S // tr),
        in_specs=[pl.BlockSpec(memory_space=pltpu.SMEM),
                  pl.BlockSpec((None, tr, H * LANES), tok),
                  pl.BlockSpec((None, tr, H * LANES), tok),
                  pl.BlockSpec((None, tr, H * RET_DV), tok),
                  pl.BlockSpec((None, tr, H * RET_DV), tok),
                  pl.BlockSpec((H, C, C), const3),
                  pl.BlockSpec((H, C, RET_DV), const3),
                  pl.BlockSpec((H, C, RET_DV), const3)],
        out_specs=pl.BlockSpec((None, tr, H * RET_DV), tok),
        out_shape=jax.ShapeDtypeStruct((B, S, H * RET_DV), BF16),
        scratch_shapes=[pltpu.VMEM((H, LANES, RET_DV), F32)],
        compiler_params=_cparams(("arbitrary", "arbitrary")),
        name="retention",
    )(decay_chunk, rq.reshape(B, S, -1), rk.reshape(B, S, -1), rv.reshape(B, S, -1),
      rg.reshape(B, S, -1), dmask, din, dout)


def _flash_kernel(q_ref, k_ref, ke_ref, v_ref, o_ref, m_ref, l_ref, acc_ref, s_ref, *, q_bias_cols):
    qi = pl.program_id(2)
    tq = q_ref.shape[0]
    tk = tq
    m_ref[...] = jnp.full(m_ref.shape, NEG_BIG, F32)
    l_ref[...] = jnp.zeros_like(l_ref)
    acc_ref[...] = jnp.zeros_like(acc_ref)
    q = q_ref[...]
    if q_bias_cols:
        lane = lax.broadcasted_iota(I32, (tq, LANES), 1)
        q = jnp.concatenate([q, jnp.where(lane < q_bias_cols, -1.0, 0.0).astype(BF16)], axis=1)

    def scores(j):
        start = pl.multiple_of(j * tk, tk)
        kj = jnp.concatenate([k_ref[pl.ds(start, tk), :], ke_ref[pl.ds(start, tk), :]], axis=1)
        return lax.dot_general(kj, q, (((1,), (1,)), ((), ())), preferred_element_type=F32)

    def update(s, j):
        m_prev = m_ref[...]
        m_new = jnp.maximum(m_prev, jnp.max(s, axis=0, keepdims=True))
        alpha = jnp.exp2(m_prev - m_new)
        p = jnp.exp2(s - m_new)
        l_ref[...] = alpha * l_ref[...] + jnp.sum(p, axis=0, keepdims=True)
        start = pl.multiple_of(j * tk, tk)
        vj = v_ref[pl.ds(start, tk), :]
        acc_ref[...] = alpha * acc_ref[...] + lax.dot_general(
            vj, p.astype(BF16), (((0,), (0,)), ((), ())), preferred_element_type=F32)
        m_ref[...] = m_new

    def causal(s):
        key = lax.broadcasted_iota(I32, s.shape, 0)
        qry = lax.broadcasted_iota(I32, s.shape, 1)
        return jnp.where(key <= qry, s, NEG_BIG)

    def run(n_steps, base):
        for t in range(n_steps):
            s = s_ref[t % 2]
            s_ref[(t + 1) % 2] = scores(base + t + 1)
            update(s, base + t)

    s_ref[0] = scores(0)

    def unrolled(p, carry):
        run(FLASH_UNROLL, FLASH_UNROLL * p)
        return carry

    lax.fori_loop(0, qi // FLASH_UNROLL, unrolled, 0)

    rem = qi % FLASH_UNROLL
    for r in range(FLASH_UNROLL):
        @pl.when(rem == r)
        def _(r=r):
            run(r, qi - r)
            update(causal(s_ref[r % 2]), qi)

    o_ref[...] = (acc_ref[...] / l_ref[...]).T.astype(o_ref.dtype)


def _flash(q, k, ke, v, B, S, H, dq, ke_per_head, q_bias_cols, name):
    tq = _pick(S, 512)
    nq = S // tq
    ke_map = (lambda b, h, i: (b, 0, h)) if ke_per_head else (lambda b, h, i: (b, 0, 0))
    return pl.pallas_call(
        functools.partial(_flash_kernel, q_bias_cols=q_bias_cols),
        grid=(B, H, nq),
        in_specs=[pl.BlockSpec((None, tq, dq), lambda b, h, i: (b, i, h)),
                  pl.BlockSpec((None, S, LANES), lambda b, h, i: (b, 0, h)),
                  pl.BlockSpec((None, S, LANES), ke_map),
                  pl.BlockSpec((None, S, LANES), lambda b, h, i: (b, 0, h))],
        out_specs=pl.BlockSpec((None, tq, LANES), lambda b, h, i: (b, i, h)),
        out_shape=jax.ShapeDtypeStruct((B, S, H * LANES), BF16),
        scratch_shapes=[pltpu.VMEM((1, tq), F32), pltpu.VMEM((1, tq), F32),
                        pltpu.VMEM((LANES, tq), F32), pltpu.VMEM((2, tq, tq), F32)],
        compiler_params=_cparams(("arbitrary", "arbitrary", "arbitrary")),
        name=name,
    )(q, k, ke, v)


def _merge_kernel(x_ref, ya_ref, yb_ref, yc_ref, g1_ref, sh1_ref, sc1_ref, gt1_ref,
                  g2_ref, sh2_ref, sc2_ref, gw_ref, gb_ref, bw_ref, ow_ref,
                  rwh_ref, rwl_ref, rb_ref, tri_ref,
                  x1_ref, h2_ref, mi_ref, mf_ref, cnt_ref, h_ref, mg_ref, carry_ref):
    tm, D = x_ref.shape
    cn = MERGE_COL_CHUNK

    @pl.when(pl.program_id(0) == 0)
    def _():
        carry_ref[...] = jnp.zeros_like(carry_ref)

    h_ref[...] = _norm_mod(x_ref[...], g1_ref[...], sh1_ref[...], sc1_ref[...]).astype(BF16)
    for n in range(D // cn):
        cols = slice(n * cn, (n + 1) * cn)
        merged = None
        for i, y_ref in enumerate((ya_ref, yb_ref, yc_ref)):
            gcols = slice(i * D + n * cn, i * D + (n + 1) * cn)
            gate = jax.nn.sigmoid(jnp.dot(h_ref[...], gw_ref[:, gcols], preferred_element_type=F32)
                                  + gb_ref[:, gcols])
            br = jnp.dot(y_ref[...], bw_ref[i, :, cols], preferred_element_type=F32)
            merged = gate * br if merged is None else merged + gate * br
        mg_ref[:, cols] = merged.astype(BF16)
    for n in range(D // cn):
        cols = slice(n * cn, (n + 1) * cn)
        mix = jnp.dot(mg_ref[...], ow_ref[:, cols], preferred_element_type=F32)
        x1_ref[:, cols] = x_ref[:, cols] + gt1_ref[:, cols] * mix
    h2 = _norm_mod(x1_ref[...], g2_ref[...], sh2_ref[...], sc2_ref[...])
    h2_ref[...] = _pack_bf16_pair(h2[:, :D // 2], h2[:, D // 2:])

    hh = h2.astype(BF16)
    hl = (h2 - hh.astype(F32)).astype(BF16)
    nt = lambda a, b: lax.dot_general(a, b, (((1,), (1,)), ((), ())), preferred_element_type=F32)
    lt = nt(rwh_ref[...], hh) + nt(rwh_ref[...], hl) + nt(rwl_ref[...], hh) + rb_ref[...]
    row8 = lax.broadcasted_iota(I32, (EXP_PER_GROUP, tm), 0)
    gl = jnp.where(row8 < N_GROUPS, lt[N_EXPERTS:N_EXPERTS + EXP_PER_GROUP, :], -jnp.inf)
    gmax = jnp.max(gl, axis=0, keepdims=True)
    g_idx = jnp.min(jnp.where(gl == gmax, row8, EXP_PER_GROUP), axis=0, keepdims=True)
    g_w = 1.0 / jnp.sum(jnp.exp(gl - gmax), axis=0, keepdims=True)
    el = lt[(N_GROUPS - 1) * EXP_PER_GROUP:N_EXPERTS, :]
    for g in range(N_GROUPS - 2, -1, -1):
        el = jnp.where(g_idx == g, lt[g * EXP_PER_GROUP:(g + 1) * EXP_PER_GROUP, :], el)
    e1 = jnp.max(el, axis=0, keepdims=True)
    i1 = jnp.min(jnp.where(el == e1, row8, EXP_PER_GROUP), axis=0, keepdims=True)
    el2 = jnp.where(row8 == i1, -jnp.inf, el)
    e2 = jnp.max(el2, axis=0, keepdims=True)
    i2 = jnp.min(jnp.where(el2 == e2, row8, EXP_PER_GROUP), axis=0, keepdims=True)
    r = jnp.exp(e2 - e1)
    w1 = g_w / (1.0 + r)
    w2 = g_w * r / (1.0 + r)
    eid1 = g_idx * EXP_PER_GROUP + i1
    eid2 = g_idx * EXP_PER_GROUP + i2

    rowe = lax.broadcasted_iota(I32, (N_EXPERTS, tm), 0)
    hit1 = rowe == eid1
    hit2 = rowe == eid2
    onehot = jnp.where(hit1 | hit2, 1.0, 0.0)
    before = (jnp.dot(onehot.astype(BF16), tri_ref[...], preferred_element_type=F32)
              + jnp.concatenate([carry_ref[...]] * (tm // LANES), axis=1))
    rank1 = jnp.sum(jnp.where(hit1, before, 0.0), axis=0, keepdims=True)
    rank2 = jnp.sum(jnp.where(hit2, before, 0.0), axis=0, keepdims=True)
    carry_ref[...] = carry_ref[...] + jnp.sum(onehot, axis=1, keepdims=True)
    cnt_ref[...] = carry_ref[...]

    mi_ref[...] = jnp.where(row8 == 0, eid1,
                  jnp.where(row8 == 1, eid2,
                  jnp.where(row8 == 2, rank1.astype(I32),
                  jnp.where(row8 == 3, rank2.astype(I32), 0))))
    mf_ref[...] = jnp.where(row8 == 0, w1, jnp.where(row8 == 1, w2, 0.0))


def _merge(x2, x_tile0, S, ya, yb, yc, g1, sh1, sc1, gt1, g2, sh2, sc2, gw, gb, bw, ow, rwh, rwl, rb):
    D = x2.shape[1]
    T = ya.shape[0]
    tm = _pick(S, 512)
    per_b = S // tm
    row = lambda i: (i, 0)
    xrow = lambda i: (i + x_tile0, 0)
    col = lambda i: (0, i)
    const = lambda i: (0, 0)
    batch = lambda i: (i // per_b, 0, 0)
    idx = jnp.arange(tm)
    tri = (idx[:, None] < idx[None, :]).astype(BF16)
    vecb = pl.BlockSpec((None, 1, D), batch)
    return pl.pallas_call(
        _merge_kernel,
        grid=(T // tm,),
        in_specs=[pl.BlockSpec((tm, D), xrow),
                  pl.BlockSpec((tm, BRANCH_W), row),
                  pl.BlockSpec((tm, BRANCH_W), row),
                  pl.BlockSpec((tm, BRANCH_W), row),
                  pl.BlockSpec((1, D), const), vecb, vecb, vecb,
                  pl.BlockSpec((1, D), const), vecb, vecb,
                  pl.BlockSpec((D, N_BRANCH * D), const),
                  pl.BlockSpec((1, N_BRANCH * D), const),
                  pl.BlockSpec((N_BRANCH, BRANCH_W, D), lambda i: (0, 0, 0)),
                  pl.BlockSpec((D, D), const),
                  pl.BlockSpec((LANES, D), const),
                  pl.BlockSpec((LANES, D), const),
                  pl.BlockSpec((LANES, 1), const),
                  pl.BlockSpec((tm, tm), const)],
        out_specs=[pl.BlockSpec((tm, D), row), pl.BlockSpec((tm, D // 2), row),
                   pl.BlockSpec((8, tm), col), pl.BlockSpec((8, tm), col),
                   pl.BlockSpec((N_EXPERTS, LANES), const)],
        out_shape=[jax.ShapeDtypeStruct((T, D), F32), jax.ShapeDtypeStruct((T, D // 2), U32),
                   jax.ShapeDtypeStruct((8, T), I32), jax.ShapeDtypeStruct((8, T), F32),
                   jax.ShapeDtypeStruct((N_EXPERTS, LANES), F32)],
        scratch_shapes=[pltpu.VMEM((tm, D), BF16), pltpu.VMEM((tm, D), BF16),
                        pltpu.VMEM((N_EXPERTS, LANES), F32)],
        compiler_params=_cparams(("arbitrary",)),
        name="merge",
    )(x2, ya, yb, yc, g1, sh1, sc1, gt1, g2, sh2, sc2, gw, gb, bw, ow, rwh, rwl, rb, tri)


def _expert_kernel(blk_e_ref, nvalid_ref, xs_ref, w1_ref, w3_ref, w2_ref, ys_ref, w1b_ref, w3b_ref, w2b_ref):
    i = pl.program_id(0)
    valid = i < nvalid_ref[0]
    new_expert = jnp.logical_or(i == 0, blk_e_ref[i] != blk_e_ref[jnp.maximum(i - 1, 0)])

    @pl.when(jnp.logical_and(valid, new_expert))
    def _():
        w1b_ref[...] = w1_ref[...].astype(BF16)
        w3b_ref[...] = w3_ref[...].astype(BF16)
        w2b_ref[...] = w2_ref[...].astype(BF16)

    @pl.when(valid)
    def _():
        half = xs_ref.shape[1]
        x_lo, x_hi = _unpack_bf16_pair(xs_ref[...])
        x_lo = x_lo.astype(BF16)
        x_hi = x_hi.astype(BF16)

        def up(w_ref):
            return (jnp.dot(x_lo, w_ref[:half, :], preferred_element_type=F32)
                    + jnp.dot(x_hi, w_ref[half:, :], preferred_element_type=F32))

        a = up(w1b_ref)
        b = up(w3b_ref)
        hid = (a * jax.nn.sigmoid(a) * b).astype(BF16)
        y = jnp.dot(hid, w2b_ref[...], preferred_element_type=F32)
        ys_ref[...] = _pack_bf16_pair(y[:, :half], y[:, half:])

    @pl.when(jnp.logical_not(valid))
    def _():
        ys_ref[...] = jnp.zeros_like(ys_ref)


def _experts(xs, blk_e, nvalid, w1, w3, w2, layer, tb):
    P, half = xs.shape
    D = 2 * half
    n_blocks = P // tb
    rows = lambda i, be, nv: (jnp.minimum(i, nv[0] - 1), 0)
    grid_spec = pltpu.PrefetchScalarGridSpec(
        num_scalar_prefetch=2,
        grid=(n_blocks,),
        in_specs=[pl.BlockSpec((tb, half), rows),
                  pl.BlockSpec((None, None, D, D_EXPERT), lambda i, be, nv: (layer, be[i], 0, 0)),
                  pl.BlockSpec((None, None, D, D_EXPERT), lambda i, be, nv: (layer, be[i], 0, 0)),
                  pl.BlockSpec((None, None, D_EXPERT, D), lambda i, be, nv: (layer, be[i], 0, 0))],
        out_specs=pl.BlockSpec((tb, half), lambda i, be, nv: (i, 0)),
        scratch_shapes=[pltpu.VMEM((D, D_EXPERT), BF16), pltpu.VMEM((D, D_EXPERT), BF16),
                        pltpu.VMEM((D_EXPERT, D), BF16)],
    )
    return pl.pallas_call(
        _expert_kernel,
        grid_spec=grid_spec,
        out_shape=jax.ShapeDtypeStruct((P, half), U32),
        compiler_params=_cparams(("arbitrary",)),
        name="moe_experts",
    )(blk_e, nvalid, xs, w1, w3, w2)


def _sc_gather(data, idx):
    M = idx.shape[0]
    D = data.shape[1]
    W = SC_GATHER_WINDOW
    assert M % W == 0, (M, W)
    mesh = plsc.VectorSubcoreMesh(core_axis_name="core", subcore_axis_name="subcore")
    n_workers = mesh.num_cores * mesh.num_subcores
    assert M % (W * n_workers) == 0, (M, W, n_workers)

    @functools.partial(pl.kernel, out_type=jax.ShapeDtypeStruct((M, D), data.dtype), mesh=mesh,
                       scratch_types=[pltpu.VMEM((W,), I32), pltpu.VMEM((W, D), data.dtype)])
    def gather_kernel(x_hbm, i_hbm, o_hbm, i_vmem, buf):
        worker = lax.axis_index("core") * mesh.num_subcores + lax.axis_index("subcore")

        @pl.loop(0, M // (W * n_workers))
        def _(t):
            start = (t * n_workers + worker) * W
            pltpu.sync_copy(i_hbm.at[pl.ds(start, W)], i_vmem)
            pltpu.sync_copy(x_hbm.at[i_vmem], buf)
            pltpu.sync_copy(buf, o_hbm.at[pl.ds(start, W)])

    return gather_kernel(data, idx)


def _sc_dispatch(h2, dest, fill_idx, P):
    T, D = h2.shape
    W = SC_GATHER_WINDOW
    n_fill = fill_idx.shape[0]
    mesh = plsc.VectorSubcoreMesh(core_axis_name="core", subcore_axis_name="subcore")
    n_workers = mesh.num_cores * mesh.num_subcores
    assert T % (W * n_workers) == 0 and n_fill % (W * n_workers) == 0, (T, n_fill, W, n_workers)
    zeros = jnp.zeros((W, D), h2.dtype)

    @functools.partial(pl.kernel, out_type=jax.ShapeDtypeStruct((P, D), h2.dtype), mesh=mesh,
                       scratch_types=[pltpu.VMEM((W,), I32), pltpu.VMEM((W, D), h2.dtype)])
    def dispatch_kernel(h_hbm, d_hbm, f_hbm, z_hbm, o_hbm, i_vmem, buf):
        worker = lax.axis_index("core") * mesh.num_subcores + lax.axis_index("subcore")

        @pl.loop(0, T // (W * n_workers))
        def _(t):
            start = (t * n_workers + worker) * W
            pltpu.sync_copy(h_hbm.at[pl.ds(start, W)], buf)
            for c in range(2):
                pltpu.sync_copy(d_hbm.at[pl.ds(c * T + start, W)], i_vmem)
                pltpu.sync_copy(buf, o_hbm.at[i_vmem])

        pltpu.sync_copy(z_hbm, buf)

        @pl.loop(0, n_fill // (W * n_workers))
        def _(t):
            start = (t * n_workers + worker) * W
            pltpu.sync_copy(f_hbm.at[pl.ds(start, W)], i_vmem)
            pltpu.sync_copy(buf, o_hbm.at[i_vmem])

    return dispatch_kernel(h2, dest, fill_idx, zeros)


def _combine_kernel(x_ref, g0_ref, g1_ref, mf_ref, gt_ref, fg_ref, *rest, final):
    o_ref = rest[-1]
    mf = mf_ref[...]
    lo0, hi0 = _unpack_bf16_pair(g0_ref[...])
    lo1, hi1 = _unpack_bf16_pair(g1_ref[...])
    w0 = mf[:, 0:1]
    w1 = mf[:, 1:2]
    ffn = jnp.concatenate([lo0 * w0 + lo1 * w1, hi0 * w0 + hi1 * w1], axis=1)
    out = x_ref[...] + gt_ref[...] * ffn
    if final:
        out = _rms(out, fg_ref[...])
    o_ref[...] = out


def _combine(x1, S, mf, gt2, final_g, g, final, out_rows=None, out_tile0=0, out_prev=None):
    T, D = x1.shape
    tm = _pick(S, 512)
    per_b = S // tm
    nt = T // tm
    in_specs = [pl.BlockSpec((tm, D), lambda i: (i, 0)),
                pl.BlockSpec((tm, D // 2), lambda i: (i, 0)),
                pl.BlockSpec((tm, D // 2), lambda i: (i + nt, 0)),
                pl.BlockSpec((tm, 2), lambda i: (i, 0)),
                pl.BlockSpec((None, 1, D), lambda i: (i // per_b, 0, 0)),
                pl.BlockSpec((1, D), lambda i: (0, 0))]
    args = [x1, g, g, mf, gt2, final_g]
    aliases = {}
    if out_prev is not None:
        in_specs.append(pl.BlockSpec(memory_space=pl.ANY))
        args.append(out_prev)
        aliases = {len(args) - 1: 0}
    return pl.pallas_call(
        functools.partial(_combine_kernel, final=final),
        grid=(nt,),
        in_specs=in_specs,
        out_specs=pl.BlockSpec((tm, D), lambda i: (i + out_tile0, 0)),
        out_shape=jax.ShapeDtypeStruct((T if out_rows is None else out_rows, D), F32),
        input_output_aliases=aliases,
        compiler_params=_cparams(("arbitrary",)),
        name="moe_combine",
    )(*args)


def _prep_w_in(w):
    offs = [0]
    for s in (256, 256, 512, 512, 512, 512, 512, FOX_HEADS, MLA_Q_RANK, MLA_KV_RANK, MLA_ROPE):
        offs.append(offs[-1] + s)
    rq, rk, rv, rg, fq, fk, fv, ff, mq, mkv, mkr = [w[:, offs[i]:offs[i + 1]] for i in range(11)]
    pad = jnp.zeros((w.shape[0], LANES - MLA_ROPE - FOX_HEADS), w.dtype)
    return jnp.concatenate([rq, rk, rv, rg, fq, fk, fv, mq, mkv, mkr, ff, pad], axis=1).astype(BF16)


def _prep_wq_up(w):
    r = w.reshape(MLA_Q_RANK, MLA_HEADS, MLA_NOPE + MLA_ROPE)
    r = jnp.pad(r, ((0, 0), (0, 0), (0, MLA_DQ - MLA_NOPE - MLA_ROPE)))
    return r.reshape(MLA_Q_RANK, MLA_HEADS * MLA_DQ).astype(BF16)


def _prep_router(w_grp, b_grp, w_exp, b_exp):
    D = w_grp.shape[0]
    pad = LANES - N_EXPERTS - N_GROUPS
    rwt = jnp.concatenate([w_exp, w_grp, jnp.zeros((D, pad), F32)], axis=1).astype(F32).T
    rwh = rwt.astype(BF16)
    rwl = (rwt - rwh.astype(F32)).astype(BF16)
    rb = jnp.concatenate([b_exp, b_grp, jnp.zeros((pad,), F32)]).astype(F32).reshape(LANES, 1)
    return rwh, rwl, rb


def kernel(x, c, positions, ada_w, ada_b, norm1_g, norm2_g, w_in, fox_fb, mla_q_norm_g, mla_wq_up, mla_kv_norm_g, mla_wkv_up, gate_w, gate_b, branch_w, out_w, router_grp_w, router_grp_b, router_exp_w, router_exp_b, exp_w1, exp_w3, exp_w2, final_g):
    B, S, D = x.shape
    L = ada_w.shape[0]
    T = B * S
    n_str = N_STREAMS if B % N_STREAMS == 0 else 1
    Bs = B // n_str
    Ts = Bs * S
    As = 2 * Ts
    tb = _pick(As, 512)
    n_blocks = As // tb + N_EXPERTS
    P = n_blocks * tb
    tiles_per_stream = Ts // _pick(S, 512)

    mod = _adaln(c, ada_w, ada_b)
    cos_t, sin_t = _rope_tables(positions)
    cos_s = [cos_t[h * Ts:(h + 1) * Ts] for h in range(n_str)]
    sin_s = [sin_t[h * Ts:(h + 1) * Ts] for h in range(n_str)]
    x_full = x.reshape(T, D)
    xs2 = [x_full] * n_str
    x_tile0 = [h * tiles_per_stream for h in range(n_str)]
    final_g2 = final_g.reshape(1, D)
    r3 = lambda a: a.reshape(Bs, S, -1)
    out = None

    for l in range(L):
        mods = [[mod[l, h * Bs:(h + 1) * Bs, i * D:(i + 1) * D].reshape(Bs, 1, D) for i in range(6)]
                for h in range(n_str)]
        g1 = norm1_g[l].reshape(1, D)
        g2 = norm2_g[l].reshape(1, D)
        w_all = _prep_w_in(w_in[l])
        wq = _prep_wq_up(mla_wq_up[l])
        wkv = mla_wkv_up[l].astype(BF16)
        gq = mla_q_norm_g[l].reshape(1, -1)
        gkv = mla_kv_norm_g[l].reshape(1, -1)
        rwh, rwl, rb = _prep_router(router_grp_w[l], router_grp_b[l], router_exp_w[l], router_exp_b[l])
        gw = gate_w[l].astype(BF16)
        gb = gate_b[l].reshape(1, -1)
        bw = branch_w[l].astype(BF16)
        ow = out_w[l].astype(BF16)
        last = l == L - 1

        merged = []
        for h in range(n_str):
            sh1, sc1, gt1, sh2, sc2, gt2 = mods[h]
            (rq, rk, rv, rg, fq, fk, fv, mq, mk, kpe, mv, ffp) = _proj(
                xs2[h], x_tile0[h], S, g1, sh1, sc1, w_all, cos_s[h], sin_s[h], gq, wq, gkv, wkv)
            ya = _retention(rq, rk, rv, rg, Bs, S).reshape(Ts, -1)
            kb = _fox_gate(ffp, fox_fb[l], S)
            yb = _flash(r3(fq), r3(fk), r3(kb), r3(fv), Bs, S, FOX_HEADS, FOX_DH, True, 3,
                        "flash_fox").reshape(Ts, -1)
            yc = _flash(r3(mq), r3(mk), r3(kpe), r3(mv), Bs, S, MLA_HEADS, MLA_DQ, False, 0,
                        "flash_mla").reshape(Ts, -1)
            merged.append(_merge(xs2[h], x_tile0[h], S, ya, yb, yc, g1, sh1, sc1, gt1, g2, sh2, sc2,
                                 gw, gb, bw, ow, rwh, rwl, rb))

        routed = []
        for h in range(n_str):
            x1, h2, mi, mf, cnt = merged[h]
            counts = cnt[:, 0].astype(I32)
            pcounts = (counts + tb - 1) // tb * tb
            pends = jnp.cumsum(pcounts)
            pstarts = pends - pcounts
            sel = mi[0:2, :, None] == jnp.arange(N_EXPERTS, dtype=I32)
            dest = (jnp.sum(jnp.where(sel, pstarts, 0), axis=-1) + mi[2:4]).reshape(As)
            blk_pos = jnp.arange(n_blocks, dtype=I32) * tb
            blk_e = jnp.minimum(jnp.sum((pends[None, :] <= blk_pos[:, None]).astype(I32), axis=1),
                                N_EXPERTS - 1)
            nvalid = (pends[-1:] // tb).astype(I32)
            fr = jnp.arange(tb, dtype=I32)[None, :]
            is_pad = (fr < (pcounts - counts)[:, None]).reshape(-1)
            pad_slot = ((pstarts + counts)[:, None] + fr).reshape(-1)
            tail_rank = jnp.cumsum(jnp.logical_not(is_pad).astype(I32)) - 1
            fill_idx = jnp.where(is_pad, pad_slot, pends[-1] + tail_rank)
            routed.append((dest, blk_e, nvalid, _sc_dispatch(h2, dest, fill_idx, P)))

        ys = [_experts(routed[h][3], routed[h][1], routed[h][2], exp_w1, exp_w3, exp_w2, l, tb)
              for h in range(n_str)]
        gathered = [_sc_gather(ys[h], routed[h][0]) for h in range(n_str)]
        for h in range(n_str):
            x1, _, _, mf, _ = merged[h]
            gt2 = mods[h][5]
            if last:
                out = _combine(x1, S, mf[0:2].T, gt2, final_g2, gathered[h], True,
                               out_rows=T, out_tile0=h * tiles_per_stream, out_prev=out)
            else:
                xs2[h] = _combine(x1, S, mf[0:2].T, gt2, final_g2, gathered[h], False)
        x_tile0 = [0] * n_str

    return out.reshape(B, S, D)
```

```python
import functools
import math

import jax
import jax.numpy as jnp
from jax import lax
from jax.experimental import pallas as pl
from jax.experimental.pallas import tpu as pltpu
from jax.experimental.pallas import tpu_sc as plsc

F32 = jnp.float32
BF16 = jnp.bfloat16
I32 = jnp.int32
U32 = jnp.uint32
HIGHEST = lax.Precision.HIGHEST

EPS = 1e-6
ROPE_THETA = 10000.0
RET_HEADS = 4
RET_DK = 64
RET_DV = 128
RET_CHUNK = 128
FOX_HEADS = 4
FOX_DH = 128
MLA_HEADS = 4
MLA_Q_RANK = 256
MLA_KV_RANK = 128
MLA_NOPE = 128
MLA_ROPE = 64
MLA_V = 128
MLA_DQ = 256
N_BRANCH = 3
BRANCH_W = 512
N_GROUPS = 4
EXP_PER_GROUP = 8
N_EXPERTS = N_GROUPS * EXP_PER_GROUP
D_EXPERT = 512

LANES = 128
V7X_VMEM_LIMIT = 56 * 1024 * 1024

C_RQ, C_RK, C_RV, C_RG = 0, 256, 512, 1024
C_FQ, C_FK, C_FV = 1536, 2048, 2560
C_MQ, C_MKV, C_TAIL = 3072, 3328, 3456
D_IN_PAD = 3584
FF_LANE = MLA_ROPE

NEG_BIG = -1e30
SC_GATHER_WINDOW = 128
N_STREAMS = 1
MERGE_COL_CHUNK = 256
LOG2E = math.log2(math.e)


def _cparams(sem):
    return pltpu.CompilerParams(dimension_semantics=sem, vmem_limit_bytes=V7X_VMEM_LIMIT)


def _pick(n, pref):
    t = min(n, pref)
    assert n % t == 0, (n, t)
    return t


def _adaln_kernel(c_ref, w_ref, b_ref, o_ref):
    c = c_ref[...]
    ca = c * jax.nn.sigmoid(c)
    o_ref[...] = jnp.dot(ca, w_ref[...], preferred_element_type=F32, precision=HIGHEST) + b_ref[...]


def _adaln(c, ada_w, ada_b):
    L, D, N = ada_w.shape
    B = c.shape[0]
    tn = _pick(N, 1536)
    return pl.pallas_call(
        _adaln_kernel,
        grid=(L, N // tn),
        in_specs=[pl.BlockSpec((B, D), lambda l, j: (0, 0)),
                  pl.BlockSpec((None, D, tn), lambda l, j: (l, 0, j)),
                  pl.BlockSpec((None, 1, tn), lambda l, j: (l, 0, j))],
        out_specs=pl.BlockSpec((None, B, tn), lambda l, j: (l, 0, j)),
        out_shape=jax.ShapeDtypeStruct((L, B, N), F32),
        compiler_params=_cparams(("arbitrary", "arbitrary")),
        name="adaln",
    )(c, ada_w, ada_b.reshape(L, 1, N))


def _rope_table_kernel(pos_ref, inv_ref, sign_ref, cos_ref, sin_ref):
    ang = pos_ref[...].astype(F32) * inv_ref[...]
    cos_ref[...] = jnp.cos(ang)
    sin_ref[...] = jnp.sin(ang) * sign_ref[...]


def _rope_tables(positions):
    T = positions.size
    tm = _pick(T, 2048)
    half = MLA_ROPE // 2
    inv = ROPE_THETA ** (-jnp.arange(0, MLA_ROPE, 2, dtype=F32) / MLA_ROPE)
    inv_t = jnp.tile(inv, LANES // half).reshape(1, LANES)
    sign = jnp.where((jnp.arange(LANES) % MLA_ROPE) < half, -1.0, 1.0).astype(F32).reshape(1, LANES)
    return pl.pallas_call(
        _rope_table_kernel,
        grid=(T // tm,),
        in_specs=[pl.BlockSpec((tm, 1), lambda i: (i, 0)),
                  pl.BlockSpec((1, LANES), lambda i: (0, 0)),
                  pl.BlockSpec((1, LANES), lambda i: (0, 0))],
        out_specs=[pl.BlockSpec((tm, LANES), lambda i: (i, 0))] * 2,
        out_shape=[jax.ShapeDtypeStruct((T, LANES), F32)] * 2,
        compiler_params=_cparams(("arbitrary",)),
        name="rope_tables",
    )(positions.reshape(T, 1), inv_t, sign)


def _rope_slab(x, cos_t, sin_t, lane):
    nxt = pltpu.roll(x, LANES - 32, axis=1)
    prv = pltpu.roll(x, 32, axis=1)
    swapped = jnp.where((lane & 32) == 0, nxt, prv)
    return x * cos_t + swapped * sin_t


def _pack_bf16_pair(lo, hi):
    lo_bits = lax.shift_right_logical(lax.bitcast_convert_type(lo.astype(BF16).astype(F32), U32), jnp.uint32(16))
    hi_bits = lax.bitcast_convert_type(hi.astype(BF16).astype(F32), U32) & jnp.uint32(0xFFFF0000)
    return hi_bits | lo_bits


def _unpack_bf16_pair(w):
    lo = lax.bitcast_convert_type(lax.shift_left(w, jnp.uint32(16)), F32)
    hi = lax.bitcast_convert_type(w & jnp.uint32(0xFFFF0000), F32)
    return lo, hi


def _norm_mod(x, g, shift, scale):
    y = x * lax.rsqrt(jnp.mean(x * x, axis=-1, keepdims=True) + EPS)
    return (y * g) * (1.0 + scale) + shift


def _rms(x, g):
    return x * lax.rsqrt(jnp.mean(x * x, axis=-1, keepdims=True) + EPS) * g


def _proj_kernel(x_ref, g_ref, sh_ref, sc_ref, w_ref, cos_ref, sin_ref,
                 gq_ref, wq_ref, gkv_ref, wkv_ref,
                 rq_ref, rk_ref, rv_ref, rg_ref, fq_ref, fk_ref, fv_ref,
                 mq_ref, mk_ref, kpe_ref, mv_ref, ff_ref):
    h = _norm_mod(x_ref[...], g_ref[...], sh_ref[...], sc_ref[...]).astype(BF16)
    cos_t = cos_ref[...]
    sin_t = sin_ref[...]
    lane = lax.broadcasted_iota(I32, cos_t.shape, 1)

    def proj(c0, width):
        return jnp.dot(h, w_ref[:, c0:c0 + width], preferred_element_type=F32)

    rq = proj(C_RQ, 256)
    rk = proj(C_RK, 256)
    for s in range(2):
        sl = slice(s * LANES, (s + 1) * LANES)
        q2 = _rope_slab(rq[:, sl], cos_t, sin_t, lane)
        k2 = _rope_slab(rk[:, sl], cos_t, sin_t, lane) * (RET_DK ** -0.5)
        for half in range(2):
            mine = (lane < RET_DK) if half == 0 else (lane >= RET_DK)
            hs = slice((2 * s + half) * LANES, (2 * s + half + 1) * LANES)
            rq_ref[:, hs] = jnp.where(mine, q2, 0.0).astype(BF16)
            rk_ref[:, hs] = jnp.where(mine, k2, 0.0).astype(BF16)
    rv_ref[...] = proj(C_RV, 512).astype(BF16)
    rg_ref[...] = proj(C_RG, 512).astype(BF16)
    fq_ref[...] = (proj(C_FQ, 512) * (FOX_DH ** -0.5 * LOG2E)).astype(BF16)
    fk_ref[...] = proj(C_FK, 512).astype(BF16)
    fv_ref[...] = proj(C_FV, 512).astype(BF16)

    tail = proj(C_TAIL, LANES)
    ff_ref[...] = tail
    kpe_ref[...] = jnp.where(lane < MLA_ROPE, _rope_slab(tail, cos_t, sin_t, lane), 0.0).astype(BF16)

    qn = _rms(proj(C_MQ, MLA_Q_RANK), gq_ref[...]).astype(BF16)
    qh = jnp.dot(qn, wq_ref[...], preferred_element_type=F32)
    q_scale = (MLA_NOPE + MLA_ROPE) ** -0.5 * LOG2E
    for hd in range(MLA_HEADS):
        c0 = hd * MLA_DQ
        mq_ref[:, c0:c0 + LANES] = (qh[:, c0:c0 + LANES] * q_scale).astype(BF16)
        pe = _rope_slab(qh[:, c0 + LANES:c0 + 2 * LANES], cos_t, sin_t, lane)
        mq_ref[:, c0 + LANES:c0 + 2 * LANES] = jnp.where(lane < MLA_ROPE, pe * q_scale, 0.0).astype(BF16)

    kvn = _rms(proj(C_MKV, MLA_KV_RANK), gkv_ref[...]).astype(BF16)
    kvh = jnp.dot(kvn, wkv_ref[...], preferred_element_type=F32)
    for hd in range(MLA_HEADS):
        c0 = hd * (MLA_NOPE + MLA_V)
        mk_ref[:, hd * MLA_NOPE:(hd + 1) * MLA_NOPE] = kvh[:, c0:c0 + MLA_NOPE].astype(BF16)
        mv_ref[:, hd * MLA_V:(hd + 1) * MLA_V] = kvh[:, c0 + MLA_NOPE:c0 + MLA_NOPE + MLA_V].astype(BF16)


def _proj(x2, x_tile0, S, g, sh, sc, w_all, cos_t, sin_t, gq, wq, gkv, wkv):
    D = x2.shape[1]
    T = cos_t.shape[0]
    tm = _pick(S, 512)
    per_b = S // tm
    row = lambda i: (i, 0)
    xrow = lambda i: (i + x_tile0, 0)
    const = lambda i: (0, 0)
    batch = lambda i: (i // per_b, 0, 0)
    widths = [512, 512, 512, 512, 512, 512, 512, MLA_HEADS * MLA_DQ, MLA_HEADS * MLA_NOPE, LANES,
              MLA_HEADS * MLA_V]
    out_shape = [jax.ShapeDtypeStruct((T, w), BF16) for w in widths]
    out_shape.append(jax.ShapeDtypeStruct((T, LANES), F32))
    out_specs = [pl.BlockSpec((tm, w), row) for w in widths] + [pl.BlockSpec((tm, LANES), row)]
    return pl.pallas_call(
        _proj_kernel,
        grid=(T // tm,),
        in_specs=[pl.BlockSpec((tm, D), xrow),
                  pl.BlockSpec((1, D), const),
                  pl.BlockSpec((None, 1, D), batch),
                  pl.BlockSpec((None, 1, D), batch),
                  pl.BlockSpec((D, D_IN_PAD), const),
                  pl.BlockSpec((tm, LANES), row),
                  pl.BlockSpec((tm, LANES), row),
                  pl.BlockSpec((1, MLA_Q_RANK), const),
                  pl.BlockSpec((MLA_Q_RANK, MLA_HEADS * MLA_DQ), const),
                  pl.BlockSpec((1, MLA_KV_RANK), const),
                  pl.BlockSpec((MLA_KV_RANK, MLA_HEADS * (MLA_NOPE + MLA_V)), const)],
        out_specs=out_specs,
        out_shape=out_shape,
        compiler_params=_cparams(("arbitrary",)),
        name="proj",
    )(x2, g, sh, sc, w_all, cos_t, sin_t, gq, wq, gkv, wkv)


def _split3(x):
    a = x.astype(BF16)
    r = x - a.astype(F32)
    b = r.astype(BF16)
    c = (r - b.astype(F32)).astype(BF16)
    return a, b, c


def _fox_gate_kernel(ff_ref, fb_ref, tri_ref, kb_ref, carry_ref, *, tiles_per_seq):
    @pl.when(pl.program_id(0) % tiles_per_seq == 0)
    def _():
        carry_ref[...] = jnp.zeros_like(carry_ref)

    tm = ff_ref.shape[0]
    lane = lax.broadcasted_iota(I32, (tm, LANES), 1)
    z = ff_ref[...] + fb_ref[...]
    ls = -(jnp.maximum(-z, 0.0) + jnp.log1p(jnp.exp(-jnp.abs(z))))
    ls = jnp.where((lane >= FF_LANE) & (lane < FF_LANE + FOX_HEADS), ls, 0.0)
    tri = tri_ref[...]
    f = carry_ref[...]
    for part in _split3(ls):
        f = f + jnp.dot(tri, part, preferred_element_type=F32)
    carry_ref[...] = f[tm - 1:tm, :]
    hi, mid, lo = [part.astype(F32) for part in _split3(f * LOG2E)]
    for hd in range(FOX_HEADS):
        src = FF_LANE + hd
        slab = jnp.where(lane == 0, pltpu.roll(hi, (LANES - src) % LANES, axis=1),
               jnp.where(lane == 1, pltpu.roll(mid, (LANES + 1 - src) % LANES, axis=1),
               jnp.where(lane == 2, pltpu.roll(lo, (LANES + 2 - src) % LANES, axis=1), 0.0)))
        kb_ref[:, hd * LANES:(hd + 1) * LANES] = slab.astype(BF16)


def _fox_gate(ffp, fb, S):
    T = ffp.shape[0]
    tm = _pick(S, 512)
    idx = jnp.arange(tm)
    tri = (idx[None, :] <= idx[:, None]).astype(BF16)
    fbv = jnp.zeros((1, LANES), F32).at[0, FF_LANE:FF_LANE + FOX_HEADS].set(fb)
    return pl.pallas_call(
        functools.partial(_fox_gate_kernel, tiles_per_seq=S // tm),
        grid=(T // tm,),
        in_specs=[pl.BlockSpec((tm, LANES), lambda i: (i, 0)),
                  pl.BlockSpec((1, LANES), lambda i: (0, 0)),
                  pl.BlockSpec((tm, tm), lambda i: (0, 0))],
        out_specs=pl.BlockSpec((tm, FOX_HEADS * LANES), lambda i: (i, 0)),
        out_shape=jax.ShapeDtypeStruct((T, FOX_HEADS * LANES), BF16),
        scratch_shapes=[pltpu.VMEM((1, LANES), F32)],
        compiler_params=_cparams(("arbitrary",)),
        name="fox_gate",
    )(ffp, fbv, tri)


def _retention_kernel(dchunk_ref, q_ref, k_ref, v_ref, g_ref, dmask_ref, din_ref, dout_ref,
                      o_ref, state_ref, *, n_chunks):
    @pl.when(pl.program_id(1) == 0)
    def _():
        state_ref[...] = jnp.zeros_like(state_ref)

    C = RET_CHUNK
    for ci in range(n_chunks):
        rows = slice(ci * C, (ci + 1) * C)
        for hd in range(RET_HEADS):
            slab = slice(hd * LANES, (hd + 1) * LANES)
            q = q_ref[rows, slab]
            k = k_ref[rows, slab]
            vcols = slice(hd * RET_DV, (hd + 1) * RET_DV)
            v = v_ref[rows, vcols]
            state = state_ref[hd]
            scores = lax.dot_general(q, k, (((1,), (1,)), ((), ())),
                                     preferred_element_type=F32) * dmask_ref[hd]
            inner = jnp.dot(scores.astype(BF16), v, preferred_element_type=F32)
            cross = jnp.dot(q, state.astype(BF16), preferred_element_type=F32) * din_ref[hd]
            o = inner + cross
            vd = (v.astype(F32) * dout_ref[hd]).astype(BF16)
            kv = lax.dot_general(k, vd, (((0,), (0,)), ((), ())), preferred_element_type=F32)
            state_ref[hd] = state * dchunk_ref[hd] + kv
            mu = jnp.mean(o, axis=-1, keepdims=True)
            d = o - mu
            var = jnp.mean(d * d, axis=-1, keepdims=True)
            on = d * lax.rsqrt(var + EPS)
            g = g_ref[rows, vcols].astype(F32)
            o_ref[rows, vcols] = (g * jax.nn.sigmoid(g) * on).astype(BF16)


def _retention(rq, rk, rv, rg, B, S):
    H, C = RET_HEADS, RET_CHUNK
    tr = _pick(S, 4 * C)
    n_chunks = tr // C
    log_gamma = jnp.log1p(-jnp.exp2(-5.0 - jnp.arange(H, dtype=F32)))
    idx = jnp.arange(C, dtype=F32)
    rel = idx[:, None] - idx[None, :]
    dmask = jnp.where(rel >= 0, jnp.exp(log_gamma[:, None, None] * jnp.maximum(rel, 0.0)), 0.0)
    decay_in = jnp.exp(log_gamma[:, None] * (idx + 1.0))
    decay_out = jnp.exp(log_gamma[:, None] * (C - 1.0 - idx))
    decay_chunk = jnp.exp(log_gamma * C)
    din = jnp.broadcast_to(decay_in[:, :, None], (H, C, RET_DV))
    dout = jnp.broadcast_to(decay_out[:, :, None], (H, C, RET_DV))
    tok = lambda b, i: (b, i, 0)
    const3 = lambda b, i: (0, 0, 0)
    return pl.pallas_call(
        functools.partial(_retention_kernel, n_chunks=n_chunks),
        grid=(B, S // tr),
        in_specs=[pl.BlockSpec(memory_space=pltpu.SMEM),
                  pl.BlockSpec((None, tr, H * LANES), tok),
                  pl.BlockSpec((None, tr, H * LANES), tok),
                  pl.BlockSpec((None, tr, H * RET_DV), tok),
                  pl.BlockSpec((None, tr, H * RET_DV), tok),
                  pl.BlockSpec((H, C, C), const3),
                  pl.BlockSpec((H, C, RET_DV), const3),
                  pl.BlockSpec((H, C, RET_DV), const3)],
        out_specs=pl.BlockSpec((None, tr, H * RET_DV), tok),
        out_shape=jax.ShapeDtypeStruct((B, S, H * RET_DV), BF16),
        scratch_shapes=[pltpu.VMEM((H, LANES, RET_DV), F32)],
        compiler_params=_cparams(("arbitrary", "arbitrary")),
        name="retention",
    )(decay_chunk, rq.reshape(B, S, -1), rk.reshape(B, S, -1), rv.reshape(B, S, -1),
      rg.reshape(B, S, -1), dmask, din, dout)


def _flash_kernel(q_ref, k_ref, ke_ref, v_ref, o_ref, m_ref, l_ref, acc_ref, s_ref, *, tq, q_bias_cols):
    S = q_ref.shape[0]
    nq = S // tq
    tk = tq
    m_ref[...] = jnp.full(m_ref.shape, NEG_BIG, F32)
    l_ref[...] = jnp.zeros_like(l_ref)
    acc_ref[...] = jnp.zeros_like(acc_ref)
    lane = lax.broadcasted_iota(I32, (tq, LANES), 1)
    key = lax.broadcasted_iota(I32, (tk, tq), 0)
    qry = lax.broadcasted_iota(I32, (tk, tq), 1)

    def scores(i, j):
        q = q_ref[i * tq:(i + 1) * tq, :]
        if q_bias_cols:
            q = jnp.concatenate([q, jnp.where(lane < q_bias_cols, -1.0, 0.0).astype(BF16)], axis=1)
        kj = jnp.concatenate([k_ref[j * tk:(j + 1) * tk, :], ke_ref[j * tk:(j + 1) * tk, :]], axis=1)
        return lax.dot_general(kj, q, (((1,), (1,)), ((), ())), preferred_element_type=F32)

    def update(s, i, j):
        m_prev = m_ref[i]
        m_new = jnp.maximum(m_prev, jnp.max(s, axis=0, keepdims=True))
        alpha = jnp.exp2(m_prev - m_new)
        p = jnp.exp2(s - m_new)
        l_ref[i] = alpha * l_ref[i] + jnp.sum(p, axis=0, keepdims=True)
        vj = v_ref[j * tk:(j + 1) * tk, :]
        acc_ref[i] = alpha * acc_ref[i] + lax.dot_general(
            vj, p.astype(BF16), (((0,), (0,)), ((), ())), preferred_element_type=F32)
        m_ref[i] = m_new

    pairs = [(i, j) for i in range(nq) for j in range(i + 1)]
    s_ref[0] = scores(*pairs[0])
    for n, (i, j) in enumerate(pairs):
        s = s_ref[n % 2]
        if n + 1 < len(pairs):
            s_ref[(n + 1) % 2] = scores(*pairs[n + 1])
        if j == i:
            s = jnp.where(key <= qry, s, NEG_BIG)
        update(s, i, j)
        if j == i:
            o_ref[i * tq:(i + 1) * tq, :] = (acc_ref[i] / l_ref[i]).T.astype(o_ref.dtype)


def _flash(q, k, ke, v, B, S, H, dq, ke_per_head, q_bias_cols, name):
    tq = _pick(S, 512)
    nq = S // tq
    ke_map = (lambda b, h: (b, 0, h)) if ke_per_head else (lambda b, h: (b, 0, 0))
    return pl.pallas_call(
        functools.partial(_flash_kernel, tq=tq, q_bias_cols=q_bias_cols),
        grid=(B, H),
        in_specs=[pl.BlockSpec((None, S, dq), lambda b, h: (b, 0, h)),
                  pl.BlockSpec((None, S, LANES), lambda b, h: (b, 0, h)),
                  pl.BlockSpec((None, S, LANES), ke_map),
                  pl.BlockSpec((None, S, LANES), lambda b, h: (b, 0, h))],
        out_specs=pl.BlockSpec((None, S, LANES), lambda b, h: (b, 0, h)),
        out_shape=jax.ShapeDtypeStruct((B, S, H * LANES), BF16),
        scratch_shapes=[pltpu.VMEM((nq, 1, tq), F32), pltpu.VMEM((nq, 1, tq), F32),
                        pltpu.VMEM((nq, LANES, tq), F32), pltpu.VMEM((2, tq, tq), F32)],
        compiler_params=_cparams(("arbitrary", "arbitrary")),
        name=name,
    )(q, k, ke, v)


def _merge_kernel(x_ref, ya_ref, yb_ref, yc_ref, g1_ref, sh1_ref, sc1_ref, gt1_ref,
                  g2_ref, sh2_ref, sc2_ref, gw_ref, gb_ref, bw_ref, ow_ref,
                  rwh_ref, rwl_ref, rb_ref, tri_ref,
                  x1_ref, h2_ref, mi_ref, mf_ref, cnt_ref, h_ref, mg_ref, carry_ref):
    tm, D = x_ref.shape
    cn = MERGE_COL_CHUNK

    @pl.when(pl.program_id(0) == 0)
    def _():
        carry_ref[...] = jnp.zeros_like(carry_ref)

    h_ref[...] = _norm_mod(x_ref[...], g1_ref[...], sh1_ref[...], sc1_ref[...]).astype(BF16)
    for n in range(D // cn):
        cols = slice(n * cn, (n + 1) * cn)
        merged = None
        for i, y_ref in enumerate((ya_ref, yb_ref, yc_ref)):
            gcols = slice(i * D + n * cn, i * D + (n + 1) * cn)
            gate = jax.nn.sigmoid(jnp.dot(h_ref[...], gw_ref[:, gcols], preferred_element_type=F32)
                                  + gb_ref[:, gcols])
            br = jnp.dot(y_ref[...], bw_ref[i, :, cols], preferred_element_type=F32)
            merged = gate * br if merged is None else merged + gate * br
        mg_ref[:, cols] = merged.astype(BF16)
    for n in range(D // cn):
        cols = slice(n * cn, (n + 1) * cn)
        mix = jnp.dot(mg_ref[...], ow_ref[:, cols], preferred_element_type=F32)
        x1_ref[:, cols] = x_ref[:, cols] + gt1_ref[:, cols] * mix
    h2 = _norm_mod(x1_ref[...], g2_ref[...], sh2_ref[...], sc2_ref[...])
    h2_ref[...] = _pack_bf16_pair(h2[:, :D // 2], h2[:, D // 2:])

    hh = h2.astype(BF16)
    hl = (h2 - hh.astype(F32)).astype(BF16)
    nt = lambda a, b: lax.dot_general(a, b, (((1,), (1,)), ((), ())), preferred_element_type=F32)
    lt = nt(rwh_ref[...], hh) + nt(rwh_ref[...], hl) + nt(rwl_ref[...], hh) + rb_ref[...]
    row8 = lax.broadcasted_iota(I32, (EXP_PER_GROUP, tm), 0)
    gl = jnp.where(row8 < N_GROUPS, lt[N_EXPERTS:N_EXPERTS + EXP_PER_GROUP, :], -jnp.inf)
    gmax = jnp.max(gl, axis=0, keepdims=True)
    g_idx = jnp.min(jnp.where(gl == gmax, row8, EXP_PER_GROUP), axis=0, keepdims=True)
    g_w = 1.0 / jnp.sum(jnp.exp(gl - gmax), axis=0, keepdims=True)
    el = lt[(N_GROUPS - 1) * EXP_PER_GROUP:N_EXPERTS, :]
    for g in range(N_GROUPS - 2, -1, -1):
        el = jnp.where(g_idx == g, lt[g * EXP_PER_GROUP:(g + 1) * EXP_PER_GROUP, :], el)
    e1 = jnp.max(el, axis=0, keepdims=True)
    i1 = jnp.min(jnp.where(el == e1, row8, EXP_PER_GROUP), axis=0, keepdims=True)
    el2 = jnp.where(row8 == i1, -jnp.inf, el)
    e2 = jnp.max(el2, axis=0, keepdims=True)
    i2 = jnp.min(jnp.where(el2 == e2, row8, EXP_PER_GROUP), axis=0, keepdims=True)
    r = jnp.exp(e2 - e1)
    w1 = g_w / (1.0 + r)
    w2 = g_w * r / (1.0 + r)
    eid1 = g_idx * EXP_PER_GROUP + i1
    eid2 = g_idx * EXP_PER_GROUP + i2

    rowe = lax.broadcasted_iota(I32, (N_EXPERTS, tm), 0)
    hit1 = rowe == eid1
    hit2 = rowe == eid2
    onehot = jnp.where(hit1 | hit2, 1.0, 0.0)
    before = (jnp.dot(onehot.astype(BF16), tri_ref[...], preferred_element_type=F32)
              + jnp.concatenate([carry_ref[...]] * (tm // LANES), axis=1))
    rank1 = jnp.sum(jnp.where(hit1, before, 0.0), axis=0, keepdims=True)
    rank2 = jnp.sum(jnp.where(hit2, before, 0.0), axis=0, keepdims=True)
    carry_ref[...] = carry_ref[...] + jnp.sum(onehot, axis=1, keepdims=True)
    cnt_ref[...] = carry_ref[...]

    mi_ref[...] = jnp.where(row8 == 0, eid1,
                  jnp.where(row8 == 1, eid2,
                  jnp.where(row8 == 2, rank1.astype(I32),
                  jnp.where(row8 == 3, rank2.astype(I32), 0))))
    mf_ref[...] = jnp.where(row8 == 0, w1, jnp.where(row8 == 1, w2, 0.0))


def _merge(x2, x_tile0, S, ya, yb, yc, g1, sh1, sc1, gt1, g2, sh2, sc2, gw, gb, bw, ow, rwh, rwl, rb):
    D = x2.shape[1]
    T = ya.shape[0]
    tm = _pick(S, 512)
    per_b = S // tm
    row = lambda i: (i, 0)
    xrow = lambda i: (i + x_tile0, 0)
    col = lambda i: (0, i)
    const = lambda i: (0, 0)
    batch = lambda i: (i // per_b, 0, 0)
    idx = jnp.arange(tm)
    tri = (idx[:, None] < idx[None, :]).astype(BF16)
    vecb = pl.BlockSpec((None, 1, D), batch)
    return pl.pallas_call(
        _merge_kernel,
        grid=(T // tm,),
        in_specs=[pl.BlockSpec((tm, D), xrow),
                  pl.BlockSpec((tm, BRANCH_W), row),
                  pl.BlockSpec((tm, BRANCH_W), row),
                  pl.BlockSpec((tm, BRANCH_W), row),
                  pl.BlockSpec((1, D), const), vecb, vecb, vecb,
                  pl.BlockSpec((1, D), const), vecb, vecb,
                  pl.BlockSpec((D, N_BRANCH * D), const),
                  pl.BlockSpec((1, N_BRANCH * D), const),
                  pl.BlockSpec((N_BRANCH, BRANCH_W, D), lambda i: (0, 0, 0)),
                  pl.BlockSpec((D, D), const),
                  pl.BlockSpec((LANES, D), const),
                  pl.BlockSpec((LANES, D), const),
                  pl.BlockSpec((LANES, 1), const),
                  pl.BlockSpec((tm, tm), const)],
        out_specs=[pl.BlockSpec((tm, D), row), pl.BlockSpec((tm, D // 2), row),
                   pl.BlockSpec((8, tm), col), pl.BlockSpec((8, tm), col),
                   pl.BlockSpec((N_EXPERTS, LANES), const)],
        out_shape=[jax.ShapeDtypeStruct((T, D), F32), jax.ShapeDtypeStruct((T, D // 2), U32),
                   jax.ShapeDtypeStruct((8, T), I32), jax.ShapeDtypeStruct((8, T), F32),
                   jax.ShapeDtypeStruct((N_EXPERTS, LANES), F32)],
        scratch_shapes=[pltpu.VMEM((tm, D), BF16), pltpu.VMEM((tm, D), BF16),
                        pltpu.VMEM((N_EXPERTS, LANES), F32)],
        compiler_params=_cparams(("arbitrary",)),
        name="merge",
    )(x2, ya, yb, yc, g1, sh1, sc1, gt1, g2, sh2, sc2, gw, gb, bw, ow, rwh, rwl, rb, tri)


def _expert_kernel(blk_e_ref, nvalid_ref, xs_ref, w1_ref, w3_ref, w2_ref, ys_ref, w1b_ref, w3b_ref, w2b_ref):
    i = pl.program_id(0)
    valid = i < nvalid_ref[0]
    new_expert = jnp.logical_or(i == 0, blk_e_ref[i] != blk_e_ref[jnp.maximum(i - 1, 0)])

    @pl.when(jnp.logical_and(valid, new_expert))
    def _():
        w1b_ref[...] = w1_ref[...].astype(BF16)
        w3b_ref[...] = w3_ref[...].astype(BF16)
        w2b_ref[...] = w2_ref[...].astype(BF16)

    @pl.when(valid)
    def _():
        half = xs_ref.shape[1]
        x_lo, x_hi = _unpack_bf16_pair(xs_ref[...])
        x_lo = x_lo.astype(BF16)
        x_hi = x_hi.astype(BF16)

        def up(w_ref):
            return (jnp.dot(x_lo, w_ref[:half, :], preferred_element_type=F32)
                    + jnp.dot(x_hi, w_ref[half:, :], preferred_element_type=F32))

        a = up(w1b_ref)
        b = up(w3b_ref)
        hid = (a * jax.nn.sigmoid(a) * b).astype(BF16)
        y = jnp.dot(hid, w2b_ref[...], preferred_element_type=F32)
        ys_ref[...] = _pack_bf16_pair(y[:, :half], y[:, half:])

    @pl.when(jnp.logical_not(valid))
    def _():
        ys_ref[...] = jnp.zeros_like(ys_ref)


def _experts(xs, blk_e, nvalid, w1, w3, w2, layer, tb):
    P, half = xs.shape
    D = 2 * half
    n_blocks = P // tb
    rows = lambda i, be, nv: (jnp.minimum(i, nv[0] - 1), 0)
    grid_spec = pltpu.PrefetchScalarGridSpec(
        num_scalar_prefetch=2,
        grid=(n_blocks,),
        in_specs=[pl.BlockSpec((tb, half), rows),
                  pl.BlockSpec((None, None, D, D_EXPERT), lambda i, be, nv: (layer, be[i], 0, 0)),
                  pl.BlockSpec((None, None, D, D_EXPERT), lambda i, be, nv: (layer, be[i], 0, 0)),
                  pl.BlockSpec((None, None, D_EXPERT, D), lambda i, be, nv: (layer, be[i], 0, 0))],
        out_specs=pl.BlockSpec((tb, half), lambda i, be, nv: (i, 0)),
        scratch_shapes=[pltpu.VMEM((D, D_EXPERT), BF16), pltpu.VMEM((D, D_EXPERT), BF16),
                        pltpu.VMEM((D_EXPERT, D), BF16)],
    )
    return pl.pallas_call(
        _expert_kernel,
        grid_spec=grid_spec,
        out_shape=jax.ShapeDtypeStruct((P, half), U32),
        compiler_params=_cparams(("arbitrary",)),
        name="moe_experts",
    )(blk_e, nvalid, xs, w1, w3, w2)


def _sc_gather(data, idx):
    M = idx.shape[0]
    D = data.shape[1]
    W = SC_GATHER_WINDOW
    assert M % W == 0, (M, W)
    mesh = plsc.VectorSubcoreMesh(core_axis_name="core", subcore_axis_name="subcore")
    n_workers = mesh.num_cores * mesh.num_subcores
    assert M % (W * n_workers) == 0, (M, W, n_workers)

    @functools.partial(pl.kernel, out_type=jax.ShapeDtypeStruct((M, D), data.dtype), mesh=mesh,
                       scratch_types=[pltpu.VMEM((W,), I32), pltpu.VMEM((W, D), data.dtype)])
    def gather_kernel(x_hbm, i_hbm, o_hbm, i_vmem, buf):
        worker = lax.axis_index("core") * mesh.num_subcores + lax.axis_index("subcore")

        @pl.loop(0, M // (W * n_workers))
        def _(t):
            start = (t * n_workers + worker) * W
            pltpu.sync_copy(i_hbm.at[pl.ds(start, W)], i_vmem)
            pltpu.sync_copy(x_hbm.at[i_vmem], buf)
            pltpu.sync_copy(buf, o_hbm.at[pl.ds(start, W)])

    return gather_kernel(data, idx)


def _sc_dispatch(h2, dest, fill_idx, P):
    T, D = h2.shape
    W = SC_GATHER_WINDOW
    n_fill = fill_idx.shape[0]
    mesh = plsc.VectorSubcoreMesh(core_axis_name="core", subcore_axis_name="subcore")
    n_workers = mesh.num_cores * mesh.num_subcores
    assert T % (W * n_workers) == 0 and n_fill % (W * n_workers) == 0, (T, n_fill, W, n_workers)
    zeros = jnp.zeros((W, D), h2.dtype)

    @functools.partial(pl.kernel, out_type=jax.ShapeDtypeStruct((P, D), h2.dtype), mesh=mesh,
                       scratch_types=[pltpu.VMEM((W,), I32), pltpu.VMEM((W, D), h2.dtype)])
    def dispatch_kernel(h_hbm, d_hbm, f_hbm, z_hbm, o_hbm, i_vmem, buf):
        worker = lax.axis_index("core") * mesh.num_subcores + lax.axis_index("subcore")

        @pl.loop(0, T // (W * n_workers))
        def _(t):
            start = (t * n_workers + worker) * W
            pltpu.sync_copy(h_hbm.at[pl.ds(start, W)], buf)
            for c in range(2):
                pltpu.sync_copy(d_hbm.at[pl.ds(c * T + start, W)], i_vmem)
                pltpu.sync_copy(buf, o_hbm.at[i_vmem])

        pltpu.sync_copy(z_hbm, buf)

        @pl.loop(0, n_fill // (W * n_workers))
        def _(t):
            start = (t * n_workers + worker) * W
            pltpu.sync_copy(f_hbm.at[pl.ds(start, W)], i_vmem)
            pltpu.sync_copy(buf, o_hbm.at[i_vmem])

    return dispatch_kernel(h2, dest, fill_idx, zeros)


def _combine_kernel(x_ref, g0_ref, g1_ref, mf_ref, gt_ref, fg_ref, *rest, final):
    o_ref = rest[-1]
    mf = mf_ref[...]
    lo0, hi0 = _unpack_bf16_pair(g0_ref[...])
    lo1, hi1 = _unpack_bf16_pair(g1_ref[...])
    w0 = mf[:, 0:1]
    w1 = mf[:, 1:2]
    ffn = jnp.concatenate([lo0 * w0 + lo1 * w1, hi0 * w0 + hi1 * w1], axis=1)
    out = x_ref[...] + gt_ref[...] * ffn
    if final:
        out = _rms(out, fg_ref[...])
    o_ref[...] = out


def _combine(x1, S, mf, gt2, final_g, g, final, out_rows=None, out_tile0=0, out_prev=None):
    T, D = x1.shape
    tm = _pick(S, 512)
    per_b = S // tm
    nt = T // tm
    in_specs = [pl.BlockSpec((tm, D), lambda i: (i, 0)),
                pl.BlockSpec((tm, D // 2), lambda i: (i, 0)),
                pl.BlockSpec((tm, D // 2), lambda i: (i + nt, 0)),
                pl.BlockSpec((tm, 2), lambda i: (i, 0)),
                pl.BlockSpec((None, 1, D), lambda i: (i // per_b, 0, 0)),
                pl.BlockSpec((1, D), lambda i: (0, 0))]
    args = [x1, g, g, mf, gt2, final_g]
    aliases = {}
    if out_prev is not None:
        in_specs.append(pl.BlockSpec(memory_space=pl.ANY))
        args.append(out_prev)
        aliases = {len(args) - 1: 0}
    return pl.pallas_call(
        functools.partial(_combine_kernel, final=final),
        grid=(nt,),
        in_specs=in_specs,
        out_specs=pl.BlockSpec((tm, D), lambda i: (i + out_tile0, 0)),
        out_shape=jax.ShapeDtypeStruct((T if out_rows is None else out_rows, D), F32),
        input_output_aliases=aliases,
        compiler_params=_cparams(("arbitrary",)),
        name="moe_combine",
    )(*args)


def _prep_w_in(w):
    offs = [0]
    for s in (256, 256, 512, 512, 512, 512, 512, FOX_HEADS, MLA_Q_RANK, MLA_KV_RANK, MLA_ROPE):
        offs.append(offs[-1] + s)
    rq, rk, rv, rg, fq, fk, fv, ff, mq, mkv, mkr = [w[:, offs[i]:offs[i + 1]] for i in range(11)]
    pad = jnp.zeros((w.shape[0], LANES - MLA_ROPE - FOX_HEADS), w.dtype)
    return jnp.concatenate([rq, rk, rv, rg, fq, fk, fv, mq, mkv, mkr, ff, pad], axis=1).astype(BF16)


def _prep_wq_up(w):
    r = w.reshape(MLA_Q_RANK, MLA_HEADS, MLA_NOPE + MLA_ROPE)
    r = jnp.pad(r, ((0, 0), (0, 0), (0, MLA_DQ - MLA_NOPE - MLA_ROPE)))
    return r.reshape(MLA_Q_RANK, MLA_HEADS * MLA_DQ).astype(BF16)


def _prep_router(w_grp, b_grp, w_exp, b_exp):
    D = w_grp.shape[0]
    pad = LANES - N_EXPERTS - N_GROUPS
    rwt = jnp.concatenate([w_exp, w_grp, jnp.zeros((D, pad), F32)], axis=1).astype(F32).T
    rwh = rwt.astype(BF16)
    rwl = (rwt - rwh.astype(F32)).astype(BF16)
    rb = jnp.concatenate([b_exp, b_grp, jnp.zeros((pad,), F32)]).astype(F32).reshape(LANES, 1)
    return rwh, rwl, rb


def kernel(x, c, positions, ada_w, ada_b, norm1_g, norm2_g, w_in, fox_fb, mla_q_norm_g, mla_wq_up, mla_kv_norm_g, mla_wkv_up, gate_w, gate_b, branch_w, out_w, router_grp_w, router_grp_b, router_exp_w, router_exp_b, exp_w1, exp_w3, exp_w2, final_g):
    B, S, D = x.shape
    L = ada_w.shape[0]
    T = B * S
    n_str = N_STREAMS if B % N_STREAMS == 0 else 1
    Bs = B // n_str
    Ts = Bs * S
    As = 2 * Ts
    tb = _pick(As, 512)
    n_blocks = As // tb + N_EXPERTS
    P = n_blocks * tb
    tiles_per_stream = Ts // _pick(S, 512)

    mod = _adaln(c, ada_w, ada_b)
    cos_t, sin_t = _rope_tables(positions)
    cos_s = [cos_t[h * Ts:(h + 1) * Ts] for h in range(n_str)]
    sin_s = [sin_t[h * Ts:(h + 1) * Ts] for h in range(n_str)]
    x_full = x.reshape(T, D)
    xs2 = [x_full] * n_str
    x_tile0 = [h * tiles_per_stream for h in range(n_str)]
    final_g2 = final_g.reshape(1, D)
    r3 = lambda a: a.reshape(Bs, S, -1)
    out = None

    for l in range(L):
        mods = [[mod[l, h * Bs:(h + 1) * Bs, i * D:(i + 1) * D].reshape(Bs, 1, D) for i in range(6)]
                for h in range(n_str)]
        g1 = norm1_g[l].reshape(1, D)
        g2 = norm2_g[l].reshape(1, D)
        w_all = _prep_w_in(w_in[l])
        wq = _prep_wq_up(mla_wq_up[l])
        wkv = mla_wkv_up[l].astype(BF16)
        gq = mla_q_norm_g[l].reshape(1, -1)
        gkv = mla_kv_norm_g[l].reshape(1, -1)
        rwh, rwl, rb = _prep_router(router_grp_w[l], router_grp_b[l], router_exp_w[l], router_exp_b[l])
        gw = gate_w[l].astype(BF16)
        gb = gate_b[l].reshape(1, -1)
        bw = branch_w[l].astype(BF16)
        ow = out_w[l].astype(BF16)
        last = l == L - 1

        merged = []
        for h in range(n_str):
            sh1, sc1, gt1, sh2, sc2, gt2 = mods[h]
            (rq, rk, rv, rg, fq, fk, fv, mq, mk, kpe, mv, ffp) = _proj(
                xs2[h], x_tile0[h], S, g1, sh1, sc1, w_all, cos_s[h], sin_s[h], gq, wq, gkv, wkv)
            ya = _retention(rq, rk, rv, rg, Bs, S).reshape(Ts, -1)
            kb = _fox_gate(ffp, fox_fb[l], S)
            yb = _flash(r3(fq), r3(fk), r3(kb), r3(fv), Bs, S, FOX_HEADS, FOX_DH, True, 3,
                        "flash_fox").reshape(Ts, -1)
            yc = _flash(r3(mq), r3(mk), r3(kpe), r3(mv), Bs, S, MLA_HEADS, MLA_DQ, False, 0,
                        "flash_mla").reshape(Ts, -1)
            merged.append(_merge(xs2[h], x_tile0[h], S, ya, yb, yc, g1, sh1, sc1, gt1, g2, sh2, sc2,
                                 gw, gb, bw, ow, rwh, rwl, rb))

        routed = []
        for h in range(n_str):
            x1, h2, mi, mf, cnt = merged[h]
            counts = cnt[:, 0].astype(I32)
            pcounts = (counts + tb - 1) // tb * tb
            pends = jnp.cumsum(pcounts)
            pstarts = pends - pcounts
            sel = mi[0:2, :, None] == jnp.arange(N_EXPERTS, dtype=I32)
            dest = (jnp.sum(jnp.where(sel, pstarts, 0), axis=-1) + mi[2:4]).reshape(As)
            blk_pos = jnp.arange(n_blocks, dtype=I32) * tb
            blk_e = jnp.minimum(jnp.sum((pends[None, :] <= blk_pos[:, None]).astype(I32), axis=1),
                                N_EXPERTS - 1)
            nvalid = (pends[-1:] // tb).astype(I32)
            fr = jnp.arange(tb, dtype=I32)[None, :]
            is_pad = (fr < (pcounts - counts)[:, None]).reshape(-1)
            pad_slot = ((pstarts + counts)[:, None] + fr).reshape(-1)
            tail_rank = jnp.cumsum(jnp.logical_not(is_pad).astype(I32)) - 1
            fill_idx = jnp.where(is_pad, pad_slot, pends[-1] + tail_rank)
            routed.append((dest, blk_e, nvalid, _sc_dispatch(h2, dest, fill_idx, P)))

        ys = [_experts(routed[h][3], routed[h][1], routed[h][2], exp_w1, exp_w3, exp_w2, l, tb)
              for h in range(n_str)]
        gathered = [_sc_gather(ys[h], routed[h][0]) for h in range(n_str)]
        for h in range(n_str):
            x1, _, _, mf, _ = merged[h]
            gt2 = mods[h][5]
            if last:
                out = _combine(x1, S, mf[0:2].T, gt2, final_g2, gathered[h], True,
                               out_rows=T, out_tile0=h * tiles_per_stream, out_prev=out)
            else:
                xs2[h] = _combine(x1, S, mf[0:2].T, gt2, final_g2, gathered[h], False)
        x_tile0 = [0] * n_str

    return out.reshape(B, S, D)
```

```python
import functools
import math

import jax
import jax.numpy as jnp
from jax import lax
from jax.experimental import pallas as pl
from jax.experimental.pallas import tpu as pltpu
from jax.experimental.pallas import tpu_sc as plsc

F32 = jnp.float32
BF16 = jnp.bfloat16
I32 = jnp.int32
U32 = jnp.uint32
HIGHEST = lax.Precision.HIGHEST

EPS = 1e-6
ROPE_THETA = 10000.0
RET_HEADS = 4
RET_DK = 64
RET_DV = 128
RET_CHUNK = 128
FOX_HEADS = 4
FOX_DH = 128
MLA_HEADS = 4
MLA_Q_RANK = 256
MLA_KV_RANK = 128
MLA_NOPE = 128
MLA_ROPE = 64
MLA_V = 128
MLA_DQ = 256
N_BRANCH = 3
BRANCH_W = 512
N_GROUPS = 4
EXP_PER_GROUP = 8
N_EXPERTS = N_GROUPS * EXP_PER_GROUP
D_EXPERT = 512

LANES = 128
V7X_VMEM_LIMIT = 56 * 1024 * 1024

C_RQ, C_RK, C_RV, C_RG = 0, 256, 512, 1024
C_FQ, C_FK, C_FV = 1536, 2048, 2560
C_MQ, C_MKV, C_TAIL = 3072, 3328, 3456
D_IN_PAD = 3584
FF_LANE = MLA_ROPE

NEG_BIG = -1e30
SC_GATHER_WINDOW = 128
N_STREAMS = 1
MERGE_COL_CHUNK = 256
LOG2E = math.log2(math.e)


def _cparams(sem):
    return pltpu.CompilerParams(dimension_semantics=sem, vmem_limit_bytes=V7X_VMEM_LIMIT)


def _pick(n, pref):
    t = min(n, pref)
    assert n % t == 0, (n, t)
    return t


def _adaln_kernel(c_ref, w_ref, b_ref, o_ref):
    c = c_ref[...]
    ca = c * jax.nn.sigmoid(c)
    o_ref[...] = jnp.dot(ca, w_ref[...], preferred_element_type=F32, precision=HIGHEST) + b_ref[...]


def _adaln(c, ada_w, ada_b):
    L, D, N = ada_w.shape
    B = c.shape[0]
    tn = _pick(N, 1536)
    return pl.pallas_call(
        _adaln_kernel,
        grid=(L, N // tn),
        in_specs=[pl.BlockSpec((B, D), lambda l, j: (0, 0)),
                  pl.BlockSpec((None, D, tn), lambda l, j: (l, 0, j)),
                  pl.BlockSpec((None, 1, tn), lambda l, j: (l, 0, j))],
        out_specs=pl.BlockSpec((None, B, tn), lambda l, j: (l, 0, j)),
        out_shape=jax.ShapeDtypeStruct((L, B, N), F32),
        compiler_params=_cparams(("arbitrary", "arbitrary")),
        name="adaln",
    )(c, ada_w, ada_b.reshape(L, 1, N))


def _rope_table_kernel(pos_ref, inv_ref, sign_ref, cos_ref, sin_ref):
    ang = pos_ref[...].astype(F32) * inv_ref[...]
    cos_ref[...] = jnp.cos(ang)
    sin_ref[...] = jnp.sin(ang) * sign_ref[...]


def _rope_tables(positions):
    T = positions.size
    tm = _pick(T, 2048)
    half = MLA_ROPE // 2
    inv = ROPE_THETA ** (-jnp.arange(0, MLA_ROPE, 2, dtype=F32) / MLA_ROPE)
    inv_t = jnp.tile(inv, LANES // half).reshape(1, LANES)
    sign = jnp.where((jnp.arange(LANES) % MLA_ROPE) < half, -1.0, 1.0).astype(F32).reshape(1, LANES)
    return pl.pallas_call(
        _rope_table_kernel,
        grid=(T // tm,),
        in_specs=[pl.BlockSpec((tm, 1), lambda i: (i, 0)),
                  pl.BlockSpec((1, LANES), lambda i: (0, 0)),
                  pl.BlockSpec((1, LANES), lambda i: (0, 0))],
        out_specs=[pl.BlockSpec((tm, LANES), lambda i: (i, 0))] * 2,
        out_shape=[jax.ShapeDtypeStruct((T, LANES), F32)] * 2,
        compiler_params=_cparams(("arbitrary",)),
        name="rope_tables",
    )(positions.reshape(T, 1), inv_t, sign)


def _rope_slab(x, cos_t, sin_t, lane):
    nxt = pltpu.roll(x, LANES - 32, axis=1)
    prv = pltpu.roll(x, 32, axis=1)
    swapped = jnp.where((lane & 32) == 0, nxt, prv)
    return x * cos_t + swapped * sin_t


def _pack_bf16_pair(lo, hi):
    lo_bits = lax.shift_right_logical(lax.bitcast_convert_type(lo.astype(BF16).astype(F32), U32), jnp.uint32(16))
    hi_bits = lax.bitcast_convert_type(hi.astype(BF16).astype(F32), U32) & jnp.uint32(0xFFFF0000)
    return hi_bits | lo_bits


def _unpack_bf16_pair(w):
    lo = lax.bitcast_convert_type(lax.shift_left(w, jnp.uint32(16)), F32)
    hi = lax.bitcast_convert_type(w & jnp.uint32(0xFFFF0000), F32)
    return lo, hi


def _norm_mod(x, g, shift, scale):
    y = x * lax.rsqrt(jnp.mean(x * x, axis=-1, keepdims=True) + EPS)
    return (y * g) * (1.0 + scale) + shift


def _rms(x, g):
    return x * lax.rsqrt(jnp.mean(x * x, axis=-1, keepdims=True) + EPS) * g


def _proj_kernel(x_ref, g_ref, sh_ref, sc_ref, w_ref, cos_ref, sin_ref,
                 gq_ref, wq_ref, gkv_ref, wkv_ref,
                 rq_ref, rk_ref, rv_ref, rg_ref, fq_ref, fk_ref, fv_ref,
                 mq_ref, mk_ref, kpe_ref, mv_ref, ff_ref):
    h = _norm_mod(x_ref[...], g_ref[...], sh_ref[...], sc_ref[...]).astype(BF16)
    cos_t = cos_ref[...]
    sin_t = sin_ref[...]
    lane = lax.broadcasted_iota(I32, cos_t.shape, 1)

    def proj(c0, width):
        return jnp.dot(h, w_ref[:, c0:c0 + width], preferred_element_type=F32)

    rq = proj(C_RQ, 256)
    rk = proj(C_RK, 256)
    for s in range(2):
        sl = slice(s * LANES, (s + 1) * LANES)
        q2 = _rope_slab(rq[:, sl], cos_t, sin_t, lane)
        k2 = _rope_slab(rk[:, sl], cos_t, sin_t, lane) * (RET_DK ** -0.5)
        for half in range(2):
            mine = (lane < RET_DK) if half == 0 else (lane >= RET_DK)
            hs = slice((2 * s + half) * LANES, (2 * s + half + 1) * LANES)
            rq_ref[:, hs] = jnp.where(mine, q2, 0.0).astype(BF16)
            rk_ref[:, hs] = jnp.where(mine, k2, 0.0).astype(BF16)
    rv_ref[...] = proj(C_RV, 512).astype(BF16)
    rg_ref[...] = proj(C_RG, 512).astype(BF16)
    fq_ref[...] = (proj(C_FQ, 512) * (FOX_DH ** -0.5 * LOG2E)).astype(BF16)
    fk_ref[...] = proj(C_FK, 512).astype(BF16)
    fv_ref[...] = proj(C_FV, 512).astype(BF16)

    tail = proj(C_TAIL, LANES)
    ff_ref[...] = tail
    kpe_ref[...] = jnp.where(lane < MLA_ROPE, _rope_slab(tail, cos_t, sin_t, lane), 0.0).astype(BF16)

    qn = _rms(proj(C_MQ, MLA_Q_RANK), gq_ref[...]).astype(BF16)
    qh = jnp.dot(qn, wq_ref[...], preferred_element_type=F32)
    q_scale = (MLA_NOPE + MLA_ROPE) ** -0.5 * LOG2E
    for hd in range(MLA_HEADS):
        c0 = hd * MLA_DQ
        mq_ref[:, c0:c0 + LANES] = (qh[:, c0:c0 + LANES] * q_scale).astype(BF16)
        pe = _rope_slab(qh[:, c0 + LANES:c0 + 2 * LANES], cos_t, sin_t, lane)
        mq_ref[:, c0 + LANES:c0 + 2 * LANES] = jnp.where(lane < MLA_ROPE, pe * q_scale, 0.0).astype(BF16)

    kvn = _rms(proj(C_MKV, MLA_KV_RANK), gkv_ref[...]).astype(BF16)
    kvh = jnp.dot(kvn, wkv_ref[...], preferred_element_type=F32)
    for hd in range(MLA_HEADS):
        c0 = hd * (MLA_NOPE + MLA_V)
        mk_ref[:, hd * MLA_NOPE:(hd + 1) * MLA_NOPE] = kvh[:, c0:c0 + MLA_NOPE].astype(BF16)
        mv_ref[:, hd * MLA_V:(hd + 1) * MLA_V] = kvh[:, c0 + MLA_NOPE:c0 + MLA_NOPE + MLA_V].astype(BF16)


def _proj(x2, x_tile0, S, g, sh, sc, w_all, cos_t, sin_t, gq, wq, gkv, wkv):
    D = x2.shape[1]
    T = cos_t.shape[0]
    tm = _pick(S, 512)
    per_b = S // tm
    row = lambda i: (i, 0)
    xrow = lambda i: (i + x_tile0, 0)
    const = lambda i: (0, 0)
    batch = lambda i: (i // per_b, 0, 0)
    widths = [512, 512, 512, 512, 512, 512, 512, MLA_HEADS * MLA_DQ, MLA_HEADS * MLA_NOPE, LANES,
              MLA_HEADS * MLA_V]
    out_shape = [jax.ShapeDtypeStruct((T, w), BF16) for w in widths]
    out_shape.append(jax.ShapeDtypeStruct((T, LANES), F32))
    out_specs = [pl.BlockSpec((tm, w), row) for w in widths] + [pl.BlockSpec((tm, LANES), row)]
    return pl.pallas_call(
        _proj_kernel,
        grid=(T // tm,),
        in_specs=[pl.BlockSpec((tm, D), xrow),
                  pl.BlockSpec((1, D), const),
                  pl.BlockSpec((None, 1, D), batch),
                  pl.BlockSpec((None, 1, D), batch),
                  pl.BlockSpec((D, D_IN_PAD), const),
                  pl.BlockSpec((tm, LANES), row),
                  pl.BlockSpec((tm, LANES), row),
                  pl.BlockSpec((1, MLA_Q_RANK), const),
                  pl.BlockSpec((MLA_Q_RANK, MLA_HEADS * MLA_DQ), const),
                  pl.BlockSpec((1, MLA_KV_RANK), const),
                  pl.BlockSpec((MLA_KV_RANK, MLA_HEADS * (MLA_NOPE + MLA_V)), const)],
        out_specs=out_specs,
        out_shape=out_shape,
        compiler_params=_cparams(("arbitrary",)),
        name="proj",
    )(x2, g, sh, sc, w_all, cos_t, sin_t, gq, wq, gkv, wkv)


def _split3(x):
    a = x.astype(BF16)
    r = x - a.astype(F32)
    b = r.astype(BF16)
    c = (r - b.astype(F32)).astype(BF16)
    return a, b, c


def _fox_gate_kernel(ff_ref, fb_ref, tri_ref, kb_ref, carry_ref, *, tiles_per_seq):
    @pl.when(pl.program_id(0) % tiles_per_seq == 0)
    def _():
        carry_ref[...] = jnp.zeros_like(carry_ref)

    tm = ff_ref.shape[0]
    lane = lax.broadcasted_iota(I32, (tm, LANES), 1)
    z = ff_ref[...] + fb_ref[...]
    ls = -(jnp.maximum(-z, 0.0) + jnp.log1p(jnp.exp(-jnp.abs(z))))
    ls = jnp.where((lane >= FF_LANE) & (lane < FF_LANE + FOX_HEADS), ls, 0.0)
    tri = tri_ref[...]
    f = carry_ref[...]
    for part in _split3(ls):
        f = f + jnp.dot(tri, part, preferred_element_type=F32)
    carry_ref[...] = f[tm - 1:tm, :]
    hi, mid, lo = [part.astype(F32) for part in _split3(f * LOG2E)]
    for hd in range(FOX_HEADS):
        src = FF_LANE + hd
        slab = jnp.where(lane == 0, pltpu.roll(hi, (LANES - src) % LANES, axis=1),
               jnp.where(lane == 1, pltpu.roll(mid, (LANES + 1 - src) % LANES, axis=1),
               jnp.where(lane == 2, pltpu.roll(lo, (LANES + 2 - src) % LANES, axis=1), 0.0)))
        kb_ref[:, hd * LANES:(hd + 1) * LANES] = slab.astype(BF16)


def _fox_gate(ffp, fb, S):
    T = ffp.shape[0]
    tm = _pick(S, 512)
    idx = jnp.arange(tm)
    tri = (idx[None, :] <= idx[:, None]).astype(BF16)
    fbv = jnp.zeros((1, LANES), F32).at[0, FF_LANE:FF_LANE + FOX_HEADS].set(fb)
    return pl.pallas_call(
        functools.partial(_fox_gate_kernel, tiles_per_seq=S // tm),
        grid=(T // tm,),
        in_specs=[pl.BlockSpec((tm, LANES), lambda i: (i, 0)),
                  pl.BlockSpec((1, LANES), lambda i: (0, 0)),
                  pl.BlockSpec((tm, tm), lambda i: (0, 0))],
        out_specs=pl.BlockSpec((tm, FOX_HEADS * LANES), lambda i: (i, 0)),
        out_shape=jax.ShapeDtypeStruct((T, FOX_HEADS * LANES), BF16),
        scratch_shapes=[pltpu.VMEM((1, LANES), F32)],
        compiler_params=_cparams(("arbitrary",)),
        name="fox_gate",
    )(ffp, fbv, tri)


def _retention_kernel(dchunk_ref, q_ref, k_ref, v_ref, g_ref, dmask_ref, din_ref, dout_ref,
                      o_ref, state_ref, *, n_chunks):
    @pl.when(pl.program_id(1) == 0)
    def _():
        state_ref[...] = jnp.zeros_like(state_ref)

    C = RET_CHUNK
    for ci in range(n_chunks):
        rows = slice(ci * C, (ci + 1) * C)
        for hd in range(RET_HEADS):
            slab = slice(hd * LANES, (hd + 1) * LANES)
            q = q_ref[rows, slab]
            k = k_ref[rows, slab]
            vcols = slice(hd * RET_DV, (hd + 1) * RET_DV)
            v = v_ref[rows, vcols]
            state = state_ref[hd]
            scores = lax.dot_general(q, k, (((1,), (1,)), ((), ())),
                                     preferred_element_type=F32) * dmask_ref[hd]
            inner = jnp.dot(scores.astype(BF16), v, preferred_element_type=F32)
            cross = jnp.dot(q, state.astype(BF16), preferred_element_type=F32) * din_ref[hd]
            o = inner + cross
            vd = (v.astype(F32) * dout_ref[hd]).astype(BF16)
            kv = lax.dot_general(k, vd, (((0,), (0,)), ((), ())), preferred_element_type=F32)
            state_ref[hd] = state * dchunk_ref[hd] + kv
            mu = jnp.mean(o, axis=-1, keepdims=True)
            d = o - mu
            var = jnp.mean(d * d, axis=-1, keepdims=True)
            on = d * lax.rsqrt(var + EPS)
            g = g_ref[rows, vcols].astype(F32)
            o_ref[rows, vcols] = (g * jax.nn.sigmoid(g) * on).astype(BF16)


def _retention(rq, rk, rv, rg, B, S):
    H, C = RET_HEADS, RET_CHUNK
    tr = _pick(S, 4 * C)
    n_chunks = tr // C
    log_gamma = jnp.log1p(-jnp.exp2(-5.0 - jnp.arange(H, dtype=F32)))
    idx = jnp.arange(C, dtype=F32)
    rel = idx[:, None] - idx[None, :]
    dmask = jnp.where(rel >= 0, jnp.exp(log_gamma[:, None, None] * jnp.maximum(rel, 0.0)), 0.0)
    decay_in = jnp.exp(log_gamma[:, None] * (idx + 1.0))
    decay_out = jnp.exp(log_gamma[:, None] * (C - 1.0 - idx))
    decay_chunk = jnp.exp(log_gamma * C)
    din = jnp.broadcast_to(decay_in[:, :, None], (H, C, RET_DV))
    dout = jnp.broadcast_to(decay_out[:, :, None], (H, C, RET_DV))
    tok = lambda b, i: (b, i, 0)
    const3 = lambda b, i: (0, 0, 0)
    return pl.pallas_call(
        functools.partial(_retention_kernel, n_chunks=n_chunks),
        grid=(B, S // tr),
        in_specs=[pl.BlockSpec(memory_space=pltpu.SMEM),
                  pl.BlockSpec((None, tr, H * LANES), tok),
                  pl.BlockSpec((None, tr, H * LANES), tok),
                  pl.BlockSpec((None, tr, H * RET_DV), tok),
                  pl.BlockSpec((None, tr, H * RET_DV), tok),
                  pl.BlockSpec((H, C, C), const3),
                  pl.BlockSpec((H, C, RET_DV), const3),
                  pl.BlockSpec((H, C, RET_DV), const3)],
        out_specs=pl.BlockSpec((None, tr, H * RET_DV), tok),
        out_shape=jax.ShapeDtypeStruct((B, S, H * RET_DV), BF16),
        scratch_shapes=[pltpu.VMEM((H, LANES, RET_DV), F32)],
        compiler_params=_cparams(("arbitrary", "arbitrary")),
        name="retention",
    )(decay_chunk, rq.reshape(B, S, -1), rk.reshape(B, S, -1), rv.reshape(B, S, -1),
      rg.reshape(B, S, -1), dmask, din, dout)


def _flash_kernel(q_ref, k_ref, ke_ref, v_ref, o_ref, m_ref, l_ref, acc_ref, s_ref, *, tq, q_bias_cols):
    S = q_ref.shape[0]
    nq = S // tq
    tk = tq
    m_ref[...] = jnp.full(m_ref.shape, NEG_BIG, F32)
    l_ref[...] = jnp.zeros_like(l_ref)
    acc_ref[...] = jnp.zeros_like(acc_ref)
    lane = lax.broadcasted_iota(I32, (tq, LANES), 1)

    half = tk // 2
    tri = (lax.broadcasted_iota(I32, (half, half), 0) <= lax.broadcasted_iota(I32, (half, half), 1))
    nt = lambda a, b: lax.dot_general(a, b, (((1,), (1,)), ((), ())), preferred_element_type=F32)

    def issue(n):
        i, j = pairs[n]
        q = q_ref[i * tq:(i + 1) * tq, :]
        if q_bias_cols:
            q = jnp.concatenate([q, jnp.where(lane < q_bias_cols, -1.0, 0.0).astype(BF16)], axis=1)
        kj = jnp.concatenate([k_ref[j * tk:(j + 1) * tk, :], ke_ref[j * tk:(j + 1) * tk, :]], axis=1)
        if j < i:
            s_ref[n % 2] = nt(kj, q)
        else:
            s_ref[n % 2, 0:half, :] = nt(kj[0:half], q)
            s_ref[n % 2, half:, half:] = nt(kj[half:], q[half:])

    def update(s, i, c0, c1, r0, r1):
        m_prev = m_ref[i, :, c0:c1]
        m_new = jnp.maximum(m_prev, jnp.max(s, axis=0, keepdims=True))
        alpha = jnp.exp2(m_prev - m_new)
        p = jnp.exp2(s - m_new)
        l_ref[i, :, c0:c1] = alpha * l_ref[i, :, c0:c1] + jnp.sum(p, axis=0, keepdims=True)
        acc_ref[i, :, c0:c1] = alpha * acc_ref[i, :, c0:c1] + lax.dot_general(
            v_ref[r0:r1, :], p.astype(BF16), (((0,), (0,)), ((), ())),
            preferred_element_type=F32)
        m_ref[i, :, c0:c1] = m_new

    def consume(n):
        i, j = pairs[n]
        slot = n % 2
        if j < i:
            update(s_ref[slot], i, 0, tq, j * tk, (j + 1) * tk)
        else:
            left = jnp.where(tri, s_ref[slot, 0:half, 0:half], NEG_BIG)
            right = jnp.concatenate([s_ref[slot, 0:half, half:],
                                     jnp.where(tri, s_ref[slot, half:, half:], NEG_BIG)], axis=0)
            update(left, i, 0, half, j * tk, j * tk + half)
            update(right, i, half, tq, j * tk, (j + 1) * tk)
            o_ref[i * tq:(i + 1) * tq, :] = (acc_ref[i] / l_ref[i]).T.astype(o_ref.dtype)

    pairs = [(i, j) for i in range(nq) for j in range(i + 1)]
    issue(0)
    for n in range(len(pairs)):
        if n + 1 < len(pairs):
            issue(n + 1)
        consume(n)


def _flash(q, k, ke, v, B, S, H, dq, ke_per_head, q_bias_cols, name):
    tq = _pick(S, 512)
    nq = S // tq
    ke_map = (lambda b, h: (b, 0, h)) if ke_per_head else (lambda b, h: (b, 0, 0))
    return pl.pallas_call(
        functools.partial(_flash_kernel, tq=tq, q_bias_cols=q_bias_cols),
        grid=(B, H),
        in_specs=[pl.BlockSpec((None, S, dq), lambda b, h: (b, 0, h)),
                  pl.BlockSpec((None, S, LANES), lambda b, h: (b, 0, h)),
                  pl.BlockSpec((None, S, LANES), ke_map),
                  pl.BlockSpec((None, S, LANES), lambda b, h: (b, 0, h))],
        out_specs=pl.BlockSpec((None, S, LANES), lambda b, h: (b, 0, h)),
        out_shape=jax.ShapeDtypeStruct((B, S, H * LANES), BF16),
        scratch_shapes=[pltpu.VMEM((nq, 1, tq), F32), pltpu.VMEM((nq, 1, tq), F32),
                        pltpu.VMEM((nq, LANES, tq), F32), pltpu.VMEM((2, tq, tq), F32)],
        compiler_params=_cparams(("arbitrary", "arbitrary")),
        name=name,
    )(q, k, ke, v)


def _merge_kernel(x_ref, ya_ref, yb_ref, yc_ref, g1_ref, sh1_ref, sc1_ref, gt1_ref,
                  g2_ref, sh2_ref, sc2_ref, gw_ref, gb_ref, bw_ref, ow_ref,
                  rwh_ref, rwl_ref, rb_ref, tri_ref,
                  x1_ref, h2_ref, mi_ref, mf_ref, cnt_ref, h_ref, mg_ref, carry_ref):
    tm, D = x_ref.shape
    cn = MERGE_COL_CHUNK

    @pl.when(pl.program_id(0) == 0)
    def _():
        carry_ref[...] = jnp.zeros_like(carry_ref)

    h_ref[...] = _norm_mod(x_ref[...], g1_ref[...], sh1_ref[...], sc1_ref[...]).astype(BF16)
    for n in range(D // cn):
        cols = slice(n * cn, (n + 1) * cn)
        merged = None
        for i, y_ref in enumerate((ya_ref, yb_ref, yc_ref)):
            gcols = slice(i * D + n * cn, i * D + (n + 1) * cn)
            gate = jax.nn.sigmoid(jnp.dot(h_ref[...], gw_ref[:, gcols], preferred_element_type=F32)
                                  + gb_ref[:, gcols])
            br = jnp.dot(y_ref[...], bw_ref[i, :, cols], preferred_element_type=F32)
            merged = gate * br if merged is None else merged + gate * br
        mg_ref[:, cols] = merged.astype(BF16)
    for n in range(D // cn):
        cols = slice(n * cn, (n + 1) * cn)
        mix = jnp.dot(mg_ref[...], ow_ref[:, cols], preferred_element_type=F32)
        x1_ref[:, cols] = x_ref[:, cols] + gt1_ref[:, cols] * mix
    h2 = _norm_mod(x1_ref[...], g2_ref[...], sh2_ref[...], sc2_ref[...])
    h2_ref[...] = _pack_bf16_pair(h2[:, :D // 2], h2[:, D // 2:])

    hh = h2.astype(BF16)
    hl = (h2 - hh.astype(F32)).astype(BF16)
    nt = lambda a, b: lax.dot_general(a, b, (((1,), (1,)), ((), ())), preferred_element_type=F32)
    lt = nt(rwh_ref[...], hh) + nt(rwh_ref[...], hl) + nt(rwl_ref[...], hh) + rb_ref[...]
    row8 = lax.broadcasted_iota(I32, (EXP_PER_GROUP, tm), 0)
    gl = jnp.where(row8 < N_GROUPS, lt[N_EXPERTS:N_EXPERTS + EXP_PER_GROUP, :], -jnp.inf)
    gmax = jnp.max(gl, axis=0, keepdims=True)
    g_idx = jnp.min(jnp.where(gl == gmax, row8, EXP_PER_GROUP), axis=0, keepdims=True)
    g_w = 1.0 / jnp.sum(jnp.exp(gl - gmax), axis=0, keepdims=True)
    el = lt[(N_GROUPS - 1) * EXP_PER_GROUP:N_EXPERTS, :]
    for g in range(N_GROUPS - 2, -1, -1):
        el = jnp.where(g_idx == g, lt[g * EXP_PER_GROUP:(g + 1) * EXP_PER_GROUP, :], el)
    e1 = jnp.max(el, axis=0, keepdims=True)
    i1 = jnp.min(jnp.where(el == e1, row8, EXP_PER_GROUP), axis=0, keepdims=True)
    el2 = jnp.where(row8 == i1, -jnp.inf, el)
    e2 = jnp.max(el2, axis=0, keepdims=True)
    i2 = jnp.min(jnp.where(el2 == e2, row8, EXP_PER_GROUP), axis=0, keepdims=True)
    r = jnp.exp(e2 - e1)
    w1 = g_w / (1.0 + r)
    w2 = g_w * r / (1.0 + r)
    eid1 = g_idx * EXP_PER_GROUP + i1
    eid2 = g_idx * EXP_PER_GROUP + i2

    rowe = lax.broadcasted_iota(I32, (N_EXPERTS, tm), 0)
    hit1 = rowe == eid1
    hit2 = rowe == eid2
    onehot = jnp.where(hit1 | hit2, 1.0, 0.0)
    before = (jnp.dot(onehot.astype(BF16), tri_ref[...], preferred_element_type=F32)
              + jnp.concatenate([carry_ref[...]] * (tm // LANES), axis=1))
    rank1 = jnp.sum(jnp.where(hit1, before, 0.0), axis=0, keepdims=True)
    rank2 = jnp.sum(jnp.where(hit2, before, 0.0), axis=0, keepdims=True)
    carry_ref[...] = carry_ref[...] + jnp.sum(onehot, axis=1, keepdims=True)
    cnt_ref[...] = carry_ref[...]

    mi_ref[...] = jnp.where(row8 == 0, eid1,
                  jnp.where(row8 == 1, eid2,
                  jnp.where(row8 == 2, rank1.astype(I32),
                  jnp.where(row8 == 3, rank2.astype(I32), 0))))
    mf_ref[...] = jnp.where(row8 == 0, w1, jnp.where(row8 == 1, w2, 0.0))


def _merge(x2, x_tile0, S, ya, yb, yc, g1, sh1, sc1, gt1, g2, sh2, sc2, gw, gb, bw, ow, rwh, rwl, rb):
    D = x2.shape[1]
    T = ya.shape[0]
    tm = _pick(S, 512)
    per_b = S // tm
    row = lambda i: (i, 0)
    xrow = lambda i: (i + x_tile0, 0)
    col = lambda i: (0, i)
    const = lambda i: (0, 0)
    batch = lambda i: (i // per_b, 0, 0)
    idx = jnp.arange(tm)
    tri = (idx[:, None] < idx[None, :]).astype(BF16)
    vecb = pl.BlockSpec((None, 1, D), batch)
    return pl.pallas_call(
        _merge_kernel,
        grid=(T // tm,),
        in_specs=[pl.BlockSpec((tm, D), xrow),
                  pl.BlockSpec((tm, BRANCH_W), row),
                  pl.BlockSpec((tm, BRANCH_W), row),
                  pl.BlockSpec((tm, BRANCH_W), row),
                  pl.BlockSpec((1, D), const), vecb, vecb, vecb,
                  pl.BlockSpec((1, D), const), vecb, vecb,
                  pl.BlockSpec((D, N_BRANCH * D), const),
                  pl.BlockSpec((1, N_BRANCH * D), const),
                  pl.BlockSpec((N_BRANCH, BRANCH_W, D), lambda i: (0, 0, 0)),
                  pl.BlockSpec((D, D), const),
                  pl.BlockSpec((LANES, D), const),
                  pl.BlockSpec((LANES, D), const),
                  pl.BlockSpec((LANES, 1), const),
                  pl.BlockSpec((tm, tm), const)],
        out_specs=[pl.BlockSpec((tm, D), row), pl.BlockSpec((tm, D // 2), row),
                   pl.BlockSpec((8, tm), col), pl.BlockSpec((8, tm), col),
                   pl.BlockSpec((N_EXPERTS, LANES), const)],
        out_shape=[jax.ShapeDtypeStruct((T, D), F32), jax.ShapeDtypeStruct((T, D // 2), U32),
                   jax.ShapeDtypeStruct((8, T), I32), jax.ShapeDtypeStruct((8, T), F32),
                   jax.ShapeDtypeStruct((N_EXPERTS, LANES), F32)],
        scratch_shapes=[pltpu.VMEM((tm, D), BF16), pltpu.VMEM((tm, D), BF16),
                        pltpu.VMEM((N_EXPERTS, LANES), F32)],
        compiler_params=_cparams(("arbitrary",)),
        name="merge",
    )(x2, ya, yb, yc, g1, sh1, sc1, gt1, g2, sh2, sc2, gw, gb, bw, ow, rwh, rwl, rb, tri)


def _expert_kernel(blk_e_ref, nvalid_ref, xs_ref, w1_ref, w3_ref, w2_ref, ys_ref, w1b_ref, w3b_ref, w2b_ref):
    i = pl.program_id(0)
    valid = i < nvalid_ref[0]
    new_expert = jnp.logical_or(i == 0, blk_e_ref[i] != blk_e_ref[jnp.maximum(i - 1, 0)])

    @pl.when(jnp.logical_and(valid, new_expert))
    def _():
        w1b_ref[...] = w1_ref[...].astype(BF16)
        w3b_ref[...] = w3_ref[...].astype(BF16)
        w2b_ref[...] = w2_ref[...].astype(BF16)

    @pl.when(valid)
    def _():
        half = xs_ref.shape[1]
        x_lo, x_hi = _unpack_bf16_pair(xs_ref[...])
        x_lo = x_lo.astype(BF16)
        x_hi = x_hi.astype(BF16)

        def up(w_ref):
            return (jnp.dot(x_lo, w_ref[:half, :], preferred_element_type=F32)
                    + jnp.dot(x_hi, w_ref[half:, :], preferred_element_type=F32))

        a = up(w1b_ref)
        b = up(w3b_ref)
        hid = (a * jax.nn.sigmoid(a) * b).astype(BF16)
        y = jnp.dot(hid, w2b_ref[...], preferred_element_type=F32)
        ys_ref[...] = _pack_bf16_pair(y[:, :half], y[:, half:])

    @pl.when(jnp.logical_not(valid))
    def _():
        ys_ref[...] = jnp.zeros_like(ys_ref)


def _experts(xs, blk_e, nvalid, w1, w3, w2, layer, tb):
    P, half = xs.shape
    D = 2 * half
    n_blocks = P // tb
    rows = lambda i, be, nv: (jnp.minimum(i, nv[0] - 1), 0)
    grid_spec = pltpu.PrefetchScalarGridSpec(
        num_scalar_prefetch=2,
        grid=(n_blocks,),
        in_specs=[pl.BlockSpec((tb, half), rows),
                  pl.BlockSpec((None, None, D, D_EXPERT), lambda i, be, nv: (layer, be[i], 0, 0)),
                  pl.BlockSpec((None, None, D, D_EXPERT), lambda i, be, nv: (layer, be[i], 0, 0)),
                  pl.BlockSpec((None, None, D_EXPERT, D), lambda i, be, nv: (layer, be[i], 0, 0))],
        out_specs=pl.BlockSpec((tb, half), lambda i, be, nv: (i, 0)),
        scratch_shapes=[pltpu.VMEM((D, D_EXPERT), BF16), pltpu.VMEM((D, D_EXPERT), BF16),
                        pltpu.VMEM((D_EXPERT, D), BF16)],
    )
    return pl.pallas_call(
        _expert_kernel,
        grid_spec=grid_spec,
        out_shape=jax.ShapeDtypeStruct((P, half), U32),
        compiler_params=_cparams(("arbitrary",)),
        name="moe_experts",
    )(blk_e, nvalid, xs, w1, w3, w2)


def _sc_gather(data, idx):
    M = idx.shape[0]
    D = data.shape[1]
    W = SC_GATHER_WINDOW
    assert M % W == 0, (M, W)
    mesh = plsc.VectorSubcoreMesh(core_axis_name="core", subcore_axis_name="subcore")
    n_workers = mesh.num_cores * mesh.num_subcores
    assert M % (W * n_workers) == 0, (M, W, n_workers)

    @functools.partial(pl.kernel, out_type=jax.ShapeDtypeStruct((M, D), data.dtype), mesh=mesh,
                       scratch_types=[pltpu.VMEM((W,), I32), pltpu.VMEM((W, D), data.dtype)])
    def gather_kernel(x_hbm, i_hbm, o_hbm, i_vmem, buf):
        worker = lax.axis_index("core") * mesh.num_subcores + lax.axis_index("subcore")

        @pl.loop(0, M // (W * n_workers))
        def _(t):
            start = (t * n_workers + worker) * W
            pltpu.sync_copy(i_hbm.at[pl.ds(start, W)], i_vmem)
            pltpu.sync_copy(x_hbm.at[i_vmem], buf)
            pltpu.sync_copy(buf, o_hbm.at[pl.ds(start, W)])

    return gather_kernel(data, idx)


def _sc_dispatch(h2, dest, fill_idx, P):
    T, D = h2.shape
    W = SC_GATHER_WINDOW
    n_fill = fill_idx.shape[0]
    mesh = plsc.VectorSubcoreMesh(core_axis_name="core", subcore_axis_name="subcore")
    n_workers = mesh.num_cores * mesh.num_subcores
    assert T % (W * n_workers) == 0 and n_fill % (W * n_workers) == 0, (T, n_fill, W, n_workers)
    zeros = jnp.zeros((W, D), h2.dtype)

    @functools.partial(pl.kernel, out_type=jax.ShapeDtypeStruct((P, D), h2.dtype), mesh=mesh,
                       scratch_types=[pltpu.VMEM((W,), I32), pltpu.VMEM((W, D), h2.dtype)])
    def dispatch_kernel(h_hbm, d_hbm, f_hbm, z_hbm, o_hbm, i_vmem, buf):
        worker = lax.axis_index("core") * mesh.num_subcores + lax.axis_index("subcore")

        @pl.loop(0, T // (W * n_workers))
        def _(t):
            start = (t * n_workers + worker) * W
            pltpu.sync_copy(h_hbm.at[pl.ds(start, W)], buf)
            for c in range(2):
                pltpu.sync_copy(d_hbm.at[pl.ds(c * T + start, W)], i_vmem)
                pltpu.sync_copy(buf, o_hbm.at[i_vmem])

        pltpu.sync_copy(z_hbm, buf)

        @pl.loop(0, n_fill // (W * n_workers))
        def _(t):
            start = (t * n_workers + worker) * W
            pltpu.sync_copy(f_hbm.at[pl.ds(start, W)], i_vmem)
            pltpu.sync_copy(buf, o_hbm.at[i_vmem])

    return dispatch_kernel(h2, dest, fill_idx, zeros)


def _combine_kernel(x_ref, g0_ref, g1_ref, mf_ref, gt_ref, fg_ref, *rest, final):
    o_ref = rest[-1]
    mf = mf_ref[...]
    lo0, hi0 = _unpack_bf16_pair(g0_ref[...])
    lo1, hi1 = _unpack_bf16_pair(g1_ref[...])
    w0 = mf[:, 0:1]
    w1 = mf[:, 1:2]
    ffn = jnp.concatenate([lo0 * w0 + lo1 * w1, hi0 * w0 + hi1 * w1], axis=1)
    out = x_ref[...] + gt_ref[...] * ffn
    if final:
        out = _rms(out, fg_ref[...])
    o_ref[...] = out


def _combine(x1, S, mf, gt2, final_g, g, final, out_rows=None, out_tile0=0, out_prev=None):
    T, D = x1.shape
    tm = _pick(S, 512)
    per_b = S // tm
    nt = T // tm
    in_specs = [pl.BlockSpec((tm, D), lambda i: (i, 0)),
                pl.BlockSpec((tm, D // 2), lambda i: (i, 0)),
                pl.BlockSpec((tm, D // 2), lambda i: (i + nt, 0)),
                pl.BlockSpec((tm, 2), lambda i: (i, 0)),
                pl.BlockSpec((None, 1, D), lambda i: (i // per_b, 0, 0)),
                pl.BlockSpec((1, D), lambda i: (0, 0))]
    args = [x1, g, g, mf, gt2, final_g]
    aliases = {}
    if out_prev is not None:
        in_specs.append(pl.BlockSpec(memory_space=pl.ANY))
        args.append(out_prev)
        aliases = {len(args) - 1: 0}
    return pl.pallas_call(
        functools.partial(_combine_kernel, final=final),
        grid=(nt,),
        in_specs=in_specs,
        out_specs=pl.BlockSpec((tm, D), lambda i: (i + out_tile0, 0)),
        out_shape=jax.ShapeDtypeStruct((T if out_rows is None else out_rows, D), F32),
        input_output_aliases=aliases,
        compiler_params=_cparams(("arbitrary",)),
        name="moe_combine",
    )(*args)


def _prep_w_in(w):
    offs = [0]
    for s in (256, 256, 512, 512, 512, 512, 512, FOX_HEADS, MLA_Q_RANK, MLA_KV_RANK, MLA_ROPE):
        offs.append(offs[-1] + s)
    rq, rk, rv, rg, fq, fk, fv, ff, mq, mkv, mkr = [w[:, offs[i]:offs[i + 1]] for i in range(11)]
    pad = jnp.zeros((w.shape[0], LANES - MLA_ROPE - FOX_HEADS), w.dtype)
    return jnp.concatenate([rq, rk, rv, rg, fq, fk, fv, mq, mkv, mkr, ff, pad], axis=1).astype(BF16)


def _prep_wq_up(w):
    r = w.reshape(MLA_Q_RANK, MLA_HEADS, MLA_NOPE + MLA_ROPE)
    r = jnp.pad(r, ((0, 0), (0, 0), (0, MLA_DQ - MLA_NOPE - MLA_ROPE)))
    return r.reshape(MLA_Q_RANK, MLA_HEADS * MLA_DQ).astype(BF16)


def _prep_router(w_grp, b_grp, w_exp, b_exp):
    D = w_grp.shape[0]
    pad = LANES - N_EXPERTS - N_GROUPS
    rwt = jnp.concatenate([w_exp, w_grp, jnp.zeros((D, pad), F32)], axis=1).astype(F32).T
    rwh = rwt.astype(BF16)
    rwl = (rwt - rwh.astype(F32)).astype(BF16)
    rb = jnp.concatenate([b_exp, b_grp, jnp.zeros((pad,), F32)]).astype(F32).reshape(LANES, 1)
    return rwh, rwl, rb


def kernel(x, c, positions, ada_w, ada_b, norm1_g, norm2_g, w_in, fox_fb, mla_q_norm_g, mla_wq_up, mla_kv_norm_g, mla_wkv_up, gate_w, gate_b, branch_w, out_w, router_grp_w, router_grp_b, router_exp_w, router_exp_b, exp_w1, exp_w3, exp_w2, final_g):
    B, S, D = x.shape
    L = ada_w.shape[0]
    T = B * S
    n_str = N_STREAMS if B % N_STREAMS == 0 else 1
    Bs = B // n_str
    Ts = Bs * S
    As = 2 * Ts
    tb = _pick(As, 512)
    n_blocks = As // tb + N_EXPERTS
    P = n_blocks * tb
    tiles_per_stream = Ts // _pick(S, 512)

    mod = _adaln(c, ada_w, ada_b)
    cos_t, sin_t = _rope_tables(positions)
    cos_s = [cos_t[h * Ts:(h + 1) * Ts] for h in range(n_str)]
    sin_s = [sin_t[h * Ts:(h + 1) * Ts] for h in range(n_str)]
    x_full = x.reshape(T, D)
    xs2 = [x_full] * n_str
    x_tile0 = [h * tiles_per_stream for h in range(n_str)]
    final_g2 = final_g.reshape(1, D)
    r3 = lambda a: a.reshape(Bs, S, -1)
    out = None

    for l in range(L):
        mods = [[mod[l, h * Bs:(h + 1) * Bs, i * D:(i + 1) * D].reshape(Bs, 1, D) for i in range(6)]
                for h in range(n_str)]
        g1 = norm1_g[l].reshape(1, D)
        g2 = norm2_g[l].reshape(1, D)
        w_all = _prep_w_in(w_in[l])
        wq = _prep_wq_up(mla_wq_up[l])
        wkv = mla_wkv_up[l].astype(BF16)
        gq = mla_q_norm_g[l].reshape(1, -1)
        gkv = mla_kv_norm_g[l].reshape(1, -1)
        rwh, rwl, rb = _prep_router(router_grp_w[l], router_grp_b[l], router_exp_w[l], router_exp_b[l])
        gw = gate_w[l].astype(BF16)
        gb = gate_b[l].reshape(1, -1)
        bw = branch_w[l].astype(BF16)
        ow = out_w[l].astype(BF16)
        last = l == L - 1

        merged = []
        for h in range(n_str):
            sh1, sc1, gt1, sh2, sc2, gt2 = mods[h]
            (rq, rk, rv, rg, fq, fk, fv, mq, mk, kpe, mv, ffp) = _proj(
                xs2[h], x_tile0[h], S, g1, sh1, sc1, w_all, cos_s[h], sin_s[h], gq, wq, gkv, wkv)
            ya = _retention(rq, rk, rv, rg, Bs, S).reshape(Ts, -1)
            kb = _fox_gate(ffp, fox_fb[l], S)
            yb = _flash(r3(fq), r3(fk), r3(kb), r3(fv), Bs, S, FOX_HEADS, FOX_DH, True, 3,
                        "flash_fox").reshape(Ts, -1)
            yc = _flash(r3(mq), r3(mk), r3(kpe), r3(mv), Bs, S, MLA_HEADS, MLA_DQ, False, 0,
                        "flash_mla").reshape(Ts, -1)
            merged.append(_merge(xs2[h], x_tile0[h], S, ya, yb, yc, g1, sh1, sc1, gt1, g2, sh2, sc2,
                                 gw, gb, bw, ow, rwh, rwl, rb))

        routed = []
        for h in range(n_str):
            x1, h2, mi, mf, cnt = merged[h]
            counts = cnt[:, 0].astype(I32)
            pcounts = (counts + tb - 1) // tb * tb
            pends = jnp.cumsum(pcounts)
            pstarts = pends - pcounts
            sel = mi[0:2, :, None] == jnp.arange(N_EXPERTS, dtype=I32)
            dest = (jnp.sum(jnp.where(sel, pstarts, 0), axis=-1) + mi[2:4]).reshape(As)
            blk_pos = jnp.arange(n_blocks, dtype=I32) * tb
            blk_e = jnp.minimum(jnp.sum((pends[None, :] <= blk_pos[:, None]).astype(I32), axis=1),
                                N_EXPERTS - 1)
            nvalid = (pends[-1:] // tb).astype(I32)
            fr = jnp.arange(tb, dtype=I32)[None, :]
            is_pad = (fr < (pcounts - counts)[:, None]).reshape(-1)
            pad_slot = ((pstarts + counts)[:, None] + fr).reshape(-1)
            tail_rank = jnp.cumsum(jnp.logical_not(is_pad).astype(I32)) - 1
            fill_idx = jnp.where(is_pad, pad_slot, pends[-1] + tail_rank)
            routed.append((dest, blk_e, nvalid, _sc_dispatch(h2, dest, fill_idx, P)))

        ys = [_experts(routed[h][3], routed[h][1], routed[h][2], exp_w1, exp_w3, exp_w2, l, tb)
              for h in range(n_str)]
        gathered = [_sc_gather(ys[h], routed[h][0]) for h in range(n_str)]
        for h in range(n_str):
            x1, _, _, mf, _ = merged[h]
            gt2 = mods[h][5]
            if last:
                out = _combine(x1, S, mf[0:2].T, gt2, final_g2, gathered[h], True,
                               out_rows=T, out_tile0=h * tiles_per_stream, out_prev=out)
            else:
                xs2[h] = _combine(x1, S, mf[0:2].T, gt2, final_g2, gathered[h], False)
        x_tile0 = [0] * n_str

    return out.reshape(B, S, D)
```

```python
import functools
import math

import jax
import jax.numpy as jnp
from jax import lax
from jax.experimental import pallas as pl
from jax.experimental.pallas import tpu as pltpu
from jax.experimental.pallas import tpu_sc as plsc

F32 = jnp.float32
BF16 = jnp.bfloat16
I32 = jnp.int32
U32 = jnp.uint32
HIGHEST = lax.Precision.HIGHEST

EPS = 1e-6
ROPE_THETA = 10000.0
RET_HEADS = 4
RET_DK = 64
RET_DV = 128
RET_CHUNK = 128
FOX_HEADS = 4
FOX_DH = 128
MLA_HEADS = 4
MLA_Q_RANK = 256
MLA_KV_RANK = 128
MLA_NOPE = 128
MLA_ROPE = 64
MLA_V = 128
MLA_DQ = 256
N_BRANCH = 3
BRANCH_W = 512
N_GROUPS = 4
EXP_PER_GROUP = 8
N_EXPERTS = N_GROUPS * EXP_PER_GROUP
D_EXPERT = 512

LANES = 128
V7X_VMEM_LIMIT = 56 * 1024 * 1024

C_RQ, C_RK, C_RV, C_RG = 0, 256, 512, 1024
C_FQ, C_FK, C_FV = 1536, 2048, 2560
C_MQ, C_MKV, C_TAIL = 3072, 3328, 3456
D_IN_PAD = 3584
FF_LANE = MLA_ROPE

NEG_BIG = -1e30
SC_GATHER_WINDOW = 128
N_STREAMS = 1
MERGE_COL_CHUNK = 256
LOG2E = math.log2(math.e)


def _cparams(sem):
    return pltpu.CompilerParams(dimension_semantics=sem, vmem_limit_bytes=V7X_VMEM_LIMIT)


def _pick(n, pref):
    t = min(n, pref)
    assert n % t == 0, (n, t)
    return t


def _adaln_kernel(c_ref, w_ref, b_ref, o_ref):
    c = c_ref[...]
    ca = c * jax.nn.sigmoid(c)
    o_ref[...] = jnp.dot(ca, w_ref[...], preferred_element_type=F32, precision=HIGHEST) + b_ref[...]


def _adaln(c, ada_w, ada_b):
    L, D, N = ada_w.shape
    B = c.shape[0]
    tn = _pick(N, 1536)
    return pl.pallas_call(
        _adaln_kernel,
        grid=(L, N // tn),
        in_specs=[pl.BlockSpec((B, D), lambda l, j: (0, 0)),
                  pl.BlockSpec((None, D, tn), lambda l, j: (l, 0, j)),
                  pl.BlockSpec((None, 1, tn), lambda l, j: (l, 0, j))],
        out_specs=pl.BlockSpec((None, B, tn), lambda l, j: (l, 0, j)),
        out_shape=jax.ShapeDtypeStruct((L, B, N), F32),
        compiler_params=_cparams(("arbitrary", "arbitrary")),
        name="adaln",
    )(c, ada_w, ada_b.reshape(L, 1, N))


def _rope_table_kernel(pos_ref, inv_ref, sign_ref, cos_ref, sin_ref):
    ang = pos_ref[...].astype(F32) * inv_ref[...]
    cos_ref[...] = jnp.cos(ang)
    sin_ref[...] = jnp.sin(ang) * sign_ref[...]


def _rope_tables(positions):
    T = positions.size
    tm = _pick(T, 2048)
    half = MLA_ROPE // 2
    inv = ROPE_THETA ** (-jnp.arange(0, MLA_ROPE, 2, dtype=F32) / MLA_ROPE)
    inv_t = jnp.tile(inv, LANES // half).reshape(1, LANES)
    sign = jnp.where((jnp.arange(LANES) % MLA_ROPE) < half, -1.0, 1.0).astype(F32).reshape(1, LANES)
    return pl.pallas_call(
        _rope_table_kernel,
        grid=(T // tm,),
        in_specs=[pl.BlockSpec((tm, 1), lambda i: (i, 0)),
                  pl.BlockSpec((1, LANES), lambda i: (0, 0)),
                  pl.BlockSpec((1, LANES), lambda i: (0, 0))],
        out_specs=[pl.BlockSpec((tm, LANES), lambda i: (i, 0))] * 2,
        out_shape=[jax.ShapeDtypeStruct((T, LANES), F32)] * 2,
        compiler_params=_cparams(("arbitrary",)),
        name="rope_tables",
    )(positions.reshape(T, 1), inv_t, sign)


def _rope_slab(x, cos_t, sin_t, lane):
    nxt = pltpu.roll(x, LANES - 32, axis=1)
    prv = pltpu.roll(x, 32, axis=1)
    swapped = jnp.where((lane & 32) == 0, nxt, prv)
    return x * cos_t + swapped * sin_t


def _pack_bf16_pair(lo, hi):
    lo_bits = lax.shift_right_logical(lax.bitcast_convert_type(lo.astype(BF16).astype(F32), U32), jnp.uint32(16))
    hi_bits = lax.bitcast_convert_type(hi.astype(BF16).astype(F32), U32) & jnp.uint32(0xFFFF0000)
    return hi_bits | lo_bits


def _unpack_bf16_pair(w):
    lo = lax.bitcast_convert_type(lax.shift_left(w, jnp.uint32(16)), F32)
    hi = lax.bitcast_convert_type(w & jnp.uint32(0xFFFF0000), F32)
    return lo, hi


def _norm_mod(x, g, shift, scale):
    y = x * lax.rsqrt(jnp.mean(x * x, axis=-1, keepdims=True) + EPS)
    return (y * g) * (1.0 + scale) + shift


def _rms(x, g):
    return x * lax.rsqrt(jnp.mean(x * x, axis=-1, keepdims=True) + EPS) * g


def _proj_kernel(x_ref, g_ref, sh_ref, sc_ref, w_ref, cos_ref, sin_ref,
                 gq_ref, wq_ref, gkv_ref, wkv_ref,
                 rq_ref, rk_ref, rv_ref, rg_ref, fq_ref, fk_ref, fv_ref,
                 mq_ref, mk_ref, kpe_ref, mv_ref, ff_ref):
    h = _norm_mod(x_ref[...], g_ref[...], sh_ref[...], sc_ref[...]).astype(BF16)
    cos_t = cos_ref[...]
    sin_t = sin_ref[...]
    lane = lax.broadcasted_iota(I32, cos_t.shape, 1)

    def proj(c0, width):
        return jnp.dot(h, w_ref[:, c0:c0 + width], preferred_element_type=F32)

    rq = proj(C_RQ, 256)
    rk = proj(C_RK, 256)
    for s in range(2):
        sl = slice(s * LANES, (s + 1) * LANES)
        q2 = _rope_slab(rq[:, sl], cos_t, sin_t, lane)
        k2 = _rope_slab(rk[:, sl], cos_t, sin_t, lane) * (RET_DK ** -0.5)
        for half in range(2):
            mine = (lane < RET_DK) if half == 0 else (lane >= RET_DK)
            hs = slice((2 * s + half) * LANES, (2 * s + half + 1) * LANES)
            rq_ref[:, hs] = jnp.where(mine, q2, 0.0).astype(BF16)
            rk_ref[:, hs] = jnp.where(mine, k2, 0.0).astype(BF16)
    rv_ref[...] = proj(C_RV, 512).astype(BF16)
    rg_ref[...] = proj(C_RG, 512).astype(BF16)
    fq_ref[...] = (proj(C_FQ, 512) * (FOX_DH ** -0.5 * LOG2E)).astype(BF16)
    fk_ref[...] = proj(C_FK, 512).astype(BF16)
    fv_ref[...] = proj(C_FV, 512).astype(BF16)

    tail = proj(C_TAIL, LANES)
    ff_ref[...] = tail
    kpe_ref[...] = jnp.where(lane < MLA_ROPE, _rope_slab(tail, cos_t, sin_t, lane), 0.0).astype(BF16)

    qn = _rms(proj(C_MQ, MLA_Q_RANK), gq_ref[...]).astype(BF16)
    qh = jnp.dot(qn, wq_ref[...], preferred_element_type=F32)
    q_scale = (MLA_NOPE + MLA_ROPE) ** -0.5 * LOG2E
    for hd in range(MLA_HEADS):
        c0 = hd * MLA_DQ
        mq_ref[:, c0:c0 + LANES] = (qh[:, c0:c0 + LANES] * q_scale).astype(BF16)
        pe = _rope_slab(qh[:, c0 + LANES:c0 + 2 * LANES], cos_t, sin_t, lane)
        mq_ref[:, c0 + LANES:c0 + 2 * LANES] = jnp.where(lane < MLA_ROPE, pe * q_scale, 0.0).astype(BF16)

    kvn = _rms(proj(C_MKV, MLA_KV_RANK), gkv_ref[...]).astype(BF16)
    kvh = jnp.dot(kvn, wkv_ref[...], preferred_element_type=F32)
    for hd in range(MLA_HEADS):
        c0 = hd * (MLA_NOPE + MLA_V)
        mk_ref[:, hd * MLA_NOPE:(hd + 1) * MLA_NOPE] = kvh[:, c0:c0 + MLA_NOPE].astype(BF16)
        mv_ref[:, hd * MLA_V:(hd + 1) * MLA_V] = kvh[:, c0 + MLA_NOPE:c0 + MLA_NOPE + MLA_V].astype(BF16)


def _proj(x2, x_tile0, S, g, sh, sc, w_all, cos_t, sin_t, gq, wq, gkv, wkv):
    D = x2.shape[1]
    T = cos_t.shape[0]
    tm = _pick(S, 512)
    per_b = S // tm
    row = lambda i: (i, 0)
    xrow = lambda i: (i + x_tile0, 0)
    const = lambda i: (0, 0)
    batch = lambda i: (i // per_b, 0, 0)
    widths = [512, 512, 512, 512, 512, 512, 512, MLA_HEADS * MLA_DQ, MLA_HEADS * MLA_NOPE, LANES,
              MLA_HEADS * MLA_V]
    out_shape = [jax.ShapeDtypeStruct((T, w), BF16) for w in widths]
    out_shape.append(jax.ShapeDtypeStruct((T, LANES), F32))
    out_specs = [pl.BlockSpec((tm, w), row) for w in widths] + [pl.BlockSpec((tm, LANES), row)]
    return pl.pallas_call(
        _proj_kernel,
        grid=(T // tm,),
        in_specs=[pl.BlockSpec((tm, D), xrow),
                  pl.BlockSpec((1, D), const),
                  pl.BlockSpec((None, 1, D), batch),
                  pl.BlockSpec((None, 1, D), batch),
                  pl.BlockSpec((D, D_IN_PAD), const),
                  pl.BlockSpec((tm, LANES), row),
                  pl.BlockSpec((tm, LANES), row),
                  pl.BlockSpec((1, MLA_Q_RANK), const),
                  pl.BlockSpec((MLA_Q_RANK, MLA_HEADS * MLA_DQ), const),
                  pl.BlockSpec((1, MLA_KV_RANK), const),
                  pl.BlockSpec((MLA_KV_RANK, MLA_HEADS * (MLA_NOPE + MLA_V)), const)],
        out_specs=out_specs,
        out_shape=out_shape,
        compiler_params=_cparams(("arbitrary",)),
        name="proj",
    )(x2, g, sh, sc, w_all, cos_t, sin_t, gq, wq, gkv, wkv)


def _split3(x):
    a = x.astype(BF16)
    r = x - a.astype(F32)
    b = r.astype(BF16)
    c = (r - b.astype(F32)).astype(BF16)
    return a, b, c


def _fox_gate_kernel(ff_ref, fb_ref, tri_ref, kb_ref):
    tm = tri_ref.shape[0]
    lane = lax.broadcasted_iota(I32, (tm, LANES), 1)
    tri = tri_ref[...]
    carry = jnp.zeros((1, LANES), F32)
    for t in range(ff_ref.shape[0] // tm):
        rows = slice(t * tm, (t + 1) * tm)
        z = ff_ref[rows, :] + fb_ref[...]
        ls = -(jnp.maximum(-z, 0.0) + jnp.log1p(jnp.exp(-jnp.abs(z))))
        ls = jnp.where((lane >= FF_LANE) & (lane < FF_LANE + FOX_HEADS), ls, 0.0)
        f = carry
        for part in _split3(ls):
            f = f + jnp.dot(tri, part, preferred_element_type=F32)
        carry = f[tm - 1:tm, :]
        hi, mid, lo = [part.astype(F32) for part in _split3(f * LOG2E)]
        for hd in range(FOX_HEADS):
            src = FF_LANE + hd
            slab = jnp.where(lane == 0, pltpu.roll(hi, (LANES - src) % LANES, axis=1),
                   jnp.where(lane == 1, pltpu.roll(mid, (LANES + 1 - src) % LANES, axis=1),
                   jnp.where(lane == 2, pltpu.roll(lo, (LANES + 2 - src) % LANES, axis=1), 0.0)))
            kb_ref[rows, hd * LANES:(hd + 1) * LANES] = slab.astype(BF16)


def _fox_gate(ffp, fb, S):
    T = ffp.shape[0]
    tm = _pick(S, 512)
    idx = jnp.arange(tm)
    tri = (idx[None, :] <= idx[:, None]).astype(BF16)
    fbv = jnp.zeros((1, LANES), F32).at[0, FF_LANE:FF_LANE + FOX_HEADS].set(fb)
    return pl.pallas_call(
        _fox_gate_kernel,
        grid=(T // S,),
        in_specs=[pl.BlockSpec((S, LANES), lambda b: (b, 0)),
                  pl.BlockSpec((1, LANES), lambda b: (0, 0)),
                  pl.BlockSpec((tm, tm), lambda b: (0, 0))],
        out_specs=pl.BlockSpec((S, FOX_HEADS * LANES), lambda b: (b, 0)),
        out_shape=jax.ShapeDtypeStruct((T, FOX_HEADS * LANES), BF16),
        compiler_params=_cparams(("arbitrary",)),
        name="fox_gate",
    )(ffp, fbv, tri)


def _retention_kernel(dchunk_ref, q_ref, k_ref, v_ref, g_ref, dmask_ref, din_ref, dout_ref,
                      o_ref, state_ref, *, n_chunks):
    @pl.when(pl.program_id(1) == 0)
    def _():
        state_ref[...] = jnp.zeros_like(state_ref)

    C = RET_CHUNK
    for ci in range(n_chunks):
        rows = slice(ci * C, (ci + 1) * C)
        for hd in range(RET_HEADS):
            slab = slice(hd * LANES, (hd + 1) * LANES)
            q = q_ref[rows, slab]
            k = k_ref[rows, slab]
            vcols = slice(hd * RET_DV, (hd + 1) * RET_DV)
            v = v_ref[rows, vcols]
            state = state_ref[hd]
            scores = lax.dot_general(q, k, (((1,), (1,)), ((), ())),
                                     preferred_element_type=F32) * dmask_ref[hd]
            inner = jnp.dot(scores.astype(BF16), v, preferred_element_type=F32)
            cross = jnp.dot(q, state.astype(BF16), preferred_element_type=F32) * din_ref[hd]
            o = inner + cross
            vd = (v.astype(F32) * dout_ref[hd]).astype(BF16)
            kv = lax.dot_general(k, vd, (((0,), (0,)), ((), ())), preferred_element_type=F32)
            state_ref[hd] = state * dchunk_ref[hd] + kv
            mu = jnp.mean(o, axis=-1, keepdims=True)
            d = o - mu
            var = jnp.mean(d * d, axis=-1, keepdims=True)
            on = d * lax.rsqrt(var + EPS)
            g = g_ref[rows, vcols].astype(F32)
            o_ref[rows, vcols] = (g * jax.nn.sigmoid(g) * on).astype(BF16)


def _retention(rq, rk, rv, rg, B, S):
    H, C = RET_HEADS, RET_CHUNK
    tr = _pick(S, 16 * C)
    n_chunks = tr // C
    log_gamma = jnp.log1p(-jnp.exp2(-5.0 - jnp.arange(H, dtype=F32)))
    idx = jnp.arange(C, dtype=F32)
    rel = idx[:, None] - idx[None, :]
    dmask = jnp.where(rel >= 0, jnp.exp(log_gamma[:, None, None] * jnp.maximum(rel, 0.0)), 0.0)
    decay_in = jnp.exp(log_gamma[:, None] * (idx + 1.0))
    decay_out = jnp.exp(log_gamma[:, None] * (C - 1.0 - idx))
    decay_chunk = jnp.exp(log_gamma * C)
    din = jnp.broadcast_to(decay_in[:, :, None], (H, C, RET_DV))
    dout = jnp.broadcast_to(decay_out[:, :, None], (H, C, RET_DV))
    tok = lambda b, i: (b, i, 0)
    const3 = lambda b, i: (0, 0, 0)
    return pl.pallas_call(
        functools.partial(_retention_kernel, n_chunks=n_chunks),
        grid=(B, S // tr),
        in_specs=[pl.BlockSpec(memory_space=pltpu.SMEM),
                  pl.BlockSpec((None, tr, H * LANES), tok),
                  pl.BlockSpec((None, tr, H * LANES), tok),
                  pl.BlockSpec((None, tr, H * RET_DV), tok),
                  pl.BlockSpec((None, tr, H * RET_DV), tok),
                  pl.BlockSpec((H, C, C), const3),
                  pl.BlockSpec((H, C, RET_DV), const3),
                  pl.BlockSpec((H, C, RET_DV), const3)],
        out_specs=pl.BlockSpec((None, tr, H * RET_DV), tok),
        out_shape=jax.ShapeDtypeStruct((B, S, H * RET_DV), BF16),
        scratch_shapes=[pltpu.VMEM((H, LANES, RET_DV), F32)],
        compiler_params=_cparams(("arbitrary", "arbitrary")),
        name="retention",
    )(decay_chunk, rq.reshape(B, S, -1), rk.reshape(B, S, -1), rv.reshape(B, S, -1),
      rg.reshape(B, S, -1), dmask, din, dout)


def _flash_kernel(q_ref, k_ref, ke_ref, v_ref, o_ref, m_ref, l_ref, acc_ref, s_ref, *, tq, q_bias_cols):
    S = q_ref.shape[0]
    nq = S // tq
    tk = tq
    m_ref[...] = jnp.full(m_ref.shape, NEG_BIG, F32)
    l_ref[...] = jnp.zeros_like(l_ref)
    acc_ref[...] = jnp.zeros_like(acc_ref)
    lane = lax.broadcasted_iota(I32, (tq, LANES), 1)

    half = tk // 2
    tri = (lax.broadcasted_iota(I32, (half, half), 0) <= lax.broadcasted_iota(I32, (half, half), 1))
    nt = lambda a, b: lax.dot_general(a, b, (((1,), (1,)), ((), ())), preferred_element_type=F32)

    def issue(n):
        i, j = pairs[n]
        q = q_ref[i * tq:(i + 1) * tq, :]
        if q_bias_cols:
            q = jnp.concatenate([q, jnp.where(lane < q_bias_cols, -1.0, 0.0).astype(BF16)], axis=1)
        kj = jnp.concatenate([k_ref[j * tk:(j + 1) * tk, :], ke_ref[j * tk:(j + 1) * tk, :]], axis=1)
        if j < i:
            s_ref[n % 2] = nt(kj, q)
        else:
            s_ref[n % 2, 0:half, :] = nt(kj[0:half], q)
            s_ref[n % 2, half:, half:] = nt(kj[half:], q[half:])

    def update(s, i, c0, c1, r0, r1):
        m_prev = m_ref[i, :, c0:c1]
        m_new = jnp.maximum(m_prev, jnp.max(s, axis=0, keepdims=True))
        alpha = jnp.exp2(m_prev - m_new)
        p = jnp.exp2(s - m_new)
        l_ref[i, :, c0:c1] = alpha * l_ref[i, :, c0:c1] + jnp.sum(p, axis=0, keepdims=True)
        acc_ref[i, :, c0:c1] = alpha * acc_ref[i, :, c0:c1] + lax.dot_general(
            v_ref[r0:r1, :], p.astype(BF16), (((0,), (0,)), ((), ())),
            preferred_element_type=F32)
        m_ref[i, :, c0:c1] = m_new

    def consume(n):
        i, j = pairs[n]
        slot = n % 2
        if j < i:
            update(s_ref[slot], i, 0, tq, j * tk, (j + 1) * tk)
        else:
            left = jnp.where(tri, s_ref[slot, 0:half, 0:half], NEG_BIG)
            right = jnp.concatenate([s_ref[slot, 0:half, half:],
                                     jnp.where(tri, s_ref[slot, half:, half:], NEG_BIG)], axis=0)
            update(left, i, 0, half, j * tk, j * tk + half)
            update(right, i, half, tq, j * tk, (j + 1) * tk)
            o_ref[i * tq:(i + 1) * tq, :] = (acc_ref[i] / l_ref[i]).T.astype(o_ref.dtype)

    pairs = [(i, j) for i in range(nq) for j in range(i + 1)]
    issue(0)
    for n in range(len(pairs)):
        if n + 1 < len(pairs):
            issue(n + 1)
        consume(n)


def _flash(q, k, ke, v, B, S, H, dq, ke_per_head, q_bias_cols, name):
    tq = _pick(S, 512)
    nq = S // tq
    ke_map = (lambda b, h: (b, 0, h)) if ke_per_head else (lambda b, h: (b, 0, 0))
    return pl.pallas_call(
        functools.partial(_flash_kernel, tq=tq, q_bias_cols=q_bias_cols),
        grid=(B, H),
        in_specs=[pl.BlockSpec((None, S, dq), lambda b, h: (b, 0, h)),
                  pl.BlockSpec((None, S, LANES), lambda b, h: (b, 0, h)),
                  pl.BlockSpec((None, S, LANES), ke_map),
                  pl.BlockSpec((None, S, LANES), lambda b, h: (b, 0, h))],
        out_specs=pl.BlockSpec((None, S, LANES), lambda b, h: (b, 0, h)),
        out_shape=jax.ShapeDtypeStruct((B, S, H * LANES), BF16),
        scratch_shapes=[pltpu.VMEM((nq, 1, tq), F32), pltpu.VMEM((nq, 1, tq), F32),
                        pltpu.VMEM((nq, LANES, tq), F32), pltpu.VMEM((2, tq, tq), F32)],
        compiler_params=_cparams(("arbitrary", "arbitrary")),
        name=name,
    )(q, k, ke, v)


def _merge_kernel(x_ref, ya_ref, yb_ref, yc_ref, g1_ref, sh1_ref, sc1_ref, gt1_ref,
                  g2_ref, sh2_ref, sc2_ref, gw_ref, gb_ref, bw_ref, ow_ref,
                  rwh_ref, rwl_ref, rb_ref, tri_ref,
                  x1_ref, h2_ref, mi_ref, mf_ref, cnt_ref, h_ref, mg_ref, carry_ref):
    tm, D = x_ref.shape
    cn = MERGE_COL_CHUNK

    @pl.when(pl.program_id(0) == 0)
    def _():
        carry_ref[...] = jnp.zeros_like(carry_ref)

    h_ref[...] = _norm_mod(x_ref[...], g1_ref[...], sh1_ref[...], sc1_ref[...]).astype(BF16)
    for n in range(D // cn):
        cols = slice(n * cn, (n + 1) * cn)
        merged = None
        for i, y_ref in enumerate((ya_ref, yb_ref, yc_ref)):
            gcols = slice(i * D + n * cn, i * D + (n + 1) * cn)
            gate = jax.nn.sigmoid(jnp.dot(h_ref[...], gw_ref[:, gcols], preferred_element_type=F32)
                                  + gb_ref[:, gcols])
            br = jnp.dot(y_ref[...], bw_ref[i, :, cols], preferred_element_type=F32)
            merged = gate * br if merged is None else merged + gate * br
        mg_ref[:, cols] = merged.astype(BF16)
    for n in range(D // cn):
        cols = slice(n * cn, (n + 1) * cn)
        mix = jnp.dot(mg_ref[...], ow_ref[:, cols], preferred_element_type=F32)
        x1_ref[:, cols] = x_ref[:, cols] + gt1_ref[:, cols] * mix
    h2 = _norm_mod(x1_ref[...], g2_ref[...], sh2_ref[...], sc2_ref[...])
    h2_ref[...] = _pack_bf16_pair(h2[:, :D // 2], h2[:, D // 2:])

    hh = h2.astype(BF16)
    hl = (h2 - hh.astype(F32)).astype(BF16)
    nt = lambda a, b: lax.dot_general(a, b, (((1,), (1,)), ((), ())), preferred_element_type=F32)
    lt = nt(rwh_ref[...], hh) + nt(rwh_ref[...], hl) + nt(rwl_ref[...], hh) + rb_ref[...]
    row8 = lax.broadcasted_iota(I32, (EXP_PER_GROUP, tm), 0)
    gl = jnp.where(row8 < N_GROUPS, lt[N_EXPERTS:N_EXPERTS + EXP_PER_GROUP, :], -jnp.inf)
    gmax = jnp.max(gl, axis=0, keepdims=True)
    g_idx = jnp.min(jnp.where(gl == gmax, row8, EXP_PER_GROUP), axis=0, keepdims=True)
    g_w = 1.0 / jnp.sum(jnp.exp(gl - gmax), axis=0, keepdims=True)
    el = lt[(N_GROUPS - 1) * EXP_PER_GROUP:N_EXPERTS, :]
    for g in range(N_GROUPS - 2, -1, -1):
        el = jnp.where(g_idx == g, lt[g * EXP_PER_GROUP:(g + 1) * EXP_PER_GROUP, :], el)
    e1 = jnp.max(el, axis=0, keepdims=True)
    i1 = jnp.min(jnp.where(el == e1, row8, EXP_PER_GROUP), axis=0, keepdims=True)
    el2 = jnp.where(row8 == i1, -jnp.inf, el)
    e2 = jnp.max(el2, axis=0, keepdims=True)
    i2 = jnp.min(jnp.where(el2 == e2, row8, EXP_PER_GROUP), axis=0, keepdims=True)
    r = jnp.exp(e2 - e1)
    w1 = g_w / (1.0 + r)
    w2 = g_w * r / (1.0 + r)
    eid1 = g_idx * EXP_PER_GROUP + i1
    eid2 = g_idx * EXP_PER_GROUP + i2

    rowe = lax.broadcasted_iota(I32, (N_EXPERTS, tm), 0)
    hit1 = rowe == eid1
    hit2 = rowe == eid2
    onehot = jnp.where(hit1 | hit2, 1.0, 0.0)
    before = (jnp.dot(onehot.astype(BF16), tri_ref[...], preferred_element_type=F32)
              + jnp.concatenate([carry_ref[...]] * (tm // LANES), axis=1))
    rank1 = jnp.sum(jnp.where(hit1, before, 0.0), axis=0, keepdims=True)
    rank2 = jnp.sum(jnp.where(hit2, before, 0.0), axis=0, keepdims=True)
    carry_ref[...] = carry_ref[...] + jnp.sum(onehot, axis=1, keepdims=True)
    cnt_ref[...] = carry_ref[...]

    mi_ref[...] = jnp.where(row8 == 0, eid1,
                  jnp.where(row8 == 1, eid2,
                  jnp.where(row8 == 2, rank1.astype(I32),
                  jnp.where(row8 == 3, rank2.astype(I32), 0))))
    mf_ref[...] = jnp.where(row8 == 0, w1, jnp.where(row8 == 1, w2, 0.0))


def _merge(x2, x_tile0, S, ya, yb, yc, g1, sh1, sc1, gt1, g2, sh2, sc2, gw, gb, bw, ow, rwh, rwl, rb):
    D = x2.shape[1]
    T = ya.shape[0]
    tm = _pick(S, 512)
    per_b = S // tm
    row = lambda i: (i, 0)
    xrow = lambda i: (i + x_tile0, 0)
    col = lambda i: (0, i)
    const = lambda i: (0, 0)
    batch = lambda i: (i // per_b, 0, 0)
    idx = jnp.arange(tm)
    tri = (idx[:, None] < idx[None, :]).astype(BF16)
    vecb = pl.BlockSpec((None, 1, D), batch)
    return pl.pallas_call(
        _merge_kernel,
        grid=(T // tm,),
        in_specs=[pl.BlockSpec((tm, D), xrow),
                  pl.BlockSpec((tm, BRANCH_W), row),
                  pl.BlockSpec((tm, BRANCH_W), row),
                  pl.BlockSpec((tm, BRANCH_W), row),
                  pl.BlockSpec((1, D), const), vecb, vecb, vecb,
                  pl.BlockSpec((1, D), const), vecb, vecb,
                  pl.BlockSpec((D, N_BRANCH * D), const),
                  pl.BlockSpec((1, N_BRANCH * D), const),
                  pl.BlockSpec((N_BRANCH, BRANCH_W, D), lambda i: (0, 0, 0)),
                  pl.BlockSpec((D, D), const),
                  pl.BlockSpec((LANES, D), const),
                  pl.BlockSpec((LANES, D), const),
                  pl.BlockSpec((LANES, 1), const),
                  pl.BlockSpec((tm, tm), const)],
        out_specs=[pl.BlockSpec((tm, D), row), pl.BlockSpec((tm, D // 2), row),
                   pl.BlockSpec((8, tm), col), pl.BlockSpec((8, tm), col),
                   pl.BlockSpec((N_EXPERTS, LANES), const)],
        out_shape=[jax.ShapeDtypeStruct((T, D), F32), jax.ShapeDtypeStruct((T, D // 2), U32),
                   jax.ShapeDtypeStruct((8, T), I32), jax.ShapeDtypeStruct((8, T), F32),
                   jax.ShapeDtypeStruct((N_EXPERTS, LANES), F32)],
        scratch_shapes=[pltpu.VMEM((tm, D), BF16), pltpu.VMEM((tm, D), BF16),
                        pltpu.VMEM((N_EXPERTS, LANES), F32)],
        compiler_params=_cparams(("arbitrary",)),
        name="merge",
    )(x2, ya, yb, yc, g1, sh1, sc1, gt1, g2, sh2, sc2, gw, gb, bw, ow, rwh, rwl, rb, tri)


def _expert_kernel(blk_e_ref, nvalid_ref, xs_ref, w1_ref, w3_ref, w2_ref, ys_ref, w1b_ref, w3b_ref, w2b_ref):
    i = pl.program_id(0)
    valid = i < nvalid_ref[0]
    new_expert = jnp.logical_or(i == 0, blk_e_ref[i] != blk_e_ref[jnp.maximum(i - 1, 0)])

    @pl.when(jnp.logical_and(valid, new_expert))
    def _():
        w1b_ref[...] = w1_ref[...].astype(BF16)
        w3b_ref[...] = w3_ref[...].astype(BF16)
        w2b_ref[...] = w2_ref[...].astype(BF16)

    @pl.when(valid)
    def _():
        half = xs_ref.shape[1]
        x_lo, x_hi = _unpack_bf16_pair(xs_ref[...])
        x_lo = x_lo.astype(BF16)
        x_hi = x_hi.astype(BF16)

        def up(w_ref):
            return (jnp.dot(x_lo, w_ref[:half, :], preferred_element_type=F32)
                    + jnp.dot(x_hi, w_ref[half:, :], preferred_element_type=F32))

        a = up(w1b_ref)
        b = up(w3b_ref)
        hid = (a * jax.nn.sigmoid(a) * b).astype(BF16)
        y = jnp.dot(hid, w2b_ref[...], preferred_element_type=F32)
        ys_ref[...] = _pack_bf16_pair(y[:, :half], y[:, half:])

    @pl.when(jnp.logical_not(valid))
    def _():
        ys_ref[...] = jnp.zeros_like(ys_ref)


def _experts(xs, blk_e, nvalid, w1, w3, w2, layer, tb):
    P, half = xs.shape
    D = 2 * half
    n_blocks = P // tb
    rows = lambda i, be, nv: (jnp.minimum(i, nv[0] - 1), 0)
    grid_spec = pltpu.PrefetchScalarGridSpec(
        num_scalar_prefetch=2,
        grid=(n_blocks,),
        in_specs=[pl.BlockSpec((tb, half), rows),
                  pl.BlockSpec((None, None, D, D_EXPERT), lambda i, be, nv: (layer, be[i], 0, 0)),
                  pl.BlockSpec((None, None, D, D_EXPERT), lambda i, be, nv: (layer, be[i], 0, 0)),
                  pl.BlockSpec((None, None, D_EXPERT, D), lambda i, be, nv: (layer, be[i], 0, 0))],
        out_specs=pl.BlockSpec((tb, half), lambda i, be, nv: (i, 0)),
        scratch_shapes=[pltpu.VMEM((D, D_EXPERT), BF16), pltpu.VMEM((D, D_EXPERT), BF16),
                        pltpu.VMEM((D_EXPERT, D), BF16)],
    )
    return pl.pallas_call(
        _expert_kernel,
        grid_spec=grid_spec,
        out_shape=jax.ShapeDtypeStruct((P, half), U32),
        compiler_params=_cparams(("arbitrary",)),
        name="moe_experts",
    )(blk_e, nvalid, xs, w1, w3, w2)


def _sc_gather(data, idx):
    M = idx.shape[0]
    D = data.shape[1]
    W = SC_GATHER_WINDOW
    assert M % W == 0, (M, W)
    mesh = plsc.VectorSubcoreMesh(core_axis_name="core", subcore_axis_name="subcore")
    n_workers = mesh.num_cores * mesh.num_subcores
    assert M % (W * n_workers) == 0, (M, W, n_workers)

    @functools.partial(pl.kernel, out_type=jax.ShapeDtypeStruct((M, D), data.dtype), mesh=mesh,
                       scratch_types=[pltpu.VMEM((W,), I32), pltpu.VMEM((W, D), data.dtype)])
    def gather_kernel(x_hbm, i_hbm, o_hbm, i_vmem, buf):
        worker = lax.axis_index("core") * mesh.num_subcores + lax.axis_index("subcore")

        @pl.loop(0, M // (W * n_workers))
        def _(t):
            start = (t * n_workers + worker) * W
            pltpu.sync_copy(i_hbm.at[pl.ds(start, W)], i_vmem)
            pltpu.sync_copy(x_hbm.at[i_vmem], buf)
            pltpu.sync_copy(buf, o_hbm.at[pl.ds(start, W)])

    return gather_kernel(data, idx)


def _sc_dispatch(h2, dest, fill_idx, P):
    T, D = h2.shape
    W = SC_GATHER_WINDOW
    n_fill = fill_idx.shape[0]
    mesh = plsc.VectorSubcoreMesh(core_axis_name="core", subcore_axis_name="subcore")
    n_workers = mesh.num_cores * mesh.num_subcores
    assert T % (W * n_workers) == 0 and n_fill % (W * n_workers) == 0, (T, n_fill, W, n_workers)
    zeros = jnp.zeros((W, D), h2.dtype)

    @functools.partial(pl.kernel, out_type=jax.ShapeDtypeStruct((P, D), h2.dtype), mesh=mesh,
                       scratch_types=[pltpu.VMEM((W,), I32), pltpu.VMEM((W, D), h2.dtype)])
    def dispatch_kernel(h_hbm, d_hbm, f_hbm, z_hbm, o_hbm, i_vmem, buf):
        worker = lax.axis_index("core") * mesh.num_subcores + lax.axis_index("subcore")

        @pl.loop(0, T // (W * n_workers))
        def _(t):
            start = (t * n_workers + worker) * W
            pltpu.sync_copy(h_hbm.at[pl.ds(start, W)], buf)
            for c in range(2):
                pltpu.sync_copy(d_hbm.at[pl.ds(c * T + start, W)], i_vmem)
                pltpu.sync_copy(buf, o_hbm.at[i_vmem])

        pltpu.sync_copy(z_hbm, buf)

        @pl.loop(0, n_fill // (W * n_workers))
        def _(t):
            start = (t * n_workers + worker) * W
            pltpu.sync_copy(f_hbm.at[pl.ds(start, W)], i_vmem)
            pltpu.sync_copy(buf, o_hbm.at[i_vmem])

    return dispatch_kernel(h2, dest, fill_idx, zeros)


def _combine_kernel(x_ref, g0_ref, g1_ref, mf_ref, gt_ref, fg_ref, *rest, final):
    o_ref = rest[-1]
    mf = mf_ref[...]
    lo0, hi0 = _unpack_bf16_pair(g0_ref[...])
    lo1, hi1 = _unpack_bf16_pair(g1_ref[...])
    w0 = mf[:, 0:1]
    w1 = mf[:, 1:2]
    ffn = jnp.concatenate([lo0 * w0 + lo1 * w1, hi0 * w0 + hi1 * w1], axis=1)
    out = x_ref[...] + gt_ref[...] * ffn
    if final:
        out = _rms(out, fg_ref[...])
    o_ref[...] = out


def _combine(x1, S, mf, gt2, final_g, g, final, out_rows=None, out_tile0=0, out_prev=None):
    T, D = x1.shape
    tm = _pick(S, 512)
    per_b = S // tm
    nt = T // tm
    in_specs = [pl.BlockSpec((tm, D), lambda i: (i, 0)),
                pl.BlockSpec((tm, D // 2), lambda i: (i, 0)),
                pl.BlockSpec((tm, D // 2), lambda i: (i + nt, 0)),
                pl.BlockSpec((tm, 2), lambda i: (i, 0)),
                pl.BlockSpec((None, 1, D), lambda i: (i // per_b, 0, 0)),
                pl.BlockSpec((1, D), lambda i: (0, 0))]
    args = [x1, g, g, mf, gt2, final_g]
    aliases = {}
    if out_prev is not None:
        in_specs.append(pl.BlockSpec(memory_space=pl.ANY))
        args.append(out_prev)
        aliases = {len(args) - 1: 0}
    return pl.pallas_call(
        functools.partial(_combine_kernel, final=final),
        grid=(nt,),
        in_specs=in_specs,
        out_specs=pl.BlockSpec((tm, D), lambda i: (i + out_tile0, 0)),
        out_shape=jax.ShapeDtypeStruct((T if out_rows is None else out_rows, D), F32),
        input_output_aliases=aliases,
        compiler_params=_cparams(("arbitrary",)),
        name="moe_combine",
    )(*args)


def _prep_w_in(w):
    offs = [0]
    for s in (256, 256, 512, 512, 512, 512, 512, FOX_HEADS, MLA_Q_RANK, MLA_KV_RANK, MLA_ROPE):
        offs.append(offs[-1] + s)
    rq, rk, rv, rg, fq, fk, fv, ff, mq, mkv, mkr = [w[:, offs[i]:offs[i + 1]] for i in range(11)]
    pad = jnp.zeros((w.shape[0], LANES - MLA_ROPE - FOX_HEADS), w.dtype)
    return jnp.concatenate([rq, rk, rv, rg, fq, fk, fv, mq, mkv, mkr, ff, pad], axis=1).astype(BF16)


def _prep_wq_up(w):
    r = w.reshape(MLA_Q_RANK, MLA_HEADS, MLA_NOPE + MLA_ROPE)
    r = jnp.pad(r, ((0, 0), (0, 0), (0, MLA_DQ - MLA_NOPE - MLA_ROPE)))
    return r.reshape(MLA_Q_RANK, MLA_HEADS * MLA_DQ).astype(BF16)


def _prep_router(w_grp, b_grp, w_exp, b_exp):
    D = w_grp.shape[0]
    pad = LANES - N_EXPERTS - N_GROUPS
    rwt = jnp.concatenate([w_exp, w_grp, jnp.zeros((D, pad), F32)], axis=1).astype(F32).T
    rwh = rwt.astype(BF16)
    rwl = (rwt - rwh.astype(F32)).astype(BF16)
    rb = jnp.concatenate([b_exp, b_grp, jnp.zeros((pad,), F32)]).astype(F32).reshape(LANES, 1)
    return rwh, rwl, rb


def kernel(x, c, positions, ada_w, ada_b, norm1_g, norm2_g, w_in, fox_fb, mla_q_norm_g, mla_wq_up, mla_kv_norm_g, mla_wkv_up, gate_w, gate_b, branch_w, out_w, router_grp_w, router_grp_b, router_exp_w, router_exp_b, exp_w1, exp_w3, exp_w2, final_g):
    B, S, D = x.shape
    L = ada_w.shape[0]
    T = B * S
    n_str = N_STREAMS if B % N_STREAMS == 0 else 1
    Bs = B // n_str
    Ts = Bs * S
    As = 2 * Ts
    tb = _pick(As, 512)
    n_blocks = As // tb + N_EXPERTS
    P = n_blocks * tb
    tiles_per_stream = Ts // _pick(S, 512)

    mod = _adaln(c, ada_w, ada_b)
    cos_t, sin_t = _rope_tables(positions)
    cos_s = [cos_t[h * Ts:(h + 1) * Ts] for h in range(n_str)]
    sin_s = [sin_t[h * Ts:(h + 1) * Ts] for h in range(n_str)]
    x_full = x.reshape(T, D)
    xs2 = [x_full] * n_str
    x_tile0 = [h * tiles_per_stream for h in range(n_str)]
    final_g2 = final_g.reshape(1, D)
    r3 = lambda a: a.reshape(Bs, S, -1)
    out = None

    for l in range(L):
        mods = [[mod[l, h * Bs:(h + 1) * Bs, i * D:(i + 1) * D].reshape(Bs, 1, D) for i in range(6)]
                for h in range(n_str)]
        g1 = norm1_g[l].reshape(1, D)
        g2 = norm2_g[l].reshape(1, D)
        w_all = _prep_w_in(w_in[l])
        wq = _prep_wq_up(mla_wq_up[l])
        wkv = mla_wkv_up[l].astype(BF16)
        gq = mla_q_norm_g[l].reshape(1, -1)
        gkv = mla_kv_norm_g[l].reshape(1, -1)
        rwh, rwl, rb = _prep_router(router_grp_w[l], router_grp_b[l], router_exp_w[l], router_exp_b[l])
        gw = gate_w[l].astype(BF16)
        gb = gate_b[l].reshape(1, -1)
        bw = branch_w[l].astype(BF16)
        ow = out_w[l].astype(BF16)
        last = l == L - 1

        merged = []
        for h in range(n_str):
            sh1, sc1, gt1, sh2, sc2, gt2 = mods[h]
            (rq, rk, rv, rg, fq, fk, fv, mq, mk, kpe, mv, ffp) = _proj(
                xs2[h], x_tile0[h], S, g1, sh1, sc1, w_all, cos_s[h], sin_s[h], gq, wq, gkv, wkv)
            ya = _retention(rq, rk, rv, rg, Bs, S).reshape(Ts, -1)
            kb = _fox_gate(ffp, fox_fb[l], S)
            yb = _flash(r3(fq), r3(fk), r3(kb), r3(fv), Bs, S, FOX_HEADS, FOX_DH, True, 3,
                        "flash_fox").reshape(Ts, -1)
            yc = _flash(r3(mq), r3(mk), r3(kpe), r3(mv), Bs, S, MLA_HEADS, MLA_DQ, False, 0,
                        "flash_mla").reshape(Ts, -1)
            merged.append(_merge(xs2[h], x_tile0[h], S, ya, yb, yc, g1, sh1, sc1, gt1, g2, sh2, sc2,
                                 gw, gb, bw, ow, rwh, rwl, rb))

        routed = []
        for h in range(n_str):
            x1, h2, mi, mf, cnt = merged[h]
            counts = cnt[:, 0].astype(I32)
            pcounts = (counts + tb - 1) // tb * tb
            pends = jnp.cumsum(pcounts)
            pstarts = pends - pcounts
            sel = mi[0:2, :, None] == jnp.arange(N_EXPERTS, dtype=I32)
            dest = (jnp.sum(jnp.where(sel, pstarts, 0), axis=-1) + mi[2:4]).reshape(As)
            blk_pos = jnp.arange(n_blocks, dtype=I32) * tb
            blk_e = jnp.minimum(jnp.sum((pends[None, :] <= blk_pos[:, None]).astype(I32), axis=1),
                                N_EXPERTS - 1)
            nvalid = (pends[-1:] // tb).astype(I32)
            fr = jnp.arange(tb, dtype=I32)[None, :]
            is_pad = (fr < (pcounts - counts)[:, None]).reshape(-1)
            pad_slot = ((pstarts + counts)[:, None] + fr).reshape(-1)
            tail_rank = jnp.cumsum(jnp.logical_not(is_pad).astype(I32)) - 1
            fill_idx = jnp.where(is_pad, pad_slot, pends[-1] + tail_rank)
            routed.append((dest, blk_e, nvalid, _sc_dispatch(h2, dest, fill_idx, P)))

        ys = [_experts(routed[h][3], routed[h][1], routed[h][2], exp_w1, exp_w3, exp_w2, l, tb)
              for h in range(n_str)]
        gathered = [_sc_gather(ys[h], routed[h][0]) for h in range(n_str)]
        for h in range(n_str):
            x1, _, _, mf, _ = merged[h]
            gt2 = mods[h][5]
            if last:
                out = _combine(x1, S, mf[0:2].T, gt2, final_g2, gathered[h], True,
                               out_rows=T, out_tile0=h * tiles_per_stream, out_prev=out)
            else:
                xs2[h] = _combine(x1, S, mf[0:2].T, gt2, final_g2, gathered[h], False)
        x_tile0 = [0] * n_str

    return out.reshape(B, S, D)
```

```python
import functools
import math

import jax
import jax.numpy as jnp
from jax import lax
from jax.experimental import pallas as pl
from jax.experimental.pallas import tpu as pltpu
from jax.experimental.pallas import tpu_sc as plsc

F32 = jnp.float32
BF16 = jnp.bfloat16
I32 = jnp.int32
U32 = jnp.uint32
HIGHEST = lax.Precision.HIGHEST

EPS = 1e-6
ROPE_THETA = 10000.0
RET_HEADS = 4
RET_DK = 64
RET_DV = 128
RET_CHUNK = 128
FOX_HEADS = 4
FOX_DH = 128
MLA_HEADS = 4
MLA_Q_RANK = 256
MLA_KV_RANK = 128
MLA_NOPE = 128
MLA_ROPE = 64
MLA_V = 128
MLA_DQ = 256
N_BRANCH = 3
BRANCH_W = 512
N_GROUPS = 4
EXP_PER_GROUP = 8
N_EXPERTS = N_GROUPS * EXP_PER_GROUP
D_EXPERT = 512

LANES = 128
V7X_VMEM_LIMIT = 56 * 1024 * 1024

C_RQ, C_RK, C_RV, C_RG = 0, 256, 512, 1024
C_FQ, C_FK, C_FV = 1536, 2048, 2560
C_MQ, C_MKV, C_TAIL = 3072, 3328, 3456
D_IN_PAD = 3584
FF_LANE = MLA_ROPE

NEG_BIG = -1e30

TOKEN_TILE = 512
ATTN_TILE = 512
EXPERT_BLOCK = 512
RET_CHUNKS_PER_STEP = 32
ADALN_COL_TILE = 1536
ROPE_TABLE_TILE = 2048
SC_GATHER_WINDOW = 128
N_STREAMS = 1
MERGE_COL_CHUNK = 256
LOG2E = math.log2(math.e)


def _cparams(sem):
    return pltpu.CompilerParams(dimension_semantics=sem, vmem_limit_bytes=V7X_VMEM_LIMIT)


def _pick(n, pref):
    t = min(n, pref)
    assert n % t == 0, (n, t)
    return t


def _adaln_kernel(c_ref, w_ref, b_ref, o_ref):
    c = c_ref[...]
    ca = c * jax.nn.sigmoid(c)
    o_ref[...] = jnp.dot(ca, w_ref[...], preferred_element_type=F32, precision=HIGHEST) + b_ref[...]


def _adaln(c, ada_w, ada_b):
    L, D, N = ada_w.shape
    B = c.shape[0]
    tn = _pick(N, ADALN_COL_TILE)
    return pl.pallas_call(
        _adaln_kernel,
        grid=(L, N // tn),
        in_specs=[pl.BlockSpec((B, D), lambda l, j: (0, 0)),
                  pl.BlockSpec((None, D, tn), lambda l, j: (l, 0, j)),
                  pl.BlockSpec((None, 1, tn), lambda l, j: (l, 0, j))],
        out_specs=pl.BlockSpec((None, B, tn), lambda l, j: (l, 0, j)),
        out_shape=jax.ShapeDtypeStruct((L, B, N), F32),
        compiler_params=_cparams(("arbitrary", "arbitrary")),
        name="adaln",
    )(c, ada_w, ada_b.reshape(L, 1, N))


def _rope_table_kernel(pos_ref, inv_ref, sign_ref, cos_ref, sin_ref):
    ang = pos_ref[...].astype(F32) * inv_ref[...]
    cos_ref[...] = jnp.cos(ang)
    sin_ref[...] = jnp.sin(ang) * sign_ref[...]


def _rope_tables(positions):
    T = positions.size
    tm = _pick(T, ROPE_TABLE_TILE)
    half = MLA_ROPE // 2
    inv = ROPE_THETA ** (-jnp.arange(0, MLA_ROPE, 2, dtype=F32) / MLA_ROPE)
    inv_t = jnp.tile(inv, LANES // half).reshape(1, LANES)
    sign = jnp.where((jnp.arange(LANES) % MLA_ROPE) < half, -1.0, 1.0).astype(F32).reshape(1, LANES)
    return pl.pallas_call(
        _rope_table_kernel,
        grid=(T // tm,),
        in_specs=[pl.BlockSpec((tm, 1), lambda i: (i, 0)),
                  pl.BlockSpec((1, LANES), lambda i: (0, 0)),
                  pl.BlockSpec((1, LANES), lambda i: (0, 0))],
        out_specs=[pl.BlockSpec((tm, LANES), lambda i: (i, 0))] * 2,
        out_shape=[jax.ShapeDtypeStruct((T, LANES), F32)] * 2,
        compiler_params=_cparams(("arbitrary",)),
        name="rope_tables",
    )(positions.reshape(T, 1), inv_t, sign)


def _rope_slab(x, cos_t, sin_t, lane):
    nxt = pltpu.roll(x, LANES - 32, axis=1)
    prv = pltpu.roll(x, 32, axis=1)
    swapped = jnp.where((lane & 32) == 0, nxt, prv)
    return x * cos_t + swapped * sin_t


def _pack_bf16_pair(lo, hi):
    lo_bits = lax.shift_right_logical(lax.bitcast_convert_type(lo.astype(BF16).astype(F32), U32), jnp.uint32(16))
    hi_bits = lax.bitcast_convert_type(hi.astype(BF16).astype(F32), U32) & jnp.uint32(0xFFFF0000)
    return hi_bits | lo_bits


def _unpack_bf16_pair(w):
    lo = lax.bitcast_convert_type(lax.shift_left(w, jnp.uint32(16)), F32)
    hi = lax.bitcast_convert_type(w & jnp.uint32(0xFFFF0000), F32)
    return lo, hi


def _norm_mod(x, g, shift, scale):
    y = x * lax.rsqrt(jnp.mean(x * x, axis=-1, keepdims=True) + EPS)
    return (y * g) * (1.0 + scale) + shift


def _rms(x, g):
    return x * lax.rsqrt(jnp.mean(x * x, axis=-1, keepdims=True) + EPS) * g


def _proj_kernel(x_ref, g_ref, sh_ref, sc_ref, w_ref, cos_ref, sin_ref,
                 gq_ref, wq_ref, gkv_ref, wkv_ref,
                 rq_ref, rk_ref, rv_ref, rg_ref, fq_ref, fk_ref, fv_ref,
                 mq_ref, mk_ref, kpe_ref, mv_ref, ff_ref):
    h = _norm_mod(x_ref[...], g_ref[...], sh_ref[...], sc_ref[...]).astype(BF16)
    cos_t = cos_ref[...]
    sin_t = sin_ref[...]
    lane = lax.broadcasted_iota(I32, cos_t.shape, 1)

    def proj(c0, width):
        return jnp.dot(h, w_ref[:, c0:c0 + width], preferred_element_type=F32)

    tail = proj(C_TAIL, LANES)
    ff_ref[...] = tail
    kpe_ref[...] = jnp.where(lane < MLA_ROPE, _rope_slab(tail, cos_t, sin_t, lane), 0.0).astype(BF16)

    qn = _rms(proj(C_MQ, MLA_Q_RANK), gq_ref[...]).astype(BF16)
    qh = jnp.dot(qn, wq_ref[...], preferred_element_type=F32)
    q_scale = (MLA_NOPE + MLA_ROPE) ** -0.5 * LOG2E
    for hd in range(MLA_HEADS):
        c0 = hd * MLA_DQ
        mq_ref[:, c0:c0 + LANES] = (qh[:, c0:c0 + LANES] * q_scale).astype(BF16)
        pe = _rope_slab(qh[:, c0 + LANES:c0 + 2 * LANES], cos_t, sin_t, lane)
        mq_ref[:, c0 + LANES:c0 + 2 * LANES] = jnp.where(lane < MLA_ROPE, pe * q_scale, 0.0).astype(BF16)

    kvn = _rms(proj(C_MKV, MLA_KV_RANK), gkv_ref[...]).astype(BF16)
    kvh = jnp.dot(kvn, wkv_ref[...], preferred_element_type=F32)
    for hd in range(MLA_HEADS):
        c0 = hd * (MLA_NOPE + MLA_V)
        mk_ref[:, hd * MLA_NOPE:(hd + 1) * MLA_NOPE] = kvh[:, c0:c0 + MLA_NOPE].astype(BF16)
        mv_ref[:, hd * MLA_V:(hd + 1) * MLA_V] = kvh[:, c0 + MLA_NOPE:c0 + MLA_NOPE + MLA_V].astype(BF16)

    rq = proj(C_RQ, 256)
    rk = proj(C_RK, 256)
    for s in range(2):
        sl = slice(s * LANES, (s + 1) * LANES)
        q2 = _rope_slab(rq[:, sl], cos_t, sin_t, lane)
        k2 = _rope_slab(rk[:, sl], cos_t, sin_t, lane) * (RET_DK ** -0.5)
        for half in range(2):
            mine = (lane < RET_DK) if half == 0 else (lane >= RET_DK)
            hs = slice((2 * s + half) * LANES, (2 * s + half + 1) * LANES)
            rq_ref[:, hs] = jnp.where(mine, q2, 0.0).astype(BF16)
            rk_ref[:, hs] = jnp.where(mine, k2, 0.0).astype(BF16)
    rv_ref[...] = proj(C_RV, 512).astype(BF16)
    rg_ref[...] = proj(C_RG, 512).astype(BF16)
    fq_ref[...] = (proj(C_FQ, 512) * (FOX_DH ** -0.5 * LOG2E)).astype(BF16)
    fk_ref[...] = proj(C_FK, 512).astype(BF16)
    fv_ref[...] = proj(C_FV, 512).astype(BF16)


def _proj(x2, x_tile0, S, g, sh, sc, w_all, cos_t, sin_t, gq, wq, gkv, wkv):
    D = x2.shape[1]
    T = cos_t.shape[0]
    tm = _pick(S, TOKEN_TILE)
    per_b = S // tm
    row = lambda i: (i, 0)
    xrow = lambda i: (i + x_tile0, 0)
    const = lambda i: (0, 0)
    batch = lambda i: (i // per_b, 0, 0)
    widths = [512, 512, 512, 512, 512, 512, 512, MLA_HEADS * MLA_DQ, MLA_HEADS * MLA_NOPE, LANES,
              MLA_HEADS * MLA_V]
    out_shape = [jax.ShapeDtypeStruct((T, w), BF16) for w in widths]
    out_shape.append(jax.ShapeDtypeStruct((T, LANES), F32))
    out_specs = [pl.BlockSpec((tm, w), row) for w in widths] + [pl.BlockSpec((tm, LANES), row)]
    return pl.pallas_call(
        _proj_kernel,
        grid=(T // tm,),
        in_specs=[pl.BlockSpec((tm, D), xrow),
                  pl.BlockSpec((1, D), const),
                  pl.BlockSpec((None, 1, D), batch),
                  pl.BlockSpec((None, 1, D), batch),
                  pl.BlockSpec((D, D_IN_PAD), const),
                  pl.BlockSpec((tm, LANES), row),
                  pl.BlockSpec((tm, LANES), row),
                  pl.BlockSpec((1, MLA_Q_RANK), const),
                  pl.BlockSpec((MLA_Q_RANK, MLA_HEADS * MLA_DQ), const),
                  pl.BlockSpec((1, MLA_KV_RANK), const),
                  pl.BlockSpec((MLA_KV_RANK, MLA_HEADS * (MLA_NOPE + MLA_V)), const)],
        out_specs=out_specs,
        out_shape=out_shape,
        compiler_params=_cparams(("arbitrary",)),
        name="proj",
    )(x2, g, sh, sc, w_all, cos_t, sin_t, gq, wq, gkv, wkv)


def _split3(x):
    a = x.astype(BF16)
    r = x - a.astype(F32)
    b = r.astype(BF16)
    c = (r - b.astype(F32)).astype(BF16)
    return a, b, c


def _fox_gate_kernel(ff_ref, fb_ref, tri_ref, kb_ref):
    tm = tri_ref.shape[0]
    lane = lax.broadcasted_iota(I32, (tm, LANES), 1)
    tri = tri_ref[...]
    carry = jnp.zeros((1, LANES), F32)
    for t in range(ff_ref.shape[0] // tm):
        rows = slice(t * tm, (t + 1) * tm)
        z = ff_ref[rows, :] + fb_ref[...]
        ls = -(jnp.maximum(-z, 0.0) + jnp.log1p(jnp.exp(-jnp.abs(z))))
        ls = jnp.where((lane >= FF_LANE) & (lane < FF_LANE + FOX_HEADS), ls, 0.0)
        f = carry
        for part in _split3(ls):
            f = f + jnp.dot(tri, part, preferred_element_type=F32)
        carry = f[tm - 1:tm, :]
        hi, mid, lo = [part.astype(F32) for part in _split3(f * LOG2E)]
        for hd in range(FOX_HEADS):
            src = FF_LANE + hd
            slab = jnp.where(lane == 0, pltpu.roll(hi, (LANES - src) % LANES, axis=1),
                   jnp.where(lane == 1, pltpu.roll(mid, (LANES + 1 - src) % LANES, axis=1),
                   jnp.where(lane == 2, pltpu.roll(lo, (LANES + 2 - src) % LANES, axis=1), 0.0)))
            kb_ref[rows, hd * LANES:(hd + 1) * LANES] = slab.astype(BF16)


def _fox_gate(ffp, fb, S):
    T = ffp.shape[0]
    tm = _pick(S, TOKEN_TILE)
    idx = jnp.arange(tm)
    tri = (idx[None, :] <= idx[:, None]).astype(BF16)
    fbv = jnp.zeros((1, LANES), F32).at[0, FF_LANE:FF_LANE + FOX_HEADS].set(fb)
    return pl.pallas_call(
        _fox_gate_kernel,
        grid=(T // S,),
        in_specs=[pl.BlockSpec((S, LANES), lambda b: (b, 0)),
                  pl.BlockSpec((1, LANES), lambda b: (0, 0)),
                  pl.BlockSpec((tm, tm), lambda b: (0, 0))],
        out_specs=pl.BlockSpec((S, FOX_HEADS * LANES), lambda b: (b, 0)),
        out_shape=jax.ShapeDtypeStruct((T, FOX_HEADS * LANES), BF16),
        compiler_params=_cparams(("arbitrary",)),
        name="fox_gate",
    )(ffp, fbv, tri)


def _retention_kernel(dchunk_ref, q_ref, k_ref, v_ref, g_ref, dmask_ref, din_ref, dout_ref,
                      o_ref, state_ref, *, n_chunks):
    @pl.when(pl.program_id(1) == 0)
    def _():
        state_ref[...] = jnp.zeros_like(state_ref)

    C = RET_CHUNK
    for ci in range(n_chunks):
        rows = slice(ci * C, (ci + 1) * C)
        for hd in range(RET_HEADS):
            slab = slice(hd * LANES, (hd + 1) * LANES)
            q = q_ref[rows, slab]
            k = k_ref[rows, slab]
            vcols = slice(hd * RET_DV, (hd + 1) * RET_DV)
            v = v_ref[rows, vcols]
            state = state_ref[hd]
            scores = lax.dot_general(q, k, (((1,), (1,)), ((), ())),
                                     preferred_element_type=F32) * dmask_ref[hd]
            inner = jnp.dot(scores.astype(BF16), v, preferred_element_type=F32)
            cross = jnp.dot(q, state.astype(BF16), preferred_element_type=F32) * din_ref[hd]
            o = inner + cross
            vd = (v.astype(F32) * dout_ref[hd]).astype(BF16)
            kv = lax.dot_general(k, vd, (((0,), (0,)), ((), ())), preferred_element_type=F32)
            state_ref[hd] = state * dchunk_ref[hd] + kv
            mu = jnp.mean(o, axis=-1, keepdims=True)
            d = o - mu
            var = jnp.mean(d * d, axis=-1, keepdims=True)
            on = d * lax.rsqrt(var + EPS)
            g = g_ref[rows, vcols].astype(F32)
            o_ref[rows, vcols] = (g * jax.nn.sigmoid(g) * on).astype(BF16)


def _retention(rq, rk, rv, rg, B, S):
    H, C = RET_HEADS, RET_CHUNK
    tr = _pick(S, RET_CHUNKS_PER_STEP * C)
    n_chunks = tr // C
    log_gamma = jnp.log1p(-jnp.exp2(-5.0 - jnp.arange(H, dtype=F32)))
    idx = jnp.arange(C, dtype=F32)
    rel = idx[:, None] - idx[None, :]
    dmask = jnp.where(rel >= 0, jnp.exp(log_gamma[:, None, None] * jnp.maximum(rel, 0.0)), 0.0)
    decay_in = jnp.exp(log_gamma[:, None] * (idx + 1.0))
    decay_out = jnp.exp(log_gamma[:, None] * (C - 1.0 - idx))
    decay_chunk = jnp.exp(log_gamma * C)
    din = jnp.broadcast_to(decay_in[:, :, None], (H, C, RET_DV))
    dout = jnp.broadcast_to(decay_out[:, :, None], (H, C, RET_DV))
    tok = lambda b, i: (b, i, 0)
    const3 = lambda b, i: (0, 0, 0)
    return pl.pallas_call(
        functools.partial(_retention_kernel, n_chunks=n_chunks),
        grid=(B, S // tr),
        in_specs=[pl.BlockSpec(memory_space=pltpu.SMEM),
                  pl.BlockSpec((None, tr, H * LANES), tok),
                  pl.BlockSpec((None, tr, H * LANES), tok),
                  pl.BlockSpec((None, tr, H * RET_DV), tok),
                  pl.BlockSpec((None, tr, H * RET_DV), tok),
                  pl.BlockSpec((H, C, C), const3),
                  pl.BlockSpec((H, C, RET_DV), const3),
                  pl.BlockSpec((H, C, RET_DV), const3)],
        out_specs=pl.BlockSpec((None, tr, H * RET_DV), tok),
        out_shape=jax.ShapeDtypeStruct((B, S, H * RET_DV), BF16),
        scratch_shapes=[pltpu.VMEM((H, LANES, RET_DV), F32)],
        compiler_params=_cparams(("arbitrary", "arbitrary")),
        name="retention",
    )(decay_chunk, rq.reshape(B, S, -1), rk.reshape(B, S, -1), rv.reshape(B, S, -1),
      rg.reshape(B, S, -1), dmask, din, dout)


def _flash_kernel(q_ref, k_ref, ke_ref, v_ref, o_ref, m_ref, l_ref, acc_ref, s_ref, *, tq, q_bias_cols):
    S = q_ref.shape[0]
    nq = S // tq
    tk = tq
    m_ref[...] = jnp.full(m_ref.shape, NEG_BIG, F32)
    l_ref[...] = jnp.zeros_like(l_ref)
    acc_ref[...] = jnp.zeros_like(acc_ref)
    lane = lax.broadcasted_iota(I32, (tq, LANES), 1)

    half = tk // 2
    tri = (lax.broadcasted_iota(I32, (half, half), 0) <= lax.broadcasted_iota(I32, (half, half), 1))
    nt = lambda a, b: lax.dot_general(a, b, (((1,), (1,)), ((), ())), preferred_element_type=F32)

    def issue(n):
        i, j = pairs[n]
        q = q_ref[i * tq:(i + 1) * tq, :]
        if q_bias_cols:
            q = jnp.concatenate([q, jnp.where(lane < q_bias_cols, -1.0, 0.0).astype(BF16)], axis=1)
        kj = jnp.concatenate([k_ref[j * tk:(j + 1) * tk, :], ke_ref[j * tk:(j + 1) * tk, :]], axis=1)
        if j < i:
            s_ref[n % 2] = nt(kj, q)
        else:
            s_ref[n % 2, 0:half, :] = nt(kj[0:half], q)
            s_ref[n % 2, half:, half:] = nt(kj[half:], q[half:])

    def update(s, i, c0, c1, r0, r1):
        m_prev = m_ref[i, :, c0:c1]
        m_new = jnp.maximum(m_prev, jnp.max(s, axis=0, keepdims=True))
        alpha = jnp.exp2(m_prev - m_new)
        p = jnp.exp2(s - m_new)
        l_ref[i, :, c0:c1] = alpha * l_ref[i, :, c0:c1] + jnp.sum(p, axis=0, keepdims=True)
        acc_ref[i, :, c0:c1] = alpha * acc_ref[i, :, c0:c1] + lax.dot_general(
            v_ref[r0:r1, :], p.astype(BF16), (((0,), (0,)), ((), ())),
            preferred_element_type=F32)
        m_ref[i, :, c0:c1] = m_new

    def consume(n):
        i, j = pairs[n]
        slot = n % 2
        if j < i:
            update(s_ref[slot], i, 0, tq, j * tk, (j + 1) * tk)
        else:
            left = jnp.where(tri, s_ref[slot, 0:half, 0:half], NEG_BIG)
            right = jnp.concatenate([s_ref[slot, 0:half, half:],
                                     jnp.where(tri, s_ref[slot, half:, half:], NEG_BIG)], axis=0)
            update(left, i, 0, half, j * tk, j * tk + half)
            update(right, i, half, tq, j * tk, (j + 1) * tk)
            o_ref[i * tq:(i + 1) * tq, :] = (acc_ref[i] / l_ref[i]).T.astype(o_ref.dtype)

    pairs = [(i, j) for i in range(nq) for j in range(i + 1)]
    issue(0)
    for n in range(len(pairs)):
        if n + 1 < len(pairs):
            issue(n + 1)
        consume(n)


def _flash(q, k, ke, v, B, S, H, dq, ke_per_head, q_bias_cols, name):
    tq = _pick(S, ATTN_TILE)
    nq = S // tq
    ke_map = (lambda b, h: (b, 0, h)) if ke_per_head else (lambda b, h: (b, 0, 0))
    return pl.pallas_call(
        functools.partial(_flash_kernel, tq=tq, q_bias_cols=q_bias_cols),
        grid=(B, H),
        in_specs=[pl.BlockSpec((None, S, dq), lambda b, h: (b, 0, h)),
                  pl.BlockSpec((None, S, LANES), lambda b, h: (b, 0, h)),
                  pl.BlockSpec((None, S, LANES), ke_map),
                  pl.BlockSpec((None, S, LANES), lambda b, h: (b, 0, h))],
        out_specs=pl.BlockSpec((None, S, LANES), lambda b, h: (b, 0, h)),
        out_shape=jax.ShapeDtypeStruct((B, S, H * LANES), BF16),
        scratch_shapes=[pltpu.VMEM((nq, 1, tq), F32), pltpu.VMEM((nq, 1, tq), F32),
                        pltpu.VMEM((nq, LANES, tq), F32), pltpu.VMEM((2, tq, tq), F32)],
        compiler_params=_cparams(("arbitrary", "arbitrary")),
        name=name,
    )(q, k, ke, v)


def _merge_kernel(x_ref, ya_ref, yb_ref, yc_ref, g1_ref, sh1_ref, sc1_ref, gt1_ref,
                  g2_ref, sh2_ref, sc2_ref, gw_ref, gb_ref, bw_ref, ow_ref,
                  rwh_ref, rwl_ref, rb_ref, tri_ref,
                  x1_ref, h2_ref, mi_ref, mf_ref, cnt_ref, h_ref, mg_ref, carry_ref):
    tm, D = x_ref.shape
    cn = MERGE_COL_CHUNK

    @pl.when(pl.program_id(0) == 0)
    def _():
        carry_ref[...] = jnp.zeros_like(carry_ref)

    h_ref[...] = _norm_mod(x_ref[...], g1_ref[...], sh1_ref[...], sc1_ref[...]).astype(BF16)
    for n in range(D // cn):
        cols = slice(n * cn, (n + 1) * cn)
        merged = None
        for i, y_ref in enumerate((ya_ref, yb_ref, yc_ref)):
            gcols = slice(i * D + n * cn, i * D + (n + 1) * cn)
            gate = jax.nn.sigmoid(jnp.dot(h_ref[...], gw_ref[:, gcols], preferred_element_type=F32)
                                  + gb_ref[:, gcols])
            br = jnp.dot(y_ref[...], bw_ref[i, :, cols], preferred_element_type=F32)
            merged = gate * br if merged is None else merged + gate * br
        mg_ref[:, cols] = merged.astype(BF16)
    for n in range(D // cn):
        cols = slice(n * cn, (n + 1) * cn)
        mix = jnp.dot(mg_ref[...], ow_ref[:, cols], preferred_element_type=F32)
        x1_ref[:, cols] = x_ref[:, cols] + gt1_ref[:, cols] * mix
    h2 = _norm_mod(x1_ref[...], g2_ref[...], sh2_ref[...], sc2_ref[...])
    h2_ref[...] = _pack_bf16_pair(h2[:, :D // 2], h2[:, D // 2:])

    hh = h2.astype(BF16)
    hl = (h2 - hh.astype(F32)).astype(BF16)
    nt = lambda a, b: lax.dot_general(a, b, (((1,), (1,)), ((), ())), preferred_element_type=F32)
    lt = nt(rwh_ref[...], hh) + nt(rwh_ref[...], hl) + nt(rwl_ref[...], hh) + rb_ref[...]
    row8 = lax.broadcasted_iota(I32, (EXP_PER_GROUP, tm), 0)
    gl = jnp.where(row8 < N_GROUPS, lt[N_EXPERTS:N_EXPERTS + EXP_PER_GROUP, :], -jnp.inf)
    gmax = jnp.max(gl, axis=0, keepdims=True)
    g_idx = jnp.min(jnp.where(gl == gmax, row8, EXP_PER_GROUP), axis=0, keepdims=True)
    g_w = 1.0 / jnp.sum(jnp.exp(gl - gmax), axis=0, keepdims=True)
    el = lt[(N_GROUPS - 1) * EXP_PER_GROUP:N_EXPERTS, :]
    for g in range(N_GROUPS - 2, -1, -1):
        el = jnp.where(g_idx == g, lt[g * EXP_PER_GROUP:(g + 1) * EXP_PER_GROUP, :], el)
    e1 = jnp.max(el, axis=0, keepdims=True)
    i1 = jnp.min(jnp.where(el == e1, row8, EXP_PER_GROUP), axis=0, keepdims=True)
    el2 = jnp.where(row8 == i1, -jnp.inf, el)
    e2 = jnp.max(el2, axis=0, keepdims=True)
    i2 = jnp.min(jnp.where(el2 == e2, row8, EXP_PER_GROUP), axis=0, keepdims=True)
    r = jnp.exp(e2 - e1)
    w1 = g_w / (1.0 + r)
    w2 = g_w * r / (1.0 + r)
    eid1 = g_idx * EXP_PER_GROUP + i1
    eid2 = g_idx * EXP_PER_GROUP + i2

    rowe = lax.broadcasted_iota(I32, (N_EXPERTS, tm), 0)
    hit1 = rowe == eid1
    hit2 = rowe == eid2
    onehot = jnp.where(hit1 | hit2, 1.0, 0.0)
    before = (jnp.dot(onehot.astype(BF16), tri_ref[...], preferred_element_type=F32)
              + jnp.concatenate([carry_ref[...]] * (tm // LANES), axis=1))
    rank1 = jnp.sum(jnp.where(hit1, before, 0.0), axis=0, keepdims=True)
    rank2 = jnp.sum(jnp.where(hit2, before, 0.0), axis=0, keepdims=True)
    carry_ref[...] = carry_ref[...] + jnp.sum(onehot, axis=1, keepdims=True)
    cnt_ref[...] = carry_ref[...]

    mi_ref[...] = jnp.where(row8 == 0, eid1,
                  jnp.where(row8 == 1, eid2,
                  jnp.where(row8 == 2, rank1.astype(I32),
                  jnp.where(row8 == 3, rank2.astype(I32), 0))))
    mf_ref[...] = jnp.where(row8 == 0, w1, jnp.where(row8 == 1, w2, 0.0))


def _merge(x2, x_tile0, S, ya, yb, yc, g1, sh1, sc1, gt1, g2, sh2, sc2, gw, gb, bw, ow, rwh, rwl, rb):
    D = x2.shape[1]
    T = ya.shape[0]
    tm = _pick(S, TOKEN_TILE)
    per_b = S // tm
    row = lambda i: (i, 0)
    xrow = lambda i: (i + x_tile0, 0)
    col = lambda i: (0, i)
    const = lambda i: (0, 0)
    batch = lambda i: (i // per_b, 0, 0)
    idx = jnp.arange(tm)
    tri = (idx[:, None] < idx[None, :]).astype(BF16)
    vecb = pl.BlockSpec((None, 1, D), batch)
    return pl.pallas_call(
        _merge_kernel,
        grid=(T // tm,),
        in_specs=[pl.BlockSpec((tm, D), xrow),
                  pl.BlockSpec((tm, BRANCH_W), row),
                  pl.BlockSpec((tm, BRANCH_W), row),
                  pl.BlockSpec((tm, BRANCH_W), row),
                  pl.BlockSpec((1, D), const), vecb, vecb, vecb,
                  pl.BlockSpec((1, D), const), vecb, vecb,
                  pl.BlockSpec((D, N_BRANCH * D), const),
                  pl.BlockSpec((1, N_BRANCH * D), const),
                  pl.BlockSpec((N_BRANCH, BRANCH_W, D), lambda i: (0, 0, 0)),
                  pl.BlockSpec((D, D), const),
                  pl.BlockSpec((LANES, D), const),
                  pl.BlockSpec((LANES, D), const),
                  pl.BlockSpec((LANES, 1), const),
                  pl.BlockSpec((tm, tm), const)],
        out_specs=[pl.BlockSpec((tm, D), row), pl.BlockSpec((tm, D // 2), row),
                   pl.BlockSpec((8, tm), col), pl.BlockSpec((8, tm), col),
                   pl.BlockSpec((N_EXPERTS, LANES), const)],
        out_shape=[jax.ShapeDtypeStruct((T, D), F32), jax.ShapeDtypeStruct((T, D // 2), U32),
                   jax.ShapeDtypeStruct((8, T), I32), jax.ShapeDtypeStruct((8, T), F32),
                   jax.ShapeDtypeStruct((N_EXPERTS, LANES), F32)],
        scratch_shapes=[pltpu.VMEM((tm, D), BF16), pltpu.VMEM((tm, D), BF16),
                        pltpu.VMEM((N_EXPERTS, LANES), F32)],
        compiler_params=_cparams(("arbitrary",)),
        name="merge",
    )(x2, ya, yb, yc, g1, sh1, sc1, gt1, g2, sh2, sc2, gw, gb, bw, ow, rwh, rwl, rb, tri)


def _expert_kernel(blk_e_ref, nvalid_ref, xs_ref, w1_ref, w3_ref, w2_ref, ys_ref, w1b_ref, w3b_ref, w2b_ref):
    i = pl.program_id(0)
    valid = i < nvalid_ref[0]
    new_expert = jnp.logical_or(i == 0, blk_e_ref[i] != blk_e_ref[jnp.maximum(i - 1, 0)])

    @pl.when(jnp.logical_and(valid, new_expert))
    def _():
        w1b_ref[...] = w1_ref[...].astype(BF16)
        w3b_ref[...] = w3_ref[...].astype(BF16)
        w2b_ref[...] = w2_ref[...].astype(BF16)

    @pl.when(valid)
    def _():
        half = xs_ref.shape[1]
        x_lo, x_hi = _unpack_bf16_pair(xs_ref[...])
        x_lo = x_lo.astype(BF16)
        x_hi = x_hi.astype(BF16)

        def up(w_ref):
            return (jnp.dot(x_lo, w_ref[:half, :], preferred_element_type=F32)
                    + jnp.dot(x_hi, w_ref[half:, :], preferred_element_type=F32))

        a = up(w1b_ref)
        b = up(w3b_ref)
        hid = (a * jax.nn.sigmoid(a) * b).astype(BF16)
        y = jnp.dot(hid, w2b_ref[...], preferred_element_type=F32)
        ys_ref[...] = _pack_bf16_pair(y[:, :half], y[:, half:])

    @pl.when(jnp.logical_not(valid))
    def _():
        ys_ref[...] = jnp.zeros_like(ys_ref)


def _experts(xs, blk_e, nvalid, w1, w3, w2, layer, tb):
    P, half = xs.shape
    D = 2 * half
    n_blocks = P // tb
    rows = lambda i, be, nv: (jnp.minimum(i, nv[0] - 1), 0)
    grid_spec = pltpu.PrefetchScalarGridSpec(
        num_scalar_prefetch=2,
        grid=(n_blocks,),
        in_specs=[pl.BlockSpec((tb, half), rows),
                  pl.BlockSpec((None, None, D, D_EXPERT), lambda i, be, nv: (layer, be[i], 0, 0)),
                  pl.BlockSpec((None, None, D, D_EXPERT), lambda i, be, nv: (layer, be[i], 0, 0)),
                  pl.BlockSpec((None, None, D_EXPERT, D), lambda i, be, nv: (layer, be[i], 0, 0))],
        out_specs=pl.BlockSpec((tb, half), lambda i, be, nv: (i, 0)),
        scratch_shapes=[pltpu.VMEM((D, D_EXPERT), BF16), pltpu.VMEM((D, D_EXPERT), BF16),
                        pltpu.VMEM((D_EXPERT, D), BF16)],
    )
    return pl.pallas_call(
        _expert_kernel,
        grid_spec=grid_spec,
        out_shape=jax.ShapeDtypeStruct((P, half), U32),
        compiler_params=_cparams(("arbitrary",)),
        name="moe_experts",
    )(blk_e, nvalid, xs, w1, w3, w2)


def _sc_gather(data, idx):
    M = idx.shape[0]
    D = data.shape[1]
    W = SC_GATHER_WINDOW
    assert M % W == 0, (M, W)
    mesh = plsc.VectorSubcoreMesh(core_axis_name="core", subcore_axis_name="subcore")
    n_workers = mesh.num_cores * mesh.num_subcores
    assert M % (W * n_workers) == 0, (M, W, n_workers)

    @functools.partial(pl.kernel, out_type=jax.ShapeDtypeStruct((M, D), data.dtype), mesh=mesh,
                       scratch_types=[pltpu.VMEM((W,), I32), pltpu.VMEM((W, D), data.dtype)])
    def gather_kernel(x_hbm, i_hbm, o_hbm, i_vmem, buf):
        worker = lax.axis_index("core") * mesh.num_subcores + lax.axis_index("subcore")

        @pl.loop(0, M // (W * n_workers))
        def _(t):
            start = (t * n_workers + worker) * W
            pltpu.sync_copy(i_hbm.at[pl.ds(start, W)], i_vmem)
            pltpu.sync_copy(x_hbm.at[i_vmem], buf)
            pltpu.sync_copy(buf, o_hbm.at[pl.ds(start, W)])

    return gather_kernel(data, idx)


def _sc_dispatch(h2, dest, fill_idx, P):
    T, D = h2.shape
    W = SC_GATHER_WINDOW
    n_fill = fill_idx.shape[0]
    mesh = plsc.VectorSubcoreMesh(core_axis_name="core", subcore_axis_name="subcore")
    n_workers = mesh.num_cores * mesh.num_subcores
    assert T % (W * n_workers) == 0 and n_fill % (W * n_workers) == 0, (T, n_fill, W, n_workers)
    zeros = jnp.zeros((W, D), h2.dtype)

    @functools.partial(pl.kernel, out_type=jax.ShapeDtypeStruct((P, D), h2.dtype), mesh=mesh,
                       scratch_types=[pltpu.VMEM((W,), I32), pltpu.VMEM((W, D), h2.dtype)])
    def dispatch_kernel(h_hbm, d_hbm, f_hbm, z_hbm, o_hbm, i_vmem, buf):
        worker = lax.axis_index("core") * mesh.num_subcores + lax.axis_index("subcore")

        @pl.loop(0, T // (W * n_workers))
        def _(t):
            start = (t * n_workers + worker) * W
            pltpu.sync_copy(h_hbm.at[pl.ds(start, W)], buf)
            for c in range(2):
                pltpu.sync_copy(d_hbm.at[pl.ds(c * T + start, W)], i_vmem)
                pltpu.sync_copy(buf, o_hbm.at[i_vmem])

        pltpu.sync_copy(z_hbm, buf)

        @pl.loop(0, n_fill // (W * n_workers))
        def _(t):
            start = (t * n_workers + worker) * W
            pltpu.sync_copy(f_hbm.at[pl.ds(start, W)], i_vmem)
            pltpu.sync_copy(buf, o_hbm.at[i_vmem])

    return dispatch_kernel(h2, dest, fill_idx, zeros)


def _combine_kernel(x_ref, g0_ref, g1_ref, mf_ref, gt_ref, fg_ref, *rest, final):
    o_ref = rest[-1]
    mf = mf_ref[...]
    lo0, hi0 = _unpack_bf16_pair(g0_ref[...])
    lo1, hi1 = _unpack_bf16_pair(g1_ref[...])
    w0 = mf[:, 0:1]
    w1 = mf[:, 1:2]
    ffn = jnp.concatenate([lo0 * w0 + lo1 * w1, hi0 * w0 + hi1 * w1], axis=1)
    out = x_ref[...] + gt_ref[...] * ffn
    if final:
        out = _rms(out, fg_ref[...])
    o_ref[...] = out


def _combine(x1, S, mf, gt2, final_g, g, final, out_rows=None, out_tile0=0, out_prev=None):
    T, D = x1.shape
    tm = _pick(S, TOKEN_TILE)
    per_b = S // tm
    nt = T // tm
    in_specs = [pl.BlockSpec((tm, D), lambda i: (i, 0)),
                pl.BlockSpec((tm, D // 2), lambda i: (i, 0)),
                pl.BlockSpec((tm, D // 2), lambda i: (i + nt, 0)),
                pl.BlockSpec((tm, 2), lambda i: (i, 0)),
                pl.BlockSpec((None, 1, D), lambda i: (i // per_b, 0, 0)),
                pl.BlockSpec((1, D), lambda i: (0, 0))]
    args = [x1, g, g, mf, gt2, final_g]
    aliases = {}
    if out_prev is not None:
        in_specs.append(pl.BlockSpec(memory_space=pl.ANY))
        args.append(out_prev)
        aliases = {len(args) - 1: 0}
    return pl.pallas_call(
        functools.partial(_combine_kernel, final=final),
        grid=(nt,),
        in_specs=in_specs,
        out_specs=pl.BlockSpec((tm, D), lambda i: (i + out_tile0, 0)),
        out_shape=jax.ShapeDtypeStruct((T if out_rows is None else out_rows, D), F32),
        input_output_aliases=aliases,
        compiler_params=_cparams(("arbitrary",)),
        name="moe_combine",
    )(*args)


def _prep_w_in(w):
    offs = [0]
    for s in (256, 256, 512, 512, 512, 512, 512, FOX_HEADS, MLA_Q_RANK, MLA_KV_RANK, MLA_ROPE):
        offs.append(offs[-1] + s)
    rq, rk, rv, rg, fq, fk, fv, ff, mq, mkv, mkr = [w[:, offs[i]:offs[i + 1]] for i in range(11)]
    pad = jnp.zeros((w.shape[0], LANES - MLA_ROPE - FOX_HEADS), w.dtype)
    return jnp.concatenate([rq, rk, rv, rg, fq, fk, fv, mq, mkv, mkr, ff, pad], axis=1).astype(BF16)


def _prep_wq_up(w):
    r = w.reshape(MLA_Q_RANK, MLA_HEADS, MLA_NOPE + MLA_ROPE)
    r = jnp.pad(r, ((0, 0), (0, 0), (0, MLA_DQ - MLA_NOPE - MLA_ROPE)))
    return r.reshape(MLA_Q_RANK, MLA_HEADS * MLA_DQ).astype(BF16)


def _prep_router(w_grp, b_grp, w_exp, b_exp):
    D = w_grp.shape[0]
    pad = LANES - N_EXPERTS - N_GROUPS
    rwt = jnp.concatenate([w_exp, w_grp, jnp.zeros((D, pad), F32)], axis=1).astype(F32).T
    rwh = rwt.astype(BF16)
    rwl = (rwt - rwh.astype(F32)).astype(BF16)
    rb = jnp.concatenate([b_exp, b_grp, jnp.zeros((pad,), F32)]).astype(F32).reshape(LANES, 1)
    return rwh, rwl, rb


def kernel(x, c, positions, ada_w, ada_b, norm1_g, norm2_g, w_in, fox_fb, mla_q_norm_g, mla_wq_up, mla_kv_norm_g, mla_wkv_up, gate_w, gate_b, branch_w, out_w, router_grp_w, router_grp_b, router_exp_w, router_exp_b, exp_w1, exp_w3, exp_w2, final_g):
    B, S, D = x.shape
    L = ada_w.shape[0]
    T = B * S
    n_str = N_STREAMS if B % N_STREAMS == 0 else 1
    Bs = B // n_str
    Ts = Bs * S
    As = 2 * Ts
    tb = _pick(As, EXPERT_BLOCK)
    n_blocks = As // tb + N_EXPERTS
    P = n_blocks * tb
    tiles_per_stream = Ts // _pick(S, TOKEN_TILE)

    mod = _adaln(c, ada_w, ada_b)
    cos_t, sin_t = _rope_tables(positions)
    cos_s = [cos_t[h * Ts:(h + 1) * Ts] for h in range(n_str)]
    sin_s = [sin_t[h * Ts:(h + 1) * Ts] for h in range(n_str)]
    x_full = x.reshape(T, D)
    xs2 = [x_full] * n_str
    x_tile0 = [h * tiles_per_stream for h in range(n_str)]
    final_g2 = final_g.reshape(1, D)
    r3 = lambda a: a.reshape(Bs, S, -1)
    out = None

    for l in range(L):
        mods = [[mod[l, h * Bs:(h + 1) * Bs, i * D:(i + 1) * D].reshape(Bs, 1, D) for i in range(6)]
                for h in range(n_str)]
        g1 = norm1_g[l].reshape(1, D)
        g2 = norm2_g[l].reshape(1, D)
        w_all = _prep_w_in(w_in[l])
        wq = _prep_wq_up(mla_wq_up[l])
        wkv = mla_wkv_up[l].astype(BF16)
        gq = mla_q_norm_g[l].reshape(1, -1)
        gkv = mla_kv_norm_g[l].reshape(1, -1)
        rwh, rwl, rb = _prep_router(router_grp_w[l], router_grp_b[l], router_exp_w[l], router_exp_b[l])
        gw = gate_w[l].astype(BF16)
        gb = gate_b[l].reshape(1, -1)
        bw = branch_w[l].astype(BF16)
        ow = out_w[l].astype(BF16)
        last = l == L - 1

        merged = []
        for h in range(n_str):
            sh1, sc1, gt1, sh2, sc2, gt2 = mods[h]
            (rq, rk, rv, rg, fq, fk, fv, mq, mk, kpe, mv, ffp) = _proj(
                xs2[h], x_tile0[h], S, g1, sh1, sc1, w_all, cos_s[h], sin_s[h], gq, wq, gkv, wkv)
            ya = _retention(rq, rk, rv, rg, Bs, S).reshape(Ts, -1)
            kb = _fox_gate(ffp, fox_fb[l], S)
            yb = _flash(r3(fq), r3(fk), r3(kb), r3(fv), Bs, S, FOX_HEADS, FOX_DH, True, 3,
                        "flash_fox").reshape(Ts, -1)
            yc = _flash(r3(mq), r3(mk), r3(kpe), r3(mv), Bs, S, MLA_HEADS, MLA_DQ, False, 0,
                        "flash_mla").reshape(Ts, -1)
            merged.append(_merge(xs2[h], x_tile0[h], S, ya, yb, yc, g1, sh1, sc1, gt1, g2, sh2, sc2,
                                 gw, gb, bw, ow, rwh, rwl, rb))

        routed = []
        for h in range(n_str):
            x1, h2, mi, mf, cnt = merged[h]
            counts = cnt[:, 0].astype(I32)
            pcounts = (counts + tb - 1) // tb * tb
            pends = jnp.cumsum(pcounts)
            pstarts = pends - pcounts
            sel = mi[0:2, :, None] == jnp.arange(N_EXPERTS, dtype=I32)
            dest = (jnp.sum(jnp.where(sel, pstarts, 0), axis=-1) + mi[2:4]).reshape(As)
            blk_pos = jnp.arange(n_blocks, dtype=I32) * tb
            blk_e = jnp.minimum(jnp.sum((pends[None, :] <= blk_pos[:, None]).astype(I32), axis=1),
                                N_EXPERTS - 1)
            nvalid = (pends[-1:] // tb).astype(I32)
            fr = jnp.arange(tb, dtype=I32)[None, :]
            is_pad = (fr < (pcounts - counts)[:, None]).reshape(-1)
            pad_slot = ((pstarts + counts)[:, None] + fr).reshape(-1)
            tail_rank = jnp.cumsum(jnp.logical_not(is_pad).astype(I32)) - 1
            fill_idx = jnp.where(is_pad, pad_slot, pends[-1] + tail_rank)
            routed.append((dest, blk_e, nvalid, _sc_dispatch(h2, dest, fill_idx, P)))

        ys = [_experts(routed[h][3], routed[h][1], routed[h][2], exp_w1, exp_w3, exp_w2, l, tb)
              for h in range(n_str)]
        gathered = [_sc_gather(ys[h], routed[h][0]) for h in range(n_str)]
        for h in range(n_str):
            x1, _, _, mf, _ = merged[h]
            gt2 = mods[h][5]
            if last:
                out = _combine(x1, S, mf[0:2].T, gt2, final_g2, gathered[h], True,
                               out_rows=T, out_tile0=h * tiles_per_stream, out_prev=out)
            else:
                xs2[h] = _combine(x1, S, mf[0:2].T, gt2, final_g2, gathered[h], False)
        x_tile0 = [0] * n_str

    return out.reshape(B, S, D)
```

```python
import functools
import math

import jax
import jax.numpy as jnp
from jax import lax
from jax.experimental import pallas as pl
from jax.experimental.pallas import tpu as pltpu
from jax.experimental.pallas import tpu_sc as plsc

F32 = jnp.float32
BF16 = jnp.bfloat16
I32 = jnp.int32
U32 = jnp.uint32
HIGHEST = lax.Precision.HIGHEST

EPS = 1e-6
ROPE_THETA = 10000.0
RET_HEADS = 4
RET_DK = 64
RET_DV = 128
RET_CHUNK = 128
FOX_HEADS = 4
FOX_DH = 128
MLA_HEADS = 4
MLA_Q_RANK = 256
MLA_KV_RANK = 128
MLA_NOPE = 128
MLA_ROPE = 64
MLA_V = 128
MLA_DQ = 256
N_BRANCH = 3
BRANCH_W = 512
N_GROUPS = 4
EXP_PER_GROUP = 8
N_EXPERTS = N_GROUPS * EXP_PER_GROUP
D_EXPERT = 512

LANES = 128
V7X_VMEM_LIMIT = 56 * 1024 * 1024

C_RQ, C_RK, C_RV, C_RG = 0, 256, 512, 1024
C_FQ, C_FK, C_FV = 1536, 2048, 2560
C_MQ, C_MKV, C_TAIL = 3072, 3328, 3456
D_IN_PAD = 3584
FF_LANE = MLA_ROPE

NEG_BIG = -1e30

TOKEN_TILE = 512
ATTN_TILE = 1024
EXPERT_BLOCK = 512
RET_CHUNKS_PER_STEP = 32
ADALN_COL_TILE = 1536
ROPE_TABLE_TILE = 2048
SC_GATHER_WINDOW = 128
N_STREAMS = 1
MERGE_COL_CHUNK = 256
LOG2E = math.log2(math.e)


def _cparams(sem):
    return pltpu.CompilerParams(dimension_semantics=sem, vmem_limit_bytes=V7X_VMEM_LIMIT)


def _pick(n, pref):
    t = min(n, pref)
    assert n % t == 0, (n, t)
    return t


def _adaln_kernel(c_ref, w_ref, b_ref, o_ref):
    c = c_ref[...]
    ca = c * jax.nn.sigmoid(c)
    o_ref[...] = jnp.dot(ca, w_ref[...], preferred_element_type=F32, precision=HIGHEST) + b_ref[...]


def _adaln(c, ada_w, ada_b):
    L, D, N = ada_w.shape
    B = c.shape[0]
    tn = _pick(N, ADALN_COL_TILE)
    return pl.pallas_call(
        _adaln_kernel,
        grid=(L, N // tn),
        in_specs=[pl.BlockSpec((B, D), lambda l, j: (0, 0)),
                  pl.BlockSpec((None, D, tn), lambda l, j: (l, 0, j)),
                  pl.BlockSpec((None, 1, tn), lambda l, j: (l, 0, j))],
        out_specs=pl.BlockSpec((None, B, tn), lambda l, j: (l, 0, j)),
        out_shape=jax.ShapeDtypeStruct((L, B, N), F32),
        compiler_params=_cparams(("arbitrary", "arbitrary")),
        name="adaln",
    )(c, ada_w, ada_b.reshape(L, 1, N))


def _rope_table_kernel(pos_ref, inv_ref, sign_ref, cos_ref, sin_ref):
    ang = pos_ref[...].astype(F32) * inv_ref[...]
    cos_ref[...] = jnp.cos(ang)
    sin_ref[...] = jnp.sin(ang) * sign_ref[...]


def _rope_tables(positions):
    T = positions.size
    tm = _pick(T, ROPE_TABLE_TILE)
    half = MLA_ROPE // 2
    inv = ROPE_THETA ** (-jnp.arange(0, MLA_ROPE, 2, dtype=F32) / MLA_ROPE)
    inv_t = jnp.tile(inv, LANES // half).reshape(1, LANES)
    sign = jnp.where((jnp.arange(LANES) % MLA_ROPE) < half, -1.0, 1.0).astype(F32).reshape(1, LANES)
    return pl.pallas_call(
        _rope_table_kernel,
        grid=(T // tm,),
        in_specs=[pl.BlockSpec((tm, 1), lambda i: (i, 0)),
                  pl.BlockSpec((1, LANES), lambda i: (0, 0)),
                  pl.BlockSpec((1, LANES), lambda i: (0, 0))],
        out_specs=[pl.BlockSpec((tm, LANES), lambda i: (i, 0))] * 2,
        out_shape=[jax.ShapeDtypeStruct((T, LANES), F32)] * 2,
        compiler_params=_cparams(("arbitrary",)),
        name="rope_tables",
    )(positions.reshape(T, 1), inv_t, sign)


def _rope_slab(x, cos_t, sin_t, lane):
    nxt = pltpu.roll(x, LANES - 32, axis=1)
    prv = pltpu.roll(x, 32, axis=1)
    swapped = jnp.where((lane & 32) == 0, nxt, prv)
    return x * cos_t + swapped * sin_t


def _pack_bf16_pair(lo, hi):
    lo_bits = lax.shift_right_logical(lax.bitcast_convert_type(lo.astype(BF16).astype(F32), U32), jnp.uint32(16))
    hi_bits = lax.bitcast_convert_type(hi.astype(BF16).astype(F32), U32) & jnp.uint32(0xFFFF0000)
    return hi_bits | lo_bits


def _unpack_bf16_pair(w):
    lo = lax.bitcast_convert_type(lax.shift_left(w, jnp.uint32(16)), F32)
    hi = lax.bitcast_convert_type(w & jnp.uint32(0xFFFF0000), F32)
    return lo, hi


def _norm_mod(x, g, shift, scale):
    y = x * lax.rsqrt(jnp.mean(x * x, axis=-1, keepdims=True) + EPS)
    return (y * g) * (1.0 + scale) + shift


def _rms(x, g):
    return x * lax.rsqrt(jnp.mean(x * x, axis=-1, keepdims=True) + EPS) * g


def _proj_kernel(x_ref, g_ref, sh_ref, sc_ref, w_ref, cos_ref, sin_ref,
                 gq_ref, wq_ref, gkv_ref, wkv_ref,
                 rq_ref, rk_ref, rv_ref, rg_ref, fq_ref, fk_ref, fv_ref,
                 mq_ref, mk_ref, kpe_ref, mv_ref, ff_ref):
    h = _norm_mod(x_ref[...], g_ref[...], sh_ref[...], sc_ref[...]).astype(BF16)
    cos_t = cos_ref[...]
    sin_t = sin_ref[...]
    lane = lax.broadcasted_iota(I32, cos_t.shape, 1)

    def proj(c0, width):
        return jnp.dot(h, w_ref[:, c0:c0 + width], preferred_element_type=F32)

    tail = proj(C_TAIL, LANES)
    ff_ref[...] = tail
    kpe_ref[...] = jnp.where(lane < MLA_ROPE, _rope_slab(tail, cos_t, sin_t, lane), 0.0).astype(BF16)

    qn = _rms(proj(C_MQ, MLA_Q_RANK), gq_ref[...]).astype(BF16)
    qh = jnp.dot(qn, wq_ref[...], preferred_element_type=F32)
    q_scale = (MLA_NOPE + MLA_ROPE) ** -0.5 * LOG2E
    for hd in range(MLA_HEADS):
        c0 = hd * MLA_DQ
        mq_ref[:, c0:c0 + LANES] = (qh[:, c0:c0 + LANES] * q_scale).astype(BF16)
        pe = _rope_slab(qh[:, c0 + LANES:c0 + 2 * LANES], cos_t, sin_t, lane)
        mq_ref[:, c0 + LANES:c0 + 2 * LANES] = jnp.where(lane < MLA_ROPE, pe * q_scale, 0.0).astype(BF16)

    kvn = _rms(proj(C_MKV, MLA_KV_RANK), gkv_ref[...]).astype(BF16)
    kvh = jnp.dot(kvn, wkv_ref[...], preferred_element_type=F32)
    for hd in range(MLA_HEADS):
        c0 = hd * (MLA_NOPE + MLA_V)
        mk_ref[:, hd * MLA_NOPE:(hd + 1) * MLA_NOPE] = kvh[:, c0:c0 + MLA_NOPE].astype(BF16)
        mv_ref[:, hd * MLA_V:(hd + 1) * MLA_V] = kvh[:, c0 + MLA_NOPE:c0 + MLA_NOPE + MLA_V].astype(BF16)

    rq = proj(C_RQ, 256)
    rk = proj(C_RK, 256)
    for s in range(2):
        sl = slice(s * LANES, (s + 1) * LANES)
        q2 = _rope_slab(rq[:, sl], cos_t, sin_t, lane)
        k2 = _rope_slab(rk[:, sl], cos_t, sin_t, lane) * (RET_DK ** -0.5)
        for half in range(2):
            mine = (lane < RET_DK) if half == 0 else (lane >= RET_DK)
            hs = slice((2 * s + half) * LANES, (2 * s + half + 1) * LANES)
            rq_ref[:, hs] = jnp.where(mine, q2, 0.0).astype(BF16)
            rk_ref[:, hs] = jnp.where(mine, k2, 0.0).astype(BF16)
    rv_ref[...] = proj(C_RV, 512).astype(BF16)
    rg_ref[...] = proj(C_RG, 512).astype(BF16)
    fq_ref[...] = (proj(C_FQ, 512) * (FOX_DH ** -0.5 * LOG2E)).astype(BF16)
    fk_ref[...] = proj(C_FK, 512).astype(BF16)
    fv_ref[...] = proj(C_FV, 512).astype(BF16)


def _proj(x2, x_tile0, S, g, sh, sc, w_all, cos_t, sin_t, gq, wq, gkv, wkv):
    D = x2.shape[1]
    T = cos_t.shape[0]
    tm = _pick(S, TOKEN_TILE)
    per_b = S // tm
    row = lambda i: (i, 0)
    xrow = lambda i: (i + x_tile0, 0)
    const = lambda i: (0, 0)
    batch = lambda i: (i // per_b, 0, 0)
    widths = [512, 512, 512, 512, 512, 512, 512, MLA_HEADS * MLA_DQ, MLA_HEADS * MLA_NOPE, LANES,
              MLA_HEADS * MLA_V]
    out_shape = [jax.ShapeDtypeStruct((T, w), BF16) for w in widths]
    out_shape.append(jax.ShapeDtypeStruct((T, LANES), F32))
    out_specs = [pl.BlockSpec((tm, w), row) for w in widths] + [pl.BlockSpec((tm, LANES), row)]
    return pl.pallas_call(
        _proj_kernel,
        grid=(T // tm,),
        in_specs=[pl.BlockSpec((tm, D), xrow),
                  pl.BlockSpec((1, D), const),
                  pl.BlockSpec((None, 1, D), batch),
                  pl.BlockSpec((None, 1, D), batch),
                  pl.BlockSpec((D, D_IN_PAD), const),
                  pl.BlockSpec((tm, LANES), row),
                  pl.BlockSpec((tm, LANES), row),
                  pl.BlockSpec((1, MLA_Q_RANK), const),
                  pl.BlockSpec((MLA_Q_RANK, MLA_HEADS * MLA_DQ), const),
                  pl.BlockSpec((1, MLA_KV_RANK), const),
                  pl.BlockSpec((MLA_KV_RANK, MLA_HEADS * (MLA_NOPE + MLA_V)), const)],
        out_specs=out_specs,
        out_shape=out_shape,
        compiler_params=_cparams(("arbitrary",)),
        name="proj",
    )(x2, g, sh, sc, w_all, cos_t, sin_t, gq, wq, gkv, wkv)


def _split3(x):
    a = x.astype(BF16)
    r = x - a.astype(F32)
    b = r.astype(BF16)
    c = (r - b.astype(F32)).astype(BF16)
    return a, b, c


def _fox_gate_kernel(ff_ref, fb_ref, tri_ref, kb_ref):
    tm = tri_ref.shape[0]
    lane = lax.broadcasted_iota(I32, (tm, LANES), 1)
    tri = tri_ref[...]
    carry = jnp.zeros((1, LANES), F32)
    for t in range(ff_ref.shape[0] // tm):
        rows = slice(t * tm, (t + 1) * tm)
        z = ff_ref[rows, :] + fb_ref[...]
        ls = -(jnp.maximum(-z, 0.0) + jnp.log1p(jnp.exp(-jnp.abs(z))))
        ls = jnp.where((lane >= FF_LANE) & (lane < FF_LANE + FOX_HEADS), ls, 0.0)
        f = carry
        for part in _split3(ls):
            f = f + jnp.dot(tri, part, preferred_element_type=F32)
        carry = f[tm - 1:tm, :]
        hi, mid, lo = [part.astype(F32) for part in _split3(f * LOG2E)]
        for hd in range(FOX_HEADS):
            src = FF_LANE + hd
            slab = jnp.where(lane == 0, pltpu.roll(hi, (LANES - src) % LANES, axis=1),
                   jnp.where(lane == 1, pltpu.roll(mid, (LANES + 1 - src) % LANES, axis=1),
                   jnp.where(lane == 2, pltpu.roll(lo, (LANES + 2 - src) % LANES, axis=1), 0.0)))
            kb_ref[rows, hd * LANES:(hd + 1) * LANES] = slab.astype(BF16)


def _fox_gate(ffp, fb, S):
    T = ffp.shape[0]
    tm = _pick(S, TOKEN_TILE)
    idx = jnp.arange(tm)
    tri = (idx[None, :] <= idx[:, None]).astype(BF16)
    fbv = jnp.zeros((1, LANES), F32).at[0, FF_LANE:FF_LANE + FOX_HEADS].set(fb)
    return pl.pallas_call(
        _fox_gate_kernel,
        grid=(T // S,),
        in_specs=[pl.BlockSpec((S, LANES), lambda b: (b, 0)),
                  pl.BlockSpec((1, LANES), lambda b: (0, 0)),
                  pl.BlockSpec((tm, tm), lambda b: (0, 0))],
        out_specs=pl.BlockSpec((S, FOX_HEADS * LANES), lambda b: (b, 0)),
        out_shape=jax.ShapeDtypeStruct((T, FOX_HEADS * LANES), BF16),
        compiler_params=_cparams(("arbitrary",)),
        name="fox_gate",
    )(ffp, fbv, tri)


def _retention_kernel(dchunk_ref, q_ref, k_ref, v_ref, g_ref, dmask_ref, din_ref, dout_ref,
                      o_ref, state_ref, *, n_chunks):
    @pl.when(pl.program_id(1) == 0)
    def _():
        state_ref[...] = jnp.zeros_like(state_ref)

    C = RET_CHUNK
    for ci in range(n_chunks):
        rows = slice(ci * C, (ci + 1) * C)
        for hd in range(RET_HEADS):
            slab = slice(hd * LANES, (hd + 1) * LANES)
            q = q_ref[rows, slab]
            k = k_ref[rows, slab]
            vcols = slice(hd * RET_DV, (hd + 1) * RET_DV)
            v = v_ref[rows, vcols]
            state = state_ref[hd]
            scores = lax.dot_general(q, k, (((1,), (1,)), ((), ())),
                                     preferred_element_type=F32) * dmask_ref[hd]
            inner = jnp.dot(scores.astype(BF16), v, preferred_element_type=F32)
            cross = jnp.dot(q, state.astype(BF16), preferred_element_type=F32) * din_ref[hd]
            o = inner + cross
            vd = (v.astype(F32) * dout_ref[hd]).astype(BF16)
            kv = lax.dot_general(k, vd, (((0,), (0,)), ((), ())), preferred_element_type=F32)
            state_ref[hd] = state * dchunk_ref[hd] + kv
            mu = jnp.mean(o, axis=-1, keepdims=True)
            d = o - mu
            var = jnp.mean(d * d, axis=-1, keepdims=True)
            on = d * lax.rsqrt(var + EPS)
            g = g_ref[rows, vcols].astype(F32)
            o_ref[rows, vcols] = (g * jax.nn.sigmoid(g) * on).astype(BF16)


def _retention(rq, rk, rv, rg, B, S):
    H, C = RET_HEADS, RET_CHUNK
    tr = _pick(S, RET_CHUNKS_PER_STEP * C)
    n_chunks = tr // C
    log_gamma = jnp.log1p(-jnp.exp2(-5.0 - jnp.arange(H, dtype=F32)))
    idx = jnp.arange(C, dtype=F32)
    rel = idx[:, None] - idx[None, :]
    dmask = jnp.where(rel >= 0, jnp.exp(log_gamma[:, None, None] * jnp.maximum(rel, 0.0)), 0.0)
    decay_in = jnp.exp(log_gamma[:, None] * (idx + 1.0))
    decay_out = jnp.exp(log_gamma[:, None] * (C - 1.0 - idx))
    decay_chunk = jnp.exp(log_gamma * C)
    din = jnp.broadcast_to(decay_in[:, :, None], (H, C, RET_DV))
    dout = jnp.broadcast_to(decay_out[:, :, None], (H, C, RET_DV))
    tok = lambda b, i: (b, i, 0)
    const3 = lambda b, i: (0, 0, 0)
    return pl.pallas_call(
        functools.partial(_retention_kernel, n_chunks=n_chunks),
        grid=(B, S // tr),
        in_specs=[pl.BlockSpec(memory_space=pltpu.SMEM),
                  pl.BlockSpec((None, tr, H * LANES), tok),
                  pl.BlockSpec((None, tr, H * LANES), tok),
                  pl.BlockSpec((None, tr, H * RET_DV), tok),
                  pl.BlockSpec((None, tr, H * RET_DV), tok),
                  pl.BlockSpec((H, C, C), const3),
                  pl.BlockSpec((H, C, RET_DV), const3),
                  pl.BlockSpec((H, C, RET_DV), const3)],
        out_specs=pl.BlockSpec((None, tr, H * RET_DV), tok),
        out_shape=jax.ShapeDtypeStruct((B, S, H * RET_DV), BF16),
        scratch_shapes=[pltpu.VMEM((H, LANES, RET_DV), F32)],
        compiler_params=_cparams(("arbitrary", "arbitrary")),
        name="retention",
    )(decay_chunk, rq.reshape(B, S, -1), rk.reshape(B, S, -1), rv.reshape(B, S, -1),
      rg.reshape(B, S, -1), dmask, din, dout)


def _flash_kernel(q_ref, k_ref, ke_ref, v_ref, o_ref, m_ref, l_ref, acc_ref, s_ref, *, tq, q_bias_cols):
    S = q_ref.shape[0]
    nq = S // tq
    tk = tq
    m_ref[...] = jnp.full(m_ref.shape, NEG_BIG, F32)
    l_ref[...] = jnp.zeros_like(l_ref)
    acc_ref[...] = jnp.zeros_like(acc_ref)
    lane = lax.broadcasted_iota(I32, (tq, LANES), 1)

    half = tk // 2
    tri = (lax.broadcasted_iota(I32, (half, half), 0) <= lax.broadcasted_iota(I32, (half, half), 1))
    nt = lambda a, b: lax.dot_general(a, b, (((1,), (1,)), ((), ())), preferred_element_type=F32)

    def issue(n):
        i, j = pairs[n]
        q = q_ref[i * tq:(i + 1) * tq, :]
        if q_bias_cols:
            q = jnp.concatenate([q, jnp.where(lane < q_bias_cols, -1.0, 0.0).astype(BF16)], axis=1)
        kj = jnp.concatenate([k_ref[j * tk:(j + 1) * tk, :], ke_ref[j * tk:(j + 1) * tk, :]], axis=1)
        if j < i:
            s_ref[n % 2] = nt(kj, q)
        else:
            s_ref[n % 2, 0:half, :] = nt(kj[0:half], q)
            s_ref[n % 2, half:, half:] = nt(kj[half:], q[half:])

    def update(s, i, c0, c1, r0, r1):
        m_prev = m_ref[i, :, c0:c1]
        m_new = jnp.maximum(m_prev, jnp.max(s, axis=0, keepdims=True))
        alpha = jnp.exp2(m_prev - m_new)
        p = jnp.exp2(s - m_new)
        l_ref[i, :, c0:c1] = alpha * l_ref[i, :, c0:c1] + jnp.sum(p, axis=0, keepdims=True)
        acc_ref[i, :, c0:c1] = alpha * acc_ref[i, :, c0:c1] + lax.dot_general(
            v_ref[r0:r1, :], p.astype(BF16), (((0,), (0,)), ((), ())),
            preferred_element_type=F32)
        m_ref[i, :, c0:c1] = m_new

    def consume(n):
        i, j = pairs[n]
        slot = n % 2
        if j < i:
            update(s_ref[slot], i, 0, tq, j * tk, (j + 1) * tk)
        else:
            left = jnp.where(tri, s_ref[slot, 0:half, 0:half], NEG_BIG)
            right = jnp.concatenate([s_ref[slot, 0:half, half:],
                                     jnp.where(tri, s_ref[slot, half:, half:], NEG_BIG)], axis=0)
            update(left, i, 0, half, j * tk, j * tk + half)
            update(right, i, half, tq, j * tk, (j + 1) * tk)
            o_ref[i * tq:(i + 1) * tq, :] = (acc_ref[i] / l_ref[i]).T.astype(o_ref.dtype)

    pairs = [(i, j) for i in range(nq) for j in range(i + 1)]
    issue(0)
    for n in range(len(pairs)):
        if n + 1 < len(pairs):
            issue(n + 1)
        consume(n)


def _flash(q, k, ke, v, B, S, H, dq, ke_per_head, q_bias_cols, name):
    tq = _pick(S, ATTN_TILE)
    nq = S // tq
    ke_map = (lambda b, h: (b, 0, h)) if ke_per_head else (lambda b, h: (b, 0, 0))
    return pl.pallas_call(
        functools.partial(_flash_kernel, tq=tq, q_bias_cols=q_bias_cols),
        grid=(B, H),
        in_specs=[pl.BlockSpec((None, S, dq), lambda b, h: (b, 0, h)),
                  pl.BlockSpec((None, S, LANES), lambda b, h: (b, 0, h)),
                  pl.BlockSpec((None, S, LANES), ke_map),
                  pl.BlockSpec((None, S, LANES), lambda b, h: (b, 0, h))],
        out_specs=pl.BlockSpec((None, S, LANES), lambda b, h: (b, 0, h)),
        out_shape=jax.ShapeDtypeStruct((B, S, H * LANES), BF16),
        scratch_shapes=[pltpu.VMEM((nq, 1, tq), F32), pltpu.VMEM((nq, 1, tq), F32),
                        pltpu.VMEM((nq, LANES, tq), F32), pltpu.VMEM((2, tq, tq), F32)],
        compiler_params=_cparams(("arbitrary", "arbitrary")),
        name=name,
    )(q, k, ke, v)


def _merge_kernel(x_ref, ya_ref, yb_ref, yc_ref, g1_ref, sh1_ref, sc1_ref, gt1_ref,
                  g2_ref, sh2_ref, sc2_ref, gw_ref, gb_ref, bw_ref, ow_ref,
                  rwh_ref, rwl_ref, rb_ref, tri_ref,
                  x1_ref, h2_ref, mi_ref, mf_ref, cnt_ref, h_ref, mg_ref, carry_ref):
    tm, D = x_ref.shape
    cn = MERGE_COL_CHUNK

    @pl.when(pl.program_id(0) == 0)
    def _():
        carry_ref[...] = jnp.zeros_like(carry_ref)

    h_ref[...] = _norm_mod(x_ref[...], g1_ref[...], sh1_ref[...], sc1_ref[...]).astype(BF16)
    for n in range(D // cn):
        cols = slice(n * cn, (n + 1) * cn)
        merged = None
        for i, y_ref in enumerate((ya_ref, yb_ref, yc_ref)):
            gcols = slice(i * D + n * cn, i * D + (n + 1) * cn)
            gate = jax.nn.sigmoid(jnp.dot(h_ref[...], gw_ref[:, gcols], preferred_element_type=F32)
                                  + gb_ref[:, gcols])
            br = jnp.dot(y_ref[...], bw_ref[i, :, cols], preferred_element_type=F32)
            merged = gate * br if merged is None else merged + gate * br
        mg_ref[:, cols] = merged.astype(BF16)
    for n in range(D // cn):
        cols = slice(n * cn, (n + 1) * cn)
        mix = jnp.dot(mg_ref[...], ow_ref[:, cols], preferred_element_type=F32)
        x1_ref[:, cols] = x_ref[:, cols] + gt1_ref[:, cols] * mix
    h2 = _norm_mod(x1_ref[...], g2_ref[...], sh2_ref[...], sc2_ref[...])
    h2_ref[...] = _pack_bf16_pair(h2[:, :D // 2], h2[:, D // 2:])

    hh = h2.astype(BF16)
    hl = (h2 - hh.astype(F32)).astype(BF16)
    nt = lambda a, b: lax.dot_general(a, b, (((1,), (1,)), ((), ())), preferred_element_type=F32)
    lt = nt(rwh_ref[...], hh) + nt(rwh_ref[...], hl) + nt(rwl_ref[...], hh) + rb_ref[...]
    row8 = lax.broadcasted_iota(I32, (EXP_PER_GROUP, tm), 0)
    gl = jnp.where(row8 < N_GROUPS, lt[N_EXPERTS:N_EXPERTS + EXP_PER_GROUP, :], -jnp.inf)
    gmax = jnp.max(gl, axis=0, keepdims=True)
    g_idx = jnp.min(jnp.where(gl == gmax, row8, EXP_PER_GROUP), axis=0, keepdims=True)
    g_w = 1.0 / jnp.sum(jnp.exp(gl - gmax), axis=0, keepdims=True)
    el = lt[(N_GROUPS - 1) * EXP_PER_GROUP:N_EXPERTS, :]
    for g in range(N_GROUPS - 2, -1, -1):
        el = jnp.where(g_idx == g, lt[g * EXP_PER_GROUP:(g + 1) * EXP_PER_GROUP, :], el)
    e1 = jnp.max(el, axis=0, keepdims=True)
    i1 = jnp.min(jnp.where(el == e1, row8, EXP_PER_GROUP), axis=0, keepdims=True)
    el2 = jnp.where(row8 == i1, -jnp.inf, el)
    e2 = jnp.max(el2, axis=0, keepdims=True)
    i2 = jnp.min(jnp.where(el2 == e2, row8, EXP_PER_GROUP), axis=0, keepdims=True)
    r = jnp.exp(e2 - e1)
    w1 = g_w / (1.0 + r)
    w2 = g_w * r / (1.0 + r)
    eid1 = g_idx * EXP_PER_GROUP + i1
    eid2 = g_idx * EXP_PER_GROUP + i2

    rowe = lax.broadcasted_iota(I32, (N_EXPERTS, tm), 0)
    hit1 = rowe == eid1
    hit2 = rowe == eid2
    onehot = jnp.where(hit1 | hit2, 1.0, 0.0)
    before = (jnp.dot(onehot.astype(BF16), tri_ref[...], preferred_element_type=F32)
              + jnp.concatenate([carry_ref[...]] * (tm // LANES), axis=1))
    rank1 = jnp.sum(jnp.where(hit1, before, 0.0), axis=0, keepdims=True)
    rank2 = jnp.sum(jnp.where(hit2, before, 0.0), axis=0, keepdims=True)
    carry_ref[...] = carry_ref[...] + jnp.sum(onehot, axis=1, keepdims=True)
    cnt_ref[...] = carry_ref[...]

    mi_ref[...] = jnp.where(row8 == 0, eid1,
                  jnp.where(row8 == 1, eid2,
                  jnp.where(row8 == 2, rank1.astype(I32),
                  jnp.where(row8 == 3, rank2.astype(I32), 0))))
    mf_ref[...] = jnp.where(row8 == 0, w1, jnp.where(row8 == 1, w2, 0.0))


def _merge(x2, x_tile0, S, ya, yb, yc, g1, sh1, sc1, gt1, g2, sh2, sc2, gw, gb, bw, ow, rwh, rwl, rb):
    D = x2.shape[1]
    T = ya.shape[0]
    tm = _pick(S, TOKEN_TILE)
    per_b = S // tm
    row = lambda i: (i, 0)
    xrow = lambda i: (i + x_tile0, 0)
    col = lambda i: (0, i)
    const = lambda i: (0, 0)
    batch = lambda i: (i // per_b, 0, 0)
    idx = jnp.arange(tm)
    tri = (idx[:, None] < idx[None, :]).astype(BF16)
    vecb = pl.BlockSpec((None, 1, D), batch)
    return pl.pallas_call(
        _merge_kernel,
        grid=(T // tm,),
        in_specs=[pl.BlockSpec((tm, D), xrow),
                  pl.BlockSpec((tm, BRANCH_W), row),
                  pl.BlockSpec((tm, BRANCH_W), row),
                  pl.BlockSpec((tm, BRANCH_W), row),
                  pl.BlockSpec((1, D), const), vecb, vecb, vecb,
                  pl.BlockSpec((1, D), const), vecb, vecb,
                  pl.BlockSpec((D, N_BRANCH * D), const),
                  pl.BlockSpec((1, N_BRANCH * D), const),
                  pl.BlockSpec((N_BRANCH, BRANCH_W, D), lambda i: (0, 0, 0)),
                  pl.BlockSpec((D, D), const),
                  pl.BlockSpec((LANES, D), const),
                  pl.BlockSpec((LANES, D), const),
                  pl.BlockSpec((LANES, 1), const),
                  pl.BlockSpec((tm, tm), const)],
        out_specs=[pl.BlockSpec((tm, D), row), pl.BlockSpec((tm, D // 2), row),
                   pl.BlockSpec((8, tm), col), pl.BlockSpec((8, tm), col),
                   pl.BlockSpec((N_EXPERTS, LANES), const)],
        out_shape=[jax.ShapeDtypeStruct((T, D), F32), jax.ShapeDtypeStruct((T, D // 2), U32),
                   jax.ShapeDtypeStruct((8, T), I32), jax.ShapeDtypeStruct((8, T), F32),
                   jax.ShapeDtypeStruct((N_EXPERTS, LANES), F32)],
        scratch_shapes=[pltpu.VMEM((tm, D), BF16), pltpu.VMEM((tm, D), BF16),
                        pltpu.VMEM((N_EXPERTS, LANES), F32)],
        compiler_params=_cparams(("arbitrary",)),
        name="merge",
    )(x2, ya, yb, yc, g1, sh1, sc1, gt1, g2, sh2, sc2, gw, gb, bw, ow, rwh, rwl, rb, tri)


def _expert_kernel(blk_e_ref, nvalid_ref, xs_ref, w1_ref, w3_ref, w2_ref, ys_ref, w1b_ref, w3b_ref, w2b_ref):
    i = pl.program_id(0)
    valid = i < nvalid_ref[0]
    new_expert = jnp.logical_or(i == 0, blk_e_ref[i] != blk_e_ref[jnp.maximum(i - 1, 0)])

    @pl.when(jnp.logical_and(valid, new_expert))
    def _():
        w1b_ref[...] = w1_ref[...].astype(BF16)
        w3b_ref[...] = w3_ref[...].astype(BF16)
        w2b_ref[...] = w2_ref[...].astype(BF16)

    @pl.when(valid)
    def _():
        half = xs_ref.shape[1]
        x_lo, x_hi = _unpack_bf16_pair(xs_ref[...])
        x_lo = x_lo.astype(BF16)
        x_hi = x_hi.astype(BF16)

        def up(w_ref):
            return (jnp.dot(x_lo, w_ref[:half, :], preferred_element_type=F32)
                    + jnp.dot(x_hi, w_ref[half:, :], preferred_element_type=F32))

        a = up(w1b_ref)
        b = up(w3b_ref)
        hid = (a * jax.nn.sigmoid(a) * b).astype(BF16)
        y = jnp.dot(hid, w2b_ref[...], preferred_element_type=F32)
        ys_ref[...] = _pack_bf16_pair(y[:, :half], y[:, half:])

    @pl.when(jnp.logical_not(valid))
    def _():
        ys_ref[...] = jnp.zeros_like(ys_ref)


def _experts(xs, blk_e, nvalid, w1, w3, w2, layer, tb):
    P, half = xs.shape
    D = 2 * half
    n_blocks = P // tb
    rows = lambda i, be, nv: (jnp.minimum(i, nv[0] - 1), 0)
    grid_spec = pltpu.PrefetchScalarGridSpec(
        num_scalar_prefetch=2,
        grid=(n_blocks,),
        in_specs=[pl.BlockSpec((tb, half), rows),
                  pl.BlockSpec((None, None, D, D_EXPERT), lambda i, be, nv: (layer, be[i], 0, 0)),
                  pl.BlockSpec((None, None, D, D_EXPERT), lambda i, be, nv: (layer, be[i], 0, 0)),
                  pl.BlockSpec((None, None, D_EXPERT, D), lambda i, be, nv: (layer, be[i], 0, 0))],
        out_specs=pl.BlockSpec((tb, half), lambda i, be, nv: (i, 0)),
        scratch_shapes=[pltpu.VMEM((D, D_EXPERT), BF16), pltpu.VMEM((D, D_EXPERT), BF16),
                        pltpu.VMEM((D_EXPERT, D), BF16)],
    )
    return pl.pallas_call(
        _expert_kernel,
        grid_spec=grid_spec,
        out_shape=jax.ShapeDtypeStruct((P, half), U32),
        compiler_params=_cparams(("arbitrary",)),
        name="moe_experts",
    )(blk_e, nvalid, xs, w1, w3, w2)


def _sc_gather(data, idx):
    M = idx.shape[0]
    D = data.shape[1]
    W = SC_GATHER_WINDOW
    assert M % W == 0, (M, W)
    mesh = plsc.VectorSubcoreMesh(core_axis_name="core", subcore_axis_name="subcore")
    n_workers = mesh.num_cores * mesh.num_subcores
    assert M % (W * n_workers) == 0, (M, W, n_workers)

    @functools.partial(pl.kernel, out_type=jax.ShapeDtypeStruct((M, D), data.dtype), mesh=mesh,
                       scratch_types=[pltpu.VMEM((W,), I32), pltpu.VMEM((W, D), data.dtype)])
    def gather_kernel(x_hbm, i_hbm, o_hbm, i_vmem, buf):
        worker = lax.axis_index("core") * mesh.num_subcores + lax.axis_index("subcore")

        @pl.loop(0, M // (W * n_workers))
        def _(t):
            start = (t * n_workers + worker) * W
            pltpu.sync_copy(i_hbm.at[pl.ds(start, W)], i_vmem)
            pltpu.sync_copy(x_hbm.at[i_vmem], buf)
            pltpu.sync_copy(buf, o_hbm.at[pl.ds(start, W)])

    return gather_kernel(data, idx)


def _sc_dispatch(h2, dest, fill_idx, P):
    T, D = h2.shape
    W = SC_GATHER_WINDOW
    n_fill = fill_idx.shape[0]
    mesh = plsc.VectorSubcoreMesh(core_axis_name="core", subcore_axis_name="subcore")
    n_workers = mesh.num_cores * mesh.num_subcores
    assert T % (W * n_workers) == 0 and n_fill % (W * n_workers) == 0, (T, n_fill, W, n_workers)
    zeros = jnp.zeros((W, D), h2.dtype)

    @functools.partial(pl.kernel, out_type=jax.ShapeDtypeStruct((P, D), h2.dtype), mesh=mesh,
                       scratch_types=[pltpu.VMEM((W,), I32), pltpu.VMEM((W, D), h2.dtype)])
    def dispatch_kernel(h_hbm, d_hbm, f_hbm, z_hbm, o_hbm, i_vmem, buf):
        worker = lax.axis_index("core") * mesh.num_subcores + lax.axis_index("subcore")

        @pl.loop(0, T // (W * n_workers))
        def _(t):
            start = (t * n_workers + worker) * W
            pltpu.sync_copy(h_hbm.at[pl.ds(start, W)], buf)
            for c in range(2):
                pltpu.sync_copy(d_hbm.at[pl.ds(c * T + start, W)], i_vmem)
                pltpu.sync_copy(buf, o_hbm.at[i_vmem])

        pltpu.sync_copy(z_hbm, buf)

        @pl.loop(0, n_fill // (W * n_workers))
        def _(t):
            start = (t * n_workers + worker) * W
            pltpu.sync_copy(f_hbm.at[pl.ds(start, W)], i_vmem)
            pltpu.sync_copy(buf, o_hbm.at[i_vmem])

    return dispatch_kernel(h2, dest, fill_idx, zeros)


def _combine_kernel(x_ref, g0_ref, g1_ref, mf_ref, gt_ref, fg_ref, *rest, final):
    o_ref = rest[-1]
    mf = mf_ref[...]
    lo0, hi0 = _unpack_bf16_pair(g0_ref[...])
    lo1, hi1 = _unpack_bf16_pair(g1_ref[...])
    w0 = mf[:, 0:1]
    w1 = mf[:, 1:2]
    ffn = jnp.concatenate([lo0 * w0 + lo1 * w1, hi0 * w0 + hi1 * w1], axis=1)
    out = x_ref[...] + gt_ref[...] * ffn
    if final:
        out = _rms(out, fg_ref[...])
    o_ref[...] = out


def _combine(x1, S, mf, gt2, final_g, g, final, out_rows=None, out_tile0=0, out_prev=None):
    T, D = x1.shape
    tm = _pick(S, TOKEN_TILE)
    per_b = S // tm
    nt = T // tm
    in_specs = [pl.BlockSpec((tm, D), lambda i: (i, 0)),
                pl.BlockSpec((tm, D // 2), lambda i: (i, 0)),
                pl.BlockSpec((tm, D // 2), lambda i: (i + nt, 0)),
                pl.BlockSpec((tm, 2), lambda i: (i, 0)),
                pl.BlockSpec((None, 1, D), lambda i: (i // per_b, 0, 0)),
                pl.BlockSpec((1, D), lambda i: (0, 0))]
    args = [x1, g, g, mf, gt2, final_g]
    aliases = {}
    if out_prev is not None:
        in_specs.append(pl.BlockSpec(memory_space=pl.ANY))
        args.append(out_prev)
        aliases = {len(args) - 1: 0}
    return pl.pallas_call(
        functools.partial(_combine_kernel, final=final),
        grid=(nt,),
        in_specs=in_specs,
        out_specs=pl.BlockSpec((tm, D), lambda i: (i + out_tile0, 0)),
        out_shape=jax.ShapeDtypeStruct((T if out_rows is None else out_rows, D), F32),
        input_output_aliases=aliases,
        compiler_params=_cparams(("arbitrary",)),
        name="moe_combine",
    )(*args)


def _prep_w_in(w):
    offs = [0]
    for s in (256, 256, 512, 512, 512, 512, 512, FOX_HEADS, MLA_Q_RANK, MLA_KV_RANK, MLA_ROPE):
        offs.append(offs[-1] + s)
    rq, rk, rv, rg, fq, fk, fv, ff, mq, mkv, mkr = [w[:, offs[i]:offs[i + 1]] for i in range(11)]
    pad = jnp.zeros((w.shape[0], LANES - MLA_ROPE - FOX_HEADS), w.dtype)
    return jnp.concatenate([rq, rk, rv, rg, fq, fk, fv, mq, mkv, mkr, ff, pad], axis=1).astype(BF16)


def _prep_wq_up(w):
    r = w.reshape(MLA_Q_RANK, MLA_HEADS, MLA_NOPE + MLA_ROPE)
    r = jnp.pad(r, ((0, 0), (0, 0), (0, MLA_DQ - MLA_NOPE - MLA_ROPE)))
    return r.reshape(MLA_Q_RANK, MLA_HEADS * MLA_DQ).astype(BF16)


def _prep_router(w_grp, b_grp, w_exp, b_exp):
    D = w_grp.shape[0]
    pad = LANES - N_EXPERTS - N_GROUPS
    rwt = jnp.concatenate([w_exp, w_grp, jnp.zeros((D, pad), F32)], axis=1).astype(F32).T
    rwh = rwt.astype(BF16)
    rwl = (rwt - rwh.astype(F32)).astype(BF16)
    rb = jnp.concatenate([b_exp, b_grp, jnp.zeros((pad,), F32)]).astype(F32).reshape(LANES, 1)
    return rwh, rwl, rb


def kernel(x, c, positions, ada_w, ada_b, norm1_g, norm2_g, w_in, fox_fb, mla_q_norm_g, mla_wq_up, mla_kv_norm_g, mla_wkv_up, gate_w, gate_b, branch_w, out_w, router_grp_w, router_grp_b, router_exp_w, router_exp_b, exp_w1, exp_w3, exp_w2, final_g):
    B, S, D = x.shape
    L = ada_w.shape[0]
    T = B * S
    n_str = N_STREAMS if B % N_STREAMS == 0 else 1
    Bs = B // n_str
    Ts = Bs * S
    As = 2 * Ts
    tb = _pick(As, EXPERT_BLOCK)
    n_blocks = As // tb + N_EXPERTS
    P = n_blocks * tb
    tiles_per_stream = Ts // _pick(S, TOKEN_TILE)

    mod = _adaln(c, ada_w, ada_b)
    cos_t, sin_t = _rope_tables(positions)
    cos_s = [cos_t[h * Ts:(h + 1) * Ts] for h in range(n_str)]
    sin_s = [sin_t[h * Ts:(h + 1) * Ts] for h in range(n_str)]
    x_full = x.reshape(T, D)
    xs2 = [x_full] * n_str
    x_tile0 = [h * tiles_per_stream for h in range(n_str)]
    final_g2 = final_g.reshape(1, D)
    r3 = lambda a: a.reshape(Bs, S, -1)
    out = None

    for l in range(L):
        mods = [[mod[l, h * Bs:(h + 1) * Bs, i * D:(i + 1) * D].reshape(Bs, 1, D) for i in range(6)]
                for h in range(n_str)]
        g1 = norm1_g[l].reshape(1, D)
        g2 = norm2_g[l].reshape(1, D)
        w_all = _prep_w_in(w_in[l])
        wq = _prep_wq_up(mla_wq_up[l])
        wkv = mla_wkv_up[l].astype(BF16)
        gq = mla_q_norm_g[l].reshape(1, -1)
        gkv = mla_kv_norm_g[l].reshape(1, -1)
        rwh, rwl, rb = _prep_router(router_grp_w[l], router_grp_b[l], router_exp_w[l], router_exp_b[l])
        gw = gate_w[l].astype(BF16)
        gb = gate_b[l].reshape(1, -1)
        bw = branch_w[l].astype(BF16)
        ow = out_w[l].astype(BF16)
        last = l == L - 1

        merged = []
        for h in range(n_str):
            sh1, sc1, gt1, sh2, sc2, gt2 = mods[h]
            (rq, rk, rv, rg, fq, fk, fv, mq, mk, kpe, mv, ffp) = _proj(
                xs2[h], x_tile0[h], S, g1, sh1, sc1, w_all, cos_s[h], sin_s[h], gq, wq, gkv, wkv)
            ya = _retention(rq, rk, rv, rg, Bs, S).reshape(Ts, -1)
            kb = _fox_gate(ffp, fox_fb[l], S)
            yb = _flash(r3(fq), r3(fk), r3(kb), r3(fv), Bs, S, FOX_HEADS, FOX_DH, True, 3,
                        "flash_fox").reshape(Ts, -1)
            yc = _flash(r3(mq), r3(mk), r3(kpe), r3(mv), Bs, S, MLA_HEADS, MLA_DQ, False, 0,
                        "flash_mla").reshape(Ts, -1)
            merged.append(_merge(xs2[h], x_tile0[h], S, ya, yb, yc, g1, sh1, sc1, gt1, g2, sh2, sc2,
                                 gw, gb, bw, ow, rwh, rwl, rb))

        routed = []
        for h in range(n_str):
            x1, h2, mi, mf, cnt = merged[h]
            counts = cnt[:, 0].astype(I32)
            pcounts = (counts + tb - 1) // tb * tb
            pends = jnp.cumsum(pcounts)
            pstarts = pends - pcounts
            sel = mi[0:2, :, None] == jnp.arange(N_EXPERTS, dtype=I32)
            dest = (jnp.sum(jnp.where(sel, pstarts, 0), axis=-1) + mi[2:4]).reshape(As)
            blk_pos = jnp.arange(n_blocks, dtype=I32) * tb
            blk_e = jnp.minimum(jnp.sum((pends[None, :] <= blk_pos[:, None]).astype(I32), axis=1),
                                N_EXPERTS - 1)
            nvalid = (pends[-1:] // tb).astype(I32)
            fr = jnp.arange(tb, dtype=I32)[None, :]
            is_pad = (fr < (pcounts - counts)[:, None]).reshape(-1)
            pad_slot = ((pstarts + counts)[:, None] + fr).reshape(-1)
            tail_rank = jnp.cumsum(jnp.logical_not(is_pad).astype(I32)) - 1
            fill_idx = jnp.where(is_pad, pad_slot, pends[-1] + tail_rank)
            routed.append((dest, blk_e, nvalid, _sc_dispatch(h2, dest, fill_idx, P)))

        ys = [_experts(routed[h][3], routed[h][1], routed[h][2], exp_w1, exp_w3, exp_w2, l, tb)
              for h in range(n_str)]
        gathered = [_sc_gather(ys[h], routed[h][0]) for h in range(n_str)]
        for h in range(n_str):
            x1, _, _, mf, _ = merged[h]
            gt2 = mods[h][5]
            if last:
                out = _combine(x1, S, mf[0:2].T, gt2, final_g2, gathered[h], True,
                               out_rows=T, out_tile0=h * tiles_per_stream, out_prev=out)
            else:
                xs2[h] = _combine(x1, S, mf[0:2].T, gt2, final_g2, gathered[h], False)
        x_tile0 = [0] * n_str

    return out.reshape(B, S, D)
```

```python
import functools
import math

import jax
import jax.numpy as jnp
from jax import lax
from jax.experimental import pallas as pl
from jax.experimental.pallas import tpu as pltpu
from jax.experimental.pallas import tpu_sc as plsc

F32 = jnp.float32
BF16 = jnp.bfloat16
I32 = jnp.int32
U32 = jnp.uint32
HIGHEST = lax.Precision.HIGHEST

EPS = 1e-6
ROPE_THETA = 10000.0
RET_HEADS = 4
RET_DK = 64
RET_DV = 128
RET_CHUNK = 128
FOX_HEADS = 4
FOX_DH = 128
MLA_HEADS = 4
MLA_Q_RANK = 256
MLA_KV_RANK = 128
MLA_NOPE = 128
MLA_ROPE = 64
MLA_V = 128
MLA_DQ = 256
N_BRANCH = 3
BRANCH_W = 512
N_GROUPS = 4
EXP_PER_GROUP = 8
N_EXPERTS = N_GROUPS * EXP_PER_GROUP
D_EXPERT = 512

LANES = 128
V7X_VMEM_LIMIT = 56 * 1024 * 1024

C_RQ, C_RK, C_RV, C_RG = 0, 256, 512, 1024
C_FQ, C_FK, C_FV = 1536, 2048, 2560
C_MQ, C_MKV, C_TAIL = 3072, 3328, 3456
D_IN_PAD = 3584
FF_LANE = MLA_ROPE

NEG_BIG = -1e30

TOKEN_TILE = 512
ATTN_TILE = 1024
MERGE_TILE = 1024
EXPERT_BLOCK = 512
RET_CHUNKS_PER_STEP = 32
ADALN_COL_TILE = 1536
ROPE_TABLE_TILE = 2048
SC_GATHER_WINDOW = 128
N_STREAMS = 1
MERGE_COL_CHUNK = 256
LOG2E = math.log2(math.e)


def _cparams(sem):
    return pltpu.CompilerParams(dimension_semantics=sem, vmem_limit_bytes=V7X_VMEM_LIMIT)


def _pick(n, pref):
    t = min(n, pref)
    assert n % t == 0, (n, t)
    return t


def _adaln_kernel(c_ref, w_ref, b_ref, o_ref):
    c = c_ref[...]
    ca = c * jax.nn.sigmoid(c)
    o_ref[...] = jnp.dot(ca, w_ref[...], preferred_element_type=F32, precision=HIGHEST) + b_ref[...]


def _adaln(c, ada_w, ada_b):
    L, D, N = ada_w.shape
    B = c.shape[0]
    tn = _pick(N, ADALN_COL_TILE)
    return pl.pallas_call(
        _adaln_kernel,
        grid=(L, N // tn),
        in_specs=[pl.BlockSpec((B, D), lambda l, j: (0, 0)),
                  pl.BlockSpec((None, D, tn), lambda l, j: (l, 0, j)),
                  pl.BlockSpec((None, 1, tn), lambda l, j: (l, 0, j))],
        out_specs=pl.BlockSpec((None, B, tn), lambda l, j: (l, 0, j)),
        out_shape=jax.ShapeDtypeStruct((L, B, N), F32),
        compiler_params=_cparams(("arbitrary", "arbitrary")),
        name="adaln",
    )(c, ada_w, ada_b.reshape(L, 1, N))


def _rope_table_kernel(pos_ref, inv_ref, sign_ref, cos_ref, sin_ref):
    ang = pos_ref[...].astype(F32) * inv_ref[...]
    cos_ref[...] = jnp.cos(ang)
    sin_ref[...] = jnp.sin(ang) * sign_ref[...]


def _rope_tables(positions):
    T = positions.size
    tm = _pick(T, ROPE_TABLE_TILE)
    half = MLA_ROPE // 2
    inv = ROPE_THETA ** (-jnp.arange(0, MLA_ROPE, 2, dtype=F32) / MLA_ROPE)
    inv_t = jnp.tile(inv, LANES // half).reshape(1, LANES)
    sign = jnp.where((jnp.arange(LANES) % MLA_ROPE) < half, -1.0, 1.0).astype(F32).reshape(1, LANES)
    return pl.pallas_call(
        _rope_table_kernel,
        grid=(T // tm,),
        in_specs=[pl.BlockSpec((tm, 1), lambda i: (i, 0)),
                  pl.BlockSpec((1, LANES), lambda i: (0, 0)),
                  pl.BlockSpec((1, LANES), lambda i: (0, 0))],
        out_specs=[pl.BlockSpec((tm, LANES), lambda i: (i, 0))] * 2,
        out_shape=[jax.ShapeDtypeStruct((T, LANES), F32)] * 2,
        compiler_params=_cparams(("arbitrary",)),
        name="rope_tables",
    )(positions.reshape(T, 1), inv_t, sign)


def _rope_slab(x, cos_t, sin_t, lane):
    nxt = pltpu.roll(x, LANES - 32, axis=1)
    prv = pltpu.roll(x, 32, axis=1)
    swapped = jnp.where((lane & 32) == 0, nxt, prv)
    return x * cos_t + swapped * sin_t


def _pack_bf16_pair(lo, hi):
    lo_bits = lax.shift_right_logical(lax.bitcast_convert_type(lo.astype(BF16).astype(F32), U32), jnp.uint32(16))
    hi_bits = lax.bitcast_convert_type(hi.astype(BF16).astype(F32), U32) & jnp.uint32(0xFFFF0000)
    return hi_bits | lo_bits


def _unpack_bf16_pair(w):
    lo = lax.bitcast_convert_type(lax.shift_left(w, jnp.uint32(16)), F32)
    hi = lax.bitcast_convert_type(w & jnp.uint32(0xFFFF0000), F32)
    return lo, hi


def _norm_mod(x, g, shift, scale):
    y = x * lax.rsqrt(jnp.mean(x * x, axis=-1, keepdims=True) + EPS)
    return (y * g) * (1.0 + scale) + shift


def _rms(x, g):
    return x * lax.rsqrt(jnp.mean(x * x, axis=-1, keepdims=True) + EPS) * g


def _proj_kernel(x_ref, g_ref, sh_ref, sc_ref, w_ref, cos_ref, sin_ref,
                 gq_ref, wq_ref, gkv_ref, wkv_ref,
                 rq_ref, rk_ref, rv_ref, rg_ref, fq_ref, fk_ref, fv_ref,
                 mq_ref, mk_ref, kpe_ref, mv_ref, ff_ref):
    h = _norm_mod(x_ref[...], g_ref[...], sh_ref[...], sc_ref[...]).astype(BF16)
    cos_t = cos_ref[...]
    sin_t = sin_ref[...]
    lane = lax.broadcasted_iota(I32, cos_t.shape, 1)

    def proj(c0, width):
        return jnp.dot(h, w_ref[:, c0:c0 + width], preferred_element_type=F32)

    tail = proj(C_TAIL, LANES)
    ff_ref[...] = tail
    kpe_ref[...] = jnp.where(lane < MLA_ROPE, _rope_slab(tail, cos_t, sin_t, lane), 0.0).astype(BF16)

    qn = _rms(proj(C_MQ, MLA_Q_RANK), gq_ref[...]).astype(BF16)
    qh = jnp.dot(qn, wq_ref[...], preferred_element_type=F32)
    q_scale = (MLA_NOPE + MLA_ROPE) ** -0.5 * LOG2E
    for hd in range(MLA_HEADS):
        c0 = hd * MLA_DQ
        mq_ref[:, c0:c0 + LANES] = (qh[:, c0:c0 + LANES] * q_scale).astype(BF16)
        pe = _rope_slab(qh[:, c0 + LANES:c0 + 2 * LANES], cos_t, sin_t, lane)
        mq_ref[:, c0 + LANES:c0 + 2 * LANES] = jnp.where(lane < MLA_ROPE, pe * q_scale, 0.0).astype(BF16)

    kvn = _rms(proj(C_MKV, MLA_KV_RANK), gkv_ref[...]).astype(BF16)
    kvh = jnp.dot(kvn, wkv_ref[...], preferred_element_type=F32)
    for hd in range(MLA_HEADS):
        c0 = hd * (MLA_NOPE + MLA_V)
        mk_ref[:, hd * MLA_NOPE:(hd + 1) * MLA_NOPE] = kvh[:, c0:c0 + MLA_NOPE].astype(BF16)
        mv_ref[:, hd * MLA_V:(hd + 1) * MLA_V] = kvh[:, c0 + MLA_NOPE:c0 + MLA_NOPE + MLA_V].astype(BF16)

    rq = proj(C_RQ, 256)
    rk = proj(C_RK, 256)
    for s in range(2):
        sl = slice(s * LANES, (s + 1) * LANES)
        q2 = _rope_slab(rq[:, sl], cos_t, sin_t, lane)
        k2 = _rope_slab(rk[:, sl], cos_t, sin_t, lane) * (RET_DK ** -0.5)
        for half in range(2):
            mine = (lane < RET_DK) if half == 0 else (lane >= RET_DK)
            hs = slice((2 * s + half) * LANES, (2 * s + half + 1) * LANES)
            rq_ref[:, hs] = jnp.where(mine, q2, 0.0).astype(BF16)
            rk_ref[:, hs] = jnp.where(mine, k2, 0.0).astype(BF16)
    rv_ref[...] = proj(C_RV, 512).astype(BF16)
    rg_ref[...] = proj(C_RG, 512).astype(BF16)
    fq_ref[...] = (proj(C_FQ, 512) * (FOX_DH ** -0.5 * LOG2E)).astype(BF16)
    fk_ref[...] = proj(C_FK, 512).astype(BF16)
    fv_ref[...] = proj(C_FV, 512).astype(BF16)


def _proj(x2, x_tile0, S, g, sh, sc, w_all, cos_t, sin_t, gq, wq, gkv, wkv):
    D = x2.shape[1]
    T = cos_t.shape[0]
    tm = _pick(S, TOKEN_TILE)
    per_b = S // tm
    row = lambda i: (i, 0)
    xrow = lambda i: (i + x_tile0, 0)
    const = lambda i: (0, 0)
    batch = lambda i: (i // per_b, 0, 0)
    widths = [512, 512, 512, 512, 512, 512, 512, MLA_HEADS * MLA_DQ, MLA_HEADS * MLA_NOPE, LANES,
              MLA_HEADS * MLA_V]
    out_shape = [jax.ShapeDtypeStruct((T, w), BF16) for w in widths]
    out_shape.append(jax.ShapeDtypeStruct((T, LANES), F32))
    out_specs = [pl.BlockSpec((tm, w), row) for w in widths] + [pl.BlockSpec((tm, LANES), row)]
    return pl.pallas_call(
        _proj_kernel,
        grid=(T // tm,),
        in_specs=[pl.BlockSpec((tm, D), xrow),
                  pl.BlockSpec((1, D), const),
                  pl.BlockSpec((None, 1, D), batch),
                  pl.BlockSpec((None, 1, D), batch),
                  pl.BlockSpec((D, D_IN_PAD), const),
                  pl.BlockSpec((tm, LANES), row),
                  pl.BlockSpec((tm, LANES), row),
                  pl.BlockSpec((1, MLA_Q_RANK), const),
                  pl.BlockSpec((MLA_Q_RANK, MLA_HEADS * MLA_DQ), const),
                  pl.BlockSpec((1, MLA_KV_RANK), const),
                  pl.BlockSpec((MLA_KV_RANK, MLA_HEADS * (MLA_NOPE + MLA_V)), const)],
        out_specs=out_specs,
        out_shape=out_shape,
        compiler_params=_cparams(("arbitrary",)),
        name="proj",
    )(x2, g, sh, sc, w_all, cos_t, sin_t, gq, wq, gkv, wkv)


def _split3(x):
    a = x.astype(BF16)
    r = x - a.astype(F32)
    b = r.astype(BF16)
    c = (r - b.astype(F32)).astype(BF16)
    return a, b, c


def _fox_gate_kernel(ff_ref, fb_ref, tri_ref, kb_ref):
    tm = tri_ref.shape[0]
    lane = lax.broadcasted_iota(I32, (tm, LANES), 1)
    tri = tri_ref[...]
    carry = jnp.zeros((1, LANES), F32)
    for t in range(ff_ref.shape[0] // tm):
        rows = slice(t * tm, (t + 1) * tm)
        z = ff_ref[rows, :] + fb_ref[...]
        ls = -(jnp.maximum(-z, 0.0) + jnp.log1p(jnp.exp(-jnp.abs(z))))
        ls = jnp.where((lane >= FF_LANE) & (lane < FF_LANE + FOX_HEADS), ls, 0.0)
        f = carry
        for part in _split3(ls):
            f = f + jnp.dot(tri, part, preferred_element_type=F32)
        carry = f[tm - 1:tm, :]
        hi, mid, lo = [part.astype(F32) for part in _split3(f * LOG2E)]
        for hd in range(FOX_HEADS):
            src = FF_LANE + hd
            slab = jnp.where(lane == 0, pltpu.roll(hi, (LANES - src) % LANES, axis=1),
                   jnp.where(lane == 1, pltpu.roll(mid, (LANES + 1 - src) % LANES, axis=1),
                   jnp.where(lane == 2, pltpu.roll(lo, (LANES + 2 - src) % LANES, axis=1), 0.0)))
            kb_ref[rows, hd * LANES:(hd + 1) * LANES] = slab.astype(BF16)


def _fox_gate(ffp, fb, S):
    T = ffp.shape[0]
    tm = _pick(S, TOKEN_TILE)
    idx = jnp.arange(tm)
    tri = (idx[None, :] <= idx[:, None]).astype(BF16)
    fbv = jnp.zeros((1, LANES), F32).at[0, FF_LANE:FF_LANE + FOX_HEADS].set(fb)
    return pl.pallas_call(
        _fox_gate_kernel,
        grid=(T // S,),
        in_specs=[pl.BlockSpec((S, LANES), lambda b: (b, 0)),
                  pl.BlockSpec((1, LANES), lambda b: (0, 0)),
                  pl.BlockSpec((tm, tm), lambda b: (0, 0))],
        out_specs=pl.BlockSpec((S, FOX_HEADS * LANES), lambda b: (b, 0)),
        out_shape=jax.ShapeDtypeStruct((T, FOX_HEADS * LANES), BF16),
        compiler_params=_cparams(("arbitrary",)),
        name="fox_gate",
    )(ffp, fbv, tri)


def _retention_kernel(dchunk_ref, q_ref, k_ref, v_ref, g_ref, dmask_ref, din_ref, dout_ref,
                      o_ref, state_ref, *, n_chunks):
    @pl.when(pl.program_id(1) == 0)
    def _():
        state_ref[...] = jnp.zeros_like(state_ref)

    C = RET_CHUNK
    for ci in range(n_chunks):
        rows = slice(ci * C, (ci + 1) * C)
        for hd in range(RET_HEADS):
            slab = slice(hd * LANES, (hd + 1) * LANES)
            q = q_ref[rows, slab]
            k = k_ref[rows, slab]
            vcols = slice(hd * RET_DV, (hd + 1) * RET_DV)
            v = v_ref[rows, vcols]
            state = state_ref[hd]
            scores = lax.dot_general(q, k, (((1,), (1,)), ((), ())),
                                     preferred_element_type=F32) * dmask_ref[hd]
            inner = jnp.dot(scores.astype(BF16), v, preferred_element_type=F32)
            cross = jnp.dot(q, state.astype(BF16), preferred_element_type=F32) * din_ref[hd]
            o = inner + cross
            vd = (v.astype(F32) * dout_ref[hd]).astype(BF16)
            kv = lax.dot_general(k, vd, (((0,), (0,)), ((), ())), preferred_element_type=F32)
            state_ref[hd] = state * dchunk_ref[hd] + kv
            mu = jnp.mean(o, axis=-1, keepdims=True)
            d = o - mu
            var = jnp.mean(d * d, axis=-1, keepdims=True)
            on = d * lax.rsqrt(var + EPS)
            g = g_ref[rows, vcols].astype(F32)
            o_ref[rows, vcols] = (g * jax.nn.sigmoid(g) * on).astype(BF16)


def _retention(rq, rk, rv, rg, B, S):
    H, C = RET_HEADS, RET_CHUNK
    tr = _pick(S, RET_CHUNKS_PER_STEP * C)
    n_chunks = tr // C
    log_gamma = jnp.log1p(-jnp.exp2(-5.0 - jnp.arange(H, dtype=F32)))
    idx = jnp.arange(C, dtype=F32)
    rel = idx[:, None] - idx[None, :]
    dmask = jnp.where(rel >= 0, jnp.exp(log_gamma[:, None, None] * jnp.maximum(rel, 0.0)), 0.0)
    decay_in = jnp.exp(log_gamma[:, None] * (idx + 1.0))
    decay_out = jnp.exp(log_gamma[:, None] * (C - 1.0 - idx))
    decay_chunk = jnp.exp(log_gamma * C)
    din = jnp.broadcast_to(decay_in[:, :, None], (H, C, RET_DV))
    dout = jnp.broadcast_to(decay_out[:, :, None], (H, C, RET_DV))
    tok = lambda b, i: (b, i, 0)
    const3 = lambda b, i: (0, 0, 0)
    return pl.pallas_call(
        functools.partial(_retention_kernel, n_chunks=n_chunks),
        grid=(B, S // tr),
        in_specs=[pl.BlockSpec(memory_space=pltpu.SMEM),
                  pl.BlockSpec((None, tr, H * LANES), tok),
                  pl.BlockSpec((None, tr, H * LANES), tok),
                  pl.BlockSpec((None, tr, H * RET_DV), tok),
                  pl.BlockSpec((None, tr, H * RET_DV), tok),
                  pl.BlockSpec((H, C, C), const3),
                  pl.BlockSpec((H, C, RET_DV), const3),
                  pl.BlockSpec((H, C, RET_DV), const3)],
        out_specs=pl.BlockSpec((None, tr, H * RET_DV), tok),
        out_shape=jax.ShapeDtypeStruct((B, S, H * RET_DV), BF16),
        scratch_shapes=[pltpu.VMEM((H, LANES, RET_DV), F32)],
        compiler_params=_cparams(("arbitrary", "arbitrary")),
        name="retention",
    )(decay_chunk, rq.reshape(B, S, -1), rk.reshape(B, S, -1), rv.reshape(B, S, -1),
      rg.reshape(B, S, -1), dmask, din, dout)


def _flash_kernel(q_ref, k_ref, ke_ref, v_ref, o_ref, m_ref, l_ref, acc_ref, s_ref, *, tq, q_bias_cols):
    S = q_ref.shape[0]
    nq = S // tq
    tk = tq
    m_ref[...] = jnp.full(m_ref.shape, NEG_BIG, F32)
    l_ref[...] = jnp.zeros_like(l_ref)
    acc_ref[...] = jnp.zeros_like(acc_ref)
    lane = lax.broadcasted_iota(I32, (tq, LANES), 1)

    half = tk // 2
    tri = (lax.broadcasted_iota(I32, (half, half), 0) <= lax.broadcasted_iota(I32, (half, half), 1))
    nt = lambda a, b: lax.dot_general(a, b, (((1,), (1,)), ((), ())), preferred_element_type=F32)

    def issue(n):
        i, j = pairs[n]
        q = q_ref[i * tq:(i + 1) * tq, :]
        if q_bias_cols:
            q = jnp.concatenate([q, jnp.where(lane < q_bias_cols, -1.0, 0.0).astype(BF16)], axis=1)
        kj = jnp.concatenate([k_ref[j * tk:(j + 1) * tk, :], ke_ref[j * tk:(j + 1) * tk, :]], axis=1)
        if j < i:
            s_ref[n % 2] = nt(kj, q)
        else:
            s_ref[n % 2, 0:half, :] = nt(kj[0:half], q)
            s_ref[n % 2, half:, half:] = nt(kj[half:], q[half:])

    def update(s, i, c0, c1, r0, r1):
        m_prev = m_ref[i, :, c0:c1]
        m_new = jnp.maximum(m_prev, jnp.max(s, axis=0, keepdims=True))
        alpha = jnp.exp2(m_prev - m_new)
        p = jnp.exp2(s - m_new)
        l_ref[i, :, c0:c1] = alpha * l_ref[i, :, c0:c1] + jnp.sum(p, axis=0, keepdims=True)
        acc_ref[i, :, c0:c1] = alpha * acc_ref[i, :, c0:c1] + lax.dot_general(
            v_ref[r0:r1, :], p.astype(BF16), (((0,), (0,)), ((), ())),
            preferred_element_type=F32)
        m_ref[i, :, c0:c1] = m_new

    def consume(n):
        i, j = pairs[n]
        slot = n % 2
        if j < i:
            update(s_ref[slot], i, 0, tq, j * tk, (j + 1) * tk)
        else:
            left = jnp.where(tri, s_ref[slot, 0:half, 0:half], NEG_BIG)
            right = jnp.concatenate([s_ref[slot, 0:half, half:],
                                     jnp.where(tri, s_ref[slot, half:, half:], NEG_BIG)], axis=0)
            update(left, i, 0, half, j * tk, j * tk + half)
            update(right, i, half, tq, j * tk, (j + 1) * tk)
            o_ref[i * tq:(i + 1) * tq, :] = (acc_ref[i] / l_ref[i]).T.astype(o_ref.dtype)

    pairs = [(i, j) for i in range(nq) for j in range(i + 1)]
    issue(0)
    for n in range(len(pairs)):
        if n + 1 < len(pairs):
            issue(n + 1)
        consume(n)


def _flash(q, k, ke, v, B, S, H, dq, ke_per_head, q_bias_cols, name):
    tq = _pick(S, ATTN_TILE)
    nq = S // tq
    ke_map = (lambda b, h: (b, 0, h)) if ke_per_head else (lambda b, h: (b, 0, 0))
    return pl.pallas_call(
        functools.partial(_flash_kernel, tq=tq, q_bias_cols=q_bias_cols),
        grid=(B, H),
        in_specs=[pl.BlockSpec((None, S, dq), lambda b, h: (b, 0, h)),
                  pl.BlockSpec((None, S, LANES), lambda b, h: (b, 0, h)),
                  pl.BlockSpec((None, S, LANES), ke_map),
                  pl.BlockSpec((None, S, LANES), lambda b, h: (b, 0, h))],
        out_specs=pl.BlockSpec((None, S, LANES), lambda b, h: (b, 0, h)),
        out_shape=jax.ShapeDtypeStruct((B, S, H * LANES), BF16),
        scratch_shapes=[pltpu.VMEM((nq, 1, tq), F32), pltpu.VMEM((nq, 1, tq), F32),
                        pltpu.VMEM((nq, LANES, tq), F32), pltpu.VMEM((2, tq, tq), F32)],
        compiler_params=_cparams(("arbitrary", "arbitrary")),
        name=name,
    )(q, k, ke, v)


def _merge_kernel(x_ref, ya_ref, yb_ref, yc_ref, g1_ref, sh1_ref, sc1_ref, gt1_ref,
                  g2_ref, sh2_ref, sc2_ref, gw_ref, gb_ref, bw_ref, ow_ref,
                  rwh_ref, rwl_ref, rb_ref, tri_ref,
                  x1_ref, h2_ref, mi_ref, mf_ref, cnt_ref, h_ref, mg_ref, carry_ref):
    tm, D = x_ref.shape
    cn = MERGE_COL_CHUNK

    @pl.when(pl.program_id(0) == 0)
    def _():
        carry_ref[...] = jnp.zeros_like(carry_ref)

    h_ref[...] = _norm_mod(x_ref[...], g1_ref[...], sh1_ref[...], sc1_ref[...]).astype(BF16)
    for n in range(D // cn):
        cols = slice(n * cn, (n + 1) * cn)
        merged = None
        for i, y_ref in enumerate((ya_ref, yb_ref, yc_ref)):
            gcols = slice(i * D + n * cn, i * D + (n + 1) * cn)
            gate = jax.nn.sigmoid(jnp.dot(h_ref[...], gw_ref[:, gcols], preferred_element_type=F32)
                                  + gb_ref[:, gcols])
            br = jnp.dot(y_ref[...], bw_ref[i, :, cols], preferred_element_type=F32)
            merged = gate * br if merged is None else merged + gate * br
        mg_ref[:, cols] = merged.astype(BF16)
    for n in range(D // cn):
        cols = slice(n * cn, (n + 1) * cn)
        mix = jnp.dot(mg_ref[...], ow_ref[:, cols], preferred_element_type=F32)
        x1_ref[:, cols] = x_ref[:, cols] + gt1_ref[:, cols] * mix
    h2 = _norm_mod(x1_ref[...], g2_ref[...], sh2_ref[...], sc2_ref[...])
    h2_ref[...] = _pack_bf16_pair(h2[:, :D // 2], h2[:, D // 2:])

    hh = h2.astype(BF16)
    hl = (h2 - hh.astype(F32)).astype(BF16)
    nt = lambda a, b: lax.dot_general(a, b, (((1,), (1,)), ((), ())), preferred_element_type=F32)
    lt = nt(rwh_ref[...], hh) + nt(rwh_ref[...], hl) + nt(rwl_ref[...], hh) + rb_ref[...]
    row8 = lax.broadcasted_iota(I32, (EXP_PER_GROUP, tm), 0)
    gl = jnp.where(row8 < N_GROUPS, lt[N_EXPERTS:N_EXPERTS + EXP_PER_GROUP, :], -jnp.inf)
    gmax = jnp.max(gl, axis=0, keepdims=True)
    g_idx = jnp.min(jnp.where(gl == gmax, row8, EXP_PER_GROUP), axis=0, keepdims=True)
    g_w = 1.0 / jnp.sum(jnp.exp(gl - gmax), axis=0, keepdims=True)
    el = lt[(N_GROUPS - 1) * EXP_PER_GROUP:N_EXPERTS, :]
    for g in range(N_GROUPS - 2, -1, -1):
        el = jnp.where(g_idx == g, lt[g * EXP_PER_GROUP:(g + 1) * EXP_PER_GROUP, :], el)
    e1 = jnp.max(el, axis=0, keepdims=True)
    i1 = jnp.min(jnp.where(el == e1, row8, EXP_PER_GROUP), axis=0, keepdims=True)
    el2 = jnp.where(row8 == i1, -jnp.inf, el)
    e2 = jnp.max(el2, axis=0, keepdims=True)
    i2 = jnp.min(jnp.where(el2 == e2, row8, EXP_PER_GROUP), axis=0, keepdims=True)
    r = jnp.exp(e2 - e1)
    w1 = g_w / (1.0 + r)
    w2 = g_w * r / (1.0 + r)
    eid1 = g_idx * EXP_PER_GROUP + i1
    eid2 = g_idx * EXP_PER_GROUP + i2

    rowe = lax.broadcasted_iota(I32, (N_EXPERTS, tm), 0)
    hit1 = rowe == eid1
    hit2 = rowe == eid2
    onehot = jnp.where(hit1 | hit2, 1.0, 0.0)
    before = (jnp.dot(onehot.astype(BF16), tri_ref[...], preferred_element_type=F32)
              + jnp.concatenate([carry_ref[...]] * (tm // LANES), axis=1))
    rank1 = jnp.sum(jnp.where(hit1, before, 0.0), axis=0, keepdims=True)
    rank2 = jnp.sum(jnp.where(hit2, before, 0.0), axis=0, keepdims=True)
    carry_ref[...] = carry_ref[...] + jnp.sum(onehot, axis=1, keepdims=True)
    cnt_ref[...] = carry_ref[...]

    mi_ref[...] = jnp.where(row8 == 0, eid1,
                  jnp.where(row8 == 1, eid2,
                  jnp.where(row8 == 2, rank1.astype(I32),
                  jnp.where(row8 == 3, rank2.astype(I32), 0))))
    mf_ref[...] = jnp.where(row8 == 0, w1, jnp.where(row8 == 1, w2, 0.0))


def _merge(x2, x_tile0, S, ya, yb, yc, g1, sh1, sc1, gt1, g2, sh2, sc2, gw, gb, bw, ow, rwh, rwl, rb):
    D = x2.shape[1]
    T = ya.shape[0]
    tm = _pick(S, MERGE_TILE)
    per_b = S // tm
    row = lambda i: (i, 0)
    xrow = lambda i: (i + x_tile0 * _pick(S, TOKEN_TILE) // tm, 0)
    col = lambda i: (0, i)
    const = lambda i: (0, 0)
    batch = lambda i: (i // per_b, 0, 0)
    idx = jnp.arange(tm)
    tri = (idx[:, None] < idx[None, :]).astype(BF16)
    vecb = pl.BlockSpec((None, 1, D), batch)
    return pl.pallas_call(
        _merge_kernel,
        grid=(T // tm,),
        in_specs=[pl.BlockSpec((tm, D), xrow),
                  pl.BlockSpec((tm, BRANCH_W), row),
                  pl.BlockSpec((tm, BRANCH_W), row),
                  pl.BlockSpec((tm, BRANCH_W), row),
                  pl.BlockSpec((1, D), const), vecb, vecb, vecb,
                  pl.BlockSpec((1, D), const), vecb, vecb,
                  pl.BlockSpec((D, N_BRANCH * D), const),
                  pl.BlockSpec((1, N_BRANCH * D), const),
                  pl.BlockSpec((N_BRANCH, BRANCH_W, D), lambda i: (0, 0, 0)),
                  pl.BlockSpec((D, D), const),
                  pl.BlockSpec((LANES, D), const),
                  pl.BlockSpec((LANES, D), const),
                  pl.BlockSpec((LANES, 1), const),
                  pl.BlockSpec((tm, tm), const)],
        out_specs=[pl.BlockSpec((tm, D), row), pl.BlockSpec((tm, D // 2), row),
                   pl.BlockSpec((8, tm), col), pl.BlockSpec((8, tm), col),
                   pl.BlockSpec((N_EXPERTS, LANES), const)],
        out_shape=[jax.ShapeDtypeStruct((T, D), F32), jax.ShapeDtypeStruct((T, D // 2), U32),
                   jax.ShapeDtypeStruct((8, T), I32), jax.ShapeDtypeStruct((8, T), F32),
                   jax.ShapeDtypeStruct((N_EXPERTS, LANES), F32)],
        scratch_shapes=[pltpu.VMEM((tm, D), BF16), pltpu.VMEM((tm, D), BF16),
                        pltpu.VMEM((N_EXPERTS, LANES), F32)],
        compiler_params=_cparams(("arbitrary",)),
        name="merge",
    )(x2, ya, yb, yc, g1, sh1, sc1, gt1, g2, sh2, sc2, gw, gb, bw, ow, rwh, rwl, rb, tri)


def _expert_kernel(blk_e_ref, nvalid_ref, xs_ref, w1_ref, w3_ref, w2_ref, ys_ref, w1b_ref, w3b_ref, w2b_ref):
    i = pl.program_id(0)
    valid = i < nvalid_ref[0]
    new_expert = jnp.logical_or(i == 0, blk_e_ref[i] != blk_e_ref[jnp.maximum(i - 1, 0)])

    @pl.when(jnp.logical_and(valid, new_expert))
    def _():
        w1b_ref[...] = w1_ref[...].astype(BF16)
        w3b_ref[...] = w3_ref[...].astype(BF16)
        w2b_ref[...] = w2_ref[...].astype(BF16)

    @pl.when(valid)
    def _():
        half = xs_ref.shape[1]
        x_lo, x_hi = _unpack_bf16_pair(xs_ref[...])
        x_lo = x_lo.astype(BF16)
        x_hi = x_hi.astype(BF16)

        def up(w_ref):
            return (jnp.dot(x_lo, w_ref[:half, :], preferred_element_type=F32)
                    + jnp.dot(x_hi, w_ref[half:, :], preferred_element_type=F32))

        a = up(w1b_ref)
        b = up(w3b_ref)
        hid = (a * jax.nn.sigmoid(a) * b).astype(BF16)
        y = jnp.dot(hid, w2b_ref[...], preferred_element_type=F32)
        ys_ref[...] = _pack_bf16_pair(y[:, :half], y[:, half:])

    @pl.when(jnp.logical_not(valid))
    def _():
        ys_ref[...] = jnp.zeros_like(ys_ref)


def _experts(xs, blk_e, nvalid, w1, w3, w2, layer, tb):
    P, half = xs.shape
    D = 2 * half
    n_blocks = P // tb
    rows = lambda i, be, nv: (jnp.minimum(i, nv[0] - 1), 0)
    grid_spec = pltpu.PrefetchScalarGridSpec(
        num_scalar_prefetch=2,
        grid=(n_blocks,),
        in_specs=[pl.BlockSpec((tb, half), rows),
                  pl.BlockSpec((None, None, D, D_EXPERT), lambda i, be, nv: (layer, be[i], 0, 0)),
                  pl.BlockSpec((None, None, D, D_EXPERT), lambda i, be, nv: (layer, be[i], 0, 0)),
                  pl.BlockSpec((None, None, D_EXPERT, D), lambda i, be, nv: (layer, be[i], 0, 0))],
        out_specs=pl.BlockSpec((tb, half), lambda i, be, nv: (i, 0)),
        scratch_shapes=[pltpu.VMEM((D, D_EXPERT), BF16), pltpu.VMEM((D, D_EXPERT), BF16),
                        pltpu.VMEM((D_EXPERT, D), BF16)],
    )
    return pl.pallas_call(
        _expert_kernel,
        grid_spec=grid_spec,
        out_shape=jax.ShapeDtypeStruct((P, half), U32),
        compiler_params=_cparams(("arbitrary",)),
        name="moe_experts",
    )(blk_e, nvalid, xs, w1, w3, w2)


def _sc_gather(data, idx):
    M = idx.shape[0]
    D = data.shape[1]
    W = SC_GATHER_WINDOW
    assert M % W == 0, (M, W)
    mesh = plsc.VectorSubcoreMesh(core_axis_name="core", subcore_axis_name="subcore")
    n_workers = mesh.num_cores * mesh.num_subcores
    assert M % (W * n_workers) == 0, (M, W, n_workers)

    @functools.partial(pl.kernel, out_type=jax.ShapeDtypeStruct((M, D), data.dtype), mesh=mesh,
                       scratch_types=[pltpu.VMEM((W,), I32), pltpu.VMEM((W, D), data.dtype)])
    def gather_kernel(x_hbm, i_hbm, o_hbm, i_vmem, buf):
        worker = lax.axis_index("core") * mesh.num_subcores + lax.axis_index("subcore")

        @pl.loop(0, M // (W * n_workers))
        def _(t):
            start = (t * n_workers + worker) * W
            pltpu.sync_copy(i_hbm.at[pl.ds(start, W)], i_vmem)
            pltpu.sync_copy(x_hbm.at[i_vmem], buf)
            pltpu.sync_copy(buf, o_hbm.at[pl.ds(start, W)])

    return gather_kernel(data, idx)


def _sc_dispatch(h2, dest, fill_idx, P):
    T, D = h2.shape
    W = SC_GATHER_WINDOW
    n_fill = fill_idx.shape[0]
    mesh = plsc.VectorSubcoreMesh(core_axis_name="core", subcore_axis_name="subcore")
    n_workers = mesh.num_cores * mesh.num_subcores
    assert T % (W * n_workers) == 0 and n_fill % (W * n_workers) == 0, (T, n_fill, W, n_workers)
    zeros = jnp.zeros((W, D), h2.dtype)

    @functools.partial(pl.kernel, out_type=jax.ShapeDtypeStruct((P, D), h2.dtype), mesh=mesh,
                       scratch_types=[pltpu.VMEM((W,), I32), pltpu.VMEM((W, D), h2.dtype)])
    def dispatch_kernel(h_hbm, d_hbm, f_hbm, z_hbm, o_hbm, i_vmem, buf):
        worker = lax.axis_index("core") * mesh.num_subcores + lax.axis_index("subcore")

        @pl.loop(0, T // (W * n_workers))
        def _(t):
            start = (t * n_workers + worker) * W
            pltpu.sync_copy(h_hbm.at[pl.ds(start, W)], buf)
            for c in range(2):
                pltpu.sync_copy(d_hbm.at[pl.ds(c * T + start, W)], i_vmem)
                pltpu.sync_copy(buf, o_hbm.at[i_vmem])

        pltpu.sync_copy(z_hbm, buf)

        @pl.loop(0, n_fill // (W * n_workers))
        def _(t):
            start = (t * n_workers + worker) * W
            pltpu.sync_copy(f_hbm.at[pl.ds(start, W)], i_vmem)
            pltpu.sync_copy(buf, o_hbm.at[i_vmem])

    return dispatch_kernel(h2, dest, fill_idx, zeros)


def _combine_kernel(x_ref, g0_ref, g1_ref, mf_ref, gt_ref, fg_ref, *rest, final):
    o_ref = rest[-1]
    mf = mf_ref[...]
    lo0, hi0 = _unpack_bf16_pair(g0_ref[...])
    lo1, hi1 = _unpack_bf16_pair(g1_ref[...])
    w0 = mf[:, 0:1]
    w1 = mf[:, 1:2]
    ffn = jnp.concatenate([lo0 * w0 + lo1 * w1, hi0 * w0 + hi1 * w1], axis=1)
    out = x_ref[...] + gt_ref[...] * ffn
    if final:
        out = _rms(out, fg_ref[...])
    o_ref[...] = out


def _combine(x1, S, mf, gt2, final_g, g, final, out_rows=None, out_tile0=0, out_prev=None):
    T, D = x1.shape
    tm = _pick(S, TOKEN_TILE)
    per_b = S // tm
    nt = T // tm
    in_specs = [pl.BlockSpec((tm, D), lambda i: (i, 0)),
                pl.BlockSpec((tm, D // 2), lambda i: (i, 0)),
                pl.BlockSpec((tm, D // 2), lambda i: (i + nt, 0)),
                pl.BlockSpec((tm, 2), lambda i: (i, 0)),
                pl.BlockSpec((None, 1, D), lambda i: (i // per_b, 0, 0)),
                pl.BlockSpec((1, D), lambda i: (0, 0))]
    args = [x1, g, g, mf, gt2, final_g]
    aliases = {}
    if out_prev is not None:
        in_specs.append(pl.BlockSpec(memory_space=pl.ANY))
        args.append(out_prev)
        aliases = {len(args) - 1: 0}
    return pl.pallas_call(
        functools.partial(_combine_kernel, final=final),
        grid=(nt,),
        in_specs=in_specs,
        out_specs=pl.BlockSpec((tm, D), lambda i: (i + out_tile0, 0)),
        out_shape=jax.ShapeDtypeStruct((T if out_rows is None else out_rows, D), F32),
        input_output_aliases=aliases,
        compiler_params=_cparams(("arbitrary",)),
        name="moe_combine",
    )(*args)


def _prep_w_in(w):
    offs = [0]
    for s in (256, 256, 512, 512, 512, 512, 512, FOX_HEADS, MLA_Q_RANK, MLA_KV_RANK, MLA_ROPE):
        offs.append(offs[-1] + s)
    rq, rk, rv, rg, fq, fk, fv, ff, mq, mkv, mkr = [w[:, offs[i]:offs[i + 1]] for i in range(11)]
    pad = jnp.zeros((w.shape[0], LANES - MLA_ROPE - FOX_HEADS), w.dtype)
    return jnp.concatenate([rq, rk, rv, rg, fq, fk, fv, mq, mkv, mkr, ff, pad], axis=1).astype(BF16)


def _prep_wq_up(w):
    r = w.reshape(MLA_Q_RANK, MLA_HEADS, MLA_NOPE + MLA_ROPE)
    r = jnp.pad(r, ((0, 0), (0, 0), (0, MLA_DQ - MLA_NOPE - MLA_ROPE)))
    return r.reshape(MLA_Q_RANK, MLA_HEADS * MLA_DQ).astype(BF16)


def _prep_router(w_grp, b_grp, w_exp, b_exp):
    D = w_grp.shape[0]
    pad = LANES - N_EXPERTS - N_GROUPS
    rwt = jnp.concatenate([w_exp, w_grp, jnp.zeros((D, pad), F32)], axis=1).astype(F32).T
    rwh = rwt.astype(BF16)
    rwl = (rwt - rwh.astype(F32)).astype(BF16)
    rb = jnp.concatenate([b_exp, b_grp, jnp.zeros((pad,), F32)]).astype(F32).reshape(LANES, 1)
    return rwh, rwl, rb


def kernel(x, c, positions, ada_w, ada_b, norm1_g, norm2_g, w_in, fox_fb, mla_q_norm_g, mla_wq_up, mla_kv_norm_g, mla_wkv_up, gate_w, gate_b, branch_w, out_w, router_grp_w, router_grp_b, router_exp_w, router_exp_b, exp_w1, exp_w3, exp_w2, final_g):
    B, S, D = x.shape
    L = ada_w.shape[0]
    T = B * S
    n_str = N_STREAMS if B % N_STREAMS == 0 else 1
    Bs = B // n_str
    Ts = Bs * S
    As = 2 * Ts
    tb = _pick(As, EXPERT_BLOCK)
    n_blocks = As // tb + N_EXPERTS
    P = n_blocks * tb
    tiles_per_stream = Ts // _pick(S, TOKEN_TILE)

    mod = _adaln(c, ada_w, ada_b)
    cos_t, sin_t = _rope_tables(positions)
    cos_s = [cos_t[h * Ts:(h + 1) * Ts] for h in range(n_str)]
    sin_s = [sin_t[h * Ts:(h + 1) * Ts] for h in range(n_str)]
    x_full = x.reshape(T, D)
    xs2 = [x_full] * n_str
    x_tile0 = [h * tiles_per_stream for h in range(n_str)]
    final_g2 = final_g.reshape(1, D)
    r3 = lambda a: a.reshape(Bs, S, -1)
    out = None

    for l in range(L):
        mods = [[mod[l, h * Bs:(h + 1) * Bs, i * D:(i + 1) * D].reshape(Bs, 1, D) for i in range(6)]
                for h in range(n_str)]
        g1 = norm1_g[l].reshape(1, D)
        g2 = norm2_g[l].reshape(1, D)
        w_all = _prep_w_in(w_in[l])
        wq = _prep_wq_up(mla_wq_up[l])
        wkv = mla_wkv_up[l].astype(BF16)
        gq = mla_q_norm_g[l].reshape(1, -1)
        gkv = mla_kv_norm_g[l].reshape(1, -1)
        rwh, rwl, rb = _prep_router(router_grp_w[l], router_grp_b[l], router_exp_w[l], router_exp_b[l])
        gw = gate_w[l].astype(BF16)
        gb = gate_b[l].reshape(1, -1)
        bw = branch_w[l].astype(BF16)
        ow = out_w[l].astype(BF16)
        last = l == L - 1

        merged = []
        for h in range(n_str):
            sh1, sc1, gt1, sh2, sc2, gt2 = mods[h]
            (rq, rk, rv, rg, fq, fk, fv, mq, mk, kpe, mv, ffp) = _proj(
                xs2[h], x_tile0[h], S, g1, sh1, sc1, w_all, cos_s[h], sin_s[h], gq, wq, gkv, wkv)
            ya = _retention(rq, rk, rv, rg, Bs, S).reshape(Ts, -1)
            kb = _fox_gate(ffp, fox_fb[l], S)
            yb = _flash(r3(fq), r3(fk), r3(kb), r3(fv), Bs, S, FOX_HEADS, FOX_DH, True, 3,
                        "flash_fox").reshape(Ts, -1)
            yc = _flash(r3(mq), r3(mk), r3(kpe), r3(mv), Bs, S, MLA_HEADS, MLA_DQ, False, 0,
                        "flash_mla").reshape(Ts, -1)
            merged.append(_merge(xs2[h], x_tile0[h], S, ya, yb, yc, g1, sh1, sc1, gt1, g2, sh2, sc2,
                                 gw, gb, bw, ow, rwh, rwl, rb))

        routed = []
        for h in range(n_str):
            x1, h2, mi, mf, cnt = merged[h]
            counts = cnt[:, 0].astype(I32)
            pcounts = (counts + tb - 1) // tb * tb
            pends = jnp.cumsum(pcounts)
            pstarts = pends - pcounts
            sel = mi[0:2, :, None] == jnp.arange(N_EXPERTS, dtype=I32)
            dest = (jnp.sum(jnp.where(sel, pstarts, 0), axis=-1) + mi[2:4]).reshape(As)
            blk_pos = jnp.arange(n_blocks, dtype=I32) * tb
            blk_e = jnp.minimum(jnp.sum((pends[None, :] <= blk_pos[:, None]).astype(I32), axis=1),
                                N_EXPERTS - 1)
            nvalid = (pends[-1:] // tb).astype(I32)
            fr = jnp.arange(tb, dtype=I32)[None, :]
            is_pad = (fr < (pcounts - counts)[:, None]).reshape(-1)
            pad_slot = ((pstarts + counts)[:, None] + fr).reshape(-1)
            tail_rank = jnp.cumsum(jnp.logical_not(is_pad).astype(I32)) - 1
            fill_idx = jnp.where(is_pad, pad_slot, pends[-1] + tail_rank)
            routed.append((dest, blk_e, nvalid, _sc_dispatch(h2, dest, fill_idx, P)))

        ys = [_experts(routed[h][3], routed[h][1], routed[h][2], exp_w1, exp_w3, exp_w2, l, tb)
              for h in range(n_str)]
        gathered = [_sc_gather(ys[h], routed[h][0]) for h in range(n_str)]
        for h in range(n_str):
            x1, _, _, mf, _ = merged[h]
            gt2 = mods[h][5]
            if last:
                out = _combine(x1, S, mf[0:2].T, gt2, final_g2, gathered[h], True,
                               out_rows=T, out_tile0=h * tiles_per_stream, out_prev=out)
            else:
                xs2[h] = _combine(x1, S, mf[0:2].T, gt2, final_g2, gathered[h], False)
        x_tile0 = [0] * n_str

    return out.reshape(B, S, D)
```

```python
import functools
import math

import jax
import jax.numpy as jnp
from jax import lax
from jax.experimental import pallas as pl
from jax.experimental.pallas import tpu as pltpu
from jax.experimental.pallas import tpu_sc as plsc

F32 = jnp.float32
BF16 = jnp.bfloat16
I32 = jnp.int32
U32 = jnp.uint32
HIGHEST = lax.Precision.HIGHEST

EPS = 1e-6
ROPE_THETA = 10000.0
RET_HEADS = 4
RET_DK = 64
RET_DV = 128
RET_CHUNK = 128
FOX_HEADS = 4
FOX_DH = 128
MLA_HEADS = 4
MLA_Q_RANK = 256
MLA_KV_RANK = 128
MLA_NOPE = 128
MLA_ROPE = 64
MLA_V = 128
MLA_DQ = 256
N_BRANCH = 3
BRANCH_W = 512
N_GROUPS = 4
EXP_PER_GROUP = 8
N_EXPERTS = N_GROUPS * EXP_PER_GROUP
D_EXPERT = 512

LANES = 128
V7X_VMEM_LIMIT = 56 * 1024 * 1024

C_RQ, C_RK, C_RV, C_RG = 0, 256, 512, 1024
C_FQ, C_FK, C_FV = 1536, 2048, 2560
C_MQ, C_MKV, C_TAIL = 3072, 3328, 3456
D_IN_PAD = 3584
FF_LANE = MLA_ROPE

NEG_BIG = -1e30

TOKEN_TILE = 512
ATTN_TILE = 1024
MERGE_TILE = 1024
PROJ_TILE = 1024
EXPERT_BLOCK = 512
RET_CHUNKS_PER_STEP = 32
ADALN_COL_TILE = 1536
ROPE_TABLE_TILE = 2048
SC_GATHER_WINDOW = 128
N_STREAMS = 1
MERGE_COL_CHUNK = 256
LOG2E = math.log2(math.e)


def _cparams(sem):
    return pltpu.CompilerParams(dimension_semantics=sem, vmem_limit_bytes=V7X_VMEM_LIMIT)


def _pick(n, pref):
    t = min(n, pref)
    assert n % t == 0, (n, t)
    return t


def _adaln_kernel(c_ref, w_ref, b_ref, o_ref):
    c = c_ref[...]
    ca = c * jax.nn.sigmoid(c)
    o_ref[...] = jnp.dot(ca, w_ref[...], preferred_element_type=F32, precision=HIGHEST) + b_ref[...]


def _adaln(c, ada_w, ada_b):
    L, D, N = ada_w.shape
    B = c.shape[0]
    tn = _pick(N, ADALN_COL_TILE)
    return pl.pallas_call(
        _adaln_kernel,
        grid=(L, N // tn),
        in_specs=[pl.BlockSpec((B, D), lambda l, j: (0, 0)),
                  pl.BlockSpec((None, D, tn), lambda l, j: (l, 0, j)),
                  pl.BlockSpec((None, 1, tn), lambda l, j: (l, 0, j))],
        out_specs=pl.BlockSpec((None, B, tn), lambda l, j: (l, 0, j)),
        out_shape=jax.ShapeDtypeStruct((L, B, N), F32),
        compiler_params=_cparams(("arbitrary", "arbitrary")),
        name="adaln",
    )(c, ada_w, ada_b.reshape(L, 1, N))


def _rope_table_kernel(pos_ref, inv_ref, sign_ref, cos_ref, sin_ref):
    ang = pos_ref[...].astype(F32) * inv_ref[...]
    cos_ref[...] = jnp.cos(ang)
    sin_ref[...] = jnp.sin(ang) * sign_ref[...]


def _rope_tables(positions):
    T = positions.size
    tm = _pick(T, ROPE_TABLE_TILE)
    half = MLA_ROPE // 2
    inv = ROPE_THETA ** (-jnp.arange(0, MLA_ROPE, 2, dtype=F32) / MLA_ROPE)
    inv_t = jnp.tile(inv, LANES // half).reshape(1, LANES)
    sign = jnp.where((jnp.arange(LANES) % MLA_ROPE) < half, -1.0, 1.0).astype(F32).reshape(1, LANES)
    return pl.pallas_call(
        _rope_table_kernel,
        grid=(T // tm,),
        in_specs=[pl.BlockSpec((tm, 1), lambda i: (i, 0)),
                  pl.BlockSpec((1, LANES), lambda i: (0, 0)),
                  pl.BlockSpec((1, LANES), lambda i: (0, 0))],
        out_specs=[pl.BlockSpec((tm, LANES), lambda i: (i, 0))] * 2,
        out_shape=[jax.ShapeDtypeStruct((T, LANES), F32)] * 2,
        compiler_params=_cparams(("arbitrary",)),
        name="rope_tables",
    )(positions.reshape(T, 1), inv_t, sign)


def _rope_slab(x, cos_t, sin_t, lane):
    nxt = pltpu.roll(x, LANES - 32, axis=1)
    prv = pltpu.roll(x, 32, axis=1)
    swapped = jnp.where((lane & 32) == 0, nxt, prv)
    return x * cos_t + swapped * sin_t


def _pack_bf16_pair(lo, hi):
    lo_bits = lax.shift_right_logical(lax.bitcast_convert_type(lo.astype(BF16).astype(F32), U32), jnp.uint32(16))
    hi_bits = lax.bitcast_convert_type(hi.astype(BF16).astype(F32), U32) & jnp.uint32(0xFFFF0000)
    return hi_bits | lo_bits


def _unpack_bf16_pair(w):
    lo = lax.bitcast_convert_type(lax.shift_left(w, jnp.uint32(16)), F32)
    hi = lax.bitcast_convert_type(w & jnp.uint32(0xFFFF0000), F32)
    return lo, hi


def _norm_mod(x, g, shift, scale):
    y = x * lax.rsqrt(jnp.mean(x * x, axis=-1, keepdims=True) + EPS)
    return (y * g) * (1.0 + scale) + shift


def _rms(x, g):
    return x * lax.rsqrt(jnp.mean(x * x, axis=-1, keepdims=True) + EPS) * g


def _proj_kernel(x_ref, g_ref, sh_ref, sc_ref, w_ref, cos_ref, sin_ref,
                 gq_ref, wq_ref, gkv_ref, wkv_ref,
                 rq_ref, rk_ref, rv_ref, rg_ref, fq_ref, fk_ref, fv_ref,
                 mq_ref, mk_ref, kpe_ref, mv_ref, ff_ref):
    h = _norm_mod(x_ref[...], g_ref[...], sh_ref[...], sc_ref[...]).astype(BF16)
    cos_t = cos_ref[...]
    sin_t = sin_ref[...]
    lane = lax.broadcasted_iota(I32, cos_t.shape, 1)

    def proj(c0, width):
        return jnp.dot(h, w_ref[:, c0:c0 + width], preferred_element_type=F32)

    tail = proj(C_TAIL, LANES)
    ff_ref[...] = tail
    kpe_ref[...] = jnp.where(lane < MLA_ROPE, _rope_slab(tail, cos_t, sin_t, lane), 0.0).astype(BF16)

    qn = _rms(proj(C_MQ, MLA_Q_RANK), gq_ref[...]).astype(BF16)
    qh = jnp.dot(qn, wq_ref[...], preferred_element_type=F32)
    q_scale = (MLA_NOPE + MLA_ROPE) ** -0.5 * LOG2E
    for hd in range(MLA_HEADS):
        c0 = hd * MLA_DQ
        mq_ref[:, c0:c0 + LANES] = (qh[:, c0:c0 + LANES] * q_scale).astype(BF16)
        pe = _rope_slab(qh[:, c0 + LANES:c0 + 2 * LANES], cos_t, sin_t, lane)
        mq_ref[:, c0 + LANES:c0 + 2 * LANES] = jnp.where(lane < MLA_ROPE, pe * q_scale, 0.0).astype(BF16)

    kvn = _rms(proj(C_MKV, MLA_KV_RANK), gkv_ref[...]).astype(BF16)
    kvh = jnp.dot(kvn, wkv_ref[...], preferred_element_type=F32)
    for hd in range(MLA_HEADS):
        c0 = hd * (MLA_NOPE + MLA_V)
        mk_ref[:, hd * MLA_NOPE:(hd + 1) * MLA_NOPE] = kvh[:, c0:c0 + MLA_NOPE].astype(BF16)
        mv_ref[:, hd * MLA_V:(hd + 1) * MLA_V] = kvh[:, c0 + MLA_NOPE:c0 + MLA_NOPE + MLA_V].astype(BF16)

    rq = proj(C_RQ, 256)
    rk = proj(C_RK, 256)
    for s in range(2):
        sl = slice(s * LANES, (s + 1) * LANES)
        q2 = _rope_slab(rq[:, sl], cos_t, sin_t, lane)
        k2 = _rope_slab(rk[:, sl], cos_t, sin_t, lane) * (RET_DK ** -0.5)
        for half in range(2):
            mine = (lane < RET_DK) if half == 0 else (lane >= RET_DK)
            hs = slice((2 * s + half) * LANES, (2 * s + half + 1) * LANES)
            rq_ref[:, hs] = jnp.where(mine, q2, 0.0).astype(BF16)
            rk_ref[:, hs] = jnp.where(mine, k2, 0.0).astype(BF16)
    rv_ref[...] = proj(C_RV, 512).astype(BF16)
    rg_ref[...] = proj(C_RG, 512).astype(BF16)
    fq_ref[...] = (proj(C_FQ, 512) * (FOX_DH ** -0.5 * LOG2E)).astype(BF16)
    fk_ref[...] = proj(C_FK, 512).astype(BF16)
    fv_ref[...] = proj(C_FV, 512).astype(BF16)


def _proj(x2, x_tile0, S, g, sh, sc, w_all, cos_t, sin_t, gq, wq, gkv, wkv):
    D = x2.shape[1]
    T = cos_t.shape[0]
    tm = _pick(S, PROJ_TILE)
    per_b = S // tm
    row = lambda i: (i, 0)
    xrow = lambda i: (i + x_tile0 * _pick(S, TOKEN_TILE) // tm, 0)
    const = lambda i: (0, 0)
    batch = lambda i: (i // per_b, 0, 0)
    widths = [512, 512, 512, 512, 512, 512, 512, MLA_HEADS * MLA_DQ, MLA_HEADS * MLA_NOPE, LANES,
              MLA_HEADS * MLA_V]
    out_shape = [jax.ShapeDtypeStruct((T, w), BF16) for w in widths]
    out_shape.append(jax.ShapeDtypeStruct((T, LANES), F32))
    out_specs = [pl.BlockSpec((tm, w), row) for w in widths] + [pl.BlockSpec((tm, LANES), row)]
    return pl.pallas_call(
        _proj_kernel,
        grid=(T // tm,),
        in_specs=[pl.BlockSpec((tm, D), xrow),
                  pl.BlockSpec((1, D), const),
                  pl.BlockSpec((None, 1, D), batch),
                  pl.BlockSpec((None, 1, D), batch),
                  pl.BlockSpec((D, D_IN_PAD), const),
                  pl.BlockSpec((tm, LANES), row),
                  pl.BlockSpec((tm, LANES), row),
                  pl.BlockSpec((1, MLA_Q_RANK), const),
                  pl.BlockSpec((MLA_Q_RANK, MLA_HEADS * MLA_DQ), const),
                  pl.BlockSpec((1, MLA_KV_RANK), const),
                  pl.BlockSpec((MLA_KV_RANK, MLA_HEADS * (MLA_NOPE + MLA_V)), const)],
        out_specs=out_specs,
        out_shape=out_shape,
        compiler_params=_cparams(("arbitrary",)),
        name="proj",
    )(x2, g, sh, sc, w_all, cos_t, sin_t, gq, wq, gkv, wkv)


def _split3(x):
    a = x.astype(BF16)
    r = x - a.astype(F32)
    b = r.astype(BF16)
    c = (r - b.astype(F32)).astype(BF16)
    return a, b, c


def _fox_gate_kernel(ff_ref, fb_ref, tri_ref, kb_ref):
    tm = tri_ref.shape[0]
    lane = lax.broadcasted_iota(I32, (tm, LANES), 1)
    tri = tri_ref[...]
    carry = jnp.zeros((1, LANES), F32)
    for t in range(ff_ref.shape[0] // tm):
        rows = slice(t * tm, (t + 1) * tm)
        z = ff_ref[rows, :] + fb_ref[...]
        ls = -(jnp.maximum(-z, 0.0) + jnp.log1p(jnp.exp(-jnp.abs(z))))
        ls = jnp.where((lane >= FF_LANE) & (lane < FF_LANE + FOX_HEADS), ls, 0.0)
        f = carry
        for part in _split3(ls):
            f = f + jnp.dot(tri, part, preferred_element_type=F32)
        carry = f[tm - 1:tm, :]
        hi, mid, lo = [part.astype(F32) for part in _split3(f * LOG2E)]
        for hd in range(FOX_HEADS):
            src = FF_LANE + hd
            slab = jnp.where(lane == 0, pltpu.roll(hi, (LANES - src) % LANES, axis=1),
                   jnp.where(lane == 1, pltpu.roll(mid, (LANES + 1 - src) % LANES, axis=1),
                   jnp.where(lane == 2, pltpu.roll(lo, (LANES + 2 - src) % LANES, axis=1), 0.0)))
            kb_ref[rows, hd * LANES:(hd + 1) * LANES] = slab.astype(BF16)


def _fox_gate(ffp, fb, S):
    T = ffp.shape[0]
    tm = _pick(S, TOKEN_TILE)
    idx = jnp.arange(tm)
    tri = (idx[None, :] <= idx[:, None]).astype(BF16)
    fbv = jnp.zeros((1, LANES), F32).at[0, FF_LANE:FF_LANE + FOX_HEADS].set(fb)
    return pl.pallas_call(
        _fox_gate_kernel,
        grid=(T // S,),
        in_specs=[pl.BlockSpec((S, LANES), lambda b: (b, 0)),
                  pl.BlockSpec((1, LANES), lambda b: (0, 0)),
                  pl.BlockSpec((tm, tm), lambda b: (0, 0))],
        out_specs=pl.BlockSpec((S, FOX_HEADS * LANES), lambda b: (b, 0)),
        out_shape=jax.ShapeDtypeStruct((T, FOX_HEADS * LANES), BF16),
        compiler_params=_cparams(("arbitrary",)),
        name="fox_gate",
    )(ffp, fbv, tri)


def _retention_kernel(dchunk_ref, q_ref, k_ref, v_ref, g_ref, dmask_ref, din_ref, dout_ref,
                      o_ref, state_ref, *, n_chunks):
    @pl.when(pl.program_id(1) == 0)
    def _():
        state_ref[...] = jnp.zeros_like(state_ref)

    C = RET_CHUNK
    for ci in range(n_chunks):
        rows = slice(ci * C, (ci + 1) * C)
        for hd in range(RET_HEADS):
            slab = slice(hd * LANES, (hd + 1) * LANES)
            q = q_ref[rows, slab]
            k = k_ref[rows, slab]
            vcols = slice(hd * RET_DV, (hd + 1) * RET_DV)
            v = v_ref[rows, vcols]
            state = state_ref[hd]
            scores = lax.dot_general(q, k, (((1,), (1,)), ((), ())),
                                     preferred_element_type=F32) * dmask_ref[hd]
            inner = jnp.dot(scores.astype(BF16), v, preferred_element_type=F32)
            cross = jnp.dot(q, state.astype(BF16), preferred_element_type=F32) * din_ref[hd]
            o = inner + cross
            vd = (v.astype(F32) * dout_ref[hd]).astype(BF16)
            kv = lax.dot_general(k, vd, (((0,), (0,)), ((), ())), preferred_element_type=F32)
            state_ref[hd] = state * dchunk_ref[hd] + kv
            mu = jnp.mean(o, axis=-1, keepdims=True)
            d = o - mu
            var = jnp.mean(d * d, axis=-1, keepdims=True)
            on = d * lax.rsqrt(var + EPS)
            g = g_ref[rows, vcols].astype(F32)
            o_ref[rows, vcols] = (g * jax.nn.sigmoid(g) * on).astype(BF16)


def _retention(rq, rk, rv, rg, B, S):
    H, C = RET_HEADS, RET_CHUNK
    tr = _pick(S, RET_CHUNKS_PER_STEP * C)
    n_chunks = tr // C
    log_gamma = jnp.log1p(-jnp.exp2(-5.0 - jnp.arange(H, dtype=F32)))
    idx = jnp.arange(C, dtype=F32)
    rel = idx[:, None] - idx[None, :]
    dmask = jnp.where(rel >= 0, jnp.exp(log_gamma[:, None, None] * jnp.maximum(rel, 0.0)), 0.0)
    decay_in = jnp.exp(log_gamma[:, None] * (idx + 1.0))
    decay_out = jnp.exp(log_gamma[:, None] * (C - 1.0 - idx))
    decay_chunk = jnp.exp(log_gamma * C)
    din = jnp.broadcast_to(decay_in[:, :, None], (H, C, RET_DV))
    dout = jnp.broadcast_to(decay_out[:, :, None], (H, C, RET_DV))
    tok = lambda b, i: (b, i, 0)
    const3 = lambda b, i: (0, 0, 0)
    return pl.pallas_call(
        functools.partial(_retention_kernel, n_chunks=n_chunks),
        grid=(B, S // tr),
        in_specs=[pl.BlockSpec(memory_space=pltpu.SMEM),
                  pl.BlockSpec((None, tr, H * LANES), tok),
                  pl.BlockSpec((None, tr, H * LANES), tok),
                  pl.BlockSpec((None, tr, H * RET_DV), tok),
                  pl.BlockSpec((None, tr, H * RET_DV), tok),
                  pl.BlockSpec((H, C, C), const3),
                  pl.BlockSpec((H, C, RET_DV), const3),
                  pl.BlockSpec((H, C, RET_DV), const3)],
        out_specs=pl.BlockSpec((None, tr, H * RET_DV), tok),
        out_shape=jax.ShapeDtypeStruct((B, S, H * RET_DV), BF16),
        scratch_shapes=[pltpu.VMEM((H, LANES, RET_DV), F32)],
        compiler_params=_cparams(("arbitrary", "arbitrary")),
        name="retention",
    )(decay_chunk, rq.reshape(B, S, -1), rk.reshape(B, S, -1), rv.reshape(B, S, -1),
      rg.reshape(B, S, -1), dmask, din, dout)


def _flash_kernel(q_ref, k_ref, ke_ref, v_ref, o_ref, m_ref, l_ref, acc_ref, s_ref, *, tq, q_bias_cols):
    S = q_ref.shape[0]
    nq = S // tq
    tk = tq
    m_ref[...] = jnp.full(m_ref.shape, NEG_BIG, F32)
    l_ref[...] = jnp.zeros_like(l_ref)
    acc_ref[...] = jnp.zeros_like(acc_ref)
    lane = lax.broadcasted_iota(I32, (tq, LANES), 1)

    half = tk // 2
    tri = (lax.broadcasted_iota(I32, (half, half), 0) <= lax.broadcasted_iota(I32, (half, half), 1))
    nt = lambda a, b: lax.dot_general(a, b, (((1,), (1,)), ((), ())), preferred_element_type=F32)

    def issue(n):
        i, j = pairs[n]
        q = q_ref[i * tq:(i + 1) * tq, :]
        if q_bias_cols:
            q = jnp.concatenate([q, jnp.where(lane < q_bias_cols, -1.0, 0.0).astype(BF16)], axis=1)
        kj = jnp.concatenate([k_ref[j * tk:(j + 1) * tk, :], ke_ref[j * tk:(j + 1) * tk, :]], axis=1)
        if j < i:
            s_ref[n % 2] = nt(kj, q)
        else:
            s_ref[n % 2, 0:half, :] = nt(kj[0:half], q)
            s_ref[n % 2, half:, half:] = nt(kj[half:], q[half:])

    def update(s, i, c0, c1, r0, r1):
        m_prev = m_ref[i, :, c0:c1]
        m_new = jnp.maximum(m_prev, jnp.max(s, axis=0, keepdims=True))
        alpha = jnp.exp2(m_prev - m_new)
        p = jnp.exp2(s - m_new)
        l_ref[i, :, c0:c1] = alpha * l_ref[i, :, c0:c1] + jnp.sum(p, axis=0, keepdims=True)
        acc_ref[i, :, c0:c1] = alpha * acc_ref[i, :, c0:c1] + lax.dot_general(
            v_ref[r0:r1, :], p.astype(BF16), (((0,), (0,)), ((), ())),
            preferred_element_type=F32)
        m_ref[i, :, c0:c1] = m_new

    def consume(n):
        i, j = pairs[n]
        slot = n % 2
        if j < i:
            update(s_ref[slot], i, 0, tq, j * tk, (j + 1) * tk)
        else:
            left = jnp.where(tri, s_ref[slot, 0:half, 0:half], NEG_BIG)
            right = jnp.concatenate([s_ref[slot, 0:half, half:],
                                     jnp.where(tri, s_ref[slot, half:, half:], NEG_BIG)], axis=0)
            update(left, i, 0, half, j * tk, j * tk + half)
            update(right, i, half, tq, j * tk, (j + 1) * tk)
            o_ref[i * tq:(i + 1) * tq, :] = (acc_ref[i] / l_ref[i]).T.astype(o_ref.dtype)

    pairs = [(i, j) for i in range(nq) for j in range(i + 1)]
    issue(0)
    for n in range(len(pairs)):
        if n + 1 < len(pairs):
            issue(n + 1)
        consume(n)


def _flash(q, k, ke, v, B, S, H, dq, ke_per_head, q_bias_cols, name):
    tq = _pick(S, ATTN_TILE)
    nq = S // tq
    ke_map = (lambda b, h: (b, 0, h)) if ke_per_head else (lambda b, h: (b, 0, 0))
    return pl.pallas_call(
        functools.partial(_flash_kernel, tq=tq, q_bias_cols=q_bias_cols),
        grid=(B, H),
        in_specs=[pl.BlockSpec((None, S, dq), lambda b, h: (b, 0, h)),
                  pl.BlockSpec((None, S, LANES), lambda b, h: (b, 0, h)),
                  pl.BlockSpec((None, S, LANES), ke_map),
                  pl.BlockSpec((None, S, LANES), lambda b, h: (b, 0, h))],
        out_specs=pl.BlockSpec((None, S, LANES), lambda b, h: (b, 0, h)),
        out_shape=jax.ShapeDtypeStruct((B, S, H * LANES), BF16),
        scratch_shapes=[pltpu.VMEM((nq, 1, tq), F32), pltpu.VMEM((nq, 1, tq), F32),
                        pltpu.VMEM((nq, LANES, tq), F32), pltpu.VMEM((2, tq, tq), F32)],
        compiler_params=_cparams(("arbitrary", "arbitrary")),
        name=name,
    )(q, k, ke, v)


def _merge_kernel(x_ref, ya_ref, yb_ref, yc_ref, g1_ref, sh1_ref, sc1_ref, gt1_ref,
                  g2_ref, sh2_ref, sc2_ref, gw_ref, gb_ref, bw_ref, ow_ref,
                  rwh_ref, rwl_ref, rb_ref, tri_ref,
                  x1_ref, h2_ref, mi_ref, mf_ref, cnt_ref, h_ref, mg_ref, carry_ref):
    tm, D = x_ref.shape
    cn = MERGE_COL_CHUNK

    @pl.when(pl.program_id(0) == 0)
    def _():
        carry_ref[...] = jnp.zeros_like(carry_ref)

    h_ref[...] = _norm_mod(x_ref[...], g1_ref[...], sh1_ref[...], sc1_ref[...]).astype(BF16)
    for n in range(D // cn):
        cols = slice(n * cn, (n + 1) * cn)
        merged = None
        for i, y_ref in enumerate((ya_ref, yb_ref, yc_ref)):
            gcols = slice(i * D + n * cn, i * D + (n + 1) * cn)
            gate = jax.nn.sigmoid(jnp.dot(h_ref[...], gw_ref[:, gcols], preferred_element_type=F32)
                                  + gb_ref[:, gcols])
            br = jnp.dot(y_ref[...], bw_ref[i, :, cols], preferred_element_type=F32)
            merged = gate * br if merged is None else merged + gate * br
        mg_ref[:, cols] = merged.astype(BF16)
    for n in range(D // cn):
        cols = slice(n * cn, (n + 1) * cn)
        mix = jnp.dot(mg_ref[...], ow_ref[:, cols], preferred_element_type=F32)
        x1_ref[:, cols] = x_ref[:, cols] + gt1_ref[:, cols] * mix
    h2 = _norm_mod(x1_ref[...], g2_ref[...], sh2_ref[...], sc2_ref[...])
    h2_ref[...] = _pack_bf16_pair(h2[:, :D // 2], h2[:, D // 2:])

    hh = h2.astype(BF16)
    hl = (h2 - hh.astype(F32)).astype(BF16)
    nt = lambda a, b: lax.dot_general(a, b, (((1,), (1,)), ((), ())), preferred_element_type=F32)
    lt = nt(rwh_ref[...], hh) + nt(rwh_ref[...], hl) + nt(rwl_ref[...], hh) + rb_ref[...]
    row8 = lax.broadcasted_iota(I32, (EXP_PER_GROUP, tm), 0)
    gl = jnp.where(row8 < N_GROUPS, lt[N_EXPERTS:N_EXPERTS + EXP_PER_GROUP, :], -jnp.inf)
    gmax = jnp.max(gl, axis=0, keepdims=True)
    g_idx = jnp.min(jnp.where(gl == gmax, row8, EXP_PER_GROUP), axis=0, keepdims=True)
    g_w = 1.0 / jnp.sum(jnp.exp(gl - gmax), axis=0, keepdims=True)
    el = lt[(N_GROUPS - 1) * EXP_PER_GROUP:N_EXPERTS, :]
    for g in range(N_GROUPS - 2, -1, -1):
        el = jnp.where(g_idx == g, lt[g * EXP_PER_GROUP:(g + 1) * EXP_PER_GROUP, :], el)
    e1 = jnp.max(el, axis=0, keepdims=True)
    i1 = jnp.min(jnp.where(el == e1, row8, EXP_PER_GROUP), axis=0, keepdims=True)
    el2 = jnp.where(row8 == i1, -jnp.inf, el)
    e2 = jnp.max(el2, axis=0, keepdims=True)
    i2 = jnp.min(jnp.where(el2 == e2, row8, EXP_PER_GROUP), axis=0, keepdims=True)
    r = jnp.exp(e2 - e1)
    w1 = g_w / (1.0 + r)
    w2 = g_w * r / (1.0 + r)
    eid1 = g_idx * EXP_PER_GROUP + i1
    eid2 = g_idx * EXP_PER_GROUP + i2

    rowe = lax.broadcasted_iota(I32, (N_EXPERTS, tm), 0)
    hit1 = rowe == eid1
    hit2 = rowe == eid2
    onehot = jnp.where(hit1 | hit2, 1.0, 0.0)
    before = (jnp.dot(onehot.astype(BF16), tri_ref[...], preferred_element_type=F32)
              + jnp.concatenate([carry_ref[...]] * (tm // LANES), axis=1))
    rank1 = jnp.sum(jnp.where(hit1, before, 0.0), axis=0, keepdims=True)
    rank2 = jnp.sum(jnp.where(hit2, before, 0.0), axis=0, keepdims=True)
    carry_ref[...] = carry_ref[...] + jnp.sum(onehot, axis=1, keepdims=True)
    cnt_ref[...] = carry_ref[...]

    mi_ref[...] = jnp.where(row8 == 0, eid1,
                  jnp.where(row8 == 1, eid2,
                  jnp.where(row8 == 2, rank1.astype(I32),
                  jnp.where(row8 == 3, rank2.astype(I32), 0))))
    mf_ref[...] = jnp.where(row8 == 0, w1, jnp.where(row8 == 1, w2, 0.0))


def _merge(x2, x_tile0, S, ya, yb, yc, g1, sh1, sc1, gt1, g2, sh2, sc2, gw, gb, bw, ow, rwh, rwl, rb):
    D = x2.shape[1]
    T = ya.shape[0]
    tm = _pick(S, MERGE_TILE)
    per_b = S // tm
    row = lambda i: (i, 0)
    xrow = lambda i: (i + x_tile0 * _pick(S, TOKEN_TILE) // tm, 0)
    col = lambda i: (0, i)
    const = lambda i: (0, 0)
    batch = lambda i: (i // per_b, 0, 0)
    idx = jnp.arange(tm)
    tri = (idx[:, None] < idx[None, :]).astype(BF16)
    vecb = pl.BlockSpec((None, 1, D), batch)
    return pl.pallas_call(
        _merge_kernel,
        grid=(T // tm,),
        in_specs=[pl.BlockSpec((tm, D), xrow),
                  pl.BlockSpec((tm, BRANCH_W), row),
                  pl.BlockSpec((tm, BRANCH_W), row),
                  pl.BlockSpec((tm, BRANCH_W), row),
                  pl.BlockSpec((1, D), const), vecb, vecb, vecb,
                  pl.BlockSpec((1, D), const), vecb, vecb,
                  pl.BlockSpec((D, N_BRANCH * D), const),
                  pl.BlockSpec((1, N_BRANCH * D), const),
                  pl.BlockSpec((N_BRANCH, BRANCH_W, D), lambda i: (0, 0, 0)),
                  pl.BlockSpec((D, D), const),
                  pl.BlockSpec((LANES, D), const),
                  pl.BlockSpec((LANES, D), const),
                  pl.BlockSpec((LANES, 1), const),
                  pl.BlockSpec((tm, tm), const)],
        out_specs=[pl.BlockSpec((tm, D), row), pl.BlockSpec((tm, D // 2), row),
                   pl.BlockSpec((8, tm), col), pl.BlockSpec((8, tm), col),
                   pl.BlockSpec((N_EXPERTS, LANES), const)],
        out_shape=[jax.ShapeDtypeStruct((T, D), F32), jax.ShapeDtypeStruct((T, D // 2), U32),
                   jax.ShapeDtypeStruct((8, T), I32), jax.ShapeDtypeStruct((8, T), F32),
                   jax.ShapeDtypeStruct((N_EXPERTS, LANES), F32)],
        scratch_shapes=[pltpu.VMEM((tm, D), BF16), pltpu.VMEM((tm, D), BF16),
                        pltpu.VMEM((N_EXPERTS, LANES), F32)],
        compiler_params=_cparams(("arbitrary",)),
        name="merge",
    )(x2, ya, yb, yc, g1, sh1, sc1, gt1, g2, sh2, sc2, gw, gb, bw, ow, rwh, rwl, rb, tri)


def _expert_kernel(blk_e_ref, nvalid_ref, xs_ref, w1_ref, w3_ref, w2_ref, ys_ref, w1b_ref, w3b_ref, w2b_ref):
    i = pl.program_id(0)
    valid = i < nvalid_ref[0]
    new_expert = jnp.logical_or(i == 0, blk_e_ref[i] != blk_e_ref[jnp.maximum(i - 1, 0)])

    @pl.when(jnp.logical_and(valid, new_expert))
    def _():
        w1b_ref[...] = w1_ref[...].astype(BF16)
        w3b_ref[...] = w3_ref[...].astype(BF16)
        w2b_ref[...] = w2_ref[...].astype(BF16)

    @pl.when(valid)
    def _():
        half = xs_ref.shape[1]
        x_lo, x_hi = _unpack_bf16_pair(xs_ref[...])
        x_lo = x_lo.astype(BF16)
        x_hi = x_hi.astype(BF16)

        def up(w_ref):
            return (jnp.dot(x_lo, w_ref[:half, :], preferred_element_type=F32)
                    + jnp.dot(x_hi, w_ref[half:, :], preferred_element_type=F32))

        a = up(w1b_ref)
        b = up(w3b_ref)
        hid = (a * jax.nn.sigmoid(a) * b).astype(BF16)
        y = jnp.dot(hid, w2b_ref[...], preferred_element_type=F32)
        ys_ref[...] = _pack_bf16_pair(y[:, :half], y[:, half:])

    @pl.when(jnp.logical_not(valid))
    def _():
        ys_ref[...] = jnp.zeros_like(ys_ref)


def _experts(xs, blk_e, nvalid, w1, w3, w2, layer, tb):
    P, half = xs.shape
    D = 2 * half
    n_blocks = P // tb
    rows = lambda i, be, nv: (jnp.minimum(i, nv[0] - 1), 0)
    grid_spec = pltpu.PrefetchScalarGridSpec(
        num_scalar_prefetch=2,
        grid=(n_blocks,),
        in_specs=[pl.BlockSpec((tb, half), rows),
                  pl.BlockSpec((None, None, D, D_EXPERT), lambda i, be, nv: (layer, be[i], 0, 0)),
                  pl.BlockSpec((None, None, D, D_EXPERT), lambda i, be, nv: (layer, be[i], 0, 0)),
                  pl.BlockSpec((None, None, D_EXPERT, D), lambda i, be, nv: (layer, be[i], 0, 0))],
        out_specs=pl.BlockSpec((tb, half), lambda i, be, nv: (i, 0)),
        scratch_shapes=[pltpu.VMEM((D, D_EXPERT), BF16), pltpu.VMEM((D, D_EXPERT), BF16),
                        pltpu.VMEM((D_EXPERT, D), BF16)],
    )
    return pl.pallas_call(
        _expert_kernel,
        grid_spec=grid_spec,
        out_shape=jax.ShapeDtypeStruct((P, half), U32),
        compiler_params=_cparams(("arbitrary",)),
        name="moe_experts",
    )(blk_e, nvalid, xs, w1, w3, w2)


def _sc_gather(data, idx):
    M = idx.shape[0]
    D = data.shape[1]
    W = SC_GATHER_WINDOW
    assert M % W == 0, (M, W)
    mesh = plsc.VectorSubcoreMesh(core_axis_name="core", subcore_axis_name="subcore")
    n_workers = mesh.num_cores * mesh.num_subcores
    assert M % (W * n_workers) == 0, (M, W, n_workers)

    @functools.partial(pl.kernel, out_type=jax.ShapeDtypeStruct((M, D), data.dtype), mesh=mesh,
                       scratch_types=[pltpu.VMEM((W,), I32), pltpu.VMEM((W, D), data.dtype)])
    def gather_kernel(x_hbm, i_hbm, o_hbm, i_vmem, buf):
        worker = lax.axis_index("core") * mesh.num_subcores + lax.axis_index("subcore")

        @pl.loop(0, M // (W * n_workers))
        def _(t):
            start = (t * n_workers + worker) * W
            pltpu.sync_copy(i_hbm.at[pl.ds(start, W)], i_vmem)
            pltpu.sync_copy(x_hbm.at[i_vmem], buf)
            pltpu.sync_copy(buf, o_hbm.at[pl.ds(start, W)])

    return gather_kernel(data, idx)


def _sc_dispatch(h2, dest, fill_idx, P):
    T, D = h2.shape
    W = SC_GATHER_WINDOW
    n_fill = fill_idx.shape[0]
    mesh = plsc.VectorSubcoreMesh(core_axis_name="core", subcore_axis_name="subcore")
    n_workers = mesh.num_cores * mesh.num_subcores
    assert T % (W * n_workers) == 0 and n_fill % (W * n_workers) == 0, (T, n_fill, W, n_workers)
    zeros = jnp.zeros((W, D), h2.dtype)

    @functools.partial(pl.kernel, out_type=jax.ShapeDtypeStruct((P, D), h2.dtype), mesh=mesh,
                       scratch_types=[pltpu.VMEM((W,), I32), pltpu.VMEM((W, D), h2.dtype)])
    def dispatch_kernel(h_hbm, d_hbm, f_hbm, z_hbm, o_hbm, i_vmem, buf):
        worker = lax.axis_index("core") * mesh.num_subcores + lax.axis_index("subcore")

        @pl.loop(0, T // (W * n_workers))
        def _(t):
            start = (t * n_workers + worker) * W
            pltpu.sync_copy(h_hbm.at[pl.ds(start, W)], buf)
            for c in range(2):
                pltpu.sync_copy(d_hbm.at[pl.ds(c * T + start, W)], i_vmem)
                pltpu.sync_copy(buf, o_hbm.at[i_vmem])

        pltpu.sync_copy(z_hbm, buf)

        @pl.loop(0, n_fill // (W * n_workers))
        def _(t):
            start = (t * n_workers + worker) * W
            pltpu.sync_copy(f_hbm.at[pl.ds(start, W)], i_vmem)
            pltpu.sync_copy(buf, o_hbm.at[i_vmem])

    return dispatch_kernel(h2, dest, fill_idx, zeros)


def _combine_kernel(x_ref, g0_ref, g1_ref, mf_ref, gt_ref, fg_ref, *rest, final):
    o_ref = rest[-1]
    mf = mf_ref[...]
    lo0, hi0 = _unpack_bf16_pair(g0_ref[...])
    lo1, hi1 = _unpack_bf16_pair(g1_ref[...])
    w0 = mf[:, 0:1]
    w1 = mf[:, 1:2]
    ffn = jnp.concatenate([lo0 * w0 + lo1 * w1, hi0 * w0 + hi1 * w1], axis=1)
    out = x_ref[...] + gt_ref[...] * ffn
    if final:
        out = _rms(out, fg_ref[...])
    o_ref[...] = out


def _combine(x1, S, mf, gt2, final_g, g, final, out_rows=None, out_tile0=0, out_prev=None):
    T, D = x1.shape
    tm = _pick(S, TOKEN_TILE)
    per_b = S // tm
    nt = T // tm
    in_specs = [pl.BlockSpec((tm, D), lambda i: (i, 0)),
                pl.BlockSpec((tm, D // 2), lambda i: (i, 0)),
                pl.BlockSpec((tm, D // 2), lambda i: (i + nt, 0)),
                pl.BlockSpec((tm, 2), lambda i: (i, 0)),
                pl.BlockSpec((None, 1, D), lambda i: (i // per_b, 0, 0)),
                pl.BlockSpec((1, D), lambda i: (0, 0))]
    args = [x1, g, g, mf, gt2, final_g]
    aliases = {}
    if out_prev is not None:
        in_specs.append(pl.BlockSpec(memory_space=pl.ANY))
        args.append(out_prev)
        aliases = {len(args) - 1: 0}
    return pl.pallas_call(
        functools.partial(_combine_kernel, final=final),
        grid=(nt,),
        in_specs=in_specs,
        out_specs=pl.BlockSpec((tm, D), lambda i: (i + out_tile0, 0)),
        out_shape=jax.ShapeDtypeStruct((T if out_rows is None else out_rows, D), F32),
        input_output_aliases=aliases,
        compiler_params=_cparams(("arbitrary",)),
        name="moe_combine",
    )(*args)


def _prep_w_in(w):
    offs = [0]
    for s in (256, 256, 512, 512, 512, 512, 512, FOX_HEADS, MLA_Q_RANK, MLA_KV_RANK, MLA_ROPE):
        offs.append(offs[-1] + s)
    rq, rk, rv, rg, fq, fk, fv, ff, mq, mkv, mkr = [w[:, offs[i]:offs[i + 1]] for i in range(11)]
    pad = jnp.zeros((w.shape[0], LANES - MLA_ROPE - FOX_HEADS), w.dtype)
    return jnp.concatenate([rq, rk, rv, rg, fq, fk, fv, mq, mkv, mkr, ff, pad], axis=1).astype(BF16)


def _prep_wq_up(w):
    r = w.reshape(MLA_Q_RANK, MLA_HEADS, MLA_NOPE + MLA_ROPE)
    r = jnp.pad(r, ((0, 0), (0, 0), (0, MLA_DQ - MLA_NOPE - MLA_ROPE)))
    return r.reshape(MLA_Q_RANK, MLA_HEADS * MLA_DQ).astype(BF16)


def _prep_router(w_grp, b_grp, w_exp, b_exp):
    D = w_grp.shape[0]
    pad = LANES - N_EXPERTS - N_GROUPS
    rwt = jnp.concatenate([w_exp, w_grp, jnp.zeros((D, pad), F32)], axis=1).astype(F32).T
    rwh = rwt.astype(BF16)
    rwl = (rwt - rwh.astype(F32)).astype(BF16)
    rb = jnp.concatenate([b_exp, b_grp, jnp.zeros((pad,), F32)]).astype(F32).reshape(LANES, 1)
    return rwh, rwl, rb


def kernel(x, c, positions, ada_w, ada_b, norm1_g, norm2_g, w_in, fox_fb, mla_q_norm_g, mla_wq_up, mla_kv_norm_g, mla_wkv_up, gate_w, gate_b, branch_w, out_w, router_grp_w, router_grp_b, router_exp_w, router_exp_b, exp_w1, exp_w3, exp_w2, final_g):
    B, S, D = x.shape
    L = ada_w.shape[0]
    T = B * S
    n_str = N_STREAMS if B % N_STREAMS == 0 else 1
    Bs = B // n_str
    Ts = Bs * S
    As = 2 * Ts
    tb = _pick(As, EXPERT_BLOCK)
    n_blocks = As // tb + N_EXPERTS
    P = n_blocks * tb
    tiles_per_stream = Ts // _pick(S, TOKEN_TILE)

    mod = _adaln(c, ada_w, ada_b)
    cos_t, sin_t = _rope_tables(positions)
    cos_s = [cos_t[h * Ts:(h + 1) * Ts] for h in range(n_str)]
    sin_s = [sin_t[h * Ts:(h + 1) * Ts] for h in range(n_str)]
    x_full = x.reshape(T, D)
    xs2 = [x_full] * n_str
    x_tile0 = [h * tiles_per_stream for h in range(n_str)]
    final_g2 = final_g.reshape(1, D)
    r3 = lambda a: a.reshape(Bs, S, -1)
    out = None

    for l in range(L):
        mods = [[mod[l, h * Bs:(h + 1) * Bs, i * D:(i + 1) * D].reshape(Bs, 1, D) for i in range(6)]
                for h in range(n_str)]
        g1 = norm1_g[l].reshape(1, D)
        g2 = norm2_g[l].reshape(1, D)
        w_all = _prep_w_in(w_in[l])
        wq = _prep_wq_up(mla_wq_up[l])
        wkv = mla_wkv_up[l].astype(BF16)
        gq = mla_q_norm_g[l].reshape(1, -1)
        gkv = mla_kv_norm_g[l].reshape(1, -1)
        rwh, rwl, rb = _prep_router(router_grp_w[l], router_grp_b[l], router_exp_w[l], router_exp_b[l])
        gw = gate_w[l].astype(BF16)
        gb = gate_b[l].reshape(1, -1)
        bw = branch_w[l].astype(BF16)
        ow = out_w[l].astype(BF16)
        last = l == L - 1

        merged = []
        for h in range(n_str):
            sh1, sc1, gt1, sh2, sc2, gt2 = mods[h]
            (rq, rk, rv, rg, fq, fk, fv, mq, mk, kpe, mv, ffp) = _proj(
                xs2[h], x_tile0[h], S, g1, sh1, sc1, w_all, cos_s[h], sin_s[h], gq, wq, gkv, wkv)
            ya = _retention(rq, rk, rv, rg, Bs, S).reshape(Ts, -1)
            kb = _fox_gate(ffp, fox_fb[l], S)
            yb = _flash(r3(fq), r3(fk), r3(kb), r3(fv), Bs, S, FOX_HEADS, FOX_DH, True, 3,
                        "flash_fox").reshape(Ts, -1)
            yc = _flash(r3(mq), r3(mk), r3(kpe), r3(mv), Bs, S, MLA_HEADS, MLA_DQ, False, 0,
                        "flash_mla").reshape(Ts, -1)
            merged.append(_merge(xs2[h], x_tile0[h], S, ya, yb, yc, g1, sh1, sc1, gt1, g2, sh2, sc2,
                                 gw, gb, bw, ow, rwh, rwl, rb))

        routed = []
        for h in range(n_str):
            x1, h2, mi, mf, cnt = merged[h]
            counts = cnt[:, 0].astype(I32)
            pcounts = (counts + tb - 1) // tb * tb
            pends = jnp.cumsum(pcounts)
            pstarts = pends - pcounts
            sel = mi[0:2, :, None] == jnp.arange(N_EXPERTS, dtype=I32)
            dest = (jnp.sum(jnp.where(sel, pstarts, 0), axis=-1) + mi[2:4]).reshape(As)
            blk_pos = jnp.arange(n_blocks, dtype=I32) * tb
            blk_e = jnp.minimum(jnp.sum((pends[None, :] <= blk_pos[:, None]).astype(I32), axis=1),
                                N_EXPERTS - 1)
            nvalid = (pends[-1:] // tb).astype(I32)
            fr = jnp.arange(tb, dtype=I32)[None, :]
            is_pad = (fr < (pcounts - counts)[:, None]).reshape(-1)
            pad_slot = ((pstarts + counts)[:, None] + fr).reshape(-1)
            tail_rank = jnp.cumsum(jnp.logical_not(is_pad).astype(I32)) - 1
            fill_idx = jnp.where(is_pad, pad_slot, pends[-1] + tail_rank)
            routed.append((dest, blk_e, nvalid, _sc_dispatch(h2, dest, fill_idx, P)))

        ys = [_experts(routed[h][3], routed[h][1], routed[h][2], exp_w1, exp_w3, exp_w2, l, tb)
              for h in range(n_str)]
        gathered = [_sc_gather(ys[h], routed[h][0]) for h in range(n_str)]
        for h in range(n_str):
            x1, _, _, mf, _ = merged[h]
            gt2 = mods[h][5]
            if last:
                out = _combine(x1, S, mf[0:2].T, gt2, final_g2, gathered[h], True,
                               out_rows=T, out_tile0=h * tiles_per_stream, out_prev=out)
            else:
                xs2[h] = _combine(x1, S, mf[0:2].T, gt2, final_g2, gathered[h], False)
        x_tile0 = [0] * n_str

    return out.reshape(B, S, D)
```

```python
import functools
import math

import jax
import jax.numpy as jnp
from jax import lax
from jax.experimental import pallas as pl
from jax.experimental.pallas import tpu as pltpu
from jax.experimental.pallas import tpu_sc as plsc

F32 = jnp.float32
BF16 = jnp.bfloat16
I32 = jnp.int32
U32 = jnp.uint32
HIGHEST = lax.Precision.HIGHEST

EPS = 1e-6
ROPE_THETA = 10000.0
RET_HEADS = 4
RET_DK = 64
RET_DV = 128
RET_CHUNK = 128
FOX_HEADS = 4
FOX_DH = 128
MLA_HEADS = 4
MLA_Q_RANK = 256
MLA_KV_RANK = 128
MLA_NOPE = 128
MLA_ROPE = 64
MLA_V = 128
MLA_DQ = 256
N_BRANCH = 3
BRANCH_W = 512
N_GROUPS = 4
EXP_PER_GROUP = 8
N_EXPERTS = N_GROUPS * EXP_PER_GROUP
D_EXPERT = 512

LANES = 128
V7X_VMEM_LIMIT = 56 * 1024 * 1024

C_RQ, C_RK, C_RV, C_RG = 0, 256, 512, 1024
C_FQ, C_FK, C_FV = 1536, 2048, 2560
C_MQ, C_MKV, C_TAIL = 3072, 3328, 3456
D_IN_PAD = 3584
FF_LANE = MLA_ROPE

NEG_BIG = -1e30

TOKEN_TILE = 512
ATTN_TILE = 1024
MERGE_TILE = 1024
PROJ_TILE = 1024
COMBINE_TILE = 1024
EXPERT_BLOCK = 512
RET_CHUNKS_PER_STEP = 32
ADALN_COL_TILE = 1536
ROPE_TABLE_TILE = 2048
SC_GATHER_WINDOW = 128
N_STREAMS = 1
MERGE_COL_CHUNK = 256
LOG2E = math.log2(math.e)


def _cparams(sem):
    return pltpu.CompilerParams(dimension_semantics=sem, vmem_limit_bytes=V7X_VMEM_LIMIT)


def _pick(n, pref):
    t = min(n, pref)
    assert n % t == 0, (n, t)
    return t


def _adaln_kernel(c_ref, w_ref, b_ref, o_ref):
    c = c_ref[...]
    ca = c * jax.nn.sigmoid(c)
    o_ref[...] = jnp.dot(ca, w_ref[...], preferred_element_type=F32, precision=HIGHEST) + b_ref[...]


def _adaln(c, ada_w, ada_b):
    L, D, N = ada_w.shape
    B = c.shape[0]
    tn = _pick(N, ADALN_COL_TILE)
    return pl.pallas_call(
        _adaln_kernel,
        grid=(L, N // tn),
        in_specs=[pl.BlockSpec((B, D), lambda l, j: (0, 0)),
                  pl.BlockSpec((None, D, tn), lambda l, j: (l, 0, j)),
                  pl.BlockSpec((None, 1, tn), lambda l, j: (l, 0, j))],
        out_specs=pl.BlockSpec((None, B, tn), lambda l, j: (l, 0, j)),
        out_shape=jax.ShapeDtypeStruct((L, B, N), F32),
        compiler_params=_cparams(("arbitrary", "arbitrary")),
        name="adaln",
    )(c, ada_w, ada_b.reshape(L, 1, N))


def _rope_table_kernel(pos_ref, inv_ref, sign_ref, cos_ref, sin_ref):
    ang = pos_ref[...].astype(F32) * inv_ref[...]
    cos_ref[...] = jnp.cos(ang)
    sin_ref[...] = jnp.sin(ang) * sign_ref[...]


def _rope_tables(positions):
    T = positions.size
    tm = _pick(T, ROPE_TABLE_TILE)
    half = MLA_ROPE // 2
    inv = ROPE_THETA ** (-jnp.arange(0, MLA_ROPE, 2, dtype=F32) / MLA_ROPE)
    inv_t = jnp.tile(inv, LANES // half).reshape(1, LANES)
    sign = jnp.where((jnp.arange(LANES) % MLA_ROPE) < half, -1.0, 1.0).astype(F32).reshape(1, LANES)
    return pl.pallas_call(
        _rope_table_kernel,
        grid=(T // tm,),
        in_specs=[pl.BlockSpec((tm, 1), lambda i: (i, 0)),
                  pl.BlockSpec((1, LANES), lambda i: (0, 0)),
                  pl.BlockSpec((1, LANES), lambda i: (0, 0))],
        out_specs=[pl.BlockSpec((tm, LANES), lambda i: (i, 0))] * 2,
        out_shape=[jax.ShapeDtypeStruct((T, LANES), F32)] * 2,
        compiler_params=_cparams(("arbitrary",)),
        name="rope_tables",
    )(positions.reshape(T, 1), inv_t, sign)


def _rope_slab(x, cos_t, sin_t, lane):
    nxt = pltpu.roll(x, LANES - 32, axis=1)
    prv = pltpu.roll(x, 32, axis=1)
    swapped = jnp.where((lane & 32) == 0, nxt, prv)
    return x * cos_t + swapped * sin_t


def _pack_bf16_pair(lo, hi):
    lo_bits = lax.shift_right_logical(lax.bitcast_convert_type(lo.astype(BF16).astype(F32), U32), jnp.uint32(16))
    hi_bits = lax.bitcast_convert_type(hi.astype(BF16).astype(F32), U32) & jnp.uint32(0xFFFF0000)
    return hi_bits | lo_bits


def _unpack_bf16_pair(w):
    lo = lax.bitcast_convert_type(lax.shift_left(w, jnp.uint32(16)), F32)
    hi = lax.bitcast_convert_type(w & jnp.uint32(0xFFFF0000), F32)
    return lo, hi


def _norm_mod(x, g, shift, scale):
    y = x * lax.rsqrt(jnp.mean(x * x, axis=-1, keepdims=True) + EPS)
    return (y * g) * (1.0 + scale) + shift


def _rms(x, g):
    return x * lax.rsqrt(jnp.mean(x * x, axis=-1, keepdims=True) + EPS) * g


def _proj_kernel(x_ref, g_ref, sh_ref, sc_ref, w_ref, cos_ref, sin_ref,
                 gq_ref, wq_ref, gkv_ref, wkv_ref,
                 rq_ref, rk_ref, rv_ref, rg_ref, fq_ref, fk_ref, fv_ref,
                 mq_ref, mk_ref, kpe_ref, mv_ref, ff_ref):
    h = _norm_mod(x_ref[...], g_ref[...], sh_ref[...], sc_ref[...]).astype(BF16)
    cos_t = cos_ref[...]
    sin_t = sin_ref[...]
    lane = lax.broadcasted_iota(I32, cos_t.shape, 1)

    def proj(c0, width):
        return jnp.dot(h, w_ref[:, c0:c0 + width], preferred_element_type=F32)

    tail = proj(C_TAIL, LANES)
    ff_ref[...] = tail
    kpe_ref[...] = jnp.where(lane < MLA_ROPE, _rope_slab(tail, cos_t, sin_t, lane), 0.0).astype(BF16)

    qn = _rms(proj(C_MQ, MLA_Q_RANK), gq_ref[...]).astype(BF16)
    qh = jnp.dot(qn, wq_ref[...], preferred_element_type=F32)
    q_scale = (MLA_NOPE + MLA_ROPE) ** -0.5 * LOG2E
    for hd in range(MLA_HEADS):
        c0 = hd * MLA_DQ
        mq_ref[:, c0:c0 + LANES] = (qh[:, c0:c0 + LANES] * q_scale).astype(BF16)
        pe = _rope_slab(qh[:, c0 + LANES:c0 + 2 * LANES], cos_t, sin_t, lane)
        mq_ref[:, c0 + LANES:c0 + 2 * LANES] = jnp.where(lane < MLA_ROPE, pe * q_scale, 0.0).astype(BF16)

    kvn = _rms(proj(C_MKV, MLA_KV_RANK), gkv_ref[...]).astype(BF16)
    kvh = jnp.dot(kvn, wkv_ref[...], preferred_element_type=F32)
    for hd in range(MLA_HEADS):
        c0 = hd * (MLA_NOPE + MLA_V)
        mk_ref[:, hd * MLA_NOPE:(hd + 1) * MLA_NOPE] = kvh[:, c0:c0 + MLA_NOPE].astype(BF16)
        mv_ref[:, hd * MLA_V:(hd + 1) * MLA_V] = kvh[:, c0 + MLA_NOPE:c0 + MLA_NOPE + MLA_V].astype(BF16)

    rq = proj(C_RQ, 256)
    rk = proj(C_RK, 256)
    for s in range(2):
        sl = slice(s * LANES, (s + 1) * LANES)
        q2 = _rope_slab(rq[:, sl], cos_t, sin_t, lane)
        k2 = _rope_slab(rk[:, sl], cos_t, sin_t, lane) * (RET_DK ** -0.5)
        for half in range(2):
            mine = (lane < RET_DK) if half == 0 else (lane >= RET_DK)
            hs = slice((2 * s + half) * LANES, (2 * s + half + 1) * LANES)
            rq_ref[:, hs] = jnp.where(mine, q2, 0.0).astype(BF16)
            rk_ref[:, hs] = jnp.where(mine, k2, 0.0).astype(BF16)
    rv_ref[...] = proj(C_RV, 512).astype(BF16)
    rg_ref[...] = proj(C_RG, 512).astype(BF16)
    fq_ref[...] = (proj(C_FQ, 512) * (FOX_DH ** -0.5 * LOG2E)).astype(BF16)
    fk_ref[...] = proj(C_FK, 512).astype(BF16)
    fv_ref[...] = proj(C_FV, 512).astype(BF16)


def _proj(x2, x_tile0, S, g, sh, sc, w_all, cos_t, sin_t, gq, wq, gkv, wkv):
    D = x2.shape[1]
    T = cos_t.shape[0]
    tm = _pick(S, PROJ_TILE)
    per_b = S // tm
    row = lambda i: (i, 0)
    xrow = lambda i: (i + x_tile0 * _pick(S, TOKEN_TILE) // tm, 0)
    const = lambda i: (0, 0)
    batch = lambda i: (i // per_b, 0, 0)
    widths = [512, 512, 512, 512, 512, 512, 512, MLA_HEADS * MLA_DQ, MLA_HEADS * MLA_NOPE, LANES,
              MLA_HEADS * MLA_V]
    out_shape = [jax.ShapeDtypeStruct((T, w), BF16) for w in widths]
    out_shape.append(jax.ShapeDtypeStruct((T, LANES), F32))
    out_specs = [pl.BlockSpec((tm, w), row) for w in widths] + [pl.BlockSpec((tm, LANES), row)]
    return pl.pallas_call(
        _proj_kernel,
        grid=(T // tm,),
        in_specs=[pl.BlockSpec((tm, D), xrow),
                  pl.BlockSpec((1, D), const),
                  pl.BlockSpec((None, 1, D), batch),
                  pl.BlockSpec((None, 1, D), batch),
                  pl.BlockSpec((D, D_IN_PAD), const),
                  pl.BlockSpec((tm, LANES), row),
                  pl.BlockSpec((tm, LANES), row),
                  pl.BlockSpec((1, MLA_Q_RANK), const),
                  pl.BlockSpec((MLA_Q_RANK, MLA_HEADS * MLA_DQ), const),
                  pl.BlockSpec((1, MLA_KV_RANK), const),
                  pl.BlockSpec((MLA_KV_RANK, MLA_HEADS * (MLA_NOPE + MLA_V)), const)],
        out_specs=out_specs,
        out_shape=out_shape,
        compiler_params=_cparams(("arbitrary",)),
        name="proj",
    )(x2, g, sh, sc, w_all, cos_t, sin_t, gq, wq, gkv, wkv)


def _split3(x):
    a = x.astype(BF16)
    r = x - a.astype(F32)
    b = r.astype(BF16)
    c = (r - b.astype(F32)).astype(BF16)
    return a, b, c


def _fox_gate_kernel(ff_ref, fb_ref, tri_ref, kb_ref):
    tm = tri_ref.shape[0]
    lane = lax.broadcasted_iota(I32, (tm, LANES), 1)
    tri = tri_ref[...]
    carry = jnp.zeros((1, LANES), F32)
    for t in range(ff_ref.shape[0] // tm):
        rows = slice(t * tm, (t + 1) * tm)
        z = ff_ref[rows, :] + fb_ref[...]
        ls = -(jnp.maximum(-z, 0.0) + jnp.log1p(jnp.exp(-jnp.abs(z))))
        ls = jnp.where((lane >= FF_LANE) & (lane < FF_LANE + FOX_HEADS), ls, 0.0)
        f = carry
        for part in _split3(ls):
            f = f + jnp.dot(tri, part, preferred_element_type=F32)
        carry = f[tm - 1:tm, :]
        hi, mid, lo = [part.astype(F32) for part in _split3(f * LOG2E)]
        for hd in range(FOX_HEADS):
            src = FF_LANE + hd
            slab = jnp.where(lane == 0, pltpu.roll(hi, (LANES - src) % LANES, axis=1),
                   jnp.where(lane == 1, pltpu.roll(mid, (LANES + 1 - src) % LANES, axis=1),
                   jnp.where(lane == 2, pltpu.roll(lo, (LANES + 2 - src) % LANES, axis=1), 0.0)))
            kb_ref[rows, hd * LANES:(hd + 1) * LANES] = slab.astype(BF16)


def _fox_gate(ffp, fb, S):
    T = ffp.shape[0]
    tm = _pick(S, TOKEN_TILE)
    idx = jnp.arange(tm)
    tri = (idx[None, :] <= idx[:, None]).astype(BF16)
    fbv = jnp.zeros((1, LANES), F32).at[0, FF_LANE:FF_LANE + FOX_HEADS].set(fb)
    return pl.pallas_call(
        _fox_gate_kernel,
        grid=(T // S,),
        in_specs=[pl.BlockSpec((S, LANES), lambda b: (b, 0)),
                  pl.BlockSpec((1, LANES), lambda b: (0, 0)),
                  pl.BlockSpec((tm, tm), lambda b: (0, 0))],
        out_specs=pl.BlockSpec((S, FOX_HEADS * LANES), lambda b: (b, 0)),
        out_shape=jax.ShapeDtypeStruct((T, FOX_HEADS * LANES), BF16),
        compiler_params=_cparams(("arbitrary",)),
        name="fox_gate",
    )(ffp, fbv, tri)


def _retention_kernel(dchunk_ref, q_ref, k_ref, v_ref, g_ref, dmask_ref, din_ref, dout_ref,
                      o_ref, state_ref, *, n_chunks):
    @pl.when(pl.program_id(1) == 0)
    def _():
        state_ref[...] = jnp.zeros_like(state_ref)

    C = RET_CHUNK
    for ci in range(n_chunks):
        rows = slice(ci * C, (ci + 1) * C)
        for hd in range(RET_HEADS):
            slab = slice(hd * LANES, (hd + 1) * LANES)
            q = q_ref[rows, slab]
            k = k_ref[rows, slab]
            vcols = slice(hd * RET_DV, (hd + 1) * RET_DV)
            v = v_ref[rows, vcols]
            state = state_ref[hd]
            scores = lax.dot_general(q, k, (((1,), (1,)), ((), ())),
                                     preferred_element_type=F32) * dmask_ref[hd]
            inner = jnp.dot(scores.astype(BF16), v, preferred_element_type=F32)
            cross = jnp.dot(q, state.astype(BF16), preferred_element_type=F32) * din_ref[hd]
            o = inner + cross
            vd = (v.astype(F32) * dout_ref[hd]).astype(BF16)
            kv = lax.dot_general(k, vd, (((0,), (0,)), ((), ())), preferred_element_type=F32)
            state_ref[hd] = state * dchunk_ref[hd] + kv
            mu = jnp.mean(o, axis=-1, keepdims=True)
            d = o - mu
            var = jnp.mean(d * d, axis=-1, keepdims=True)
            on = d * lax.rsqrt(var + EPS)
            g = g_ref[rows, vcols].astype(F32)
            o_ref[rows, vcols] = (g * jax.nn.sigmoid(g) * on).astype(BF16)


def _retention(rq, rk, rv, rg, B, S):
    H, C = RET_HEADS, RET_CHUNK
    tr = _pick(S, RET_CHUNKS_PER_STEP * C)
    n_chunks = tr // C
    log_gamma = jnp.log1p(-jnp.exp2(-5.0 - jnp.arange(H, dtype=F32)))
    idx = jnp.arange(C, dtype=F32)
    rel = idx[:, None] - idx[None, :]
    dmask = jnp.where(rel >= 0, jnp.exp(log_gamma[:, None, None] * jnp.maximum(rel, 0.0)), 0.0)
    decay_in = jnp.exp(log_gamma[:, None] * (idx + 1.0))
    decay_out = jnp.exp(log_gamma[:, None] * (C - 1.0 - idx))
    decay_chunk = jnp.exp(log_gamma * C)
    din = jnp.broadcast_to(decay_in[:, :, None], (H, C, RET_DV))
    dout = jnp.broadcast_to(decay_out[:, :, None], (H, C, RET_DV))
    tok = lambda b, i: (b, i, 0)
    const3 = lambda b, i: (0, 0, 0)
    return pl.pallas_call(
        functools.partial(_retention_kernel, n_chunks=n_chunks),
        grid=(B, S // tr),
        in_specs=[pl.BlockSpec(memory_space=pltpu.SMEM),
                  pl.BlockSpec((None, tr, H * LANES), tok),
                  pl.BlockSpec((None, tr, H * LANES), tok),
                  pl.BlockSpec((None, tr, H * RET_DV), tok),
                  pl.BlockSpec((None, tr, H * RET_DV), tok),
                  pl.BlockSpec((H, C, C), const3),
                  pl.BlockSpec((H, C, RET_DV), const3),
                  pl.BlockSpec((H, C, RET_DV), const3)],
        out_specs=pl.BlockSpec((None, tr, H * RET_DV), tok),
        out_shape=jax.ShapeDtypeStruct((B, S, H * RET_DV), BF16),
        scratch_shapes=[pltpu.VMEM((H, LANES, RET_DV), F32)],
        compiler_params=_cparams(("arbitrary", "arbitrary")),
        name="retention",
    )(decay_chunk, rq.reshape(B, S, -1), rk.reshape(B, S, -1), rv.reshape(B, S, -1),
      rg.reshape(B, S, -1), dmask, din, dout)


def _flash_kernel(q_ref, k_ref, ke_ref, v_ref, o_ref, m_ref, l_ref, acc_ref, s_ref, *, tq, q_bias_cols):
    S = q_ref.shape[0]
    nq = S // tq
    tk = tq
    m_ref[...] = jnp.full(m_ref.shape, NEG_BIG, F32)
    l_ref[...] = jnp.zeros_like(l_ref)
    acc_ref[...] = jnp.zeros_like(acc_ref)
    lane = lax.broadcasted_iota(I32, (tq, LANES), 1)

    half = tk // 2
    tri = (lax.broadcasted_iota(I32, (half, half), 0) <= lax.broadcasted_iota(I32, (half, half), 1))
    nt = lambda a, b: lax.dot_general(a, b, (((1,), (1,)), ((), ())), preferred_element_type=F32)

    def issue(n):
        i, j = pairs[n]
        q = q_ref[i * tq:(i + 1) * tq, :]
        if q_bias_cols:
            q = jnp.concatenate([q, jnp.where(lane < q_bias_cols, -1.0, 0.0).astype(BF16)], axis=1)
        kj = jnp.concatenate([k_ref[j * tk:(j + 1) * tk, :], ke_ref[j * tk:(j + 1) * tk, :]], axis=1)
        if j < i:
            s_ref[n % 2] = nt(kj, q)
        else:
            s_ref[n % 2, 0:half, :] = nt(kj[0:half], q)
            s_ref[n % 2, half:, half:] = nt(kj[half:], q[half:])

    def update(s, i, c0, c1, r0, r1):
        m_prev = m_ref[i, :, c0:c1]
        m_new = jnp.maximum(m_prev, jnp.max(s, axis=0, keepdims=True))
        alpha = jnp.exp2(m_prev - m_new)
        p = jnp.exp2(s - m_new)
        l_ref[i, :, c0:c1] = alpha * l_ref[i, :, c0:c1] + jnp.sum(p, axis=0, keepdims=True)
        acc_ref[i, :, c0:c1] = alpha * acc_ref[i, :, c0:c1] + lax.dot_general(
            v_ref[r0:r1, :], p.astype(BF16), (((0,), (0,)), ((), ())),
            preferred_element_type=F32)
        m_ref[i, :, c0:c1] = m_new

    def consume(n):
        i, j = pairs[n]
        slot = n % 2
        if j < i:
            update(s_ref[slot], i, 0, tq, j * tk, (j + 1) * tk)
        else:
            left = jnp.where(tri, s_ref[slot, 0:half, 0:half], NEG_BIG)
            right = jnp.concatenate([s_ref[slot, 0:half, half:],
                                     jnp.where(tri, s_ref[slot, half:, half:], NEG_BIG)], axis=0)
            update(left, i, 0, half, j * tk, j * tk + half)
            update(right, i, half, tq, j * tk, (j + 1) * tk)
            o_ref[i * tq:(i + 1) * tq, :] = (acc_ref[i] / l_ref[i]).T.astype(o_ref.dtype)

    pairs = [(i, j) for i in range(nq) for j in range(i + 1)]
    issue(0)
    for n in range(len(pairs)):
        if n + 1 < len(pairs):
            issue(n + 1)
        consume(n)


def _flash(q, k, ke, v, B, S, H, dq, ke_per_head, q_bias_cols, name):
    tq = _pick(S, ATTN_TILE)
    nq = S // tq
    ke_map = (lambda b, h: (b, 0, h)) if ke_per_head else (lambda b, h: (b, 0, 0))
    return pl.pallas_call(
        functools.partial(_flash_kernel, tq=tq, q_bias_cols=q_bias_cols),
        grid=(B, H),
        in_specs=[pl.BlockSpec((None, S, dq), lambda b, h: (b, 0, h)),
                  pl.BlockSpec((None, S, LANES), lambda b, h: (b, 0, h)),
                  pl.BlockSpec((None, S, LANES), ke_map),
                  pl.BlockSpec((None, S, LANES), lambda b, h: (b, 0, h))],
        out_specs=pl.BlockSpec((None, S, LANES), lambda b, h: (b, 0, h)),
        out_shape=jax.ShapeDtypeStruct((B, S, H * LANES), BF16),
        scratch_shapes=[pltpu.VMEM((nq, 1, tq), F32), pltpu.VMEM((nq, 1, tq), F32),
                        pltpu.VMEM((nq, LANES, tq), F32), pltpu.VMEM((2, tq, tq), F32)],
        compiler_params=_cparams(("arbitrary", "arbitrary")),
        name=name,
    )(q, k, ke, v)


def _merge_kernel(x_ref, ya_ref, yb_ref, yc_ref, g1_ref, sh1_ref, sc1_ref, gt1_ref,
                  g2_ref, sh2_ref, sc2_ref, gw_ref, gb_ref, bw_ref, ow_ref,
                  rwh_ref, rwl_ref, rb_ref, tri_ref,
                  x1_ref, h2_ref, mi_ref, mf_ref, cnt_ref, h_ref, mg_ref, carry_ref):
    tm, D = x_ref.shape
    cn = MERGE_COL_CHUNK

    @pl.when(pl.program_id(0) == 0)
    def _():
        carry_ref[...] = jnp.zeros_like(carry_ref)

    h_ref[...] = _norm_mod(x_ref[...], g1_ref[...], sh1_ref[...], sc1_ref[...]).astype(BF16)
    for n in range(D // cn):
        cols = slice(n * cn, (n + 1) * cn)
        merged = None
        for i, y_ref in enumerate((ya_ref, yb_ref, yc_ref)):
            gcols = slice(i * D + n * cn, i * D + (n + 1) * cn)
            gate = jax.nn.sigmoid(jnp.dot(h_ref[...], gw_ref[:, gcols], preferred_element_type=F32)
                                  + gb_ref[:, gcols])
            br = jnp.dot(y_ref[...], bw_ref[i, :, cols], preferred_element_type=F32)
            merged = gate * br if merged is None else merged + gate * br
        mg_ref[:, cols] = merged.astype(BF16)
    for n in range(D // cn):
        cols = slice(n * cn, (n + 1) * cn)
        mix = jnp.dot(mg_ref[...], ow_ref[:, cols], preferred_element_type=F32)
        x1_ref[:, cols] = x_ref[:, cols] + gt1_ref[:, cols] * mix
    h2 = _norm_mod(x1_ref[...], g2_ref[...], sh2_ref[...], sc2_ref[...])
    h2_ref[...] = _pack_bf16_pair(h2[:, :D // 2], h2[:, D // 2:])

    hh = h2.astype(BF16)
    hl = (h2 - hh.astype(F32)).astype(BF16)
    nt = lambda a, b: lax.dot_general(a, b, (((1,), (1,)), ((), ())), preferred_element_type=F32)
    lt = nt(rwh_ref[...], hh) + nt(rwh_ref[...], hl) + nt(rwl_ref[...], hh) + rb_ref[...]
    row8 = lax.broadcasted_iota(I32, (EXP_PER_GROUP, tm), 0)
    gl = jnp.where(row8 < N_GROUPS, lt[N_EXPERTS:N_EXPERTS + EXP_PER_GROUP, :], -jnp.inf)
    gmax = jnp.max(gl, axis=0, keepdims=True)
    g_idx = jnp.min(jnp.where(gl == gmax, row8, EXP_PER_GROUP), axis=0, keepdims=True)
    g_w = 1.0 / jnp.sum(jnp.exp(gl - gmax), axis=0, keepdims=True)
    el = lt[(N_GROUPS - 1) * EXP_PER_GROUP:N_EXPERTS, :]
    for g in range(N_GROUPS - 2, -1, -1):
        el = jnp.where(g_idx == g, lt[g * EXP_PER_GROUP:(g + 1) * EXP_PER_GROUP, :], el)
    e1 = jnp.max(el, axis=0, keepdims=True)
    i1 = jnp.min(jnp.where(el == e1, row8, EXP_PER_GROUP), axis=0, keepdims=True)
    el2 = jnp.where(row8 == i1, -jnp.inf, el)
    e2 = jnp.max(el2, axis=0, keepdims=True)
    i2 = jnp.min(jnp.where(el2 == e2, row8, EXP_PER_GROUP), axis=0, keepdims=True)
    r = jnp.exp(e2 - e1)
    w1 = g_w / (1.0 + r)
    w2 = g_w * r / (1.0 + r)
    eid1 = g_idx * EXP_PER_GROUP + i1
    eid2 = g_idx * EXP_PER_GROUP + i2

    rowe = lax.broadcasted_iota(I32, (N_EXPERTS, tm), 0)
    hit1 = rowe == eid1
    hit2 = rowe == eid2
    onehot = jnp.where(hit1 | hit2, 1.0, 0.0)
    before = (jnp.dot(onehot.astype(BF16), tri_ref[...], preferred_element_type=F32)
              + jnp.concatenate([carry_ref[...]] * (tm // LANES), axis=1))
    rank1 = jnp.sum(jnp.where(hit1, before, 0.0), axis=0, keepdims=True)
    rank2 = jnp.sum(jnp.where(hit2, before, 0.0), axis=0, keepdims=True)
    carry_ref[...] = carry_ref[...] + jnp.sum(onehot, axis=1, keepdims=True)
    cnt_ref[...] = carry_ref[...]

    mi_ref[...] = jnp.where(row8 == 0, eid1,
                  jnp.where(row8 == 1, eid2,
                  jnp.where(row8 == 2, rank1.astype(I32),
                  jnp.where(row8 == 3, rank2.astype(I32), 0))))
    mf_ref[...] = jnp.where(row8 == 0, w1, jnp.where(row8 == 1, w2, 0.0))


def _merge(x2, x_tile0, S, ya, yb, yc, g1, sh1, sc1, gt1, g2, sh2, sc2, gw, gb, bw, ow, rwh, rwl, rb):
    D = x2.shape[1]
    T = ya.shape[0]
    tm = _pick(S, MERGE_TILE)
    per_b = S // tm
    row = lambda i: (i, 0)
    xrow = lambda i: (i + x_tile0 * _pick(S, TOKEN_TILE) // tm, 0)
    col = lambda i: (0, i)
    const = lambda i: (0, 0)
    batch = lambda i: (i // per_b, 0, 0)
    idx = jnp.arange(tm)
    tri = (idx[:, None] < idx[None, :]).astype(BF16)
    vecb = pl.BlockSpec((None, 1, D), batch)
    return pl.pallas_call(
        _merge_kernel,
        grid=(T // tm,),
        in_specs=[pl.BlockSpec((tm, D), xrow),
                  pl.BlockSpec((tm, BRANCH_W), row),
                  pl.BlockSpec((tm, BRANCH_W), row),
                  pl.BlockSpec((tm, BRANCH_W), row),
                  pl.BlockSpec((1, D), const), vecb, vecb, vecb,
                  pl.BlockSpec((1, D), const), vecb, vecb,
                  pl.BlockSpec((D, N_BRANCH * D), const),
                  pl.BlockSpec((1, N_BRANCH * D), const),
                  pl.BlockSpec((N_BRANCH, BRANCH_W, D), lambda i: (0, 0, 0)),
                  pl.BlockSpec((D, D), const),
                  pl.BlockSpec((LANES, D), const),
                  pl.BlockSpec((LANES, D), const),
                  pl.BlockSpec((LANES, 1), const),
                  pl.BlockSpec((tm, tm), const)],
        out_specs=[pl.BlockSpec((tm, D), row), pl.BlockSpec((tm, D // 2), row),
                   pl.BlockSpec((8, tm), col), pl.BlockSpec((8, tm), col),
                   pl.BlockSpec((N_EXPERTS, LANES), const)],
        out_shape=[jax.ShapeDtypeStruct((T, D), F32), jax.ShapeDtypeStruct((T, D // 2), U32),
                   jax.ShapeDtypeStruct((8, T), I32), jax.ShapeDtypeStruct((8, T), F32),
                   jax.ShapeDtypeStruct((N_EXPERTS, LANES), F32)],
        scratch_shapes=[pltpu.VMEM((tm, D), BF16), pltpu.VMEM((tm, D), BF16),
                        pltpu.VMEM((N_EXPERTS, LANES), F32)],
        compiler_params=_cparams(("arbitrary",)),
        name="merge",
    )(x2, ya, yb, yc, g1, sh1, sc1, gt1, g2, sh2, sc2, gw, gb, bw, ow, rwh, rwl, rb, tri)


def _expert_kernel(blk_e_ref, nvalid_ref, xs_ref, w1_ref, w3_ref, w2_ref, ys_ref, w1b_ref, w3b_ref, w2b_ref):
    i = pl.program_id(0)
    valid = i < nvalid_ref[0]
    new_expert = jnp.logical_or(i == 0, blk_e_ref[i] != blk_e_ref[jnp.maximum(i - 1, 0)])

    @pl.when(jnp.logical_and(valid, new_expert))
    def _():
        w1b_ref[...] = w1_ref[...].astype(BF16)
        w3b_ref[...] = w3_ref[...].astype(BF16)
        w2b_ref[...] = w2_ref[...].astype(BF16)

    @pl.when(valid)
    def _():
        half = xs_ref.shape[1]
        x_lo, x_hi = _unpack_bf16_pair(xs_ref[...])
        x_lo = x_lo.astype(BF16)
        x_hi = x_hi.astype(BF16)

        def up(w_ref):
            return (jnp.dot(x_lo, w_ref[:half, :], preferred_element_type=F32)
                    + jnp.dot(x_hi, w_ref[half:, :], preferred_element_type=F32))

        a = up(w1b_ref)
        b = up(w3b_ref)
        hid = (a * jax.nn.sigmoid(a) * b).astype(BF16)
        y = jnp.dot(hid, w2b_ref[...], preferred_element_type=F32)
        ys_ref[...] = _pack_bf16_pair(y[:, :half], y[:, half:])

    @pl.when(jnp.logical_not(valid))
    def _():
        ys_ref[...] = jnp.zeros_like(ys_ref)


def _experts(xs, blk_e, nvalid, w1, w3, w2, layer, tb):
    P, half = xs.shape
    D = 2 * half
    n_blocks = P // tb
    rows = lambda i, be, nv: (jnp.minimum(i, nv[0] - 1), 0)
    grid_spec = pltpu.PrefetchScalarGridSpec(
        num_scalar_prefetch=2,
        grid=(n_blocks,),
        in_specs=[pl.BlockSpec((tb, half), rows),
                  pl.BlockSpec((None, None, D, D_EXPERT), lambda i, be, nv: (layer, be[i], 0, 0)),
                  pl.BlockSpec((None, None, D, D_EXPERT), lambda i, be, nv: (layer, be[i], 0, 0)),
                  pl.BlockSpec((None, None, D_EXPERT, D), lambda i, be, nv: (layer, be[i], 0, 0))],
        out_specs=pl.BlockSpec((tb, half), lambda i, be, nv: (i, 0)),
        scratch_shapes=[pltpu.VMEM((D, D_EXPERT), BF16), pltpu.VMEM((D, D_EXPERT), BF16),
                        pltpu.VMEM((D_EXPERT, D), BF16)],
    )
    return pl.pallas_call(
        _expert_kernel,
        grid_spec=grid_spec,
        out_shape=jax.ShapeDtypeStruct((P, half), U32),
        compiler_params=_cparams(("arbitrary",)),
        name="moe_experts",
    )(blk_e, nvalid, xs, w1, w3, w2)


def _sc_gather(data, idx):
    M = idx.shape[0]
    D = data.shape[1]
    W = SC_GATHER_WINDOW
    assert M % W == 0, (M, W)
    mesh = plsc.VectorSubcoreMesh(core_axis_name="core", subcore_axis_name="subcore")
    n_workers = mesh.num_cores * mesh.num_subcores
    assert M % (W * n_workers) == 0, (M, W, n_workers)

    @functools.partial(pl.kernel, out_type=jax.ShapeDtypeStruct((M, D), data.dtype), mesh=mesh,
                       scratch_types=[pltpu.VMEM((W,), I32), pltpu.VMEM((W, D), data.dtype)])
    def gather_kernel(x_hbm, i_hbm, o_hbm, i_vmem, buf):
        worker = lax.axis_index("core") * mesh.num_subcores + lax.axis_index("subcore")

        @pl.loop(0, M // (W * n_workers))
        def _(t):
            start = (t * n_workers + worker) * W
            pltpu.sync_copy(i_hbm.at[pl.ds(start, W)], i_vmem)
            pltpu.sync_copy(x_hbm.at[i_vmem], buf)
            pltpu.sync_copy(buf, o_hbm.at[pl.ds(start, W)])

    return gather_kernel(data, idx)


def _sc_dispatch(h2, dest, fill_idx, P):
    T, D = h2.shape
    W = SC_GATHER_WINDOW
    n_fill = fill_idx.shape[0]
    mesh = plsc.VectorSubcoreMesh(core_axis_name="core", subcore_axis_name="subcore")
    n_workers = mesh.num_cores * mesh.num_subcores
    assert T % (W * n_workers) == 0 and n_fill % (W * n_workers) == 0, (T, n_fill, W, n_workers)
    zeros = jnp.zeros((W, D), h2.dtype)

    @functools.partial(pl.kernel, out_type=jax.ShapeDtypeStruct((P, D), h2.dtype), mesh=mesh,
                       scratch_types=[pltpu.VMEM((W,), I32), pltpu.VMEM((W, D), h2.dtype)])
    def dispatch_kernel(h_hbm, d_hbm, f_hbm, z_hbm, o_hbm, i_vmem, buf):
        worker = lax.axis_index("core") * mesh.num_subcores + lax.axis_index("subcore")

        @pl.loop(0, T // (W * n_workers))
        def _(t):
            start = (t * n_workers + worker) * W
            pltpu.sync_copy(h_hbm.at[pl.ds(start, W)], buf)
            for c in range(2):
                pltpu.sync_copy(d_hbm.at[pl.ds(c * T + start, W)], i_vmem)
                pltpu.sync_copy(buf, o_hbm.at[i_vmem])

        pltpu.sync_copy(z_hbm, buf)

        @pl.loop(0, n_fill // (W * n_workers))
        def _(t):
            start = (t * n_workers + worker) * W
            pltpu.sync_copy(f_hbm.at[pl.ds(start, W)], i_vmem)
            pltpu.sync_copy(buf, o_hbm.at[i_vmem])

    return dispatch_kernel(h2, dest, fill_idx, zeros)


def _combine_kernel(x_ref, g0_ref, g1_ref, mf_ref, gt_ref, fg_ref, *rest, final):
    o_ref = rest[-1]
    mf = mf_ref[...]
    lo0, hi0 = _unpack_bf16_pair(g0_ref[...])
    lo1, hi1 = _unpack_bf16_pair(g1_ref[...])
    w0 = mf[:, 0:1]
    w1 = mf[:, 1:2]
    ffn = jnp.concatenate([lo0 * w0 + lo1 * w1, hi0 * w0 + hi1 * w1], axis=1)
    out = x_ref[...] + gt_ref[...] * ffn
    if final:
        out = _rms(out, fg_ref[...])
    o_ref[...] = out


def _combine(x1, S, mf, gt2, final_g, g, final, out_rows=None, out_tile0=0, out_prev=None):
    T, D = x1.shape
    tm = _pick(S, COMBINE_TILE)
    out_tile0 = out_tile0 * _pick(S, TOKEN_TILE) // tm
    per_b = S // tm
    nt = T // tm
    in_specs = [pl.BlockSpec((tm, D), lambda i: (i, 0)),
                pl.BlockSpec((tm, D // 2), lambda i: (i, 0)),
                pl.BlockSpec((tm, D // 2), lambda i: (i + nt, 0)),
                pl.BlockSpec((tm, 2), lambda i: (i, 0)),
                pl.BlockSpec((None, 1, D), lambda i: (i // per_b, 0, 0)),
                pl.BlockSpec((1, D), lambda i: (0, 0))]
    args = [x1, g, g, mf, gt2, final_g]
    aliases = {}
    if out_prev is not None:
        in_specs.append(pl.BlockSpec(memory_space=pl.ANY))
        args.append(out_prev)
        aliases = {len(args) - 1: 0}
    return pl.pallas_call(
        functools.partial(_combine_kernel, final=final),
        grid=(nt,),
        in_specs=in_specs,
        out_specs=pl.BlockSpec((tm, D), lambda i: (i + out_tile0, 0)),
        out_shape=jax.ShapeDtypeStruct((T if out_rows is None else out_rows, D), F32),
        input_output_aliases=aliases,
        compiler_params=_cparams(("arbitrary",)),
        name="moe_combine",
    )(*args)


def _prep_w_in(w):
    offs = [0]
    for s in (256, 256, 512, 512, 512, 512, 512, FOX_HEADS, MLA_Q_RANK, MLA_KV_RANK, MLA_ROPE):
        offs.append(offs[-1] + s)
    rq, rk, rv, rg, fq, fk, fv, ff, mq, mkv, mkr = [w[:, offs[i]:offs[i + 1]] for i in range(11)]
    pad = jnp.zeros((w.shape[0], LANES - MLA_ROPE - FOX_HEADS), w.dtype)
    return jnp.concatenate([rq, rk, rv, rg, fq, fk, fv, mq, mkv, mkr, ff, pad], axis=1).astype(BF16)


def _prep_wq_up(w):
    r = w.reshape(MLA_Q_RANK, MLA_HEADS, MLA_NOPE + MLA_ROPE)
    r = jnp.pad(r, ((0, 0), (0, 0), (0, MLA_DQ - MLA_NOPE - MLA_ROPE)))
    return r.reshape(MLA_Q_RANK, MLA_HEADS * MLA_DQ).astype(BF16)


def _prep_router(w_grp, b_grp, w_exp, b_exp):
    D = w_grp.shape[0]
    pad = LANES - N_EXPERTS - N_GROUPS
    rwt = jnp.concatenate([w_exp, w_grp, jnp.zeros((D, pad), F32)], axis=1).astype(F32).T
    rwh = rwt.astype(BF16)
    rwl = (rwt - rwh.astype(F32)).astype(BF16)
    rb = jnp.concatenate([b_exp, b_grp, jnp.zeros((pad,), F32)]).astype(F32).reshape(LANES, 1)
    return rwh, rwl, rb


def kernel(x, c, positions, ada_w, ada_b, norm1_g, norm2_g, w_in, fox_fb, mla_q_norm_g, mla_wq_up, mla_kv_norm_g, mla_wkv_up, gate_w, gate_b, branch_w, out_w, router_grp_w, router_grp_b, router_exp_w, router_exp_b, exp_w1, exp_w3, exp_w2, final_g):
    B, S, D = x.shape
    L = ada_w.shape[0]
    T = B * S
    n_str = N_STREAMS if B % N_STREAMS == 0 else 1
    Bs = B // n_str
    Ts = Bs * S
    As = 2 * Ts
    tb = _pick(As, EXPERT_BLOCK)
    n_blocks = As // tb + N_EXPERTS
    P = n_blocks * tb
    tiles_per_stream = Ts // _pick(S, TOKEN_TILE)

    mod = _adaln(c, ada_w, ada_b)
    cos_t, sin_t = _rope_tables(positions)
    cos_s = [cos_t[h * Ts:(h + 1) * Ts] for h in range(n_str)]
    sin_s = [sin_t[h * Ts:(h + 1) * Ts] for h in range(n_str)]
    x_full = x.reshape(T, D)
    xs2 = [x_full] * n_str
    x_tile0 = [h * tiles_per_stream for h in range(n_str)]
    final_g2 = final_g.reshape(1, D)
    r3 = lambda a: a.reshape(Bs, S, -1)
    out = None

    for l in range(L):
        mods = [[mod[l, h * Bs:(h + 1) * Bs, i * D:(i + 1) * D].reshape(Bs, 1, D) for i in range(6)]
                for h in range(n_str)]
        g1 = norm1_g[l].reshape(1, D)
        g2 = norm2_g[l].reshape(1, D)
        w_all = _prep_w_in(w_in[l])
        wq = _prep_wq_up(mla_wq_up[l])
        wkv = mla_wkv_up[l].astype(BF16)
        gq = mla_q_norm_g[l].reshape(1, -1)
        gkv = mla_kv_norm_g[l].reshape(1, -1)
        rwh, rwl, rb = _prep_router(router_grp_w[l], router_grp_b[l], router_exp_w[l], router_exp_b[l])
        gw = gate_w[l].astype(BF16)
        gb = gate_b[l].reshape(1, -1)
        bw = branch_w[l].astype(BF16)
        ow = out_w[l].astype(BF16)
        last = l == L - 1

        merged = []
        for h in range(n_str):
            sh1, sc1, gt1, sh2, sc2, gt2 = mods[h]
            (rq, rk, rv, rg, fq, fk, fv, mq, mk, kpe, mv, ffp) = _proj(
                xs2[h], x_tile0[h], S, g1, sh1, sc1, w_all, cos_s[h], sin_s[h], gq, wq, gkv, wkv)
            ya = _retention(rq, rk, rv, rg, Bs, S).reshape(Ts, -1)
            kb = _fox_gate(ffp, fox_fb[l], S)
            yb = _flash(r3(fq), r3(fk), r3(kb), r3(fv), Bs, S, FOX_HEADS, FOX_DH, True, 3,
                        "flash_fox").reshape(Ts, -1)
            yc = _flash(r3(mq), r3(mk), r3(kpe), r3(mv), Bs, S, MLA_HEADS, MLA_DQ, False, 0,
                        "flash_mla").reshape(Ts, -1)
            merged.append(_merge(xs2[h], x_tile0[h], S, ya, yb, yc, g1, sh1, sc1, gt1, g2, sh2, sc2,
                                 gw, gb, bw, ow, rwh, rwl, rb))

        routed = []
        for h in range(n_str):
            x1, h2, mi, mf, cnt = merged[h]
            counts = cnt[:, 0].astype(I32)
            pcounts = (counts + tb - 1) // tb * tb
            pends = jnp.cumsum(pcounts)
            pstarts = pends - pcounts
            sel = mi[0:2, :, None] == jnp.arange(N_EXPERTS, dtype=I32)
            dest = (jnp.sum(jnp.where(sel, pstarts, 0), axis=-1) + mi[2:4]).reshape(As)
            blk_pos = jnp.arange(n_blocks, dtype=I32) * tb
            blk_e = jnp.minimum(jnp.sum((pends[None, :] <= blk_pos[:, None]).astype(I32), axis=1),
                                N_EXPERTS - 1)
            nvalid = (pends[-1:] // tb).astype(I32)
            fr = jnp.arange(tb, dtype=I32)[None, :]
            is_pad = (fr < (pcounts - counts)[:, None]).reshape(-1)
            pad_slot = ((pstarts + counts)[:, None] + fr).reshape(-1)
            tail_rank = jnp.cumsum(jnp.logical_not(is_pad).astype(I32)) - 1
            fill_idx = jnp.where(is_pad, pad_slot, pends[-1] + tail_rank)
            routed.append((dest, blk_e, nvalid, _sc_dispatch(h2, dest, fill_idx, P)))

        ys = [_experts(routed[h][3], routed[h][1], routed[h][2], exp_w1, exp_w3, exp_w2, l, tb)
              for h in range(n_str)]
        gathered = [_sc_gather(ys[h], routed[h][0]) for h in range(n_str)]
        for h in range(n_str):
            x1, _, _, mf, _ = merged[h]
            gt2 = mods[h][5]
            if last:
                out = _combine(x1, S, mf[0:2].T, gt2, final_g2, gathered[h], True,
                               out_rows=T, out_tile0=h * tiles_per_stream, out_prev=out)
            else:
                xs2[h] = _combine(x1, S, mf[0:2].T, gt2, final_g2, gathered[h], False)
        x_tile0 = [0] * n_str

    return out.reshape(B, S, D)
```
